```python
import jax, jax.numpy as jnp
from jax import lax
import numpy as np

D_MODEL = 1024
BATCH = 32
SEQ = 256
DEPTH = 1
DEC_BATCH = 4
DEC_SEQ = 1024
PAST_LEN = 512

GRID_W = 64
HEAD_DIM = 128
N_HEADS = D_MODEL // HEAD_DIM
N_KV_HEADS = N_HEADS // 4
Q_BLOCK = 128
ROPE_THETA = 10000.0
ATTN_W = N_HEADS * HEAD_DIM
KV_W = N_KV_HEADS * HEAD_DIM
POOL_WINDOWS = (2, 4, 8, 16)
POOL_GROUPS = len(POOL_WINDOWS)
POOL_W = D_MODEL // 2
POOL_GC = POOL_W // POOL_GROUPS
IN_W = ATTN_W + 2 * KV_W + POOL_W + 2 * D_MODEL
N_EXP_GROUPS = 4
EXP_PER_GROUP = 4
N_EXPERTS = N_EXP_GROUPS * EXP_PER_GROUP
TOP_K = 2
D_EXPERT = 256
EPS = 1e-6

kernel_name = "hybrid_diffusion_prefix_gqa_pool_hiermoe_step"


def rms_norm(x, g):
    xf = x.astype(jnp.float32)
    y = xf * lax.rsqrt(jnp.mean(xf * xf, axis=-1, keepdims=True) + EPS)
    return (y * g.astype(jnp.float32)).astype(x.dtype)


def axial_rope_tables(n_tokens):
    rows = n_tokens // GRID_W
    t = jnp.arange(rows * GRID_W)
    row = (t // GRID_W).astype(jnp.float32)
    col = (t % GRID_W).astype(jnp.float32)
    nf = HEAD_DIM // 4
    freq = ROPE_THETA ** (-jnp.arange(nf, dtype=jnp.float32) / nf)
    ang = jnp.concatenate([row[:, None] * freq, col[:, None] * freq], axis=-1)
    return jnp.cos(ang), jnp.sin(ang)


def apply_axial_rope(x, cos, sin):
    xf = x.astype(jnp.float32)
    half = HEAD_DIM // 2
    nf = half // 2
    c = cos[None, :, None, :]
    s = sin[None, :, None, :]

    def rot(xh, ch, sh):
        x1, x2 = xh[..., :nf], xh[..., nf:]
        return jnp.concatenate([x1 * ch - x2 * sh, x2 * ch + x1 * sh], axis=-1)

    xr = rot(xf[..., :half], c[..., :nf], s[..., :nf])
    xc = rot(xf[..., half:], c[..., nf:], s[..., nf:])
    return jnp.concatenate([xr, xc], axis=-1).astype(x.dtype)


def block_attention(q, k, v):
    b, lq = q.shape[0], q.shape[1]
    nb = lq // Q_BLOCK
    grp = N_HEADS // N_KV_HEADS
    qb = q.reshape(b, nb, Q_BLOCK, N_KV_HEADS, grp, HEAD_DIM).transpose(1, 0, 2, 3, 4, 5)
    scale = HEAD_DIM ** -0.5

    def one_block(qi):
        s = jnp.einsum('bqkgd,bskd->bkgqs', qi, k).astype(jnp.float32) * scale
        p = jax.nn.softmax(s, axis=-1).astype(v.dtype)
        return jnp.einsum('bkgqs,bskd->bqkgd', p, v)

    o = lax.map(one_block, qb)
    return o.transpose(1, 0, 2, 3, 4, 5).reshape(b, lq, ATTN_W)


def multiscale_pool(xp, w_pool, pool_scale):
    b, l, _ = xp.shape
    xf = xp.astype(jnp.float32)
    cs = jnp.concatenate([jnp.zeros((b, 1, POOL_W), jnp.float32), jnp.cumsum(xf, axis=1)], axis=1)
    t = jnp.arange(l)
    outs = []
    for gi, w in enumerate(POOL_WINDOWS):
        lo = jnp.clip(t - w // 2, 0, l)
        hi = jnp.clip(t + w // 2, 0, l)
        sl = slice(gi * POOL_GC, (gi + 1) * POOL_GC)
        csg = cs[:, :, sl]
        s = jnp.take(csg, hi, axis=1) - jnp.take(csg, lo, axis=1)
        cnt = (hi - lo).astype(jnp.float32)[None, :, None]
        outs.append(s / cnt - xf[:, :, sl])
    d = jnp.stack(outs, axis=2).astype(xp.dtype)
    y = jnp.einsum('blgc,gce->blge', d, w_pool).reshape(b, l, POOL_W)
    return y * pool_scale


def hierarchical_moe(h, w_rg, w_re, w_g, w_u, w_d):
    b, l, d = h.shape
    x = h.reshape(b * l, d)
    xf = x.astype(jnp.float32)
    g_logits = xf @ w_rg.astype(jnp.float32)
    p_group = jax.nn.softmax(g_logits, axis=-1)
    g_sel = jnp.argmax(g_logits, axis=-1)
    p_sel = jnp.take_along_axis(p_group, g_sel[:, None], axis=-1)
    e_logits = (xf @ w_re.astype(jnp.float32)).reshape(-1, N_EXP_GROUPS, EXP_PER_GROUP)
    e_in = jnp.take_along_axis(e_logits, g_sel[:, None, None], axis=1)[:, 0]
    top_v, top_i = lax.top_k(e_in, TOP_K)
    w_top = jax.nn.softmax(top_v, axis=-1) * p_sel
    eid = g_sel[:, None] * EXP_PER_GROUP + top_i
    gate = jnp.sum(jax.nn.one_hot(eid, N_EXPERTS, dtype=jnp.float32) * w_top[..., None], axis=1)
    hid = jax.nn.silu(jnp.einsum('td,edf->tef', x, w_g)) * jnp.einsum('td,edf->tef', x, w_u)
    hid = hid * gate[:, :, None].astype(hid.dtype)
    y = jnp.einsum('tef,efd->td', hid, w_d)
    return y.reshape(b, l, d)


def trunk_layer(x, cvec, rope, ctx_kv, lp):
    b, l, _ = x.shape
    mod = jnp.einsum('bd,de->be', jax.nn.silu(cvec), lp['w_ada']) + lp['b_ada']
    sh1, sc1, gt1, sh2, sc2, gt2 = jnp.split(mod[:, None, :], 6, axis=-1)
    h = rms_norm(x, lp['norm1_g']) * (1 + sc1) + sh1
    proj = h @ lp['w_in']
    cuts = [ATTN_W, ATTN_W + KV_W, ATTN_W + 2 * KV_W, ATTN_W + 2 * KV_W + POOL_W,
            ATTN_W + 2 * KV_W + POOL_W + D_MODEL]
    q, k, v, xp, ga, gb = jnp.split(proj, cuts, axis=-1)
    q = rms_norm(q.reshape(b, l, N_HEADS, HEAD_DIM), lp['q_norm_g'])
    k = rms_norm(k.reshape(b, l, N_KV_HEADS, HEAD_DIM), lp['k_norm_g'])
    v = v.reshape(b, l, N_KV_HEADS, HEAD_DIM)
    k_ctx_out, v_ctx_out = k, v
    if rope is not None:
        q = apply_axial_rope(q, rope[0], rope[1])
        k = apply_axial_rope(k, rope[0], rope[1])
    if ctx_kv is not None:
        k_all = jnp.concatenate([ctx_kv[0].astype(k.dtype), k], axis=1)
        v_all = jnp.concatenate([ctx_kv[1].astype(v.dtype), v], axis=1)
    else:
        k_all, v_all = k, v
    attn = block_attention(q, k_all, v_all)
    pool = multiscale_pool(xp, lp['w_pool'], lp['pool_scale'])
    merged = jax.nn.sigmoid(ga) * (attn @ lp['w_branch_a']) + jax.nn.sigmoid(gb) * (pool @ lp['w_branch_b'])
    x = x + gt1 * (merged @ lp['w_out'])
    h2 = rms_norm(x, lp['norm2_g']) * (1 + sc2) + sh2
    x = x + gt2 * hierarchical_moe(h2, lp['w_router_group'], lp['w_router_expert'],
                                   lp['w_exp_gate'], lp['w_exp_up'], lp['w_exp_down'])
    return x, k_ctx_out, v_ctx_out


def setup_inputs(seed: int = 0) -> dict:
    key = jax.random.key(seed)
    ks = jax.random.split(key, 26)
    f32 = jnp.float32
    nrm = lambda k, shape, s: jax.random.normal(k, shape, f32) * s
    return {
        'x_prompt': nrm(ks[0], (BATCH, SEQ, D_MODEL), 1.0),
        'x_sample': nrm(ks[1], (DEC_BATCH, DEC_SEQ, D_MODEL), 1.0),
        'cache_k': nrm(ks[2], (DEC_BATCH, DEPTH, PAST_LEN, N_KV_HEADS, HEAD_DIM), 1.0),
        'cache_v': nrm(ks[3], (DEC_BATCH, DEPTH, PAST_LEN, N_KV_HEADS, HEAD_DIM), 1.0),
        'c': nrm(ks[4], (DEC_BATCH, D_MODEL), 1.0),
        'c_ctx': nrm(ks[5], (D_MODEL,), 1.0),
        'norm1_g': 1.0 + nrm(ks[6], (DEPTH, D_MODEL), 0.02),
        'norm2_g': 1.0 + nrm(ks[7], (DEPTH, D_MODEL), 0.02),
        'w_ada': nrm(ks[8], (DEPTH, D_MODEL, 6 * D_MODEL), 0.5 * D_MODEL ** -0.5),
        'b_ada': nrm(ks[9], (DEPTH, 6 * D_MODEL), 0.02),
        'w_in': nrm(ks[10], (DEPTH, D_MODEL, IN_W), D_MODEL ** -0.5),
        'q_norm_g': 1.0 + nrm(ks[11], (DEPTH, HEAD_DIM), 0.02),
        'k_norm_g': 1.0 + nrm(ks[12], (DEPTH, HEAD_DIM), 0.02),
        'w_pool': nrm(ks[13], (DEPTH, POOL_GROUPS, POOL_GC, POOL_GC), POOL_GC ** -0.5),
        'pool_scale': 1.0 + nrm(ks[14], (DEPTH, POOL_W), 0.02),
        'w_branch_a': nrm(ks[15], (DEPTH, ATTN_W, D_MODEL), ATTN_W ** -0.5),
        'w_branch_b': nrm(ks[16], (DEPTH, POOL_W, D_MODEL), POOL_W ** -0.5),
        'w_out': nrm(ks[17], (DEPTH, D_MODEL, D_MODEL), D_MODEL ** -0.5),
        'w_router_group': nrm(ks[18], (DEPTH, D_MODEL, N_EXP_GROUPS), D_MODEL ** -0.5),
        'w_router_expert': nrm(ks[19], (DEPTH, D_MODEL, N_EXPERTS), D_MODEL ** -0.5),
        'w_exp_gate': nrm(ks[20], (DEPTH, N_EXPERTS, D_MODEL, D_EXPERT), D_MODEL ** -0.5),
        'w_exp_up': nrm(ks[21], (DEPTH, N_EXPERTS, D_MODEL, D_EXPERT), D_MODEL ** -0.5),
        'w_exp_down': nrm(ks[22], (DEPTH, N_EXPERTS, D_EXPERT, D_MODEL), D_EXPERT ** -0.5),
        'final_norm_g': 1.0 + nrm(ks[23], (D_MODEL,), 0.02),
    }


def reference(x_prompt, x_sample, cache_k, cache_v, c, c_ctx, norm1_g, norm2_g, w_ada, b_ada, w_in,
              q_norm_g, k_norm_g, w_pool, pool_scale, w_branch_a, w_branch_b, w_out,
              w_router_group, w_router_expert, w_exp_gate, w_exp_up, w_exp_down, final_norm_g):
    rope = axial_rope_tables(x_sample.shape[1])
    cvec_ctx = jnp.broadcast_to(c_ctx[None, :], (x_prompt.shape[0], D_MODEL))
    xp, xs = x_prompt, x_sample
    new_k, new_v = [], []
    for i in range(DEPTH):
        lp = {
            'norm1_g': norm1_g[i], 'norm2_g': norm2_g[i], 'w_ada': w_ada[i], 'b_ada': b_ada[i],
            'w_in': w_in[i], 'q_norm_g': q_norm_g[i], 'k_norm_g': k_norm_g[i],
            'w_pool': w_pool[i], 'pool_scale': pool_scale[i], 'w_branch_a': w_branch_a[i],
            'w_branch_b': w_branch_b[i], 'w_out': w_out[i], 'w_router_group': w_router_group[i],
            'w_router_expert': w_router_expert[i], 'w_exp_gate': w_exp_gate[i],
            'w_exp_up': w_exp_up[i], 'w_exp_down': w_exp_down[i],
        }
        xp, kc, vc = trunk_layer(xp, cvec_ctx, None, None, lp)
        new_k.append(kc)
        new_v.append(vc)
        xs, _, _ = trunk_layer(xs, c, rope, (cache_k[:, i], cache_v[:, i]), lp)
    y_prompt = rms_norm(xp, final_norm_g)
    y_sample = rms_norm(xs, final_norm_g)
    new_cache_k = jnp.stack(new_k, axis=1)
    new_cache_v = jnp.stack(new_v, axis=1)
    return (y_prompt, y_sample, new_cache_k, new_cache_v)
```

```python
import functools

import jax
import jax.numpy as jnp
from jax import lax
from jax.experimental import pallas as pl
from jax.experimental.pallas import tpu as pltpu

F32 = jnp.float32
BF16 = jnp.bfloat16

D_MODEL = 1024
HEAD_DIM = 128
N_HEADS = 8
N_KV_HEADS = 2
GROUP = N_HEADS // N_KV_HEADS
ATTN_W = N_HEADS * HEAD_DIM
KV_W = N_KV_HEADS * HEAD_DIM
POOL_WINDOWS = (2, 4, 8, 16)
POOL_GC = 128
POOL_W = POOL_GC * len(POOL_WINDOWS)
IN_W = ATTN_W + 2 * KV_W + POOL_W + 2 * D_MODEL
GATE_COL = ATTN_W + 2 * KV_W + POOL_W
GRID_W = 64
ROPE_THETA = 10000.0
N_EXP_GROUPS = 4
EXP_PER_GROUP = 4
N_EXPERTS = 16
D_EXPERT = 256
EPS = 1e-6

POOL_HALO = 8
ROW_BLOCK = 256
MOE_TILE = 512
ADA_COLS = 768
ROUTER_LANES = 128
V7X_VMEM_LIMIT_BYTES = 56 * 1024 * 1024


def _sigmoid(x):
    return 1.0 / (1.0 + jnp.exp(-x))


def _resident(shape):
    return pl.BlockSpec(shape, lambda i: (0,) * len(shape), pipeline_mode=pl.Buffered(1))


def _ada_kernel(c_ref, w_ref, b_ref, o_ref):
    c = c_ref[...]
    s = (c * _sigmoid(c)).astype(BF16)
    o_ref[...] = jnp.dot(s, w_ref[...].astype(BF16), preferred_element_type=F32) + b_ref[...]


def _ada(cond8, w_ada, b_ada):
    n = w_ada.shape[1]
    return pl.pallas_call(
        _ada_kernel,
        grid=(n // ADA_COLS,),
        in_specs=[
            pl.BlockSpec((8, D_MODEL), lambda j: (0, 0)),
            pl.BlockSpec((D_MODEL, ADA_COLS), lambda j: (0, j)),
            pl.BlockSpec((1, ADA_COLS), lambda j: (0, j)),
        ],
        out_specs=pl.BlockSpec((8, ADA_COLS), lambda j: (0, j)),
        out_shape=jax.ShapeDtypeStruct((8, n), F32),
        name="ada_mod",
    )(cond8, w_ada, b_ada)


def _mix_kernel(*refs, S, L, P, use_rope, emit_kv):
    it = iter(refs)
    x_ref = next(it)
    mod_ref = next(it)
    if P:
        ck_ref = next(it)
        cv_ref = next(it)
    if use_rope:
        cos_ref = next(it)
        sneg_ref = next(it)
        spos_ref = next(it)
    g1_ref, win_ref, qg_ref, kg_ref, wpool_ref, pscale_ref, wa_ref, wb_ref, wo_ref = (
        next(it) for _ in range(9))
    xmid_ref = next(it)
    if emit_kv:
        knew_ref = next(it)
        vnew_ref = next(it)
    q_s, k_s, v_s, xp_s, h_s, attn_s = (next(it) for _ in range(6))

    TM = S * L
    RB = ROW_BLOCK
    scale = HEAD_DIM ** -0.5

    sh1 = mod_ref[0, 0:1, :]
    gain1 = g1_ref[...] * (1.0 + mod_ref[0, 1:2, :])
    gt1 = mod_ref[0, 2:3, :]
    qg = qg_ref[...]
    kg = kg_ref[...]

    if P:
        k_s[0, 0:P, :] = ck_ref[0].astype(BF16)
        v_s[0, 0:P, :] = cv_ref[0].astype(BF16)
    xp_s[:, 0:POOL_HALO, :] = jnp.zeros((S, POOL_HALO, POOL_W), F32)
    xp_s[:, L + POOL_HALO:L + 2 * POOL_HALO, :] = jnp.zeros((S, POOL_HALO, POOL_W), F32)

    def head_norm(t, g):
        return t * lax.rsqrt(jnp.mean(t * t, axis=-1, keepdims=True) + EPS) * g

    def project(r, carry):
        r0 = pl.multiple_of(r * RB, RB)
        s = r0 // L
        o = pl.multiple_of(r0 % L, RB)
        x = x_ref[pl.ds(r0, RB), :]
        h = x * lax.rsqrt(jnp.mean(x * x, axis=-1, keepdims=True) + EPS) * gain1 + sh1
        hb = h.astype(BF16)
        h_s[pl.ds(r0, RB), :] = hb
        p1 = jnp.dot(hb, win_ref[:, 0:GATE_COL], preferred_element_type=F32)
        if use_rope:
            cs = cos_ref[pl.ds(o, RB), :]
            sn = sneg_ref[pl.ds(o, RB), :]
            sp = spos_ref[pl.ds(o, RB), :]

        def rope(t):
            return t * cs + pltpu.roll(t, HEAD_DIM - 32, 1) * sn + pltpu.roll(t, 32, 1) * sp

        for hd in range(N_HEADS):
            qh = head_norm(p1[:, hd * HEAD_DIM:(hd + 1) * HEAD_DIM], qg)
            if use_rope:
                qh = rope(qh)
            q_s[hd, pl.ds(r0, RB), :] = qh.astype(BF16)
        for kh in range(N_KV_HEADS):
            c0 = ATTN_W + kh * HEAD_DIM
            kk = head_norm(p1[:, c0:c0 + HEAD_DIM], kg)
            if emit_kv:
                knew_ref[pl.ds(r0, RB), kh * HEAD_DIM:(kh + 1) * HEAD_DIM] = kk
            if use_rope:
                kk = rope(kk)
            k_s[s, pl.ds(P + o, RB), kh * HEAD_DIM:(kh + 1) * HEAD_DIM] = kk.astype(BF16)
        vv = p1[:, ATTN_W + KV_W:ATTN_W + 2 * KV_W]
        if emit_kv:
            vnew_ref[pl.ds(r0, RB), :] = vv
        v_s[s, pl.ds(P + o, RB), :] = vv.astype(BF16)
        xp_s[s, pl.ds(POOL_HALO + o, RB), :] = p1[:, ATTN_W + 2 * KV_W:GATE_COL]
        return carry

    lax.fori_loop(0, TM // RB, project, 0)

    def mix(r, carry):
        r0 = pl.multiple_of(r * RB, RB)
        s = r0 // L
        o = pl.multiple_of(r0 % L, RB)

        for kh in range(N_KV_HEADS):
            k = k_s[s, :, kh * HEAD_DIM:(kh + 1) * HEAD_DIM]
            v = v_s[s, :, kh * HEAD_DIM:(kh + 1) * HEAD_DIM]
            q4 = q_s[kh * GROUP:(kh + 1) * GROUP, pl.ds(r0, RB), :].reshape(GROUP * RB, HEAD_DIM)
            sc = lax.dot_general(q4, k, (((1,), (1,)), ((), ())), preferred_element_type=F32) * scale
            e = jnp.exp(sc - jnp.max(sc, axis=-1, keepdims=True))
            den = jnp.sum(e, axis=-1, keepdims=True)
            o4 = jnp.dot(e.astype(BF16), v, preferred_element_type=F32) / den
            for g in range(GROUP):
                hd = kh * GROUP + g
                attn_s[:, hd * HEAD_DIM:(hd + 1) * HEAD_DIM] = o4[g * RB:(g + 1) * RB].astype(BF16)
        a = jnp.dot(attn_s[...], wa_ref[...], preferred_element_type=F32)

        t = o + lax.broadcasted_iota(jnp.int32, (RB, 1), 0)
        RW = RB + 2 * POOL_HALO
        parts = []
        for gi, w in enumerate(POOL_WINDOWS):
            cols = slice(gi * POOL_GC, (gi + 1) * POOL_GC)
            xw = xp_s[s, pl.ds(o, RW), cols]
            run = xw
            span = 1
            while span < w:
                run = run + pltpu.roll(run, span, 0)
                span *= 2
            if w // 2 > 1:
                run = pltpu.roll(run, RW - (w // 2 - 1), 0)
            tot = run[POOL_HALO:POOL_HALO + RB]
            cnt = (jnp.minimum(t + w // 2, L) - jnp.maximum(t - w // 2, 0)).astype(F32)
            parts.append(tot / cnt - xw[POOL_HALO:POOL_HALO + RB])
        dpool = jnp.concatenate(parts, axis=1).astype(BF16)
        pooled = jnp.dot(dpool, wpool_ref[...], preferred_element_type=F32) * pscale_ref[...]
        b = jnp.dot(pooled.astype(BF16), wb_ref[...], preferred_element_type=F32)

        gates = jnp.dot(h_s[pl.ds(r0, RB), :], win_ref[:, GATE_COL:IN_W], preferred_element_type=F32)
        merged = _sigmoid(gates[:, 0:D_MODEL]) * a + _sigmoid(gates[:, D_MODEL:2 * D_MODEL]) * b
        u = jnp.dot(merged.astype(BF16), wo_ref[...], preferred_element_type=F32)
        xmid_ref[pl.ds(r0, RB), :] = x_ref[pl.ds(r0, RB), :] + gt1 * u
        return carry

    lax.fori_loop(0, TM // RB, mix, 0)


def _mix(x2d, mod, mod_row, cache, rope_tabs, weights, *, S, L, emit_kv):
    T = x2d.shape[0]
    TM = S * L
    P = cache[0].shape[1] if cache is not None else 0
    use_rope = rope_tabs is not None
    assert T % TM == 0 and L % ROW_BLOCK == 0
    assert not (use_rope or P) or S == 1
    Lk = P + L

    args = [x2d, mod]
    in_specs = [
        pl.BlockSpec((TM, D_MODEL), lambda i: (i, 0)),
        pl.BlockSpec((1, 6, D_MODEL), lambda i: (mod_row(i), 0, 0)),
    ]
    if P:
        args += list(cache)
        in_specs += [pl.BlockSpec((1, P, KV_W), lambda i: (i, 0, 0))] * 2
    if use_rope:
        args += list(rope_tabs)
        in_specs += [_resident((L, HEAD_DIM))] * 3
    args += list(weights)
    in_specs += [_resident(w.shape) for w in weights]

    out_shape = [jax.ShapeDtypeStruct((T, D_MODEL), F32)]
    out_specs = [pl.BlockSpec((TM, D_MODEL), lambda i: (i, 0))]
    if emit_kv:
        out_shape += [jax.ShapeDtypeStruct((T, KV_W), F32)] * 2
        out_specs += [pl.BlockSpec((TM, KV_W), lambda i: (i, 0))] * 2

    scratch = [
        pltpu.VMEM((N_HEADS, TM, HEAD_DIM), BF16),
        pltpu.VMEM((S, Lk, KV_W), BF16),
        pltpu.VMEM((S, Lk, KV_W), BF16),
        pltpu.VMEM((S, L + 2 * POOL_HALO, POOL_W), F32),
        pltpu.VMEM((TM, D_MODEL), BF16),
        pltpu.VMEM((ROW_BLOCK, ATTN_W), BF16),
    ]
    kern = functools.partial(_mix_kernel, S=S, L=L, P=P, use_rope=use_rope, emit_kv=emit_kv)
    return pl.pallas_call(
        kern,
        grid=(T // TM,),
        in_specs=in_specs,
        out_specs=out_specs,
        out_shape=out_shape,
        scratch_shapes=scratch,
        compiler_params=pltpu.CompilerParams(
            dimension_semantics=("arbitrary",), vmem_limit_bytes=V7X_VMEM_LIMIT_BYTES),
        name="mixer_rope" if use_rope else "mixer_ctx",
    )(*args)


def _moe_kernel(x_ref, mod_ref, g2_ref, gf_ref, wr_ref, wg_ref, wu_ref, wd_ref, o_ref, hid_s):
    x = x_ref[...]
    sh2 = mod_ref[0, 3:4, :]
    gain2 = g2_ref[...] * (1.0 + mod_ref[0, 4:5, :])
    gt2 = mod_ref[0, 5:6, :]
    h2 = x * lax.rsqrt(jnp.mean(x * x, axis=-1, keepdims=True) + EPS) * gain2 + sh2
    hb = h2.astype(BF16)

    logits = jnp.dot(h2, wr_ref[...], preferred_element_type=F32, precision=lax.Precision.HIGHEST)
    lane = lax.broadcasted_iota(jnp.int32, logits.shape, 1).astype(F32)
    neg = jnp.float32(-1e30)
    far = jnp.float32(ROUTER_LANES)
    is_g = lane < N_EXP_GROUPS
    gl = jnp.where(is_g, logits, neg)
    gmax = jnp.max(gl, axis=-1, keepdims=True)
    gsel = jnp.min(jnp.where(gl == gmax, lane, far), axis=-1, keepdims=True)
    psel = 1.0 / jnp.sum(jnp.where(is_g, jnp.exp(gl - gmax), 0.0), axis=-1, keepdims=True)
    e_lo = N_EXP_GROUPS + EXP_PER_GROUP * gsel
    el = jnp.where(lane >= e_lo, jnp.where(lane < e_lo + EXP_PER_GROUP, logits, neg), neg)
    v1 = jnp.max(el, axis=-1, keepdims=True)
    i1 = jnp.min(jnp.where(el == v1, lane, far), axis=-1, keepdims=True)
    el2 = jnp.where(lane == i1, neg, el)
    v2 = jnp.max(el2, axis=-1, keepdims=True)
    i2 = jnp.min(jnp.where(el2 == v2, jnp.where(lane == i1, far, lane), far), axis=-1, keepdims=True)
    e2 = jnp.exp(v2 - v1)
    w1 = psel / (1.0 + e2)
    w2 = psel * e2 / (1.0 + e2)
    gate = jnp.where(lane == i1, w1, jnp.where(lane == i2, w2, 0.0))

    for e in range(N_EXPERTS):
        hg = jnp.dot(hb, wg_ref[e], preferred_element_type=F32)
        hu = jnp.dot(hb, wu_ref[e], preferred_element_type=F32)
        ge = gate[:, N_EXP_GROUPS + e:N_EXP_GROUPS + e + 1]
        hid_s[:, e * D_EXPERT:(e + 1) * D_EXPERT] = (hg * _sigmoid(hg) * hu * ge).astype(BF16)
    moe = jnp.dot(hid_s[...], wd_ref[...], preferred_element_type=F32)
    y = x + gt2 * moe
    o_ref[...] = y * lax.rsqrt(jnp.mean(y * y, axis=-1, keepdims=True) + EPS) * gf_ref[...]


def _moe(x2d, mod, mod_row, g2, gf, wr, wg, wu, wd):
    T = x2d.shape[0]
    TM = MOE_TILE
    assert T % TM == 0
    return pl.pallas_call(
        _moe_kernel,
        grid=(T // TM,),
        in_specs=[
            pl.BlockSpec((TM, D_MODEL), lambda i: (i, 0)),
            pl.BlockSpec((1, 6, D_MODEL), lambda i: (mod_row(i), 0, 0)),
            _resident(g2.shape), _resident(gf.shape), _resident(wr.shape),
            _resident(wg.shape), _resident(wu.shape), _resident(wd.shape),
        ],
        out_specs=pl.BlockSpec((TM, D_MODEL), lambda i: (i, 0)),
        out_shape=jax.ShapeDtypeStruct((T, D_MODEL), F32),
        scratch_shapes=[pltpu.VMEM((TM, N_EXPERTS * D_EXPERT), BF16)],
        compiler_params=pltpu.CompilerParams(
            dimension_semantics=("arbitrary",), vmem_limit_bytes=V7X_VMEM_LIMIT_BYTES),
        name="moe_dense",
    )(x2d, mod, g2, gf, wr, wg, wu, wd)


def _rope_tables(n_tokens):
    t = jnp.arange(n_tokens)
    row = (t // GRID_W).astype(F32)
    col = (t % GRID_W).astype(F32)
    nf = HEAD_DIM // 4
    freq = ROPE_THETA ** (-jnp.arange(nf, dtype=F32) / nf)
    ang = jnp.concatenate([row[:, None] * freq] * 2 + [col[:, None] * freq] * 2, axis=-1)
    first = (jnp.arange(HEAD_DIM) % (2 * nf)) < nf
    sin = jnp.sin(ang)
    return jnp.cos(ang), jnp.where(first, -sin, 0.0), jnp.where(first, 0.0, sin)


def kernel(x_prompt, x_sample, cache_k, cache_v, c, c_ctx, norm1_g, norm2_g, w_ada, b_ada, w_in, q_norm_g, k_norm_g, w_pool, pool_scale, w_branch_a, w_branch_b, w_out, w_router_group, w_router_expert, w_exp_gate, w_exp_up, w_exp_down, final_norm_g):
    assert norm1_g.shape[0] == 1, "single-layer trunk"
    B, L_ctx, _ = x_prompt.shape
    Bs, L_lat, _ = x_sample.shape
    P = cache_k.shape[2]

    cond8 = jnp.zeros((8, D_MODEL), F32).at[0].set(c_ctx).at[1:1 + Bs].set(c)
    mod = _ada(cond8, w_ada[0], b_ada[0][None, :]).reshape(8, 6, D_MODEL)

    wpool_bd = jax.scipy.linalg.block_diag(*[w_pool[0, g] for g in range(len(POOL_WINDOWS))])
    mix_w = (norm1_g[0][None, :], w_in[0].astype(BF16), q_norm_g[0][None, :], k_norm_g[0][None, :],
             wpool_bd.astype(BF16), pool_scale[0][None, :], w_branch_a[0].astype(BF16),
             w_branch_b[0].astype(BF16), w_out[0].astype(BF16))

    wr = jnp.zeros((D_MODEL, ROUTER_LANES), F32)
    wr = wr.at[:, 0:N_EXP_GROUPS].set(w_router_group[0])
    wr = wr.at[:, N_EXP_GROUPS:N_EXP_GROUPS + N_EXPERTS].set(w_router_expert[0])
    moe_w = (norm2_g[0][None, :], final_norm_g[None, :], wr, w_exp_gate[0].astype(BF16),
             w_exp_up[0].astype(BF16), w_exp_down[0].reshape(N_EXPERTS * D_EXPERT, D_MODEL).astype(BF16))

    xp2 = x_prompt.reshape(B * L_ctx, D_MODEL)
    xmid_p, knew, vnew = _mix(xp2, mod, lambda i: 0, None, None, mix_w, S=2, L=L_ctx, emit_kv=True)
    y_prompt = _moe(xmid_p, mod, lambda i: 0, *moe_w)

    xs2 = x_sample.reshape(Bs * L_lat, D_MODEL)
    cache = (cache_k[:, 0].reshape(Bs, P, KV_W), cache_v[:, 0].reshape(Bs, P, KV_W))
    (xmid_s,) = _mix(xs2, mod, lambda i: 1 + i, cache, _rope_tables(L_lat), mix_w,
                     S=1, L=L_lat, emit_kv=False)
    tiles_per_seq = L_lat // MOE_TILE
    y_sample = _moe(xmid_s, mod, lambda i: 1 + i // tiles_per_seq, *moe_w)

    return (y_prompt.reshape(B, L_ctx, D_MODEL), y_sample.reshape(Bs, L_lat, D_MODEL),
            knew.reshape(B, 1, L_ctx, N_KV_HEADS, HEAD_DIM), vnew.reshape(B, 1, L_ctx, N_KV_HEADS, HEAD_DIM))
```

```python
import functools

import jax
import jax.numpy as jnp
from jax import lax
from jax.experimental import pallas as pl
from jax.experimental.pallas import tpu as pltpu

F32 = jnp.float32
BF16 = jnp.bfloat16
I32 = jnp.int32

D_MODEL = 1024
HEAD_DIM = 128
N_HEADS = 8
N_KV_HEADS = 2
GROUP = N_HEADS // N_KV_HEADS
ATTN_W = N_HEADS * HEAD_DIM
KV_W = N_KV_HEADS * HEAD_DIM
POOL_WINDOWS = (2, 4, 8, 16)
POOL_GC = 128
POOL_W = POOL_GC * len(POOL_WINDOWS)
IN_W = ATTN_W + 2 * KV_W + POOL_W + 2 * D_MODEL
GATE_COL = ATTN_W + 2 * KV_W + POOL_W
GRID_W = 64
ROPE_THETA = 10000.0
ROPE_NF = HEAD_DIM // 4
N_EXP_GROUPS = 4
EXP_PER_GROUP = 4
N_EXPERTS = 16
D_EXPERT = 256
EPS = 1e-6

LANES = 128
SUBLANES = 8
COND_ROWS = SUBLANES
POOL_HALO = 8
ROW_BLOCK = 256
ADA_COLS = 768
EXPERT_LANE0 = N_EXP_GROUPS
PAIRS_PER_GROUP = EXP_PER_GROUP * (EXP_PER_GROUP - 1) // 2
N_BUCKETS = N_EXP_GROUPS * PAIRS_PER_GROUP
PAYLOAD_W = D_MODEL + LANES
SORT_TILE = 128
TOKEN_BLOCK = 512
V7X_VMEM_LIMIT_BYTES = 56 * 1024 * 1024


def _sigmoid(x):
    return 1.0 / (1.0 + jnp.exp(-x))


def _rms(x):
    return x * lax.rsqrt(jnp.mean(x * x, axis=-1, keepdims=True) + EPS)


def _resident(shape):
    zeros = (0,) * len(shape)
    return pl.BlockSpec(shape, lambda i, *_: zeros, pipeline_mode=pl.Buffered(1))


def _row(x):
    return jnp.transpose(jnp.broadcast_to(x, (x.shape[0], LANES)))[0:1, :]


def _ada_kernel(c_ref, w_ref, b_ref, o_ref):
    c = c_ref[...]
    s = (c * _sigmoid(c)).astype(BF16)
    o_ref[...] = jnp.dot(s, w_ref[...].astype(BF16), preferred_element_type=F32) + b_ref[...]


def _ada(cond, w_ada, b_ada):
    n = w_ada.shape[1]
    return pl.pallas_call(
        _ada_kernel,
        grid=(n // ADA_COLS,),
        in_specs=[
            pl.BlockSpec((COND_ROWS, D_MODEL), lambda j: (0, 0)),
            pl.BlockSpec((D_MODEL, ADA_COLS), lambda j: (0, j)),
            pl.BlockSpec((1, ADA_COLS), lambda j: (0, j)),
        ],
        out_specs=pl.BlockSpec((COND_ROWS, ADA_COLS), lambda j: (0, j)),
        out_shape=jax.ShapeDtypeStruct((COND_ROWS, n), F32),
        name="ada_mod",
    )(cond, w_ada, b_ada)


def _route(logits):
    lane = lax.broadcasted_iota(I32, logits.shape, 1).astype(F32)
    neg = jnp.float32(-1e30)
    far = jnp.float32(LANES)
    is_g = lane < N_EXP_GROUPS
    gl = jnp.where(is_g, logits, neg)
    gmax = jnp.max(gl, axis=-1, keepdims=True)
    gsel = jnp.min(jnp.where(gl == gmax, lane, far), axis=-1, keepdims=True)
    psel = 1.0 / jnp.sum(jnp.where(is_g, jnp.exp(gl - gmax), 0.0), axis=-1, keepdims=True)
    e_lo = EXPERT_LANE0 + EXP_PER_GROUP * gsel
    el = jnp.where(lane >= e_lo, jnp.where(lane < e_lo + EXP_PER_GROUP, logits, neg), neg)
    v1 = jnp.max(el, axis=-1, keepdims=True)
    i1 = jnp.min(jnp.where(el == v1, lane, far), axis=-1, keepdims=True)
    el2 = jnp.where(lane == i1, neg, el)
    v2 = jnp.max(el2, axis=-1, keepdims=True)
    i2 = jnp.min(jnp.where(el2 == v2, jnp.where(lane == i1, far, lane), far), axis=-1, keepdims=True)
    e2 = jnp.exp(v2 - v1)
    w1 = psel / (1.0 + e2)
    w2 = psel * e2 / (1.0 + e2)
    gate = jnp.where(lane == i1, w1, jnp.where(lane == i2, w2, 0.0))
    a = jnp.minimum(i1, i2) - e_lo
    b = jnp.maximum(i1, i2) - e_lo
    pair = a * (7.0 - a) * 0.5 + (b - a - 1.0)
    return gate, gsel * PAIRS_PER_GROUP + pair


def _mix_kernel(*refs, S, L, P, use_rope, emit_kv):
    it = iter(refs)
    x_ref = next(it)
    mod_ref = next(it)
    if P:
        ck_ref = next(it)
        cv_ref = next(it)
    if use_rope:
        cos_ref = next(it)
        sneg_ref = next(it)
        spos_ref = next(it)
    (g1_ref, win_ref, qg_ref, kg_ref, wpool_ref, pscale_ref, wa_ref, wb_ref, wo_ref,
     g2_ref, wr_ref) = (next(it) for _ in range(11))
    xmid_ref = next(it)
    pay_ref = next(it)
    oh_ref = next(it)
    if emit_kv:
        knew_ref = next(it)
        vnew_ref = next(it)
    q_s, k_s, v_s, xp_s, h_s, attn_s = (next(it) for _ in range(6))

    TM = S * L
    RB = ROW_BLOCK
    scale = HEAD_DIM ** -0.5

    sh1 = mod_ref[0, 0:1, :]
    gain1 = g1_ref[...] * (1.0 + mod_ref[0, 1:2, :])
    gt1 = mod_ref[0, 2:3, :]
    sh2 = mod_ref[0, 3:4, :]
    gain2 = g2_ref[...] * (1.0 + mod_ref[0, 4:5, :])
    qg = qg_ref[...]
    kg = kg_ref[...]

    def project(r, carry):
        r0 = pl.multiple_of(r * RB, RB)
        s = r0 // L
        o = pl.multiple_of(r0 % L, RB)
        hb = (_rms(x_ref[pl.ds(r0, RB), :]) * gain1 + sh1).astype(BF16)
        h_s[pl.ds(r0, RB), :] = hb
        p1 = jnp.dot(hb, win_ref[:, 0:GATE_COL], preferred_element_type=F32)
        if use_rope:
            cs = cos_ref[pl.ds(o, RB), :]
            sn = sneg_ref[pl.ds(o, RB), :]
            sp = spos_ref[pl.ds(o, RB), :]

        def rope(t):
            return (t * cs + pltpu.roll(t, HEAD_DIM - ROPE_NF, 1) * sn + pltpu.roll(t, ROPE_NF, 1) * sp)

        for hd in range(N_HEADS):
            qh = _rms(p1[:, hd * HEAD_DIM:(hd + 1) * HEAD_DIM]) * qg
            if use_rope:
                qh = rope(qh)
            q_s[hd, pl.ds(r0, RB), :] = qh.astype(BF16)
        for kh in range(N_KV_HEADS):
            c0 = ATTN_W + kh * HEAD_DIM
            kk = _rms(p1[:, c0:c0 + HEAD_DIM]) * kg
            if emit_kv:
                knew_ref[pl.ds(r0, RB), kh * HEAD_DIM:(kh + 1) * HEAD_DIM] = kk
            if use_rope:
                kk = rope(kk)
            k_s[s, pl.ds(P + o, RB), kh * HEAD_DIM:(kh + 1) * HEAD_DIM] = kk.astype(BF16)
        vv = p1[:, ATTN_W + KV_W:ATTN_W + 2 * KV_W]
        if emit_kv:
            vnew_ref[pl.ds(r0, RB), :] = vv
        v_s[s, pl.ds(P + o, RB), :] = vv.astype(BF16)
        xp_s[s, pl.ds(POOL_HALO + o, RB), :] = p1[:, ATTN_W + 2 * KV_W:GATE_COL]
        return carry

    @pl.when(pl.program_id(1) == 0)
    def _():
        if P:
            k_s[0, 0:P, :] = ck_ref[0].astype(BF16)
            v_s[0, 0:P, :] = cv_ref[0].astype(BF16)
        xp_s[:, 0:POOL_HALO, :] = jnp.zeros((S, POOL_HALO, POOL_W), F32)
        xp_s[:, L + POOL_HALO:L + 2 * POOL_HALO, :] = jnp.zeros((S, POOL_HALO, POOL_W), F32)
        lax.fori_loop(0, TM // RB, project, 0)

    def mix(r):
        r0 = pl.multiple_of(r * RB, RB)
        s = r0 // L
        o = pl.multiple_of(r0 % L, RB)

        for kh in range(N_KV_HEADS):
            k = k_s[s, :, kh * HEAD_DIM:(kh + 1) * HEAD_DIM]
            v = v_s[s, :, kh * HEAD_DIM:(kh + 1) * HEAD_DIM]
            q4 = q_s[kh * GROUP:(kh + 1) * GROUP, pl.ds(r0, RB), :].reshape(GROUP * RB, HEAD_DIM)
            sc = lax.dot_general(q4, k, (((1,), (1,)), ((), ())), preferred_element_type=F32) * scale
            e = jnp.exp(sc - jnp.max(sc, axis=-1, keepdims=True))
            den = jnp.sum(e, axis=-1, keepdims=True)
            o4 = jnp.dot(e.astype(BF16), v, preferred_element_type=F32) / den
            for g in range(GROUP):
                hd = kh * GROUP + g
                attn_s[:, hd * HEAD_DIM:(hd + 1) * HEAD_DIM] = o4[g * RB:(g + 1) * RB].astype(BF16)
        a = jnp.dot(attn_s[...], wa_ref[...], preferred_element_type=F32)

        t = o + lax.broadcasted_iota(I32, (RB, 1), 0)
        RW = RB + 2 * POOL_HALO
        parts = []
        for gi, w in enumerate(POOL_WINDOWS):
            cols = slice(gi * POOL_GC, (gi + 1) * POOL_GC)
            xw = xp_s[s, pl.ds(o, RW), cols]
            run = xw
            span = 1
            while span < w:
                run = run + pltpu.roll(run, span, 0)
                span *= 2
            if w // 2 > 1:
                run = pltpu.roll(run, RW - (w // 2 - 1), 0)
            tot = run[POOL_HALO:POOL_HALO + RB]
            cnt = (jnp.minimum(t + w // 2, L) - jnp.maximum(t - w // 2, 0)).astype(F32)
            parts.append(tot / cnt - xw[POOL_HALO:POOL_HALO + RB])
        dpool = jnp.concatenate(parts, axis=1).astype(BF16)
        pooled = jnp.dot(dpool, wpool_ref[...], preferred_element_type=F32) * pscale_ref[...]
        b = jnp.dot(pooled.astype(BF16), wb_ref[...], preferred_element_type=F32)

        gates = jnp.dot(h_s[pl.ds(r0, RB), :], win_ref[:, GATE_COL:IN_W], preferred_element_type=F32)
        merged = _sigmoid(gates[:, 0:D_MODEL]) * a + _sigmoid(gates[:, D_MODEL:2 * D_MODEL]) * b
        u = jnp.dot(merged.astype(BF16), wo_ref[...], preferred_element_type=F32)
        xm = x_ref[pl.ds(r0, RB), :] + gt1 * u
        xmid_ref[...] = xm

        h2 = _rms(xm) * gain2 + sh2
        hi = h2.astype(BF16)
        lo = (h2 - hi.astype(F32)).astype(BF16)
        l1 = jnp.dot(hi, wr_ref[...], preferred_element_type=F32)
        l2 = jnp.dot(lo, wr_ref[:, 0:LANES], preferred_element_type=F32)
        gate, bucket = _route(l1[:, 0:LANES] + l1[:, LANES:2 * LANES] + l2)
        pay_ref[:, 0:D_MODEL] = h2
        pay_ref[:, D_MODEL:PAYLOAD_W] = gate
        lane = lax.broadcasted_iota(I32, (RB, LANES), 1).astype(F32)
        oh_ref[...] = jnp.where(lane == bucket, 1.0, 0.0).astype(BF16)

    mix(pl.program_id(1))


def _mix(x2d, mod, mod_row, cache, rope_tabs, weights, *, S, L, emit_kv):
    T = x2d.shape[0]
    TM = S * L
    P = cache[0].shape[1] if cache is not None else 0
    use_rope = rope_tabs is not None
    assert T % TM == 0 and L % ROW_BLOCK == 0
    assert not (use_rope or P) or S == 1
    Lk = P + L

    args = [x2d, mod]
    nrb = TM // ROW_BLOCK
    in_specs = [
        pl.BlockSpec((TM, D_MODEL), lambda i, j: (i, 0)),
        pl.BlockSpec((1, 6, D_MODEL), lambda i, j: (mod_row(i), 0, 0)),
    ]
    if P:
        args += list(cache)
        in_specs += [pl.BlockSpec((1, P, KV_W), lambda i, j: (i, 0, 0))] * 2
    if use_rope:
        args += list(rope_tabs)
        in_specs += [_resident((L, HEAD_DIM))] * 3
    args += list(weights)
    in_specs += [_resident(w.shape) for w in weights]

    out_shape = [jax.ShapeDtypeStruct((T, D_MODEL), F32), jax.ShapeDtypeStruct((T, PAYLOAD_W), F32),
                 jax.ShapeDtypeStruct((T, LANES), BF16)]
    out_specs = [pl.BlockSpec((ROW_BLOCK, D_MODEL), lambda i, j: (i * nrb + j, 0)),
                 pl.BlockSpec((ROW_BLOCK, PAYLOAD_W), lambda i, j: (i * nrb + j, 0)),
                 pl.BlockSpec((ROW_BLOCK, LANES), lambda i, j: (i * nrb + j, 0))]
    if emit_kv:
        out_shape += [jax.ShapeDtypeStruct((T, KV_W), F32)] * 2
        out_specs += [pl.BlockSpec((TM, KV_W), lambda i, j: (i, 0))] * 2

    scratch = [
        pltpu.VMEM((N_HEADS, TM, HEAD_DIM), BF16),
        pltpu.VMEM((S, Lk, KV_W), BF16),
        pltpu.VMEM((S, Lk, KV_W), BF16),
        pltpu.VMEM((S, L + 2 * POOL_HALO, POOL_W), F32),
        pltpu.VMEM((TM, D_MODEL), BF16),
        pltpu.VMEM((ROW_BLOCK, ATTN_W), BF16),
    ]
    kern = functools.partial(_mix_kernel, S=S, L=L, P=P, use_rope=use_rope, emit_kv=emit_kv)
    return pl.pallas_call(
        kern,
        grid=(T // TM, nrb),
        in_specs=in_specs,
        out_specs=out_specs,
        out_shape=out_shape,
        scratch_shapes=scratch,
        compiler_params=pltpu.CompilerParams(
            dimension_semantics=("arbitrary", "arbitrary"), vmem_limit_bytes=V7X_VMEM_LIMIT_BYTES),
        name="mixer_rope" if use_rope else "mixer_ctx",
    )(*args)


def _plan_kernel(oh_ref, dest_ref, meta_ref, *, n_blocks):
    TB = TOKEN_BLOCK
    lane = lax.broadcasted_iota(I32, (SUBLANES, LANES), 1)

    def count(b, acc):
        oh = oh_ref[pl.ds(pl.multiple_of(b * TB, TB), TB), :].astype(F32)
        return acc + jnp.sum(oh, axis=0, keepdims=True)

    counts = lax.fori_loop(0, n_blocks, count, jnp.zeros((SUBLANES, LANES), F32))
    padded = jnp.ceil(counts * (1.0 / SORT_TILE)) * SORT_TILE
    ends = padded
    step = 1
    while step < LANES:
        ends = ends + jnp.where(lane >= step, pltpu.roll(ends, step, 1), 0.0)
        step *= 2
    starts = ends - padded

    tri = jnp.where(lax.broadcasted_iota(I32, (TB, TB), 1) < lax.broadcasted_iota(I32, (TB, TB), 0),
                    1.0, 0.0).astype(BF16)

    def place(b, seen):
        oh = oh_ref[pl.ds(pl.multiple_of(b * TB, TB), TB), :]
        ohf = oh.astype(F32)
        rank = jnp.dot(tri, oh, preferred_element_type=F32)
        base = (starts + seen)[0:1, :]
        d = jnp.sum(ohf * (rank + base), axis=1, keepdims=True)
        dest_ref[b] = _row(d).astype(I32)
        return seen + jnp.sum(ohf, axis=0, keepdims=True)

    lax.fori_loop(0, n_blocks, place, jnp.zeros((SUBLANES, LANES), F32))

    tile_row0 = lax.broadcasted_iota(I32, (LANES, LANES), 0).astype(F32) * SORT_TILE
    is_bucket = lax.broadcasted_iota(I32, (LANES, LANES), 1) < N_BUCKETS
    done = jnp.sum(jnp.where(is_bucket, jnp.where(ends[0:1, :] <= tile_row0, 1.0, 0.0), 0.0),
                   axis=1, keepdims=True)
    bkt = jnp.minimum(done, N_BUCKETS - 1.0)
    grp = (jnp.where(bkt >= PAIRS_PER_GROUP, 1.0, 0.0) + jnp.where(bkt >= 2 * PAIRS_PER_GROUP, 1.0, 0.0)
           + jnp.where(bkt >= 3 * PAIRS_PER_GROUP, 1.0, 0.0))
    pair = bkt - PAIRS_PER_GROUP * grp
    a = jnp.where(pair >= 3.0, 1.0, 0.0) + jnp.where(pair >= 5.0, 1.0, 0.0)
    b = pair - a * (7.0 - a) * 0.5 + a + 1.0
    e1 = EXP_PER_GROUP * grp + a
    e2 = EXP_PER_GROUP * grp + b
    meta = jnp.concatenate(
        [_row(e1), _row(e2), ends[0:1, :] * (1.0 / SORT_TILE), (ends - SORT_TILE)[0:1, :], counts[0:1, :],
         jnp.zeros((SUBLANES - 5, LANES), F32)], axis=0)
    meta_ref[...] = meta.astype(I32)


def _plan(onehot):
    T = onehot.shape[0]
    n_blocks = T // TOKEN_BLOCK
    dest, meta = pl.pallas_call(
        functools.partial(_plan_kernel, n_blocks=n_blocks),
        out_shape=[jax.ShapeDtypeStruct((n_blocks, 1, TOKEN_BLOCK), I32),
                   jax.ShapeDtypeStruct((SUBLANES, LANES), I32)],
        name="moe_plan",
    )(onehot)
    return dest.reshape(T), meta


def _dispatch_kernel(dest, last_tile, count, n_used, pay_ref, out_hbm, zeros_s, sem, zsem, *, n_tiles):
    i = pl.program_id(0)

    def zero_copy(row0):
        r0 = pl.multiple_of(row0, SORT_TILE)
        return pltpu.make_async_copy(zeros_s, out_hbm.at[pl.ds(r0, SORT_TILE), :], zsem)

    @pl.when(i == 0)
    def _():
        zeros_s[...] = jnp.zeros(zeros_s.shape, F32)
        for b in range(N_BUCKETS):
            @pl.when(count[b] > 0)
            def _():
                zero_copy(last_tile[b]).start()
        lax.fori_loop(n_used[0], n_tiles, lambda j, c: (zero_copy(j * SORT_TILE).start(), c)[1], 0)
        for b in range(N_BUCKETS):
            @pl.when(count[b] > 0)
            def _():
                zero_copy(last_tile[b]).wait()
        lax.fori_loop(n_used[0], n_tiles, lambda j, c: (zero_copy(j * SORT_TILE).wait(), c)[1], 0)

    base = i * TOKEN_BLOCK

    def issue(k, carry):
        d = dest[base + k]
        pltpu.make_async_copy(pay_ref.at[pl.ds(k, 1), :], out_hbm.at[pl.ds(d, 1), :], sem).start()
        return carry

    lax.fori_loop(0, TOKEN_BLOCK, issue, 0, unroll=8)
    pltpu.make_async_copy(pay_ref, out_hbm.at[pl.ds(0, TOKEN_BLOCK), :], sem).wait()


def _dispatch(payload, dest, meta, n_tiles):
    T = payload.shape[0]
    return pl.pallas_call(
        functools.partial(_dispatch_kernel, n_tiles=n_tiles),
        grid_spec=pltpu.PrefetchScalarGridSpec(
            num_scalar_prefetch=4,
            grid=(T // TOKEN_BLOCK,),
            in_specs=[pl.BlockSpec((TOKEN_BLOCK, PAYLOAD_W), lambda i, *_: (i, 0))],
            out_specs=pl.BlockSpec(memory_space=pl.ANY),
            scratch_shapes=[pltpu.VMEM((SORT_TILE, PAYLOAD_W), F32), pltpu.SemaphoreType.DMA(()),
                            pltpu.SemaphoreType.DMA(())],
        ),
        out_shape=jax.ShapeDtypeStruct((n_tiles * SORT_TILE, PAYLOAD_W), F32),
        compiler_params=pltpu.CompilerParams(dimension_semantics=("arbitrary",)),
        name="moe_dispatch",
    )(dest, meta[3, :N_BUCKETS], meta[4, :N_BUCKETS], meta[2, LANES - 1:LANES], payload)


def _expert_kernel(e1s, e2s, n_used, x_ref, wg_ref, wu_ref, wd_ref, o_ref):
    i = pl.program_id(0)

    @pl.when(i < n_used[0])
    def _():
        e1 = e1s[i]
        e2 = e2s[i]
        x = x_ref[:, 0:D_MODEL].astype(BF16)
        gv = x_ref[:, D_MODEL:PAYLOAD_W]
        lane = lax.broadcasted_iota(I32, gv.shape, 1)
        out = None
        for e in (e1, e2):
            ge = jnp.sum(jnp.where(lane == EXPERT_LANE0 + e, gv, 0.0), axis=-1, keepdims=True)
            hg = jnp.dot(x, wg_ref[e], preferred_element_type=F32)
            hu = jnp.dot(x, wu_ref[e], preferred_element_type=F32)
            hid = (hg * _sigmoid(hg) * hu * ge).astype(BF16)
            y = jnp.dot(hid, wd_ref[e], preferred_element_type=F32)
            out = y if out is None else out + y
        o_ref[...] = out

    @pl.when(i >= n_used[0])
    def _():
        o_ref[...] = jnp.zeros(o_ref.shape, F32)


def _experts(sorted_rows, meta, wg, wu, wd):
    n_tiles = sorted_rows.shape[0] // SORT_TILE
    return pl.pallas_call(
        _expert_kernel,
        grid_spec=pltpu.PrefetchScalarGridSpec(
            num_scalar_prefetch=3,
            grid=(n_tiles,),
            in_specs=[
                pl.BlockSpec((SORT_TILE, PAYLOAD_W), lambda i, e1, e2, nu: (jnp.minimum(i, nu[0] - 1), 0)),
                _resident(wg.shape), _resident(wu.shape), _resident(wd.shape),
            ],
            out_specs=pl.BlockSpec((SORT_TILE, D_MODEL), lambda i, *_: (i, 0)),
        ),
        out_shape=jax.ShapeDtypeStruct((n_tiles * SORT_TILE, D_MODEL), F32),
        compiler_params=pltpu.CompilerParams(
            dimension_semantics=("arbitrary",), vmem_limit_bytes=V7X_VMEM_LIMIT_BYTES),
        name="moe_experts",
    )(meta[0, :n_tiles], meta[1, :n_tiles], meta[2, LANES - 1:LANES], sorted_rows, wg, wu, wd)


def _combine_kernel(dest, x_ref, mod_ref, gf_ref, moe_hbm, o_ref, rows_s, sem):
    base = pl.program_id(0) * TOKEN_BLOCK

    def issue(k, carry):
        d = dest[base + k]
        pltpu.make_async_copy(moe_hbm.at[pl.ds(d, 1), :], rows_s.at[pl.ds(k, 1), :], sem).start()
        return carry

    lax.fori_loop(0, TOKEN_BLOCK, issue, 0, unroll=8)
    pltpu.make_async_copy(moe_hbm.at[pl.ds(0, TOKEN_BLOCK), :], rows_s, sem).wait()
    y = x_ref[...] + mod_ref[0, 5:6, :] * rows_s[...]
    o_ref[...] = _rms(y) * gf_ref[...]


def _combine(xmid, mod, mod_row, gf, moe_sorted, dest):
    T = xmid.shape[0]
    return pl.pallas_call(
        _combine_kernel,
        grid_spec=pltpu.PrefetchScalarGridSpec(
            num_scalar_prefetch=1,
            grid=(T // TOKEN_BLOCK,),
            in_specs=[
                pl.BlockSpec((TOKEN_BLOCK, D_MODEL), lambda i, *_: (i, 0)),
                pl.BlockSpec((1, 6, D_MODEL), lambda i, *_: (mod_row(i), 0, 0)),
                pl.BlockSpec((1, D_MODEL), lambda i, *_: (0, 0)),
                pl.BlockSpec(memory_space=pl.ANY),
            ],
            out_specs=pl.BlockSpec((TOKEN_BLOCK, D_MODEL), lambda i, *_: (i, 0)),
            scratch_shapes=[pltpu.VMEM((TOKEN_BLOCK, D_MODEL), F32), pltpu.SemaphoreType.DMA(())],
        ),
        out_shape=jax.ShapeDtypeStruct((T, D_MODEL), F32),
        compiler_params=pltpu.CompilerParams(dimension_semantics=("arbitrary",)),
        name="moe_combine",
    )(dest, xmid, mod, gf, moe_sorted)


def _moe(xmid, payload, onehot, mod, mod_row, gf, wg, wu, wd):
    T = xmid.shape[0]
    n_tiles = T // SORT_TILE + N_BUCKETS
    assert n_tiles <= LANES and T % TOKEN_BLOCK == 0
    dest, meta = _plan(onehot)
    sorted_rows = _dispatch(payload, dest, meta, n_tiles)
    moe_sorted = _experts(sorted_rows, meta, wg, wu, wd)
    return _combine(xmid, mod, mod_row, gf, moe_sorted, dest)


def _rope_tables(n_tokens):
    t = jnp.arange(n_tokens)
    row = (t // GRID_W).astype(F32)
    col = (t % GRID_W).astype(F32)
    freq = ROPE_THETA ** (-jnp.arange(ROPE_NF, dtype=F32) / ROPE_NF)
    ang = jnp.concatenate([row[:, None] * freq] * 2 + [col[:, None] * freq] * 2, axis=-1)
    first = (jnp.arange(HEAD_DIM) % (2 * ROPE_NF)) < ROPE_NF
    sin = jnp.sin(ang)
    return jnp.cos(ang), jnp.where(first, -sin, 0.0), jnp.where(first, 0.0, sin)


def kernel(x_prompt, x_sample, cache_k, cache_v, c, c_ctx, norm1_g, norm2_g, w_ada, b_ada, w_in, q_norm_g, k_norm_g, w_pool, pool_scale, w_branch_a, w_branch_b, w_out, w_router_group, w_router_expert, w_exp_gate, w_exp_up, w_exp_down, final_norm_g):
    assert norm1_g.shape[0] == 1, "single-layer trunk"
    B, L_ctx, _ = x_prompt.shape
    Bs, L_lat, _ = x_sample.shape
    P = cache_k.shape[2]
    assert 1 + Bs <= COND_ROWS

    cond = jnp.zeros((COND_ROWS, D_MODEL), F32).at[0].set(c_ctx).at[1:1 + Bs].set(c)
    mod = _ada(cond, w_ada[0], b_ada[0][None, :]).reshape(COND_ROWS, 6, D_MODEL)

    wpool_bd = jax.scipy.linalg.block_diag(*[w_pool[0, g] for g in range(len(POOL_WINDOWS))])
    wr = jnp.zeros((D_MODEL, LANES), F32)
    wr = wr.at[:, 0:N_EXP_GROUPS].set(w_router_group[0])
    wr = wr.at[:, EXPERT_LANE0:EXPERT_LANE0 + N_EXPERTS].set(w_router_expert[0])
    wr_hi = wr.astype(BF16)
    wr_lo = (wr - wr_hi.astype(F32)).astype(BF16)
    mix_w = (norm1_g[0][None, :], w_in[0].astype(BF16), q_norm_g[0][None, :], k_norm_g[0][None, :],
             wpool_bd.astype(BF16), pool_scale[0][None, :], w_branch_a[0].astype(BF16),
             w_branch_b[0].astype(BF16), w_out[0].astype(BF16),
             norm2_g[0][None, :], jnp.concatenate([wr_hi, wr_lo], axis=1))
    moe_w = (final_norm_g[None, :], w_exp_gate[0].astype(BF16), w_exp_up[0].astype(BF16),
             w_exp_down[0].astype(BF16))

    xp2 = x_prompt.reshape(B * L_ctx, D_MODEL)
    xmid_p, pay_p, oh_p, knew, vnew = _mix(xp2, mod, lambda i: 0, None, None, mix_w,
                                           S=2, L=L_ctx, emit_kv=True)
    y_prompt = _moe(xmid_p, pay_p, oh_p, mod, lambda i: 0, *moe_w)

    xs2 = x_sample.reshape(Bs * L_lat, D_MODEL)
    cache = (cache_k[:, 0].reshape(Bs, P, KV_W), cache_v[:, 0].reshape(Bs, P, KV_W))
    xmid_s, pay_s, oh_s = _mix(xs2, mod, lambda i: 1 + i, cache, _rope_tables(L_lat), mix_w,
                               S=1, L=L_lat, emit_kv=False)
    blocks_per_seq = L_lat // TOKEN_BLOCK
    y_sample = _moe(xmid_s, pay_s, oh_s, mod, lambda i: 1 + i // blocks_per_seq, *moe_w)

    return (y_prompt.reshape(B, L_ctx, D_MODEL), y_sample.reshape(Bs, L_lat, D_MODEL),
            knew.reshape(B, 1, L_ctx, N_KV_HEADS, HEAD_DIM), vnew.reshape(B, 1, L_ctx, N_KV_HEADS, HEAD_DIM))
```

```python
import functools

import jax
import jax.numpy as jnp
from jax import lax
from jax.experimental import pallas as pl
from jax.experimental.pallas import tpu as pltpu

F32 = jnp.float32
BF16 = jnp.bfloat16
I32 = jnp.int32

D_MODEL = 1024
HEAD_DIM = 128
N_HEADS = 8
N_KV_HEADS = 2
GROUP = N_HEADS // N_KV_HEADS
ATTN_W = N_HEADS * HEAD_DIM
KV_W = N_KV_HEADS * HEAD_DIM
POOL_WINDOWS = (2, 4, 8, 16)
POOL_GC = 128
POOL_W = POOL_GC * len(POOL_WINDOWS)
IN_W = ATTN_W + 2 * KV_W + POOL_W + 2 * D_MODEL
GATE_COL = ATTN_W + 2 * KV_W + POOL_W
GRID_W = 64
ROPE_THETA = 10000.0
ROPE_NF = HEAD_DIM // 4
N_EXP_GROUPS = 4
EXP_PER_GROUP = 4
N_EXPERTS = 16
D_EXPERT = 256
EPS = 1e-6

LANES = 128
SUBLANES = 8
COND_ROWS = SUBLANES
POOL_HALO = 8
ROW_BLOCK = 256
ADA_COLS = 768
EXPERT_LANE0 = N_EXP_GROUPS
PAIRS_PER_GROUP = EXP_PER_GROUP * (EXP_PER_GROUP - 1) // 2
N_BUCKETS = N_EXP_GROUPS * PAIRS_PER_GROUP
PAYLOAD_W = D_MODEL + LANES
SORT_TILE = 256
ZERO_CHUNK = 64
TOKEN_BLOCK = 512
V7X_VMEM_LIMIT_BYTES = 56 * 1024 * 1024


def _sigmoid(x):
    return 1.0 / (1.0 + jnp.exp(-x))


def _rms(x):
    return x * lax.rsqrt(jnp.mean(x * x, axis=-1, keepdims=True) + EPS)


def _resident(shape):
    zeros = (0,) * len(shape)
    return pl.BlockSpec(shape, lambda i, *_: zeros, pipeline_mode=pl.Buffered(1))


def _row(x):
    return jnp.transpose(jnp.broadcast_to(x, (x.shape[0], LANES)))[0:1, :]


def _ada_kernel(c_ref, w_ref, b_ref, o_ref):
    c = c_ref[...]
    s = (c * _sigmoid(c)).astype(BF16)
    o_ref[...] = jnp.dot(s, w_ref[...].astype(BF16), preferred_element_type=F32) + b_ref[...]


def _ada(cond, w_ada, b_ada):
    n = w_ada.shape[1]
    return pl.pallas_call(
        _ada_kernel,
        grid=(n // ADA_COLS,),
        in_specs=[
            pl.BlockSpec((COND_ROWS, D_MODEL), lambda j: (0, 0)),
            pl.BlockSpec((D_MODEL, ADA_COLS), lambda j: (0, j)),
            pl.BlockSpec((1, ADA_COLS), lambda j: (0, j)),
        ],
        out_specs=pl.BlockSpec((COND_ROWS, ADA_COLS), lambda j: (0, j)),
        out_shape=jax.ShapeDtypeStruct((COND_ROWS, n), F32),
        name="ada_mod",
    )(cond, w_ada, b_ada)


def _route(logits):
    lane = lax.broadcasted_iota(I32, logits.shape, 1).astype(F32)
    neg = jnp.float32(-1e30)
    far = jnp.float32(LANES)
    is_g = lane < N_EXP_GROUPS
    gl = jnp.where(is_g, logits, neg)
    gmax = jnp.max(gl, axis=-1, keepdims=True)
    gsel = jnp.min(jnp.where(gl == gmax, lane, far), axis=-1, keepdims=True)
    psel = 1.0 / jnp.sum(jnp.where(is_g, jnp.exp(gl - gmax), 0.0), axis=-1, keepdims=True)
    e_lo = EXPERT_LANE0 + EXP_PER_GROUP * gsel
    el = jnp.where(lane >= e_lo, jnp.where(lane < e_lo + EXP_PER_GROUP, logits, neg), neg)
    v1 = jnp.max(el, axis=-1, keepdims=True)
    i1 = jnp.min(jnp.where(el == v1, lane, far), axis=-1, keepdims=True)
    el2 = jnp.where(lane == i1, neg, el)
    v2 = jnp.max(el2, axis=-1, keepdims=True)
    i2 = jnp.min(jnp.where(el2 == v2, jnp.where(lane == i1, far, lane), far), axis=-1, keepdims=True)
    e2 = jnp.exp(v2 - v1)
    w1 = psel / (1.0 + e2)
    w2 = psel * e2 / (1.0 + e2)
    gate = jnp.where(lane == i1, w1, jnp.where(lane == i2, w2, 0.0))
    a = jnp.minimum(i1, i2) - e_lo
    b = jnp.maximum(i1, i2) - e_lo
    pair = a * (7.0 - a) * 0.5 + (b - a - 1.0)
    return gate, gsel * PAIRS_PER_GROUP + pair


def _mix_kernel(*refs, S, L, P, use_rope, emit_kv):
    it = iter(refs)
    x_ref = next(it)
    mod_ref = next(it)
    if P:
        ck_ref = next(it)
        cv_ref = next(it)
    if use_rope:
        cos_ref = next(it)
        sneg_ref = next(it)
        spos_ref = next(it)
    (g1_ref, win_ref, qg_ref, kg_ref, wpool_ref, pscale_ref, wa_ref, wb_ref, wo_ref,
     g2_ref, wr_ref) = (next(it) for _ in range(11))
    xmid_ref = next(it)
    pay_ref = next(it)
    oh_ref = next(it)
    if emit_kv:
        knew_ref = next(it)
        vnew_ref = next(it)
    q_s, k_s, v_s, xp_s, h_s, attn_s = (next(it) for _ in range(6))

    TM = S * L
    RB = ROW_BLOCK
    scale = HEAD_DIM ** -0.5

    sh1 = mod_ref[0, 0:1, :]
    gain1 = g1_ref[...] * (1.0 + mod_ref[0, 1:2, :])
    gt1 = mod_ref[0, 2:3, :]
    sh2 = mod_ref[0, 3:4, :]
    gain2 = g2_ref[...] * (1.0 + mod_ref[0, 4:5, :])
    qg = qg_ref[...]
    kg = kg_ref[...]

    def project(r, carry):
        r0 = pl.multiple_of(r * RB, RB)
        s = r0 // L
        o = pl.multiple_of(r0 % L, RB)
        hb = (_rms(x_ref[pl.ds(r0, RB), :]) * gain1 + sh1).astype(BF16)
        h_s[pl.ds(r0, RB), :] = hb
        p1 = jnp.dot(hb, win_ref[:, 0:GATE_COL], preferred_element_type=F32)
        if use_rope:
            cs = cos_ref[pl.ds(o, RB), :]
            sn = sneg_ref[pl.ds(o, RB), :]
            sp = spos_ref[pl.ds(o, RB), :]

        def rope(t):
            return (t * cs + pltpu.roll(t, HEAD_DIM - ROPE_NF, 1) * sn + pltpu.roll(t, ROPE_NF, 1) * sp)

        for hd in range(N_HEADS):
            qh = _rms(p1[:, hd * HEAD_DIM:(hd + 1) * HEAD_DIM]) * qg
            if use_rope:
                qh = rope(qh)
            q_s[hd, pl.ds(r0, RB), :] = qh.astype(BF16)
        for kh in range(N_KV_HEADS):
            c0 = ATTN_W + kh * HEAD_DIM
            kk = _rms(p1[:, c0:c0 + HEAD_DIM]) * kg
            if emit_kv:
                knew_ref[pl.ds(r0, RB), kh * HEAD_DIM:(kh + 1) * HEAD_DIM] = kk
            if use_rope:
                kk = rope(kk)
            k_s[s, pl.ds(P + o, RB), kh * HEAD_DIM:(kh + 1) * HEAD_DIM] = kk.astype(BF16)
        vv = p1[:, ATTN_W + KV_W:ATTN_W + 2 * KV_W]
        if emit_kv:
            vnew_ref[pl.ds(r0, RB), :] = vv
        v_s[s, pl.ds(P + o, RB), :] = vv.astype(BF16)
        xp_s[s, pl.ds(POOL_HALO + o, RB), :] = p1[:, ATTN_W + 2 * KV_W:GATE_COL]
        return carry

    @pl.when(pl.program_id(1) == 0)
    def _():
        if P:
            k_s[0, 0:P, :] = ck_ref[0].astype(BF16)
            v_s[0, 0:P, :] = cv_ref[0].astype(BF16)
        xp_s[:, 0:POOL_HALO, :] = jnp.zeros((S, POOL_HALO, POOL_W), F32)
        xp_s[:, L + POOL_HALO:L + 2 * POOL_HALO, :] = jnp.zeros((S, POOL_HALO, POOL_W), F32)
        lax.fori_loop(0, TM // RB, project, 0)

    def mix(r):
        r0 = pl.multiple_of(r * RB, RB)
        s = r0 // L
        o = pl.multiple_of(r0 % L, RB)

        for kh in range(N_KV_HEADS):
            k = k_s[s, :, kh * HEAD_DIM:(kh + 1) * HEAD_DIM]
            v = v_s[s, :, kh * HEAD_DIM:(kh + 1) * HEAD_DIM]
            q4 = q_s[kh * GROUP:(kh + 1) * GROUP, pl.ds(r0, RB), :].reshape(GROUP * RB, HEAD_DIM)
            sc = lax.dot_general(q4, k, (((1,), (1,)), ((), ())), preferred_element_type=F32) * scale
            e = jnp.exp(sc - jnp.max(sc, axis=-1, keepdims=True))
            den = jnp.sum(e, axis=-1, keepdims=True)
            o4 = jnp.dot(e.astype(BF16), v, preferred_element_type=F32) / den
            for g in range(GROUP):
                hd = kh * GROUP + g
                attn_s[:, hd * HEAD_DIM:(hd + 1) * HEAD_DIM] = o4[g * RB:(g + 1) * RB].astype(BF16)
        a = jnp.dot(attn_s[...], wa_ref[...], preferred_element_type=F32)

        t = o + lax.broadcasted_iota(I32, (RB, 1), 0)
        RW = RB + 2 * POOL_HALO
        parts = []
        for gi, w in enumerate(POOL_WINDOWS):
            cols = slice(gi * POOL_GC, (gi + 1) * POOL_GC)
            xw = xp_s[s, pl.ds(o, RW), cols]
            run = xw
            span = 1
            while span < w:
                run = run + pltpu.roll(run, span, 0)
                span *= 2
            if w // 2 > 1:
                run = pltpu.roll(run, RW - (w // 2 - 1), 0)
            tot = run[POOL_HALO:POOL_HALO + RB]
            cnt = (jnp.minimum(t + w // 2, L) - jnp.maximum(t - w // 2, 0)).astype(F32)
            parts.append(tot / cnt - xw[POOL_HALO:POOL_HALO + RB])
        dpool = jnp.concatenate(parts, axis=1).astype(BF16)
        pooled = jnp.dot(dpool, wpool_ref[...], preferred_element_type=F32) * pscale_ref[...]
        b = jnp.dot(pooled.astype(BF16), wb_ref[...], preferred_element_type=F32)

        gates = jnp.dot(h_s[pl.ds(r0, RB), :], win_ref[:, GATE_COL:IN_W], preferred_element_type=F32)
        merged = _sigmoid(gates[:, 0:D_MODEL]) * a + _sigmoid(gates[:, D_MODEL:2 * D_MODEL]) * b
        u = jnp.dot(merged.astype(BF16), wo_ref[...], preferred_element_type=F32)
        xm = x_ref[pl.ds(r0, RB), :] + gt1 * u
        xmid_ref[...] = xm

        h2 = _rms(xm) * gain2 + sh2
        hi = h2.astype(BF16)
        lo = (h2 - hi.astype(F32)).astype(BF16)
        l1 = jnp.dot(hi, wr_ref[...], preferred_element_type=F32)
        l2 = jnp.dot(lo, wr_ref[:, 0:LANES], preferred_element_type=F32)
        gate, bucket = _route(l1[:, 0:LANES] + l1[:, LANES:2 * LANES] + l2)
        pay_ref[:, 0:D_MODEL] = h2
        pay_ref[:, D_MODEL:PAYLOAD_W] = gate
        lane = lax.broadcasted_iota(I32, (RB, LANES), 1).astype(F32)
        oh_ref[...] = jnp.where(lane == bucket, 1.0, 0.0).astype(BF16)

    mix(pl.program_id(1))


def _mix(x2d, mod, mod_row, cache, rope_tabs, weights, *, S, L, emit_kv):
    T = x2d.shape[0]
    TM = S * L
    P = cache[0].shape[1] if cache is not None else 0
    use_rope = rope_tabs is not None
    assert T % TM == 0 and L % ROW_BLOCK == 0
    assert not (use_rope or P) or S == 1
    Lk = P + L

    args = [x2d, mod]
    nrb = TM // ROW_BLOCK
    in_specs = [
        pl.BlockSpec((TM, D_MODEL), lambda i, j: (i, 0)),
        pl.BlockSpec((1, 6, D_MODEL), lambda i, j: (mod_row(i), 0, 0)),
    ]
    if P:
        args += list(cache)
        in_specs += [pl.BlockSpec((1, P, KV_W), lambda i, j: (i, 0, 0))] * 2
    if use_rope:
        args += list(rope_tabs)
        in_specs += [_resident((L, HEAD_DIM))] * 3
    args += list(weights)
    in_specs += [_resident(w.shape) for w in weights]

    out_shape = [jax.ShapeDtypeStruct((T, D_MODEL), F32), jax.ShapeDtypeStruct((T, PAYLOAD_W), F32),
                 jax.ShapeDtypeStruct((T, LANES), BF16)]
    out_specs = [pl.BlockSpec((ROW_BLOCK, D_MODEL), lambda i, j: (i * nrb + j, 0)),
                 pl.BlockSpec((ROW_BLOCK, PAYLOAD_W), lambda i, j: (i * nrb + j, 0)),
                 pl.BlockSpec((ROW_BLOCK, LANES), lambda i, j: (i * nrb + j, 0))]
    if emit_kv:
        out_shape += [jax.ShapeDtypeStruct((T, KV_W), F32)] * 2
        out_specs += [pl.BlockSpec((TM, KV_W), lambda i, j: (i, 0))] * 2

    scratch = [
        pltpu.VMEM((N_HEADS, TM, HEAD_DIM), BF16),
        pltpu.VMEM((S, Lk, KV_W), BF16),
        pltpu.VMEM((S, Lk, KV_W), BF16),
        pltpu.VMEM((S, L + 2 * POOL_HALO, POOL_W), F32),
        pltpu.VMEM((TM, D_MODEL), BF16),
        pltpu.VMEM((ROW_BLOCK, ATTN_W), BF16),
    ]
    kern = functools.partial(_mix_kernel, S=S, L=L, P=P, use_rope=use_rope, emit_kv=emit_kv)
    return pl.pallas_call(
        kern,
        grid=(T // TM, nrb),
        in_specs=in_specs,
        out_specs=out_specs,
        out_shape=out_shape,
        scratch_shapes=scratch,
        compiler_params=pltpu.CompilerParams(
            dimension_semantics=("arbitrary", "arbitrary"), vmem_limit_bytes=V7X_VMEM_LIMIT_BYTES),
        name="mixer_rope" if use_rope else "mixer_ctx",
    )(*args)


def _plan_kernel(oh_ref, dest_ref, meta_ref, *, n_blocks, n_tiles):
    TB = TOKEN_BLOCK
    lane = lax.broadcasted_iota(I32, (SUBLANES, LANES), 1)

    def count(b, acc):
        oh = oh_ref[pl.ds(pl.multiple_of(b * TB, TB), TB), :].astype(F32)
        return acc + jnp.sum(oh, axis=0, keepdims=True)

    counts = lax.fori_loop(0, n_blocks, count, jnp.zeros((SUBLANES, LANES), F32))
    padded = jnp.ceil(counts * (1.0 / SORT_TILE)) * SORT_TILE
    ends = padded
    step = 1
    while step < LANES:
        ends = ends + jnp.where(lane >= step, pltpu.roll(ends, step, 1), 0.0)
        step *= 2
    starts = ends - padded

    tri = jnp.where(lax.broadcasted_iota(I32, (TB, TB), 1) < lax.broadcasted_iota(I32, (TB, TB), 0),
                    1.0, 0.0).astype(BF16)

    def place(b, seen):
        oh = oh_ref[pl.ds(pl.multiple_of(b * TB, TB), TB), :]
        ohf = oh.astype(F32)
        rank = jnp.dot(tri, oh, preferred_element_type=F32)
        base = (starts + seen)[0:1, :]
        d = jnp.sum(ohf * (rank + base), axis=1, keepdims=True)
        dest_ref[b] = _row(d).astype(I32)
        return seen + jnp.sum(ohf, axis=0, keepdims=True)

    lax.fori_loop(0, n_blocks, place, jnp.zeros((SUBLANES, LANES), F32))

    tile_row0 = lax.broadcasted_iota(I32, (LANES, LANES), 0).astype(F32) * SORT_TILE
    is_bucket = lax.broadcasted_iota(I32, (LANES, LANES), 1) < N_BUCKETS
    done = jnp.sum(jnp.where(is_bucket, jnp.where(ends[0:1, :] <= tile_row0, 1.0, 0.0), 0.0),
                   axis=1, keepdims=True)
    bkt = jnp.minimum(done, N_BUCKETS - 1.0)
    grp = (jnp.where(bkt >= PAIRS_PER_GROUP, 1.0, 0.0) + jnp.where(bkt >= 2 * PAIRS_PER_GROUP, 1.0, 0.0)
           + jnp.where(bkt >= 3 * PAIRS_PER_GROUP, 1.0, 0.0))
    pair = bkt - PAIRS_PER_GROUP * grp
    a = jnp.where(pair >= 3.0, 1.0, 0.0) + jnp.where(pair >= 5.0, 1.0, 0.0)
    b = pair - a * (7.0 - a) * 0.5 + a + 1.0
    e1 = EXP_PER_GROUP * grp + a
    e2 = EXP_PER_GROUP * grp + b
    tail = lane[0:1, :] == N_BUCKETS
    fill_lo = jnp.where(tail, ends, jnp.floor((starts + counts) * (1.0 / ZERO_CHUNK)) * ZERO_CHUNK)[0:1, :]
    fill_hi = jnp.where(tail, float(n_tiles * SORT_TILE), ends)[0:1, :]
    meta = jnp.concatenate(
        [_row(e1), _row(e2), ends[0:1, :] * (1.0 / SORT_TILE), fill_lo * (1.0 / ZERO_CHUNK),
         fill_hi * (1.0 / ZERO_CHUNK), jnp.zeros((SUBLANES - 5, LANES), F32)], axis=0)
    meta_ref[...] = meta.astype(I32)


def _plan(onehot, n_tiles):
    T = onehot.shape[0]
    n_blocks = T // TOKEN_BLOCK
    dest, meta = pl.pallas_call(
        functools.partial(_plan_kernel, n_blocks=n_blocks, n_tiles=n_tiles),
        out_shape=[jax.ShapeDtypeStruct((n_blocks, 1, TOKEN_BLOCK), I32),
                   jax.ShapeDtypeStruct((SUBLANES, LANES), I32)],
        name="moe_plan",
    )(onehot)
    return dest.reshape(T), meta


def _dispatch_kernel(dest, fill_lo, fill_hi, pay_ref, out_hbm, zeros_s, sem, zsem):
    i = pl.program_id(0)

    def zero_copy(c):
        r0 = pl.multiple_of(c * ZERO_CHUNK, ZERO_CHUNK)
        return pltpu.make_async_copy(zeros_s, out_hbm.at[pl.ds(r0, ZERO_CHUNK), :], zsem)

    @pl.when(i == 0)
    def _():
        zeros_s[...] = jnp.zeros(zeros_s.shape, F32)
        for b in range(N_BUCKETS + 1):
            lax.fori_loop(fill_lo[b], fill_hi[b], lambda c, x: (zero_copy(c).start(), x)[1], 0)
        for b in range(N_BUCKETS + 1):
            lax.fori_loop(fill_lo[b], fill_hi[b], lambda c, x: (zero_copy(c).wait(), x)[1], 0)

    base = i * TOKEN_BLOCK
    for k in range(TOKEN_BLOCK):
        row = pltpu.make_async_copy(pay_ref.at[pl.ds(k, 1), :], out_hbm.at[pl.ds(dest[base + k], 1), :], sem)
        row.start(priority=k % 2)
    pltpu.make_async_copy(pay_ref, out_hbm.at[pl.ds(0, TOKEN_BLOCK), :], sem).wait()


def _dispatch(payload, dest, meta, n_tiles):
    T = payload.shape[0]
    return pl.pallas_call(
        _dispatch_kernel,
        grid_spec=pltpu.PrefetchScalarGridSpec(
            num_scalar_prefetch=3,
            grid=(T // TOKEN_BLOCK,),
            in_specs=[pl.BlockSpec((TOKEN_BLOCK, PAYLOAD_W), lambda i, *_: (i, 0))],
            out_specs=pl.BlockSpec(memory_space=pl.ANY),
            scratch_shapes=[pltpu.VMEM((ZERO_CHUNK, PAYLOAD_W), F32), pltpu.SemaphoreType.DMA(()),
                            pltpu.SemaphoreType.DMA(())],
        ),
        out_shape=jax.ShapeDtypeStruct((n_tiles * SORT_TILE, PAYLOAD_W), F32),
        compiler_params=pltpu.CompilerParams(dimension_semantics=("arbitrary",)),
        name="moe_dispatch",
    )(dest, meta[3, :N_BUCKETS + 1], meta[4, :N_BUCKETS + 1], payload)


def _expert_kernel(e1s, e2s, n_used, x_ref, wg_ref, wu_ref, wd_ref, o_ref):
    i = pl.program_id(0)

    @pl.when(i < n_used[0])
    def _():
        e1 = e1s[i]
        e2 = e2s[i]
        x = x_ref[:, 0:D_MODEL].astype(BF16)
        gv = x_ref[:, D_MODEL:PAYLOAD_W]
        lane = lax.broadcasted_iota(I32, gv.shape, 1)
        out = None
        for e in (e1, e2):
            ge = jnp.sum(jnp.where(lane == EXPERT_LANE0 + e, gv, 0.0), axis=-1, keepdims=True)
            hg = jnp.dot(x, wg_ref[e], preferred_element_type=F32)
            hu = jnp.dot(x, wu_ref[e], preferred_element_type=F32)
            hid = (hg * _sigmoid(hg) * hu * ge).astype(BF16)
            y = jnp.dot(hid, wd_ref[e], preferred_element_type=F32)
            out = y if out is None else out + y
        o_ref[...] = out

    @pl.when(i >= n_used[0])
    def _():
        o_ref[...] = jnp.zeros(o_ref.shape, F32)


def _experts(sorted_rows, meta, wg, wu, wd):
    n_tiles = sorted_rows.shape[0] // SORT_TILE
    return pl.pallas_call(
        _expert_kernel,
        grid_spec=pltpu.PrefetchScalarGridSpec(
            num_scalar_prefetch=3,
            grid=(n_tiles,),
            in_specs=[
                pl.BlockSpec((SORT_TILE, PAYLOAD_W), lambda i, e1, e2, nu: (jnp.minimum(i, nu[0] - 1), 0)),
                _resident(wg.shape), _resident(wu.shape), _resident(wd.shape),
            ],
            out_specs=pl.BlockSpec((SORT_TILE, D_MODEL), lambda i, *_: (i, 0)),
        ),
        out_shape=jax.ShapeDtypeStruct((n_tiles * SORT_TILE, D_MODEL), F32),
        compiler_params=pltpu.CompilerParams(
            dimension_semantics=("arbitrary",), vmem_limit_bytes=V7X_VMEM_LIMIT_BYTES),
        name="moe_experts",
    )(meta[0, :n_tiles], meta[1, :n_tiles], meta[2, LANES - 1:LANES], sorted_rows, wg, wu, wd)


def _combine_kernel(dest, x_ref, mod_ref, gf_ref, moe_hbm, o_ref, rows_s, sem):
    base = pl.program_id(0) * TOKEN_BLOCK
    for k in range(TOKEN_BLOCK):
        row = pltpu.make_async_copy(moe_hbm.at[pl.ds(dest[base + k], 1), :], rows_s.at[pl.ds(k, 1), :], sem)
        row.start(priority=k % 2)
    pltpu.make_async_copy(moe_hbm.at[pl.ds(0, TOKEN_BLOCK), :], rows_s, sem).wait()
    y = x_ref[...] + mod_ref[0, 5:6, :] * rows_s[...]
    o_ref[...] = _rms(y) * gf_ref[...]


def _combine(xmid, mod, mod_row, gf, moe_sorted, dest):
    T = xmid.shape[0]
    return pl.pallas_call(
        _combine_kernel,
        grid_spec=pltpu.PrefetchScalarGridSpec(
            num_scalar_prefetch=1,
            grid=(T // TOKEN_BLOCK,),
            in_specs=[
                pl.BlockSpec((TOKEN_BLOCK, D_MODEL), lambda i, *_: (i, 0)),
                pl.BlockSpec((1, 6, D_MODEL), lambda i, *_: (mod_row(i), 0, 0)),
                pl.BlockSpec((1, D_MODEL), lambda i, *_: (0, 0)),
                pl.BlockSpec(memory_space=pl.ANY),
            ],
            out_specs=pl.BlockSpec((TOKEN_BLOCK, D_MODEL), lambda i, *_: (i, 0)),
            scratch_shapes=[pltpu.VMEM((TOKEN_BLOCK, D_MODEL), F32), pltpu.SemaphoreType.DMA(())],
        ),
        out_shape=jax.ShapeDtypeStruct((T, D_MODEL), F32),
        compiler_params=pltpu.CompilerParams(dimension_semantics=("arbitrary",)),
        name="moe_combine",
    )(dest, xmid, mod, gf, moe_sorted)


def _moe(xmid, payload, onehot, mod, mod_row, gf, wg, wu, wd):
    T = xmid.shape[0]
    n_tiles = T // SORT_TILE + N_BUCKETS
    assert n_tiles <= LANES and T % TOKEN_BLOCK == 0
    dest, meta = _plan(onehot, n_tiles)
    sorted_rows = _dispatch(payload, dest, meta, n_tiles)
    moe_sorted = _experts(sorted_rows, meta, wg, wu, wd)
    return _combine(xmid, mod, mod_row, gf, moe_sorted, dest)


def _rope_tables(n_tokens):
    t = jnp.arange(n_tokens)
    row = (t // GRID_W).astype(F32)
    col = (t % GRID_W).astype(F32)
    freq = ROPE_THETA ** (-jnp.arange(ROPE_NF, dtype=F32) / ROPE_NF)
    ang = jnp.concatenate([row[:, None] * freq] * 2 + [col[:, None] * freq] * 2, axis=-1)
    first = (jnp.arange(HEAD_DIM) % (2 * ROPE_NF)) < ROPE_NF
    sin = jnp.sin(ang)
    return jnp.cos(ang), jnp.where(first, -sin, 0.0), jnp.where(first, 0.0, sin)


def kernel(x_prompt, x_sample, cache_k, cache_v, c, c_ctx, norm1_g, norm2_g, w_ada, b_ada, w_in, q_norm_g, k_norm_g, w_pool, pool_scale, w_branch_a, w_branch_b, w_out, w_router_group, w_router_expert, w_exp_gate, w_exp_up, w_exp_down, final_norm_g):
    assert norm1_g.shape[0] == 1, "single-layer trunk"
    B, L_ctx, _ = x_prompt.shape
    Bs, L_lat, _ = x_sample.shape
    P = cache_k.shape[2]
    assert 1 + Bs <= COND_ROWS

    cond = jnp.zeros((COND_ROWS, D_MODEL), F32).at[0].set(c_ctx).at[1:1 + Bs].set(c)
    mod = _ada(cond, w_ada[0], b_ada[0][None, :]).reshape(COND_ROWS, 6, D_MODEL)

    wpool_bd = jax.scipy.linalg.block_diag(*[w_pool[0, g] for g in range(len(POOL_WINDOWS))])
    wr = jnp.zeros((D_MODEL, LANES), F32)
    wr = wr.at[:, 0:N_EXP_GROUPS].set(w_router_group[0])
    wr = wr.at[:, EXPERT_LANE0:EXPERT_LANE0 + N_EXPERTS].set(w_router_expert[0])
    wr_hi = wr.astype(BF16)
    wr_lo = (wr - wr_hi.astype(F32)).astype(BF16)
    mix_w = (norm1_g[0][None, :], w_in[0].astype(BF16), q_norm_g[0][None, :], k_norm_g[0][None, :],
             wpool_bd.astype(BF16), pool_scale[0][None, :], w_branch_a[0].astype(BF16),
             w_branch_b[0].astype(BF16), w_out[0].astype(BF16),
             norm2_g[0][None, :], jnp.concatenate([wr_hi, wr_lo], axis=1))
    moe_w = (final_norm_g[None, :], w_exp_gate[0].astype(BF16), w_exp_up[0].astype(BF16),
             w_exp_down[0].astype(BF16))

    xp2 = x_prompt.reshape(B * L_ctx, D_MODEL)
    xmid_p, pay_p, oh_p, knew, vnew = _mix(xp2, mod, lambda i: 0, None, None, mix_w,
                                           S=2, L=L_ctx, emit_kv=True)
    y_prompt = _moe(xmid_p, pay_p, oh_p, mod, lambda i: 0, *moe_w)

    xs2 = x_sample.reshape(Bs * L_lat, D_MODEL)
    cache = (cache_k[:, 0].reshape(Bs, P, KV_W), cache_v[:, 0].reshape(Bs, P, KV_W))
    xmid_s, pay_s, oh_s = _mix(xs2, mod, lambda i: 1 + i, cache, _rope_tables(L_lat), mix_w,
                               S=1, L=L_lat, emit_kv=False)
    blocks_per_seq = L_lat // TOKEN_BLOCK
    y_sample = _moe(xmid_s, pay_s, oh_s, mod, lambda i: 1 + i // blocks_per_seq, *moe_w)

    return (y_prompt.reshape(B, L_ctx, D_MODEL), y_sample.reshape(Bs, L_lat, D_MODEL),
            knew.reshape(B, 1, L_ctx, N_KV_HEADS, HEAD_DIM), vnew.reshape(B, 1, L_ctx, N_KV_HEADS, HEAD_DIM))
```

```python
import functools

import jax
import jax.numpy as jnp
from jax import lax
from jax.experimental import pallas as pl
from jax.experimental.pallas import tpu as pltpu
from jax.experimental.pallas import tpu_sc as plsc

F32 = jnp.float32
BF16 = jnp.bfloat16
I32 = jnp.int32

D_MODEL = 1024
HEAD_DIM = 128
N_HEADS = 8
N_KV_HEADS = 2
GROUP = N_HEADS // N_KV_HEADS
ATTN_W = N_HEADS * HEAD_DIM
KV_W = N_KV_HEADS * HEAD_DIM
POOL_WINDOWS = (2, 4, 8, 16)
POOL_GC = 128
POOL_W = POOL_GC * len(POOL_WINDOWS)
IN_W = ATTN_W + 2 * KV_W + POOL_W + 2 * D_MODEL
GATE_COL = ATTN_W + 2 * KV_W + POOL_W
GRID_W = 64
ROPE_THETA = 10000.0
ROPE_NF = HEAD_DIM // 4
N_EXP_GROUPS = 4
EXP_PER_GROUP = 4
N_EXPERTS = 16
D_EXPERT = 256
EPS = 1e-6

LANES = 128
SUBLANES = 8
COND_ROWS = SUBLANES
POOL_HALO = 8
ROW_BLOCK = 256
ADA_COLS = 768
EXPERT_LANE0 = N_EXP_GROUPS
PAIRS_PER_GROUP = EXP_PER_GROUP * (EXP_PER_GROUP - 1) // 2
N_BUCKETS = N_EXP_GROUPS * PAIRS_PER_GROUP
PAYLOAD_W = D_MODEL + LANES
SORT_TILE = 256
ZERO_CHUNK = 64
TOKEN_BLOCK = 512
ROW_CHUNKS = D_MODEL // LANES
SC_CORES = 2
SC_SUBCORES = 16
SC_WORKERS = SC_CORES * SC_SUBCORES
SC_GATHER_CHUNK = 64
V7X_VMEM_LIMIT_BYTES = 56 * 1024 * 1024


def _sigmoid(x):
    return 1.0 / (1.0 + jnp.exp(-x))


def _rms(x):
    return x * lax.rsqrt(jnp.mean(x * x, axis=-1, keepdims=True) + EPS)


def _resident(shape):
    zeros = (0,) * len(shape)
    return pl.BlockSpec(shape, lambda i, *_: zeros, pipeline_mode=pl.Buffered(1))


def _row(x):
    return jnp.transpose(jnp.broadcast_to(x, (x.shape[0], LANES)))[0:1, :]


def _ada_kernel(c_ref, w_ref, b_ref, o_ref):
    c = c_ref[...]
    s = (c * _sigmoid(c)).astype(BF16)
    o_ref[...] = jnp.dot(s, w_ref[...].astype(BF16), preferred_element_type=F32) + b_ref[...]


def _ada(cond, w_ada, b_ada):
    n = w_ada.shape[1]
    return pl.pallas_call(
        _ada_kernel,
        grid=(n // ADA_COLS,),
        in_specs=[
            pl.BlockSpec((COND_ROWS, D_MODEL), lambda j: (0, 0)),
            pl.BlockSpec((D_MODEL, ADA_COLS), lambda j: (0, j)),
            pl.BlockSpec((1, ADA_COLS), lambda j: (0, j)),
        ],
        out_specs=pl.BlockSpec((COND_ROWS, ADA_COLS), lambda j: (0, j)),
        out_shape=jax.ShapeDtypeStruct((COND_ROWS, n), F32),
        name="ada_mod",
    )(cond, w_ada, b_ada)


def _route(logits):
    lane = lax.broadcasted_iota(I32, logits.shape, 1).astype(F32)
    neg = jnp.float32(-1e30)
    far = jnp.float32(LANES)
    is_g = lane < N_EXP_GROUPS
    gl = jnp.where(is_g, logits, neg)
    gmax = jnp.max(gl, axis=-1, keepdims=True)
    gsel = jnp.min(jnp.where(gl == gmax, lane, far), axis=-1, keepdims=True)
    psel = 1.0 / jnp.sum(jnp.where(is_g, jnp.exp(gl - gmax), 0.0), axis=-1, keepdims=True)
    e_lo = EXPERT_LANE0 + EXP_PER_GROUP * gsel
    el = jnp.where(lane >= e_lo, jnp.where(lane < e_lo + EXP_PER_GROUP, logits, neg), neg)
    v1 = jnp.max(el, axis=-1, keepdims=True)
    i1 = jnp.min(jnp.where(el == v1, lane, far), axis=-1, keepdims=True)
    el2 = jnp.where(lane == i1, neg, el)
    v2 = jnp.max(el2, axis=-1, keepdims=True)
    i2 = jnp.min(jnp.where(el2 == v2, jnp.where(lane == i1, far, lane), far), axis=-1, keepdims=True)
    e2 = jnp.exp(v2 - v1)
    w1 = psel / (1.0 + e2)
    w2 = psel * e2 / (1.0 + e2)
    gate = jnp.where(lane == i1, w1, jnp.where(lane == i2, w2, 0.0))
    a = jnp.minimum(i1, i2) - e_lo
    b = jnp.maximum(i1, i2) - e_lo
    pair = a * (7.0 - a) * 0.5 + (b - a - 1.0)
    return gate, gsel * PAIRS_PER_GROUP + pair


def _mix_kernel(*refs, S, L, P, use_rope, emit_kv):
    it = iter(refs)
    x_ref = next(it)
    mod_ref = next(it)
    if P:
        ck_ref = next(it)
        cv_ref = next(it)
    if use_rope:
        cos_ref = next(it)
        sneg_ref = next(it)
        spos_ref = next(it)
    (g1_ref, win_ref, qg_ref, kg_ref, wpool_ref, pscale_ref, wa_ref, wb_ref, wo_ref,
     g2_ref, wr_ref) = (next(it) for _ in range(11))
    xmid_ref = next(it)
    pay_ref = next(it)
    oh_ref = next(it)
    if emit_kv:
        knew_ref = next(it)
        vnew_ref = next(it)
    q_s, k_s, v_s, xp_s, h_s, attn_s = (next(it) for _ in range(6))

    TM = S * L
    RB = ROW_BLOCK
    scale = HEAD_DIM ** -0.5

    sh1 = mod_ref[0, 0:1, :]
    gain1 = g1_ref[...] * (1.0 + mod_ref[0, 1:2, :])
    gt1 = mod_ref[0, 2:3, :]
    sh2 = mod_ref[0, 3:4, :]
    gain2 = g2_ref[...] * (1.0 + mod_ref[0, 4:5, :])
    qg = qg_ref[...]
    kg = kg_ref[...]

    def project(r, carry):
        r0 = pl.multiple_of(r * RB, RB)
        s = r0 // L
        o = pl.multiple_of(r0 % L, RB)
        hb = (_rms(x_ref[pl.ds(r0, RB), :]) * gain1 + sh1).astype(BF16)
        h_s[pl.ds(r0, RB), :] = hb
        p1 = jnp.dot(hb, win_ref[:, 0:GATE_COL], preferred_element_type=F32)
        if use_rope:
            cs = cos_ref[pl.ds(o, RB), :]
            sn = sneg_ref[pl.ds(o, RB), :]
            sp = spos_ref[pl.ds(o, RB), :]

        def rope(t):
            return (t * cs + pltpu.roll(t, HEAD_DIM - ROPE_NF, 1) * sn + pltpu.roll(t, ROPE_NF, 1) * sp)

        for hd in range(N_HEADS):
            qh = _rms(p1[:, hd * HEAD_DIM:(hd + 1) * HEAD_DIM]) * qg
            if use_rope:
                qh = rope(qh)
            q_s[hd, pl.ds(r0, RB), :] = qh.astype(BF16)
        for kh in range(N_KV_HEADS):
            c0 = ATTN_W + kh * HEAD_DIM
            kk = _rms(p1[:, c0:c0 + HEAD_DIM]) * kg
            if emit_kv:
                knew_ref[pl.ds(r0, RB), kh * HEAD_DIM:(kh + 1) * HEAD_DIM] = kk
            if use_rope:
                kk = rope(kk)
            k_s[s, pl.ds(P + o, RB), kh * HEAD_DIM:(kh + 1) * HEAD_DIM] = kk.astype(BF16)
        vv = p1[:, ATTN_W + KV_W:ATTN_W + 2 * KV_W]
        if emit_kv:
            vnew_ref[pl.ds(r0, RB), :] = vv
        v_s[s, pl.ds(P + o, RB), :] = vv.astype(BF16)
        xp_s[s, pl.ds(POOL_HALO + o, RB), :] = p1[:, ATTN_W + 2 * KV_W:GATE_COL]
        return carry

    @pl.when(pl.program_id(1) == 0)
    def _():
        if P:
            k_s[0, 0:P, :] = ck_ref[0].astype(BF16)
            v_s[0, 0:P, :] = cv_ref[0].astype(BF16)
        xp_s[:, 0:POOL_HALO, :] = jnp.zeros((S, POOL_HALO, POOL_W), F32)
        xp_s[:, L + POOL_HALO:L + 2 * POOL_HALO, :] = jnp.zeros((S, POOL_HALO, POOL_W), F32)
        lax.fori_loop(0, TM // RB, project, 0)

    def mix(r):
        r0 = pl.multiple_of(r * RB, RB)
        s = r0 // L
        o = pl.multiple_of(r0 % L, RB)

        for kh in range(N_KV_HEADS):
            k = k_s[s, :, kh * HEAD_DIM:(kh + 1) * HEAD_DIM]
            v = v_s[s, :, kh * HEAD_DIM:(kh + 1) * HEAD_DIM]
            q4 = q_s[kh * GROUP:(kh + 1) * GROUP, pl.ds(r0, RB), :].reshape(GROUP * RB, HEAD_DIM)
            sc = lax.dot_general(q4, k, (((1,), (1,)), ((), ())), preferred_element_type=F32) * scale
            e = jnp.exp(sc - jnp.max(sc, axis=-1, keepdims=True))
            den = jnp.sum(e, axis=-1, keepdims=True)
            o4 = jnp.dot(e.astype(BF16), v, preferred_element_type=F32) / den
            for g in range(GROUP):
                hd = kh * GROUP + g
                attn_s[:, hd * HEAD_DIM:(hd + 1) * HEAD_DIM] = o4[g * RB:(g + 1) * RB].astype(BF16)
        a = jnp.dot(attn_s[...], wa_ref[...], preferred_element_type=F32)

        t = o + lax.broadcasted_iota(I32, (RB, 1), 0)
        RW = RB + 2 * POOL_HALO
        parts = []
        for gi, w in enumerate(POOL_WINDOWS):
            cols = slice(gi * POOL_GC, (gi + 1) * POOL_GC)
            xw = xp_s[s, pl.ds(o, RW), cols]
            run = xw
            span = 1
            while span < w:
                run = run + pltpu.roll(run, span, 0)
                span *= 2
            if w // 2 > 1:
                run = pltpu.roll(run, RW - (w // 2 - 1), 0)
            tot = run[POOL_HALO:POOL_HALO + RB]
            cnt = (jnp.minimum(t + w // 2, L) - jnp.maximum(t - w // 2, 0)).astype(F32)
            parts.append(tot / cnt - xw[POOL_HALO:POOL_HALO + RB])
        dpool = jnp.concatenate(parts, axis=1).astype(BF16)
        pooled = jnp.dot(dpool, wpool_ref[...], preferred_element_type=F32) * pscale_ref[...]
        b = jnp.dot(pooled.astype(BF16), wb_ref[...], preferred_element_type=F32)

        gates = jnp.dot(h_s[pl.ds(r0, RB), :], win_ref[:, GATE_COL:IN_W], preferred_element_type=F32)
        merged = _sigmoid(gates[:, 0:D_MODEL]) * a + _sigmoid(gates[:, D_MODEL:2 * D_MODEL]) * b
        u = jnp.dot(merged.astype(BF16), wo_ref[...], preferred_element_type=F32)
        xm = x_ref[pl.ds(r0, RB), :] + gt1 * u
        xmid_ref[...] = xm

        h2 = _rms(xm) * gain2 + sh2
        hi = h2.astype(BF16)
        lo = (h2 - hi.astype(F32)).astype(BF16)
        l1 = jnp.dot(hi, wr_ref[...], preferred_element_type=F32)
        l2 = jnp.dot(lo, wr_ref[:, 0:LANES], preferred_element_type=F32)
        gate, bucket = _route(l1[:, 0:LANES] + l1[:, LANES:2 * LANES] + l2)
        pay_ref[:, 0:D_MODEL] = h2
        pay_ref[:, D_MODEL:PAYLOAD_W] = gate
        lane = lax.broadcasted_iota(I32, (RB, LANES), 1).astype(F32)
        oh_ref[...] = jnp.where(lane == bucket, 1.0, 0.0).astype(BF16)

    mix(pl.program_id(1))


def _mix(x2d, mod, mod_row, cache, rope_tabs, weights, *, S, L, emit_kv):
    T = x2d.shape[0]
    TM = S * L
    P = cache[0].shape[1] if cache is not None else 0
    use_rope = rope_tabs is not None
    assert T % TM == 0 and L % ROW_BLOCK == 0
    assert not (use_rope or P) or S == 1
    Lk = P + L

    args = [x2d, mod]
    nrb = TM // ROW_BLOCK
    in_specs = [
        pl.BlockSpec((TM, D_MODEL), lambda i, j: (i, 0)),
        pl.BlockSpec((1, 6, D_MODEL), lambda i, j: (mod_row(i), 0, 0)),
    ]
    if P:
        args += list(cache)
        in_specs += [pl.BlockSpec((1, P, KV_W), lambda i, j: (i, 0, 0))] * 2
    if use_rope:
        args += list(rope_tabs)
        in_specs += [_resident((L, HEAD_DIM))] * 3
    args += list(weights)
    in_specs += [_resident(w.shape) for w in weights]

    out_shape = [jax.ShapeDtypeStruct((T, D_MODEL), F32), jax.ShapeDtypeStruct((T, PAYLOAD_W), F32),
                 jax.ShapeDtypeStruct((T, LANES), BF16)]
    out_specs = [pl.BlockSpec((ROW_BLOCK, D_MODEL), lambda i, j: (i * nrb + j, 0)),
                 pl.BlockSpec((ROW_BLOCK, PAYLOAD_W), lambda i, j: (i * nrb + j, 0)),
                 pl.BlockSpec((ROW_BLOCK, LANES), lambda i, j: (i * nrb + j, 0))]
    if emit_kv:
        out_shape += [jax.ShapeDtypeStruct((T, KV_W), F32)] * 2
        out_specs += [pl.BlockSpec((TM, KV_W), lambda i, j: (i, 0))] * 2

    scratch = [
        pltpu.VMEM((N_HEADS, TM, HEAD_DIM), BF16),
        pltpu.VMEM((S, Lk, KV_W), BF16),
        pltpu.VMEM((S, Lk, KV_W), BF16),
        pltpu.VMEM((S, L + 2 * POOL_HALO, POOL_W), F32),
        pltpu.VMEM((TM, D_MODEL), BF16),
        pltpu.VMEM((ROW_BLOCK, ATTN_W), BF16),
    ]
    kern = functools.partial(_mix_kernel, S=S, L=L, P=P, use_rope=use_rope, emit_kv=emit_kv)
    return pl.pallas_call(
        kern,
        grid=(T // TM, nrb),
        in_specs=in_specs,
        out_specs=out_specs,
        out_shape=out_shape,
        scratch_shapes=scratch,
        compiler_params=pltpu.CompilerParams(
            dimension_semantics=("arbitrary", "arbitrary"), vmem_limit_bytes=V7X_VMEM_LIMIT_BYTES),
        name="mixer_rope" if use_rope else "mixer_ctx",
    )(*args)


def _plan_kernel(oh_ref, dest_ref, meta_ref, *, n_blocks, n_tiles):
    TB = TOKEN_BLOCK
    lane = lax.broadcasted_iota(I32, (SUBLANES, LANES), 1)

    def count(b, acc):
        oh = oh_ref[pl.ds(pl.multiple_of(b * TB, TB), TB), :].astype(F32)
        return acc + jnp.sum(oh, axis=0, keepdims=True)

    counts = lax.fori_loop(0, n_blocks, count, jnp.zeros((SUBLANES, LANES), F32))
    padded = jnp.ceil(counts * (1.0 / SORT_TILE)) * SORT_TILE
    ends = padded
    step = 1
    while step < LANES:
        ends = ends + jnp.where(lane >= step, pltpu.roll(ends, step, 1), 0.0)
        step *= 2
    starts = ends - padded

    tri = jnp.where(lax.broadcasted_iota(I32, (TB, TB), 1) < lax.broadcasted_iota(I32, (TB, TB), 0),
                    1.0, 0.0).astype(BF16)

    def place(b, seen):
        oh = oh_ref[pl.ds(pl.multiple_of(b * TB, TB), TB), :]
        ohf = oh.astype(F32)
        rank = jnp.dot(tri, oh, preferred_element_type=F32)
        base = (starts + seen)[0:1, :]
        d = jnp.sum(ohf * (rank + base), axis=1, keepdims=True)
        dest_ref[b] = _row(d).astype(I32)
        return seen + jnp.sum(ohf, axis=0, keepdims=True)

    lax.fori_loop(0, n_blocks, place, jnp.zeros((SUBLANES, LANES), F32))

    tile_row0 = lax.broadcasted_iota(I32, (LANES, LANES), 0).astype(F32) * SORT_TILE
    is_bucket = lax.broadcasted_iota(I32, (LANES, LANES), 1) < N_BUCKETS
    done = jnp.sum(jnp.where(is_bucket, jnp.where(ends[0:1, :] <= tile_row0, 1.0, 0.0), 0.0),
                   axis=1, keepdims=True)
    bkt = jnp.minimum(done, N_BUCKETS - 1.0)
    grp = (jnp.where(bkt >= PAIRS_PER_GROUP, 1.0, 0.0) + jnp.where(bkt >= 2 * PAIRS_PER_GROUP, 1.0, 0.0)
           + jnp.where(bkt >= 3 * PAIRS_PER_GROUP, 1.0, 0.0))
    pair = bkt - PAIRS_PER_GROUP * grp
    a = jnp.where(pair >= 3.0, 1.0, 0.0) + jnp.where(pair >= 5.0, 1.0, 0.0)
    b = pair - a * (7.0 - a) * 0.5 + a + 1.0
    e1 = EXP_PER_GROUP * grp + a
    e2 = EXP_PER_GROUP * grp + b
    tail = lane[0:1, :] == N_BUCKETS
    fill_lo = jnp.where(tail, ends, jnp.floor((starts + counts) * (1.0 / ZERO_CHUNK)) * ZERO_CHUNK)[0:1, :]
    fill_hi = jnp.where(tail, float(n_tiles * SORT_TILE), ends)[0:1, :]
    meta = jnp.concatenate(
        [_row(e1), _row(e2), ends[0:1, :] * (1.0 / SORT_TILE), fill_lo * (1.0 / ZERO_CHUNK),
         fill_hi * (1.0 / ZERO_CHUNK), jnp.zeros((SUBLANES - 5, LANES), F32)], axis=0)
    meta_ref[...] = meta.astype(I32)


def _plan(onehot, n_tiles):
    T = onehot.shape[0]
    n_blocks = T // TOKEN_BLOCK
    dest, meta = pl.pallas_call(
        functools.partial(_plan_kernel, n_blocks=n_blocks, n_tiles=n_tiles),
        out_shape=[jax.ShapeDtypeStruct((n_blocks, 1, TOKEN_BLOCK), I32),
                   jax.ShapeDtypeStruct((SUBLANES, LANES), I32)],
        name="moe_plan",
    )(onehot)
    return dest.reshape(T), meta


def _dispatch_kernel(dest, fill_lo, fill_hi, pay_ref, out_hbm, zeros_s, sem, zsem):
    i = pl.program_id(0)

    def zero_copy(c):
        r0 = pl.multiple_of(c * ZERO_CHUNK, ZERO_CHUNK)
        return pltpu.make_async_copy(zeros_s, out_hbm.at[pl.ds(r0, ZERO_CHUNK), :], zsem)

    @pl.when(i == 0)
    def _():
        zeros_s[...] = jnp.zeros(zeros_s.shape, F32)
        for b in range(N_BUCKETS + 1):
            lax.fori_loop(fill_lo[b], fill_hi[b], lambda c, x: (zero_copy(c).start(), x)[1], 0)
        for b in range(N_BUCKETS + 1):
            lax.fori_loop(fill_lo[b], fill_hi[b], lambda c, x: (zero_copy(c).wait(), x)[1], 0)

    base = i * TOKEN_BLOCK
    for k in range(TOKEN_BLOCK):
        row = pltpu.make_async_copy(pay_ref.at[pl.ds(k, 1), :], out_hbm.at[pl.ds(dest[base + k], 1), :], sem)
        row.start(priority=k % 2)
    pltpu.make_async_copy(pay_ref, out_hbm.at[pl.ds(0, TOKEN_BLOCK), :], sem).wait()


def _dispatch(payload, dest, meta, n_tiles):
    T = payload.shape[0]
    return pl.pallas_call(
        _dispatch_kernel,
        grid_spec=pltpu.PrefetchScalarGridSpec(
            num_scalar_prefetch=3,
            grid=(T // TOKEN_BLOCK,),
            in_specs=[pl.BlockSpec((TOKEN_BLOCK, PAYLOAD_W), lambda i, *_: (i, 0))],
            out_specs=pl.BlockSpec(memory_space=pl.ANY),
            scratch_shapes=[pltpu.VMEM((ZERO_CHUNK, PAYLOAD_W), F32), pltpu.SemaphoreType.DMA(()),
                            pltpu.SemaphoreType.DMA(())],
        ),
        out_shape=jax.ShapeDtypeStruct((n_tiles * SORT_TILE, PAYLOAD_W), F32),
        compiler_params=pltpu.CompilerParams(dimension_semantics=("arbitrary",)),
        name="moe_dispatch",
    )(dest, meta[3, :N_BUCKETS + 1], meta[4, :N_BUCKETS + 1], payload)


def _expert_kernel(e1s, e2s, n_used, x_ref, wg_ref, wu_ref, wd_ref, o_ref):
    i = pl.program_id(0)

    @pl.when(i < n_used[0])
    def _():
        e1 = e1s[i]
        e2 = e2s[i]
        x = x_ref[:, 0:D_MODEL].astype(BF16)
        gv = x_ref[:, D_MODEL:PAYLOAD_W]
        lane = lax.broadcasted_iota(I32, gv.shape, 1)
        out = None
        for e in (e1, e2):
            ge = jnp.sum(jnp.where(lane == EXPERT_LANE0 + e, gv, 0.0), axis=-1, keepdims=True)
            hg = jnp.dot(x, wg_ref[e], preferred_element_type=F32)
            hu = jnp.dot(x, wu_ref[e], preferred_element_type=F32)
            hid = (hg * _sigmoid(hg) * hu * ge).astype(BF16)
            y = jnp.dot(hid, wd_ref[e], preferred_element_type=F32)
            out = y if out is None else out + y
        for c in range(ROW_CHUNKS):
            o_ref[:, c, :] = out[:, c * LANES:(c + 1) * LANES]

    @pl.when(i >= n_used[0])
    def _():
        o_ref[...] = jnp.zeros(o_ref.shape, F32)


def _experts(sorted_rows, meta, wg, wu, wd):
    n_tiles = sorted_rows.shape[0] // SORT_TILE
    return pl.pallas_call(
        _expert_kernel,
        grid_spec=pltpu.PrefetchScalarGridSpec(
            num_scalar_prefetch=3,
            grid=(n_tiles,),
            in_specs=[
                pl.BlockSpec((SORT_TILE, PAYLOAD_W), lambda i, e1, e2, nu: (jnp.minimum(i, nu[0] - 1), 0)),
                _resident(wg.shape), _resident(wu.shape), _resident(wd.shape),
            ],
            out_specs=pl.BlockSpec((SORT_TILE, ROW_CHUNKS, LANES), lambda i, *_: (i, 0, 0)),
        ),
        out_shape=jax.ShapeDtypeStruct((n_tiles * SORT_TILE, ROW_CHUNKS, LANES), F32),
        compiler_params=pltpu.CompilerParams(
            dimension_semantics=("arbitrary",), vmem_limit_bytes=V7X_VMEM_LIMIT_BYTES),
        name="moe_experts",
    )(meta[0, :n_tiles], meta[1, :n_tiles], meta[2, LANES - 1:LANES], sorted_rows, wg, wu, wd)


def _sc_row_gather(table, idx):
    n = idx.shape[0]
    per_worker = n // SC_WORKERS
    assert n % SC_WORKERS == 0 and per_worker % SC_GATHER_CHUNK == 0
    mesh = plsc.VectorSubcoreMesh(core_axis_name="c", subcore_axis_name="s")

    @functools.partial(
        pl.kernel, mesh=mesh,
        out_type=jax.ShapeDtypeStruct((n,) + table.shape[1:], table.dtype),
        scratch_types=[pltpu.VMEM((per_worker,), I32),
                       pltpu.VMEM((SC_GATHER_CHUNK,) + table.shape[1:], table.dtype),
                       pltpu.SemaphoreType.DMA],
        compiler_params=pltpu.CompilerParams(use_tc_tiling_on_sc=True),
        name="sc_row_gather",
    )
    def gather(table_hbm, idx_hbm, out_hbm, idx_v, rows_v, sem):
        worker = lax.axis_index("s") * SC_CORES + lax.axis_index("c")
        base = worker * per_worker
        pltpu.sync_copy(idx_hbm.at[pl.ds(base, per_worker)], idx_v)

        @pl.loop(0, per_worker // SC_GATHER_CHUNK)
        def _(j):
            off = pl.multiple_of(j * SC_GATHER_CHUNK, SC_GATHER_CHUNK)
            pltpu.async_copy(table_hbm.at[idx_v.at[pl.ds(off, SC_GATHER_CHUNK)]], rows_v, sem).wait()
            pltpu.sync_copy(rows_v, out_hbm.at[pl.ds(base + off, SC_GATHER_CHUNK)])

    return gather(table, idx)


def _final_kernel(x_ref, moe_ref, mod_ref, gf_ref, o_ref):
    moe = jnp.concatenate([moe_ref[:, c, :] for c in range(ROW_CHUNKS)], axis=1)
    y = x_ref[...] + mod_ref[0, 5:6, :] * moe
    o_ref[...] = _rms(y) * gf_ref[...]


def _final(xmid, moe_rows, mod, mod_row, gf):
    T = xmid.shape[0]
    return pl.pallas_call(
        _final_kernel,
        grid=(T // TOKEN_BLOCK,),
        in_specs=[
            pl.BlockSpec((TOKEN_BLOCK, D_MODEL), lambda i: (i, 0)),
            pl.BlockSpec((TOKEN_BLOCK, ROW_CHUNKS, LANES), lambda i: (i, 0, 0)),
            pl.BlockSpec((1, 6, D_MODEL), lambda i: (mod_row(i), 0, 0)),
            pl.BlockSpec((1, D_MODEL), lambda i: (0, 0)),
        ],
        out_specs=pl.BlockSpec((TOKEN_BLOCK, D_MODEL), lambda i: (i, 0)),
        out_shape=jax.ShapeDtypeStruct((T, D_MODEL), F32),
        compiler_params=pltpu.CompilerParams(dimension_semantics=("arbitrary",)),
        name="moe_final",
    )(xmid, moe_rows, mod, gf)


def _moe(xmid, payload, onehot, mod, mod_row, gf, wg, wu, wd):
    T = xmid.shape[0]
    n_tiles = T // SORT_TILE + N_BUCKETS
    assert n_tiles <= LANES and T % TOKEN_BLOCK == 0
    dest, meta = _plan(onehot, n_tiles)
    sorted_rows = _dispatch(payload, dest, meta, n_tiles)
    moe_sorted = _experts(sorted_rows, meta, wg, wu, wd)
    return _final(xmid, _sc_row_gather(moe_sorted, dest), mod, mod_row, gf)


def _rope_tables(n_tokens):
    t = jnp.arange(n_tokens)
    row = (t // GRID_W).astype(F32)
    col = (t % GRID_W).astype(F32)
    freq = ROPE_THETA ** (-jnp.arange(ROPE_NF, dtype=F32) / ROPE_NF)
    ang = jnp.concatenate([row[:, None] * freq] * 2 + [col[:, None] * freq] * 2, axis=-1)
    first = (jnp.arange(HEAD_DIM) % (2 * ROPE_NF)) < ROPE_NF
    sin = jnp.sin(ang)
    return jnp.cos(ang), jnp.where(first, -sin, 0.0), jnp.where(first, 0.0, sin)


def kernel(x_prompt, x_sample, cache_k, cache_v, c, c_ctx, norm1_g, norm2_g, w_ada, b_ada, w_in, q_norm_g, k_norm_g, w_pool, pool_scale, w_branch_a, w_branch_b, w_out, w_router_group, w_router_expert, w_exp_gate, w_exp_up, w_exp_down, final_norm_g):
    assert norm1_g.shape[0] == 1, "single-layer trunk"
    B, L_ctx, _ = x_prompt.shape
    Bs, L_lat, _ = x_sample.shape
    P = cache_k.shape[2]
    assert 1 + Bs <= COND_ROWS

    cond = jnp.zeros((COND_ROWS, D_MODEL), F32).at[0].set(c_ctx).at[1:1 + Bs].set(c)
    mod = _ada(cond, w_ada[0], b_ada[0][None, :]).reshape(COND_ROWS, 6, D_MODEL)

    wpool_bd = jax.scipy.linalg.block_diag(*[w_pool[0, g] for g in range(len(POOL_WINDOWS))])
    wr = jnp.zeros((D_MODEL, LANES), F32)
    wr = wr.at[:, 0:N_EXP_GROUPS].set(w_router_group[0])
    wr = wr.at[:, EXPERT_LANE0:EXPERT_LANE0 + N_EXPERTS].set(w_router_expert[0])
    wr_hi = wr.astype(BF16)
    wr_lo = (wr - wr_hi.astype(F32)).astype(BF16)
    mix_w = (norm1_g[0][None, :], w_in[0].astype(BF16), q_norm_g[0][None, :], k_norm_g[0][None, :],
             wpool_bd.astype(BF16), pool_scale[0][None, :], w_branch_a[0].astype(BF16),
             w_branch_b[0].astype(BF16), w_out[0].astype(BF16),
             norm2_g[0][None, :], jnp.concatenate([wr_hi, wr_lo], axis=1))
    moe_w = (final_norm_g[None, :], w_exp_gate[0].astype(BF16), w_exp_up[0].astype(BF16),
             w_exp_down[0].astype(BF16))

    xp2 = x_prompt.reshape(B * L_ctx, D_MODEL)
    xmid_p, pay_p, oh_p, knew, vnew = _mix(xp2, mod, lambda i: 0, None, None, mix_w,
                                           S=2, L=L_ctx, emit_kv=True)
    y_prompt = _moe(xmid_p, pay_p, oh_p, mod, lambda i: 0, *moe_w)

    xs2 = x_sample.reshape(Bs * L_lat, D_MODEL)
    cache = (cache_k[:, 0].reshape(Bs, P, KV_W), cache_v[:, 0].reshape(Bs, P, KV_W))
    xmid_s, pay_s, oh_s = _mix(xs2, mod, lambda i: 1 + i, cache, _rope_tables(L_lat), mix_w,
                               S=1, L=L_lat, emit_kv=False)
    blocks_per_seq = L_lat // TOKEN_BLOCK
    y_sample = _moe(xmid_s, pay_s, oh_s, mod, lambda i: 1 + i // blocks_per_seq, *moe_w)

    return (y_prompt.reshape(B, L_ctx, D_MODEL), y_sample.reshape(Bs, L_lat, D_MODEL),
            knew.reshape(B, 1, L_ctx, N_KV_HEADS, HEAD_DIM), vnew.reshape(B, 1, L_ctx, N_KV_HEADS, HEAD_DIM))
```

```python
import functools

import jax
import jax.numpy as jnp
from jax import lax
from jax.experimental import pallas as pl
from jax.experimental.pallas import tpu as pltpu
from jax.experimental.pallas import tpu_sc as plsc

F32 = jnp.float32
BF16 = jnp.bfloat16
I32 = jnp.int32

D_MODEL = 1024
HEAD_DIM = 128
N_HEADS = 8
N_KV_HEADS = 2
GROUP = N_HEADS // N_KV_HEADS
ATTN_W = N_HEADS * HEAD_DIM
KV_W = N_KV_HEADS * HEAD_DIM
POOL_WINDOWS = (2, 4, 8, 16)
POOL_GC = 128
POOL_W = POOL_GC * len(POOL_WINDOWS)
IN_W = ATTN_W + 2 * KV_W + POOL_W + 2 * D_MODEL
GATE_COL = ATTN_W + 2 * KV_W + POOL_W
GRID_W = 64
ROPE_THETA = 10000.0
ROPE_NF = HEAD_DIM // 4
N_EXP_GROUPS = 4
EXP_PER_GROUP = 4
N_EXPERTS = 16
D_EXPERT = 256
EPS = 1e-6

LANES = 128
SUBLANES = 8
COND_ROWS = SUBLANES
POOL_HALO = 8
ROW_BLOCK = 256
ADA_COLS = 768
EXPERT_LANE0 = N_EXP_GROUPS
PAIRS_PER_GROUP = EXP_PER_GROUP * (EXP_PER_GROUP - 1) // 2
N_BUCKETS = N_EXP_GROUPS * PAIRS_PER_GROUP
SORT_TILE = 256
TOKEN_BLOCK = 512
ROW_CHUNKS = D_MODEL // LANES
SC_CORES = 2
SC_SUBCORES = 16
SC_WORKERS = SC_CORES * SC_SUBCORES
SC_LANES = 16
SC_GATHER_CHUNK = 64
V7X_VMEM_LIMIT_BYTES = 56 * 1024 * 1024


def _sigmoid(x):
    return 1.0 / (1.0 + jnp.exp(-x))


def _rms(x):
    return x * lax.rsqrt(jnp.mean(x * x, axis=-1, keepdims=True) + EPS)


def _resident(shape):
    zeros = (0,) * len(shape)
    return pl.BlockSpec(shape, lambda i, *_: zeros, pipeline_mode=pl.Buffered(1))


def _row(x):
    return jnp.transpose(jnp.broadcast_to(x, (x.shape[0], LANES)))[0:1, :]


def _ada_kernel(c_ref, w_ref, b_ref, o_ref):
    c = c_ref[...]
    s = (c * _sigmoid(c)).astype(BF16)
    o_ref[...] = jnp.dot(s, w_ref[...].astype(BF16), preferred_element_type=F32) + b_ref[...]


def _ada(cond, w_ada, b_ada):
    n = w_ada.shape[1]
    return pl.pallas_call(
        _ada_kernel,
        grid=(n // ADA_COLS,),
        in_specs=[
            pl.BlockSpec((COND_ROWS, D_MODEL), lambda j: (0, 0)),
            pl.BlockSpec((D_MODEL, ADA_COLS), lambda j: (0, j)),
            pl.BlockSpec((1, ADA_COLS), lambda j: (0, j)),
        ],
        out_specs=pl.BlockSpec((COND_ROWS, ADA_COLS), lambda j: (0, j)),
        out_shape=jax.ShapeDtypeStruct((COND_ROWS, n), F32),
        name="ada_mod",
    )(cond, w_ada, b_ada)


def _route(logits):
    lane = lax.broadcasted_iota(I32, logits.shape, 1).astype(F32)
    neg = jnp.float32(-1e30)
    far = jnp.float32(LANES)
    is_g = lane < N_EXP_GROUPS
    gl = jnp.where(is_g, logits, neg)
    gmax = jnp.max(gl, axis=-1, keepdims=True)
    gsel = jnp.min(jnp.where(gl == gmax, lane, far), axis=-1, keepdims=True)
    psel = 1.0 / jnp.sum(jnp.where(is_g, jnp.exp(gl - gmax), 0.0), axis=-1, keepdims=True)
    e_lo = EXPERT_LANE0 + EXP_PER_GROUP * gsel
    el = jnp.where(lane >= e_lo, jnp.where(lane < e_lo + EXP_PER_GROUP, logits, neg), neg)
    v1 = jnp.max(el, axis=-1, keepdims=True)
    i1 = jnp.min(jnp.where(el == v1, lane, far), axis=-1, keepdims=True)
    el2 = jnp.where(lane == i1, neg, el)
    v2 = jnp.max(el2, axis=-1, keepdims=True)
    i2 = jnp.min(jnp.where(el2 == v2, jnp.where(lane == i1, far, lane), far), axis=-1, keepdims=True)
    e2 = jnp.exp(v2 - v1)
    w1 = psel / (1.0 + e2)
    w2 = psel * e2 / (1.0 + e2)
    gate = jnp.where(lane == i1, w1, jnp.where(lane == i2, w2, 0.0))
    a = jnp.minimum(i1, i2) - e_lo
    b = jnp.maximum(i1, i2) - e_lo
    pair = a * (7.0 - a) * 0.5 + (b - a - 1.0)
    return gate, gsel * PAIRS_PER_GROUP + pair


def _mix_kernel(*refs, S, L, P, use_rope, emit_kv):
    it = iter(refs)
    x_ref = next(it)
    mod_ref = next(it)
    if P:
        ck_ref = next(it)
        cv_ref = next(it)
    if use_rope:
        cos_ref = next(it)
        sneg_ref = next(it)
        spos_ref = next(it)
    (g1_ref, win_ref, qg_ref, kg_ref, wpool_ref, pscale_ref, wa_ref, wb_ref, wo_ref,
     g2_ref, wr_ref) = (next(it) for _ in range(11))
    xmid_ref = next(it)
    h2_ref = next(it)
    gate_ref = next(it)
    oh_ref = next(it)
    if emit_kv:
        knew_ref = next(it)
        vnew_ref = next(it)
    q_s, k_s, v_s, xp_s, h_s, attn_s = (next(it) for _ in range(6))

    TM = S * L
    RB = ROW_BLOCK
    scale = HEAD_DIM ** -0.5

    sh1 = mod_ref[0, 0:1, :]
    gain1 = g1_ref[...] * (1.0 + mod_ref[0, 1:2, :])
    gt1 = mod_ref[0, 2:3, :]
    sh2 = mod_ref[0, 3:4, :]
    gain2 = g2_ref[...] * (1.0 + mod_ref[0, 4:5, :])
    qg = qg_ref[...]
    kg = kg_ref[...]

    def project(r, carry):
        r0 = pl.multiple_of(r * RB, RB)
        s = r0 // L
        o = pl.multiple_of(r0 % L, RB)
        hb = (_rms(x_ref[pl.ds(r0, RB), :]) * gain1 + sh1).astype(BF16)
        h_s[pl.ds(r0, RB), :] = hb
        p1 = jnp.dot(hb, win_ref[:, 0:GATE_COL], preferred_element_type=F32)
        if use_rope:
            cs = cos_ref[pl.ds(o, RB), :]
            sn = sneg_ref[pl.ds(o, RB), :]
            sp = spos_ref[pl.ds(o, RB), :]

        def rope(t):
            return (t * cs + pltpu.roll(t, HEAD_DIM - ROPE_NF, 1) * sn + pltpu.roll(t, ROPE_NF, 1) * sp)

        for hd in range(N_HEADS):
            qh = _rms(p1[:, hd * HEAD_DIM:(hd + 1) * HEAD_DIM]) * qg
            if use_rope:
                qh = rope(qh)
            q_s[hd, pl.ds(r0, RB), :] = qh.astype(BF16)
        for kh in range(N_KV_HEADS):
            c0 = ATTN_W + kh * HEAD_DIM
            kk = _rms(p1[:, c0:c0 + HEAD_DIM]) * kg
            if emit_kv:
                knew_ref[pl.ds(r0, RB), kh * HEAD_DIM:(kh + 1) * HEAD_DIM] = kk
            if use_rope:
                kk = rope(kk)
            k_s[s, pl.ds(P + o, RB), kh * HEAD_DIM:(kh + 1) * HEAD_DIM] = kk.astype(BF16)
        vv = p1[:, ATTN_W + KV_W:ATTN_W + 2 * KV_W]
        if emit_kv:
            vnew_ref[pl.ds(r0, RB), :] = vv
        v_s[s, pl.ds(P + o, RB), :] = vv.astype(BF16)
        xp_s[s, pl.ds(POOL_HALO + o, RB), :] = p1[:, ATTN_W + 2 * KV_W:GATE_COL]
        return carry

    @pl.when(pl.program_id(1) == 0)
    def _():
        if P:
            k_s[0, 0:P, :] = ck_ref[0].astype(BF16)
            v_s[0, 0:P, :] = cv_ref[0].astype(BF16)
        xp_s[:, 0:POOL_HALO, :] = jnp.zeros((S, POOL_HALO, POOL_W), F32)
        xp_s[:, L + POOL_HALO:L + 2 * POOL_HALO, :] = jnp.zeros((S, POOL_HALO, POOL_W), F32)
        lax.fori_loop(0, TM // RB, project, 0)

    def mix(r):
        r0 = pl.multiple_of(r * RB, RB)
        s = r0 // L
        o = pl.multiple_of(r0 % L, RB)

        for kh in range(N_KV_HEADS):
            k = k_s[s, :, kh * HEAD_DIM:(kh + 1) * HEAD_DIM]
            v = v_s[s, :, kh * HEAD_DIM:(kh + 1) * HEAD_DIM]
            q4 = q_s[kh * GROUP:(kh + 1) * GROUP, pl.ds(r0, RB), :].reshape(GROUP * RB, HEAD_DIM)
            sc = lax.dot_general(q4, k, (((1,), (1,)), ((), ())), preferred_element_type=F32) * scale
            e = jnp.exp(sc - jnp.max(sc, axis=-1, keepdims=True))
            den = jnp.sum(e, axis=-1, keepdims=True)
            o4 = jnp.dot(e.astype(BF16), v, preferred_element_type=F32) / den
            for g in range(GROUP):
                hd = kh * GROUP + g
                attn_s[:, hd * HEAD_DIM:(hd + 1) * HEAD_DIM] = o4[g * RB:(g + 1) * RB].astype(BF16)
        a = jnp.dot(attn_s[...], wa_ref[...], preferred_element_type=F32)

        t = o + lax.broadcasted_iota(I32, (RB, 1), 0)
        RW = RB + 2 * POOL_HALO
        parts = []
        for gi, w in enumerate(POOL_WINDOWS):
            cols = slice(gi * POOL_GC, (gi + 1) * POOL_GC)
            xw = xp_s[s, pl.ds(o, RW), cols]
            run = xw
            span = 1
            while span < w:
                run = run + pltpu.roll(run, span, 0)
                span *= 2
            if w // 2 > 1:
                run = pltpu.roll(run, RW - (w // 2 - 1), 0)
            tot = run[POOL_HALO:POOL_HALO + RB]
            cnt = (jnp.minimum(t + w // 2, L) - jnp.maximum(t - w // 2, 0)).astype(F32)
            parts.append(tot / cnt - xw[POOL_HALO:POOL_HALO + RB])
        dpool = jnp.concatenate(parts, axis=1).astype(BF16)
        pooled = jnp.dot(dpool, wpool_ref[...], preferred_element_type=F32) * pscale_ref[...]
        b = jnp.dot(pooled.astype(BF16), wb_ref[...], preferred_element_type=F32)

        gates = jnp.dot(h_s[pl.ds(r0, RB), :], win_ref[:, GATE_COL:IN_W], preferred_element_type=F32)
        merged = _sigmoid(gates[:, 0:D_MODEL]) * a + _sigmoid(gates[:, D_MODEL:2 * D_MODEL]) * b
        u = jnp.dot(merged.astype(BF16), wo_ref[...], preferred_element_type=F32)
        xm = x_ref[pl.ds(r0, RB), :] + gt1 * u
        xmid_ref[...] = xm

        h2 = _rms(xm) * gain2 + sh2
        hi = h2.astype(BF16)
        lo = (h2 - hi.astype(F32)).astype(BF16)
        l1 = jnp.dot(hi, wr_ref[...], preferred_element_type=F32)
        l2 = jnp.dot(lo, wr_ref[:, 0:LANES], preferred_element_type=F32)
        gate, bucket = _route(l1[:, 0:LANES] + l1[:, LANES:2 * LANES] + l2)
        for c in range(ROW_CHUNKS):
            h2_ref[:, c, :] = h2[:, c * LANES:(c + 1) * LANES]
        gate_ref[...] = gate
        lane = lax.broadcasted_iota(I32, (RB, LANES), 1).astype(F32)
        oh_ref[...] = jnp.where(lane == bucket, 1.0, 0.0).astype(BF16)

    mix(pl.program_id(1))


def _mix(x2d, mod, mod_row, cache, rope_tabs, weights, *, S, L, emit_kv):
    T = x2d.shape[0]
    TM = S * L
    P = cache[0].shape[1] if cache is not None else 0
    use_rope = rope_tabs is not None
    assert T % TM == 0 and L % ROW_BLOCK == 0
    assert not (use_rope or P) or S == 1
    Lk = P + L

    args = [x2d, mod]
    nrb = TM // ROW_BLOCK
    in_specs = [
        pl.BlockSpec((TM, D_MODEL), lambda i, j: (i, 0)),
        pl.BlockSpec((1, 6, D_MODEL), lambda i, j: (mod_row(i), 0, 0)),
    ]
    if P:
        args += list(cache)
        in_specs += [pl.BlockSpec((1, P, KV_W), lambda i, j: (i, 0, 0))] * 2
    if use_rope:
        args += list(rope_tabs)
        in_specs += [_resident((L, HEAD_DIM))] * 3
    args += list(weights)
    in_specs += [_resident(w.shape) for w in weights]

    out_shape = [jax.ShapeDtypeStruct((T, D_MODEL), F32), jax.ShapeDtypeStruct((T, ROW_CHUNKS, LANES), F32),
                 jax.ShapeDtypeStruct((T, LANES), F32),
                 jax.ShapeDtypeStruct((T, LANES), BF16)]
    out_specs = [pl.BlockSpec((ROW_BLOCK, D_MODEL), lambda i, j: (i * nrb + j, 0)),
                 pl.BlockSpec((ROW_BLOCK, ROW_CHUNKS, LANES), lambda i, j: (i * nrb + j, 0, 0)),
                 pl.BlockSpec((ROW_BLOCK, LANES), lambda i, j: (i * nrb + j, 0)),
                 pl.BlockSpec((ROW_BLOCK, LANES), lambda i, j: (i * nrb + j, 0))]
    if emit_kv:
        out_shape += [jax.ShapeDtypeStruct((T, KV_W), F32)] * 2
        out_specs += [pl.BlockSpec((TM, KV_W), lambda i, j: (i, 0))] * 2

    scratch = [
        pltpu.VMEM((N_HEADS, TM, HEAD_DIM), BF16),
        pltpu.VMEM((S, Lk, KV_W), BF16),
        pltpu.VMEM((S, Lk, KV_W), BF16),
        pltpu.VMEM((S, L + 2 * POOL_HALO, POOL_W), F32),
        pltpu.VMEM((TM, D_MODEL), BF16),
        pltpu.VMEM((ROW_BLOCK, ATTN_W), BF16),
    ]
    kern = functools.partial(_mix_kernel, S=S, L=L, P=P, use_rope=use_rope, emit_kv=emit_kv)
    return pl.pallas_call(
        kern,
        grid=(T // TM, nrb),
        in_specs=in_specs,
        out_specs=out_specs,
        out_shape=out_shape,
        scratch_shapes=scratch,
        compiler_params=pltpu.CompilerParams(
            dimension_semantics=("arbitrary", "arbitrary"), vmem_limit_bytes=V7X_VMEM_LIMIT_BYTES),
        name="mixer_rope" if use_rope else "mixer_ctx",
    )(*args)


def _plan_kernel(oh_ref, dest_ref, meta_ref, *, n_blocks):
    TB = TOKEN_BLOCK
    lane = lax.broadcasted_iota(I32, (SUBLANES, LANES), 1)

    def count(b, acc):
        oh = oh_ref[pl.ds(pl.multiple_of(b * TB, TB), TB), :].astype(F32)
        return acc + jnp.sum(oh, axis=0, keepdims=True)

    counts = lax.fori_loop(0, n_blocks, count, jnp.zeros((SUBLANES, LANES), F32))
    padded = jnp.ceil(counts * (1.0 / SORT_TILE)) * SORT_TILE
    ends = padded
    step = 1
    while step < LANES:
        ends = ends + jnp.where(lane >= step, pltpu.roll(ends, step, 1), 0.0)
        step *= 2
    starts = ends - padded

    tri = jnp.where(lax.broadcasted_iota(I32, (TB, TB), 1) < lax.broadcasted_iota(I32, (TB, TB), 0),
                    1.0, 0.0).astype(BF16)

    def place(b, seen):
        oh = oh_ref[pl.ds(pl.multiple_of(b * TB, TB), TB), :]
        ohf = oh.astype(F32)
        rank = jnp.dot(tri, oh, preferred_element_type=F32)
        base = (starts + seen)[0:1, :]
        d = jnp.sum(ohf * (rank + base), axis=1, keepdims=True)
        dest_ref[b] = _row(d).astype(I32)
        return seen + jnp.sum(ohf, axis=0, keepdims=True)

    lax.fori_loop(0, n_blocks, place, jnp.zeros((SUBLANES, LANES), F32))

    tile_row0 = lax.broadcasted_iota(I32, (LANES, LANES), 0).astype(F32) * SORT_TILE
    is_bucket = lax.broadcasted_iota(I32, (LANES, LANES), 1) < N_BUCKETS
    done = jnp.sum(jnp.where(is_bucket, jnp.where(ends[0:1, :] <= tile_row0, 1.0, 0.0), 0.0),
                   axis=1, keepdims=True)
    bkt = jnp.minimum(done, N_BUCKETS - 1.0)
    grp = (jnp.where(bkt >= PAIRS_PER_GROUP, 1.0, 0.0) + jnp.where(bkt >= 2 * PAIRS_PER_GROUP, 1.0, 0.0)
           + jnp.where(bkt >= 3 * PAIRS_PER_GROUP, 1.0, 0.0))
    pair = bkt - PAIRS_PER_GROUP * grp
    a = jnp.where(pair >= 3.0, 1.0, 0.0) + jnp.where(pair >= 5.0, 1.0, 0.0)
    b = pair - a * (7.0 - a) * 0.5 + a + 1.0
    e1 = EXP_PER_GROUP * grp + a
    e2 = EXP_PER_GROUP * grp + b
    meta = jnp.concatenate(
        [_row(e1), _row(e2), ends[0:1, :] * (1.0 / SORT_TILE), jnp.zeros((SUBLANES - 3, LANES), F32)], axis=0)
    meta_ref[...] = meta.astype(I32)


def _plan(onehot):
    T = onehot.shape[0]
    n_blocks = T // TOKEN_BLOCK
    dest, meta = pl.pallas_call(
        functools.partial(_plan_kernel, n_blocks=n_blocks),
        out_shape=[jax.ShapeDtypeStruct((n_blocks, 1, TOKEN_BLOCK), I32),
                   jax.ShapeDtypeStruct((SUBLANES, LANES), I32)],
        name="moe_plan",
    )(onehot)
    return dest.reshape(T), meta


def _sc_dispatch(h2_rows, gate_rows, dest, n_rows):
    T = dest.shape[0]
    per_worker = n_rows // SC_WORKERS
    assert n_rows % SC_WORKERS == 0 and per_worker % SC_GATHER_CHUNK == 0 and T % SC_LANES == 0
    mesh = plsc.VectorSubcoreMesh(core_axis_name="c", subcore_axis_name="s")

    @functools.partial(
        pl.kernel, mesh=mesh,
        out_type=[jax.ShapeDtypeStruct((n_rows,) + h2_rows.shape[1:], F32),
                  jax.ShapeDtypeStruct((n_rows, LANES), F32)],
        scratch_types=[pltpu.VMEM((T,), I32), pltpu.VMEM((per_worker,), I32),
                       pltpu.VMEM((SC_GATHER_CHUNK,) + h2_rows.shape[1:], F32),
                       pltpu.VMEM((SC_GATHER_CHUNK, LANES), F32),
                       pltpu.SemaphoreType.DMA, pltpu.SemaphoreType.DMA],
        compiler_params=pltpu.CompilerParams(use_tc_tiling_on_sc=True, needs_layout_passes=False),
        name="sc_dispatch",
    )
    def dispatch(h2_hbm, gate_hbm, dest_hbm, out_h_hbm, out_g_hbm, dest_v, src_v, rows_v, gates_v, sem_h, sem_g):
        worker = lax.axis_index("s") * SC_CORES + lax.axis_index("c")
        lo = worker * per_worker
        pltpu.sync_copy(dest_hbm, dest_v)

        @pl.loop(0, per_worker // SC_LANES)
        def _(j):
            src_v[pl.ds(pl.multiple_of(j * SC_LANES, SC_LANES), SC_LANES)] = jnp.zeros((SC_LANES,), I32)

        @pl.loop(0, T // SC_LANES)
        def _(j):
            t0 = pl.multiple_of(j * SC_LANES, SC_LANES)
            d = dest_v[pl.ds(t0, SC_LANES)] - lo
            mine = (d >= 0) & (d < per_worker)
            plsc.store_scatter(src_v, [jnp.where(mine, d, 0)], t0 + lax.iota(I32, SC_LANES), mask=mine)

        @pl.loop(0, per_worker // SC_GATHER_CHUNK)
        def _(j):
            off = pl.multiple_of(j * SC_GATHER_CHUNK, SC_GATHER_CHUNK)
            idx = src_v.at[pl.ds(off, SC_GATHER_CHUNK)]
            rows = pltpu.async_copy(h2_hbm.at[idx], rows_v, sem_h)
            gates = pltpu.async_copy(gate_hbm.at[idx], gates_v, sem_g)
            rows.wait()
            gates.wait()
            pltpu.sync_copy(rows_v, out_h_hbm.at[pl.ds(lo + off, SC_GATHER_CHUNK)])
            pltpu.sync_copy(gates_v, out_g_hbm.at[pl.ds(lo + off, SC_GATHER_CHUNK)])

    return dispatch(h2_rows, gate_rows, dest)


def _expert_kernel(e1s, e2s, n_used, x_ref, gv_ref, wg_ref, wu_ref, wd_ref, o_ref):
    i = pl.program_id(0)

    @pl.when(i < n_used[0])
    def _():
        e1 = e1s[i]
        e2 = e2s[i]
        x = jnp.concatenate([x_ref[:, c, :].astype(BF16) for c in range(ROW_CHUNKS)], axis=1)
        gv = gv_ref[...]
        lane = lax.broadcasted_iota(I32, gv.shape, 1)
        out = None
        for e in (e1, e2):
            ge = jnp.sum(jnp.where(lane == EXPERT_LANE0 + e, gv, 0.0), axis=-1, keepdims=True)
            hg = jnp.dot(x, wg_ref[e], preferred_element_type=F32)
            hu = jnp.dot(x, wu_ref[e], preferred_element_type=F32)
            hid = (hg * _sigmoid(hg) * hu * ge).astype(BF16)
            y = jnp.dot(hid, wd_ref[e], preferred_element_type=F32)
            out = y if out is None else out + y
        for c in range(ROW_CHUNKS):
            o_ref[:, c, :] = out[:, c * LANES:(c + 1) * LANES]

    @pl.when(i >= n_used[0])
    def _():
        o_ref[...] = jnp.zeros(o_ref.shape, F32)


def _experts(sorted_h2, sorted_gates, meta, wg, wu, wd):
    n_tiles = sorted_h2.shape[0] // SORT_TILE
    return pl.pallas_call(
        _expert_kernel,
        grid_spec=pltpu.PrefetchScalarGridSpec(
            num_scalar_prefetch=3,
            grid=(n_tiles,),
            in_specs=[
                pl.BlockSpec((SORT_TILE, ROW_CHUNKS, LANES),
                             lambda i, e1, e2, nu: (jnp.minimum(i, nu[0] - 1), 0, 0)),
                pl.BlockSpec((SORT_TILE, LANES), lambda i, e1, e2, nu: (jnp.minimum(i, nu[0] - 1), 0)),
                _resident(wg.shape), _resident(wu.shape), _resident(wd.shape),
            ],
            out_specs=pl.BlockSpec((SORT_TILE, ROW_CHUNKS, LANES), lambda i, *_: (i, 0, 0)),
        ),
        out_shape=jax.ShapeDtypeStruct((n_tiles * SORT_TILE, ROW_CHUNKS, LANES), F32),
        compiler_params=pltpu.CompilerParams(
            dimension_semantics=("arbitrary",), vmem_limit_bytes=V7X_VMEM_LIMIT_BYTES),
        name="moe_experts",
    )(meta[0, :n_tiles], meta[1, :n_tiles], meta[2, LANES - 1:LANES], sorted_h2, sorted_gates, wg, wu, wd)


def _sc_row_gather(table, idx):
    n = idx.shape[0]
    per_worker = n // SC_WORKERS
    assert n % SC_WORKERS == 0 and per_worker % SC_GATHER_CHUNK == 0
    mesh = plsc.VectorSubcoreMesh(core_axis_name="c", subcore_axis_name="s")

    @functools.partial(
        pl.kernel, mesh=mesh,
        out_type=jax.ShapeDtypeStruct((n,) + table.shape[1:], table.dtype),
        scratch_types=[pltpu.VMEM((per_worker,), I32),
                       pltpu.VMEM((SC_GATHER_CHUNK,) + table.shape[1:], table.dtype),
                       pltpu.SemaphoreType.DMA],
        compiler_params=pltpu.CompilerParams(use_tc_tiling_on_sc=True),
        name="sc_row_gather",
    )
    def gather(table_hbm, idx_hbm, out_hbm, idx_v, rows_v, sem):
        worker = lax.axis_index("s") * SC_CORES + lax.axis_index("c")
        base = worker * per_worker
        pltpu.sync_copy(idx_hbm.at[pl.ds(base, per_worker)], idx_v)

        @pl.loop(0, per_worker // SC_GATHER_CHUNK)
        def _(j):
            off = pl.multiple_of(j * SC_GATHER_CHUNK, SC_GATHER_CHUNK)
            pltpu.async_copy(table_hbm.at[idx_v.at[pl.ds(off, SC_GATHER_CHUNK)]], rows_v, sem).wait()
            pltpu.sync_copy(rows_v, out_hbm.at[pl.ds(base + off, SC_GATHER_CHUNK)])

    return gather(table, idx)


def _final_kernel(x_ref, moe_ref, mod_ref, gf_ref, o_ref):
    moe = jnp.concatenate([moe_ref[:, c, :] for c in range(ROW_CHUNKS)], axis=1)
    y = x_ref[...] + mod_ref[0, 5:6, :] * moe
    o_ref[...] = _rms(y) * gf_ref[...]


def _final(xmid, moe_rows, mod, mod_row, gf):
    T = xmid.shape[0]
    return pl.pallas_call(
        _final_kernel,
        grid=(T // TOKEN_BLOCK,),
        in_specs=[
            pl.BlockSpec((TOKEN_BLOCK, D_MODEL), lambda i: (i, 0)),
            pl.BlockSpec((TOKEN_BLOCK, ROW_CHUNKS, LANES), lambda i: (i, 0, 0)),
            pl.BlockSpec((1, 6, D_MODEL), lambda i: (mod_row(i), 0, 0)),
            pl.BlockSpec((1, D_MODEL), lambda i: (0, 0)),
        ],
        out_specs=pl.BlockSpec((TOKEN_BLOCK, D_MODEL), lambda i: (i, 0)),
        out_shape=jax.ShapeDtypeStruct((T, D_MODEL), F32),
        compiler_params=pltpu.CompilerParams(dimension_semantics=("arbitrary",)),
        name="moe_final",
    )(xmid, moe_rows, mod, gf)


def _moe_dispatch(h2_rows, gate_rows, onehot):
    T = h2_rows.shape[0]
    n_tiles = T // SORT_TILE + N_BUCKETS
    assert n_tiles <= LANES and T % TOKEN_BLOCK == 0
    dest, meta = _plan(onehot)
    return tuple(_sc_dispatch(h2_rows, gate_rows, dest, n_tiles * SORT_TILE)) + (dest, meta)


def _rope_tables(n_tokens):
    t = jnp.arange(n_tokens)
    row = (t // GRID_W).astype(F32)
    col = (t % GRID_W).astype(F32)
    freq = ROPE_THETA ** (-jnp.arange(ROPE_NF, dtype=F32) / ROPE_NF)
    ang = jnp.concatenate([row[:, None] * freq] * 2 + [col[:, None] * freq] * 2, axis=-1)
    first = (jnp.arange(HEAD_DIM) % (2 * ROPE_NF)) < ROPE_NF
    sin = jnp.sin(ang)
    return jnp.cos(ang), jnp.where(first, -sin, 0.0), jnp.where(first, 0.0, sin)


def kernel(x_prompt, x_sample, cache_k, cache_v, c, c_ctx, norm1_g, norm2_g, w_ada, b_ada, w_in, q_norm_g, k_norm_g, w_pool, pool_scale, w_branch_a, w_branch_b, w_out, w_router_group, w_router_expert, w_exp_gate, w_exp_up, w_exp_down, final_norm_g):
    assert norm1_g.shape[0] == 1, "single-layer trunk"
    B, L_ctx, _ = x_prompt.shape
    Bs, L_lat, _ = x_sample.shape
    P = cache_k.shape[2]
    assert 1 + Bs <= COND_ROWS

    cond = jnp.zeros((COND_ROWS, D_MODEL), F32).at[0].set(c_ctx).at[1:1 + Bs].set(c)
    mod = _ada(cond, w_ada[0], b_ada[0][None, :]).reshape(COND_ROWS, 6, D_MODEL)

    wpool_bd = jax.scipy.linalg.block_diag(*[w_pool[0, g] for g in range(len(POOL_WINDOWS))])
    wr = jnp.zeros((D_MODEL, LANES), F32)
    wr = wr.at[:, 0:N_EXP_GROUPS].set(w_router_group[0])
    wr = wr.at[:, EXPERT_LANE0:EXPERT_LANE0 + N_EXPERTS].set(w_router_expert[0])
    wr_hi = wr.astype(BF16)
    wr_lo = (wr - wr_hi.astype(F32)).astype(BF16)
    mix_w = (norm1_g[0][None, :], w_in[0].astype(BF16), q_norm_g[0][None, :], k_norm_g[0][None, :],
             wpool_bd.astype(BF16), pool_scale[0][None, :], w_branch_a[0].astype(BF16),
             w_branch_b[0].astype(BF16), w_out[0].astype(BF16),
             norm2_g[0][None, :], jnp.concatenate([wr_hi, wr_lo], axis=1))
    moe_w = (final_norm_g[None, :], w_exp_gate[0].astype(BF16), w_exp_up[0].astype(BF16),
             w_exp_down[0].astype(BF16))

    xp2 = x_prompt.reshape(B * L_ctx, D_MODEL)
    xmid_p, h2_p, gate_p, oh_p, knew, vnew = _mix(xp2, mod, lambda i: 0, None, None, mix_w,
                                                  S=2, L=L_ctx, emit_kv=True)
    sh_p, sg_p, dest_p, meta_p = _moe_dispatch(h2_p, gate_p, oh_p)

    xs2 = x_sample.reshape(Bs * L_lat, D_MODEL)
    cache = (cache_k[:, 0].reshape(Bs, P, KV_W), cache_v[:, 0].reshape(Bs, P, KV_W))
    xmid_s, h2_s, gate_s, oh_s = _mix(xs2, mod, lambda i: 1 + i, cache, _rope_tables(L_lat), mix_w,
                                      S=1, L=L_lat, emit_kv=False)
    sh_s, sg_s, dest_s, meta_s = _moe_dispatch(h2_s, gate_s, oh_s)

    gf, wg, wu, wd = moe_w
    moe_p = _sc_row_gather(_experts(sh_p, sg_p, meta_p, wg, wu, wd), dest_p)
    moe_s = _sc_row_gather(_experts(sh_s, sg_s, meta_s, wg, wu, wd), dest_s)
    y_prompt = _final(xmid_p, moe_p, mod, lambda i: 0, gf)
    blocks_per_seq = L_lat // TOKEN_BLOCK
    y_sample = _final(xmid_s, moe_s, mod, lambda i: 1 + i // blocks_per_seq, gf)

    return (y_prompt.reshape(B, L_ctx, D_MODEL), y_sample.reshape(Bs, L_lat, D_MODEL),
            knew.reshape(B, 1, L_ctx, N_KV_HEADS, HEAD_DIM), vnew.reshape(B, 1, L_ctx, N_KV_HEADS, HEAD_DIM))
```

```python
import functools

import jax
import jax.numpy as jnp
from jax import lax
from jax.experimental import pallas as pl
from jax.experimental.pallas import tpu as pltpu
from jax.experimental.pallas import tpu_sc as plsc

F32 = jnp.float32
BF16 = jnp.bfloat16
I32 = jnp.int32

D_MODEL = 1024
HEAD_DIM = 128
N_HEADS = 8
N_KV_HEADS = 2
GROUP = N_HEADS // N_KV_HEADS
ATTN_W = N_HEADS * HEAD_DIM
KV_W = N_KV_HEADS * HEAD_DIM
POOL_WINDOWS = (2, 4, 8, 16)
POOL_GC = 128
POOL_W = POOL_GC * len(POOL_WINDOWS)
IN_W = ATTN_W + 2 * KV_W + POOL_W + 2 * D_MODEL
GATE_COL = ATTN_W + 2 * KV_W + POOL_W
GRID_W = 64
ROPE_THETA = 10000.0
ROPE_NF = HEAD_DIM // 4
N_EXP_GROUPS = 4
EXP_PER_GROUP = 4
N_EXPERTS = 16
D_EXPERT = 256
EPS = 1e-6

LANES = 128
SUBLANES = 8
COND_ROWS = SUBLANES
POOL_HALO = 8
ROW_BLOCK = 256
ADA_COLS = 768
EXPERT_LANE0 = N_EXP_GROUPS
PAIRS_PER_GROUP = EXP_PER_GROUP * (EXP_PER_GROUP - 1) // 2
N_BUCKETS = N_EXP_GROUPS * PAIRS_PER_GROUP
SORT_TILE = 256
TOKEN_BLOCK = 512
ROW_CHUNKS = D_MODEL // LANES
SC_CORES = 2
SC_SUBCORES = 16
SC_WORKERS = SC_CORES * SC_SUBCORES
SC_LANES = 16
SC_GATHER_CHUNK = 64
V7X_VMEM_LIMIT_BYTES = 56 * 1024 * 1024


def _sigmoid(x):
    return 1.0 / (1.0 + jnp.exp(-x))


def _rms(x):
    return x * lax.rsqrt(jnp.mean(x * x, axis=-1, keepdims=True) + EPS)


def _resident(shape):
    zeros = (0,) * len(shape)
    return pl.BlockSpec(shape, lambda i, *_: zeros, pipeline_mode=pl.Buffered(1))


def _row(x):
    return jnp.transpose(jnp.broadcast_to(x, (x.shape[0], LANES)))[0:1, :]


def _ada_kernel(c_ref, w_ref, b_ref, o_ref):
    c = c_ref[...]
    s = (c * _sigmoid(c)).astype(BF16)
    o_ref[...] = jnp.dot(s, w_ref[...].astype(BF16), preferred_element_type=F32) + b_ref[...]


def _ada(cond, w_ada, b_ada):
    n = w_ada.shape[1]
    return pl.pallas_call(
        _ada_kernel,
        grid=(n // ADA_COLS,),
        in_specs=[
            pl.BlockSpec((COND_ROWS, D_MODEL), lambda j: (0, 0)),
            pl.BlockSpec((D_MODEL, ADA_COLS), lambda j: (0, j)),
            pl.BlockSpec((1, ADA_COLS), lambda j: (0, j)),
        ],
        out_specs=pl.BlockSpec((COND_ROWS, ADA_COLS), lambda j: (0, j)),
        out_shape=jax.ShapeDtypeStruct((COND_ROWS, n), F32),
        name="ada_mod",
    )(cond, w_ada, b_ada)


def _route(logits):
    lane = lax.broadcasted_iota(I32, logits.shape, 1).astype(F32)
    neg = jnp.float32(-1e30)
    far = jnp.float32(LANES)
    is_g = lane < N_EXP_GROUPS
    gl = jnp.where(is_g, logits, neg)
    gmax = jnp.max(gl, axis=-1, keepdims=True)
    gsel = jnp.min(jnp.where(gl == gmax, lane, far), axis=-1, keepdims=True)
    psel = 1.0 / jnp.sum(jnp.where(is_g, jnp.exp(gl - gmax), 0.0), axis=-1, keepdims=True)
    e_lo = EXPERT_LANE0 + EXP_PER_GROUP * gsel
    el = jnp.where(lane >= e_lo, jnp.where(lane < e_lo + EXP_PER_GROUP, logits, neg), neg)
    v1 = jnp.max(el, axis=-1, keepdims=True)
    i1 = jnp.min(jnp.where(el == v1, lane, far), axis=-1, keepdims=True)
    el2 = jnp.where(lane == i1, neg, el)
    v2 = jnp.max(el2, axis=-1, keepdims=True)
    i2 = jnp.min(jnp.where(el2 == v2, jnp.where(lane == i1, far, lane), far), axis=-1, keepdims=True)
    e2 = jnp.exp(v2 - v1)
    w1 = psel / (1.0 + e2)
    w2 = psel * e2 / (1.0 + e2)
    gate = jnp.where(lane == i1, w1, jnp.where(lane == i2, w2, 0.0))
    a = jnp.minimum(i1, i2) - e_lo
    b = jnp.maximum(i1, i2) - e_lo
    pair = a * (7.0 - a) * 0.5 + (b - a - 1.0)
    return gate, gsel * PAIRS_PER_GROUP + pair


def _mix_kernel(*refs, S, L, P, use_rope, emit_kv):
    it = iter(refs)
    x_ref = next(it)
    mod_ref = next(it)
    if P:
        ck_ref = next(it)
        cv_ref = next(it)
    if use_rope:
        cos_ref = next(it)
        sneg_ref = next(it)
        spos_ref = next(it)
    (g1_ref, win_ref, qg_ref, kg_ref, wpool_ref, pscale_ref, wa_ref, wb_ref, wo_ref,
     g2_ref, wr_ref) = (next(it) for _ in range(11))
    xmid_ref = next(it)
    h2_ref = next(it)
    gate_ref = next(it)
    oh_ref = next(it)
    if emit_kv:
        knew_ref = next(it)
        vnew_ref = next(it)
    q_s, k_s, v_s, xp_s, h_s, attn_s = (next(it) for _ in range(6))

    TM = S * L
    RB = ROW_BLOCK
    scale = HEAD_DIM ** -0.5

    sh1 = mod_ref[0, 0:1, :]
    gain1 = g1_ref[...] * (1.0 + mod_ref[0, 1:2, :])
    gt1 = mod_ref[0, 2:3, :]
    sh2 = mod_ref[0, 3:4, :]
    gain2 = g2_ref[...] * (1.0 + mod_ref[0, 4:5, :])
    qg = qg_ref[...]
    kg = kg_ref[...]

    def project(r, carry):
        r0 = pl.multiple_of(r * RB, RB)
        s = r0 // L
        o = pl.multiple_of(r0 % L, RB)
        hb = (_rms(x_ref[pl.ds(r0, RB), :]) * gain1 + sh1).astype(BF16)
        h_s[pl.ds(r0, RB), :] = hb
        p1 = jnp.dot(hb, win_ref[:, 0:GATE_COL], preferred_element_type=F32)
        if use_rope:
            cs = cos_ref[pl.ds(o, RB), :]
            sn = sneg_ref[pl.ds(o, RB), :]
            sp = spos_ref[pl.ds(o, RB), :]

        def rope(t):
            return (t * cs + pltpu.roll(t, HEAD_DIM - ROPE_NF, 1) * sn + pltpu.roll(t, ROPE_NF, 1) * sp)

        for hd in range(N_HEADS):
            qh = _rms(p1[:, hd * HEAD_DIM:(hd + 1) * HEAD_DIM]) * qg
            if use_rope:
                qh = rope(qh)
            q_s[hd, pl.ds(r0, RB), :] = qh.astype(BF16)
        for kh in range(N_KV_HEADS):
            c0 = ATTN_W + kh * HEAD_DIM
            kk = _rms(p1[:, c0:c0 + HEAD_DIM]) * kg
            if emit_kv:
                knew_ref[pl.ds(r0, RB), kh * HEAD_DIM:(kh + 1) * HEAD_DIM] = kk
            if use_rope:
                kk = rope(kk)
            k_s[s, pl.ds(P + o, RB), kh * HEAD_DIM:(kh + 1) * HEAD_DIM] = kk.astype(BF16)
        vv = p1[:, ATTN_W + KV_W:ATTN_W + 2 * KV_W]
        if emit_kv:
            vnew_ref[pl.ds(r0, RB), :] = vv
        v_s[s, pl.ds(P + o, RB), :] = vv.astype(BF16)
        xp_s[s, pl.ds(POOL_HALO + o, RB), :] = p1[:, ATTN_W + 2 * KV_W:GATE_COL]
        return carry

    @pl.when(pl.program_id(1) == 0)
    def _():
        if P:
            k_s[0, 0:P, :] = ck_ref[0].astype(BF16)
            v_s[0, 0:P, :] = cv_ref[0].astype(BF16)
        xp_s[:, 0:POOL_HALO, :] = jnp.zeros((S, POOL_HALO, POOL_W), F32)
        xp_s[:, L + POOL_HALO:L + 2 * POOL_HALO, :] = jnp.zeros((S, POOL_HALO, POOL_W), F32)
        lax.fori_loop(0, TM // RB, project, 0)

    def mix(r):
        r0 = pl.multiple_of(r * RB, RB)
        s = r0 // L
        o = pl.multiple_of(r0 % L, RB)

        for kh in range(N_KV_HEADS):
            k = k_s[s, :, kh * HEAD_DIM:(kh + 1) * HEAD_DIM]
            v = v_s[s, :, kh * HEAD_DIM:(kh + 1) * HEAD_DIM]
            q4 = q_s[kh * GROUP:(kh + 1) * GROUP, pl.ds(r0, RB), :].reshape(GROUP * RB, HEAD_DIM)
            sc = lax.dot_general(q4, k, (((1,), (1,)), ((), ())), preferred_element_type=F32) * scale
            e = jnp.exp(sc - jnp.max(sc, axis=-1, keepdims=True))
            den = jnp.sum(e, axis=-1, keepdims=True)
            o4 = jnp.dot(e.astype(BF16), v, preferred_element_type=F32) / den
            for g in range(GROUP):
                hd = kh * GROUP + g
                attn_s[:, hd * HEAD_DIM:(hd + 1) * HEAD_DIM] = o4[g * RB:(g + 1) * RB].astype(BF16)
        a = jnp.dot(attn_s[...], wa_ref[...], preferred_element_type=F32)

        t = o + lax.broadcasted_iota(I32, (RB, 1), 0)
        RW = RB + 2 * POOL_HALO
        parts = []
        for gi, w in enumerate(POOL_WINDOWS):
            cols = slice(gi * POOL_GC, (gi + 1) * POOL_GC)
            xw = xp_s[s, pl.ds(o, RW), cols]
            run = xw
            span = 1
            while span < w:
                run = run + pltpu.roll(run, span, 0)
                span *= 2
            if w // 2 > 1:
                run = pltpu.roll(run, RW - (w // 2 - 1), 0)
            tot = run[POOL_HALO:POOL_HALO + RB]
            cnt = (jnp.minimum(t + w // 2, L) - jnp.maximum(t - w // 2, 0)).astype(F32)
            parts.append(tot / cnt - xw[POOL_HALO:POOL_HALO + RB])
        dpool = jnp.concatenate(parts, axis=1).astype(BF16)
        pooled = jnp.dot(dpool, wpool_ref[...], preferred_element_type=F32) * pscale_ref[...]
        b = jnp.dot(pooled.astype(BF16), wb_ref[...], preferred_element_type=F32)

        gates = jnp.dot(h_s[pl.ds(r0, RB), :], win_ref[:, GATE_COL:IN_W], preferred_element_type=F32)
        merged = _sigmoid(gates[:, 0:D_MODEL]) * a + _sigmoid(gates[:, D_MODEL:2 * D_MODEL]) * b
        u = jnp.dot(merged.astype(BF16), wo_ref[...], preferred_element_type=F32)
        xm = x_ref[pl.ds(r0, RB), :] + gt1 * u
        xmid_ref[...] = xm

        h2 = _rms(xm) * gain2 + sh2
        hi = h2.astype(BF16)
        lo = (h2 - hi.astype(F32)).astype(BF16)
        l1 = jnp.dot(hi, wr_ref[...], preferred_element_type=F32)
        l2 = jnp.dot(lo, wr_ref[:, 0:LANES], preferred_element_type=F32)
        gate, bucket = _route(l1[:, 0:LANES] + l1[:, LANES:2 * LANES] + l2)
        for c in range(ROW_CHUNKS):
            h2_ref[:, c, :] = h2[:, c * LANES:(c + 1) * LANES]
        gate_ref[...] = gate
        lane = lax.broadcasted_iota(I32, (RB, LANES), 1).astype(F32)
        oh_ref[...] = jnp.where(lane == bucket, 1.0, 0.0).astype(BF16)

    mix(pl.program_id(1))


def _mix(x2d, mod, mod_row, cache, rope_tabs, weights, *, S, L, emit_kv):
    T = x2d.shape[0]
    TM = S * L
    P = cache[0].shape[1] if cache is not None else 0
    use_rope = rope_tabs is not None
    assert T % TM == 0 and L % ROW_BLOCK == 0
    assert not (use_rope or P) or S == 1
    Lk = P + L

    args = [x2d, mod]
    nrb = TM // ROW_BLOCK
    in_specs = [
        pl.BlockSpec((TM, D_MODEL), lambda i, j: (i, 0)),
        pl.BlockSpec((1, 6, D_MODEL), lambda i, j: (mod_row(i), 0, 0)),
    ]
    if P:
        args += list(cache)
        in_specs += [pl.BlockSpec((1, P, KV_W), lambda i, j: (i, 0, 0))] * 2
    if use_rope:
        args += list(rope_tabs)
        in_specs += [_resident((L, HEAD_DIM))] * 3
    args += list(weights)
    in_specs += [_resident(w.shape) for w in weights]

    out_shape = [jax.ShapeDtypeStruct((T, D_MODEL), F32), jax.ShapeDtypeStruct((T, ROW_CHUNKS, LANES), F32),
                 jax.ShapeDtypeStruct((T, LANES), F32),
                 jax.ShapeDtypeStruct((T, LANES), BF16)]
    out_specs = [pl.BlockSpec((ROW_BLOCK, D_MODEL), lambda i, j: (i * nrb + j, 0)),
                 pl.BlockSpec((ROW_BLOCK, ROW_CHUNKS, LANES), lambda i, j: (i * nrb + j, 0, 0)),
                 pl.BlockSpec((ROW_BLOCK, LANES), lambda i, j: (i * nrb + j, 0)),
                 pl.BlockSpec((ROW_BLOCK, LANES), lambda i, j: (i * nrb + j, 0))]
    if emit_kv:
        out_shape += [jax.ShapeDtypeStruct((T, KV_W), F32)] * 2
        out_specs += [pl.BlockSpec((TM, KV_W), lambda i, j: (i, 0))] * 2

    scratch = [
        pltpu.VMEM((N_HEADS, TM, HEAD_DIM), BF16),
        pltpu.VMEM((S, Lk, KV_W), BF16),
        pltpu.VMEM((S, Lk, KV_W), BF16),
        pltpu.VMEM((S, L + 2 * POOL_HALO, POOL_W), F32),
        pltpu.VMEM((TM, D_MODEL), BF16),
        pltpu.VMEM((ROW_BLOCK, ATTN_W), BF16),
    ]
    kern = functools.partial(_mix_kernel, S=S, L=L, P=P, use_rope=use_rope, emit_kv=emit_kv)
    return pl.pallas_call(
        kern,
        grid=(T // TM, nrb),
        in_specs=in_specs,
        out_specs=out_specs,
        out_shape=out_shape,
        scratch_shapes=scratch,
        compiler_params=pltpu.CompilerParams(
            dimension_semantics=("arbitrary", "arbitrary"), vmem_limit_bytes=V7X_VMEM_LIMIT_BYTES),
        name="mixer_rope" if use_rope else "mixer_ctx",
    )(*args)


def _plan_kernel(oh_ref, dest_ref, meta_ref, *, n_blocks):
    TB = TOKEN_BLOCK
    lane = lax.broadcasted_iota(I32, (SUBLANES, LANES), 1)

    def count(b, acc):
        oh = oh_ref[pl.ds(pl.multiple_of(b * TB, TB), TB), :].astype(F32)
        return acc + jnp.sum(oh, axis=0, keepdims=True)

    counts = lax.fori_loop(0, n_blocks, count, jnp.zeros((SUBLANES, LANES), F32))
    padded = jnp.ceil(counts * (1.0 / SORT_TILE)) * SORT_TILE
    ends = padded
    step = 1
    while step < LANES:
        ends = ends + jnp.where(lane >= step, pltpu.roll(ends, step, 1), 0.0)
        step *= 2
    starts = ends - padded

    tri = jnp.where(lax.broadcasted_iota(I32, (TB, TB), 1) < lax.broadcasted_iota(I32, (TB, TB), 0),
                    1.0, 0.0).astype(BF16)

    def place(b, seen):
        oh = oh_ref[pl.ds(pl.multiple_of(b * TB, TB), TB), :]
        ohf = oh.astype(F32)
        rank = jnp.dot(tri, oh, preferred_element_type=F32)
        base = (starts + seen)[0:1, :]
        d = jnp.sum(ohf * (rank + base), axis=1, keepdims=True)
        dest_ref[b] = _row(d).astype(I32)
        return seen + jnp.sum(ohf, axis=0, keepdims=True)

    lax.fori_loop(0, n_blocks, place, jnp.zeros((SUBLANES, LANES), F32))

    tile_row0 = lax.broadcasted_iota(I32, (LANES, LANES), 0).astype(F32) * SORT_TILE
    is_bucket = lax.broadcasted_iota(I32, (LANES, LANES), 1) < N_BUCKETS
    done = jnp.sum(jnp.where(is_bucket, jnp.where(ends[0:1, :] <= tile_row0, 1.0, 0.0), 0.0),
                   axis=1, keepdims=True)
    bkt = jnp.minimum(done, N_BUCKETS - 1.0)
    grp = (jnp.where(bkt >= PAIRS_PER_GROUP, 1.0, 0.0) + jnp.where(bkt >= 2 * PAIRS_PER_GROUP, 1.0, 0.0)
           + jnp.where(bkt >= 3 * PAIRS_PER_GROUP, 1.0, 0.0))
    pair = bkt - PAIRS_PER_GROUP * grp
    a = jnp.where(pair >= 3.0, 1.0, 0.0) + jnp.where(pair >= 5.0, 1.0, 0.0)
    b = pair - a * (7.0 - a) * 0.5 + a + 1.0
    e1 = EXP_PER_GROUP * grp + a
    e2 = EXP_PER_GROUP * grp + b
    meta = jnp.concatenate(
        [_row(e1), _row(e2), ends[0:1, :] * (1.0 / SORT_TILE), jnp.zeros((SUBLANES - 3, LANES), F32)], axis=0)
    meta_ref[...] = meta.astype(I32)


def _plan(onehot):
    T = onehot.shape[0]
    n_blocks = T // TOKEN_BLOCK
    dest, meta = pl.pallas_call(
        functools.partial(_plan_kernel, n_blocks=n_blocks),
        out_shape=[jax.ShapeDtypeStruct((n_blocks, 1, TOKEN_BLOCK), I32),
                   jax.ShapeDtypeStruct((SUBLANES, LANES), I32)],
        name="moe_plan",
    )(onehot)
    return dest.reshape(T), meta


def _sc_dispatch(h2_rows, gate_rows, dest, n_rows):
    T = dest.shape[0]
    per_worker = n_rows // SC_WORKERS
    assert n_rows % SC_WORKERS == 0 and per_worker % SC_GATHER_CHUNK == 0 and T % SC_LANES == 0
    mesh = plsc.VectorSubcoreMesh(core_axis_name="c", subcore_axis_name="s")

    @functools.partial(
        pl.kernel, mesh=mesh,
        out_type=[jax.ShapeDtypeStruct((n_rows,) + h2_rows.shape[1:], F32),
                  jax.ShapeDtypeStruct((n_rows, LANES), F32)],
        scratch_types=[pltpu.VMEM((T,), I32), pltpu.VMEM((per_worker,), I32),
                       pltpu.VMEM((SC_GATHER_CHUNK,) + h2_rows.shape[1:], F32),
                       pltpu.VMEM((SC_GATHER_CHUNK, LANES), F32),
                       pltpu.SemaphoreType.DMA, pltpu.SemaphoreType.DMA],
        compiler_params=pltpu.CompilerParams(use_tc_tiling_on_sc=True, needs_layout_passes=False),
        name="sc_dispatch",
    )
    def dispatch(h2_hbm, gate_hbm, dest_hbm, out_h_hbm, out_g_hbm, dest_v, src_v, rows_v, gates_v, sem_h, sem_g):
        worker = lax.axis_index("s") * SC_CORES + lax.axis_index("c")
        lo = worker * per_worker
        pltpu.sync_copy(dest_hbm, dest_v)

        @pl.loop(0, per_worker // SC_LANES)
        def _(j):
            j0 = pl.multiple_of(j * SC_LANES, SC_LANES)
            src_v[pl.ds(j0, SC_LANES)] = lax.rem(lo + j0 + lax.iota(I32, SC_LANES), T)

        @pl.loop(0, T // SC_LANES)
        def _(j):
            t0 = pl.multiple_of(j * SC_LANES, SC_LANES)
            d = dest_v[pl.ds(t0, SC_LANES)] - lo
            mine = (d >= 0) & (d < per_worker)
            plsc.store_scatter(src_v, [jnp.where(mine, d, 0)], t0 + lax.iota(I32, SC_LANES), mask=mine)

        @pl.loop(0, per_worker // SC_GATHER_CHUNK)
        def _(j):
            off = pl.multiple_of(j * SC_GATHER_CHUNK, SC_GATHER_CHUNK)
            idx = src_v.at[pl.ds(off, SC_GATHER_CHUNK)]
            rows = pltpu.async_copy(h2_hbm.at[idx], rows_v, sem_h)
            gates = pltpu.async_copy(gate_hbm.at[idx], gates_v, sem_g)
            rows.wait()
            gates.wait()
            pltpu.sync_copy(rows_v, out_h_hbm.at[pl.ds(lo + off, SC_GATHER_CHUNK)])
            pltpu.sync_copy(gates_v, out_g_hbm.at[pl.ds(lo + off, SC_GATHER_CHUNK)])

    return dispatch(h2_rows, gate_rows, dest)


def _expert_kernel(e1s, e2s, n_used, x_ref, gv_ref, wg_ref, wu_ref, wd_ref, o_ref):
    i = pl.program_id(0)

    @pl.when(i < n_used[0])
    def _():
        e1 = e1s[i]
        e2 = e2s[i]
        x = jnp.concatenate([x_ref[:, c, :].astype(BF16) for c in range(ROW_CHUNKS)], axis=1)
        gv = gv_ref[...]
        lane = lax.broadcasted_iota(I32, gv.shape, 1)
        out = None
        for e in (e1, e2):
            ge = jnp.sum(jnp.where(lane == EXPERT_LANE0 + e, gv, 0.0), axis=-1, keepdims=True)
            hg = jnp.dot(x, wg_ref[e], preferred_element_type=F32)
            hu = jnp.dot(x, wu_ref[e], preferred_element_type=F32)
            hid = (hg * _sigmoid(hg) * hu * ge).astype(BF16)
            y = jnp.dot(hid, wd_ref[e], preferred_element_type=F32)
            out = y if out is None else out + y
        for c in range(ROW_CHUNKS):
            o_ref[:, c, :] = out[:, c * LANES:(c + 1) * LANES]

    @pl.when(i >= n_used[0])
    def _():
        o_ref[...] = jnp.zeros(o_ref.shape, F32)


def _experts(sorted_h2, sorted_gates, meta, wg, wu, wd):
    n_tiles = sorted_h2.shape[0] // SORT_TILE
    return pl.pallas_call(
        _expert_kernel,
        grid_spec=pltpu.PrefetchScalarGridSpec(
            num_scalar_prefetch=3,
            grid=(n_tiles,),
            in_specs=[
                pl.BlockSpec((SORT_TILE, ROW_CHUNKS, LANES),
                             lambda i, e1, e2, nu: (jnp.minimum(i, nu[0] - 1), 0, 0)),
                pl.BlockSpec((SORT_TILE, LANES), lambda i, e1, e2, nu: (jnp.minimum(i, nu[0] - 1), 0)),
                _resident(wg.shape), _resident(wu.shape), _resident(wd.shape),
            ],
            out_specs=pl.BlockSpec((SORT_TILE, ROW_CHUNKS, LANES), lambda i, *_: (i, 0, 0)),
        ),
        out_shape=jax.ShapeDtypeStruct((n_tiles * SORT_TILE, ROW_CHUNKS, LANES), F32),
        compiler_params=pltpu.CompilerParams(
            dimension_semantics=("arbitrary",), vmem_limit_bytes=V7X_VMEM_LIMIT_BYTES),
        name="moe_experts",
    )(meta[0, :n_tiles], meta[1, :n_tiles], meta[2, LANES - 1:LANES], sorted_h2, sorted_gates, wg, wu, wd)


def _sc_row_gather(table, idx):
    n = idx.shape[0]
    per_worker = n // SC_WORKERS
    assert n % SC_WORKERS == 0 and per_worker % SC_GATHER_CHUNK == 0
    mesh = plsc.VectorSubcoreMesh(core_axis_name="c", subcore_axis_name="s")

    @functools.partial(
        pl.kernel, mesh=mesh,
        out_type=jax.ShapeDtypeStruct((n,) + table.shape[1:], table.dtype),
        scratch_types=[pltpu.VMEM((per_worker,), I32),
                       pltpu.VMEM((SC_GATHER_CHUNK,) + table.shape[1:], table.dtype),
                       pltpu.SemaphoreType.DMA],
        compiler_params=pltpu.CompilerParams(use_tc_tiling_on_sc=True),
        name="sc_row_gather",
    )
    def gather(table_hbm, idx_hbm, out_hbm, idx_v, rows_v, sem):
        worker = lax.axis_index("s") * SC_CORES + lax.axis_index("c")
        base = worker * per_worker
        pltpu.sync_copy(idx_hbm.at[pl.ds(base, per_worker)], idx_v)

        @pl.loop(0, per_worker // SC_GATHER_CHUNK)
        def _(j):
            off = pl.multiple_of(j * SC_GATHER_CHUNK, SC_GATHER_CHUNK)
            pltpu.async_copy(table_hbm.at[idx_v.at[pl.ds(off, SC_GATHER_CHUNK)]], rows_v, sem).wait()
            pltpu.sync_copy(rows_v, out_hbm.at[pl.ds(base + off, SC_GATHER_CHUNK)])

    return gather(table, idx)


def _final_kernel(x_ref, moe_ref, mod_ref, gf_ref, o_ref):
    moe = jnp.concatenate([moe_ref[:, c, :] for c in range(ROW_CHUNKS)], axis=1)
    y = x_ref[...] + mod_ref[0, 5:6, :] * moe
    o_ref[...] = _rms(y) * gf_ref[...]


def _final(xmid, moe_rows, mod, mod_row, gf):
    T = xmid.shape[0]
    return pl.pallas_call(
        _final_kernel,
        grid=(T // TOKEN_BLOCK,),
        in_specs=[
            pl.BlockSpec((TOKEN_BLOCK, D_MODEL), lambda i: (i, 0)),
            pl.BlockSpec((TOKEN_BLOCK, ROW_CHUNKS, LANES), lambda i: (i, 0, 0)),
            pl.BlockSpec((1, 6, D_MODEL), lambda i: (mod_row(i), 0, 0)),
            pl.BlockSpec((1, D_MODEL), lambda i: (0, 0)),
        ],
        out_specs=pl.BlockSpec((TOKEN_BLOCK, D_MODEL), lambda i: (i, 0)),
        out_shape=jax.ShapeDtypeStruct((T, D_MODEL), F32),
        compiler_params=pltpu.CompilerParams(dimension_semantics=("arbitrary",)),
        name="moe_final",
    )(xmid, moe_rows, mod, gf)


def _moe_dispatch(h2_rows, gate_rows, onehot):
    T = h2_rows.shape[0]
    n_tiles = T // SORT_TILE + N_BUCKETS
    assert n_tiles <= LANES and T % TOKEN_BLOCK == 0
    dest, meta = _plan(onehot)
    return tuple(_sc_dispatch(h2_rows, gate_rows, dest, n_tiles * SORT_TILE)) + (dest, meta)


def _rope_tables(n_tokens):
    t = jnp.arange(n_tokens)
    row = (t // GRID_W).astype(F32)
    col = (t % GRID_W).astype(F32)
    freq = ROPE_THETA ** (-jnp.arange(ROPE_NF, dtype=F32) / ROPE_NF)
    ang = jnp.concatenate([row[:, None] * freq] * 2 + [col[:, None] * freq] * 2, axis=-1)
    first = (jnp.arange(HEAD_DIM) % (2 * ROPE_NF)) < ROPE_NF
    sin = jnp.sin(ang)
    return jnp.cos(ang), jnp.where(first, -sin, 0.0), jnp.where(first, 0.0, sin)


def kernel(x_prompt, x_sample, cache_k, cache_v, c, c_ctx, norm1_g, norm2_g, w_ada, b_ada, w_in, q_norm_g, k_norm_g, w_pool, pool_scale, w_branch_a, w_branch_b, w_out, w_router_group, w_router_expert, w_exp_gate, w_exp_up, w_exp_down, final_norm_g):
    assert norm1_g.shape[0] == 1, "single-layer trunk"
    B, L_ctx, _ = x_prompt.shape
    Bs, L_lat, _ = x_sample.shape
    P = cache_k.shape[2]
    assert 1 + Bs <= COND_ROWS

    cond = jnp.zeros((COND_ROWS, D_MODEL), F32).at[0].set(c_ctx).at[1:1 + Bs].set(c)
    mod = _ada(cond, w_ada[0], b_ada[0][None, :]).reshape(COND_ROWS, 6, D_MODEL)

    wpool_bd = jax.scipy.linalg.block_diag(*[w_pool[0, g] for g in range(len(POOL_WINDOWS))])
    wr = jnp.zeros((D_MODEL, LANES), F32)
    wr = wr.at[:, 0:N_EXP_GROUPS].set(w_router_group[0])
    wr = wr.at[:, EXPERT_LANE0:EXPERT_LANE0 + N_EXPERTS].set(w_router_expert[0])
    wr_hi = wr.astype(BF16)
    wr_lo = (wr - wr_hi.astype(F32)).astype(BF16)
    mix_w = (norm1_g[0][None, :], w_in[0].astype(BF16), q_norm_g[0][None, :], k_norm_g[0][None, :],
             wpool_bd.astype(BF16), pool_scale[0][None, :], w_branch_a[0].astype(BF16),
             w_branch_b[0].astype(BF16), w_out[0].astype(BF16),
             norm2_g[0][None, :], jnp.concatenate([wr_hi, wr_lo], axis=1))
    moe_w = (final_norm_g[None, :], w_exp_gate[0].astype(BF16), w_exp_up[0].astype(BF16),
             w_exp_down[0].astype(BF16))

    xp2 = x_prompt.reshape(B * L_ctx, D_MODEL)
    xmid_p, h2_p, gate_p, oh_p, knew, vnew = _mix(xp2, mod, lambda i: 0, None, None, mix_w,
                                                  S=2, L=L_ctx, emit_kv=True)
    sh_p, sg_p, dest_p, meta_p = _moe_dispatch(h2_p, gate_p, oh_p)

    xs2 = x_sample.reshape(Bs * L_lat, D_MODEL)
    cache = (cache_k[:, 0].reshape(Bs, P, KV_W), cache_v[:, 0].reshape(Bs, P, KV_W))
    xmid_s, h2_s, gate_s, oh_s = _mix(xs2, mod, lambda i: 1 + i, cache, _rope_tables(L_lat), mix_w,
                                      S=1, L=L_lat, emit_kv=False)
    sh_s, sg_s, dest_s, meta_s = _moe_dispatch(h2_s, gate_s, oh_s)

    gf, wg, wu, wd = moe_w
    moe_p = _sc_row_gather(_experts(sh_p, sg_p, meta_p, wg, wu, wd), dest_p)
    moe_s = _sc_row_gather(_experts(sh_s, sg_s, meta_s, wg, wu, wd), dest_s)
    y_prompt = _final(xmid_p, moe_p, mod, lambda i: 0, gf)
    blocks_per_seq = L_lat // TOKEN_BLOCK
    y_sample = _final(xmid_s, moe_s, mod, lambda i: 1 + i // blocks_per_seq, gf)

    return (y_prompt.reshape(B, L_ctx, D_MODEL), y_sample.reshape(Bs, L_lat, D_MODEL),
            knew.reshape(B, 1, L_ctx, N_KV_HEADS, HEAD_DIM), vnew.reshape(B, 1, L_ctx, N_KV_HEADS, HEAD_DIM))
```

```python
import functools

import jax
import jax.numpy as jnp
from jax import lax
from jax.experimental import pallas as pl
from jax.experimental.pallas import tpu as pltpu
from jax.experimental.pallas import tpu_sc as plsc

F32 = jnp.float32
BF16 = jnp.bfloat16
I32 = jnp.int32

D_MODEL = 1024
HEAD_DIM = 128
N_HEADS = 8
N_KV_HEADS = 2
GROUP = N_HEADS // N_KV_HEADS
ATTN_W = N_HEADS * HEAD_DIM
KV_W = N_KV_HEADS * HEAD_DIM
POOL_WINDOWS = (2, 4, 8, 16)
POOL_GC = 128
POOL_W = POOL_GC * len(POOL_WINDOWS)
IN_W = ATTN_W + 2 * KV_W + POOL_W + 2 * D_MODEL
GATE_COL = ATTN_W + 2 * KV_W + POOL_W
GRID_W = 64
ROPE_THETA = 10000.0
ROPE_NF = HEAD_DIM // 4
N_EXP_GROUPS = 4
EXP_PER_GROUP = 4
N_EXPERTS = 16
D_EXPERT = 256
EPS = 1e-6

LANES = 128
SUBLANES = 8
COND_ROWS = SUBLANES
POOL_HALO = 8
ROW_BLOCK = 256
ADA_COLS = 768
EXPERT_LANE0 = N_EXP_GROUPS
PAIRS_PER_GROUP = EXP_PER_GROUP * (EXP_PER_GROUP - 1) // 2
N_BUCKETS = N_EXP_GROUPS * PAIRS_PER_GROUP
SORT_TILE = 256
TOKEN_BLOCK = 512
ROW_CHUNKS = D_MODEL // LANES
SC_CORES = 2
SC_SUBCORES = 16
SC_WORKERS = SC_CORES * SC_SUBCORES
SC_LANES = 16
SC_PIECES_PER_GATHER = 128
SC_GATHERS_IN_FLIGHT = 4
V7X_VMEM_LIMIT_BYTES = 56 * 1024 * 1024


def _sigmoid(x):
    return 1.0 / (1.0 + jnp.exp(-x))


def _rms(x):
    return x * lax.rsqrt(jnp.mean(x * x, axis=-1, keepdims=True) + EPS)


def _resident(shape):
    zeros = (0,) * len(shape)
    return pl.BlockSpec(shape, lambda i, *_: zeros, pipeline_mode=pl.Buffered(1))


def _tiles_shape(n):
    return (n // SUBLANES, ROW_CHUNKS, SUBLANES, LANES)


def _tiles_spec(n, block_index):
    return pl.BlockSpec(_tiles_shape(n), lambda *a: (block_index(*a), 0, 0, 0))


def _store_tiles(ref, x):
    for c in range(ROW_CHUNKS):
        ref[:, c, :, :] = x[:, c * LANES:(c + 1) * LANES].reshape(x.shape[0] // SUBLANES, SUBLANES, LANES)


def _load_tiles(ref):
    n = ref.shape[0] * SUBLANES
    return jnp.concatenate([ref[:, c, :, :].reshape(n, LANES) for c in range(ROW_CHUNKS)], axis=1)


def _row(x):
    return jnp.transpose(jnp.broadcast_to(x, (x.shape[0], LANES)))[0:1, :]


def _ada_kernel(c_ref, w_ref, b_ref, o_ref):
    c = c_ref[...]
    s = (c * _sigmoid(c)).astype(BF16)
    o_ref[...] = jnp.dot(s, w_ref[...].astype(BF16), preferred_element_type=F32) + b_ref[...]


def _ada(cond, w_ada, b_ada):
    n = w_ada.shape[1]
    return pl.pallas_call(
        _ada_kernel,
        grid=(n // ADA_COLS,),
        in_specs=[
            pl.BlockSpec((COND_ROWS, D_MODEL), lambda j: (0, 0)),
            pl.BlockSpec((D_MODEL, ADA_COLS), lambda j: (0, j)),
            pl.BlockSpec((1, ADA_COLS), lambda j: (0, j)),
        ],
        out_specs=pl.BlockSpec((COND_ROWS, ADA_COLS), lambda j: (0, j)),
        out_shape=jax.ShapeDtypeStruct((COND_ROWS, n), F32),
        name="ada_mod",
    )(cond, w_ada, b_ada)


def _route(logits):
    lane = lax.broadcasted_iota(I32, logits.shape, 1).astype(F32)
    neg = jnp.float32(-1e30)
    far = jnp.float32(LANES)
    is_g = lane < N_EXP_GROUPS
    gl = jnp.where(is_g, logits, neg)
    gmax = jnp.max(gl, axis=-1, keepdims=True)
    gsel = jnp.min(jnp.where(gl == gmax, lane, far), axis=-1, keepdims=True)
    psel = 1.0 / jnp.sum(jnp.where(is_g, jnp.exp(gl - gmax), 0.0), axis=-1, keepdims=True)
    e_lo = EXPERT_LANE0 + EXP_PER_GROUP * gsel
    el = jnp.where(lane >= e_lo, jnp.where(lane < e_lo + EXP_PER_GROUP, logits, neg), neg)
    v1 = jnp.max(el, axis=-1, keepdims=True)
    i1 = jnp.min(jnp.where(el == v1, lane, far), axis=-1, keepdims=True)
    el2 = jnp.where(lane == i1, neg, el)
    v2 = jnp.max(el2, axis=-1, keepdims=True)
    i2 = jnp.min(jnp.where(el2 == v2, jnp.where(lane == i1, far, lane), far), axis=-1, keepdims=True)
    e2 = jnp.exp(v2 - v1)
    w1 = psel / (1.0 + e2)
    w2 = psel * e2 / (1.0 + e2)
    gate = jnp.where(lane == i1, w1, jnp.where(lane == i2, w2, 0.0))
    a = jnp.minimum(i1, i2) - e_lo
    b = jnp.maximum(i1, i2) - e_lo
    pair = a * (7.0 - a) * 0.5 + (b - a - 1.0)
    return gate, gsel * PAIRS_PER_GROUP + pair


def _mix_kernel(*refs, S, L, P, use_rope, emit_kv):
    it = iter(refs)
    x_ref = next(it)
    mod_ref = next(it)
    if P:
        ck_ref = next(it)
        cv_ref = next(it)
    if use_rope:
        cos_ref = next(it)
        sneg_ref = next(it)
        spos_ref = next(it)
    (g1_ref, win_ref, qg_ref, kg_ref, wpool_ref, pscale_ref, wa_ref, wb_ref, wo_ref,
     g2_ref, wr_ref) = (next(it) for _ in range(11))
    xmid_ref = next(it)
    h2_ref = next(it)
    gate_ref = next(it)
    oh_ref = next(it)
    if emit_kv:
        knew_ref = next(it)
        vnew_ref = next(it)
    q_s, k_s, v_s, xp_s, h_s, attn_s = (next(it) for _ in range(6))

    TM = S * L
    RB = ROW_BLOCK
    scale = HEAD_DIM ** -0.5

    sh1 = mod_ref[0, 0:1, :]
    gain1 = g1_ref[...] * (1.0 + mod_ref[0, 1:2, :])
    gt1 = mod_ref[0, 2:3, :]
    sh2 = mod_ref[0, 3:4, :]
    gain2 = g2_ref[...] * (1.0 + mod_ref[0, 4:5, :])
    qg = qg_ref[...]
    kg = kg_ref[...]

    def project(r, carry):
        r0 = pl.multiple_of(r * RB, RB)
        s = r0 // L
        o = pl.multiple_of(r0 % L, RB)
        hb = (_rms(x_ref[pl.ds(r0, RB), :]) * gain1 + sh1).astype(BF16)
        h_s[pl.ds(r0, RB), :] = hb
        p1 = jnp.dot(hb, win_ref[:, 0:GATE_COL], preferred_element_type=F32)
        if use_rope:
            cs = cos_ref[pl.ds(o, RB), :]
            sn = sneg_ref[pl.ds(o, RB), :]
            sp = spos_ref[pl.ds(o, RB), :]

        def rope(t):
            return (t * cs + pltpu.roll(t, HEAD_DIM - ROPE_NF, 1) * sn + pltpu.roll(t, ROPE_NF, 1) * sp)

        for hd in range(N_HEADS):
            qh = _rms(p1[:, hd * HEAD_DIM:(hd + 1) * HEAD_DIM]) * qg
            if use_rope:
                qh = rope(qh)
            q_s[hd, pl.ds(r0, RB), :] = qh.astype(BF16)
        for kh in range(N_KV_HEADS):
            c0 = ATTN_W + kh * HEAD_DIM
            kk = _rms(p1[:, c0:c0 + HEAD_DIM]) * kg
            if emit_kv:
                knew_ref[pl.ds(r0, RB), kh * HEAD_DIM:(kh + 1) * HEAD_DIM] = kk
            if use_rope:
                kk = rope(kk)
            k_s[s, pl.ds(P + o, RB), kh * HEAD_DIM:(kh + 1) * HEAD_DIM] = kk.astype(BF16)
        vv = p1[:, ATTN_W + KV_W:ATTN_W + 2 * KV_W]
        if emit_kv:
            vnew_ref[pl.ds(r0, RB), :] = vv
        v_s[s, pl.ds(P + o, RB), :] = vv.astype(BF16)
        xp_s[s, pl.ds(POOL_HALO + o, RB), :] = p1[:, ATTN_W + 2 * KV_W:GATE_COL]
        return carry

    @pl.when(pl.program_id(1) == 0)
    def _():
        if P:
            k_s[0, 0:P, :] = ck_ref[0].astype(BF16)
            v_s[0, 0:P, :] = cv_ref[0].astype(BF16)
        xp_s[:, 0:POOL_HALO, :] = jnp.zeros((S, POOL_HALO, POOL_W), F32)
        xp_s[:, L + POOL_HALO:L + 2 * POOL_HALO, :] = jnp.zeros((S, POOL_HALO, POOL_W), F32)
        lax.fori_loop(0, TM // RB, project, 0)

    def mix(r):
        r0 = pl.multiple_of(r * RB, RB)
        s = r0 // L
        o = pl.multiple_of(r0 % L, RB)

        for kh in range(N_KV_HEADS):
            k = k_s[s, :, kh * HEAD_DIM:(kh + 1) * HEAD_DIM]
            v = v_s[s, :, kh * HEAD_DIM:(kh + 1) * HEAD_DIM]
            q4 = q_s[kh * GROUP:(kh + 1) * GROUP, pl.ds(r0, RB), :].reshape(GROUP * RB, HEAD_DIM)
            sc = lax.dot_general(q4, k, (((1,), (1,)), ((), ())), preferred_element_type=F32) * scale
            e = jnp.exp(sc - jnp.max(sc, axis=-1, keepdims=True))
            den = jnp.sum(e, axis=-1, keepdims=True)
            o4 = jnp.dot(e.astype(BF16), v, preferred_element_type=F32) / den
            for g in range(GROUP):
                hd = kh * GROUP + g
                attn_s[:, hd * HEAD_DIM:(hd + 1) * HEAD_DIM] = o4[g * RB:(g + 1) * RB].astype(BF16)
        a = jnp.dot(attn_s[...], wa_ref[...], preferred_element_type=F32)

        t = o + lax.broadcasted_iota(I32, (RB, 1), 0)
        RW = RB + 2 * POOL_HALO
        parts = []
        for gi, w in enumerate(POOL_WINDOWS):
            cols = slice(gi * POOL_GC, (gi + 1) * POOL_GC)
            xw = xp_s[s, pl.ds(o, RW), cols]
            run = xw
            span = 1
            while span < w:
                run = run + pltpu.roll(run, span, 0)
                span *= 2
            if w // 2 > 1:
                run = pltpu.roll(run, RW - (w // 2 - 1), 0)
            tot = run[POOL_HALO:POOL_HALO + RB]
            cnt = (jnp.minimum(t + w // 2, L) - jnp.maximum(t - w // 2, 0)).astype(F32)
            parts.append(tot / cnt - xw[POOL_HALO:POOL_HALO + RB])
        dpool = jnp.concatenate(parts, axis=1).astype(BF16)
        pooled = jnp.dot(dpool, wpool_ref[...], preferred_element_type=F32) * pscale_ref[...]
        b = jnp.dot(pooled.astype(BF16), wb_ref[...], preferred_element_type=F32)

        gates = jnp.dot(h_s[pl.ds(r0, RB), :], win_ref[:, GATE_COL:IN_W], preferred_element_type=F32)
        merged = _sigmoid(gates[:, 0:D_MODEL]) * a + _sigmoid(gates[:, D_MODEL:2 * D_MODEL]) * b
        u = jnp.dot(merged.astype(BF16), wo_ref[...], preferred_element_type=F32)
        xm = x_ref[pl.ds(r0, RB), :] + gt1 * u
        xmid_ref[...] = xm

        h2 = _rms(xm) * gain2 + sh2
        hi = h2.astype(BF16)
        lo = (h2 - hi.astype(F32)).astype(BF16)
        l1 = jnp.dot(hi, wr_ref[...], preferred_element_type=F32)
        l2 = jnp.dot(lo, wr_ref[:, 0:LANES], preferred_element_type=F32)
        gate, bucket = _route(l1[:, 0:LANES] + l1[:, LANES:2 * LANES] + l2)
        _store_tiles(h2_ref, h2)
        gate_ref[...] = gate
        lane = lax.broadcasted_iota(I32, (RB, LANES), 1).astype(F32)
        oh_ref[...] = jnp.where(lane == bucket, 1.0, 0.0).astype(BF16)

    mix(pl.program_id(1))


def _mix(x2d, mod, mod_row, cache, rope_tabs, weights, *, S, L, emit_kv):
    T = x2d.shape[0]
    TM = S * L
    P = cache[0].shape[1] if cache is not None else 0
    use_rope = rope_tabs is not None
    assert T % TM == 0 and L % ROW_BLOCK == 0
    assert not (use_rope or P) or S == 1
    Lk = P + L

    args = [x2d, mod]
    nrb = TM // ROW_BLOCK
    in_specs = [
        pl.BlockSpec((TM, D_MODEL), lambda i, j: (i, 0)),
        pl.BlockSpec((1, 6, D_MODEL), lambda i, j: (mod_row(i), 0, 0)),
    ]
    if P:
        args += list(cache)
        in_specs += [pl.BlockSpec((1, P, KV_W), lambda i, j: (i, 0, 0))] * 2
    if use_rope:
        args += list(rope_tabs)
        in_specs += [_resident((L, HEAD_DIM))] * 3
    args += list(weights)
    in_specs += [_resident(w.shape) for w in weights]

    out_shape = [jax.ShapeDtypeStruct((T, D_MODEL), F32), jax.ShapeDtypeStruct(_tiles_shape(T), F32),
                 jax.ShapeDtypeStruct((T, LANES), F32),
                 jax.ShapeDtypeStruct((T, LANES), BF16)]
    out_specs = [pl.BlockSpec((ROW_BLOCK, D_MODEL), lambda i, j: (i * nrb + j, 0)),
                 _tiles_spec(ROW_BLOCK, lambda i, j: i * nrb + j),
                 pl.BlockSpec((ROW_BLOCK, LANES), lambda i, j: (i * nrb + j, 0)),
                 pl.BlockSpec((ROW_BLOCK, LANES), lambda i, j: (i * nrb + j, 0))]
    if emit_kv:
        out_shape += [jax.ShapeDtypeStruct((T, KV_W), F32)] * 2
        out_specs += [pl.BlockSpec((TM, KV_W), lambda i, j: (i, 0))] * 2

    scratch = [
        pltpu.VMEM((N_HEADS, TM, HEAD_DIM), BF16),
        pltpu.VMEM((S, Lk, KV_W), BF16),
        pltpu.VMEM((S, Lk, KV_W), BF16),
        pltpu.VMEM((S, L + 2 * POOL_HALO, POOL_W), F32),
        pltpu.VMEM((TM, D_MODEL), BF16),
        pltpu.VMEM((ROW_BLOCK, ATTN_W), BF16),
    ]
    kern = functools.partial(_mix_kernel, S=S, L=L, P=P, use_rope=use_rope, emit_kv=emit_kv)
    return pl.pallas_call(
        kern,
        grid=(T // TM, nrb),
        in_specs=in_specs,
        out_specs=out_specs,
        out_shape=out_shape,
        scratch_shapes=scratch,
        compiler_params=pltpu.CompilerParams(
            dimension_semantics=("arbitrary", "arbitrary"), vmem_limit_bytes=V7X_VMEM_LIMIT_BYTES),
        name="mixer_rope" if use_rope else "mixer_ctx",
    )(*args)


def _plan_kernel(oh_ref, dest_ref, meta_ref, *, n_blocks):
    TB = TOKEN_BLOCK
    lane = lax.broadcasted_iota(I32, (SUBLANES, LANES), 1)

    def count(b, acc):
        oh = oh_ref[pl.ds(pl.multiple_of(b * TB, TB), TB), :].astype(F32)
        return acc + jnp.sum(oh, axis=0, keepdims=True)

    counts = lax.fori_loop(0, n_blocks, count, jnp.zeros((SUBLANES, LANES), F32))
    padded = jnp.ceil(counts * (1.0 / SORT_TILE)) * SORT_TILE
    ends = padded
    step = 1
    while step < LANES:
        ends = ends + jnp.where(lane >= step, pltpu.roll(ends, step, 1), 0.0)
        step *= 2
    starts = ends - padded

    tri = jnp.where(lax.broadcasted_iota(I32, (TB, TB), 1) < lax.broadcasted_iota(I32, (TB, TB), 0),
                    1.0, 0.0).astype(BF16)

    def place(b, seen):
        oh = oh_ref[pl.ds(pl.multiple_of(b * TB, TB), TB), :]
        ohf = oh.astype(F32)
        rank = jnp.dot(tri, oh, preferred_element_type=F32)
        base = (starts + seen)[0:1, :]
        d = jnp.sum(ohf * (rank + base), axis=1, keepdims=True)
        dest_ref[b] = _row(d).astype(I32)
        return seen + jnp.sum(ohf, axis=0, keepdims=True)

    lax.fori_loop(0, n_blocks, place, jnp.zeros((SUBLANES, LANES), F32))

    tile_row0 = lax.broadcasted_iota(I32, (LANES, LANES), 0).astype(F32) * SORT_TILE
    is_bucket = lax.broadcasted_iota(I32, (LANES, LANES), 1) < N_BUCKETS
    done = jnp.sum(jnp.where(is_bucket, jnp.where(ends[0:1, :] <= tile_row0, 1.0, 0.0), 0.0),
                   axis=1, keepdims=True)
    bkt = jnp.minimum(done, N_BUCKETS - 1.0)
    grp = (jnp.where(bkt >= PAIRS_PER_GROUP, 1.0, 0.0) + jnp.where(bkt >= 2 * PAIRS_PER_GROUP, 1.0, 0.0)
           + jnp.where(bkt >= 3 * PAIRS_PER_GROUP, 1.0, 0.0))
    pair = bkt - PAIRS_PER_GROUP * grp
    a = jnp.where(pair >= 3.0, 1.0, 0.0) + jnp.where(pair >= 5.0, 1.0, 0.0)
    b = pair - a * (7.0 - a) * 0.5 + a + 1.0
    e1 = EXP_PER_GROUP * grp + a
    e2 = EXP_PER_GROUP * grp + b
    meta = jnp.concatenate(
        [_row(e1), _row(e2), ends[0:1, :] * (1.0 / SORT_TILE), jnp.zeros((SUBLANES - 3, LANES), F32)], axis=0)
    meta_ref[...] = meta.astype(I32)


def _plan(onehot):
    T = onehot.shape[0]
    n_blocks = T // TOKEN_BLOCK
    dest, meta = pl.pallas_call(
        functools.partial(_plan_kernel, n_blocks=n_blocks),
        out_shape=[jax.ShapeDtypeStruct((n_blocks, 1, TOKEN_BLOCK), I32),
                   jax.ShapeDtypeStruct((SUBLANES, LANES), I32)],
        name="moe_plan",
    )(onehot)
    return dest.reshape(T), meta


def _sc_move_rows(src_v, table_hbm, out_hbm, lo, n_rows, idx_v, pieces_v, sem):
    lane = lax.iota(I32, SC_LANES)
    row_in_group = lane & (SUBLANES - 1)
    chunk_in_pair = lane >> 3
    rows_per_gather = SC_PIECES_PER_GATHER // ROW_CHUNKS
    rows_per_step = rows_per_gather * SC_GATHERS_IN_FLIGHT

    @pl.loop(0, n_rows // rows_per_step)
    def _(step):
        copies = []
        for g in range(SC_GATHERS_IN_FLIGHT):
            r0 = step * rows_per_step + g * rows_per_gather
            for v in range(SC_PIECES_PER_GATHER // SC_LANES):
                group, chunk0 = v // (ROW_CHUNKS // 2), 2 * (v % (ROW_CHUNKS // 2))
                tok = plsc.load_gather(src_v, [r0 + group * SUBLANES + row_in_group])
                piece = (tok >> 3) * (SUBLANES * ROW_CHUNKS) + (chunk0 + chunk_in_pair) * SUBLANES + (tok & 7)
                idx_v[pl.ds(g * SC_PIECES_PER_GATHER + v * SC_LANES, SC_LANES)] = piece
            window = pl.ds(g * SC_PIECES_PER_GATHER, SC_PIECES_PER_GATHER)
            copies.append(pltpu.async_copy(table_hbm.at[idx_v.at[window]], pieces_v.at[window], sem))
        for cp in copies:
            cp.wait()
        first = pl.multiple_of((lo + step * rows_per_step) * ROW_CHUNKS, rows_per_step * ROW_CHUNKS)
        pltpu.sync_copy(pieces_v, out_hbm.at[pl.ds(first, rows_per_step * ROW_CHUNKS)])


def _sc_dispatch(h2_flat, gate_rows, dest, n_rows):
    T = dest.shape[0]
    per_worker = n_rows // SC_WORKERS
    rows_per_step = SC_PIECES_PER_GATHER // ROW_CHUNKS * SC_GATHERS_IN_FLIGHT
    assert n_rows % SC_WORKERS == 0 and per_worker % rows_per_step == 0 and T % SC_LANES == 0
    mesh = plsc.VectorSubcoreMesh(core_axis_name="c", subcore_axis_name="s")

    @functools.partial(
        pl.kernel, mesh=mesh,
        out_type=[jax.ShapeDtypeStruct((n_rows * ROW_CHUNKS, LANES), F32),
                  jax.ShapeDtypeStruct((n_rows, LANES), F32)],
        scratch_types=[pltpu.VMEM((T,), I32), pltpu.VMEM((per_worker,), I32),
                       pltpu.VMEM((SC_GATHERS_IN_FLIGHT * SC_PIECES_PER_GATHER,), I32),
                       pltpu.VMEM((SC_GATHERS_IN_FLIGHT * SC_PIECES_PER_GATHER, LANES), F32),
                       pltpu.VMEM((rows_per_step, LANES), F32),
                       pltpu.SemaphoreType.DMA, pltpu.SemaphoreType.DMA],
        compiler_params=pltpu.CompilerParams(use_tc_tiling_on_sc=True, needs_layout_passes=False),
        name="sc_dispatch",
    )
    def dispatch(h2_hbm, gate_hbm, dest_hbm, out_h_hbm, out_g_hbm,
                 dest_v, src_v, idx_v, pieces_v, gates_v, sem_h, sem_g):
        worker = lax.axis_index("s") * SC_CORES + lax.axis_index("c")
        lo = worker * per_worker
        pltpu.sync_copy(dest_hbm, dest_v)

        @pl.loop(0, per_worker // SC_LANES)
        def _(j):
            j0 = pl.multiple_of(j * SC_LANES, SC_LANES)
            src_v[pl.ds(j0, SC_LANES)] = lax.rem(lo + j0 + lax.iota(I32, SC_LANES), T)

        @pl.loop(0, T // SC_LANES)
        def _(j):
            t0 = pl.multiple_of(j * SC_LANES, SC_LANES)
            d = dest_v[pl.ds(t0, SC_LANES)] - lo
            mine = (d >= 0) & (d < per_worker)
            plsc.store_scatter(src_v, [jnp.where(mine, d, 0)], t0 + lax.iota(I32, SC_LANES), mask=mine)

        @pl.loop(0, per_worker // rows_per_step)
        def _(j):
            off = pl.multiple_of(j * rows_per_step, rows_per_step)
            pltpu.async_copy(gate_hbm.at[src_v.at[pl.ds(off, rows_per_step)]], gates_v, sem_g).wait()
            pltpu.sync_copy(gates_v, out_g_hbm.at[pl.ds(lo + off, rows_per_step)])

        _sc_move_rows(src_v, h2_hbm, out_h_hbm, lo, per_worker, idx_v, pieces_v, sem_h)

    return dispatch(h2_flat, gate_rows, dest)


def _expert_kernel(e1s, e2s, n_used, x_ref, gv_ref, wg_ref, wu_ref, wd_ref, o_ref):
    i = pl.program_id(0)

    @pl.when(i < n_used[0])
    def _():
        e1 = e1s[i]
        e2 = e2s[i]
        x = _load_tiles(x_ref).astype(BF16)
        gv = gv_ref[...]
        lane = lax.broadcasted_iota(I32, gv.shape, 1)
        out = None
        for e in (e1, e2):
            ge = jnp.sum(jnp.where(lane == EXPERT_LANE0 + e, gv, 0.0), axis=-1, keepdims=True)
            hg = jnp.dot(x, wg_ref[e], preferred_element_type=F32)
            hu = jnp.dot(x, wu_ref[e], preferred_element_type=F32)
            hid = (hg * _sigmoid(hg) * hu * ge).astype(BF16)
            y = jnp.dot(hid, wd_ref[e], preferred_element_type=F32)
            out = y if out is None else out + y
        _store_tiles(o_ref, out)

    @pl.when(i >= n_used[0])
    def _():
        o_ref[...] = jnp.zeros(o_ref.shape, F32)


def _experts(sorted_h2, sorted_gates, meta, wg, wu, wd):
    n_tiles = sorted_h2.shape[0] * SUBLANES // SORT_TILE
    return pl.pallas_call(
        _expert_kernel,
        grid_spec=pltpu.PrefetchScalarGridSpec(
            num_scalar_prefetch=3,
            grid=(n_tiles,),
            in_specs=[
                _tiles_spec(SORT_TILE, lambda i, e1, e2, nu: jnp.minimum(i, nu[0] - 1)),
                pl.BlockSpec((SORT_TILE, LANES), lambda i, e1, e2, nu: (jnp.minimum(i, nu[0] - 1), 0)),
                _resident(wg.shape), _resident(wu.shape), _resident(wd.shape),
            ],
            out_specs=_tiles_spec(SORT_TILE, lambda i, *_: i),
        ),
        out_shape=jax.ShapeDtypeStruct(_tiles_shape(n_tiles * SORT_TILE), F32),
        compiler_params=pltpu.CompilerParams(
            dimension_semantics=("arbitrary",), vmem_limit_bytes=V7X_VMEM_LIMIT_BYTES),
        name="moe_experts",
    )(meta[0, :n_tiles], meta[1, :n_tiles], meta[2, LANES - 1:LANES], sorted_h2, sorted_gates, wg, wu, wd)


def _sc_row_gather(table_flat, idx):
    n = idx.shape[0]
    per_worker = n // SC_WORKERS
    rows_per_step = SC_PIECES_PER_GATHER // ROW_CHUNKS * SC_GATHERS_IN_FLIGHT
    assert n % SC_WORKERS == 0 and per_worker % rows_per_step == 0
    mesh = plsc.VectorSubcoreMesh(core_axis_name="c", subcore_axis_name="s")

    @functools.partial(
        pl.kernel, mesh=mesh,
        out_type=jax.ShapeDtypeStruct((n * ROW_CHUNKS, LANES), F32),
        scratch_types=[pltpu.VMEM((per_worker,), I32),
                       pltpu.VMEM((SC_GATHERS_IN_FLIGHT * SC_PIECES_PER_GATHER,), I32),
                       pltpu.VMEM((SC_GATHERS_IN_FLIGHT * SC_PIECES_PER_GATHER, LANES), F32),
                       pltpu.SemaphoreType.DMA],
        compiler_params=pltpu.CompilerParams(use_tc_tiling_on_sc=True, needs_layout_passes=False),
        name="sc_row_gather",
    )
    def gather(table_hbm, idx_hbm, out_hbm, src_v, idx_v, pieces_v, sem):
        worker = lax.axis_index("s") * SC_CORES + lax.axis_index("c")
        lo = worker * per_worker
        pltpu.sync_copy(idx_hbm.at[pl.ds(lo, per_worker)], src_v)
        _sc_move_rows(src_v, table_hbm, out_hbm, lo, per_worker, idx_v, pieces_v, sem)

    return gather(table_flat, idx)


def _final_kernel(x_ref, moe_ref, mod_ref, gf_ref, o_ref):
    y = x_ref[...] + mod_ref[0, 5:6, :] * _load_tiles(moe_ref)
    o_ref[...] = _rms(y) * gf_ref[...]


def _final(xmid, moe_rows, mod, mod_row, gf):
    T = xmid.shape[0]
    return pl.pallas_call(
        _final_kernel,
        grid=(T // TOKEN_BLOCK,),
        in_specs=[
            pl.BlockSpec((TOKEN_BLOCK, D_MODEL), lambda i: (i, 0)),
            _tiles_spec(TOKEN_BLOCK, lambda i: i),
            pl.BlockSpec((1, 6, D_MODEL), lambda i: (mod_row(i), 0, 0)),
            pl.BlockSpec((1, D_MODEL), lambda i: (0, 0)),
        ],
        out_specs=pl.BlockSpec((TOKEN_BLOCK, D_MODEL), lambda i: (i, 0)),
        out_shape=jax.ShapeDtypeStruct((T, D_MODEL), F32),
        compiler_params=pltpu.CompilerParams(dimension_semantics=("arbitrary",)),
        name="moe_final",
    )(xmid, moe_rows, mod, gf)


def _flat(tiles):
    return tiles.reshape(-1, LANES)


def _moe_dispatch(h2_tiles, gate_rows, onehot):
    T = gate_rows.shape[0]
    n_tiles = T // SORT_TILE + N_BUCKETS
    n_rows = n_tiles * SORT_TILE
    assert n_tiles <= LANES and T % TOKEN_BLOCK == 0
    dest, meta = _plan(onehot)
    sorted_h2, sorted_gates = _sc_dispatch(_flat(h2_tiles), gate_rows, dest, n_rows)
    return sorted_h2.reshape(_tiles_shape(n_rows)), sorted_gates, dest, meta


def _moe_unpermute(moe_sorted_tiles, dest):
    return _sc_row_gather(_flat(moe_sorted_tiles), dest).reshape(_tiles_shape(dest.shape[0]))


def _rope_tables(n_tokens):
    t = jnp.arange(n_tokens)
    row = (t // GRID_W).astype(F32)
    col = (t % GRID_W).astype(F32)
    freq = ROPE_THETA ** (-jnp.arange(ROPE_NF, dtype=F32) / ROPE_NF)
    ang = jnp.concatenate([row[:, None] * freq] * 2 + [col[:, None] * freq] * 2, axis=-1)
    first = (jnp.arange(HEAD_DIM) % (2 * ROPE_NF)) < ROPE_NF
    sin = jnp.sin(ang)
    return jnp.cos(ang), jnp.where(first, -sin, 0.0), jnp.where(first, 0.0, sin)


def kernel(x_prompt, x_sample, cache_k, cache_v, c, c_ctx, norm1_g, norm2_g, w_ada, b_ada, w_in, q_norm_g, k_norm_g, w_pool, pool_scale, w_branch_a, w_branch_b, w_out, w_router_group, w_router_expert, w_exp_gate, w_exp_up, w_exp_down, final_norm_g):
    assert norm1_g.shape[0] == 1, "single-layer trunk"
    B, L_ctx, _ = x_prompt.shape
    Bs, L_lat, _ = x_sample.shape
    P = cache_k.shape[2]
    assert 1 + Bs <= COND_ROWS

    cond = jnp.zeros((COND_ROWS, D_MODEL), F32).at[0].set(c_ctx).at[1:1 + Bs].set(c)
    mod = _ada(cond, w_ada[0], b_ada[0][None, :]).reshape(COND_ROWS, 6, D_MODEL)

    wpool_bd = jax.scipy.linalg.block_diag(*[w_pool[0, g] for g in range(len(POOL_WINDOWS))])
    wr = jnp.zeros((D_MODEL, LANES), F32)
    wr = wr.at[:, 0:N_EXP_GROUPS].set(w_router_group[0])
    wr = wr.at[:, EXPERT_LANE0:EXPERT_LANE0 + N_EXPERTS].set(w_router_expert[0])
    wr_hi = wr.astype(BF16)
    wr_lo = (wr - wr_hi.astype(F32)).astype(BF16)
    mix_w = (norm1_g[0][None, :], w_in[0].astype(BF16), q_norm_g[0][None, :], k_norm_g[0][None, :],
             wpool_bd.astype(BF16), pool_scale[0][None, :], w_branch_a[0].astype(BF16),
             w_branch_b[0].astype(BF16), w_out[0].astype(BF16),
             norm2_g[0][None, :], jnp.concatenate([wr_hi, wr_lo], axis=1))
    moe_w = (final_norm_g[None, :], w_exp_gate[0].astype(BF16), w_exp_up[0].astype(BF16),
             w_exp_down[0].astype(BF16))

    xp2 = x_prompt.reshape(B * L_ctx, D_MODEL)
    xmid_p, h2_p, gate_p, oh_p, knew, vnew = _mix(xp2, mod, lambda i: 0, None, None, mix_w,
                                                  S=2, L=L_ctx, emit_kv=True)
    sh_p, sg_p, dest_p, meta_p = _moe_dispatch(h2_p, gate_p, oh_p)

    xs2 = x_sample.reshape(Bs * L_lat, D_MODEL)
    cache = (cache_k[:, 0].reshape(Bs, P, KV_W), cache_v[:, 0].reshape(Bs, P, KV_W))
    xmid_s, h2_s, gate_s, oh_s = _mix(xs2, mod, lambda i: 1 + i, cache, _rope_tables(L_lat), mix_w,
                                      S=1, L=L_lat, emit_kv=False)
    sh_s, sg_s, dest_s, meta_s = _moe_dispatch(h2_s, gate_s, oh_s)

    gf, wg, wu, wd = moe_w
    moe_p = _moe_unpermute(_experts(sh_p, sg_p, meta_p, wg, wu, wd), dest_p)
    moe_s = _moe_unpermute(_experts(sh_s, sg_s, meta_s, wg, wu, wd), dest_s)
    y_prompt = _final(xmid_p, moe_p, mod, lambda i: 0, gf)
    blocks_per_seq = L_lat // TOKEN_BLOCK
    y_sample = _final(xmid_s, moe_s, mod, lambda i: 1 + i // blocks_per_seq, gf)

    return (y_prompt.reshape(B, L_ctx, D_MODEL), y_sample.reshape(Bs, L_lat, D_MODEL),
            knew.reshape(B, 1, L_ctx, N_KV_HEADS, HEAD_DIM), vnew.reshape(B, 1, L_ctx, N_KV_HEADS, HEAD_DIM))
```

```python
import functools

import jax
import jax.numpy as jnp
from jax import lax
from jax.experimental import pallas as pl
from jax.experimental.pallas import tpu as pltpu
from jax.experimental.pallas import tpu_sc as plsc

F32 = jnp.float32
BF16 = jnp.bfloat16
I32 = jnp.int32

D_MODEL = 1024
HEAD_DIM = 128
N_HEADS = 8
N_KV_HEADS = 2
GROUP = N_HEADS // N_KV_HEADS
ATTN_W = N_HEADS * HEAD_DIM
KV_W = N_KV_HEADS * HEAD_DIM
POOL_WINDOWS = (2, 4, 8, 16)
POOL_GC = 128
POOL_W = POOL_GC * len(POOL_WINDOWS)
IN_W = ATTN_W + 2 * KV_W + POOL_W + 2 * D_MODEL
GATE_COL = ATTN_W + 2 * KV_W + POOL_W
GRID_W = 64
ROPE_THETA = 10000.0
ROPE_NF = HEAD_DIM // 4
N_EXP_GROUPS = 4
EXP_PER_GROUP = 4
N_EXPERTS = 16
D_EXPERT = 256
EPS = 1e-6

LANES = 128
SUBLANES = 8
COND_ROWS = SUBLANES
POOL_HALO = 8
ROW_BLOCK = 256
ADA_COLS = 768
EXPERT_LANE0 = N_EXP_GROUPS
PAIRS_PER_GROUP = EXP_PER_GROUP * (EXP_PER_GROUP - 1) // 2
N_BUCKETS = N_EXP_GROUPS * PAIRS_PER_GROUP
SORT_TILE = 256
TOKEN_BLOCK = 512
ROW_CHUNKS = D_MODEL // LANES
SC_CORES = 2
SC_SUBCORES = 16
SC_WORKERS = SC_CORES * SC_SUBCORES
SC_LANES = 16
SC_PIECES_PER_GATHER = 128
SC_GATHERS_IN_FLIGHT = 4
V7X_VMEM_LIMIT_BYTES = 56 * 1024 * 1024


def _sigmoid(x):
    return 1.0 / (1.0 + jnp.exp(-x))


def _rms(x):
    return x * lax.rsqrt(jnp.mean(x * x, axis=-1, keepdims=True) + EPS)


def _resident(shape):
    zeros = (0,) * len(shape)
    return pl.BlockSpec(shape, lambda i, *_: zeros, pipeline_mode=pl.Buffered(1))


def _tiles_shape(n):
    return (n // SUBLANES, ROW_CHUNKS, SUBLANES, LANES)


def _tiles_spec(n, block_index):
    return pl.BlockSpec(_tiles_shape(n), lambda *a: (block_index(*a), 0, 0, 0))


def _store_tiles(ref, x):
    for c in range(ROW_CHUNKS):
        ref[:, c, :, :] = x[:, c * LANES:(c + 1) * LANES].reshape(x.shape[0] // SUBLANES, SUBLANES, LANES)


def _load_tiles(ref):
    n = ref.shape[0] * SUBLANES
    return jnp.concatenate([ref[:, c, :, :].reshape(n, LANES) for c in range(ROW_CHUNKS)], axis=1)


def _row(x):
    return jnp.transpose(jnp.broadcast_to(x, (x.shape[0], LANES)))[0:1, :]


def _ada_kernel(c_ref, w_ref, b_ref, o_ref):
    c = c_ref[...]
    s = (c * _sigmoid(c)).astype(BF16)
    o_ref[...] = jnp.dot(s, w_ref[...].astype(BF16), preferred_element_type=F32) + b_ref[...]


def _ada(cond, w_ada, b_ada):
    n = w_ada.shape[1]
    return pl.pallas_call(
        _ada_kernel,
        grid=(n // ADA_COLS,),
        in_specs=[
            pl.BlockSpec((COND_ROWS, D_MODEL), lambda j: (0, 0)),
            pl.BlockSpec((D_MODEL, ADA_COLS), lambda j: (0, j)),
            pl.BlockSpec((1, ADA_COLS), lambda j: (0, j)),
        ],
        out_specs=pl.BlockSpec((COND_ROWS, ADA_COLS), lambda j: (0, j)),
        out_shape=jax.ShapeDtypeStruct((COND_ROWS, n), F32),
        name="ada_mod",
    )(cond, w_ada, b_ada)


def _route(logits):
    lane = lax.broadcasted_iota(I32, logits.shape, 1).astype(F32)
    neg = jnp.float32(-1e30)
    far = jnp.float32(LANES)
    is_g = lane < N_EXP_GROUPS
    gl = jnp.where(is_g, logits, neg)
    gmax = jnp.max(gl, axis=-1, keepdims=True)
    gsel = jnp.min(jnp.where(gl == gmax, lane, far), axis=-1, keepdims=True)
    psel = 1.0 / jnp.sum(jnp.where(is_g, jnp.exp(gl - gmax), 0.0), axis=-1, keepdims=True)
    e_lo = EXPERT_LANE0 + EXP_PER_GROUP * gsel
    el = jnp.where(lane >= e_lo, jnp.where(lane < e_lo + EXP_PER_GROUP, logits, neg), neg)
    v1 = jnp.max(el, axis=-1, keepdims=True)
    i1 = jnp.min(jnp.where(el == v1, lane, far), axis=-1, keepdims=True)
    el2 = jnp.where(lane == i1, neg, el)
    v2 = jnp.max(el2, axis=-1, keepdims=True)
    i2 = jnp.min(jnp.where(el2 == v2, jnp.where(lane == i1, far, lane), far), axis=-1, keepdims=True)
    e2 = jnp.exp(v2 - v1)
    w1 = psel / (1.0 + e2)
    w2 = psel * e2 / (1.0 + e2)
    gate = jnp.where(lane == i1, w1, jnp.where(lane == i2, w2, 0.0))
    a = jnp.minimum(i1, i2) - e_lo
    b = jnp.maximum(i1, i2) - e_lo
    pair = a * (7.0 - a) * 0.5 + (b - a - 1.0)
    return gate, gsel * PAIRS_PER_GROUP + pair


def _mix_kernel(*refs, S, L, P, use_rope, emit_kv, n_cast):
    it = iter(refs)
    x_ref = next(it)
    mod_ref = next(it)
    if P:
        ck_ref = next(it)
        cv_ref = next(it)
    if use_rope:
        cos_ref = next(it)
        sneg_ref = next(it)
        spos_ref = next(it)
    (g1_ref, win_ref, qg_ref, kg_ref, wpool_ref, pscale_ref, wa_ref, wb_ref, wo_ref,
     g2_ref, wr_ref) = (next(it) for _ in range(11))
    cast_in = [next(it) for _ in range(n_cast)]
    xmid_ref = next(it)
    h2_ref = next(it)
    gate_ref = next(it)
    oh_ref = next(it)
    if emit_kv:
        knew_ref = next(it)
        vnew_ref = next(it)
    cast_out = [next(it) for _ in range(n_cast)]
    q_s, k_s, v_s, xp_s, h_s, attn_s = (next(it) for _ in range(6))

    TM = S * L
    RB = ROW_BLOCK
    scale = HEAD_DIM ** -0.5

    sh1 = mod_ref[0, 0:1, :]
    gain1 = g1_ref[...] * (1.0 + mod_ref[0, 1:2, :])
    gt1 = mod_ref[0, 2:3, :]
    sh2 = mod_ref[0, 3:4, :]
    gain2 = g2_ref[...] * (1.0 + mod_ref[0, 4:5, :])
    qg = qg_ref[...]
    kg = kg_ref[...]

    def project(r, carry):
        r0 = pl.multiple_of(r * RB, RB)
        s = r0 // L
        o = pl.multiple_of(r0 % L, RB)
        hb = (_rms(x_ref[pl.ds(r0, RB), :]) * gain1 + sh1).astype(BF16)
        h_s[pl.ds(r0, RB), :] = hb
        p1 = jnp.dot(hb, win_ref[:, 0:GATE_COL], preferred_element_type=F32)
        if use_rope:
            cs = cos_ref[pl.ds(o, RB), :]
            sn = sneg_ref[pl.ds(o, RB), :]
            sp = spos_ref[pl.ds(o, RB), :]

        def rope(t):
            return (t * cs + pltpu.roll(t, HEAD_DIM - ROPE_NF, 1) * sn + pltpu.roll(t, ROPE_NF, 1) * sp)

        for hd in range(N_HEADS):
            qh = _rms(p1[:, hd * HEAD_DIM:(hd + 1) * HEAD_DIM]) * qg
            if use_rope:
                qh = rope(qh)
            q_s[hd, pl.ds(r0, RB), :] = qh.astype(BF16)
        for kh in range(N_KV_HEADS):
            c0 = ATTN_W + kh * HEAD_DIM
            kk = _rms(p1[:, c0:c0 + HEAD_DIM]) * kg
            if emit_kv:
                knew_ref[pl.ds(N_KV_HEADS * r0 + kh, RB, stride=N_KV_HEADS), :] = kk
            if use_rope:
                kk = rope(kk)
            k_s[s, pl.ds(P + o, RB), kh * HEAD_DIM:(kh + 1) * HEAD_DIM] = kk.astype(BF16)
        vv = p1[:, ATTN_W + KV_W:ATTN_W + 2 * KV_W]
        if emit_kv:
            for kh in range(N_KV_HEADS):
                vnew_ref[pl.ds(N_KV_HEADS * r0 + kh, RB, stride=N_KV_HEADS), :] = (
                    vv[:, kh * HEAD_DIM:(kh + 1) * HEAD_DIM])
        v_s[s, pl.ds(P + o, RB), :] = vv.astype(BF16)
        xp_s[s, pl.ds(POOL_HALO + o, RB), :] = p1[:, ATTN_W + 2 * KV_W:GATE_COL]
        return carry

    @pl.when(pl.program_id(1) == 0)
    def _():
        if P:
            k_s[0, 0:P, :] = ck_ref[0].astype(BF16)
            v_s[0, 0:P, :] = cv_ref[0].astype(BF16)
        xp_s[:, 0:POOL_HALO, :] = jnp.zeros((S, POOL_HALO, POOL_W), F32)
        xp_s[:, L + POOL_HALO:L + 2 * POOL_HALO, :] = jnp.zeros((S, POOL_HALO, POOL_W), F32)
        lax.fori_loop(0, TM // RB, project, 0)
        for src, dst in zip(cast_in, cast_out):
            dst[...] = src[...].astype(BF16)

    def mix(r):
        r0 = pl.multiple_of(r * RB, RB)
        s = r0 // L
        o = pl.multiple_of(r0 % L, RB)

        for kh in range(N_KV_HEADS):
            k = k_s[s, :, kh * HEAD_DIM:(kh + 1) * HEAD_DIM]
            v = v_s[s, :, kh * HEAD_DIM:(kh + 1) * HEAD_DIM]
            q4 = q_s[kh * GROUP:(kh + 1) * GROUP, pl.ds(r0, RB), :].reshape(GROUP * RB, HEAD_DIM)
            sc = lax.dot_general(q4, k, (((1,), (1,)), ((), ())), preferred_element_type=F32) * scale
            e = jnp.exp(sc - jnp.max(sc, axis=-1, keepdims=True))
            den = jnp.sum(e, axis=-1, keepdims=True)
            o4 = jnp.dot(e.astype(BF16), v, preferred_element_type=F32) / den
            for g in range(GROUP):
                hd = kh * GROUP + g
                attn_s[:, hd * HEAD_DIM:(hd + 1) * HEAD_DIM] = o4[g * RB:(g + 1) * RB].astype(BF16)
        a = jnp.dot(attn_s[...], wa_ref[...], preferred_element_type=F32)

        t = o + lax.broadcasted_iota(I32, (RB, 1), 0)
        RW = RB + 2 * POOL_HALO
        parts = []
        for gi, w in enumerate(POOL_WINDOWS):
            cols = slice(gi * POOL_GC, (gi + 1) * POOL_GC)
            xw = xp_s[s, pl.ds(o, RW), cols]
            run = xw
            span = 1
            while span < w:
                run = run + pltpu.roll(run, span, 0)
                span *= 2
            if w // 2 > 1:
                run = pltpu.roll(run, RW - (w // 2 - 1), 0)
            tot = run[POOL_HALO:POOL_HALO + RB]
            cnt = (jnp.minimum(t + w // 2, L) - jnp.maximum(t - w // 2, 0)).astype(F32)
            parts.append(tot / cnt - xw[POOL_HALO:POOL_HALO + RB])
        dpool = jnp.concatenate(parts, axis=1).astype(BF16)
        pooled = jnp.dot(dpool, wpool_ref[...], preferred_element_type=F32) * pscale_ref[...]
        b = jnp.dot(pooled.astype(BF16), wb_ref[...], preferred_element_type=F32)

        gates = jnp.dot(h_s[pl.ds(r0, RB), :], win_ref[:, GATE_COL:IN_W], preferred_element_type=F32)
        merged = _sigmoid(gates[:, 0:D_MODEL]) * a + _sigmoid(gates[:, D_MODEL:2 * D_MODEL]) * b
        u = jnp.dot(merged.astype(BF16), wo_ref[...], preferred_element_type=F32)
        xm = x_ref[pl.ds(r0, RB), :] + gt1 * u
        xmid_ref[...] = xm

        h2 = _rms(xm) * gain2 + sh2
        hi = h2.astype(BF16)
        lo = (h2 - hi.astype(F32)).astype(BF16)
        l1 = jnp.dot(hi, wr_ref[...], preferred_element_type=F32)
        l2 = jnp.dot(lo, wr_ref[:, 0:LANES], preferred_element_type=F32)
        gate, bucket = _route(l1[:, 0:LANES] + l1[:, LANES:2 * LANES] + l2)
        _store_tiles(h2_ref, h2)
        gate_ref[...] = gate
        lane = lax.broadcasted_iota(I32, (RB, LANES), 1).astype(F32)
        oh_ref[...] = jnp.where(lane == bucket, 1.0, 0.0).astype(BF16)

    mix(pl.program_id(1))


def _mix(x2d, mod, mod_row, cache, rope_tabs, weights, *, S, L, emit_kv, cast=()):
    T = x2d.shape[0]
    TM = S * L
    P = cache[0].shape[1] if cache is not None else 0
    use_rope = rope_tabs is not None
    assert T % TM == 0 and L % ROW_BLOCK == 0
    assert not (use_rope or P) or S == 1
    Lk = P + L

    args = [x2d, mod]
    nrb = TM // ROW_BLOCK
    in_specs = [
        pl.BlockSpec((TM, D_MODEL), lambda i, j: (i, 0)),
        pl.BlockSpec((1, 6, D_MODEL), lambda i, j: (mod_row(i), 0, 0)),
    ]
    if P:
        args += list(cache)
        in_specs += [pl.BlockSpec((1, P, KV_W), lambda i, j: (i, 0, 0))] * 2
    if use_rope:
        args += list(rope_tabs)
        in_specs += [_resident((L, HEAD_DIM))] * 3
    args += list(weights)
    in_specs += [_resident(w.shape) for w in weights]
    n_steps = T // TM
    cast_specs = []
    for w in cast:
        assert w.shape[0] % n_steps == 0
        blk = (w.shape[0] // n_steps,) + w.shape[1:]
        cast_specs.append(pl.BlockSpec(blk, lambda i, j, n=len(blk): (i,) + (0,) * (n - 1)))
    args += list(cast)
    in_specs += cast_specs

    out_shape = [jax.ShapeDtypeStruct((T, D_MODEL), F32), jax.ShapeDtypeStruct(_tiles_shape(T), F32),
                 jax.ShapeDtypeStruct((T, LANES), F32),
                 jax.ShapeDtypeStruct((T, LANES), BF16)]
    out_specs = [pl.BlockSpec((ROW_BLOCK, D_MODEL), lambda i, j: (i * nrb + j, 0)),
                 _tiles_spec(ROW_BLOCK, lambda i, j: i * nrb + j),
                 pl.BlockSpec((ROW_BLOCK, LANES), lambda i, j: (i * nrb + j, 0)),
                 pl.BlockSpec((ROW_BLOCK, LANES), lambda i, j: (i * nrb + j, 0))]
    if emit_kv:
        out_shape += [jax.ShapeDtypeStruct((T * N_KV_HEADS, HEAD_DIM), F32)] * 2
        out_specs += [pl.BlockSpec((TM * N_KV_HEADS, HEAD_DIM), lambda i, j: (i, 0))] * 2
    out_shape += [jax.ShapeDtypeStruct(w.shape, BF16) for w in cast]
    out_specs += cast_specs

    scratch = [
        pltpu.VMEM((N_HEADS, TM, HEAD_DIM), BF16),
        pltpu.VMEM((S, Lk, KV_W), BF16),
        pltpu.VMEM((S, Lk, KV_W), BF16),
        pltpu.VMEM((S, L + 2 * POOL_HALO, POOL_W), F32),
        pltpu.VMEM((TM, D_MODEL), BF16),
        pltpu.VMEM((ROW_BLOCK, ATTN_W), BF16),
    ]
    kern = functools.partial(_mix_kernel, S=S, L=L, P=P, use_rope=use_rope, emit_kv=emit_kv, n_cast=len(cast))
    return pl.pallas_call(
        kern,
        grid=(T // TM, nrb),
        in_specs=in_specs,
        out_specs=out_specs,
        out_shape=out_shape,
        scratch_shapes=scratch,
        compiler_params=pltpu.CompilerParams(
            dimension_semantics=("arbitrary", "arbitrary"), vmem_limit_bytes=V7X_VMEM_LIMIT_BYTES),
        name="mixer_rope" if use_rope else "mixer_ctx",
    )(*args)


def _plan_kernel(oh_ref, dest_ref, meta_ref, *, n_blocks):
    TB = TOKEN_BLOCK
    lane = lax.broadcasted_iota(I32, (SUBLANES, LANES), 1)

    def count(b, acc):
        oh = oh_ref[pl.ds(pl.multiple_of(b * TB, TB), TB), :].astype(F32)
        return acc + jnp.sum(oh, axis=0, keepdims=True)

    counts = lax.fori_loop(0, n_blocks, count, jnp.zeros((SUBLANES, LANES), F32))
    padded = jnp.ceil(counts * (1.0 / SORT_TILE)) * SORT_TILE
    ends = padded
    step = 1
    while step < LANES:
        ends = ends + jnp.where(lane >= step, pltpu.roll(ends, step, 1), 0.0)
        step *= 2
    starts = ends - padded

    tri = jnp.where(lax.broadcasted_iota(I32, (TB, TB), 1) < lax.broadcasted_iota(I32, (TB, TB), 0),
                    1.0, 0.0).astype(BF16)

    def place(b, seen):
        oh = oh_ref[pl.ds(pl.multiple_of(b * TB, TB), TB), :]
        ohf = oh.astype(F32)
        rank = jnp.dot(tri, oh, preferred_element_type=F32)
        base = (starts + seen)[0:1, :]
        d = jnp.sum(ohf * (rank + base), axis=1, keepdims=True)
        dest_ref[b] = _row(d).astype(I32)
        return seen + jnp.sum(ohf, axis=0, keepdims=True)

    lax.fori_loop(0, n_blocks, place, jnp.zeros((SUBLANES, LANES), F32))

    tile_row0 = lax.broadcasted_iota(I32, (LANES, LANES), 0).astype(F32) * SORT_TILE
    is_bucket = lax.broadcasted_iota(I32, (LANES, LANES), 1) < N_BUCKETS
    done = jnp.sum(jnp.where(is_bucket, jnp.where(ends[0:1, :] <= tile_row0, 1.0, 0.0), 0.0),
                   axis=1, keepdims=True)
    bkt = jnp.minimum(done, N_BUCKETS - 1.0)
    grp = (jnp.where(bkt >= PAIRS_PER_GROUP, 1.0, 0.0) + jnp.where(bkt >= 2 * PAIRS_PER_GROUP, 1.0, 0.0)
           + jnp.where(bkt >= 3 * PAIRS_PER_GROUP, 1.0, 0.0))
    pair = bkt - PAIRS_PER_GROUP * grp
    a = jnp.where(pair >= 3.0, 1.0, 0.0) + jnp.where(pair >= 5.0, 1.0, 0.0)
    b = pair - a * (7.0 - a) * 0.5 + a + 1.0
    e1 = EXP_PER_GROUP * grp + a
    e2 = EXP_PER_GROUP * grp + b
    meta = jnp.concatenate(
        [_row(e1), _row(e2), ends[0:1, :] * (1.0 / SORT_TILE), jnp.zeros((SUBLANES - 3, LANES), F32)], axis=0)
    meta_ref[...] = meta.astype(I32)


def _plan(onehot):
    T = onehot.shape[0]
    n_blocks = T // TOKEN_BLOCK
    dest, meta = pl.pallas_call(
        functools.partial(_plan_kernel, n_blocks=n_blocks),
        out_shape=[jax.ShapeDtypeStruct((n_blocks, 1, TOKEN_BLOCK), I32),
                   jax.ShapeDtypeStruct((SUBLANES, LANES), I32)],
        name="moe_plan",
    )(onehot)
    return dest.reshape(T), meta


def _sc_move_rows(src_v, table_hbm, out_hbm, lo, n_rows, idx_v, pieces_v, sem):
    lane = lax.iota(I32, SC_LANES)
    row_in_group = lane & (SUBLANES - 1)
    chunk_in_pair = lane >> 3
    rows_per_gather = SC_PIECES_PER_GATHER // ROW_CHUNKS
    rows_per_step = rows_per_gather * SC_GATHERS_IN_FLIGHT

    @pl.loop(0, n_rows // rows_per_step)
    def _(step):
        copies = []
        for g in range(SC_GATHERS_IN_FLIGHT):
            r0 = step * rows_per_step + g * rows_per_gather
            for v in range(SC_PIECES_PER_GATHER // SC_LANES):
                group, chunk0 = v // (ROW_CHUNKS // 2), 2 * (v % (ROW_CHUNKS // 2))
                tok = plsc.load_gather(src_v, [r0 + group * SUBLANES + row_in_group])
                piece = (tok >> 3) * (SUBLANES * ROW_CHUNKS) + (chunk0 + chunk_in_pair) * SUBLANES + (tok & 7)
                idx_v[pl.ds(g * SC_PIECES_PER_GATHER + v * SC_LANES, SC_LANES)] = piece
            window = pl.ds(g * SC_PIECES_PER_GATHER, SC_PIECES_PER_GATHER)
            copies.append(pltpu.async_copy(table_hbm.at[idx_v.at[window]], pieces_v.at[window], sem))
        for cp in copies:
            cp.wait()
        first = pl.multiple_of((lo + step * rows_per_step) * ROW_CHUNKS, rows_per_step * ROW_CHUNKS)
        pltpu.sync_copy(pieces_v, out_hbm.at[pl.ds(first, rows_per_step * ROW_CHUNKS)])


def _sc_dispatch(h2_flat, gate_rows, dest, n_rows):
    T = dest.shape[0]
    per_worker = n_rows // SC_WORKERS
    rows_per_step = SC_PIECES_PER_GATHER // ROW_CHUNKS * SC_GATHERS_IN_FLIGHT
    assert n_rows % SC_WORKERS == 0 and per_worker % rows_per_step == 0 and T % SC_LANES == 0
    mesh = plsc.VectorSubcoreMesh(core_axis_name="c", subcore_axis_name="s")

    @functools.partial(
        pl.kernel, mesh=mesh,
        out_type=[jax.ShapeDtypeStruct((n_rows * ROW_CHUNKS, LANES), F32),
                  jax.ShapeDtypeStruct((n_rows, LANES), F32)],
        scratch_types=[pltpu.VMEM((T,), I32), pltpu.VMEM((per_worker,), I32),
                       pltpu.VMEM((SC_GATHERS_IN_FLIGHT * SC_PIECES_PER_GATHER,), I32),
                       pltpu.VMEM((SC_GATHERS_IN_FLIGHT * SC_PIECES_PER_GATHER, LANES), F32),
                       pltpu.VMEM((rows_per_step, LANES), F32),
                       pltpu.SemaphoreType.DMA, pltpu.SemaphoreType.DMA],
        compiler_params=pltpu.CompilerParams(use_tc_tiling_on_sc=True, needs_layout_passes=False),
        name="sc_dispatch",
    )
    def dispatch(h2_hbm, gate_hbm, dest_hbm, out_h_hbm, out_g_hbm,
                 dest_v, src_v, idx_v, pieces_v, gates_v, sem_h, sem_g):
        worker = lax.axis_index("s") * SC_CORES + lax.axis_index("c")
        lo = worker * per_worker
        pltpu.sync_copy(dest_hbm, dest_v)

        @pl.loop(0, per_worker // SC_LANES)
        def _(j):
            j0 = pl.multiple_of(j * SC_LANES, SC_LANES)
            src_v[pl.ds(j0, SC_LANES)] = lax.rem(lo + j0 + lax.iota(I32, SC_LANES), T)

        @pl.loop(0, T // SC_LANES)
        def _(j):
            t0 = pl.multiple_of(j * SC_LANES, SC_LANES)
            d = dest_v[pl.ds(t0, SC_LANES)] - lo
            mine = (d >= 0) & (d < per_worker)
            plsc.store_scatter(src_v, [jnp.where(mine, d, 0)], t0 + lax.iota(I32, SC_LANES), mask=mine)

        @pl.loop(0, per_worker // rows_per_step)
        def _(j):
            off = pl.multiple_of(j * rows_per_step, rows_per_step)
            pltpu.async_copy(gate_hbm.at[src_v.at[pl.ds(off, rows_per_step)]], gates_v, sem_g).wait()
            pltpu.sync_copy(gates_v, out_g_hbm.at[pl.ds(lo + off, rows_per_step)])

        _sc_move_rows(src_v, h2_hbm, out_h_hbm, lo, per_worker, idx_v, pieces_v, sem_h)

    return dispatch(h2_flat, gate_rows, dest)


def _expert_kernel(e1s, e2s, n_used, x_ref, gv_ref, wg_ref, wu_ref, wd_ref, o_ref, wup_s, wdn_s):
    i = pl.program_id(0)
    e1 = e1s[i]
    e2 = e2s[i]
    prev = jnp.maximum(i - 1, 0)
    new_pair = (i == 0) | (e1 != e1s[prev]) | (e2 != e2s[prev])

    @pl.when((i < n_used[0]) & new_pair)
    def _():
        for slot, e in enumerate((e1, e2)):
            wup_s[:, (2 * slot) * D_EXPERT:(2 * slot + 1) * D_EXPERT] = wg_ref[e]
            wup_s[:, (2 * slot + 1) * D_EXPERT:(2 * slot + 2) * D_EXPERT] = wu_ref[e]
            wdn_s[slot * D_EXPERT:(slot + 1) * D_EXPERT, :] = wd_ref[e]

    @pl.when(i < n_used[0])
    def _():
        x = _load_tiles(x_ref).astype(BF16)
        gv = gv_ref[...]
        lane = lax.broadcasted_iota(I32, gv.shape, 1)
        h = jnp.dot(x, wup_s[...], preferred_element_type=F32)
        hid = []
        for slot, e in enumerate((e1, e2)):
            ge = jnp.sum(jnp.where(lane == EXPERT_LANE0 + e, gv, 0.0), axis=-1, keepdims=True)
            hg = h[:, (2 * slot) * D_EXPERT:(2 * slot + 1) * D_EXPERT]
            hu = h[:, (2 * slot + 1) * D_EXPERT:(2 * slot + 2) * D_EXPERT]
            hid.append((hg * _sigmoid(hg) * hu * ge).astype(BF16))
        out = jnp.dot(jnp.concatenate(hid, axis=1), wdn_s[...], preferred_element_type=F32)
        _store_tiles(o_ref, out)

    @pl.when(i >= n_used[0])
    def _():
        o_ref[...] = jnp.zeros(o_ref.shape, F32)


def _experts(sorted_h2, sorted_gates, meta, wg, wu, wd):
    n_tiles = sorted_h2.shape[0] * SUBLANES // SORT_TILE
    return pl.pallas_call(
        _expert_kernel,
        grid_spec=pltpu.PrefetchScalarGridSpec(
            num_scalar_prefetch=3,
            grid=(n_tiles,),
            in_specs=[
                _tiles_spec(SORT_TILE, lambda i, e1, e2, nu: jnp.minimum(i, nu[0] - 1)),
                pl.BlockSpec((SORT_TILE, LANES), lambda i, e1, e2, nu: (jnp.minimum(i, nu[0] - 1), 0)),
                _resident(wg.shape), _resident(wu.shape), _resident(wd.shape),
            ],
            out_specs=_tiles_spec(SORT_TILE, lambda i, *_: i),
            scratch_shapes=[pltpu.VMEM((D_MODEL, 4 * D_EXPERT), BF16), pltpu.VMEM((2 * D_EXPERT, D_MODEL), BF16)],
        ),
        out_shape=jax.ShapeDtypeStruct(_tiles_shape(n_tiles * SORT_TILE), F32),
        compiler_params=pltpu.CompilerParams(
            dimension_semantics=("arbitrary",), vmem_limit_bytes=V7X_VMEM_LIMIT_BYTES),
        name="moe_experts",
    )(meta[0, :n_tiles], meta[1, :n_tiles], meta[2, LANES - 1:LANES], sorted_h2, sorted_gates, wg, wu, wd)


def _sc_row_gather(table_flat, idx):
    n = idx.shape[0]
    per_worker = n // SC_WORKERS
    rows_per_step = SC_PIECES_PER_GATHER // ROW_CHUNKS * SC_GATHERS_IN_FLIGHT
    assert n % SC_WORKERS == 0 and per_worker % rows_per_step == 0
    mesh = plsc.VectorSubcoreMesh(core_axis_name="c", subcore_axis_name="s")

    @functools.partial(
        pl.kernel, mesh=mesh,
        out_type=jax.ShapeDtypeStruct((n * ROW_CHUNKS, LANES), F32),
        scratch_types=[pltpu.VMEM((per_worker,), I32),
                       pltpu.VMEM((SC_GATHERS_IN_FLIGHT * SC_PIECES_PER_GATHER,), I32),
                       pltpu.VMEM((SC_GATHERS_IN_FLIGHT * SC_PIECES_PER_GATHER, LANES), F32),
                       pltpu.SemaphoreType.DMA],
        compiler_params=pltpu.CompilerParams(use_tc_tiling_on_sc=True, needs_layout_passes=False),
        name="sc_row_gather",
    )
    def gather(table_hbm, idx_hbm, out_hbm, src_v, idx_v, pieces_v, sem):
        worker = lax.axis_index("s") * SC_CORES + lax.axis_index("c")
        lo = worker * per_worker
        pltpu.sync_copy(idx_hbm.at[pl.ds(lo, per_worker)], src_v)
        _sc_move_rows(src_v, table_hbm, out_hbm, lo, per_worker, idx_v, pieces_v, sem)

    return gather(table_flat, idx)


def _final_kernel(x_ref, moe_ref, mod_ref, gf_ref, o_ref):
    y = x_ref[...] + mod_ref[0, 5:6, :] * _load_tiles(moe_ref)
    o_ref[...] = _rms(y) * gf_ref[...]


def _final(xmid, moe_rows, mod, mod_row, gf):
    T = xmid.shape[0]
    return pl.pallas_call(
        _final_kernel,
        grid=(T // TOKEN_BLOCK,),
        in_specs=[
            pl.BlockSpec((TOKEN_BLOCK, D_MODEL), lambda i: (i, 0)),
            _tiles_spec(TOKEN_BLOCK, lambda i: i),
            pl.BlockSpec((1, 6, D_MODEL), lambda i: (mod_row(i), 0, 0)),
            pl.BlockSpec((1, D_MODEL), lambda i: (0, 0)),
        ],
        out_specs=pl.BlockSpec((TOKEN_BLOCK, D_MODEL), lambda i: (i, 0)),
        out_shape=jax.ShapeDtypeStruct((T, D_MODEL), F32),
        compiler_params=pltpu.CompilerParams(dimension_semantics=("arbitrary",)),
        name="moe_final",
    )(xmid, moe_rows, mod, gf)


def _flat(tiles):
    return tiles.reshape(-1, LANES)


def _moe_dispatch(h2_tiles, gate_rows, onehot):
    T = gate_rows.shape[0]
    n_tiles = T // SORT_TILE + N_BUCKETS
    n_rows = n_tiles * SORT_TILE
    assert n_tiles <= LANES and T % TOKEN_BLOCK == 0
    dest, meta = _plan(onehot)
    sorted_h2, sorted_gates = _sc_dispatch(_flat(h2_tiles), gate_rows, dest, n_rows)
    return sorted_h2.reshape(_tiles_shape(n_rows)), sorted_gates, dest, meta


def _moe_unpermute(moe_sorted_tiles, dest):
    return _sc_row_gather(_flat(moe_sorted_tiles), dest).reshape(_tiles_shape(dest.shape[0]))


def _rope_tables(n_tokens):
    t = jnp.arange(n_tokens)
    row = (t // GRID_W).astype(F32)
    col = (t % GRID_W).astype(F32)
    freq = ROPE_THETA ** (-jnp.arange(ROPE_NF, dtype=F32) / ROPE_NF)
    ang = jnp.concatenate([row[:, None] * freq] * 2 + [col[:, None] * freq] * 2, axis=-1)
    first = (jnp.arange(HEAD_DIM) % (2 * ROPE_NF)) < ROPE_NF
    sin = jnp.sin(ang)
    return jnp.cos(ang), jnp.where(first, -sin, 0.0), jnp.where(first, 0.0, sin)


def kernel(x_prompt, x_sample, cache_k, cache_v, c, c_ctx, norm1_g, norm2_g, w_ada, b_ada, w_in, q_norm_g, k_norm_g, w_pool, pool_scale, w_branch_a, w_branch_b, w_out, w_router_group, w_router_expert, w_exp_gate, w_exp_up, w_exp_down, final_norm_g):
    assert norm1_g.shape[0] == 1, "single-layer trunk"
    B, L_ctx, _ = x_prompt.shape
    Bs, L_lat, _ = x_sample.shape
    P = cache_k.shape[2]
    assert 1 + Bs <= COND_ROWS

    cond = jnp.zeros((COND_ROWS, D_MODEL), F32).at[0].set(c_ctx).at[1:1 + Bs].set(c)
    mod = _ada(cond, w_ada[0], b_ada[0][None, :]).reshape(COND_ROWS, 6, D_MODEL)

    wpool_bd = jax.scipy.linalg.block_diag(*[w_pool[0, g] for g in range(len(POOL_WINDOWS))])
    wr = jnp.zeros((D_MODEL, LANES), F32)
    wr = wr.at[:, 0:N_EXP_GROUPS].set(w_router_group[0])
    wr = wr.at[:, EXPERT_LANE0:EXPERT_LANE0 + N_EXPERTS].set(w_router_expert[0])
    wr_hi = wr.astype(BF16)
    wr_lo = (wr - wr_hi.astype(F32)).astype(BF16)
    mix_w = (norm1_g[0][None, :], w_in[0].astype(BF16), q_norm_g[0][None, :], k_norm_g[0][None, :],
             wpool_bd.astype(BF16), pool_scale[0][None, :], w_branch_a[0].astype(BF16),
             w_branch_b[0].astype(BF16), w_out[0].astype(BF16),
             norm2_g[0][None, :], jnp.concatenate([wr_hi, wr_lo], axis=1))
    gf = final_norm_g[None, :]

    xp2 = x_prompt.reshape(B * L_ctx, D_MODEL)
    xmid_p, h2_p, gate_p, oh_p, knew, vnew, wg, wu, wd = _mix(
        xp2, mod, lambda i: 0, None, None, mix_w, S=2, L=L_ctx, emit_kv=True,
        cast=(w_exp_gate[0], w_exp_up[0], w_exp_down[0]))
    sh_p, sg_p, dest_p, meta_p = _moe_dispatch(h2_p, gate_p, oh_p)

    xs2 = x_sample.reshape(Bs * L_lat, D_MODEL)
    cache = (cache_k[:, 0].reshape(Bs, P, KV_W), cache_v[:, 0].reshape(Bs, P, KV_W))
    xmid_s, h2_s, gate_s, oh_s = _mix(xs2, mod, lambda i: 1 + i, cache, _rope_tables(L_lat), mix_w,
                                      S=1, L=L_lat, emit_kv=False)
    sh_s, sg_s, dest_s, meta_s = _moe_dispatch(h2_s, gate_s, oh_s)

    moe_p = _moe_unpermute(_experts(sh_p, sg_p, meta_p, wg, wu, wd), dest_p)
    moe_s = _moe_unpermute(_experts(sh_s, sg_s, meta_s, wg, wu, wd), dest_s)
    y_prompt = _final(xmid_p, moe_p, mod, lambda i: 0, gf)
    blocks_per_seq = L_lat // TOKEN_BLOCK
    y_sample = _final(xmid_s, moe_s, mod, lambda i: 1 + i // blocks_per_seq, gf)

    return (y_prompt.reshape(B, L_ctx, D_MODEL), y_sample.reshape(Bs, L_lat, D_MODEL),
            knew.reshape(B, 1, L_ctx, N_KV_HEADS, HEAD_DIM), vnew.reshape(B, 1, L_ctx, N_KV_HEADS, HEAD_DIM))
```

```python
import functools

import jax
import jax.numpy as jnp
from jax import lax
from jax.experimental import pallas as pl
from jax.experimental.pallas import tpu as pltpu
from jax.experimental.pallas import tpu_sc as plsc

F32 = jnp.float32
BF16 = jnp.bfloat16
I32 = jnp.int32
U32 = jnp.uint32

D_MODEL = 1024
HEAD_DIM = 128
N_HEADS = 8
N_KV_HEADS = 2
GROUP = N_HEADS // N_KV_HEADS
ATTN_W = N_HEADS * HEAD_DIM
KV_W = N_KV_HEADS * HEAD_DIM
POOL_WINDOWS = (2, 4, 8, 16)
POOL_GC = 128
POOL_W = POOL_GC * len(POOL_WINDOWS)
IN_W = ATTN_W + 2 * KV_W + POOL_W + 2 * D_MODEL
GATE_COL = ATTN_W + 2 * KV_W + POOL_W
GRID_W = 64
ROPE_THETA = 10000.0
ROPE_NF = HEAD_DIM // 4
N_EXP_GROUPS = 4
EXP_PER_GROUP = 4
N_EXPERTS = 16
D_EXPERT = 256
EPS = 1e-6

LANES = 128
SUBLANES = 8
COND_ROWS = SUBLANES
POOL_HALO = 8
ROW_BLOCK = 256
ADA_COLS = 768
EXPERT_LANE0 = N_EXP_GROUPS
PAIRS_PER_GROUP = EXP_PER_GROUP * (EXP_PER_GROUP - 1) // 2
N_BUCKETS = N_EXP_GROUPS * PAIRS_PER_GROUP
SORT_TILE = 256
TOKEN_BLOCK = 512
ROW_CHUNKS = D_MODEL // LANES
SC_CORES = 2
SC_SUBCORES = 16
SC_WORKERS = SC_CORES * SC_SUBCORES
SC_LANES = 16
SC_PIECES_PER_GATHER = 128
SC_ROWS_PER_STEP = 64
PACKED_CHUNKS = ROW_CHUNKS // 2
V7X_VMEM_LIMIT_BYTES = 56 * 1024 * 1024


def _sigmoid(x):
    return 1.0 / (1.0 + jnp.exp(-x))


def _rms(x):
    return x * lax.rsqrt(jnp.mean(x * x, axis=-1, keepdims=True) + EPS)


def _resident(shape):
    zeros = (0,) * len(shape)
    return pl.BlockSpec(shape, lambda i, *_: zeros, pipeline_mode=pl.Buffered(1))


def _tiles_shape(n, chunks=ROW_CHUNKS):
    return (n // SUBLANES, chunks, SUBLANES, LANES)


def _tiles_spec(n, block_index, chunks=ROW_CHUNKS):
    return pl.BlockSpec(_tiles_shape(n, chunks), lambda *a: (block_index(*a), 0, 0, 0))


def _store_tiles(ref, x):
    for c in range(ref.shape[1]):
        ref[:, c, :, :] = x[:, c * LANES:(c + 1) * LANES].reshape(x.shape[0] // SUBLANES, SUBLANES, LANES)


def _load_tiles(ref):
    n = ref.shape[0] * SUBLANES
    return jnp.concatenate([ref[:, c, :, :].reshape(n, LANES) for c in range(ref.shape[1])], axis=1)


def _pack_bf16_pairs(x):
    bits = pltpu.bitcast(x.astype(BF16).astype(F32), U32)
    w = x.shape[1] // 2
    return bits[:, :w] | (bits[:, w:] >> 16)


def _unpack_bf16_pairs(words):
    hi = pltpu.bitcast(words & jnp.uint32(0xFFFF0000), F32).astype(BF16)
    lo = pltpu.bitcast(words << 16, F32).astype(BF16)
    return jnp.concatenate([hi, lo], axis=1)


def _row(x):
    return jnp.transpose(jnp.broadcast_to(x, (x.shape[0], LANES)))[0:1, :]


def _ada_kernel(c_ref, w_ref, b_ref, o_ref):
    c = c_ref[...]
    s = (c * _sigmoid(c)).astype(BF16)
    o_ref[...] = jnp.dot(s, w_ref[...].astype(BF16), preferred_element_type=F32) + b_ref[...]


def _ada(cond, w_ada, b_ada):
    n = w_ada.shape[1]
    return pl.pallas_call(
        _ada_kernel,
        grid=(n // ADA_COLS,),
        in_specs=[
            pl.BlockSpec((COND_ROWS, D_MODEL), lambda j: (0, 0)),
            pl.BlockSpec((D_MODEL, ADA_COLS), lambda j: (0, j)),
            pl.BlockSpec((1, ADA_COLS), lambda j: (0, j)),
        ],
        out_specs=pl.BlockSpec((COND_ROWS, ADA_COLS), lambda j: (0, j)),
        out_shape=jax.ShapeDtypeStruct((COND_ROWS, n), F32),
        name="ada_mod",
    )(cond, w_ada, b_ada)


def _route(logits):
    lane = lax.broadcasted_iota(I32, logits.shape, 1).astype(F32)
    neg = jnp.float32(-1e30)
    far = jnp.float32(LANES)
    is_g = lane < N_EXP_GROUPS
    gl = jnp.where(is_g, logits, neg)
    gmax = jnp.max(gl, axis=-1, keepdims=True)
    gsel = jnp.min(jnp.where(gl == gmax, lane, far), axis=-1, keepdims=True)
    psel = 1.0 / jnp.sum(jnp.where(is_g, jnp.exp(gl - gmax), 0.0), axis=-1, keepdims=True)
    e_lo = EXPERT_LANE0 + EXP_PER_GROUP * gsel
    el = jnp.where(lane >= e_lo, jnp.where(lane < e_lo + EXP_PER_GROUP, logits, neg), neg)
    v1 = jnp.max(el, axis=-1, keepdims=True)
    i1 = jnp.min(jnp.where(el == v1, lane, far), axis=-1, keepdims=True)
    el2 = jnp.where(lane == i1, neg, el)
    v2 = jnp.max(el2, axis=-1, keepdims=True)
    i2 = jnp.min(jnp.where(el2 == v2, jnp.where(lane == i1, far, lane), far), axis=-1, keepdims=True)
    e2 = jnp.exp(v2 - v1)
    w1 = psel / (1.0 + e2)
    w2 = psel * e2 / (1.0 + e2)
    gate = jnp.where(lane == i1, w1, jnp.where(lane == i2, w2, 0.0))
    a = jnp.minimum(i1, i2) - e_lo
    b = jnp.maximum(i1, i2) - e_lo
    pair = a * (7.0 - a) * 0.5 + (b - a - 1.0)
    return gate, gsel * PAIRS_PER_GROUP + pair


def _mix_kernel(*refs, S, L, P, use_rope, emit_kv, n_cast):
    it = iter(refs)
    x_ref = next(it)
    mod_ref = next(it)
    if P:
        ck_ref = next(it)
        cv_ref = next(it)
    if use_rope:
        cos_ref = next(it)
        sneg_ref = next(it)
        spos_ref = next(it)
    (g1_ref, win_ref, qg_ref, kg_ref, wpool_ref, pscale_ref, wa_ref, wb_ref, wo_ref,
     g2_ref, wr_ref) = (next(it) for _ in range(11))
    cast_in = [next(it) for _ in range(n_cast)]
    xmid_ref = next(it)
    h2_ref = next(it)
    gate_ref = next(it)
    oh_ref = next(it)
    if emit_kv:
        knew_ref = next(it)
        vnew_ref = next(it)
    cast_out = [next(it) for _ in range(n_cast)]
    q_s, k_s, v_s, xp_s, h_s, attn_s = (next(it) for _ in range(6))

    TM = S * L
    RB = ROW_BLOCK
    scale = HEAD_DIM ** -0.5

    sh1 = mod_ref[0, 0:1, :]
    gain1 = g1_ref[...] * (1.0 + mod_ref[0, 1:2, :])
    gt1 = mod_ref[0, 2:3, :]
    sh2 = mod_ref[0, 3:4, :]
    gain2 = g2_ref[...] * (1.0 + mod_ref[0, 4:5, :])
    qg = qg_ref[...]
    kg = kg_ref[...]

    def project(r, carry):
        r0 = pl.multiple_of(r * RB, RB)
        s = r0 // L
        o = pl.multiple_of(r0 % L, RB)
        hb = (_rms(x_ref[pl.ds(r0, RB), :]) * gain1 + sh1).astype(BF16)
        h_s[pl.ds(r0, RB), :] = hb
        p1 = jnp.dot(hb, win_ref[:, 0:GATE_COL], preferred_element_type=F32)
        if use_rope:
            cs = cos_ref[pl.ds(o, RB), :]
            sn = sneg_ref[pl.ds(o, RB), :]
            sp = spos_ref[pl.ds(o, RB), :]

        def rope(t):
            return (t * cs + pltpu.roll(t, HEAD_DIM - ROPE_NF, 1) * sn + pltpu.roll(t, ROPE_NF, 1) * sp)

        for hd in range(N_HEADS):
            qh = _rms(p1[:, hd * HEAD_DIM:(hd + 1) * HEAD_DIM]) * qg
            if use_rope:
                qh = rope(qh)
            q_s[hd, pl.ds(r0, RB), :] = qh.astype(BF16)
        for kh in range(N_KV_HEADS):
            c0 = ATTN_W + kh * HEAD_DIM
            kk = _rms(p1[:, c0:c0 + HEAD_DIM]) * kg
            if emit_kv:
                knew_ref[pl.ds(N_KV_HEADS * r0 + kh, RB, stride=N_KV_HEADS), :] = kk
            if use_rope:
                kk = rope(kk)
            k_s[s, pl.ds(P + o, RB), kh * HEAD_DIM:(kh + 1) * HEAD_DIM] = kk.astype(BF16)
        vv = p1[:, ATTN_W + KV_W:ATTN_W + 2 * KV_W]
        if emit_kv:
            for kh in range(N_KV_HEADS):
                vnew_ref[pl.ds(N_KV_HEADS * r0 + kh, RB, stride=N_KV_HEADS), :] = (
                    vv[:, kh * HEAD_DIM:(kh + 1) * HEAD_DIM])
        v_s[s, pl.ds(P + o, RB), :] = vv.astype(BF16)
        xp_s[s, pl.ds(POOL_HALO + o, RB), :] = p1[:, ATTN_W + 2 * KV_W:GATE_COL]
        return carry

    @pl.when(pl.program_id(1) == 0)
    def _():
        if P:
            k_s[0, 0:P, :] = ck_ref[0].astype(BF16)
            v_s[0, 0:P, :] = cv_ref[0].astype(BF16)
        xp_s[:, 0:POOL_HALO, :] = jnp.zeros((S, POOL_HALO, POOL_W), F32)
        xp_s[:, L + POOL_HALO:L + 2 * POOL_HALO, :] = jnp.zeros((S, POOL_HALO, POOL_W), F32)
        lax.fori_loop(0, TM // RB, project, 0)
        for src, dst in zip(cast_in, cast_out):
            dst[...] = src[...].astype(BF16)

    def mix(r):
        r0 = pl.multiple_of(r * RB, RB)
        s = r0 // L
        o = pl.multiple_of(r0 % L, RB)

        for kh in range(N_KV_HEADS):
            k = k_s[s, :, kh * HEAD_DIM:(kh + 1) * HEAD_DIM]
            v = v_s[s, :, kh * HEAD_DIM:(kh + 1) * HEAD_DIM]
            q4 = q_s[kh * GROUP:(kh + 1) * GROUP, pl.ds(r0, RB), :].reshape(GROUP * RB, HEAD_DIM)
            sc = lax.dot_general(q4, k, (((1,), (1,)), ((), ())), preferred_element_type=F32) * scale
            e = jnp.exp(sc - jnp.max(sc, axis=-1, keepdims=True))
            den = jnp.sum(e, axis=-1, keepdims=True)
            o4 = jnp.dot(e.astype(BF16), v, preferred_element_type=F32) / den
            for g in range(GROUP):
                hd = kh * GROUP + g
                attn_s[:, hd * HEAD_DIM:(hd + 1) * HEAD_DIM] = o4[g * RB:(g + 1) * RB].astype(BF16)
        a = jnp.dot(attn_s[...], wa_ref[...], preferred_element_type=F32)

        t = o + lax.broadcasted_iota(I32, (RB, 1), 0)
        RW = RB + 2 * POOL_HALO
        parts = []
        for gi, w in enumerate(POOL_WINDOWS):
            cols = slice(gi * POOL_GC, (gi + 1) * POOL_GC)
            xw = xp_s[s, pl.ds(o, RW), cols]
            run = xw
            span = 1
            while span < w:
                run = run + pltpu.roll(run, span, 0)
                span *= 2
            if w // 2 > 1:
                run = pltpu.roll(run, RW - (w // 2 - 1), 0)
            tot = run[POOL_HALO:POOL_HALO + RB]
            cnt = (jnp.minimum(t + w // 2, L) - jnp.maximum(t - w // 2, 0)).astype(F32)
            parts.append(tot / cnt - xw[POOL_HALO:POOL_HALO + RB])
        dpool = jnp.concatenate(parts, axis=1).astype(BF16)
        pooled = jnp.dot(dpool, wpool_ref[...], preferred_element_type=F32) * pscale_ref[...]
        b = jnp.dot(pooled.astype(BF16), wb_ref[...], preferred_element_type=F32)

        gates = jnp.dot(h_s[pl.ds(r0, RB), :], win_ref[:, GATE_COL:IN_W], preferred_element_type=F32)
        merged = _sigmoid(gates[:, 0:D_MODEL]) * a + _sigmoid(gates[:, D_MODEL:2 * D_MODEL]) * b
        u = jnp.dot(merged.astype(BF16), wo_ref[...], preferred_element_type=F32)
        xm = x_ref[pl.ds(r0, RB), :] + gt1 * u
        xmid_ref[...] = xm

        h2 = _rms(xm) * gain2 + sh2
        hi = h2.astype(BF16)
        lo = (h2 - hi.astype(F32)).astype(BF16)
        l1 = jnp.dot(hi, wr_ref[...], preferred_element_type=F32)
        l2 = jnp.dot(lo, wr_ref[:, 0:LANES], preferred_element_type=F32)
        gate, bucket = _route(l1[:, 0:LANES] + l1[:, LANES:2 * LANES] + l2)
        _store_tiles(h2_ref, _pack_bf16_pairs(h2))
        gate_ref[...] = gate
        lane = lax.broadcasted_iota(I32, (RB, LANES), 1).astype(F32)
        oh_ref[...] = jnp.where(lane == bucket, 1.0, 0.0).astype(BF16)

    mix(pl.program_id(1))


def _mix(x2d, mod, mod_row, cache, rope_tabs, weights, *, S, L, emit_kv, cast=()):
    T = x2d.shape[0]
    TM = S * L
    P = cache[0].shape[1] if cache is not None else 0
    use_rope = rope_tabs is not None
    assert T % TM == 0 and L % ROW_BLOCK == 0
    assert not (use_rope or P) or S == 1
    Lk = P + L

    args = [x2d, mod]
    nrb = TM // ROW_BLOCK
    in_specs = [
        pl.BlockSpec((TM, D_MODEL), lambda i, j: (i, 0)),
        pl.BlockSpec((1, 6, D_MODEL), lambda i, j: (mod_row(i), 0, 0)),
    ]
    if P:
        args += list(cache)
        in_specs += [pl.BlockSpec((1, P, KV_W), lambda i, j: (i, 0, 0))] * 2
    if use_rope:
        args += list(rope_tabs)
        in_specs += [_resident((L, HEAD_DIM))] * 3
    args += list(weights)
    in_specs += [_resident(w.shape) for w in weights]
    n_steps = T // TM
    cast_specs = []
    for w in cast:
        assert w.shape[0] % n_steps == 0
        blk = (w.shape[0] // n_steps,) + w.shape[1:]
        cast_specs.append(pl.BlockSpec(blk, lambda i, j, n=len(blk): (i,) + (0,) * (n - 1)))
    args += list(cast)
    in_specs += cast_specs

    out_shape = [jax.ShapeDtypeStruct((T, D_MODEL), F32), jax.ShapeDtypeStruct(_tiles_shape(T, PACKED_CHUNKS), U32),
                 jax.ShapeDtypeStruct((T, LANES), F32),
                 jax.ShapeDtypeStruct((T, LANES), BF16)]
    out_specs = [pl.BlockSpec((ROW_BLOCK, D_MODEL), lambda i, j: (i * nrb + j, 0)),
                 _tiles_spec(ROW_BLOCK, lambda i, j: i * nrb + j, PACKED_CHUNKS),
                 pl.BlockSpec((ROW_BLOCK, LANES), lambda i, j: (i * nrb + j, 0)),
                 pl.BlockSpec((ROW_BLOCK, LANES), lambda i, j: (i * nrb + j, 0))]
    if emit_kv:
        out_shape += [jax.ShapeDtypeStruct((T * N_KV_HEADS, HEAD_DIM), F32)] * 2
        out_specs += [pl.BlockSpec((TM * N_KV_HEADS, HEAD_DIM), lambda i, j: (i, 0))] * 2
    out_shape += [jax.ShapeDtypeStruct(w.shape, BF16) for w in cast]
    out_specs += cast_specs

    scratch = [
        pltpu.VMEM((N_HEADS, TM, HEAD_DIM), BF16),
        pltpu.VMEM((S, Lk, KV_W), BF16),
        pltpu.VMEM((S, Lk, KV_W), BF16),
        pltpu.VMEM((S, L + 2 * POOL_HALO, POOL_W), F32),
        pltpu.VMEM((TM, D_MODEL), BF16),
        pltpu.VMEM((ROW_BLOCK, ATTN_W), BF16),
    ]
    kern = functools.partial(_mix_kernel, S=S, L=L, P=P, use_rope=use_rope, emit_kv=emit_kv, n_cast=len(cast))
    return pl.pallas_call(
        kern,
        grid=(T // TM, nrb),
        in_specs=in_specs,
        out_specs=out_specs,
        out_shape=out_shape,
        scratch_shapes=scratch,
        compiler_params=pltpu.CompilerParams(
            dimension_semantics=("arbitrary", "arbitrary"), vmem_limit_bytes=V7X_VMEM_LIMIT_BYTES),
        name="mixer_rope" if use_rope else "mixer_ctx",
    )(*args)


def _plan_kernel(oh_ref, dest_ref, meta_ref, *, n_blocks):
    TB = TOKEN_BLOCK
    lane = lax.broadcasted_iota(I32, (SUBLANES, LANES), 1)

    def count(b, acc):
        oh = oh_ref[pl.ds(pl.multiple_of(b * TB, TB), TB), :].astype(F32)
        return acc + jnp.sum(oh, axis=0, keepdims=True)

    counts = lax.fori_loop(0, n_blocks, count, jnp.zeros((SUBLANES, LANES), F32))
    padded = jnp.ceil(counts * (1.0 / SORT_TILE)) * SORT_TILE
    ends = padded
    step = 1
    while step < LANES:
        ends = ends + jnp.where(lane >= step, pltpu.roll(ends, step, 1), 0.0)
        step *= 2
    starts = ends - padded

    tri = jnp.where(lax.broadcasted_iota(I32, (TB, TB), 1) < lax.broadcasted_iota(I32, (TB, TB), 0),
                    1.0, 0.0).astype(BF16)

    def place(b, seen):
        oh = oh_ref[pl.ds(pl.multiple_of(b * TB, TB), TB), :]
        ohf = oh.astype(F32)
        rank = jnp.dot(tri, oh, preferred_element_type=F32)
        base = (starts + seen)[0:1, :]
        d = jnp.sum(ohf * (rank + base), axis=1, keepdims=True)
        dest_ref[b] = _row(d).astype(I32)
        return seen + jnp.sum(ohf, axis=0, keepdims=True)

    lax.fori_loop(0, n_blocks, place, jnp.zeros((SUBLANES, LANES), F32))

    tile_row0 = lax.broadcasted_iota(I32, (LANES, LANES), 0).astype(F32) * SORT_TILE
    is_bucket = lax.broadcasted_iota(I32, (LANES, LANES), 1) < N_BUCKETS
    done = jnp.sum(jnp.where(is_bucket, jnp.where(ends[0:1, :] <= tile_row0, 1.0, 0.0), 0.0),
                   axis=1, keepdims=True)
    bkt = jnp.minimum(done, N_BUCKETS - 1.0)
    grp = (jnp.where(bkt >= PAIRS_PER_GROUP, 1.0, 0.0) + jnp.where(bkt >= 2 * PAIRS_PER_GROUP, 1.0, 0.0)
           + jnp.where(bkt >= 3 * PAIRS_PER_GROUP, 1.0, 0.0))
    pair = bkt - PAIRS_PER_GROUP * grp
    a = jnp.where(pair >= 3.0, 1.0, 0.0) + jnp.where(pair >= 5.0, 1.0, 0.0)
    b = pair - a * (7.0 - a) * 0.5 + a + 1.0
    e1 = EXP_PER_GROUP * grp + a
    e2 = EXP_PER_GROUP * grp + b
    meta = jnp.concatenate(
        [_row(e1), _row(e2), ends[0:1, :] * (1.0 / SORT_TILE), jnp.zeros((SUBLANES - 3, LANES), F32)], axis=0)
    meta_ref[...] = meta.astype(I32)


def _plan(onehot):
    T = onehot.shape[0]
    n_blocks = T // TOKEN_BLOCK
    dest, meta = pl.pallas_call(
        functools.partial(_plan_kernel, n_blocks=n_blocks),
        out_shape=[jax.ShapeDtypeStruct((n_blocks, 1, TOKEN_BLOCK), I32),
                   jax.ShapeDtypeStruct((SUBLANES, LANES), I32)],
        name="moe_plan",
    )(onehot)
    return dest.reshape(T), meta


def _sc_move_rows(src_v, table_hbm, out_hbm, lo, n_rows, idx_v, pieces_v, sem):
    chunks = pieces_v.shape[0] // SC_ROWS_PER_STEP
    lane = lax.iota(I32, SC_LANES)
    row_in_group = lane & (SUBLANES - 1)
    chunk_in_pair = lane >> 3
    rows_per_gather = SC_PIECES_PER_GATHER // chunks

    @pl.loop(0, n_rows // SC_ROWS_PER_STEP)
    def _(step):
        copies = []
        for g in range(SC_ROWS_PER_STEP // rows_per_gather):
            r0 = step * SC_ROWS_PER_STEP + g * rows_per_gather
            for v in range(SC_PIECES_PER_GATHER // SC_LANES):
                group, chunk0 = v // (chunks // 2), 2 * (v % (chunks // 2))
                tok = plsc.load_gather(src_v, [r0 + group * SUBLANES + row_in_group])
                piece = (tok >> 3) * (SUBLANES * chunks) + (chunk0 + chunk_in_pair) * SUBLANES + (tok & 7)
                idx_v[pl.ds(g * SC_PIECES_PER_GATHER + v * SC_LANES, SC_LANES)] = piece
            window = pl.ds(g * SC_PIECES_PER_GATHER, SC_PIECES_PER_GATHER)
            copies.append(pltpu.async_copy(table_hbm.at[idx_v.at[window]], pieces_v.at[window], sem))
        for cp in copies:
            cp.wait()
        first = pl.multiple_of((lo + step * SC_ROWS_PER_STEP) * chunks, SC_ROWS_PER_STEP * chunks)
        pltpu.sync_copy(pieces_v, out_hbm.at[pl.ds(first, SC_ROWS_PER_STEP * chunks)])


def _sc_scratch(chunks, dtype):
    return [pltpu.VMEM((SC_ROWS_PER_STEP * chunks,), I32), pltpu.VMEM((SC_ROWS_PER_STEP * chunks, LANES), dtype)]


def _sc_dispatch(h2_flat, gate_rows, dest, n_rows):
    T = dest.shape[0]
    per_worker = n_rows // SC_WORKERS
    rows_per_step = SC_ROWS_PER_STEP
    chunks = h2_flat.shape[0] // T
    assert n_rows % SC_WORKERS == 0 and per_worker % rows_per_step == 0 and T % SC_LANES == 0
    mesh = plsc.VectorSubcoreMesh(core_axis_name="c", subcore_axis_name="s")

    @functools.partial(
        pl.kernel, mesh=mesh,
        out_type=[jax.ShapeDtypeStruct((n_rows * chunks, LANES), h2_flat.dtype),
                  jax.ShapeDtypeStruct((n_rows, LANES), F32)],
        scratch_types=[pltpu.VMEM((T,), I32), pltpu.VMEM((per_worker,), I32)]
        + _sc_scratch(chunks, h2_flat.dtype)
        + [pltpu.VMEM((rows_per_step, LANES), F32), pltpu.SemaphoreType.DMA, pltpu.SemaphoreType.DMA],
        compiler_params=pltpu.CompilerParams(use_tc_tiling_on_sc=True, needs_layout_passes=False),
        name="sc_dispatch",
    )
    def dispatch(h2_hbm, gate_hbm, dest_hbm, out_h_hbm, out_g_hbm,
                 dest_v, src_v, idx_v, pieces_v, gates_v, sem_h, sem_g):
        worker = lax.axis_index("s") * SC_CORES + lax.axis_index("c")
        lo = worker * per_worker
        pltpu.sync_copy(dest_hbm, dest_v)

        @pl.loop(0, per_worker // SC_LANES)
        def _(j):
            j0 = pl.multiple_of(j * SC_LANES, SC_LANES)
            src_v[pl.ds(j0, SC_LANES)] = lax.rem(lo + j0 + lax.iota(I32, SC_LANES), T)

        @pl.loop(0, T // SC_LANES)
        def _(j):
            t0 = pl.multiple_of(j * SC_LANES, SC_LANES)
            d = dest_v[pl.ds(t0, SC_LANES)] - lo
            mine = (d >= 0) & (d < per_worker)
            plsc.store_scatter(src_v, [jnp.where(mine, d, 0)], t0 + lax.iota(I32, SC_LANES), mask=mine)

        @pl.loop(0, per_worker // rows_per_step)
        def _(j):
            off = pl.multiple_of(j * rows_per_step, rows_per_step)
            pltpu.async_copy(gate_hbm.at[src_v.at[pl.ds(off, rows_per_step)]], gates_v, sem_g).wait()
            pltpu.sync_copy(gates_v, out_g_hbm.at[pl.ds(lo + off, rows_per_step)])

        _sc_move_rows(src_v, h2_hbm, out_h_hbm, lo, per_worker, idx_v, pieces_v, sem_h)

    return dispatch(h2_flat, gate_rows, dest)


def _expert_kernel(e1s, e2s, n_used, x_ref, gv_ref, wg_ref, wu_ref, wd_ref, o_ref, wup_s, wdn_s):
    i = pl.program_id(0)
    e1 = e1s[i]
    e2 = e2s[i]
    prev = jnp.maximum(i - 1, 0)
    new_pair = (i == 0) | (e1 != e1s[prev]) | (e2 != e2s[prev])

    @pl.when((i < n_used[0]) & new_pair)
    def _():
        for slot, e in enumerate((e1, e2)):
            wup_s[:, (2 * slot) * D_EXPERT:(2 * slot + 1) * D_EXPERT] = wg_ref[e]
            wup_s[:, (2 * slot + 1) * D_EXPERT:(2 * slot + 2) * D_EXPERT] = wu_ref[e]
            wdn_s[slot * D_EXPERT:(slot + 1) * D_EXPERT, :] = wd_ref[e]

    @pl.when(i < n_used[0])
    def _():
        x = _unpack_bf16_pairs(_load_tiles(x_ref))
        gv = gv_ref[...]
        lane = lax.broadcasted_iota(I32, gv.shape, 1)
        h = jnp.dot(x, wup_s[...], preferred_element_type=F32)
        hid = []
        for slot, e in enumerate((e1, e2)):
            ge = jnp.sum(jnp.where(lane == EXPERT_LANE0 + e, gv, 0.0), axis=-1, keepdims=True)
            hg = h[:, (2 * slot) * D_EXPERT:(2 * slot + 1) * D_EXPERT]
            hu = h[:, (2 * slot + 1) * D_EXPERT:(2 * slot + 2) * D_EXPERT]
            hid.append((hg * _sigmoid(hg) * hu * ge).astype(BF16))
        out = jnp.dot(jnp.concatenate(hid, axis=1), wdn_s[...], preferred_element_type=F32)
        _store_tiles(o_ref, out)

    @pl.when(i >= n_used[0])
    def _():
        o_ref[...] = jnp.zeros(o_ref.shape, F32)


def _experts(sorted_h2, sorted_gates, meta, wg, wu, wd):
    n_tiles = sorted_h2.shape[0] * SUBLANES // SORT_TILE
    return pl.pallas_call(
        _expert_kernel,
        grid_spec=pltpu.PrefetchScalarGridSpec(
            num_scalar_prefetch=3,
            grid=(n_tiles,),
            in_specs=[
                _tiles_spec(SORT_TILE, lambda i, e1, e2, nu: jnp.minimum(i, nu[0] - 1), PACKED_CHUNKS),
                pl.BlockSpec((SORT_TILE, LANES), lambda i, e1, e2, nu: (jnp.minimum(i, nu[0] - 1), 0)),
                _resident(wg.shape), _resident(wu.shape), _resident(wd.shape),
            ],
            out_specs=_tiles_spec(SORT_TILE, lambda i, *_: i),
            scratch_shapes=[pltpu.VMEM((D_MODEL, 4 * D_EXPERT), BF16), pltpu.VMEM((2 * D_EXPERT, D_MODEL), BF16)],
        ),
        out_shape=jax.ShapeDtypeStruct(_tiles_shape(n_tiles * SORT_TILE), F32),
        compiler_params=pltpu.CompilerParams(
            dimension_semantics=("arbitrary",), vmem_limit_bytes=V7X_VMEM_LIMIT_BYTES),
        name="moe_experts",
    )(meta[0, :n_tiles], meta[1, :n_tiles], meta[2, LANES - 1:LANES], sorted_h2, sorted_gates, wg, wu, wd)


def _sc_row_gather(table_flat, idx):
    n = idx.shape[0]
    per_worker = n // SC_WORKERS
    chunks = ROW_CHUNKS
    assert n % SC_WORKERS == 0 and per_worker % SC_ROWS_PER_STEP == 0
    mesh = plsc.VectorSubcoreMesh(core_axis_name="c", subcore_axis_name="s")

    @functools.partial(
        pl.kernel, mesh=mesh,
        out_type=jax.ShapeDtypeStruct((n * chunks, LANES), table_flat.dtype),
        scratch_types=[pltpu.VMEM((per_worker,), I32)] + _sc_scratch(chunks, table_flat.dtype)
        + [pltpu.SemaphoreType.DMA],
        compiler_params=pltpu.CompilerParams(use_tc_tiling_on_sc=True, needs_layout_passes=False),
        name="sc_row_gather",
    )
    def gather(table_hbm, idx_hbm, out_hbm, src_v, idx_v, pieces_v, sem):
        worker = lax.axis_index("s") * SC_CORES + lax.axis_index("c")
        lo = worker * per_worker
        pltpu.sync_copy(idx_hbm.at[pl.ds(lo, per_worker)], src_v)
        _sc_move_rows(src_v, table_hbm, out_hbm, lo, per_worker, idx_v, pieces_v, sem)

    return gather(table_flat, idx)


def _final_kernel(x_ref, moe_ref, mod_ref, gf_ref, o_ref):
    y = x_ref[...] + mod_ref[0, 5:6, :] * _load_tiles(moe_ref)
    o_ref[...] = _rms(y) * gf_ref[...]


def _final(xmid, moe_rows, mod, mod_row, gf):
    T = xmid.shape[0]
    return pl.pallas_call(
        _final_kernel,
        grid=(T // TOKEN_BLOCK,),
        in_specs=[
            pl.BlockSpec((TOKEN_BLOCK, D_MODEL), lambda i: (i, 0)),
            _tiles_spec(TOKEN_BLOCK, lambda i: i),
            pl.BlockSpec((1, 6, D_MODEL), lambda i: (mod_row(i), 0, 0)),
            pl.BlockSpec((1, D_MODEL), lambda i: (0, 0)),
        ],
        out_specs=pl.BlockSpec((TOKEN_BLOCK, D_MODEL), lambda i: (i, 0)),
        out_shape=jax.ShapeDtypeStruct((T, D_MODEL), F32),
        compiler_params=pltpu.CompilerParams(dimension_semantics=("arbitrary",)),
        name="moe_final",
    )(xmid, moe_rows, mod, gf)


def _flat(tiles):
    return tiles.reshape(-1, LANES)


def _moe_dispatch(h2_tiles, gate_rows, onehot):
    T = gate_rows.shape[0]
    n_tiles = T // SORT_TILE + N_BUCKETS
    n_rows = n_tiles * SORT_TILE
    assert n_tiles <= LANES and T % TOKEN_BLOCK == 0
    dest, meta = _plan(onehot)
    sorted_h2, sorted_gates = _sc_dispatch(_flat(h2_tiles), gate_rows, dest, n_rows)
    return sorted_h2.reshape(_tiles_shape(n_rows, PACKED_CHUNKS)), sorted_gates, dest, meta


def _moe_unpermute(moe_sorted_tiles, dest):
    return _sc_row_gather(_flat(moe_sorted_tiles), dest).reshape(_tiles_shape(dest.shape[0]))


def _rope_tables(n_tokens):
    t = jnp.arange(n_tokens)
    row = (t // GRID_W).astype(F32)
    col = (t % GRID_W).astype(F32)
    freq = ROPE_THETA ** (-jnp.arange(ROPE_NF, dtype=F32) / ROPE_NF)
    ang = jnp.concatenate([row[:, None] * freq] * 2 + [col[:, None] * freq] * 2, axis=-1)
    first = (jnp.arange(HEAD_DIM) % (2 * ROPE_NF)) < ROPE_NF
    sin = jnp.sin(ang)
    return jnp.cos(ang), jnp.where(first, -sin, 0.0), jnp.where(first, 0.0, sin)


def kernel(x_prompt, x_sample, cache_k, cache_v, c, c_ctx, norm1_g, norm2_g, w_ada, b_ada, w_in, q_norm_g, k_norm_g, w_pool, pool_scale, w_branch_a, w_branch_b, w_out, w_router_group, w_router_expert, w_exp_gate, w_exp_up, w_exp_down, final_norm_g):
    assert norm1_g.shape[0] == 1, "single-layer trunk"
    B, L_ctx, _ = x_prompt.shape
    Bs, L_lat, _ = x_sample.shape
    P = cache_k.shape[2]
    assert 1 + Bs <= COND_ROWS

    cond = jnp.zeros((COND_ROWS, D_MODEL), F32).at[0].set(c_ctx).at[1:1 + Bs].set(c)
    mod = _ada(cond, w_ada[0], b_ada[0][None, :]).reshape(COND_ROWS, 6, D_MODEL)

    wpool_bd = jax.scipy.linalg.block_diag(*[w_pool[0, g] for g in range(len(POOL_WINDOWS))])
    wr = jnp.zeros((D_MODEL, LANES), F32)
    wr = wr.at[:, 0:N_EXP_GROUPS].set(w_router_group[0])
    wr = wr.at[:, EXPERT_LANE0:EXPERT_LANE0 + N_EXPERTS].set(w_router_expert[0])
    wr_hi = wr.astype(BF16)
    wr_lo = (wr - wr_hi.astype(F32)).astype(BF16)
    mix_w = (norm1_g[0][None, :], w_in[0].astype(BF16), q_norm_g[0][None, :], k_norm_g[0][None, :],
             wpool_bd.astype(BF16), pool_scale[0][None, :], w_branch_a[0].astype(BF16),
             w_branch_b[0].astype(BF16), w_out[0].astype(BF16),
             norm2_g[0][None, :], jnp.concatenate([wr_hi, wr_lo], axis=1))
    gf = final_norm_g[None, :]

    xp2 = x_prompt.reshape(B * L_ctx, D_MODEL)
    xmid_p, h2_p, gate_p, oh_p, knew, vnew, wg, wu, wd = _mix(
        xp2, mod, lambda i: 0, None, None, mix_w, S=2, L=L_ctx, emit_kv=True,
        cast=(w_exp_gate[0], w_exp_up[0], w_exp_down[0]))
    sh_p, sg_p, dest_p, meta_p = _moe_dispatch(h2_p, gate_p, oh_p)

    xs2 = x_sample.reshape(Bs * L_lat, D_MODEL)
    cache = (cache_k[:, 0].reshape(Bs, P, KV_W), cache_v[:, 0].reshape(Bs, P, KV_W))
    xmid_s, h2_s, gate_s, oh_s = _mix(xs2, mod, lambda i: 1 + i, cache, _rope_tables(L_lat), mix_w,
                                      S=1, L=L_lat, emit_kv=False)
    sh_s, sg_s, dest_s, meta_s = _moe_dispatch(h2_s, gate_s, oh_s)

    moe_p = _moe_unpermute(_experts(sh_p, sg_p, meta_p, wg, wu, wd), dest_p)
    moe_s = _moe_unpermute(_experts(sh_s, sg_s, meta_s, wg, wu, wd), dest_s)
    y_prompt = _final(xmid_p, moe_p, mod, lambda i: 0, gf)
    blocks_per_seq = L_lat // TOKEN_BLOCK
    y_sample = _final(xmid_s, moe_s, mod, lambda i: 1 + i // blocks_per_seq, gf)

    return (y_prompt.reshape(B, L_ctx, D_MODEL), y_sample.reshape(Bs, L_lat, D_MODEL),
            knew.reshape(B, 1, L_ctx, N_KV_HEADS, HEAD_DIM), vnew.reshape(B, 1, L_ctx, N_KV_HEADS, HEAD_DIM))
```

```python
import functools

import jax
import jax.numpy as jnp
from jax import lax
from jax.experimental import pallas as pl
from jax.experimental.pallas import tpu as pltpu
from jax.experimental.pallas import tpu_sc as plsc

F32 = jnp.float32
BF16 = jnp.bfloat16
I32 = jnp.int32
U32 = jnp.uint32

D_MODEL = 1024
HEAD_DIM = 128
N_HEADS = 8
N_KV_HEADS = 2
GROUP = N_HEADS // N_KV_HEADS
ATTN_W = N_HEADS * HEAD_DIM
KV_W = N_KV_HEADS * HEAD_DIM
POOL_WINDOWS = (2, 4, 8, 16)
POOL_GC = 128
POOL_W = POOL_GC * len(POOL_WINDOWS)
IN_W = ATTN_W + 2 * KV_W + POOL_W + 2 * D_MODEL
GATE_COL = ATTN_W + 2 * KV_W + POOL_W
GRID_W = 64
ROPE_THETA = 10000.0
ROPE_NF = HEAD_DIM // 4
N_EXP_GROUPS = 4
EXP_PER_GROUP = 4
N_EXPERTS = 16
D_EXPERT = 256
EPS = 1e-6

LANES = 128
SUBLANES = 8
COND_ROWS = SUBLANES
POOL_HALO = 8
ROW_BLOCK = 256
ADA_COLS = 768
EXPERT_LANE0 = N_EXP_GROUPS
PAIRS_PER_GROUP = EXP_PER_GROUP * (EXP_PER_GROUP - 1) // 2
N_BUCKETS = N_EXP_GROUPS * PAIRS_PER_GROUP
SORT_TILE = 256
EXPERT_TILES_PER_STEP = 4
TOKEN_BLOCK = 512
ROW_CHUNKS = D_MODEL // LANES
SC_CORES = 2
SC_SUBCORES = 16
SC_WORKERS = SC_CORES * SC_SUBCORES
SC_LANES = 16
SC_PIECES_PER_GATHER = 128
SC_ROWS_PER_STEP = 64
PACKED_CHUNKS = ROW_CHUNKS // 2
V7X_VMEM_LIMIT_BYTES = 56 * 1024 * 1024


def _sigmoid(x):
    return 1.0 / (1.0 + jnp.exp(-x))


def _rms(x):
    return x * lax.rsqrt(jnp.mean(x * x, axis=-1, keepdims=True) + EPS)


def _resident(shape):
    zeros = (0,) * len(shape)
    return pl.BlockSpec(shape, lambda i, *_: zeros, pipeline_mode=pl.Buffered(1))


def _tiles_shape(n, chunks=ROW_CHUNKS):
    return (n // SUBLANES, chunks, SUBLANES, LANES)


def _tiles_spec(n, block_index, chunks=ROW_CHUNKS):
    return pl.BlockSpec(_tiles_shape(n, chunks), lambda *a: (block_index(*a), 0, 0, 0))


def _store_tiles(ref, x):
    for c in range(ref.shape[1]):
        ref[:, c, :, :] = x[:, c * LANES:(c + 1) * LANES].reshape(x.shape[0] // SUBLANES, SUBLANES, LANES)


def _load_tiles(ref):
    n = ref.shape[0] * SUBLANES
    return jnp.concatenate([ref[:, c, :, :].reshape(n, LANES) for c in range(ref.shape[1])], axis=1)


def _pack_bf16_pairs(x):
    bits = pltpu.bitcast(x.astype(BF16).astype(F32), U32)
    w = x.shape[1] // 2
    return bits[:, :w] | (bits[:, w:] >> 16)


def _unpack_bf16_pairs(words):
    hi = pltpu.bitcast(words & jnp.uint32(0xFFFF0000), F32).astype(BF16)
    lo = pltpu.bitcast(words << 16, F32).astype(BF16)
    return jnp.concatenate([hi, lo], axis=1)


def _row(x):
    return jnp.transpose(jnp.broadcast_to(x, (x.shape[0], LANES)))[0:1, :]


def _ada_kernel(c_ref, w_ref, b_ref, o_ref):
    c = c_ref[...]
    s = (c * _sigmoid(c)).astype(BF16)
    o_ref[...] = jnp.dot(s, w_ref[...].astype(BF16), preferred_element_type=F32) + b_ref[...]


def _ada(cond, w_ada, b_ada):
    n = w_ada.shape[1]
    return pl.pallas_call(
        _ada_kernel,
        grid=(n // ADA_COLS,),
        in_specs=[
            pl.BlockSpec((COND_ROWS, D_MODEL), lambda j: (0, 0)),
            pl.BlockSpec((D_MODEL, ADA_COLS), lambda j: (0, j)),
            pl.BlockSpec((1, ADA_COLS), lambda j: (0, j)),
        ],
        out_specs=pl.BlockSpec((COND_ROWS, ADA_COLS), lambda j: (0, j)),
        out_shape=jax.ShapeDtypeStruct((COND_ROWS, n), F32),
        name="ada_mod",
    )(cond, w_ada, b_ada)


def _route(logits):
    lane = lax.broadcasted_iota(I32, logits.shape, 1).astype(F32)
    neg = jnp.float32(-1e30)
    far = jnp.float32(LANES)
    is_g = lane < N_EXP_GROUPS
    gl = jnp.where(is_g, logits, neg)
    gmax = jnp.max(gl, axis=-1, keepdims=True)
    gsel = jnp.min(jnp.where(gl == gmax, lane, far), axis=-1, keepdims=True)
    psel = 1.0 / jnp.sum(jnp.where(is_g, jnp.exp(gl - gmax), 0.0), axis=-1, keepdims=True)
    e_lo = EXPERT_LANE0 + EXP_PER_GROUP * gsel
    el = jnp.where(lane >= e_lo, jnp.where(lane < e_lo + EXP_PER_GROUP, logits, neg), neg)
    v1 = jnp.max(el, axis=-1, keepdims=True)
    i1 = jnp.min(jnp.where(el == v1, lane, far), axis=-1, keepdims=True)
    el2 = jnp.where(lane == i1, neg, el)
    v2 = jnp.max(el2, axis=-1, keepdims=True)
    i2 = jnp.min(jnp.where(el2 == v2, jnp.where(lane == i1, far, lane), far), axis=-1, keepdims=True)
    e2 = jnp.exp(v2 - v1)
    w1 = psel / (1.0 + e2)
    w2 = psel * e2 / (1.0 + e2)
    gate = jnp.where(lane == i1, w1, jnp.where(lane == i2, w2, 0.0))
    a = jnp.minimum(i1, i2) - e_lo
    b = jnp.maximum(i1, i2) - e_lo
    pair = a * (7.0 - a) * 0.5 + (b - a - 1.0)
    return gate, gsel * PAIRS_PER_GROUP + pair


def _mix_kernel(*refs, S, L, P, use_rope, emit_kv, n_cast):
    it = iter(refs)
    x_ref = next(it)
    mod_ref = next(it)
    if P:
        ck_ref = next(it)
        cv_ref = next(it)
    if use_rope:
        cos_ref = next(it)
        sneg_ref = next(it)
        spos_ref = next(it)
    (g1_ref, win_ref, qg_ref, kg_ref, wpool_ref, pscale_ref, wa_ref, wb_ref, wo_ref,
     g2_ref, wr_ref) = (next(it) for _ in range(11))
    cast_in = [next(it) for _ in range(n_cast)]
    xmid_ref = next(it)
    h2_ref = next(it)
    gate_ref = next(it)
    oh_ref = next(it)
    if emit_kv:
        knew_ref = next(it)
        vnew_ref = next(it)
    cast_out = [next(it) for _ in range(n_cast)]
    q_s, k_s, v_s, xp_s, h_s, attn_s = (next(it) for _ in range(6))

    TM = S * L
    RB = ROW_BLOCK
    scale = HEAD_DIM ** -0.5

    sh1 = mod_ref[0, 0:1, :]
    gain1 = g1_ref[...] * (1.0 + mod_ref[0, 1:2, :])
    gt1 = mod_ref[0, 2:3, :]
    sh2 = mod_ref[0, 3:4, :]
    gain2 = g2_ref[...] * (1.0 + mod_ref[0, 4:5, :])
    qg = qg_ref[...]
    kg = kg_ref[...]

    def project(r, carry):
        r0 = pl.multiple_of(r * RB, RB)
        s = r0 // L
        o = pl.multiple_of(r0 % L, RB)
        hb = (_rms(x_ref[pl.ds(r0, RB), :]) * gain1 + sh1).astype(BF16)
        h_s[pl.ds(r0, RB), :] = hb
        p1 = jnp.dot(hb, win_ref[:, 0:GATE_COL], preferred_element_type=F32)
        if use_rope:
            cs = cos_ref[pl.ds(o, RB), :]
            sn = sneg_ref[pl.ds(o, RB), :]
            sp = spos_ref[pl.ds(o, RB), :]

        def rope(t):
            return (t * cs + pltpu.roll(t, HEAD_DIM - ROPE_NF, 1) * sn + pltpu.roll(t, ROPE_NF, 1) * sp)

        for hd in range(N_HEADS):
            qh = _rms(p1[:, hd * HEAD_DIM:(hd + 1) * HEAD_DIM]) * qg
            if use_rope:
                qh = rope(qh)
            q_s[hd, pl.ds(r0, RB), :] = qh.astype(BF16)
        for kh in range(N_KV_HEADS):
            c0 = ATTN_W + kh * HEAD_DIM
            kk = _rms(p1[:, c0:c0 + HEAD_DIM]) * kg
            if emit_kv:
                knew_ref[pl.ds(N_KV_HEADS * r0 + kh, RB, stride=N_KV_HEADS), :] = kk
            if use_rope:
                kk = rope(kk)
            k_s[s, pl.ds(P + o, RB), kh * HEAD_DIM:(kh + 1) * HEAD_DIM] = kk.astype(BF16)
        vv = p1[:, ATTN_W + KV_W:ATTN_W + 2 * KV_W]
        if emit_kv:
            for kh in range(N_KV_HEADS):
                vnew_ref[pl.ds(N_KV_HEADS * r0 + kh, RB, stride=N_KV_HEADS), :] = (
                    vv[:, kh * HEAD_DIM:(kh + 1) * HEAD_DIM])
        v_s[s, pl.ds(P + o, RB), :] = vv.astype(BF16)
        xp_s[s, pl.ds(POOL_HALO + o, RB), :] = p1[:, ATTN_W + 2 * KV_W:GATE_COL]
        return carry

    @pl.when(pl.program_id(1) == 0)
    def _():
        if P:
            k_s[0, 0:P, :] = ck_ref[0].astype(BF16)
            v_s[0, 0:P, :] = cv_ref[0].astype(BF16)
        xp_s[:, 0:POOL_HALO, :] = jnp.zeros((S, POOL_HALO, POOL_W), F32)
        xp_s[:, L + POOL_HALO:L + 2 * POOL_HALO, :] = jnp.zeros((S, POOL_HALO, POOL_W), F32)
        lax.fori_loop(0, TM // RB, project, 0)
        for src, dst in zip(cast_in, cast_out):
            dst[...] = src[...].astype(BF16)

    def mix(r):
        r0 = pl.multiple_of(r * RB, RB)
        s = r0 // L
        o = pl.multiple_of(r0 % L, RB)

        for kh in range(N_KV_HEADS):
            k = k_s[s, :, kh * HEAD_DIM:(kh + 1) * HEAD_DIM]
            v = v_s[s, :, kh * HEAD_DIM:(kh + 1) * HEAD_DIM]
            q4 = q_s[kh * GROUP:(kh + 1) * GROUP, pl.ds(r0, RB), :].reshape(GROUP * RB, HEAD_DIM)
            sc = lax.dot_general(q4, k, (((1,), (1,)), ((), ())), preferred_element_type=F32) * scale
            e = jnp.exp(sc - jnp.max(sc, axis=-1, keepdims=True))
            den = jnp.sum(e, axis=-1, keepdims=True)
            o4 = jnp.dot(e.astype(BF16), v, preferred_element_type=F32) / den
            for g in range(GROUP):
                hd = kh * GROUP + g
                attn_s[:, hd * HEAD_DIM:(hd + 1) * HEAD_DIM] = o4[g * RB:(g + 1) * RB].astype(BF16)
        a = jnp.dot(attn_s[...], wa_ref[...], preferred_element_type=F32)

        t = o + lax.broadcasted_iota(I32, (RB, 1), 0)
        RW = RB + 2 * POOL_HALO
        parts = []
        for gi, w in enumerate(POOL_WINDOWS):
            cols = slice(gi * POOL_GC, (gi + 1) * POOL_GC)
            xw = xp_s[s, pl.ds(o, RW), cols]
            run = xw
            span = 1
            while span < w:
                run = run + pltpu.roll(run, span, 0)
                span *= 2
            if w // 2 > 1:
                run = pltpu.roll(run, RW - (w // 2 - 1), 0)
            tot = run[POOL_HALO:POOL_HALO + RB]
            cnt = (jnp.minimum(t + w // 2, L) - jnp.maximum(t - w // 2, 0)).astype(F32)
            parts.append(tot / cnt - xw[POOL_HALO:POOL_HALO + RB])
        dpool = jnp.concatenate(parts, axis=1).astype(BF16)
        pooled = jnp.dot(dpool, wpool_ref[...], preferred_element_type=F32) * pscale_ref[...]
        b = jnp.dot(pooled.astype(BF16), wb_ref[...], preferred_element_type=F32)

        gates = jnp.dot(h_s[pl.ds(r0, RB), :], win_ref[:, GATE_COL:IN_W], preferred_element_type=F32)
        merged = _sigmoid(gates[:, 0:D_MODEL]) * a + _sigmoid(gates[:, D_MODEL:2 * D_MODEL]) * b
        u = jnp.dot(merged.astype(BF16), wo_ref[...], preferred_element_type=F32)
        xm = x_ref[pl.ds(r0, RB), :] + gt1 * u
        xmid_ref[...] = xm

        h2 = _rms(xm) * gain2 + sh2
        hi = h2.astype(BF16)
        lo = (h2 - hi.astype(F32)).astype(BF16)
        l1 = jnp.dot(hi, wr_ref[...], preferred_element_type=F32)
        l2 = jnp.dot(lo, wr_ref[:, 0:LANES], preferred_element_type=F32)
        gate, bucket = _route(l1[:, 0:LANES] + l1[:, LANES:2 * LANES] + l2)
        _store_tiles(h2_ref, _pack_bf16_pairs(h2))
        gate_ref[...] = gate
        lane = lax.broadcasted_iota(I32, (RB, LANES), 1).astype(F32)
        oh_ref[...] = jnp.where(lane == bucket, 1.0, 0.0).astype(BF16)

    mix(pl.program_id(1))


def _mix(x2d, mod, mod_row, cache, rope_tabs, weights, *, S, L, emit_kv, cast=()):
    T = x2d.shape[0]
    TM = S * L
    P = cache[0].shape[1] if cache is not None else 0
    use_rope = rope_tabs is not None
    assert T % TM == 0 and L % ROW_BLOCK == 0
    assert not (use_rope or P) or S == 1
    Lk = P + L

    args = [x2d, mod]
    nrb = TM // ROW_BLOCK
    in_specs = [
        pl.BlockSpec((TM, D_MODEL), lambda i, j: (i, 0)),
        pl.BlockSpec((1, 6, D_MODEL), lambda i, j: (mod_row(i), 0, 0)),
    ]
    if P:
        args += list(cache)
        in_specs += [pl.BlockSpec((1, P, KV_W), lambda i, j: (i, 0, 0))] * 2
    if use_rope:
        args += list(rope_tabs)
        in_specs += [_resident((L, HEAD_DIM))] * 3
    args += list(weights)
    in_specs += [_resident(w.shape) for w in weights]
    n_steps = T // TM
    cast_specs = []
    for w in cast:
        assert w.shape[0] % n_steps == 0
        blk = (w.shape[0] // n_steps,) + w.shape[1:]
        cast_specs.append(pl.BlockSpec(blk, lambda i, j, n=len(blk): (i,) + (0,) * (n - 1)))
    args += list(cast)
    in_specs += cast_specs

    out_shape = [jax.ShapeDtypeStruct((T, D_MODEL), F32), jax.ShapeDtypeStruct(_tiles_shape(T, PACKED_CHUNKS), U32),
                 jax.ShapeDtypeStruct((T, LANES), F32),
                 jax.ShapeDtypeStruct((T, LANES), BF16)]
    out_specs = [pl.BlockSpec((ROW_BLOCK, D_MODEL), lambda i, j: (i * nrb + j, 0)),
                 _tiles_spec(ROW_BLOCK, lambda i, j: i * nrb + j, PACKED_CHUNKS),
                 pl.BlockSpec((ROW_BLOCK, LANES), lambda i, j: (i * nrb + j, 0)),
                 pl.BlockSpec((ROW_BLOCK, LANES), lambda i, j: (i * nrb + j, 0))]
    if emit_kv:
        out_shape += [jax.ShapeDtypeStruct((T * N_KV_HEADS, HEAD_DIM), F32)] * 2
        out_specs += [pl.BlockSpec((TM * N_KV_HEADS, HEAD_DIM), lambda i, j: (i, 0))] * 2
    out_shape += [jax.ShapeDtypeStruct(w.shape, BF16) for w in cast]
    out_specs += cast_specs

    scratch = [
        pltpu.VMEM((N_HEADS, TM, HEAD_DIM), BF16),
        pltpu.VMEM((S, Lk, KV_W), BF16),
        pltpu.VMEM((S, Lk, KV_W), BF16),
        pltpu.VMEM((S, L + 2 * POOL_HALO, POOL_W), F32),
        pltpu.VMEM((TM, D_MODEL), BF16),
        pltpu.VMEM((ROW_BLOCK, ATTN_W), BF16),
    ]
    kern = functools.partial(_mix_kernel, S=S, L=L, P=P, use_rope=use_rope, emit_kv=emit_kv, n_cast=len(cast))
    return pl.pallas_call(
        kern,
        grid=(T // TM, nrb),
        in_specs=in_specs,
        out_specs=out_specs,
        out_shape=out_shape,
        scratch_shapes=scratch,
        compiler_params=pltpu.CompilerParams(
            dimension_semantics=("arbitrary", "arbitrary"), vmem_limit_bytes=V7X_VMEM_LIMIT_BYTES),
        name="mixer_rope" if use_rope else "mixer_ctx",
    )(*args)


def _plan_kernel(oh_ref, dest_ref, meta_ref, *, n_blocks):
    TB = TOKEN_BLOCK
    lane = lax.broadcasted_iota(I32, (SUBLANES, LANES), 1)

    def count(b, acc):
        oh = oh_ref[pl.ds(pl.multiple_of(b * TB, TB), TB), :].astype(F32)
        return acc + jnp.sum(oh, axis=0, keepdims=True)

    counts = lax.fori_loop(0, n_blocks, count, jnp.zeros((SUBLANES, LANES), F32))
    padded = jnp.ceil(counts * (1.0 / SORT_TILE)) * SORT_TILE
    ends = padded
    step = 1
    while step < LANES:
        ends = ends + jnp.where(lane >= step, pltpu.roll(ends, step, 1), 0.0)
        step *= 2
    starts = ends - padded

    tri = jnp.where(lax.broadcasted_iota(I32, (TB, TB), 1) < lax.broadcasted_iota(I32, (TB, TB), 0),
                    1.0, 0.0).astype(BF16)

    def place(b, seen):
        oh = oh_ref[pl.ds(pl.multiple_of(b * TB, TB), TB), :]
        ohf = oh.astype(F32)
        rank = jnp.dot(tri, oh, preferred_element_type=F32)
        base = (starts + seen)[0:1, :]
        d = jnp.sum(ohf * (rank + base), axis=1, keepdims=True)
        dest_ref[b] = _row(d).astype(I32)
        return seen + jnp.sum(ohf, axis=0, keepdims=True)

    lax.fori_loop(0, n_blocks, place, jnp.zeros((SUBLANES, LANES), F32))

    tile_row0 = lax.broadcasted_iota(I32, (LANES, LANES), 0).astype(F32) * SORT_TILE
    is_bucket = lax.broadcasted_iota(I32, (LANES, LANES), 1) < N_BUCKETS
    done = jnp.sum(jnp.where(is_bucket, jnp.where(ends[0:1, :] <= tile_row0, 1.0, 0.0), 0.0),
                   axis=1, keepdims=True)
    bkt = jnp.minimum(done, N_BUCKETS - 1.0)
    grp = (jnp.where(bkt >= PAIRS_PER_GROUP, 1.0, 0.0) + jnp.where(bkt >= 2 * PAIRS_PER_GROUP, 1.0, 0.0)
           + jnp.where(bkt >= 3 * PAIRS_PER_GROUP, 1.0, 0.0))
    pair = bkt - PAIRS_PER_GROUP * grp
    a = jnp.where(pair >= 3.0, 1.0, 0.0) + jnp.where(pair >= 5.0, 1.0, 0.0)
    b = pair - a * (7.0 - a) * 0.5 + a + 1.0
    e1 = EXP_PER_GROUP * grp + a
    e2 = EXP_PER_GROUP * grp + b
    meta = jnp.concatenate(
        [_row(e1), _row(e2), ends[0:1, :] * (1.0 / SORT_TILE), jnp.zeros((SUBLANES - 3, LANES), F32)], axis=0)
    meta_ref[...] = meta.astype(I32)


def _plan(onehot):
    T = onehot.shape[0]
    n_blocks = T // TOKEN_BLOCK
    dest, meta = pl.pallas_call(
        functools.partial(_plan_kernel, n_blocks=n_blocks),
        out_shape=[jax.ShapeDtypeStruct((n_blocks, 1, TOKEN_BLOCK), I32),
                   jax.ShapeDtypeStruct((SUBLANES, LANES), I32)],
        name="moe_plan",
    )(onehot)
    return dest.reshape(T), meta


def _sc_move_rows(src_v, table_hbm, out_hbm, lo, n_rows, idx_v, pieces_v, sem):
    chunks = pieces_v.shape[0] // SC_ROWS_PER_STEP
    lane = lax.iota(I32, SC_LANES)
    row_in_group = lane & (SUBLANES - 1)
    chunk_in_pair = lane >> 3
    rows_per_gather = SC_PIECES_PER_GATHER // chunks

    @pl.loop(0, n_rows // SC_ROWS_PER_STEP)
    def _(step):
        copies = []
        for g in range(SC_ROWS_PER_STEP // rows_per_gather):
            r0 = step * SC_ROWS_PER_STEP + g * rows_per_gather
            for v in range(SC_PIECES_PER_GATHER // SC_LANES):
                group, chunk0 = v // (chunks // 2), 2 * (v % (chunks // 2))
                tok = plsc.load_gather(src_v, [r0 + group * SUBLANES + row_in_group])
                piece = (tok >> 3) * (SUBLANES * chunks) + (chunk0 + chunk_in_pair) * SUBLANES + (tok & 7)
                idx_v[pl.ds(g * SC_PIECES_PER_GATHER + v * SC_LANES, SC_LANES)] = piece
            window = pl.ds(g * SC_PIECES_PER_GATHER, SC_PIECES_PER_GATHER)
            copies.append(pltpu.async_copy(table_hbm.at[idx_v.at[window]], pieces_v.at[window], sem))
        for cp in copies:
            cp.wait()
        first = pl.multiple_of((lo + step * SC_ROWS_PER_STEP) * chunks, SC_ROWS_PER_STEP * chunks)
        pltpu.sync_copy(pieces_v, out_hbm.at[pl.ds(first, SC_ROWS_PER_STEP * chunks)])


def _sc_scratch(chunks, dtype):
    return [pltpu.VMEM((SC_ROWS_PER_STEP * chunks,), I32), pltpu.VMEM((SC_ROWS_PER_STEP * chunks, LANES), dtype)]


def _sc_dispatch(h2_flat, gate_rows, dest, n_rows):
    T = dest.shape[0]
    per_worker = n_rows // SC_WORKERS
    rows_per_step = SC_ROWS_PER_STEP
    chunks = h2_flat.shape[0] // T
    assert n_rows % SC_WORKERS == 0 and per_worker % rows_per_step == 0 and T % SC_LANES == 0
    mesh = plsc.VectorSubcoreMesh(core_axis_name="c", subcore_axis_name="s")

    @functools.partial(
        pl.kernel, mesh=mesh,
        out_type=[jax.ShapeDtypeStruct((n_rows * chunks, LANES), h2_flat.dtype),
                  jax.ShapeDtypeStruct((n_rows, LANES), F32)],
        scratch_types=[pltpu.VMEM((T,), I32), pltpu.VMEM((per_worker,), I32)]
        + _sc_scratch(chunks, h2_flat.dtype)
        + [pltpu.VMEM((rows_per_step, LANES), F32), pltpu.SemaphoreType.DMA, pltpu.SemaphoreType.DMA],
        compiler_params=pltpu.CompilerParams(use_tc_tiling_on_sc=True, needs_layout_passes=False),
        name="sc_dispatch",
    )
    def dispatch(h2_hbm, gate_hbm, dest_hbm, out_h_hbm, out_g_hbm,
                 dest_v, src_v, idx_v, pieces_v, gates_v, sem_h, sem_g):
        worker = lax.axis_index("s") * SC_CORES + lax.axis_index("c")
        lo = worker * per_worker
        pltpu.sync_copy(dest_hbm, dest_v)

        @pl.loop(0, per_worker // SC_LANES)
        def _(j):
            j0 = pl.multiple_of(j * SC_LANES, SC_LANES)
            src_v[pl.ds(j0, SC_LANES)] = lax.rem(lo + j0 + lax.iota(I32, SC_LANES), T)

        @pl.loop(0, T // SC_LANES)
        def _(j):
            t0 = pl.multiple_of(j * SC_LANES, SC_LANES)
            d = dest_v[pl.ds(t0, SC_LANES)] - lo
            mine = (d >= 0) & (d < per_worker)
            plsc.store_scatter(src_v, [jnp.where(mine, d, 0)], t0 + lax.iota(I32, SC_LANES), mask=mine)

        @pl.loop(0, per_worker // rows_per_step)
        def _(j):
            off = pl.multiple_of(j * rows_per_step, rows_per_step)
            pltpu.async_copy(gate_hbm.at[src_v.at[pl.ds(off, rows_per_step)]], gates_v, sem_g).wait()
            pltpu.sync_copy(gates_v, out_g_hbm.at[pl.ds(lo + off, rows_per_step)])

        _sc_move_rows(src_v, h2_hbm, out_h_hbm, lo, per_worker, idx_v, pieces_v, sem_h)

    return dispatch(h2_flat, gate_rows, dest)


def _expert_kernel(e1s, e2s, n_used, x_ref, gv_ref, wg_ref, wu_ref, wd_ref, o_ref, wup_s, wdn_s):
    groups = SORT_TILE // SUBLANES

    def one_tile(k, carry):
        t = pl.program_id(0) * EXPERT_TILES_PER_STEP + k
        e1 = e1s[t]
        e2 = e2s[t]
        prev = jnp.maximum(t - 1, 0)
        new_pair = (t == 0) | (e1 != e1s[prev]) | (e2 != e2s[prev])
        rows = pl.ds(pl.multiple_of(k * groups, groups), groups)

        @pl.when((t < n_used[0]) & new_pair)
        def _():
            for slot, e in enumerate((e1, e2)):
                wup_s[:, (2 * slot) * D_EXPERT:(2 * slot + 1) * D_EXPERT] = wg_ref[e]
                wup_s[:, (2 * slot + 1) * D_EXPERT:(2 * slot + 2) * D_EXPERT] = wu_ref[e]
                wdn_s[slot * D_EXPERT:(slot + 1) * D_EXPERT, :] = wd_ref[e]

        @pl.when(t < n_used[0])
        def _():
            x = _unpack_bf16_pairs(_load_tiles(x_ref.at[rows]))
            gv = gv_ref[pl.ds(pl.multiple_of(k * SORT_TILE, SORT_TILE), SORT_TILE), :]
            lane = lax.broadcasted_iota(I32, gv.shape, 1)
            h = jnp.dot(x, wup_s[...], preferred_element_type=F32)
            hid = []
            for slot, e in enumerate((e1, e2)):
                ge = jnp.sum(jnp.where(lane == EXPERT_LANE0 + e, gv, 0.0), axis=-1, keepdims=True)
                hg = h[:, (2 * slot) * D_EXPERT:(2 * slot + 1) * D_EXPERT]
                hu = h[:, (2 * slot + 1) * D_EXPERT:(2 * slot + 2) * D_EXPERT]
                hid.append((hg * _sigmoid(hg) * hu * ge).astype(BF16))
            out = jnp.dot(jnp.concatenate(hid, axis=1), wdn_s[...], preferred_element_type=F32)
            _store_tiles(o_ref.at[rows], out)

        @pl.when(t >= n_used[0])
        def _():
            o_ref[rows] = jnp.zeros((groups,) + o_ref.shape[1:], F32)

        return carry

    lax.fori_loop(0, EXPERT_TILES_PER_STEP, one_tile, 0)


def _experts(sorted_h2, sorted_gates, meta, wg, wu, wd):
    n_tiles = sorted_h2.shape[0] * SUBLANES // SORT_TILE
    step_rows = SORT_TILE * EXPERT_TILES_PER_STEP
    assert n_tiles % EXPERT_TILES_PER_STEP == 0

    def last_used(i, e1, e2, nu):
        return jnp.minimum(i, (nu[0] - 1) // EXPERT_TILES_PER_STEP)

    return pl.pallas_call(
        _expert_kernel,
        grid_spec=pltpu.PrefetchScalarGridSpec(
            num_scalar_prefetch=3,
            grid=(n_tiles // EXPERT_TILES_PER_STEP,),
            in_specs=[
                _tiles_spec(step_rows, last_used, PACKED_CHUNKS),
                pl.BlockSpec((step_rows, LANES), lambda *a: (last_used(*a), 0)),
                _resident(wg.shape), _resident(wu.shape), _resident(wd.shape),
            ],
            out_specs=_tiles_spec(step_rows, lambda i, *_: i),
            scratch_shapes=[pltpu.VMEM((D_MODEL, 4 * D_EXPERT), BF16), pltpu.VMEM((2 * D_EXPERT, D_MODEL), BF16)],
        ),
        out_shape=jax.ShapeDtypeStruct(_tiles_shape(n_tiles * SORT_TILE), F32),
        compiler_params=pltpu.CompilerParams(
            dimension_semantics=("arbitrary",), vmem_limit_bytes=V7X_VMEM_LIMIT_BYTES),
        name="moe_experts",
    )(meta[0, :n_tiles], meta[1, :n_tiles], meta[2, LANES - 1:LANES], sorted_h2, sorted_gates, wg, wu, wd)


def _sc_row_gather(table_flat, idx):
    n = idx.shape[0]
    per_worker = n // SC_WORKERS
    chunks = ROW_CHUNKS
    assert n % SC_WORKERS == 0 and per_worker % SC_ROWS_PER_STEP == 0
    mesh = plsc.VectorSubcoreMesh(core_axis_name="c", subcore_axis_name="s")

    @functools.partial(
        pl.kernel, mesh=mesh,
        out_type=jax.ShapeDtypeStruct((n * chunks, LANES), table_flat.dtype),
        scratch_types=[pltpu.VMEM((per_worker,), I32)] + _sc_scratch(chunks, table_flat.dtype)
        + [pltpu.SemaphoreType.DMA],
        compiler_params=pltpu.CompilerParams(use_tc_tiling_on_sc=True, needs_layout_passes=False),
        name="sc_row_gather",
    )
    def gather(table_hbm, idx_hbm, out_hbm, src_v, idx_v, pieces_v, sem):
        worker = lax.axis_index("s") * SC_CORES + lax.axis_index("c")
        lo = worker * per_worker
        pltpu.sync_copy(idx_hbm.at[pl.ds(lo, per_worker)], src_v)
        _sc_move_rows(src_v, table_hbm, out_hbm, lo, per_worker, idx_v, pieces_v, sem)

    return gather(table_flat, idx)


def _final_kernel(x_ref, moe_ref, mod_ref, gf_ref, o_ref):
    y = x_ref[...] + mod_ref[0, 5:6, :] * _load_tiles(moe_ref)
    o_ref[...] = _rms(y) * gf_ref[...]


def _final(xmid, moe_rows, mod, mod_row, gf):
    T = xmid.shape[0]
    return pl.pallas_call(
        _final_kernel,
        grid=(T // TOKEN_BLOCK,),
        in_specs=[
            pl.BlockSpec((TOKEN_BLOCK, D_MODEL), lambda i: (i, 0)),
            _tiles_spec(TOKEN_BLOCK, lambda i: i),
            pl.BlockSpec((1, 6, D_MODEL), lambda i: (mod_row(i), 0, 0)),
            pl.BlockSpec((1, D_MODEL), lambda i: (0, 0)),
        ],
        out_specs=pl.BlockSpec((TOKEN_BLOCK, D_MODEL), lambda i: (i, 0)),
        out_shape=jax.ShapeDtypeStruct((T, D_MODEL), F32),
        compiler_params=pltpu.CompilerParams(dimension_semantics=("arbitrary",)),
        name="moe_final",
    )(xmid, moe_rows, mod, gf)


def _flat(tiles):
    return tiles.reshape(-1, LANES)


def _moe_dispatch(h2_tiles, gate_rows, onehot):
    T = gate_rows.shape[0]
    n_tiles = T // SORT_TILE + N_BUCKETS
    n_rows = n_tiles * SORT_TILE
    assert n_tiles <= LANES and T % TOKEN_BLOCK == 0
    dest, meta = _plan(onehot)
    sorted_h2, sorted_gates = _sc_dispatch(_flat(h2_tiles), gate_rows, dest, n_rows)
    return sorted_h2.reshape(_tiles_shape(n_rows, PACKED_CHUNKS)), sorted_gates, dest, meta


def _moe_unpermute(moe_sorted_tiles, dest):
    return _sc_row_gather(_flat(moe_sorted_tiles), dest).reshape(_tiles_shape(dest.shape[0]))


def _rope_tables(n_tokens):
    t = jnp.arange(n_tokens)
    row = (t // GRID_W).astype(F32)
    col = (t % GRID_W).astype(F32)
    freq = ROPE_THETA ** (-jnp.arange(ROPE_NF, dtype=F32) / ROPE_NF)
    ang = jnp.concatenate([row[:, None] * freq] * 2 + [col[:, None] * freq] * 2, axis=-1)
    first = (jnp.arange(HEAD_DIM) % (2 * ROPE_NF)) < ROPE_NF
    sin = jnp.sin(ang)
    return jnp.cos(ang), jnp.where(first, -sin, 0.0), jnp.where(first, 0.0, sin)


def kernel(x_prompt, x_sample, cache_k, cache_v, c, c_ctx, norm1_g, norm2_g, w_ada, b_ada, w_in, q_norm_g, k_norm_g, w_pool, pool_scale, w_branch_a, w_branch_b, w_out, w_router_group, w_router_expert, w_exp_gate, w_exp_up, w_exp_down, final_norm_g):
    assert norm1_g.shape[0] == 1, "single-layer trunk"
    B, L_ctx, _ = x_prompt.shape
    Bs, L_lat, _ = x_sample.shape
    P = cache_k.shape[2]
    assert 1 + Bs <= COND_ROWS

    cond = jnp.zeros((COND_ROWS, D_MODEL), F32).at[0].set(c_ctx).at[1:1 + Bs].set(c)
    mod = _ada(cond, w_ada[0], b_ada[0][None, :]).reshape(COND_ROWS, 6, D_MODEL)

    wpool_bd = jax.scipy.linalg.block_diag(*[w_pool[0, g] for g in range(len(POOL_WINDOWS))])
    wr = jnp.zeros((D_MODEL, LANES), F32)
    wr = wr.at[:, 0:N_EXP_GROUPS].set(w_router_group[0])
    wr = wr.at[:, EXPERT_LANE0:EXPERT_LANE0 + N_EXPERTS].set(w_router_expert[0])
    wr_hi = wr.astype(BF16)
    wr_lo = (wr - wr_hi.astype(F32)).astype(BF16)
    mix_w = (norm1_g[0][None, :], w_in[0].astype(BF16), q_norm_g[0][None, :], k_norm_g[0][None, :],
             wpool_bd.astype(BF16), pool_scale[0][None, :], w_branch_a[0].astype(BF16),
             w_branch_b[0].astype(BF16), w_out[0].astype(BF16),
             norm2_g[0][None, :], jnp.concatenate([wr_hi, wr_lo], axis=1))
    gf = final_norm_g[None, :]

    xp2 = x_prompt.reshape(B * L_ctx, D_MODEL)
    xmid_p, h2_p, gate_p, oh_p, knew, vnew, wg, wu, wd = _mix(
        xp2, mod, lambda i: 0, None, None, mix_w, S=2, L=L_ctx, emit_kv=True,
        cast=(w_exp_gate[0], w_exp_up[0], w_exp_down[0]))
    sh_p, sg_p, dest_p, meta_p = _moe_dispatch(h2_p, gate_p, oh_p)

    xs2 = x_sample.reshape(Bs * L_lat, D_MODEL)
    cache = (cache_k[:, 0].reshape(Bs, P, KV_W), cache_v[:, 0].reshape(Bs, P, KV_W))
    xmid_s, h2_s, gate_s, oh_s = _mix(xs2, mod, lambda i: 1 + i, cache, _rope_tables(L_lat), mix_w,
                                      S=1, L=L_lat, emit_kv=False)
    sh_s, sg_s, dest_s, meta_s = _moe_dispatch(h2_s, gate_s, oh_s)

    moe_p = _moe_unpermute(_experts(sh_p, sg_p, meta_p, wg, wu, wd), dest_p)
    moe_s = _moe_unpermute(_experts(sh_s, sg_s, meta_s, wg, wu, wd), dest_s)
    y_prompt = _final(xmid_p, moe_p, mod, lambda i: 0, gf)
    blocks_per_seq = L_lat // TOKEN_BLOCK
    y_sample = _final(xmid_s, moe_s, mod, lambda i: 1 + i // blocks_per_seq, gf)

    return (y_prompt.reshape(B, L_ctx, D_MODEL), y_sample.reshape(Bs, L_lat, D_MODEL),
            knew.reshape(B, 1, L_ctx, N_KV_HEADS, HEAD_DIM), vnew.reshape(B, 1, L_ctx, N_KV_HEADS, HEAD_DIM))
```

```python
import functools

import jax
import jax.numpy as jnp
from jax import lax
from jax.experimental import pallas as pl
from jax.experimental.pallas import tpu as pltpu
from jax.experimental.pallas import tpu_sc as plsc

F32 = jnp.float32
BF16 = jnp.bfloat16
I32 = jnp.int32
U32 = jnp.uint32

D_MODEL = 1024
HEAD_DIM = 128
N_HEADS = 8
N_KV_HEADS = 2
GROUP = N_HEADS // N_KV_HEADS
ATTN_W = N_HEADS * HEAD_DIM
KV_W = N_KV_HEADS * HEAD_DIM
POOL_WINDOWS = (2, 4, 8, 16)
POOL_GC = 128
POOL_W = POOL_GC * len(POOL_WINDOWS)
IN_W = ATTN_W + 2 * KV_W + POOL_W + 2 * D_MODEL
GATE_COL = ATTN_W + 2 * KV_W + POOL_W
GRID_W = 64
ROPE_THETA = 10000.0
ROPE_NF = HEAD_DIM // 4
N_EXP_GROUPS = 4
EXP_PER_GROUP = 4
N_EXPERTS = 16
D_EXPERT = 256
EPS = 1e-6

LANES = 128
SUBLANES = 8
COND_ROWS = SUBLANES
POOL_HALO = 8
ROW_BLOCK = 256
ADA_COLS = 768
EXPERT_LANE0 = N_EXP_GROUPS
PAIRS_PER_GROUP = EXP_PER_GROUP * (EXP_PER_GROUP - 1) // 2
N_BUCKETS = N_EXP_GROUPS * PAIRS_PER_GROUP
SORT_TILE = 256
EXPERT_TILES_PER_STEP = 4
TOKEN_BLOCK = 512
ROW_CHUNKS = D_MODEL // LANES
SC_CORES = 2
SC_SUBCORES = 16
SC_WORKERS = SC_CORES * SC_SUBCORES
SC_LANES = 16
SC_PIECES_PER_GATHER = 128
SC_ROWS_PER_STEP = 64
PACKED_CHUNKS = ROW_CHUNKS // 2
V7X_VMEM_LIMIT_BYTES = 56 * 1024 * 1024


def _sigmoid(x):
    return 1.0 / (1.0 + jnp.exp(-x))


def _rms(x):
    return x * lax.rsqrt(jnp.mean(x * x, axis=-1, keepdims=True) + EPS)


def _resident(shape):
    zeros = (0,) * len(shape)
    return pl.BlockSpec(shape, lambda i, *_: zeros, pipeline_mode=pl.Buffered(1))


def _tiles_shape(n, chunks=ROW_CHUNKS):
    return (n // SUBLANES, chunks, SUBLANES, LANES)


def _tiles_spec(n, block_index, chunks=ROW_CHUNKS):
    return pl.BlockSpec(_tiles_shape(n, chunks), lambda *a: (block_index(*a), 0, 0, 0))


def _store_tiles(ref, x):
    for c in range(ref.shape[1]):
        ref[:, c, :, :] = x[:, c * LANES:(c + 1) * LANES].reshape(x.shape[0] // SUBLANES, SUBLANES, LANES)


def _load_tiles(ref):
    n = ref.shape[0] * SUBLANES
    return jnp.concatenate([ref[:, c, :, :].reshape(n, LANES) for c in range(ref.shape[1])], axis=1)


def _pack_bf16_pairs(x):
    bits = pltpu.bitcast(x.astype(BF16).astype(F32), U32)
    w = x.shape[1] // 2
    return bits[:, :w] | (bits[:, w:] >> 16)


def _unpack_bf16_pairs(words):
    hi = pltpu.bitcast(words & jnp.uint32(0xFFFF0000), F32).astype(BF16)
    lo = pltpu.bitcast(words << 16, F32).astype(BF16)
    return jnp.concatenate([hi, lo], axis=1)


def _row(x):
    return jnp.transpose(jnp.broadcast_to(x, (x.shape[0], LANES)))[0:1, :]


def _ada_kernel(c_ref, w_ref, b_ref, o_ref):
    c = c_ref[...]
    s = (c * _sigmoid(c)).astype(BF16)
    o_ref[...] = jnp.dot(s, w_ref[...].astype(BF16), preferred_element_type=F32) + b_ref[...]


def _ada(cond, w_ada, b_ada):
    n = w_ada.shape[1]
    return pl.pallas_call(
        _ada_kernel,
        grid=(n // ADA_COLS,),
        in_specs=[
            pl.BlockSpec((COND_ROWS, D_MODEL), lambda j: (0, 0)),
            pl.BlockSpec((D_MODEL, ADA_COLS), lambda j: (0, j)),
            pl.BlockSpec((1, ADA_COLS), lambda j: (0, j)),
        ],
        out_specs=pl.BlockSpec((COND_ROWS, ADA_COLS), lambda j: (0, j)),
        out_shape=jax.ShapeDtypeStruct((COND_ROWS, n), F32),
        name="ada_mod",
    )(cond, w_ada, b_ada)


def _route(logits):
    lane = lax.broadcasted_iota(I32, logits.shape, 1).astype(F32)
    neg = jnp.float32(-1e30)
    far = jnp.float32(LANES)
    is_g = lane < N_EXP_GROUPS
    gl = jnp.where(is_g, logits, neg)
    gmax = jnp.max(gl, axis=-1, keepdims=True)
    gsel = jnp.min(jnp.where(gl == gmax, lane, far), axis=-1, keepdims=True)
    psel = 1.0 / jnp.sum(jnp.where(is_g, jnp.exp(gl - gmax), 0.0), axis=-1, keepdims=True)
    e_lo = EXPERT_LANE0 + EXP_PER_GROUP * gsel
    el = jnp.where(lane >= e_lo, jnp.where(lane < e_lo + EXP_PER_GROUP, logits, neg), neg)
    v1 = jnp.max(el, axis=-1, keepdims=True)
    i1 = jnp.min(jnp.where(el == v1, lane, far), axis=-1, keepdims=True)
    el2 = jnp.where(lane == i1, neg, el)
    v2 = jnp.max(el2, axis=-1, keepdims=True)
    i2 = jnp.min(jnp.where(el2 == v2, jnp.where(lane == i1, far, lane), far), axis=-1, keepdims=True)
    e2 = jnp.exp(v2 - v1)
    w1 = psel / (1.0 + e2)
    w2 = psel * e2 / (1.0 + e2)
    gate = jnp.where(lane == i1, w1, jnp.where(lane == i2, w2, 0.0))
    a = jnp.minimum(i1, i2) - e_lo
    b = jnp.maximum(i1, i2) - e_lo
    pair = a * (7.0 - a) * 0.5 + (b - a - 1.0)
    return gate, gsel * PAIRS_PER_GROUP + pair


def _mix_kernel(*refs, S, L, P, use_rope, emit_kv, n_cast, n_blocks):
    it = iter(refs)
    x_ref = next(it)
    mod_ref = next(it)
    if P:
        ck_ref = next(it)
        cv_ref = next(it)
    if use_rope:
        cos_ref = next(it)
        sneg_ref = next(it)
        spos_ref = next(it)
    (g1_ref, win_ref, qg_ref, kg_ref, wpool_ref, pscale_ref, wa_ref, wb_ref, wo_ref,
     g2_ref, wr_ref) = (next(it) for _ in range(11))
    cast_in = [next(it) for _ in range(n_cast)]
    xmid_ref = next(it)
    h2_ref = next(it)
    gate_ref = next(it)
    oh_ref = next(it)
    if emit_kv:
        knew_ref = next(it)
        vnew_ref = next(it)
    cast_out = [next(it) for _ in range(n_cast)]
    q_s, k_s, v_s, xp_s, h_s, attn_s, xm_s, mod2_s = (next(it) for _ in range(8))

    TM = S * L
    RB = ROW_BLOCK
    nrb = TM // RB
    scale = HEAD_DIM ** -0.5
    step = pl.program_id(0)
    block = jnp.minimum(step, n_blocks - 1)
    slot = step % 2

    sh1 = mod_ref[0, 0:1, :]
    gain1 = g1_ref[...] * (1.0 + mod_ref[0, 1:2, :])
    gt1 = mod_ref[0, 2:3, :]
    sh2 = mod_ref[0, 3:4, :]
    gain2 = g2_ref[...] * (1.0 + mod_ref[0, 4:5, :])
    qg = qg_ref[...]
    kg = kg_ref[...]

    def project(r, carry):
        r0 = pl.multiple_of(r * RB, RB)
        s = r0 // L
        o = pl.multiple_of(r0 % L, RB)
        hb = (_rms(x_ref[pl.ds(r0, RB), :]) * gain1 + sh1).astype(BF16)
        h_s[pl.ds(r0, RB), :] = hb
        p1 = jnp.dot(hb, win_ref[:, 0:GATE_COL], preferred_element_type=F32)
        if use_rope:
            cs = cos_ref[pl.ds(o, RB), :]
            sn = sneg_ref[pl.ds(o, RB), :]
            sp = spos_ref[pl.ds(o, RB), :]

        def rope(t):
            return (t * cs + pltpu.roll(t, HEAD_DIM - ROPE_NF, 1) * sn + pltpu.roll(t, ROPE_NF, 1) * sp)

        for hd in range(N_HEADS):
            qh = _rms(p1[:, hd * HEAD_DIM:(hd + 1) * HEAD_DIM]) * qg
            if use_rope:
                qh = rope(qh)
            q_s[hd, pl.ds(r0, RB), :] = qh.astype(BF16)
        for kh in range(N_KV_HEADS):
            c0 = ATTN_W + kh * HEAD_DIM
            kk = _rms(p1[:, c0:c0 + HEAD_DIM]) * kg
            if emit_kv:
                knew_ref[pl.ds(N_KV_HEADS * r0 + kh, RB, stride=N_KV_HEADS), :] = kk
            if use_rope:
                kk = rope(kk)
            k_s[s, pl.ds(P + o, RB), kh * HEAD_DIM:(kh + 1) * HEAD_DIM] = kk.astype(BF16)
        vv = p1[:, ATTN_W + KV_W:ATTN_W + 2 * KV_W]
        if emit_kv:
            for kh in range(N_KV_HEADS):
                vnew_ref[pl.ds(N_KV_HEADS * r0 + kh, RB, stride=N_KV_HEADS), :] = (
                    vv[:, kh * HEAD_DIM:(kh + 1) * HEAD_DIM])
        v_s[s, pl.ds(P + o, RB), :] = vv.astype(BF16)
        xp_s[s, pl.ds(POOL_HALO + o, RB), :] = p1[:, ATTN_W + 2 * KV_W:GATE_COL]
        return carry

    @pl.when(step == 0)
    def _():
        xm_s[1] = jnp.zeros((RB, D_MODEL), F32)
        mod2_s[1] = jnp.zeros((2, D_MODEL), F32)

    @pl.when((step < n_blocks) & (step % nrb == 0))
    def _():
        if P:
            k_s[0, 0:P, :] = ck_ref[0].astype(BF16)
            v_s[0, 0:P, :] = cv_ref[0].astype(BF16)
        xp_s[:, 0:POOL_HALO, :] = jnp.zeros((S, POOL_HALO, POOL_W), F32)
        xp_s[:, L + POOL_HALO:L + 2 * POOL_HALO, :] = jnp.zeros((S, POOL_HALO, POOL_W), F32)
        lax.fori_loop(0, TM // RB, project, 0)
        for src, dst in zip(cast_in, cast_out):
            dst[...] = src[...].astype(BF16)

    def mix(r):
        r0 = pl.multiple_of(r * RB, RB)
        s = r0 // L
        o = pl.multiple_of(r0 % L, RB)

        for kh in range(N_KV_HEADS):
            k = k_s[s, :, kh * HEAD_DIM:(kh + 1) * HEAD_DIM]
            v = v_s[s, :, kh * HEAD_DIM:(kh + 1) * HEAD_DIM]
            q4 = q_s[kh * GROUP:(kh + 1) * GROUP, pl.ds(r0, RB), :].reshape(GROUP * RB, HEAD_DIM)
            sc = lax.dot_general(q4, k, (((1,), (1,)), ((), ())), preferred_element_type=F32) * scale
            e = jnp.exp(sc - jnp.max(sc, axis=-1, keepdims=True))
            den = jnp.sum(e, axis=-1, keepdims=True)
            o4 = jnp.dot(e.astype(BF16), v, preferred_element_type=F32) / den
            for g in range(GROUP):
                hd = kh * GROUP + g
                attn_s[:, hd * HEAD_DIM:(hd + 1) * HEAD_DIM] = o4[g * RB:(g + 1) * RB].astype(BF16)
        a = jnp.dot(attn_s[...], wa_ref[...], preferred_element_type=F32)

        t = o + lax.broadcasted_iota(I32, (RB, 1), 0)
        RW = RB + 2 * POOL_HALO
        parts = []
        for gi, w in enumerate(POOL_WINDOWS):
            cols = slice(gi * POOL_GC, (gi + 1) * POOL_GC)
            xw = xp_s[s, pl.ds(o, RW), cols]
            run = xw
            span = 1
            while span < w:
                run = run + pltpu.roll(run, span, 0)
                span *= 2
            if w // 2 > 1:
                run = pltpu.roll(run, RW - (w // 2 - 1), 0)
            tot = run[POOL_HALO:POOL_HALO + RB]
            cnt = (jnp.minimum(t + w // 2, L) - jnp.maximum(t - w // 2, 0)).astype(F32)
            parts.append(tot / cnt - xw[POOL_HALO:POOL_HALO + RB])
        dpool = jnp.concatenate(parts, axis=1).astype(BF16)
        pooled = jnp.dot(dpool, wpool_ref[...], preferred_element_type=F32) * pscale_ref[...]
        b = jnp.dot(pooled.astype(BF16), wb_ref[...], preferred_element_type=F32)

        gates = jnp.dot(h_s[pl.ds(r0, RB), :], win_ref[:, GATE_COL:IN_W], preferred_element_type=F32)
        merged = _sigmoid(gates[:, 0:D_MODEL]) * a + _sigmoid(gates[:, D_MODEL:2 * D_MODEL]) * b
        u = jnp.dot(merged.astype(BF16), wo_ref[...], preferred_element_type=F32)
        xm = x_ref[pl.ds(r0, RB), :] + gt1 * u
        xmid_ref[...] = xm
        xm_s[slot] = xm
        mod2_s[slot, 0:1, :] = gain2
        mod2_s[slot, 1:2, :] = sh2

    def moe_prep():
        h2 = _rms(xm_s[1 - slot]) * mod2_s[1 - slot, 0:1, :] + mod2_s[1 - slot, 1:2, :]
        hi = h2.astype(BF16)
        lo = (h2 - hi.astype(F32)).astype(BF16)
        l1 = jnp.dot(hi, wr_ref[...], preferred_element_type=F32)
        l2 = jnp.dot(lo, wr_ref[:, 0:LANES], preferred_element_type=F32)
        gate, bucket = _route(l1[:, 0:LANES] + l1[:, LANES:2 * LANES] + l2)
        _store_tiles(h2_ref, _pack_bf16_pairs(h2))
        gate_ref[...] = gate
        lane = lax.broadcasted_iota(I32, (RB, LANES), 1).astype(F32)
        oh_ref[...] = jnp.where(lane == bucket, 1.0, 0.0).astype(BF16)

    moe_prep()
    mix(block % nrb)


def _mix(x2d, mod, mod_row, cache, rope_tabs, weights, *, S, L, emit_kv, cast=()):
    T = x2d.shape[0]
    TM = S * L
    P = cache[0].shape[1] if cache is not None else 0
    use_rope = rope_tabs is not None
    assert T % TM == 0 and L % ROW_BLOCK == 0
    assert not (use_rope or P) or S == 1
    Lk = P + L

    args = [x2d, mod]
    nrb = TM // ROW_BLOCK
    n_blocks = T // ROW_BLOCK

    def mixed(s):
        return jnp.minimum(s, n_blocks - 1)

    def group(s):
        return mixed(s) // nrb

    def prepared(s):
        return jnp.maximum(s - 1, 0)

    in_specs = [
        pl.BlockSpec((TM, D_MODEL), lambda s: (group(s), 0)),
        pl.BlockSpec((1, 6, D_MODEL), lambda s: (mod_row(group(s)), 0, 0)),
    ]
    if P:
        args += list(cache)
        in_specs += [pl.BlockSpec((1, P, KV_W), lambda s: (group(s), 0, 0))] * 2
    if use_rope:
        args += list(rope_tabs)
        in_specs += [_resident((L, HEAD_DIM))] * 3
    args += list(weights)
    in_specs += [_resident(w.shape) for w in weights]
    n_steps = T // TM
    cast_specs = []
    for w in cast:
        assert w.shape[0] % n_steps == 0
        blk = (w.shape[0] // n_steps,) + w.shape[1:]
        cast_specs.append(pl.BlockSpec(blk, lambda s, n=len(blk): (group(s),) + (0,) * (n - 1)))
    args += list(cast)
    in_specs += cast_specs

    out_shape = [jax.ShapeDtypeStruct((T, D_MODEL), F32), jax.ShapeDtypeStruct(_tiles_shape(T, PACKED_CHUNKS), U32),
                 jax.ShapeDtypeStruct((T, LANES), F32),
                 jax.ShapeDtypeStruct((T, LANES), BF16)]
    out_specs = [pl.BlockSpec((ROW_BLOCK, D_MODEL), lambda s: (mixed(s), 0)),
                 _tiles_spec(ROW_BLOCK, prepared, PACKED_CHUNKS),
                 pl.BlockSpec((ROW_BLOCK, LANES), lambda s: (prepared(s), 0)),
                 pl.BlockSpec((ROW_BLOCK, LANES), lambda s: (prepared(s), 0))]
    if emit_kv:
        out_shape += [jax.ShapeDtypeStruct((T * N_KV_HEADS, HEAD_DIM), F32)] * 2
        out_specs += [pl.BlockSpec((TM * N_KV_HEADS, HEAD_DIM), lambda s: (group(s), 0))] * 2
    out_shape += [jax.ShapeDtypeStruct(w.shape, BF16) for w in cast]
    out_specs += cast_specs

    scratch = [
        pltpu.VMEM((N_HEADS, TM, HEAD_DIM), BF16),
        pltpu.VMEM((S, Lk, KV_W), BF16),
        pltpu.VMEM((S, Lk, KV_W), BF16),
        pltpu.VMEM((S, L + 2 * POOL_HALO, POOL_W), F32),
        pltpu.VMEM((TM, D_MODEL), BF16),
        pltpu.VMEM((ROW_BLOCK, ATTN_W), BF16),
        pltpu.VMEM((2, ROW_BLOCK, D_MODEL), F32),
        pltpu.VMEM((2, 2, D_MODEL), F32),
    ]
    kern = functools.partial(_mix_kernel, S=S, L=L, P=P, use_rope=use_rope, emit_kv=emit_kv,
                             n_cast=len(cast), n_blocks=n_blocks)
    return pl.pallas_call(
        kern,
        grid=(n_blocks + 1,),
        in_specs=in_specs,
        out_specs=out_specs,
        out_shape=out_shape,
        scratch_shapes=scratch,
        compiler_params=pltpu.CompilerParams(
            dimension_semantics=("arbitrary",), vmem_limit_bytes=V7X_VMEM_LIMIT_BYTES),
        name="mixer_rope" if use_rope else "mixer_ctx",
    )(*args)


def _plan_kernel(oh_ref, dest_ref, meta_ref, *, n_blocks):
    TB = TOKEN_BLOCK
    lane = lax.broadcasted_iota(I32, (SUBLANES, LANES), 1)

    def count(b, acc):
        oh = oh_ref[pl.ds(pl.multiple_of(b * TB, TB), TB), :].astype(F32)
        return acc + jnp.sum(oh, axis=0, keepdims=True)

    counts = lax.fori_loop(0, n_blocks, count, jnp.zeros((SUBLANES, LANES), F32))
    padded = jnp.ceil(counts * (1.0 / SORT_TILE)) * SORT_TILE
    ends = padded
    step = 1
    while step < LANES:
        ends = ends + jnp.where(lane >= step, pltpu.roll(ends, step, 1), 0.0)
        step *= 2
    starts = ends - padded

    tri = jnp.where(lax.broadcasted_iota(I32, (TB, TB), 1) < lax.broadcasted_iota(I32, (TB, TB), 0),
                    1.0, 0.0).astype(BF16)

    def place(b, seen):
        oh = oh_ref[pl.ds(pl.multiple_of(b * TB, TB), TB), :]
        ohf = oh.astype(F32)
        rank = jnp.dot(tri, oh, preferred_element_type=F32)
        base = (starts + seen)[0:1, :]
        d = jnp.sum(ohf * (rank + base), axis=1, keepdims=True)
        dest_ref[b] = _row(d).astype(I32)
        return seen + jnp.sum(ohf, axis=0, keepdims=True)

    lax.fori_loop(0, n_blocks, place, jnp.zeros((SUBLANES, LANES), F32))

    tile_row0 = lax.broadcasted_iota(I32, (LANES, LANES), 0).astype(F32) * SORT_TILE
    is_bucket = lax.broadcasted_iota(I32, (LANES, LANES), 1) < N_BUCKETS
    done = jnp.sum(jnp.where(is_bucket, jnp.where(ends[0:1, :] <= tile_row0, 1.0, 0.0), 0.0),
                   axis=1, keepdims=True)
    bkt = jnp.minimum(done, N_BUCKETS - 1.0)
    grp = (jnp.where(bkt >= PAIRS_PER_GROUP, 1.0, 0.0) + jnp.where(bkt >= 2 * PAIRS_PER_GROUP, 1.0, 0.0)
           + jnp.where(bkt >= 3 * PAIRS_PER_GROUP, 1.0, 0.0))
    pair = bkt - PAIRS_PER_GROUP * grp
    a = jnp.where(pair >= 3.0, 1.0, 0.0) + jnp.where(pair >= 5.0, 1.0, 0.0)
    b = pair - a * (7.0 - a) * 0.5 + a + 1.0
    e1 = EXP_PER_GROUP * grp + a
    e2 = EXP_PER_GROUP * grp + b
    meta = jnp.concatenate(
        [_row(e1), _row(e2), ends[0:1, :] * (1.0 / SORT_TILE), jnp.zeros((SUBLANES - 3, LANES), F32)], axis=0)
    meta_ref[...] = meta.astype(I32)


def _plan(onehot):
    T = onehot.shape[0]
    n_blocks = T // TOKEN_BLOCK
    dest, meta = pl.pallas_call(
        functools.partial(_plan_kernel, n_blocks=n_blocks),
        out_shape=[jax.ShapeDtypeStruct((n_blocks, 1, TOKEN_BLOCK), I32),
                   jax.ShapeDtypeStruct((SUBLANES, LANES), I32)],
        name="moe_plan",
    )(onehot)
    return dest.reshape(T), meta


def _sc_move_rows(src_v, table_hbm, out_hbm, lo, n_rows, idx_v, pieces_v, sem):
    chunks = pieces_v.shape[0] // SC_ROWS_PER_STEP
    lane = lax.iota(I32, SC_LANES)
    row_in_group = lane & (SUBLANES - 1)
    chunk_in_pair = lane >> 3
    rows_per_gather = SC_PIECES_PER_GATHER // chunks

    @pl.loop(0, n_rows // SC_ROWS_PER_STEP)
    def _(step):
        copies = []
        for g in range(SC_ROWS_PER_STEP // rows_per_gather):
            r0 = step * SC_ROWS_PER_STEP + g * rows_per_gather
            for v in range(SC_PIECES_PER_GATHER // SC_LANES):
                group, chunk0 = v // (chunks // 2), 2 * (v % (chunks // 2))
                tok = plsc.load_gather(src_v, [r0 + group * SUBLANES + row_in_group])
                piece = (tok >> 3) * (SUBLANES * chunks) + (chunk0 + chunk_in_pair) * SUBLANES + (tok & 7)
                idx_v[pl.ds(g * SC_PIECES_PER_GATHER + v * SC_LANES, SC_LANES)] = piece
            window = pl.ds(g * SC_PIECES_PER_GATHER, SC_PIECES_PER_GATHER)
            copies.append(pltpu.async_copy(table_hbm.at[idx_v.at[window]], pieces_v.at[window], sem))
        for cp in copies:
            cp.wait()
        first = pl.multiple_of((lo + step * SC_ROWS_PER_STEP) * chunks, SC_ROWS_PER_STEP * chunks)
        pltpu.sync_copy(pieces_v, out_hbm.at[pl.ds(first, SC_ROWS_PER_STEP * chunks)])


def _sc_scratch(chunks, dtype):
    return [pltpu.VMEM((SC_ROWS_PER_STEP * chunks,), I32), pltpu.VMEM((SC_ROWS_PER_STEP * chunks, LANES), dtype)]


def _sc_dispatch(h2_flat, gate_rows, dest, n_rows):
    T = dest.shape[0]
    per_worker = n_rows // SC_WORKERS
    rows_per_step = SC_ROWS_PER_STEP
    chunks = h2_flat.shape[0] // T
    assert n_rows % SC_WORKERS == 0 and per_worker % rows_per_step == 0 and T % SC_LANES == 0
    mesh = plsc.VectorSubcoreMesh(core_axis_name="c", subcore_axis_name="s")

    @functools.partial(
        pl.kernel, mesh=mesh,
        out_type=[jax.ShapeDtypeStruct((n_rows * chunks, LANES), h2_flat.dtype),
                  jax.ShapeDtypeStruct((n_rows, LANES), F32)],
        scratch_types=[pltpu.VMEM((T,), I32), pltpu.VMEM((per_worker,), I32)]
        + _sc_scratch(chunks, h2_flat.dtype)
        + [pltpu.VMEM((rows_per_step, LANES), F32), pltpu.SemaphoreType.DMA, pltpu.SemaphoreType.DMA],
        compiler_params=pltpu.CompilerParams(use_tc_tiling_on_sc=True, needs_layout_passes=False),
        name="sc_dispatch",
    )
    def dispatch(h2_hbm, gate_hbm, dest_hbm, out_h_hbm, out_g_hbm,
                 dest_v, src_v, idx_v, pieces_v, gates_v, sem_h, sem_g):
        worker = lax.axis_index("s") * SC_CORES + lax.axis_index("c")
        lo = worker * per_worker
        pltpu.sync_copy(dest_hbm, dest_v)

        @pl.loop(0, per_worker // SC_LANES)
        def _(j):
            j0 = pl.multiple_of(j * SC_LANES, SC_LANES)
            src_v[pl.ds(j0, SC_LANES)] = lax.rem(lo + j0 + lax.iota(I32, SC_LANES), T)

        @pl.loop(0, T // SC_LANES)
        def _(j):
            t0 = pl.multiple_of(j * SC_LANES, SC_LANES)
            d = dest_v[pl.ds(t0, SC_LANES)] - lo
            mine = (d >= 0) & (d < per_worker)
            plsc.store_scatter(src_v, [jnp.where(mine, d, 0)], t0 + lax.iota(I32, SC_LANES), mask=mine)

        @pl.loop(0, per_worker // rows_per_step)
        def _(j):
            off = pl.multiple_of(j * rows_per_step, rows_per_step)
            pltpu.async_copy(gate_hbm.at[src_v.at[pl.ds(off, rows_per_step)]], gates_v, sem_g).wait()
            pltpu.sync_copy(gates_v, out_g_hbm.at[pl.ds(lo + off, rows_per_step)])

        _sc_move_rows(src_v, h2_hbm, out_h_hbm, lo, per_worker, idx_v, pieces_v, sem_h)

    return dispatch(h2_flat, gate_rows, dest)


def _expert_kernel(e1s, e2s, n_used, x_ref, gv_ref, wg_ref, wu_ref, wd_ref, o_ref, wup_s, wdn_s):
    groups = SORT_TILE // SUBLANES

    def one_tile(k, carry):
        t = pl.program_id(0) * EXPERT_TILES_PER_STEP + k
        e1 = e1s[t]
        e2 = e2s[t]
        prev = jnp.maximum(t - 1, 0)
        new_pair = (t == 0) | (e1 != e1s[prev]) | (e2 != e2s[prev])
        rows = pl.ds(pl.multiple_of(k * groups, groups), groups)

        @pl.when((t < n_used[0]) & new_pair)
        def _():
            for slot, e in enumerate((e1, e2)):
                wup_s[:, (2 * slot) * D_EXPERT:(2 * slot + 1) * D_EXPERT] = wg_ref[e]
                wup_s[:, (2 * slot + 1) * D_EXPERT:(2 * slot + 2) * D_EXPERT] = wu_ref[e]
                wdn_s[slot * D_EXPERT:(slot + 1) * D_EXPERT, :] = wd_ref[e]

        @pl.when(t < n_used[0])
        def _():
            x = _unpack_bf16_pairs(_load_tiles(x_ref.at[rows]))
            gv = gv_ref[pl.ds(pl.multiple_of(k * SORT_TILE, SORT_TILE), SORT_TILE), :]
            lane = lax.broadcasted_iota(I32, gv.shape, 1)
            h = jnp.dot(x, wup_s[...], preferred_element_type=F32)
            hid = []
            for slot, e in enumerate((e1, e2)):
                ge = jnp.sum(jnp.where(lane == EXPERT_LANE0 + e, gv, 0.0), axis=-1, keepdims=True)
                hg = h[:, (2 * slot) * D_EXPERT:(2 * slot + 1) * D_EXPERT]
                hu = h[:, (2 * slot + 1) * D_EXPERT:(2 * slot + 2) * D_EXPERT]
                hid.append((hg * _sigmoid(hg) * hu * ge).astype(BF16))
            out = jnp.dot(jnp.concatenate(hid, axis=1), wdn_s[...], preferred_element_type=F32)
            _store_tiles(o_ref.at[rows], out)

        @pl.when(t >= n_used[0])
        def _():
            o_ref[rows] = jnp.zeros((groups,) + o_ref.shape[1:], F32)

        return carry

    lax.fori_loop(0, EXPERT_TILES_PER_STEP, one_tile, 0)


def _experts(sorted_h2, sorted_gates, meta, wg, wu, wd):
    n_tiles = sorted_h2.shape[0] * SUBLANES // SORT_TILE
    step_rows = SORT_TILE * EXPERT_TILES_PER_STEP
    assert n_tiles % EXPERT_TILES_PER_STEP == 0

    def last_used(i, e1, e2, nu):
        return jnp.minimum(i, (nu[0] - 1) // EXPERT_TILES_PER_STEP)

    return pl.pallas_call(
        _expert_kernel,
        grid_spec=pltpu.PrefetchScalarGridSpec(
            num_scalar_prefetch=3,
            grid=(n_tiles // EXPERT_TILES_PER_STEP,),
            in_specs=[
                _tiles_spec(step_rows, last_used, PACKED_CHUNKS),
                pl.BlockSpec((step_rows, LANES), lambda *a: (last_used(*a), 0)),
                _resident(wg.shape), _resident(wu.shape), _resident(wd.shape),
            ],
            out_specs=_tiles_spec(step_rows, lambda i, *_: i),
            scratch_shapes=[pltpu.VMEM((D_MODEL, 4 * D_EXPERT), BF16), pltpu.VMEM((2 * D_EXPERT, D_MODEL), BF16)],
        ),
        out_shape=jax.ShapeDtypeStruct(_tiles_shape(n_tiles * SORT_TILE), F32),
        compiler_params=pltpu.CompilerParams(
            dimension_semantics=("arbitrary",), vmem_limit_bytes=V7X_VMEM_LIMIT_BYTES),
        name="moe_experts",
    )(meta[0, :n_tiles], meta[1, :n_tiles], meta[2, LANES - 1:LANES], sorted_h2, sorted_gates, wg, wu, wd)


def _sc_row_gather(table_flat, idx):
    n = idx.shape[0]
    per_worker = n // SC_WORKERS
    chunks = ROW_CHUNKS
    assert n % SC_WORKERS == 0 and per_worker % SC_ROWS_PER_STEP == 0
    mesh = plsc.VectorSubcoreMesh(core_axis_name="c", subcore_axis_name="s")

    @functools.partial(
        pl.kernel, mesh=mesh,
        out_type=jax.ShapeDtypeStruct((n * chunks, LANES), table_flat.dtype),
        scratch_types=[pltpu.VMEM((per_worker,), I32)] + _sc_scratch(chunks, table_flat.dtype)
        + [pltpu.SemaphoreType.DMA],
        compiler_params=pltpu.CompilerParams(use_tc_tiling_on_sc=True, needs_layout_passes=False),
        name="sc_row_gather",
    )
    def gather(table_hbm, idx_hbm, out_hbm, src_v, idx_v, pieces_v, sem):
        worker = lax.axis_index("s") * SC_CORES + lax.axis_index("c")
        lo = worker * per_worker
        pltpu.sync_copy(idx_hbm.at[pl.ds(lo, per_worker)], src_v)
        _sc_move_rows(src_v, table_hbm, out_hbm, lo, per_worker, idx_v, pieces_v, sem)

    return gather(table_flat, idx)


def _final_kernel(x_ref, moe_ref, mod_ref, gf_ref, o_ref):
    y = x_ref[...] + mod_ref[0, 5:6, :] * _load_tiles(moe_ref)
    o_ref[...] = _rms(y) * gf_ref[...]


def _final(xmid, moe_rows, mod, mod_row, gf):
    T = xmid.shape[0]
    return pl.pallas_call(
        _final_kernel,
        grid=(T // TOKEN_BLOCK,),
        in_specs=[
            pl.BlockSpec((TOKEN_BLOCK, D_MODEL), lambda i: (i, 0)),
            _tiles_spec(TOKEN_BLOCK, lambda i: i),
            pl.BlockSpec((1, 6, D_MODEL), lambda i: (mod_row(i), 0, 0)),
            pl.BlockSpec((1, D_MODEL), lambda i: (0, 0)),
        ],
        out_specs=pl.BlockSpec((TOKEN_BLOCK, D_MODEL), lambda i: (i, 0)),
        out_shape=jax.ShapeDtypeStruct((T, D_MODEL), F32),
        compiler_params=pltpu.CompilerParams(dimension_semantics=("arbitrary",)),
        name="moe_final",
    )(xmid, moe_rows, mod, gf)


def _flat(tiles):
    return tiles.reshape(-1, LANES)


def _moe_dispatch(h2_tiles, gate_rows, onehot):
    T = gate_rows.shape[0]
    n_tiles = T // SORT_TILE + N_BUCKETS
    n_rows = n_tiles * SORT_TILE
    assert n_tiles <= LANES and T % TOKEN_BLOCK == 0
    dest, meta = _plan(onehot)
    sorted_h2, sorted_gates = _sc_dispatch(_flat(h2_tiles), gate_rows, dest, n_rows)
    return sorted_h2.reshape(_tiles_shape(n_rows, PACKED_CHUNKS)), sorted_gates, dest, meta


def _moe_unpermute(moe_sorted_tiles, dest):
    return _sc_row_gather(_flat(moe_sorted_tiles), dest).reshape(_tiles_shape(dest.shape[0]))


def _rope_tables(n_tokens):
    t = jnp.arange(n_tokens)
    row = (t // GRID_W).astype(F32)
    col = (t % GRID_W).astype(F32)
    freq = ROPE_THETA ** (-jnp.arange(ROPE_NF, dtype=F32) / ROPE_NF)
    ang = jnp.concatenate([row[:, None] * freq] * 2 + [col[:, None] * freq] * 2, axis=-1)
    first = (jnp.arange(HEAD_DIM) % (2 * ROPE_NF)) < ROPE_NF
    sin = jnp.sin(ang)
    return jnp.cos(ang), jnp.where(first, -sin, 0.0), jnp.where(first, 0.0, sin)


def kernel(x_prompt, x_sample, cache_k, cache_v, c, c_ctx, norm1_g, norm2_g, w_ada, b_ada, w_in, q_norm_g, k_norm_g, w_pool, pool_scale, w_branch_a, w_branch_b, w_out, w_router_group, w_router_expert, w_exp_gate, w_exp_up, w_exp_down, final_norm_g):
    assert norm1_g.shape[0] == 1, "single-layer trunk"
    B, L_ctx, _ = x_prompt.shape
    Bs, L_lat, _ = x_sample.shape
    P = cache_k.shape[2]
    assert 1 + Bs <= COND_ROWS

    cond = jnp.zeros((COND_ROWS, D_MODEL), F32).at[0].set(c_ctx).at[1:1 + Bs].set(c)
    mod = _ada(cond, w_ada[0], b_ada[0][None, :]).reshape(COND_ROWS, 6, D_MODEL)

    wpool_bd = jax.scipy.linalg.block_diag(*[w_pool[0, g] for g in range(len(POOL_WINDOWS))])
    wr = jnp.zeros((D_MODEL, LANES), F32)
    wr = wr.at[:, 0:N_EXP_GROUPS].set(w_router_group[0])
    wr = wr.at[:, EXPERT_LANE0:EXPERT_LANE0 + N_EXPERTS].set(w_router_expert[0])
    wr_hi = wr.astype(BF16)
    wr_lo = (wr - wr_hi.astype(F32)).astype(BF16)
    mix_w = (norm1_g[0][None, :], w_in[0].astype(BF16), q_norm_g[0][None, :], k_norm_g[0][None, :],
             wpool_bd.astype(BF16), pool_scale[0][None, :], w_branch_a[0].astype(BF16),
             w_branch_b[0].astype(BF16), w_out[0].astype(BF16),
             norm2_g[0][None, :], jnp.concatenate([wr_hi, wr_lo], axis=1))
    gf = final_norm_g[None, :]

    xp2 = x_prompt.reshape(B * L_ctx, D_MODEL)
    xmid_p, h2_p, gate_p, oh_p, knew, vnew, wg, wu, wd = _mix(
        xp2, mod, lambda i: 0, None, None, mix_w, S=2, L=L_ctx, emit_kv=True,
        cast=(w_exp_gate[0], w_exp_up[0], w_exp_down[0]))
    sh_p, sg_p, dest_p, meta_p = _moe_dispatch(h2_p, gate_p, oh_p)

    xs2 = x_sample.reshape(Bs * L_lat, D_MODEL)
    cache = (cache_k[:, 0].reshape(Bs, P, KV_W), cache_v[:, 0].reshape(Bs, P, KV_W))
    xmid_s, h2_s, gate_s, oh_s = _mix(xs2, mod, lambda i: 1 + i, cache, _rope_tables(L_lat), mix_w,
                                      S=1, L=L_lat, emit_kv=False)
    sh_s, sg_s, dest_s, meta_s = _moe_dispatch(h2_s, gate_s, oh_s)

    moe_p = _moe_unpermute(_experts(sh_p, sg_p, meta_p, wg, wu, wd), dest_p)
    moe_s = _moe_unpermute(_experts(sh_s, sg_s, meta_s, wg, wu, wd), dest_s)
    y_prompt = _final(xmid_p, moe_p, mod, lambda i: 0, gf)
    blocks_per_seq = L_lat // TOKEN_BLOCK
    y_sample = _final(xmid_s, moe_s, mod, lambda i: 1 + i // blocks_per_seq, gf)

    return (y_prompt.reshape(B, L_ctx, D_MODEL), y_sample.reshape(Bs, L_lat, D_MODEL),
            knew.reshape(B, 1, L_ctx, N_KV_HEADS, HEAD_DIM), vnew.reshape(B, 1, L_ctx, N_KV_HEADS, HEAD_DIM))
```

```python
import functools

import jax
import jax.numpy as jnp
from jax import lax
from jax.experimental import pallas as pl
from jax.experimental.pallas import tpu as pltpu
from jax.experimental.pallas import tpu_sc as plsc

F32 = jnp.float32
BF16 = jnp.bfloat16
I32 = jnp.int32
U32 = jnp.uint32

D_MODEL = 1024
HEAD_DIM = 128
N_HEADS = 8
N_KV_HEADS = 2
GROUP = N_HEADS // N_KV_HEADS
ATTN_W = N_HEADS * HEAD_DIM
KV_W = N_KV_HEADS * HEAD_DIM
POOL_WINDOWS = (2, 4, 8, 16)
POOL_GC = 128
POOL_W = POOL_GC * len(POOL_WINDOWS)
IN_W = ATTN_W + 2 * KV_W + POOL_W + 2 * D_MODEL
GATE_COL = ATTN_W + 2 * KV_W + POOL_W
GRID_W = 64
ROPE_THETA = 10000.0
ROPE_NF = HEAD_DIM // 4
N_EXP_GROUPS = 4
EXP_PER_GROUP = 4
N_EXPERTS = 16
D_EXPERT = 256
EPS = 1e-6

LANES = 128
SUBLANES = 8
COND_ROWS = SUBLANES
POOL_HALO = 8
ROW_BLOCK = 256
ADA_COLS = 768
EXPERT_LANE0 = N_EXP_GROUPS
PAIRS_PER_GROUP = EXP_PER_GROUP * (EXP_PER_GROUP - 1) // 2
N_BUCKETS = N_EXP_GROUPS * PAIRS_PER_GROUP
SORT_TILE = 256
EXPERT_TILES_PER_STEP = 4
TOKEN_BLOCK = 512
ROW_CHUNKS = D_MODEL // LANES
SC_CORES = 2
SC_SUBCORES = 16
SC_WORKERS = SC_CORES * SC_SUBCORES
SC_LANES = 16
SC_PIECES_PER_GATHER = 128
SC_ROWS_PER_STEP = 64
PACKED_CHUNKS = ROW_CHUNKS // 2
V7X_VMEM_LIMIT_BYTES = 56 * 1024 * 1024


def _sigmoid(x):
    return 1.0 / (1.0 + jnp.exp(-x))


def _rms(x):
    return x * lax.rsqrt(jnp.mean(x * x, axis=-1, keepdims=True) + EPS)


def _resident(shape):
    zeros = (0,) * len(shape)
    return pl.BlockSpec(shape, lambda i, *_: zeros, pipeline_mode=pl.Buffered(1))


def _tiles_shape(n, chunks=ROW_CHUNKS):
    return (n // SUBLANES, chunks, SUBLANES, LANES)


def _tiles_spec(n, block_index, chunks=ROW_CHUNKS):
    return pl.BlockSpec(_tiles_shape(n, chunks), lambda *a: (block_index(*a), 0, 0, 0))


def _store_tiles(ref, x):
    for c in range(ref.shape[1]):
        ref[:, c, :, :] = x[:, c * LANES:(c + 1) * LANES].reshape(x.shape[0] // SUBLANES, SUBLANES, LANES)


def _load_tiles(ref):
    n = ref.shape[0] * SUBLANES
    return jnp.concatenate([ref[:, c, :, :].reshape(n, LANES) for c in range(ref.shape[1])], axis=1)


def _pack_bf16_pairs(x):
    bits = pltpu.bitcast(x.astype(BF16).astype(F32), U32)
    w = x.shape[1] // 2
    return bits[:, :w] | (bits[:, w:] >> 16)


def _unpack_bf16_pairs(words):
    hi = pltpu.bitcast(words & jnp.uint32(0xFFFF0000), F32).astype(BF16)
    lo = pltpu.bitcast(words << 16, F32).astype(BF16)
    return jnp.concatenate([hi, lo], axis=1)


def _row(x):
    return jnp.transpose(jnp.broadcast_to(x, (x.shape[0], LANES)))[0:1, :]


def _ada_kernel(c_ref, w_ref, b_ref, o_ref):
    c = c_ref[...]
    s = (c * _sigmoid(c)).astype(BF16)
    o_ref[...] = jnp.dot(s, w_ref[...].astype(BF16), preferred_element_type=F32) + b_ref[...]


def _ada(cond, w_ada, b_ada):
    n = w_ada.shape[1]
    return pl.pallas_call(
        _ada_kernel,
        grid=(n // ADA_COLS,),
        in_specs=[
            pl.BlockSpec((COND_ROWS, D_MODEL), lambda j: (0, 0)),
            pl.BlockSpec((D_MODEL, ADA_COLS), lambda j: (0, j)),
            pl.BlockSpec((1, ADA_COLS), lambda j: (0, j)),
        ],
        out_specs=pl.BlockSpec((COND_ROWS, ADA_COLS), lambda j: (0, j)),
        out_shape=jax.ShapeDtypeStruct((COND_ROWS, n), F32),
        name="ada_mod",
    )(cond, w_ada, b_ada)


def _route(logits):
    lane = lax.broadcasted_iota(I32, logits.shape, 1).astype(F32)
    neg = jnp.float32(-1e30)
    far = jnp.float32(LANES)
    is_g = lane < N_EXP_GROUPS
    gl = jnp.where(is_g, logits, neg)
    gmax = jnp.max(gl, axis=-1, keepdims=True)
    gsel = jnp.min(jnp.where(gl == gmax, lane, far), axis=-1, keepdims=True)
    psel = 1.0 / jnp.sum(jnp.where(is_g, jnp.exp(gl - gmax), 0.0), axis=-1, keepdims=True)
    e_lo = EXPERT_LANE0 + EXP_PER_GROUP * gsel
    el = jnp.where(lane >= e_lo, jnp.where(lane < e_lo + EXP_PER_GROUP, logits, neg), neg)
    v1 = jnp.max(el, axis=-1, keepdims=True)
    i1 = jnp.min(jnp.where(el == v1, lane, far), axis=-1, keepdims=True)
    el2 = jnp.where(lane == i1, neg, el)
    v2 = jnp.max(el2, axis=-1, keepdims=True)
    i2 = jnp.min(jnp.where(el2 == v2, jnp.where(lane == i1, far, lane), far), axis=-1, keepdims=True)
    e2 = jnp.exp(v2 - v1)
    w1 = psel / (1.0 + e2)
    w2 = psel * e2 / (1.0 + e2)
    gate = jnp.where(lane == i1, w1, jnp.where(lane == i2, w2, 0.0))
    a = jnp.minimum(i1, i2) - e_lo
    b = jnp.maximum(i1, i2) - e_lo
    pair = a * (7.0 - a) * 0.5 + (b - a - 1.0)
    return gate, gsel * PAIRS_PER_GROUP + pair


def _mix_kernel(*refs, S, L, P, use_rope, emit_kv, n_cast, n_blocks, U):
    it = iter(refs)
    x_ref = next(it)
    mod_ref = next(it)
    if P:
        ck_ref = next(it)
        cv_ref = next(it)
    if use_rope:
        cos_ref = next(it)
        sneg_ref = next(it)
        spos_ref = next(it)
    (g1_ref, win_ref, qg_ref, kg_ref, wpool_ref, pscale_ref, wa_ref, wb_ref, wo_ref,
     g2_ref, wr_ref) = (next(it) for _ in range(11))
    cast_in = [next(it) for _ in range(n_cast)]
    xmid_ref = next(it)
    h2_ref = next(it)
    gate_ref = next(it)
    oh_ref = next(it)
    if emit_kv:
        knew_ref = next(it)
        vnew_ref = next(it)
    cast_out = [next(it) for _ in range(n_cast)]
    q_s, k_s, v_s, xp_s, h_s, attn_s, xm_s, mod2_s = (next(it) for _ in range(8))

    TM = S * L
    RB = ROW_BLOCK
    nrb = TM // RB
    n_steps = n_blocks // U
    scale = HEAD_DIM ** -0.5
    step = pl.program_id(0)
    block0 = U * jnp.minimum(step, n_steps - 1)
    slot = step % 2

    sh1 = mod_ref[0, 0:1, :]
    gain1 = g1_ref[...] * (1.0 + mod_ref[0, 1:2, :])
    gt1 = mod_ref[0, 2:3, :]
    sh2 = mod_ref[0, 3:4, :]
    gain2 = g2_ref[...] * (1.0 + mod_ref[0, 4:5, :])
    qg = qg_ref[...]
    kg = kg_ref[...]

    def project(r, carry):
        r0 = pl.multiple_of(r * RB, RB)
        s = r0 // L
        o = pl.multiple_of(r0 % L, RB)
        hb = (_rms(x_ref[pl.ds(r0, RB), :]) * gain1 + sh1).astype(BF16)
        h_s[pl.ds(r0, RB), :] = hb
        p1 = jnp.dot(hb, win_ref[:, 0:GATE_COL], preferred_element_type=F32)
        if use_rope:
            cs = cos_ref[pl.ds(o, RB), :]
            sn = sneg_ref[pl.ds(o, RB), :]
            sp = spos_ref[pl.ds(o, RB), :]

        def rope(t):
            return (t * cs + pltpu.roll(t, HEAD_DIM - ROPE_NF, 1) * sn + pltpu.roll(t, ROPE_NF, 1) * sp)

        for hd in range(N_HEADS):
            qh = _rms(p1[:, hd * HEAD_DIM:(hd + 1) * HEAD_DIM]) * qg
            if use_rope:
                qh = rope(qh)
            q_s[hd, pl.ds(r0, RB), :] = qh.astype(BF16)
        for kh in range(N_KV_HEADS):
            c0 = ATTN_W + kh * HEAD_DIM
            kk = _rms(p1[:, c0:c0 + HEAD_DIM]) * kg
            if emit_kv:
                knew_ref[pl.ds(N_KV_HEADS * r0 + kh, RB, stride=N_KV_HEADS), :] = kk
            if use_rope:
                kk = rope(kk)
            k_s[s, pl.ds(P + o, RB), kh * HEAD_DIM:(kh + 1) * HEAD_DIM] = kk.astype(BF16)
        vv = p1[:, ATTN_W + KV_W:ATTN_W + 2 * KV_W]
        if emit_kv:
            for kh in range(N_KV_HEADS):
                vnew_ref[pl.ds(N_KV_HEADS * r0 + kh, RB, stride=N_KV_HEADS), :] = (
                    vv[:, kh * HEAD_DIM:(kh + 1) * HEAD_DIM])
        v_s[s, pl.ds(P + o, RB), :] = vv.astype(BF16)
        xp_s[s, pl.ds(POOL_HALO + o, RB), :] = p1[:, ATTN_W + 2 * KV_W:GATE_COL]
        return carry

    @pl.when(step == 0)
    def _():
        xm_s[1] = jnp.zeros((U * RB, D_MODEL), F32)
        mod2_s[1] = jnp.zeros((2, D_MODEL), F32)

    @pl.when((step < n_steps) & (step % (nrb // U) == 0))
    def _():
        if P:
            k_s[0, 0:P, :] = ck_ref[0].astype(BF16)
            v_s[0, 0:P, :] = cv_ref[0].astype(BF16)
        xp_s[:, 0:POOL_HALO, :] = jnp.zeros((S, POOL_HALO, POOL_W), F32)
        xp_s[:, L + POOL_HALO:L + 2 * POOL_HALO, :] = jnp.zeros((S, POOL_HALO, POOL_W), F32)
        lax.fori_loop(0, TM // RB, project, 0)
        for src, dst in zip(cast_in, cast_out):
            dst[...] = src[...].astype(BF16)

    def mix(u):
        r0 = pl.multiple_of(((block0 + u) % nrb) * RB, RB)
        s = r0 // L
        o = pl.multiple_of(r0 % L, RB)
        attn_u = attn_s.at[u]
        rows = slice(u * RB, (u + 1) * RB)

        for kh in range(N_KV_HEADS):
            k = k_s[s, :, kh * HEAD_DIM:(kh + 1) * HEAD_DIM]
            v = v_s[s, :, kh * HEAD_DIM:(kh + 1) * HEAD_DIM]
            q4 = q_s[kh * GROUP:(kh + 1) * GROUP, pl.ds(r0, RB), :].reshape(GROUP * RB, HEAD_DIM)
            sc = lax.dot_general(q4, k, (((1,), (1,)), ((), ())), preferred_element_type=F32) * scale
            e = jnp.exp(sc - jnp.max(sc, axis=-1, keepdims=True))
            den = jnp.sum(e, axis=-1, keepdims=True)
            o4 = jnp.dot(e.astype(BF16), v, preferred_element_type=F32) / den
            for g in range(GROUP):
                hd = kh * GROUP + g
                attn_u[:, hd * HEAD_DIM:(hd + 1) * HEAD_DIM] = o4[g * RB:(g + 1) * RB].astype(BF16)
        a = jnp.dot(attn_u[...], wa_ref[...], preferred_element_type=F32)

        t = o + lax.broadcasted_iota(I32, (RB, 1), 0)
        RW = RB + 2 * POOL_HALO
        parts = []
        for gi, w in enumerate(POOL_WINDOWS):
            cols = slice(gi * POOL_GC, (gi + 1) * POOL_GC)
            xw = xp_s[s, pl.ds(o, RW), cols]
            run = xw
            span = 1
            while span < w:
                run = run + pltpu.roll(run, span, 0)
                span *= 2
            if w // 2 > 1:
                run = pltpu.roll(run, RW - (w // 2 - 1), 0)
            tot = run[POOL_HALO:POOL_HALO + RB]
            cnt = (jnp.minimum(t + w // 2, L) - jnp.maximum(t - w // 2, 0)).astype(F32)
            parts.append(tot / cnt - xw[POOL_HALO:POOL_HALO + RB])
        dpool = jnp.concatenate(parts, axis=1).astype(BF16)
        pooled = jnp.dot(dpool, wpool_ref[...], preferred_element_type=F32) * pscale_ref[...]
        b = jnp.dot(pooled.astype(BF16), wb_ref[...], preferred_element_type=F32)

        gates = jnp.dot(h_s[pl.ds(r0, RB), :], win_ref[:, GATE_COL:IN_W], preferred_element_type=F32)
        merged = _sigmoid(gates[:, 0:D_MODEL]) * a + _sigmoid(gates[:, D_MODEL:2 * D_MODEL]) * b
        upd = jnp.dot(merged.astype(BF16), wo_ref[...], preferred_element_type=F32)
        xm = x_ref[pl.ds(r0, RB), :] + gt1 * upd
        xmid_ref[rows, :] = xm
        xm_s[slot, rows, :] = xm

    def moe_prep(u):
        rows = slice(u * RB, (u + 1) * RB)
        h2 = _rms(xm_s[1 - slot, rows, :]) * mod2_s[1 - slot, 0:1, :] + mod2_s[1 - slot, 1:2, :]
        hi = h2.astype(BF16)
        lo = (h2 - hi.astype(F32)).astype(BF16)
        l1 = jnp.dot(hi, wr_ref[...], preferred_element_type=F32)
        l2 = jnp.dot(lo, wr_ref[:, 0:LANES], preferred_element_type=F32)
        gate, bucket = _route(l1[:, 0:LANES] + l1[:, LANES:2 * LANES] + l2)
        groups = pl.ds(u * (RB // SUBLANES), RB // SUBLANES)
        _store_tiles(h2_ref.at[groups], _pack_bf16_pairs(h2))
        gate_ref[rows, :] = gate
        lane = lax.broadcasted_iota(I32, (RB, LANES), 1).astype(F32)
        oh_ref[rows, :] = jnp.where(lane == bucket, 1.0, 0.0).astype(BF16)

    mod2_s[slot, 0:1, :] = gain2
    mod2_s[slot, 1:2, :] = sh2
    for u in range(U):
        moe_prep(u)
    for u in range(U):
        mix(u)


def _mix(x2d, mod, mod_row, cache, rope_tabs, weights, *, S, L, emit_kv, blocks_per_step, cast=()):
    T = x2d.shape[0]
    TM = S * L
    P = cache[0].shape[1] if cache is not None else 0
    use_rope = rope_tabs is not None
    assert T % TM == 0 and L % ROW_BLOCK == 0
    assert not (use_rope or P) or S == 1
    Lk = P + L

    args = [x2d, mod]
    nrb = TM // ROW_BLOCK
    n_blocks = T // ROW_BLOCK
    step_rows = blocks_per_step * ROW_BLOCK
    steps_per_group = nrb // blocks_per_step
    n_mix_steps = n_blocks // blocks_per_step
    assert nrb % blocks_per_step == 0

    def mixed(s):
        return jnp.minimum(s, n_mix_steps - 1)

    def group(s):
        return mixed(s) // steps_per_group

    def prepared(s):
        return jnp.maximum(s - 1, 0)

    in_specs = [
        pl.BlockSpec((TM, D_MODEL), lambda s: (group(s), 0)),
        pl.BlockSpec((1, 6, D_MODEL), lambda s: (mod_row(group(s)), 0, 0)),
    ]
    if P:
        args += list(cache)
        in_specs += [pl.BlockSpec((1, P, KV_W), lambda s: (group(s), 0, 0))] * 2
    if use_rope:
        args += list(rope_tabs)
        in_specs += [_resident((L, HEAD_DIM))] * 3
    args += list(weights)
    in_specs += [_resident(w.shape) for w in weights]
    n_steps = T // TM
    cast_specs = []
    for w in cast:
        assert w.shape[0] % n_steps == 0
        blk = (w.shape[0] // n_steps,) + w.shape[1:]
        cast_specs.append(pl.BlockSpec(blk, lambda s, n=len(blk): (group(s),) + (0,) * (n - 1)))
    args += list(cast)
    in_specs += cast_specs

    out_shape = [jax.ShapeDtypeStruct((T, D_MODEL), F32), jax.ShapeDtypeStruct(_tiles_shape(T, PACKED_CHUNKS), U32),
                 jax.ShapeDtypeStruct((T, LANES), F32),
                 jax.ShapeDtypeStruct((T, LANES), BF16)]
    out_specs = [pl.BlockSpec((step_rows, D_MODEL), lambda s: (mixed(s), 0)),
                 _tiles_spec(step_rows, prepared, PACKED_CHUNKS),
                 pl.BlockSpec((step_rows, LANES), lambda s: (prepared(s), 0)),
                 pl.BlockSpec((step_rows, LANES), lambda s: (prepared(s), 0))]
    if emit_kv:
        out_shape += [jax.ShapeDtypeStruct((T * N_KV_HEADS, HEAD_DIM), F32)] * 2
        out_specs += [pl.BlockSpec((TM * N_KV_HEADS, HEAD_DIM), lambda s: (group(s), 0))] * 2
    out_shape += [jax.ShapeDtypeStruct(w.shape, BF16) for w in cast]
    out_specs += cast_specs

    scratch = [
        pltpu.VMEM((N_HEADS, TM, HEAD_DIM), BF16),
        pltpu.VMEM((S, Lk, KV_W), BF16),
        pltpu.VMEM((S, Lk, KV_W), BF16),
        pltpu.VMEM((S, L + 2 * POOL_HALO, POOL_W), F32),
        pltpu.VMEM((TM, D_MODEL), BF16),
        pltpu.VMEM((blocks_per_step, ROW_BLOCK, ATTN_W), BF16),
        pltpu.VMEM((2, step_rows, D_MODEL), F32),
        pltpu.VMEM((2, 2, D_MODEL), F32),
    ]
    kern = functools.partial(_mix_kernel, S=S, L=L, P=P, use_rope=use_rope, emit_kv=emit_kv,
                             n_cast=len(cast), n_blocks=n_blocks, U=blocks_per_step)
    return pl.pallas_call(
        kern,
        grid=(n_mix_steps + 1,),
        in_specs=in_specs,
        out_specs=out_specs,
        out_shape=out_shape,
        scratch_shapes=scratch,
        compiler_params=pltpu.CompilerParams(
            dimension_semantics=("arbitrary",), vmem_limit_bytes=V7X_VMEM_LIMIT_BYTES),
        name="mixer_rope" if use_rope else "mixer_ctx",
    )(*args)


def _plan_kernel(oh_ref, dest_ref, meta_ref, *, n_blocks):
    TB = TOKEN_BLOCK
    lane = lax.broadcasted_iota(I32, (SUBLANES, LANES), 1)

    def count(b, acc):
        oh = oh_ref[pl.ds(pl.multiple_of(b * TB, TB), TB), :].astype(F32)
        return acc + jnp.sum(oh, axis=0, keepdims=True)

    counts = lax.fori_loop(0, n_blocks, count, jnp.zeros((SUBLANES, LANES), F32))
    padded = jnp.ceil(counts * (1.0 / SORT_TILE)) * SORT_TILE
    ends = padded
    step = 1
    while step < LANES:
        ends = ends + jnp.where(lane >= step, pltpu.roll(ends, step, 1), 0.0)
        step *= 2
    starts = ends - padded

    tri = jnp.where(lax.broadcasted_iota(I32, (TB, TB), 1) < lax.broadcasted_iota(I32, (TB, TB), 0),
                    1.0, 0.0).astype(BF16)

    def place(b, seen):
        oh = oh_ref[pl.ds(pl.multiple_of(b * TB, TB), TB), :]
        ohf = oh.astype(F32)
        rank = jnp.dot(tri, oh, preferred_element_type=F32)
        base = (starts + seen)[0:1, :]
        d = jnp.sum(ohf * (rank + base), axis=1, keepdims=True)
        dest_ref[b] = _row(d).astype(I32)
        return seen + jnp.sum(ohf, axis=0, keepdims=True)

    lax.fori_loop(0, n_blocks, place, jnp.zeros((SUBLANES, LANES), F32))

    tile_row0 = lax.broadcasted_iota(I32, (LANES, LANES), 0).astype(F32) * SORT_TILE
    is_bucket = lax.broadcasted_iota(I32, (LANES, LANES), 1) < N_BUCKETS
    done = jnp.sum(jnp.where(is_bucket, jnp.where(ends[0:1, :] <= tile_row0, 1.0, 0.0), 0.0),
                   axis=1, keepdims=True)
    bkt = jnp.minimum(done, N_BUCKETS - 1.0)
    grp = (jnp.where(bkt >= PAIRS_PER_GROUP, 1.0, 0.0) + jnp.where(bkt >= 2 * PAIRS_PER_GROUP, 1.0, 0.0)
           + jnp.where(bkt >= 3 * PAIRS_PER_GROUP, 1.0, 0.0))
    pair = bkt - PAIRS_PER_GROUP * grp
    a = jnp.where(pair >= 3.0, 1.0, 0.0) + jnp.where(pair >= 5.0, 1.0, 0.0)
    b = pair - a * (7.0 - a) * 0.5 + a + 1.0
    e1 = EXP_PER_GROUP * grp + a
    e2 = EXP_PER_GROUP * grp + b
    meta = jnp.concatenate(
        [_row(e1), _row(e2), ends[0:1, :] * (1.0 / SORT_TILE), jnp.zeros((SUBLANES - 3, LANES), F32)], axis=0)
    meta_ref[...] = meta.astype(I32)


def _plan(onehot):
    T = onehot.shape[0]
    n_blocks = T // TOKEN_BLOCK
    dest, meta = pl.pallas_call(
        functools.partial(_plan_kernel, n_blocks=n_blocks),
        out_shape=[jax.ShapeDtypeStruct((n_blocks, 1, TOKEN_BLOCK), I32),
                   jax.ShapeDtypeStruct((SUBLANES, LANES), I32)],
        name="moe_plan",
    )(onehot)
    return dest.reshape(T), meta


def _sc_move_rows(src_v, table_hbm, out_hbm, lo, n_rows, idx_v, pieces_v, sem):
    chunks = pieces_v.shape[0] // SC_ROWS_PER_STEP
    lane = lax.iota(I32, SC_LANES)
    row_in_group = lane & (SUBLANES - 1)
    chunk_in_pair = lane >> 3
    rows_per_gather = SC_PIECES_PER_GATHER // chunks

    @pl.loop(0, n_rows // SC_ROWS_PER_STEP)
    def _(step):
        copies = []
        for g in range(SC_ROWS_PER_STEP // rows_per_gather):
            r0 = step * SC_ROWS_PER_STEP + g * rows_per_gather
            for v in range(SC_PIECES_PER_GATHER // SC_LANES):
                group, chunk0 = v // (chunks // 2), 2 * (v % (chunks // 2))
                tok = plsc.load_gather(src_v, [r0 + group * SUBLANES + row_in_group])
                piece = (tok >> 3) * (SUBLANES * chunks) + (chunk0 + chunk_in_pair) * SUBLANES + (tok & 7)
                idx_v[pl.ds(g * SC_PIECES_PER_GATHER + v * SC_LANES, SC_LANES)] = piece
            window = pl.ds(g * SC_PIECES_PER_GATHER, SC_PIECES_PER_GATHER)
            copies.append(pltpu.async_copy(table_hbm.at[idx_v.at[window]], pieces_v.at[window], sem))
        for cp in copies:
            cp.wait()
        first = pl.multiple_of((lo + step * SC_ROWS_PER_STEP) * chunks, SC_ROWS_PER_STEP * chunks)
        pltpu.sync_copy(pieces_v, out_hbm.at[pl.ds(first, SC_ROWS_PER_STEP * chunks)])


def _sc_scratch(chunks, dtype):
    return [pltpu.VMEM((SC_ROWS_PER_STEP * chunks,), I32), pltpu.VMEM((SC_ROWS_PER_STEP * chunks, LANES), dtype)]


def _sc_dispatch(h2_flat, gate_rows, dest, n_rows):
    T = dest.shape[0]
    per_worker = n_rows // SC_WORKERS
    rows_per_step = SC_ROWS_PER_STEP
    chunks = h2_flat.shape[0] // T
    assert n_rows % SC_WORKERS == 0 and per_worker % rows_per_step == 0 and T % SC_LANES == 0
    mesh = plsc.VectorSubcoreMesh(core_axis_name="c", subcore_axis_name="s")

    @functools.partial(
        pl.kernel, mesh=mesh,
        out_type=[jax.ShapeDtypeStruct((n_rows * chunks, LANES), h2_flat.dtype),
                  jax.ShapeDtypeStruct((n_rows, LANES), F32)],
        scratch_types=[pltpu.VMEM((T,), I32), pltpu.VMEM((per_worker,), I32)]
        + _sc_scratch(chunks, h2_flat.dtype)
        + [pltpu.VMEM((rows_per_step, LANES), F32), pltpu.SemaphoreType.DMA, pltpu.SemaphoreType.DMA],
        compiler_params=pltpu.CompilerParams(use_tc_tiling_on_sc=True, needs_layout_passes=False),
        name="sc_dispatch",
    )
    def dispatch(h2_hbm, gate_hbm, dest_hbm, out_h_hbm, out_g_hbm,
                 dest_v, src_v, idx_v, pieces_v, gates_v, sem_h, sem_g):
        worker = lax.axis_index("s") * SC_CORES + lax.axis_index("c")
        lo = worker * per_worker
        pltpu.sync_copy(dest_hbm, dest_v)

        @pl.loop(0, per_worker // SC_LANES)
        def _(j):
            j0 = pl.multiple_of(j * SC_LANES, SC_LANES)
            src_v[pl.ds(j0, SC_LANES)] = lax.rem(lo + j0 + lax.iota(I32, SC_LANES), T)

        @pl.loop(0, T // SC_LANES)
        def _(j):
            t0 = pl.multiple_of(j * SC_LANES, SC_LANES)
            d = dest_v[pl.ds(t0, SC_LANES)] - lo
            mine = (d >= 0) & (d < per_worker)
            plsc.store_scatter(src_v, [jnp.where(mine, d, 0)], t0 + lax.iota(I32, SC_LANES), mask=mine)

        @pl.loop(0, per_worker // rows_per_step)
        def _(j):
            off = pl.multiple_of(j * rows_per_step, rows_per_step)
            pltpu.async_copy(gate_hbm.at[src_v.at[pl.ds(off, rows_per_step)]], gates_v, sem_g).wait()
            pltpu.sync_copy(gates_v, out_g_hbm.at[pl.ds(lo + off, rows_per_step)])

        _sc_move_rows(src_v, h2_hbm, out_h_hbm, lo, per_worker, idx_v, pieces_v, sem_h)

    return dispatch(h2_flat, gate_rows, dest)


def _expert_kernel(e1s, e2s, n_used, x_ref, gv_ref, wg_ref, wu_ref, wd_ref, o_ref, wup_s, wdn_s):
    groups = SORT_TILE // SUBLANES

    def one_tile(k, carry):
        t = pl.program_id(0) * EXPERT_TILES_PER_STEP + k
        e1 = e1s[t]
        e2 = e2s[t]
        prev = jnp.maximum(t - 1, 0)
        new_pair = (t == 0) | (e1 != e1s[prev]) | (e2 != e2s[prev])
        rows = pl.ds(pl.multiple_of(k * groups, groups), groups)

        @pl.when((t < n_used[0]) & new_pair)
        def _():
            for slot, e in enumerate((e1, e2)):
                wup_s[:, (2 * slot) * D_EXPERT:(2 * slot + 1) * D_EXPERT] = wg_ref[e]
                wup_s[:, (2 * slot + 1) * D_EXPERT:(2 * slot + 2) * D_EXPERT] = wu_ref[e]
                wdn_s[slot * D_EXPERT:(slot + 1) * D_EXPERT, :] = wd_ref[e]

        @pl.when(t < n_used[0])
        def _():
            x = _unpack_bf16_pairs(_load_tiles(x_ref.at[rows]))
            gv = gv_ref[pl.ds(pl.multiple_of(k * SORT_TILE, SORT_TILE), SORT_TILE), :]
            lane = lax.broadcasted_iota(I32, gv.shape, 1)
            h = jnp.dot(x, wup_s[...], preferred_element_type=F32)
            hid = []
            for slot, e in enumerate((e1, e2)):
                ge = jnp.sum(jnp.where(lane == EXPERT_LANE0 + e, gv, 0.0), axis=-1, keepdims=True)
                hg = h[:, (2 * slot) * D_EXPERT:(2 * slot + 1) * D_EXPERT]
                hu = h[:, (2 * slot + 1) * D_EXPERT:(2 * slot + 2) * D_EXPERT]
                hid.append((hg * _sigmoid(hg) * hu * ge).astype(BF16))
            out = jnp.dot(jnp.concatenate(hid, axis=1), wdn_s[...], preferred_element_type=F32)
            _store_tiles(o_ref.at[rows], out)

        @pl.when(t >= n_used[0])
        def _():
            o_ref[rows] = jnp.zeros((groups,) + o_ref.shape[1:], F32)

        return carry

    lax.fori_loop(0, EXPERT_TILES_PER_STEP, one_tile, 0)


def _experts(sorted_h2, sorted_gates, meta, wg, wu, wd):
    n_tiles = sorted_h2.shape[0] * SUBLANES // SORT_TILE
    step_rows = SORT_TILE * EXPERT_TILES_PER_STEP
    assert n_tiles % EXPERT_TILES_PER_STEP == 0

    def last_used(i, e1, e2, nu):
        return jnp.minimum(i, (nu[0] - 1) // EXPERT_TILES_PER_STEP)

    return pl.pallas_call(
        _expert_kernel,
        grid_spec=pltpu.PrefetchScalarGridSpec(
            num_scalar_prefetch=3,
            grid=(n_tiles // EXPERT_TILES_PER_STEP,),
            in_specs=[
                _tiles_spec(step_rows, last_used, PACKED_CHUNKS),
                pl.BlockSpec((step_rows, LANES), lambda *a: (last_used(*a), 0)),
                _resident(wg.shape), _resident(wu.shape), _resident(wd.shape),
            ],
            out_specs=_tiles_spec(step_rows, lambda i, *_: i),
            scratch_shapes=[pltpu.VMEM((D_MODEL, 4 * D_EXPERT), BF16), pltpu.VMEM((2 * D_EXPERT, D_MODEL), BF16)],
        ),
        out_shape=jax.ShapeDtypeStruct(_tiles_shape(n_tiles * SORT_TILE), F32),
        compiler_params=pltpu.CompilerParams(
            dimension_semantics=("arbitrary",), vmem_limit_bytes=V7X_VMEM_LIMIT_BYTES),
        name="moe_experts",
    )(meta[0, :n_tiles], meta[1, :n_tiles], meta[2, LANES - 1:LANES], sorted_h2, sorted_gates, wg, wu, wd)


def _sc_row_gather(table_flat, idx):
    n = idx.shape[0]
    per_worker = n // SC_WORKERS
    chunks = ROW_CHUNKS
    assert n % SC_WORKERS == 0 and per_worker % SC_ROWS_PER_STEP == 0
    mesh = plsc.VectorSubcoreMesh(core_axis_name="c", subcore_axis_name="s")

    @functools.partial(
        pl.kernel, mesh=mesh,
        out_type=jax.ShapeDtypeStruct((n * chunks, LANES), table_flat.dtype),
        scratch_types=[pltpu.VMEM((per_worker,), I32)] + _sc_scratch(chunks, table_flat.dtype)
        + [pltpu.SemaphoreType.DMA],
        compiler_params=pltpu.CompilerParams(use_tc_tiling_on_sc=True, needs_layout_passes=False),
        name="sc_row_gather",
    )
    def gather(table_hbm, idx_hbm, out_hbm, src_v, idx_v, pieces_v, sem):
        worker = lax.axis_index("s") * SC_CORES + lax.axis_index("c")
        lo = worker * per_worker
        pltpu.sync_copy(idx_hbm.at[pl.ds(lo, per_worker)], src_v)
        _sc_move_rows(src_v, table_hbm, out_hbm, lo, per_worker, idx_v, pieces_v, sem)

    return gather(table_flat, idx)


def _final_kernel(x_ref, moe_ref, mod_ref, gf_ref, o_ref):
    y = x_ref[...] + mod_ref[0, 5:6, :] * _load_tiles(moe_ref)
    o_ref[...] = _rms(y) * gf_ref[...]


def _final(xmid, moe_rows, mod, mod_row, gf):
    T = xmid.shape[0]
    return pl.pallas_call(
        _final_kernel,
        grid=(T // TOKEN_BLOCK,),
        in_specs=[
            pl.BlockSpec((TOKEN_BLOCK, D_MODEL), lambda i: (i, 0)),
            _tiles_spec(TOKEN_BLOCK, lambda i: i),
            pl.BlockSpec((1, 6, D_MODEL), lambda i: (mod_row(i), 0, 0)),
            pl.BlockSpec((1, D_MODEL), lambda i: (0, 0)),
        ],
        out_specs=pl.BlockSpec((TOKEN_BLOCK, D_MODEL), lambda i: (i, 0)),
        out_shape=jax.ShapeDtypeStruct((T, D_MODEL), F32),
        compiler_params=pltpu.CompilerParams(dimension_semantics=("arbitrary",)),
        name="moe_final",
    )(xmid, moe_rows, mod, gf)


def _flat(tiles):
    return tiles.reshape(-1, LANES)


def _moe_dispatch(h2_tiles, gate_rows, onehot):
    T = gate_rows.shape[0]
    n_tiles = T // SORT_TILE + N_BUCKETS
    n_rows = n_tiles * SORT_TILE
    assert n_tiles <= LANES and T % TOKEN_BLOCK == 0
    dest, meta = _plan(onehot)
    sorted_h2, sorted_gates = _sc_dispatch(_flat(h2_tiles), gate_rows, dest, n_rows)
    return sorted_h2.reshape(_tiles_shape(n_rows, PACKED_CHUNKS)), sorted_gates, dest, meta


def _moe_unpermute(moe_sorted_tiles, dest):
    return _sc_row_gather(_flat(moe_sorted_tiles), dest).reshape(_tiles_shape(dest.shape[0]))


def _rope_tables(n_tokens):
    t = jnp.arange(n_tokens)
    row = (t // GRID_W).astype(F32)
    col = (t % GRID_W).astype(F32)
    freq = ROPE_THETA ** (-jnp.arange(ROPE_NF, dtype=F32) / ROPE_NF)
    ang = jnp.concatenate([row[:, None] * freq] * 2 + [col[:, None] * freq] * 2, axis=-1)
    first = (jnp.arange(HEAD_DIM) % (2 * ROPE_NF)) < ROPE_NF
    sin = jnp.sin(ang)
    return jnp.cos(ang), jnp.where(first, -sin, 0.0), jnp.where(first, 0.0, sin)


def kernel(x_prompt, x_sample, cache_k, cache_v, c, c_ctx, norm1_g, norm2_g, w_ada, b_ada, w_in, q_norm_g, k_norm_g, w_pool, pool_scale, w_branch_a, w_branch_b, w_out, w_router_group, w_router_expert, w_exp_gate, w_exp_up, w_exp_down, final_norm_g):
    assert norm1_g.shape[0] == 1, "single-layer trunk"
    B, L_ctx, _ = x_prompt.shape
    Bs, L_lat, _ = x_sample.shape
    P = cache_k.shape[2]
    assert 1 + Bs <= COND_ROWS

    cond = jnp.zeros((COND_ROWS, D_MODEL), F32).at[0].set(c_ctx).at[1:1 + Bs].set(c)
    mod = _ada(cond, w_ada[0], b_ada[0][None, :]).reshape(COND_ROWS, 6, D_MODEL)

    wpool_bd = jax.scipy.linalg.block_diag(*[w_pool[0, g] for g in range(len(POOL_WINDOWS))])
    wr = jnp.zeros((D_MODEL, LANES), F32)
    wr = wr.at[:, 0:N_EXP_GROUPS].set(w_router_group[0])
    wr = wr.at[:, EXPERT_LANE0:EXPERT_LANE0 + N_EXPERTS].set(w_router_expert[0])
    wr_hi = wr.astype(BF16)
    wr_lo = (wr - wr_hi.astype(F32)).astype(BF16)
    mix_w = (norm1_g[0][None, :], w_in[0].astype(BF16), q_norm_g[0][None, :], k_norm_g[0][None, :],
             wpool_bd.astype(BF16), pool_scale[0][None, :], w_branch_a[0].astype(BF16),
             w_branch_b[0].astype(BF16), w_out[0].astype(BF16),
             norm2_g[0][None, :], jnp.concatenate([wr_hi, wr_lo], axis=1))
    gf = final_norm_g[None, :]

    xp2 = x_prompt.reshape(B * L_ctx, D_MODEL)
    xmid_p, h2_p, gate_p, oh_p, knew, vnew, wg, wu, wd = _mix(
        xp2, mod, lambda i: 0, None, None, mix_w, S=2, L=L_ctx, emit_kv=True, blocks_per_step=2,
        cast=(w_exp_gate[0], w_exp_up[0], w_exp_down[0]))
    sh_p, sg_p, dest_p, meta_p = _moe_dispatch(h2_p, gate_p, oh_p)

    xs2 = x_sample.reshape(Bs * L_lat, D_MODEL)
    cache = (cache_k[:, 0].reshape(Bs, P, KV_W), cache_v[:, 0].reshape(Bs, P, KV_W))
    xmid_s, h2_s, gate_s, oh_s = _mix(xs2, mod, lambda i: 1 + i, cache, _rope_tables(L_lat), mix_w,
                                      S=1, L=L_lat, emit_kv=False, blocks_per_step=1)
    sh_s, sg_s, dest_s, meta_s = _moe_dispatch(h2_s, gate_s, oh_s)

    moe_p = _moe_unpermute(_experts(sh_p, sg_p, meta_p, wg, wu, wd), dest_p)
    moe_s = _moe_unpermute(_experts(sh_s, sg_s, meta_s, wg, wu, wd), dest_s)
    y_prompt = _final(xmid_p, moe_p, mod, lambda i: 0, gf)
    blocks_per_seq = L_lat // TOKEN_BLOCK
    y_sample = _final(xmid_s, moe_s, mod, lambda i: 1 + i // blocks_per_seq, gf)

    return (y_prompt.reshape(B, L_ctx, D_MODEL), y_sample.reshape(Bs, L_lat, D_MODEL),
            knew.reshape(B, 1, L_ctx, N_KV_HEADS, HEAD_DIM), vnew.reshape(B, 1, L_ctx, N_KV_HEADS, HEAD_DIM))
```

```python
import functools

import jax
import jax.numpy as jnp
from jax import lax
from jax.experimental import pallas as pl
from jax.experimental.pallas import tpu as pltpu
from jax.experimental.pallas import tpu_sc as plsc

F32 = jnp.float32
BF16 = jnp.bfloat16
I32 = jnp.int32
U32 = jnp.uint32

D_MODEL = 1024
HEAD_DIM = 128
N_HEADS = 8
N_KV_HEADS = 2
GROUP = N_HEADS // N_KV_HEADS
ATTN_W = N_HEADS * HEAD_DIM
KV_W = N_KV_HEADS * HEAD_DIM
POOL_WINDOWS = (2, 4, 8, 16)
POOL_GC = 128
POOL_W = POOL_GC * len(POOL_WINDOWS)
IN_W = ATTN_W + 2 * KV_W + POOL_W + 2 * D_MODEL
GATE_COL = ATTN_W + 2 * KV_W + POOL_W
GRID_W = 64
ROPE_THETA = 10000.0
ROPE_NF = HEAD_DIM // 4
N_EXP_GROUPS = 4
EXP_PER_GROUP = 4
N_EXPERTS = 16
D_EXPERT = 256
EPS = 1e-6

LANES = 128
SUBLANES = 8
COND_ROWS = SUBLANES
POOL_HALO = 8
ROW_BLOCK = 256
ADA_COLS = 768
EXPERT_LANE0 = N_EXP_GROUPS
PAIRS_PER_GROUP = EXP_PER_GROUP * (EXP_PER_GROUP - 1) // 2
N_BUCKETS = N_EXP_GROUPS * PAIRS_PER_GROUP
SORT_TILE = 256
EXPERT_TILES_PER_STEP = 4
TOKEN_BLOCK = 512
FINAL_BLOCK = 1024
ROW_CHUNKS = D_MODEL // LANES
SC_CORES = 2
SC_SUBCORES = 16
SC_WORKERS = SC_CORES * SC_SUBCORES
SC_LANES = 16
SC_PIECES_PER_GATHER = 128
SC_ROWS_PER_STEP = 64
PACKED_CHUNKS = ROW_CHUNKS // 2
V7X_VMEM_LIMIT_BYTES = 56 * 1024 * 1024


def _sigmoid(x):
    return 1.0 / (1.0 + jnp.exp(-x))


def _rms(x):
    return x * lax.rsqrt(jnp.mean(x * x, axis=-1, keepdims=True) + EPS)


def _resident(shape):
    zeros = (0,) * len(shape)
    return pl.BlockSpec(shape, lambda i, *_: zeros, pipeline_mode=pl.Buffered(1))


def _tiles_shape(n, chunks=ROW_CHUNKS):
    return (n // SUBLANES, chunks, SUBLANES, LANES)


def _tiles_spec(n, block_index, chunks=ROW_CHUNKS):
    return pl.BlockSpec(_tiles_shape(n, chunks), lambda *a: (block_index(*a), 0, 0, 0))


def _store_tiles(ref, x):
    for c in range(ref.shape[1]):
        ref[:, c, :, :] = x[:, c * LANES:(c + 1) * LANES].reshape(x.shape[0] // SUBLANES, SUBLANES, LANES)


def _load_tiles(ref):
    n = ref.shape[0] * SUBLANES
    return jnp.concatenate([ref[:, c, :, :].reshape(n, LANES) for c in range(ref.shape[1])], axis=1)


def _pack_bf16_pairs(x):
    bits = pltpu.bitcast(x.astype(BF16).astype(F32), U32)
    w = x.shape[1] // 2
    return bits[:, :w] | (bits[:, w:] >> 16)


def _unpack_bf16_pairs(words):
    hi = pltpu.bitcast(words & jnp.uint32(0xFFFF0000), F32).astype(BF16)
    lo = pltpu.bitcast(words << 16, F32).astype(BF16)
    return jnp.concatenate([hi, lo], axis=1)


def _row(x):
    return jnp.transpose(jnp.broadcast_to(x, (x.shape[0], LANES)))[0:1, :]


def _ada_kernel(c_ref, w_ref, b_ref, *refs):
    n_cast = (len(refs) - 1) // 2
    c = c_ref[...]
    s = (c * _sigmoid(c)).astype(BF16)
    refs[n_cast][...] = jnp.dot(s, w_ref[...].astype(BF16), preferred_element_type=F32) + b_ref[...]
    for src, dst in zip(refs[:n_cast], refs[n_cast + 1:]):
        dst[...] = src[...].astype(BF16)


def _ada(cond, w_ada, b_ada, cast=()):
    n = w_ada.shape[1]
    n_steps = n // ADA_COLS
    cast_specs = []
    for w in cast:
        assert w.ndim == 2 and w.shape[0] % (n_steps * 2 * SUBLANES) == 0
        cast_specs.append(pl.BlockSpec((w.shape[0] // n_steps, w.shape[1]), lambda j: (j, 0)))
    return pl.pallas_call(
        _ada_kernel,
        grid=(n_steps,),
        in_specs=[
            pl.BlockSpec((COND_ROWS, D_MODEL), lambda j: (0, 0)),
            pl.BlockSpec((D_MODEL, ADA_COLS), lambda j: (0, j)),
            pl.BlockSpec((1, ADA_COLS), lambda j: (0, j)),
        ] + cast_specs,
        out_specs=[pl.BlockSpec((COND_ROWS, ADA_COLS), lambda j: (0, j))] + cast_specs,
        out_shape=[jax.ShapeDtypeStruct((COND_ROWS, n), F32)] + [jax.ShapeDtypeStruct(w.shape, BF16) for w in cast],
        name="ada_mod",
    )(cond, w_ada, b_ada, *cast)


def _route(logits):
    lane = lax.broadcasted_iota(I32, logits.shape, 1).astype(F32)
    neg = jnp.float32(-1e30)
    far = jnp.float32(LANES)
    is_g = lane < N_EXP_GROUPS
    gl = jnp.where(is_g, logits, neg)
    gmax = jnp.max(gl, axis=-1, keepdims=True)
    gsel = jnp.min(jnp.where(gl == gmax, lane, far), axis=-1, keepdims=True)
    psel = 1.0 / jnp.sum(jnp.where(is_g, jnp.exp(gl - gmax), 0.0), axis=-1, keepdims=True)
    e_lo = EXPERT_LANE0 + EXP_PER_GROUP * gsel
    el = jnp.where(lane >= e_lo, jnp.where(lane < e_lo + EXP_PER_GROUP, logits, neg), neg)
    v1 = jnp.max(el, axis=-1, keepdims=True)
    i1 = jnp.min(jnp.where(el == v1, lane, far), axis=-1, keepdims=True)
    el2 = jnp.where(lane == i1, neg, el)
    v2 = jnp.max(el2, axis=-1, keepdims=True)
    i2 = jnp.min(jnp.where(el2 == v2, jnp.where(lane == i1, far, lane), far), axis=-1, keepdims=True)
    e2 = jnp.exp(v2 - v1)
    w1 = psel / (1.0 + e2)
    w2 = psel * e2 / (1.0 + e2)
    gate = jnp.where(lane == i1, w1, jnp.where(lane == i2, w2, 0.0))
    a = jnp.minimum(i1, i2) - e_lo
    b = jnp.maximum(i1, i2) - e_lo
    pair = a * (7.0 - a) * 0.5 + (b - a - 1.0)
    return gate, gsel * PAIRS_PER_GROUP + pair


def _mix_kernel(*refs, S, L, P, use_rope, emit_kv, n_cast, n_blocks, U):
    it = iter(refs)
    x_ref = next(it)
    mod_ref = next(it)
    if P:
        ck_ref = next(it)
        cv_ref = next(it)
    if use_rope:
        cos_ref = next(it)
        sneg_ref = next(it)
        spos_ref = next(it)
    (g1_ref, win_ref, qg_ref, kg_ref, wpool_ref, pscale_ref, wa_ref, wb_ref, wo_ref,
     g2_ref, wr_ref) = (next(it) for _ in range(11))
    cast_in = [next(it) for _ in range(n_cast)]
    xmid_ref = next(it)
    h2_ref = next(it)
    gate_ref = next(it)
    oh_ref = next(it)
    if emit_kv:
        knew_ref = next(it)
        vnew_ref = next(it)
    cast_out = [next(it) for _ in range(n_cast)]
    q_s, k_s, v_s, xp_s, h_s, attn_s, xm_s, mod2_s = (next(it) for _ in range(8))

    TM = S * L
    RB = ROW_BLOCK
    nrb = TM // RB
    n_steps = n_blocks // U
    scale = HEAD_DIM ** -0.5
    step = pl.program_id(0)
    block0 = U * jnp.minimum(step, n_steps - 1)
    slot = step % 2

    sh1 = mod_ref[0, 0:1, :]
    gain1 = g1_ref[...] * (1.0 + mod_ref[0, 1:2, :])
    gt1 = mod_ref[0, 2:3, :]
    sh2 = mod_ref[0, 3:4, :]
    gain2 = g2_ref[...] * (1.0 + mod_ref[0, 4:5, :])
    qg = qg_ref[...]
    kg = kg_ref[...]

    def project(r, carry):
        r0 = pl.multiple_of(r * RB, RB)
        s = r0 // L
        o = pl.multiple_of(r0 % L, RB)
        hb = (_rms(x_ref[pl.ds(r0, RB), :]) * gain1 + sh1).astype(BF16)
        h_s[pl.ds(r0, RB), :] = hb
        p1 = jnp.dot(hb, win_ref[:, 0:GATE_COL], preferred_element_type=F32)
        if use_rope:
            cs = cos_ref[pl.ds(o, RB), :]
            sn = sneg_ref[pl.ds(o, RB), :]
            sp = spos_ref[pl.ds(o, RB), :]

        def rope(t):
            return (t * cs + pltpu.roll(t, HEAD_DIM - ROPE_NF, 1) * sn + pltpu.roll(t, ROPE_NF, 1) * sp)

        for hd in range(N_HEADS):
            qh = _rms(p1[:, hd * HEAD_DIM:(hd + 1) * HEAD_DIM]) * qg
            if use_rope:
                qh = rope(qh)
            q_s[hd, pl.ds(r0, RB), :] = qh.astype(BF16)
        for kh in range(N_KV_HEADS):
            c0 = ATTN_W + kh * HEAD_DIM
            kk = _rms(p1[:, c0:c0 + HEAD_DIM]) * kg
            if emit_kv:
                knew_ref[pl.ds(N_KV_HEADS * r0 + kh, RB, stride=N_KV_HEADS), :] = kk
            if use_rope:
                kk = rope(kk)
            k_s[s, pl.ds(P + o, RB), kh * HEAD_DIM:(kh + 1) * HEAD_DIM] = kk.astype(BF16)
        vv = p1[:, ATTN_W + KV_W:ATTN_W + 2 * KV_W]
        if emit_kv:
            for kh in range(N_KV_HEADS):
                vnew_ref[pl.ds(N_KV_HEADS * r0 + kh, RB, stride=N_KV_HEADS), :] = (
                    vv[:, kh * HEAD_DIM:(kh + 1) * HEAD_DIM])
        v_s[s, pl.ds(P + o, RB), :] = vv.astype(BF16)
        xp_s[s, pl.ds(POOL_HALO + o, RB), :] = p1[:, ATTN_W + 2 * KV_W:GATE_COL]
        return carry

    @pl.when(step == 0)
    def _():
        xm_s[1] = jnp.zeros((U * RB, D_MODEL), F32)
        mod2_s[1] = jnp.zeros((2, D_MODEL), F32)

    @pl.when((step < n_steps) & (step % (nrb // U) == 0))
    def _():
        if P:
            k_s[0, 0:P, :] = ck_ref[0].astype(BF16)
            v_s[0, 0:P, :] = cv_ref[0].astype(BF16)
        xp_s[:, 0:POOL_HALO, :] = jnp.zeros((S, POOL_HALO, POOL_W), F32)
        xp_s[:, L + POOL_HALO:L + 2 * POOL_HALO, :] = jnp.zeros((S, POOL_HALO, POOL_W), F32)
        lax.fori_loop(0, TM // RB, project, 0)
        for src, dst in zip(cast_in, cast_out):
            dst[...] = src[...].astype(BF16)

    def mix(u):
        r0 = pl.multiple_of(((block0 + u) % nrb) * RB, RB)
        s = r0 // L
        o = pl.multiple_of(r0 % L, RB)
        attn_u = attn_s.at[u]
        rows = slice(u * RB, (u + 1) * RB)

        for kh in range(N_KV_HEADS):
            k = k_s[s, :, kh * HEAD_DIM:(kh + 1) * HEAD_DIM]
            v = v_s[s, :, kh * HEAD_DIM:(kh + 1) * HEAD_DIM]
            q4 = q_s[kh * GROUP:(kh + 1) * GROUP, pl.ds(r0, RB), :].reshape(GROUP * RB, HEAD_DIM)
            sc = lax.dot_general(q4, k, (((1,), (1,)), ((), ())), preferred_element_type=F32) * scale
            e = jnp.exp(sc - jnp.max(sc, axis=-1, keepdims=True))
            den = jnp.sum(e, axis=-1, keepdims=True)
            o4 = jnp.dot(e.astype(BF16), v, preferred_element_type=F32) / den
            for g in range(GROUP):
                hd = kh * GROUP + g
                attn_u[:, hd * HEAD_DIM:(hd + 1) * HEAD_DIM] = o4[g * RB:(g + 1) * RB].astype(BF16)
        a = jnp.dot(attn_u[...], wa_ref[...], preferred_element_type=F32)

        t = o + lax.broadcasted_iota(I32, (RB, 1), 0)
        RW = RB + 2 * POOL_HALO
        parts = []
        for gi, w in enumerate(POOL_WINDOWS):
            cols = slice(gi * POOL_GC, (gi + 1) * POOL_GC)
            xw = xp_s[s, pl.ds(o, RW), cols]
            run = xw
            span = 1
            while span < w:
                run = run + pltpu.roll(run, span, 0)
                span *= 2
            if w // 2 > 1:
                run = pltpu.roll(run, RW - (w // 2 - 1), 0)
            tot = run[POOL_HALO:POOL_HALO + RB]
            cnt = (jnp.minimum(t + w // 2, L) - jnp.maximum(t - w // 2, 0)).astype(F32)
            parts.append(tot / cnt - xw[POOL_HALO:POOL_HALO + RB])
        dpool = jnp.concatenate(parts, axis=1).astype(BF16)
        pooled = jnp.dot(dpool, wpool_ref[...], preferred_element_type=F32) * pscale_ref[...]
        b = jnp.dot(pooled.astype(BF16), wb_ref[...], preferred_element_type=F32)

        gates = jnp.dot(h_s[pl.ds(r0, RB), :], win_ref[:, GATE_COL:IN_W], preferred_element_type=F32)
        merged = _sigmoid(gates[:, 0:D_MODEL]) * a + _sigmoid(gates[:, D_MODEL:2 * D_MODEL]) * b
        upd = jnp.dot(merged.astype(BF16), wo_ref[...], preferred_element_type=F32)
        xm = x_ref[pl.ds(r0, RB), :] + gt1 * upd
        xmid_ref[rows, :] = xm
        xm_s[slot, rows, :] = xm

    def moe_prep(u):
        rows = slice(u * RB, (u + 1) * RB)
        h2 = _rms(xm_s[1 - slot, rows, :]) * mod2_s[1 - slot, 0:1, :] + mod2_s[1 - slot, 1:2, :]
        hi = h2.astype(BF16)
        lo = (h2 - hi.astype(F32)).astype(BF16)
        l1 = jnp.dot(hi, wr_ref[...], preferred_element_type=F32)
        l2 = jnp.dot(lo, wr_ref[:, 0:LANES], preferred_element_type=F32)
        gate, bucket = _route(l1[:, 0:LANES] + l1[:, LANES:2 * LANES] + l2)
        groups = pl.ds(u * (RB // SUBLANES), RB // SUBLANES)
        _store_tiles(h2_ref.at[groups], _pack_bf16_pairs(h2))
        gate_ref[rows, :] = gate
        lane = lax.broadcasted_iota(I32, (RB, LANES), 1).astype(F32)
        oh_ref[rows, :] = jnp.where(lane == bucket, 1.0, 0.0).astype(BF16)

    mod2_s[slot, 0:1, :] = gain2
    mod2_s[slot, 1:2, :] = sh2
    for u in range(U):
        moe_prep(u)
    for u in range(U):
        mix(u)


def _mix(x2d, mod, mod_row, cache, rope_tabs, weights, *, S, L, emit_kv, blocks_per_step, cast=()):
    T = x2d.shape[0]
    TM = S * L
    P = cache[0].shape[1] if cache is not None else 0
    use_rope = rope_tabs is not None
    assert T % TM == 0 and L % ROW_BLOCK == 0
    assert not (use_rope or P) or S == 1
    Lk = P + L

    args = [x2d, mod]
    nrb = TM // ROW_BLOCK
    n_blocks = T // ROW_BLOCK
    step_rows = blocks_per_step * ROW_BLOCK
    steps_per_group = nrb // blocks_per_step
    n_mix_steps = n_blocks // blocks_per_step
    assert nrb % blocks_per_step == 0

    def mixed(s):
        return jnp.minimum(s, n_mix_steps - 1)

    def group(s):
        return mixed(s) // steps_per_group

    def prepared(s):
        return jnp.maximum(s - 1, 0)

    in_specs = [
        pl.BlockSpec((TM, D_MODEL), lambda s: (group(s), 0)),
        pl.BlockSpec((1, 6, D_MODEL), lambda s: (mod_row(group(s)), 0, 0)),
    ]
    if P:
        args += list(cache)
        in_specs += [pl.BlockSpec((1, P, KV_W), lambda s: (group(s), 0, 0))] * 2
    if use_rope:
        args += list(rope_tabs)
        in_specs += [_resident((L, HEAD_DIM))] * 3
    args += list(weights)
    in_specs += [_resident(w.shape) for w in weights]
    n_steps = T // TM
    cast_specs = []
    for w in cast:
        assert w.shape[0] % n_steps == 0
        blk = (w.shape[0] // n_steps,) + w.shape[1:]
        cast_specs.append(pl.BlockSpec(blk, lambda s, n=len(blk): (group(s),) + (0,) * (n - 1)))
    args += list(cast)
    in_specs += cast_specs

    out_shape = [jax.ShapeDtypeStruct((T, D_MODEL), F32), jax.ShapeDtypeStruct(_tiles_shape(T, PACKED_CHUNKS), U32),
                 jax.ShapeDtypeStruct((T, LANES), F32),
                 jax.ShapeDtypeStruct((T, LANES), BF16)]
    out_specs = [pl.BlockSpec((step_rows, D_MODEL), lambda s: (mixed(s), 0)),
                 _tiles_spec(step_rows, prepared, PACKED_CHUNKS),
                 pl.BlockSpec((step_rows, LANES), lambda s: (prepared(s), 0)),
                 pl.BlockSpec((step_rows, LANES), lambda s: (prepared(s), 0))]
    if emit_kv:
        out_shape += [jax.ShapeDtypeStruct((T * N_KV_HEADS, HEAD_DIM), F32)] * 2
        out_specs += [pl.BlockSpec((TM * N_KV_HEADS, HEAD_DIM), lambda s: (group(s), 0))] * 2
    out_shape += [jax.ShapeDtypeStruct(w.shape, BF16) for w in cast]
    out_specs += cast_specs

    scratch = [
        pltpu.VMEM((N_HEADS, TM, HEAD_DIM), BF16),
        pltpu.VMEM((S, Lk, KV_W), BF16),
        pltpu.VMEM((S, Lk, KV_W), BF16),
        pltpu.VMEM((S, L + 2 * POOL_HALO, POOL_W), F32),
        pltpu.VMEM((TM, D_MODEL), BF16),
        pltpu.VMEM((blocks_per_step, ROW_BLOCK, ATTN_W), BF16),
        pltpu.VMEM((2, step_rows, D_MODEL), F32),
        pltpu.VMEM((2, 2, D_MODEL), F32),
    ]
    kern = functools.partial(_mix_kernel, S=S, L=L, P=P, use_rope=use_rope, emit_kv=emit_kv,
                             n_cast=len(cast), n_blocks=n_blocks, U=blocks_per_step)
    return pl.pallas_call(
        kern,
        grid=(n_mix_steps + 1,),
        in_specs=in_specs,
        out_specs=out_specs,
        out_shape=out_shape,
        scratch_shapes=scratch,
        compiler_params=pltpu.CompilerParams(
            dimension_semantics=("arbitrary",), vmem_limit_bytes=V7X_VMEM_LIMIT_BYTES),
        name="mixer_rope" if use_rope else "mixer_ctx",
    )(*args)


def _plan_kernel(oh_ref, dest_ref, meta_ref, *, n_blocks):
    TB = TOKEN_BLOCK
    lane = lax.broadcasted_iota(I32, (SUBLANES, LANES), 1)

    def count(b, acc):
        oh = oh_ref[pl.ds(pl.multiple_of(b * TB, TB), TB), :].astype(F32)
        return acc + jnp.sum(oh, axis=0, keepdims=True)

    counts = lax.fori_loop(0, n_blocks, count, jnp.zeros((SUBLANES, LANES), F32))
    padded = jnp.ceil(counts * (1.0 / SORT_TILE)) * SORT_TILE
    ends = padded
    step = 1
    while step < LANES:
        ends = ends + jnp.where(lane >= step, pltpu.roll(ends, step, 1), 0.0)
        step *= 2
    starts = ends - padded

    tri = jnp.where(lax.broadcasted_iota(I32, (TB, TB), 1) < lax.broadcasted_iota(I32, (TB, TB), 0),
                    1.0, 0.0).astype(BF16)

    def place(b, seen):
        oh = oh_ref[pl.ds(pl.multiple_of(b * TB, TB), TB), :]
        ohf = oh.astype(F32)
        rank = jnp.dot(tri, oh, preferred_element_type=F32)
        base = (starts + seen)[0:1, :]
        d = jnp.sum(ohf * (rank + base), axis=1, keepdims=True)
        dest_ref[b] = _row(d).astype(I32)
        return seen + jnp.sum(ohf, axis=0, keepdims=True)

    lax.fori_loop(0, n_blocks, place, jnp.zeros((SUBLANES, LANES), F32))

    tile_row0 = lax.broadcasted_iota(I32, (LANES, LANES), 0).astype(F32) * SORT_TILE
    is_bucket = lax.broadcasted_iota(I32, (LANES, LANES), 1) < N_BUCKETS
    done = jnp.sum(jnp.where(is_bucket, jnp.where(ends[0:1, :] <= tile_row0, 1.0, 0.0), 0.0),
                   axis=1, keepdims=True)
    bkt = jnp.minimum(done, N_BUCKETS - 1.0)
    grp = (jnp.where(bkt >= PAIRS_PER_GROUP, 1.0, 0.0) + jnp.where(bkt >= 2 * PAIRS_PER_GROUP, 1.0, 0.0)
           + jnp.where(bkt >= 3 * PAIRS_PER_GROUP, 1.0, 0.0))
    pair = bkt - PAIRS_PER_GROUP * grp
    a = jnp.where(pair >= 3.0, 1.0, 0.0) + jnp.where(pair >= 5.0, 1.0, 0.0)
    b = pair - a * (7.0 - a) * 0.5 + a + 1.0
    e1 = EXP_PER_GROUP * grp + a
    e2 = EXP_PER_GROUP * grp + b
    meta = jnp.concatenate(
        [_row(e1), _row(e2), ends[0:1, :] * (1.0 / SORT_TILE), jnp.zeros((SUBLANES - 3, LANES), F32)], axis=0)
    meta_ref[...] = meta.astype(I32)


def _plan(onehot):
    T = onehot.shape[0]
    n_blocks = T // TOKEN_BLOCK
    dest, meta = pl.pallas_call(
        functools.partial(_plan_kernel, n_blocks=n_blocks),
        out_shape=[jax.ShapeDtypeStruct((n_blocks, 1, TOKEN_BLOCK), I32),
                   jax.ShapeDtypeStruct((SUBLANES, LANES), I32)],
        name="moe_plan",
    )(onehot)
    return dest.reshape(T), meta


def _sc_move_rows(src_v, table_hbm, out_hbm, lo, n_rows, idx_v, pieces_v, sem):
    chunks = pieces_v.shape[0] // SC_ROWS_PER_STEP
    lane = lax.iota(I32, SC_LANES)
    row_in_group = lane & (SUBLANES - 1)
    chunk_in_pair = lane >> 3
    rows_per_gather = SC_PIECES_PER_GATHER // chunks

    @pl.loop(0, n_rows // SC_ROWS_PER_STEP)
    def _(step):
        copies = []
        for g in range(SC_ROWS_PER_STEP // rows_per_gather):
            r0 = step * SC_ROWS_PER_STEP + g * rows_per_gather
            for v in range(SC_PIECES_PER_GATHER // SC_LANES):
                group, chunk0 = v // (chunks // 2), 2 * (v % (chunks // 2))
                tok = plsc.load_gather(src_v, [r0 + group * SUBLANES + row_in_group])
                piece = (tok >> 3) * (SUBLANES * chunks) + (chunk0 + chunk_in_pair) * SUBLANES + (tok & 7)
                idx_v[pl.ds(g * SC_PIECES_PER_GATHER + v * SC_LANES, SC_LANES)] = piece
            window = pl.ds(g * SC_PIECES_PER_GATHER, SC_PIECES_PER_GATHER)
            copies.append(pltpu.async_copy(table_hbm.at[idx_v.at[window]], pieces_v.at[window], sem))
        for cp in copies:
            cp.wait()
        first = pl.multiple_of((lo + step * SC_ROWS_PER_STEP) * chunks, SC_ROWS_PER_STEP * chunks)
        pltpu.sync_copy(pieces_v, out_hbm.at[pl.ds(first, SC_ROWS_PER_STEP * chunks)])


def _sc_scratch(chunks, dtype):
    return [pltpu.VMEM((SC_ROWS_PER_STEP * chunks,), I32), pltpu.VMEM((SC_ROWS_PER_STEP * chunks, LANES), dtype)]


def _sc_dispatch(h2_flat, gate_rows, dest, n_rows):
    T = dest.shape[0]
    per_worker = n_rows // SC_WORKERS
    rows_per_step = SC_ROWS_PER_STEP
    chunks = h2_flat.shape[0] // T
    assert n_rows % SC_WORKERS == 0 and per_worker % rows_per_step == 0 and T % SC_LANES == 0
    mesh = plsc.VectorSubcoreMesh(core_axis_name="c", subcore_axis_name="s")

    @functools.partial(
        pl.kernel, mesh=mesh,
        out_type=[jax.ShapeDtypeStruct((n_rows * chunks, LANES), h2_flat.dtype),
                  jax.ShapeDtypeStruct((n_rows, LANES), F32)],
        scratch_types=[pltpu.VMEM((T,), I32), pltpu.VMEM((per_worker,), I32)]
        + _sc_scratch(chunks, h2_flat.dtype)
        + [pltpu.VMEM((rows_per_step, LANES), F32), pltpu.SemaphoreType.DMA, pltpu.SemaphoreType.DMA],
        compiler_params=pltpu.CompilerParams(use_tc_tiling_on_sc=True, needs_layout_passes=False),
        name="sc_dispatch",
    )
    def dispatch(h2_hbm, gate_hbm, dest_hbm, out_h_hbm, out_g_hbm,
                 dest_v, src_v, idx_v, pieces_v, gates_v, sem_h, sem_g):
        worker = lax.axis_index("s") * SC_CORES + lax.axis_index("c")
        lo = worker * per_worker
        pltpu.sync_copy(dest_hbm, dest_v)

        @pl.loop(0, per_worker // SC_LANES)
        def _(j):
            j0 = pl.multiple_of(j * SC_LANES, SC_LANES)
            src_v[pl.ds(j0, SC_LANES)] = lax.rem(lo + j0 + lax.iota(I32, SC_LANES), T)

        @pl.loop(0, T // SC_LANES)
        def _(j):
            t0 = pl.multiple_of(j * SC_LANES, SC_LANES)
            d = dest_v[pl.ds(t0, SC_LANES)] - lo
            mine = (d >= 0) & (d < per_worker)
            plsc.store_scatter(src_v, [jnp.where(mine, d, 0)], t0 + lax.iota(I32, SC_LANES), mask=mine)

        @pl.loop(0, per_worker // rows_per_step)
        def _(j):
            off = pl.multiple_of(j * rows_per_step, rows_per_step)
            pltpu.async_copy(gate_hbm.at[src_v.at[pl.ds(off, rows_per_step)]], gates_v, sem_g).wait()
            pltpu.sync_copy(gates_v, out_g_hbm.at[pl.ds(lo + off, rows_per_step)])

        _sc_move_rows(src_v, h2_hbm, out_h_hbm, lo, per_worker, idx_v, pieces_v, sem_h)

    return dispatch(h2_flat, gate_rows, dest)


def _expert_kernel(e1s, e2s, n_used, x_ref, gv_ref, wg_ref, wu_ref, wd_ref, o_ref, wup_s, wdn_s):
    groups = SORT_TILE // SUBLANES

    def one_tile(k, carry):
        t = pl.program_id(0) * EXPERT_TILES_PER_STEP + k
        e1 = e1s[t]
        e2 = e2s[t]
        prev = jnp.maximum(t - 1, 0)
        new_pair = (t == 0) | (e1 != e1s[prev]) | (e2 != e2s[prev])
        rows = pl.ds(pl.multiple_of(k * groups, groups), groups)

        @pl.when((t < n_used[0]) & new_pair)
        def _():
            for slot, e in enumerate((e1, e2)):
                wup_s[:, (2 * slot) * D_EXPERT:(2 * slot + 1) * D_EXPERT] = wg_ref[e]
                wup_s[:, (2 * slot + 1) * D_EXPERT:(2 * slot + 2) * D_EXPERT] = wu_ref[e]
                wdn_s[slot * D_EXPERT:(slot + 1) * D_EXPERT, :] = wd_ref[e]

        @pl.when(t < n_used[0])
        def _():
            x = _unpack_bf16_pairs(_load_tiles(x_ref.at[rows]))
            gv = gv_ref[pl.ds(pl.multiple_of(k * SORT_TILE, SORT_TILE), SORT_TILE), :]
            lane = lax.broadcasted_iota(I32, gv.shape, 1)
            h = jnp.dot(x, wup_s[...], preferred_element_type=F32)
            hid = []
            for slot, e in enumerate((e1, e2)):
                ge = jnp.sum(jnp.where(lane == EXPERT_LANE0 + e, gv, 0.0), axis=-1, keepdims=True)
                hg = h[:, (2 * slot) * D_EXPERT:(2 * slot + 1) * D_EXPERT]
                hu = h[:, (2 * slot + 1) * D_EXPERT:(2 * slot + 2) * D_EXPERT]
                hid.append((hg * _sigmoid(hg) * hu * ge).astype(BF16))
            out = jnp.dot(jnp.concatenate(hid, axis=1), wdn_s[...], preferred_element_type=F32)
            _store_tiles(o_ref.at[rows], out)

        @pl.when(t >= n_used[0])
        def _():
            o_ref[rows] = jnp.zeros((groups,) + o_ref.shape[1:], F32)

        return carry

    lax.fori_loop(0, EXPERT_TILES_PER_STEP, one_tile, 0)


def _experts(sorted_h2, sorted_gates, meta, wg, wu, wd):
    n_tiles = sorted_h2.shape[0] * SUBLANES // SORT_TILE
    step_rows = SORT_TILE * EXPERT_TILES_PER_STEP
    assert n_tiles % EXPERT_TILES_PER_STEP == 0

    def last_used(i, e1, e2, nu):
        return jnp.minimum(i, (nu[0] - 1) // EXPERT_TILES_PER_STEP)

    return pl.pallas_call(
        _expert_kernel,
        grid_spec=pltpu.PrefetchScalarGridSpec(
            num_scalar_prefetch=3,
            grid=(n_tiles // EXPERT_TILES_PER_STEP,),
            in_specs=[
                _tiles_spec(step_rows, last_used, PACKED_CHUNKS),
                pl.BlockSpec((step_rows, LANES), lambda *a: (last_used(*a), 0)),
                _resident(wg.shape), _resident(wu.shape), _resident(wd.shape),
            ],
            out_specs=_tiles_spec(step_rows, lambda i, *_: i),
            scratch_shapes=[pltpu.VMEM((D_MODEL, 4 * D_EXPERT), BF16), pltpu.VMEM((2 * D_EXPERT, D_MODEL), BF16)],
        ),
        out_shape=jax.ShapeDtypeStruct(_tiles_shape(n_tiles * SORT_TILE), F32),
        compiler_params=pltpu.CompilerParams(
            dimension_semantics=("arbitrary",), vmem_limit_bytes=V7X_VMEM_LIMIT_BYTES),
        name="moe_experts",
    )(meta[0, :n_tiles], meta[1, :n_tiles], meta[2, LANES - 1:LANES], sorted_h2, sorted_gates, wg, wu, wd)


def _sc_row_gather(table_flat, idx):
    n = idx.shape[0]
    per_worker = n // SC_WORKERS
    chunks = ROW_CHUNKS
    assert n % SC_WORKERS == 0 and per_worker % SC_ROWS_PER_STEP == 0
    mesh = plsc.VectorSubcoreMesh(core_axis_name="c", subcore_axis_name="s")

    @functools.partial(
        pl.kernel, mesh=mesh,
        out_type=jax.ShapeDtypeStruct((n * chunks, LANES), table_flat.dtype),
        scratch_types=[pltpu.VMEM((per_worker,), I32)] + _sc_scratch(chunks, table_flat.dtype)
        + [pltpu.SemaphoreType.DMA],
        compiler_params=pltpu.CompilerParams(use_tc_tiling_on_sc=True, needs_layout_passes=False),
        name="sc_row_gather",
    )
    def gather(table_hbm, idx_hbm, out_hbm, src_v, idx_v, pieces_v, sem):
        worker = lax.axis_index("s") * SC_CORES + lax.axis_index("c")
        lo = worker * per_worker
        pltpu.sync_copy(idx_hbm.at[pl.ds(lo, per_worker)], src_v)
        _sc_move_rows(src_v, table_hbm, out_hbm, lo, per_worker, idx_v, pieces_v, sem)

    return gather(table_flat, idx)


def _final_kernel(x_ref, moe_ref, mod_ref, gf_ref, o_ref):
    y = x_ref[...] + mod_ref[0, 5:6, :] * _load_tiles(moe_ref)
    o_ref[...] = _rms(y) * gf_ref[...]


def _final(xmid, moe_rows, mod, mod_row, gf):
    T = xmid.shape[0]
    return pl.pallas_call(
        _final_kernel,
        grid=(T // FINAL_BLOCK,),
        in_specs=[
            pl.BlockSpec((FINAL_BLOCK, D_MODEL), lambda i: (i, 0)),
            _tiles_spec(FINAL_BLOCK, lambda i: i),
            pl.BlockSpec((1, 6, D_MODEL), lambda i: (mod_row(i), 0, 0)),
            pl.BlockSpec((1, D_MODEL), lambda i: (0, 0)),
        ],
        out_specs=pl.BlockSpec((FINAL_BLOCK, D_MODEL), lambda i: (i, 0)),
        out_shape=jax.ShapeDtypeStruct((T, D_MODEL), F32),
        compiler_params=pltpu.CompilerParams(
            dimension_semantics=("arbitrary",), vmem_limit_bytes=V7X_VMEM_LIMIT_BYTES),
        name="moe_final",
    )(xmid, moe_rows, mod, gf)


def _flat(tiles):
    return tiles.reshape(-1, LANES)


def _moe_dispatch(h2_tiles, gate_rows, onehot):
    T = gate_rows.shape[0]
    n_tiles = T // SORT_TILE + N_BUCKETS
    n_rows = n_tiles * SORT_TILE
    assert n_tiles <= LANES and T % TOKEN_BLOCK == 0
    dest, meta = _plan(onehot)
    sorted_h2, sorted_gates = _sc_dispatch(_flat(h2_tiles), gate_rows, dest, n_rows)
    return sorted_h2.reshape(_tiles_shape(n_rows, PACKED_CHUNKS)), sorted_gates, dest, meta


def _moe_unpermute(moe_sorted_tiles, dest):
    return _sc_row_gather(_flat(moe_sorted_tiles), dest).reshape(_tiles_shape(dest.shape[0]))


def _rope_tables(n_tokens):
    t = jnp.arange(n_tokens)
    row = (t // GRID_W).astype(F32)
    col = (t % GRID_W).astype(F32)
    freq = ROPE_THETA ** (-jnp.arange(ROPE_NF, dtype=F32) / ROPE_NF)
    ang = jnp.concatenate([row[:, None] * freq] * 2 + [col[:, None] * freq] * 2, axis=-1)
    first = (jnp.arange(HEAD_DIM) % (2 * ROPE_NF)) < ROPE_NF
    sin = jnp.sin(ang)
    return jnp.cos(ang), jnp.where(first, -sin, 0.0), jnp.where(first, 0.0, sin)


def kernel(x_prompt, x_sample, cache_k, cache_v, c, c_ctx, norm1_g, norm2_g, w_ada, b_ada, w_in, q_norm_g, k_norm_g, w_pool, pool_scale, w_branch_a, w_branch_b, w_out, w_router_group, w_router_expert, w_exp_gate, w_exp_up, w_exp_down, final_norm_g):
    assert norm1_g.shape[0] == 1, "single-layer trunk"
    B, L_ctx, _ = x_prompt.shape
    Bs, L_lat, _ = x_sample.shape
    P = cache_k.shape[2]
    assert 1 + Bs <= COND_ROWS

    cond = jnp.concatenate([c_ctx[None, :], c, jnp.zeros((COND_ROWS - 1 - Bs, D_MODEL), F32)], axis=0)
    wpool_bd = jax.scipy.linalg.block_diag(*[w_pool[0, g] for g in range(len(POOL_WINDOWS))])
    mod, w_in_b, wpool_b, wa_b, wb_b, wo_b = _ada(
        cond, w_ada[0], b_ada[0][None, :],
        cast=(w_in[0], wpool_bd, w_branch_a[0], w_branch_b[0], w_out[0]))
    mod = mod.reshape(COND_ROWS, 6, D_MODEL)

    wr = jnp.concatenate([w_router_group[0], w_router_expert[0],
                          jnp.zeros((D_MODEL, LANES - N_EXP_GROUPS - N_EXPERTS), F32)], axis=1)
    wr_hi = wr.astype(BF16)
    wr_lo = (wr - wr_hi.astype(F32)).astype(BF16)
    mix_w = (norm1_g[0][None, :], w_in_b, q_norm_g[0][None, :], k_norm_g[0][None, :],
             wpool_b, pool_scale[0][None, :], wa_b, wb_b, wo_b,
             norm2_g[0][None, :], jnp.concatenate([wr_hi, wr_lo], axis=1))
    gf = final_norm_g[None, :]

    xp2 = x_prompt.reshape(B * L_ctx, D_MODEL)
    xmid_p, h2_p, gate_p, oh_p, knew, vnew, wg, wu, wd = _mix(
        xp2, mod, lambda i: 0, None, None, mix_w, S=2, L=L_ctx, emit_kv=True, blocks_per_step=2,
        cast=(w_exp_gate[0], w_exp_up[0], w_exp_down[0]))
    sh_p, sg_p, dest_p, meta_p = _moe_dispatch(h2_p, gate_p, oh_p)

    xs2 = x_sample.reshape(Bs * L_lat, D_MODEL)
    cache = (cache_k[:, 0].reshape(Bs, P, KV_W), cache_v[:, 0].reshape(Bs, P, KV_W))
    xmid_s, h2_s, gate_s, oh_s = _mix(xs2, mod, lambda i: 1 + i, cache, _rope_tables(L_lat), mix_w,
                                      S=1, L=L_lat, emit_kv=False, blocks_per_step=1)
    sh_s, sg_s, dest_s, meta_s = _moe_dispatch(h2_s, gate_s, oh_s)

    moe_p = _moe_unpermute(_experts(sh_p, sg_p, meta_p, wg, wu, wd), dest_p)
    moe_s = _moe_unpermute(_experts(sh_s, sg_s, meta_s, wg, wu, wd), dest_s)
    y_prompt = _final(xmid_p, moe_p, mod, lambda i: 0, gf)
    blocks_per_seq = L_lat // FINAL_BLOCK
    y_sample = _final(xmid_s, moe_s, mod, lambda i: 1 + i // blocks_per_seq, gf)

    return (y_prompt.reshape(B, L_ctx, D_MODEL), y_sample.reshape(Bs, L_lat, D_MODEL),
            knew.reshape(B, 1, L_ctx, N_KV_HEADS, HEAD_DIM), vnew.reshape(B, 1, L_ctx, N_KV_HEADS, HEAD_DIM))
```

```python
import functools

import numpy as np
import jax
import jax.numpy as jnp
from jax import lax
from jax.experimental import pallas as pl
from jax.experimental.pallas import tpu as pltpu
from jax.experimental.pallas import tpu_sc as plsc

F32 = jnp.float32
BF16 = jnp.bfloat16
I32 = jnp.int32
U32 = jnp.uint32

D_MODEL = 1024
HEAD_DIM = 128
N_HEADS = 8
N_KV_HEADS = 2
GROUP = N_HEADS // N_KV_HEADS
ATTN_W = N_HEADS * HEAD_DIM
KV_W = N_KV_HEADS * HEAD_DIM
POOL_WINDOWS = (2, 4, 8, 16)
POOL_GC = 128
POOL_W = POOL_GC * len(POOL_WINDOWS)
IN_W = ATTN_W + 2 * KV_W + POOL_W + 2 * D_MODEL
GATE_COL = ATTN_W + 2 * KV_W + POOL_W
GRID_W = 64
ROPE_THETA = 10000.0
ROPE_NF = HEAD_DIM // 4
N_EXP_GROUPS = 4
EXP_PER_GROUP = 4
N_EXPERTS = 16
D_EXPERT = 256
EPS = 1e-6

LANES = 128
SUBLANES = 8
COND_ROWS = SUBLANES
POOL_HALO = 8
ROW_BLOCK = 256
ADA_COLS = 768
EXPERT_LANE0 = N_EXP_GROUPS
PAIRS_PER_GROUP = EXP_PER_GROUP * (EXP_PER_GROUP - 1) // 2
N_BUCKETS = N_EXP_GROUPS * PAIRS_PER_GROUP
SORT_TILE = 256
EXPERT_TILES_PER_STEP = 4
TOKEN_BLOCK = 1024
FINAL_BLOCK = 1024
ROW_CHUNKS = D_MODEL // LANES
SC_CORES = 2
SC_SUBCORES = 16
SC_WORKERS = SC_CORES * SC_SUBCORES
SC_LANES = 16
SC_PIECES_PER_GATHER = 128
SC_ROWS_PER_STEP = 64
PACKED_CHUNKS = ROW_CHUNKS // 2
V7X_VMEM_LIMIT_BYTES = 56 * 1024 * 1024


def _sigmoid(x):
    return 1.0 / (1.0 + jnp.exp(-x))


def _rms(x):
    return x * lax.rsqrt(jnp.mean(x * x, axis=-1, keepdims=True) + EPS)


def _resident(shape):
    zeros = (0,) * len(shape)
    return pl.BlockSpec(shape, lambda i, *_: zeros, pipeline_mode=pl.Buffered(1))


def _tiles_shape(n, chunks=ROW_CHUNKS):
    return (n // SUBLANES, chunks, SUBLANES, LANES)


def _tiles_spec(n, block_index, chunks=ROW_CHUNKS):
    return pl.BlockSpec(_tiles_shape(n, chunks), lambda *a: (block_index(*a), 0, 0, 0))


def _store_tiles(ref, x):
    for c in range(ref.shape[1]):
        ref[:, c, :, :] = x[:, c * LANES:(c + 1) * LANES].reshape(x.shape[0] // SUBLANES, SUBLANES, LANES)


def _load_tiles(ref):
    n = ref.shape[0] * SUBLANES
    return jnp.concatenate([ref[:, c, :, :].reshape(n, LANES) for c in range(ref.shape[1])], axis=1)


def _pack_bf16_pairs(x):
    bits = pltpu.bitcast(x.astype(BF16).astype(F32), U32)
    w = x.shape[1] // 2
    return bits[:, :w] | (bits[:, w:] >> 16)


def _unpack_bf16_pairs(words):
    hi = pltpu.bitcast(words & jnp.uint32(0xFFFF0000), F32).astype(BF16)
    lo = pltpu.bitcast(words << 16, F32).astype(BF16)
    return jnp.concatenate([hi, lo], axis=1)


def _row(x):
    return jnp.transpose(jnp.broadcast_to(x, (x.shape[0], LANES)))[0:1, :]


def _ada_kernel(c_ref, w_ref, b_ref, *refs):
    n_cast = (len(refs) - 1) // 2
    c = c_ref[...]
    s = (c * _sigmoid(c)).astype(BF16)
    refs[n_cast][...] = jnp.dot(s, w_ref[...].astype(BF16), preferred_element_type=F32) + b_ref[...]
    for src, dst in zip(refs[:n_cast], refs[n_cast + 1:]):
        dst[...] = src[...].astype(BF16)


def _ada(cond, w_ada, b_ada, cast=()):
    n = w_ada.shape[1]
    n_steps = n // ADA_COLS
    cast_specs = []
    for w in cast:
        assert w.ndim == 2 and w.shape[0] % (n_steps * 2 * SUBLANES) == 0
        cast_specs.append(pl.BlockSpec((w.shape[0] // n_steps, w.shape[1]), lambda j: (j, 0)))
    return pl.pallas_call(
        _ada_kernel,
        grid=(n_steps,),
        in_specs=[
            pl.BlockSpec((COND_ROWS, D_MODEL), lambda j: (0, 0)),
            pl.BlockSpec((D_MODEL, ADA_COLS), lambda j: (0, j)),
            pl.BlockSpec((1, ADA_COLS), lambda j: (0, j)),
        ] + cast_specs,
        out_specs=[pl.BlockSpec((COND_ROWS, ADA_COLS), lambda j: (0, j))] + cast_specs,
        out_shape=[jax.ShapeDtypeStruct((COND_ROWS, n), F32)] + [jax.ShapeDtypeStruct(w.shape, BF16) for w in cast],
        name="ada_mod",
    )(cond, w_ada, b_ada, *cast)


def _route(logits):
    lane = lax.broadcasted_iota(I32, logits.shape, 1).astype(F32)
    neg = jnp.float32(-1e30)
    far = jnp.float32(LANES)
    is_g = lane < N_EXP_GROUPS
    gl = jnp.where(is_g, logits, neg)
    gmax = jnp.max(gl, axis=-1, keepdims=True)
    gsel = jnp.min(jnp.where(gl == gmax, lane, far), axis=-1, keepdims=True)
    psel = 1.0 / jnp.sum(jnp.where(is_g, jnp.exp(gl - gmax), 0.0), axis=-1, keepdims=True)
    e_lo = EXPERT_LANE0 + EXP_PER_GROUP * gsel
    el = jnp.where(lane >= e_lo, jnp.where(lane < e_lo + EXP_PER_GROUP, logits, neg), neg)
    v1 = jnp.max(el, axis=-1, keepdims=True)
    i1 = jnp.min(jnp.where(el == v1, lane, far), axis=-1, keepdims=True)
    el2 = jnp.where(lane == i1, neg, el)
    v2 = jnp.max(el2, axis=-1, keepdims=True)
    i2 = jnp.min(jnp.where(el2 == v2, jnp.where(lane == i1, far, lane), far), axis=-1, keepdims=True)
    e2 = jnp.exp(v2 - v1)
    w1 = psel / (1.0 + e2)
    w2 = psel * e2 / (1.0 + e2)
    gate = jnp.where(lane == i1, w1, jnp.where(lane == i2, w2, 0.0))
    a = jnp.minimum(i1, i2) - e_lo
    b = jnp.maximum(i1, i2) - e_lo
    pair = a * (7.0 - a) * 0.5 + (b - a - 1.0)
    return gate, gsel * PAIRS_PER_GROUP + pair


def _mix_kernel(*refs, S, L, P, use_rope, emit_kv, n_cast, n_blocks, U):
    it = iter(refs)
    x_ref = next(it)
    mod_ref = next(it)
    if P:
        ck_ref = next(it)
        cv_ref = next(it)
    if use_rope:
        cos_ref = next(it)
        sneg_ref = next(it)
        spos_ref = next(it)
    (g1_ref, win_ref, qg_ref, kg_ref, wpool_ref, pscale_ref, wa_ref, wb_ref, wo_ref,
     g2_ref, wr_ref) = (next(it) for _ in range(11))
    cast_in = [next(it) for _ in range(n_cast)]
    xmid_ref = next(it)
    h2_ref = next(it)
    gate_ref = next(it)
    oh_ref = next(it)
    if emit_kv:
        knew_ref = next(it)
        vnew_ref = next(it)
    cast_out = [next(it) for _ in range(n_cast)]
    q_s, k_s, v_s, xp_s, h_s, attn_s, xm_s, mod2_s = (next(it) for _ in range(8))

    TM = S * L
    RB = ROW_BLOCK
    nrb = TM // RB
    n_steps = n_blocks // U
    scale = HEAD_DIM ** -0.5
    step = pl.program_id(0)
    block0 = U * jnp.minimum(step, n_steps - 1)
    slot = step % 2

    sh1 = mod_ref[0, 0:1, :]
    gain1 = g1_ref[...] * (1.0 + mod_ref[0, 1:2, :])
    gt1 = mod_ref[0, 2:3, :]
    sh2 = mod_ref[0, 3:4, :]
    gain2 = g2_ref[...] * (1.0 + mod_ref[0, 4:5, :])
    qg = qg_ref[...]
    kg = kg_ref[...]

    def project(r, carry):
        r0 = pl.multiple_of(r * RB, RB)
        s = r0 // L
        o = pl.multiple_of(r0 % L, RB)
        hb = (_rms(x_ref[pl.ds(r0, RB), :]) * gain1 + sh1).astype(BF16)
        h_s[pl.ds(r0, RB), :] = hb
        p1 = jnp.dot(hb, win_ref[:, 0:GATE_COL], preferred_element_type=F32)
        if use_rope:
            cs = cos_ref[pl.ds(o, RB), :]
            sn = sneg_ref[pl.ds(o, RB), :]
            sp = spos_ref[pl.ds(o, RB), :]

        def rope(t):
            return (t * cs + pltpu.roll(t, HEAD_DIM - ROPE_NF, 1) * sn + pltpu.roll(t, ROPE_NF, 1) * sp)

        for hd in range(N_HEADS):
            qh = _rms(p1[:, hd * HEAD_DIM:(hd + 1) * HEAD_DIM]) * qg
            if use_rope:
                qh = rope(qh)
            q_s[hd, pl.ds(r0, RB), :] = qh.astype(BF16)
        for kh in range(N_KV_HEADS):
            c0 = ATTN_W + kh * HEAD_DIM
            kk = _rms(p1[:, c0:c0 + HEAD_DIM]) * kg
            if emit_kv:
                knew_ref[pl.ds(N_KV_HEADS * r0 + kh, RB, stride=N_KV_HEADS), :] = kk
            if use_rope:
                kk = rope(kk)
            k_s[s, pl.ds(P + o, RB), kh * HEAD_DIM:(kh + 1) * HEAD_DIM] = kk.astype(BF16)
        vv = p1[:, ATTN_W + KV_W:ATTN_W + 2 * KV_W]
        if emit_kv:
            for kh in range(N_KV_HEADS):
                vnew_ref[pl.ds(N_KV_HEADS * r0 + kh, RB, stride=N_KV_HEADS), :] = (
                    vv[:, kh * HEAD_DIM:(kh + 1) * HEAD_DIM])
        v_s[s, pl.ds(P + o, RB), :] = vv.astype(BF16)
        xp_s[s, pl.ds(POOL_HALO + o, RB), :] = p1[:, ATTN_W + 2 * KV_W:GATE_COL]
        return carry

    @pl.when(step == 0)
    def _():
        xm_s[1] = jnp.zeros((U * RB, D_MODEL), F32)
        mod2_s[1] = jnp.zeros((2, D_MODEL), F32)

    @pl.when((step < n_steps) & (step % (nrb // U) == 0))
    def _():
        if P:
            k_s[0, 0:P, :] = ck_ref[0].astype(BF16)
            v_s[0, 0:P, :] = cv_ref[0].astype(BF16)
        xp_s[:, 0:POOL_HALO, :] = jnp.zeros((S, POOL_HALO, POOL_W), F32)
        xp_s[:, L + POOL_HALO:L + 2 * POOL_HALO, :] = jnp.zeros((S, POOL_HALO, POOL_W), F32)
        lax.fori_loop(0, TM // RB, project, 0)
        for src, dst in zip(cast_in, cast_out):
            dst[...] = src[...].astype(BF16)

    def mix(u):
        r0 = pl.multiple_of(((block0 + u) % nrb) * RB, RB)
        s = r0 // L
        o = pl.multiple_of(r0 % L, RB)
        attn_u = attn_s.at[u]
        rows = slice(u * RB, (u + 1) * RB)

        for kh in range(N_KV_HEADS):
            k = k_s[s, :, kh * HEAD_DIM:(kh + 1) * HEAD_DIM]
            v = v_s[s, :, kh * HEAD_DIM:(kh + 1) * HEAD_DIM]
            q4 = q_s[kh * GROUP:(kh + 1) * GROUP, pl.ds(r0, RB), :].reshape(GROUP * RB, HEAD_DIM)
            sc = lax.dot_general(q4, k, (((1,), (1,)), ((), ())), preferred_element_type=F32) * scale
            e = jnp.exp(sc - jnp.max(sc, axis=-1, keepdims=True))
            den = jnp.sum(e, axis=-1, keepdims=True)
            o4 = jnp.dot(e.astype(BF16), v, preferred_element_type=F32) / den
            for g in range(GROUP):
                hd = kh * GROUP + g
                attn_u[:, hd * HEAD_DIM:(hd + 1) * HEAD_DIM] = o4[g * RB:(g + 1) * RB].astype(BF16)
        a = jnp.dot(attn_u[...], wa_ref[...], preferred_element_type=F32)

        t = o + lax.broadcasted_iota(I32, (RB, 1), 0)
        RW = RB + 2 * POOL_HALO
        parts = []
        for gi, w in enumerate(POOL_WINDOWS):
            cols = slice(gi * POOL_GC, (gi + 1) * POOL_GC)
            xw = xp_s[s, pl.ds(o, RW), cols]
            run = xw
            span = 1
            while span < w:
                run = run + pltpu.roll(run, span, 0)
                span *= 2
            if w // 2 > 1:
                run = pltpu.roll(run, RW - (w // 2 - 1), 0)
            tot = run[POOL_HALO:POOL_HALO + RB]
            cnt = (jnp.minimum(t + w // 2, L) - jnp.maximum(t - w // 2, 0)).astype(F32)
            parts.append(tot / cnt - xw[POOL_HALO:POOL_HALO + RB])
        dpool = jnp.concatenate(parts, axis=1).astype(BF16)
        pooled = jnp.dot(dpool, wpool_ref[...], preferred_element_type=F32) * pscale_ref[...]
        b = jnp.dot(pooled.astype(BF16), wb_ref[...], preferred_element_type=F32)

        gates = jnp.dot(h_s[pl.ds(r0, RB), :], win_ref[:, GATE_COL:IN_W], preferred_element_type=F32)
        merged = _sigmoid(gates[:, 0:D_MODEL]) * a + _sigmoid(gates[:, D_MODEL:2 * D_MODEL]) * b
        upd = jnp.dot(merged.astype(BF16), wo_ref[...], preferred_element_type=F32)
        xm = x_ref[pl.ds(r0, RB), :] + gt1 * upd
        xmid_ref[rows, :] = xm
        xm_s[slot, rows, :] = xm

    def moe_prep(u):
        rows = slice(u * RB, (u + 1) * RB)
        h2 = _rms(xm_s[1 - slot, rows, :]) * mod2_s[1 - slot, 0:1, :] + mod2_s[1 - slot, 1:2, :]
        hi = h2.astype(BF16)
        lo = (h2 - hi.astype(F32)).astype(BF16)
        l1 = jnp.dot(hi, wr_ref[...], preferred_element_type=F32)
        l2 = jnp.dot(lo, wr_ref[:, 0:LANES], preferred_element_type=F32)
        gate, bucket = _route(l1[:, 0:LANES] + l1[:, LANES:2 * LANES] + l2)
        groups = pl.ds(u * (RB // SUBLANES), RB // SUBLANES)
        _store_tiles(h2_ref.at[groups], _pack_bf16_pairs(h2))
        gate_ref[rows, :] = gate
        lane = lax.broadcasted_iota(I32, (RB, LANES), 1).astype(F32)
        oh_ref[rows, :] = jnp.where(lane == bucket, 1.0, 0.0).astype(BF16)

    mod2_s[slot, 0:1, :] = gain2
    mod2_s[slot, 1:2, :] = sh2
    for u in range(U):
        moe_prep(u)
    for u in range(U):
        mix(u)


def _mix(x2d, mod, mod_row, cache, rope_tabs, weights, *, S, L, emit_kv, blocks_per_step, cast=()):
    T = x2d.shape[0]
    TM = S * L
    P = cache[0].shape[1] if cache is not None else 0
    use_rope = rope_tabs is not None
    assert T % TM == 0 and L % ROW_BLOCK == 0
    assert not (use_rope or P) or S == 1
    Lk = P + L

    args = [x2d, mod]
    nrb = TM // ROW_BLOCK
    n_blocks = T // ROW_BLOCK
    step_rows = blocks_per_step * ROW_BLOCK
    steps_per_group = nrb // blocks_per_step
    n_mix_steps = n_blocks // blocks_per_step
    assert nrb % blocks_per_step == 0

    def mixed(s):
        return jnp.minimum(s, n_mix_steps - 1)

    def group(s):
        return mixed(s) // steps_per_group

    def prepared(s):
        return jnp.maximum(s - 1, 0)

    in_specs = [
        pl.BlockSpec((TM, D_MODEL), lambda s: (group(s), 0)),
        pl.BlockSpec((1, 6, D_MODEL), lambda s: (mod_row(group(s)), 0, 0)),
    ]
    if P:
        args += list(cache)
        in_specs += [pl.BlockSpec((1, P, KV_W), lambda s: (group(s), 0, 0))] * 2
    if use_rope:
        args += list(rope_tabs)
        in_specs += [_resident((L, HEAD_DIM))] * 3
    args += list(weights)
    in_specs += [_resident(w.shape) for w in weights]
    n_steps = T // TM
    cast_specs = []
    for w in cast:
        assert w.shape[0] % n_steps == 0
        blk = (w.shape[0] // n_steps,) + w.shape[1:]
        cast_specs.append(pl.BlockSpec(blk, lambda s, n=len(blk): (group(s),) + (0,) * (n - 1)))
    args += list(cast)
    in_specs += cast_specs

    out_shape = [jax.ShapeDtypeStruct((T, D_MODEL), F32), jax.ShapeDtypeStruct(_tiles_shape(T, PACKED_CHUNKS), U32),
                 jax.ShapeDtypeStruct((T, LANES), F32),
                 jax.ShapeDtypeStruct((T, LANES), BF16)]
    out_specs = [pl.BlockSpec((step_rows, D_MODEL), lambda s: (mixed(s), 0)),
                 _tiles_spec(step_rows, prepared, PACKED_CHUNKS),
                 pl.BlockSpec((step_rows, LANES), lambda s: (prepared(s), 0)),
                 pl.BlockSpec((step_rows, LANES), lambda s: (prepared(s), 0))]
    if emit_kv:
        out_shape += [jax.ShapeDtypeStruct((T * N_KV_HEADS, HEAD_DIM), F32)] * 2
        out_specs += [pl.BlockSpec((TM * N_KV_HEADS, HEAD_DIM), lambda s: (group(s), 0))] * 2
    out_shape += [jax.ShapeDtypeStruct(w.shape, BF16) for w in cast]
    out_specs += cast_specs

    scratch = [
        pltpu.VMEM((N_HEADS, TM, HEAD_DIM), BF16),
        pltpu.VMEM((S, Lk, KV_W), BF16),
        pltpu.VMEM((S, Lk, KV_W), BF16),
        pltpu.VMEM((S, L + 2 * POOL_HALO, POOL_W), F32),
        pltpu.VMEM((TM, D_MODEL), BF16),
        pltpu.VMEM((blocks_per_step, ROW_BLOCK, ATTN_W), BF16),
        pltpu.VMEM((2, step_rows, D_MODEL), F32),
        pltpu.VMEM((2, 2, D_MODEL), F32),
    ]
    kern = functools.partial(_mix_kernel, S=S, L=L, P=P, use_rope=use_rope, emit_kv=emit_kv,
                             n_cast=len(cast), n_blocks=n_blocks, U=blocks_per_step)
    return pl.pallas_call(
        kern,
        grid=(n_mix_steps + 1,),
        in_specs=in_specs,
        out_specs=out_specs,
        out_shape=out_shape,
        scratch_shapes=scratch,
        compiler_params=pltpu.CompilerParams(
            dimension_semantics=("arbitrary",), vmem_limit_bytes=V7X_VMEM_LIMIT_BYTES),
        name="mixer_rope" if use_rope else "mixer_ctx",
    )(*args)


def _plan_kernel(oh_ref, dest_ref, meta_ref, *, n_blocks):
    TB = TOKEN_BLOCK
    lane = lax.broadcasted_iota(I32, (SUBLANES, LANES), 1)

    def count(b, acc):
        oh = oh_ref[pl.ds(pl.multiple_of(b * TB, TB), TB), :].astype(F32)
        return acc + jnp.sum(oh, axis=0, keepdims=True)

    counts = lax.fori_loop(0, n_blocks, count, jnp.zeros((SUBLANES, LANES), F32))
    padded = jnp.ceil(counts * (1.0 / SORT_TILE)) * SORT_TILE
    ends = padded
    step = 1
    while step < LANES:
        ends = ends + jnp.where(lane >= step, pltpu.roll(ends, step, 1), 0.0)
        step *= 2
    starts = ends - padded

    tri = jnp.where(lax.broadcasted_iota(I32, (TB, TB), 1) < lax.broadcasted_iota(I32, (TB, TB), 0),
                    1.0, 0.0).astype(BF16)

    def place(b, seen):
        oh = oh_ref[pl.ds(pl.multiple_of(b * TB, TB), TB), :]
        ohf = oh.astype(F32)
        rank = jnp.dot(tri, oh, preferred_element_type=F32)
        base = (starts + seen)[0:1, :]
        d = jnp.sum(ohf * (rank + base), axis=1, keepdims=True)
        dest_ref[b] = _row(d).astype(I32)
        return seen + jnp.sum(ohf, axis=0, keepdims=True)

    lax.fori_loop(0, n_blocks, place, jnp.zeros((SUBLANES, LANES), F32))

    tile_row0 = lax.broadcasted_iota(I32, (LANES, LANES), 0).astype(F32) * SORT_TILE
    is_bucket = lax.broadcasted_iota(I32, (LANES, LANES), 1) < N_BUCKETS
    done = jnp.sum(jnp.where(is_bucket, jnp.where(ends[0:1, :] <= tile_row0, 1.0, 0.0), 0.0),
                   axis=1, keepdims=True)
    bkt = jnp.minimum(done, N_BUCKETS - 1.0)
    grp = (jnp.where(bkt >= PAIRS_PER_GROUP, 1.0, 0.0) + jnp.where(bkt >= 2 * PAIRS_PER_GROUP, 1.0, 0.0)
           + jnp.where(bkt >= 3 * PAIRS_PER_GROUP, 1.0, 0.0))
    pair = bkt - PAIRS_PER_GROUP * grp
    a = jnp.where(pair >= 3.0, 1.0, 0.0) + jnp.where(pair >= 5.0, 1.0, 0.0)
    b = pair - a * (7.0 - a) * 0.5 + a + 1.0
    e1 = EXP_PER_GROUP * grp + a
    e2 = EXP_PER_GROUP * grp + b
    meta = jnp.concatenate(
        [_row(e1), _row(e2), ends[0:1, :] * (1.0 / SORT_TILE), jnp.zeros((SUBLANES - 3, LANES), F32)], axis=0)
    meta_ref[...] = meta.astype(I32)


def _plan(onehot):
    T = onehot.shape[0]
    n_blocks = T // TOKEN_BLOCK
    dest, meta = pl.pallas_call(
        functools.partial(_plan_kernel, n_blocks=n_blocks),
        out_shape=[jax.ShapeDtypeStruct((n_blocks, 1, TOKEN_BLOCK), I32),
                   jax.ShapeDtypeStruct((SUBLANES, LANES), I32)],
        name="moe_plan",
    )(onehot)
    return dest.reshape(T), meta


def _sc_move_rows(src_v, table_hbm, out_hbm, lo, n_rows, idx_v, pieces_v, sem):
    chunks = pieces_v.shape[0] // SC_ROWS_PER_STEP
    lane = lax.iota(I32, SC_LANES)
    row_in_group = lane & (SUBLANES - 1)
    chunk_in_pair = lane >> 3
    rows_per_gather = SC_PIECES_PER_GATHER // chunks

    @pl.loop(0, n_rows // SC_ROWS_PER_STEP)
    def _(step):
        copies = []
        for g in range(SC_ROWS_PER_STEP // rows_per_gather):
            r0 = step * SC_ROWS_PER_STEP + g * rows_per_gather
            for v in range(SC_PIECES_PER_GATHER // SC_LANES):
                group, chunk0 = v // (chunks // 2), 2 * (v % (chunks // 2))
                tok = plsc.load_gather(src_v, [r0 + group * SUBLANES + row_in_group])
                piece = (tok >> 3) * (SUBLANES * chunks) + (chunk0 + chunk_in_pair) * SUBLANES + (tok & 7)
                idx_v[pl.ds(g * SC_PIECES_PER_GATHER + v * SC_LANES, SC_LANES)] = piece
            window = pl.ds(g * SC_PIECES_PER_GATHER, SC_PIECES_PER_GATHER)
            copies.append(pltpu.async_copy(table_hbm.at[idx_v.at[window]], pieces_v.at[window], sem))
        for cp in copies:
            cp.wait()
        first = pl.multiple_of((lo + step * SC_ROWS_PER_STEP) * chunks, SC_ROWS_PER_STEP * chunks)
        pltpu.sync_copy(pieces_v, out_hbm.at[pl.ds(first, SC_ROWS_PER_STEP * chunks)])


def _sc_scratch(chunks, dtype):
    return [pltpu.VMEM((SC_ROWS_PER_STEP * chunks,), I32), pltpu.VMEM((SC_ROWS_PER_STEP * chunks, LANES), dtype)]


def _sc_dispatch(h2_flat, gate_rows, dest, n_rows):
    T = dest.shape[0]
    per_worker = n_rows // SC_WORKERS
    rows_per_step = SC_ROWS_PER_STEP
    chunks = h2_flat.shape[0] // T
    assert n_rows % SC_WORKERS == 0 and per_worker % rows_per_step == 0 and T % SC_LANES == 0
    mesh = plsc.VectorSubcoreMesh(core_axis_name="c", subcore_axis_name="s")

    @functools.partial(
        pl.kernel, mesh=mesh,
        out_type=[jax.ShapeDtypeStruct((n_rows * chunks, LANES), h2_flat.dtype),
                  jax.ShapeDtypeStruct((n_rows, LANES), F32)],
        scratch_types=[pltpu.VMEM((T,), I32), pltpu.VMEM((per_worker,), I32)]
        + _sc_scratch(chunks, h2_flat.dtype)
        + [pltpu.VMEM((rows_per_step, LANES), F32), pltpu.SemaphoreType.DMA, pltpu.SemaphoreType.DMA],
        compiler_params=pltpu.CompilerParams(use_tc_tiling_on_sc=True, needs_layout_passes=False),
        name="sc_dispatch",
    )
    def dispatch(h2_hbm, gate_hbm, dest_hbm, out_h_hbm, out_g_hbm,
                 dest_v, src_v, idx_v, pieces_v, gates_v, sem_h, sem_g):
        worker = lax.axis_index("s") * SC_CORES + lax.axis_index("c")
        lo = worker * per_worker
        pltpu.sync_copy(dest_hbm, dest_v)

        @pl.loop(0, per_worker // SC_LANES)
        def _(j):
            j0 = pl.multiple_of(j * SC_LANES, SC_LANES)
            src_v[pl.ds(j0, SC_LANES)] = lax.rem(lo + j0 + lax.iota(I32, SC_LANES), T)

        @pl.loop(0, T // SC_LANES)
        def _(j):
            t0 = pl.multiple_of(j * SC_LANES, SC_LANES)
            d = dest_v[pl.ds(t0, SC_LANES)] - lo
            mine = (d >= 0) & (d < per_worker)
            plsc.store_scatter(src_v, [jnp.where(mine, d, 0)], t0 + lax.iota(I32, SC_LANES), mask=mine)

        @pl.loop(0, per_worker // rows_per_step)
        def _(j):
            off = pl.multiple_of(j * rows_per_step, rows_per_step)
            pltpu.async_copy(gate_hbm.at[src_v.at[pl.ds(off, rows_per_step)]], gates_v, sem_g).wait()
            pltpu.sync_copy(gates_v, out_g_hbm.at[pl.ds(lo + off, rows_per_step)])

        _sc_move_rows(src_v, h2_hbm, out_h_hbm, lo, per_worker, idx_v, pieces_v, sem_h)

    return dispatch(h2_flat, gate_rows, dest)


def _expert_kernel(e1s, e2s, n_used, x_ref, gv_ref, wg_ref, wu_ref, wd_ref, o_ref, wup_s, wdn_s):
    groups = SORT_TILE // SUBLANES

    def one_tile(k, carry):
        t = pl.program_id(0) * EXPERT_TILES_PER_STEP + k
        e1 = e1s[t]
        e2 = e2s[t]
        prev = jnp.maximum(t - 1, 0)
        new_pair = (t == 0) | (e1 != e1s[prev]) | (e2 != e2s[prev])
        rows = pl.ds(pl.multiple_of(k * groups, groups), groups)

        @pl.when((t < n_used[0]) & new_pair)
        def _():
            for slot, e in enumerate((e1, e2)):
                wup_s[:, (2 * slot) * D_EXPERT:(2 * slot + 1) * D_EXPERT] = wg_ref[e]
                wup_s[:, (2 * slot + 1) * D_EXPERT:(2 * slot + 2) * D_EXPERT] = wu_ref[e]
                wdn_s[slot * D_EXPERT:(slot + 1) * D_EXPERT, :] = wd_ref[e]

        @pl.when(t < n_used[0])
        def _():
            x = _unpack_bf16_pairs(_load_tiles(x_ref.at[rows]))
            gv = gv_ref[pl.ds(pl.multiple_of(k * SORT_TILE, SORT_TILE), SORT_TILE), :]
            lane = lax.broadcasted_iota(I32, gv.shape, 1)
            h = jnp.dot(x, wup_s[...], preferred_element_type=F32)
            hid = []
            for slot, e in enumerate((e1, e2)):
                ge = jnp.sum(jnp.where(lane == EXPERT_LANE0 + e, gv, 0.0), axis=-1, keepdims=True)
                hg = h[:, (2 * slot) * D_EXPERT:(2 * slot + 1) * D_EXPERT]
                hu = h[:, (2 * slot + 1) * D_EXPERT:(2 * slot + 2) * D_EXPERT]
                hid.append((hg * _sigmoid(hg) * hu * ge).astype(BF16))
            out = jnp.dot(jnp.concatenate(hid, axis=1), wdn_s[...], preferred_element_type=F32)
            _store_tiles(o_ref.at[rows], _pack_bf16_pairs(out))

        @pl.when(t >= n_used[0])
        def _():
            o_ref[rows] = jnp.zeros((groups,) + o_ref.shape[1:], U32)

        return carry

    lax.fori_loop(0, EXPERT_TILES_PER_STEP, one_tile, 0)


def _experts(sorted_h2, sorted_gates, meta, wg, wu, wd):
    n_tiles = sorted_h2.shape[0] * SUBLANES // SORT_TILE
    step_rows = SORT_TILE * EXPERT_TILES_PER_STEP
    assert n_tiles % EXPERT_TILES_PER_STEP == 0

    def last_used(i, e1, e2, nu):
        return jnp.minimum(i, (nu[0] - 1) // EXPERT_TILES_PER_STEP)

    return pl.pallas_call(
        _expert_kernel,
        grid_spec=pltpu.PrefetchScalarGridSpec(
            num_scalar_prefetch=3,
            grid=(n_tiles // EXPERT_TILES_PER_STEP,),
            in_specs=[
                _tiles_spec(step_rows, last_used, PACKED_CHUNKS),
                pl.BlockSpec((step_rows, LANES), lambda *a: (last_used(*a), 0)),
                _resident(wg.shape), _resident(wu.shape), _resident(wd.shape),
            ],
            out_specs=_tiles_spec(step_rows, lambda i, *_: i, PACKED_CHUNKS),
            scratch_shapes=[pltpu.VMEM((D_MODEL, 4 * D_EXPERT), BF16), pltpu.VMEM((2 * D_EXPERT, D_MODEL), BF16)],
        ),
        out_shape=jax.ShapeDtypeStruct(_tiles_shape(n_tiles * SORT_TILE, PACKED_CHUNKS), U32),
        compiler_params=pltpu.CompilerParams(
            dimension_semantics=("arbitrary",), vmem_limit_bytes=V7X_VMEM_LIMIT_BYTES),
        name="moe_experts",
    )(meta[0, :n_tiles], meta[1, :n_tiles], meta[2, LANES - 1:LANES], sorted_h2, sorted_gates, wg, wu, wd)


def _sc_row_gather(table_flat, idx, chunks):
    n = idx.shape[0]
    per_worker = n // SC_WORKERS
    assert n % SC_WORKERS == 0 and per_worker % SC_ROWS_PER_STEP == 0
    mesh = plsc.VectorSubcoreMesh(core_axis_name="c", subcore_axis_name="s")

    @functools.partial(
        pl.kernel, mesh=mesh,
        out_type=jax.ShapeDtypeStruct((n * chunks, LANES), table_flat.dtype),
        scratch_types=[pltpu.VMEM((per_worker,), I32)] + _sc_scratch(chunks, table_flat.dtype)
        + [pltpu.SemaphoreType.DMA],
        compiler_params=pltpu.CompilerParams(use_tc_tiling_on_sc=True, needs_layout_passes=False),
        name="sc_row_gather",
    )
    def gather(table_hbm, idx_hbm, out_hbm, src_v, idx_v, pieces_v, sem):
        worker = lax.axis_index("s") * SC_CORES + lax.axis_index("c")
        lo = worker * per_worker
        pltpu.sync_copy(idx_hbm.at[pl.ds(lo, per_worker)], src_v)
        _sc_move_rows(src_v, table_hbm, out_hbm, lo, per_worker, idx_v, pieces_v, sem)

    return gather(table_flat, idx)


def _final_kernel(x_ref, moe_ref, mod_ref, gf_ref, o_ref):
    y = x_ref[...] + mod_ref[0, 5:6, :] * _unpack_bf16_pairs(_load_tiles(moe_ref)).astype(F32)
    o_ref[...] = _rms(y) * gf_ref[...]


def _final(xmid, moe_rows, mod, mod_row, gf):
    T = xmid.shape[0]
    return pl.pallas_call(
        _final_kernel,
        grid=(T // FINAL_BLOCK,),
        in_specs=[
            pl.BlockSpec((FINAL_BLOCK, D_MODEL), lambda i: (i, 0)),
            _tiles_spec(FINAL_BLOCK, lambda i: i, PACKED_CHUNKS),
            pl.BlockSpec((1, 6, D_MODEL), lambda i: (mod_row(i), 0, 0)),
            pl.BlockSpec((1, D_MODEL), lambda i: (0, 0)),
        ],
        out_specs=pl.BlockSpec((FINAL_BLOCK, D_MODEL), lambda i: (i, 0)),
        out_shape=jax.ShapeDtypeStruct((T, D_MODEL), F32),
        compiler_params=pltpu.CompilerParams(
            dimension_semantics=("arbitrary",), vmem_limit_bytes=V7X_VMEM_LIMIT_BYTES),
        name="moe_final",
    )(xmid, moe_rows, mod, gf)


def _flat(tiles):
    return tiles.reshape(-1, LANES)


def _moe_dispatch(h2_tiles, gate_rows, onehot):
    T = gate_rows.shape[0]
    n_tiles = T // SORT_TILE + N_BUCKETS
    n_rows = n_tiles * SORT_TILE
    assert n_tiles <= LANES and T % TOKEN_BLOCK == 0
    dest, meta = _plan(onehot)
    sorted_h2, sorted_gates = _sc_dispatch(_flat(h2_tiles), gate_rows, dest, n_rows)
    return sorted_h2.reshape(_tiles_shape(n_rows, PACKED_CHUNKS)), sorted_gates, dest, meta


def _moe_unpermute(moe_sorted_tiles, dest):
    chunks = moe_sorted_tiles.shape[1]
    return _sc_row_gather(_flat(moe_sorted_tiles), dest, chunks).reshape(_tiles_shape(dest.shape[0], chunks))


def _rope_tables(n_tokens):
    t = np.arange(n_tokens)
    row = (t // GRID_W).astype(np.float32)
    col = (t % GRID_W).astype(np.float32)
    freq = np.float32(ROPE_THETA) ** (-np.arange(ROPE_NF, dtype=np.float32) / np.float32(ROPE_NF))
    ang = np.concatenate([row[:, None] * freq] * 2 + [col[:, None] * freq] * 2, axis=-1)
    first = (np.arange(HEAD_DIM) % (2 * ROPE_NF)) < ROPE_NF
    sin = np.sin(ang)
    zero = np.float32(0.0)
    return (jnp.asarray(np.cos(ang)), jnp.asarray(np.where(first, -sin, zero)),
            jnp.asarray(np.where(first, zero, sin)))


def kernel(x_prompt, x_sample, cache_k, cache_v, c, c_ctx, norm1_g, norm2_g, w_ada, b_ada, w_in, q_norm_g, k_norm_g, w_pool, pool_scale, w_branch_a, w_branch_b, w_out, w_router_group, w_router_expert, w_exp_gate, w_exp_up, w_exp_down, final_norm_g):
    assert norm1_g.shape[0] == 1, "single-layer trunk"
    B, L_ctx, _ = x_prompt.shape
    Bs, L_lat, _ = x_sample.shape
    P = cache_k.shape[2]
    assert 1 + Bs <= COND_ROWS

    cond = jnp.concatenate([c_ctx[None, :], c, jnp.zeros((COND_ROWS - 1 - Bs, D_MODEL), F32)], axis=0)
    wpool_bd = jax.scipy.linalg.block_diag(*[w_pool[0, g] for g in range(len(POOL_WINDOWS))])
    mod, w_in_b, wpool_b, wa_b, wb_b, wo_b = _ada(
        cond, w_ada[0], b_ada[0][None, :],
        cast=(w_in[0], wpool_bd, w_branch_a[0], w_branch_b[0], w_out[0]))
    mod = mod.reshape(COND_ROWS, 6, D_MODEL)

    wr = jnp.concatenate([w_router_group[0], w_router_expert[0],
                          jnp.zeros((D_MODEL, LANES - N_EXP_GROUPS - N_EXPERTS), F32)], axis=1)
    wr_hi = wr.astype(BF16)
    wr_lo = (wr - wr_hi.astype(F32)).astype(BF16)
    mix_w = (norm1_g[0][None, :], w_in_b, q_norm_g[0][None, :], k_norm_g[0][None, :],
             wpool_b, pool_scale[0][None, :], wa_b, wb_b, wo_b,
             norm2_g[0][None, :], jnp.concatenate([wr_hi, wr_lo], axis=1))
    gf = final_norm_g[None, :]

    xp2 = x_prompt.reshape(B * L_ctx, D_MODEL)
    xmid_p, h2_p, gate_p, oh_p, knew, vnew, wg, wu, wd = _mix(
        xp2, mod, lambda i: 0, None, None, mix_w, S=2, L=L_ctx, emit_kv=True, blocks_per_step=2,
        cast=(w_exp_gate[0], w_exp_up[0], w_exp_down[0]))
    sh_p, sg_p, dest_p, meta_p = _moe_dispatch(h2_p, gate_p, oh_p)

    xs2 = x_sample.reshape(Bs * L_lat, D_MODEL)
    cache = (cache_k[:, 0].reshape(Bs, P, KV_W), cache_v[:, 0].reshape(Bs, P, KV_W))
    xmid_s, h2_s, gate_s, oh_s = _mix(xs2, mod, lambda i: 1 + i, cache, _rope_tables(L_lat), mix_w,
                                      S=1, L=L_lat, emit_kv=False, blocks_per_step=1)
    sh_s, sg_s, dest_s, meta_s = _moe_dispatch(h2_s, gate_s, oh_s)

    moe_p = _moe_unpermute(_experts(sh_p, sg_p, meta_p, wg, wu, wd), dest_p)
    moe_s = _moe_unpermute(_experts(sh_s, sg_s, meta_s, wg, wu, wd), dest_s)
    y_prompt = _final(xmid_p, moe_p, mod, lambda i: 0, gf)
    blocks_per_seq = L_lat // FINAL_BLOCK
    y_sample = _final(xmid_s, moe_s, mod, lambda i: 1 + i // blocks_per_seq, gf)

    return (y_prompt.reshape(B, L_ctx, D_MODEL), y_sample.reshape(Bs, L_lat, D_MODEL),
            knew.reshape(B, 1, L_ctx, N_KV_HEADS, HEAD_DIM), vnew.reshape(B, 1, L_ctx, N_KV_HEADS, HEAD_DIM))
```

```python
import functools

import numpy as np
import jax
import jax.numpy as jnp
from jax import lax
from jax.experimental import pallas as pl
from jax.experimental.pallas import tpu as pltpu
from jax.experimental.pallas import tpu_sc as plsc

F32 = jnp.float32
BF16 = jnp.bfloat16
I32 = jnp.int32
U32 = jnp.uint32

D_MODEL = 1024
HEAD_DIM = 128
N_HEADS = 8
N_KV_HEADS = 2
GROUP = N_HEADS // N_KV_HEADS
ATTN_W = N_HEADS * HEAD_DIM
KV_W = N_KV_HEADS * HEAD_DIM
POOL_WINDOWS = (2, 4, 8, 16)
POOL_GC = 128
POOL_W = POOL_GC * len(POOL_WINDOWS)
IN_W = ATTN_W + 2 * KV_W + POOL_W + 2 * D_MODEL
GATE_COL = ATTN_W + 2 * KV_W + POOL_W
GRID_W = 64
ROPE_THETA = 10000.0
ROPE_NF = HEAD_DIM // 4
N_EXP_GROUPS = 4
EXP_PER_GROUP = 4
N_EXPERTS = 16
D_EXPERT = 256
EPS = 1e-6

LANES = 128
SUBLANES = 8
COND_ROWS = SUBLANES
POOL_HALO = 8
ROW_BLOCK = 256
ADA_COLS = 768
EXPERT_LANE0 = N_EXP_GROUPS
PAIRS_PER_GROUP = EXP_PER_GROUP * (EXP_PER_GROUP - 1) // 2
N_BUCKETS = N_EXP_GROUPS * PAIRS_PER_GROUP
SORT_TILE = 256
EXPERT_TILES_PER_STEP = 4
TOKEN_BLOCK = 1024
FINAL_BLOCK = 1024
ROW_CHUNKS = D_MODEL // LANES
SC_CORES = 2
SC_SUBCORES = 16
SC_WORKERS = SC_CORES * SC_SUBCORES
SC_LANES = 16
SC_PIECES_PER_GATHER = 128
SC_ROWS_PER_STEP = 64
PACKED_CHUNKS = ROW_CHUNKS // 2
V7X_VMEM_LIMIT_BYTES = 56 * 1024 * 1024


def _sigmoid(x):
    return 1.0 / (1.0 + jnp.exp(-x))


def _rms(x):
    return x * lax.rsqrt(jnp.mean(x * x, axis=-1, keepdims=True) + EPS)


def _resident(shape):
    zeros = (0,) * len(shape)
    return pl.BlockSpec(shape, lambda i, *_: zeros, pipeline_mode=pl.Buffered(1))


def _tiles_shape(n, chunks=ROW_CHUNKS):
    return (n // SUBLANES, chunks, SUBLANES, LANES)


def _tiles_spec(n, block_index, chunks=ROW_CHUNKS):
    return pl.BlockSpec(_tiles_shape(n, chunks), lambda *a: (block_index(*a), 0, 0, 0))


def _store_tiles(ref, x):
    for c in range(ref.shape[1]):
        ref[:, c, :, :] = x[:, c * LANES:(c + 1) * LANES].reshape(x.shape[0] // SUBLANES, SUBLANES, LANES)


def _load_tiles(ref):
    n = ref.shape[0] * SUBLANES
    return jnp.concatenate([ref[:, c, :, :].reshape(n, LANES) for c in range(ref.shape[1])], axis=1)


def _pack_bf16_pairs(x):
    bits = pltpu.bitcast(x.astype(BF16).astype(F32), U32)
    w = x.shape[1] // 2
    return bits[:, :w] | (bits[:, w:] >> 16)


def _unpack_bf16_pairs(words):
    hi = pltpu.bitcast(words & jnp.uint32(0xFFFF0000), F32).astype(BF16)
    lo = pltpu.bitcast(words << 16, F32).astype(BF16)
    return jnp.concatenate([hi, lo], axis=1)


def _row(x):
    return jnp.transpose(jnp.broadcast_to(x, (x.shape[0], LANES)))[0:1, :]


def _ada_kernel(c_ref, w_ref, b_ref, *refs):
    n_cast = (len(refs) - 1) // 2
    c = c_ref[...]
    s = (c * _sigmoid(c)).astype(BF16)
    refs[n_cast][...] = jnp.dot(s, w_ref[...].astype(BF16), preferred_element_type=F32) + b_ref[...]
    for src, dst in zip(refs[:n_cast], refs[n_cast + 1:]):
        dst[...] = src[...].astype(BF16)


def _ada(cond, w_ada, b_ada, cast=()):
    n = w_ada.shape[1]
    n_steps = n // ADA_COLS
    cast_specs = []
    for w in cast:
        assert w.ndim == 2 and w.shape[0] % (n_steps * 2 * SUBLANES) == 0
        cast_specs.append(pl.BlockSpec((w.shape[0] // n_steps, w.shape[1]), lambda j: (j, 0)))
    return pl.pallas_call(
        _ada_kernel,
        grid=(n_steps,),
        in_specs=[
            pl.BlockSpec((COND_ROWS, D_MODEL), lambda j: (0, 0)),
            pl.BlockSpec((D_MODEL, ADA_COLS), lambda j: (0, j)),
            pl.BlockSpec((1, ADA_COLS), lambda j: (0, j)),
        ] + cast_specs,
        out_specs=[pl.BlockSpec((COND_ROWS, ADA_COLS), lambda j: (0, j))] + cast_specs,
        out_shape=[jax.ShapeDtypeStruct((COND_ROWS, n), F32)] + [jax.ShapeDtypeStruct(w.shape, BF16) for w in cast],
        name="ada_mod",
    )(cond, w_ada, b_ada, *cast)


def _route(logits):
    lane = lax.broadcasted_iota(I32, logits.shape, 1).astype(F32)
    neg = jnp.float32(-1e30)
    far = jnp.float32(LANES)
    is_g = lane < N_EXP_GROUPS
    gl = jnp.where(is_g, logits, neg)
    gmax = jnp.max(gl, axis=-1, keepdims=True)
    gsel = jnp.min(jnp.where(gl == gmax, lane, far), axis=-1, keepdims=True)
    psel = 1.0 / jnp.sum(jnp.where(is_g, jnp.exp(gl - gmax), 0.0), axis=-1, keepdims=True)
    e_lo = EXPERT_LANE0 + EXP_PER_GROUP * gsel
    el = jnp.where(lane >= e_lo, jnp.where(lane < e_lo + EXP_PER_GROUP, logits, neg), neg)
    v1 = jnp.max(el, axis=-1, keepdims=True)
    i1 = jnp.min(jnp.where(el == v1, lane, far), axis=-1, keepdims=True)
    el2 = jnp.where(lane == i1, neg, el)
    v2 = jnp.max(el2, axis=-1, keepdims=True)
    i2 = jnp.min(jnp.where(el2 == v2, jnp.where(lane == i1, far, lane), far), axis=-1, keepdims=True)
    e2 = jnp.exp(v2 - v1)
    w1 = psel / (1.0 + e2)
    w2 = psel * e2 / (1.0 + e2)
    gate = jnp.where(lane == i1, w1, jnp.where(lane == i2, w2, 0.0))
    a = jnp.minimum(i1, i2) - e_lo
    b = jnp.maximum(i1, i2) - e_lo
    pair = a * (7.0 - a) * 0.5 + (b - a - 1.0)
    return gate, gsel * PAIRS_PER_GROUP + pair


def _mix_kernel(*refs, S, L, P, use_rope, emit_kv, n_cast, n_blocks, U):
    it = iter(refs)
    x_ref = next(it)
    mod_ref = next(it)
    if P:
        ck_ref = next(it)
        cv_ref = next(it)
    if use_rope:
        cos_ref = next(it)
        sneg_ref = next(it)
        spos_ref = next(it)
    (g1_ref, win_ref, qg_ref, kg_ref, wpool_ref, pscale_ref, wa_ref, wb_ref, wo_ref,
     g2_ref, wr_ref) = (next(it) for _ in range(11))
    cast_in = [next(it) for _ in range(n_cast)]
    xmid_ref = next(it)
    h2_ref = next(it)
    gate_ref = next(it)
    oh_ref = next(it)
    if emit_kv:
        knew_ref = next(it)
        vnew_ref = next(it)
    cast_out = [next(it) for _ in range(n_cast)]
    q_s, k_s, v_s, xp_s, h_s, attn_s, xm_s, mod2_s = (next(it) for _ in range(8))

    TM = S * L
    RB = ROW_BLOCK
    nrb = TM // RB
    n_steps = n_blocks // U
    scale = HEAD_DIM ** -0.5
    step = pl.program_id(0)
    block0 = U * jnp.minimum(step, n_steps - 1)
    slot = step % 2

    sh1 = mod_ref[0, 0:1, :]
    gain1 = g1_ref[...] * (1.0 + mod_ref[0, 1:2, :])
    gt1 = mod_ref[0, 2:3, :]
    sh2 = mod_ref[0, 3:4, :]
    gain2 = g2_ref[...] * (1.0 + mod_ref[0, 4:5, :])
    qg = qg_ref[...]
    kg = kg_ref[...]

    def project(r, carry):
        r0 = pl.multiple_of(r * RB, RB)
        s = r0 // L
        o = pl.multiple_of(r0 % L, RB)
        hb = (_rms(x_ref[pl.ds(r0, RB), :]) * gain1 + sh1).astype(BF16)
        h_s[pl.ds(r0, RB), :] = hb
        p1 = jnp.dot(hb, win_ref[:, 0:GATE_COL], preferred_element_type=F32)
        if use_rope:
            cs = cos_ref[pl.ds(o, RB), :]
            sn = sneg_ref[pl.ds(o, RB), :]
            sp = spos_ref[pl.ds(o, RB), :]

        def rope(t):
            return (t * cs + pltpu.roll(t, HEAD_DIM - ROPE_NF, 1) * sn + pltpu.roll(t, ROPE_NF, 1) * sp)

        for hd in range(N_HEADS):
            qh = _rms(p1[:, hd * HEAD_DIM:(hd + 1) * HEAD_DIM]) * qg
            if use_rope:
                qh = rope(qh)
            q_s[hd, pl.ds(r0, RB), :] = qh.astype(BF16)
        for kh in range(N_KV_HEADS):
            c0 = ATTN_W + kh * HEAD_DIM
            kk = _rms(p1[:, c0:c0 + HEAD_DIM]) * kg
            if emit_kv:
                knew_ref[pl.ds(N_KV_HEADS * r0 + kh, RB, stride=N_KV_HEADS), :] = kk
            if use_rope:
                kk = rope(kk)
            k_s[s, pl.ds(P + o, RB), kh * HEAD_DIM:(kh + 1) * HEAD_DIM] = kk.astype(BF16)
        vv = p1[:, ATTN_W + KV_W:ATTN_W + 2 * KV_W]
        if emit_kv:
            for kh in range(N_KV_HEADS):
                vnew_ref[pl.ds(N_KV_HEADS * r0 + kh, RB, stride=N_KV_HEADS), :] = (
                    vv[:, kh * HEAD_DIM:(kh + 1) * HEAD_DIM])
        v_s[s, pl.ds(P + o, RB), :] = vv.astype(BF16)
        xp_s[s, pl.ds(POOL_HALO + o, RB), :] = p1[:, ATTN_W + 2 * KV_W:GATE_COL]
        return carry

    @pl.when(step == 0)
    def _():
        xm_s[1] = jnp.zeros((U * RB, D_MODEL), F32)
        mod2_s[1] = jnp.zeros((2, D_MODEL), F32)

    @pl.when((step < n_steps) & (step % (nrb // U) == 0))
    def _():
        if P:
            k_s[0, 0:P, :] = ck_ref[0].astype(BF16)
            v_s[0, 0:P, :] = cv_ref[0].astype(BF16)
        xp_s[:, 0:POOL_HALO, :] = jnp.zeros((S, POOL_HALO, POOL_W), F32)
        xp_s[:, L + POOL_HALO:L + 2 * POOL_HALO, :] = jnp.zeros((S, POOL_HALO, POOL_W), F32)
        lax.fori_loop(0, TM // RB, project, 0)
        for src, dst in zip(cast_in, cast_out):
            dst[...] = src[...].astype(BF16)

    def mix(u):
        r0 = pl.multiple_of(((block0 + u) % nrb) * RB, RB)
        s = r0 // L
        o = pl.multiple_of(r0 % L, RB)
        attn_u = attn_s.at[u]
        rows = slice(u * RB, (u + 1) * RB)

        for kh in range(N_KV_HEADS):
            k = k_s[s, :, kh * HEAD_DIM:(kh + 1) * HEAD_DIM]
            v = v_s[s, :, kh * HEAD_DIM:(kh + 1) * HEAD_DIM]
            q4 = q_s[kh * GROUP:(kh + 1) * GROUP, pl.ds(r0, RB), :].reshape(GROUP * RB, HEAD_DIM)
            sc = lax.dot_general(q4, k, (((1,), (1,)), ((), ())), preferred_element_type=F32) * scale
            e = jnp.exp(sc - jnp.max(sc, axis=-1, keepdims=True))
            den = jnp.sum(e, axis=-1, keepdims=True)
            o4 = jnp.dot(e.astype(BF16), v, preferred_element_type=F32) / den
            for g in range(GROUP):
                hd = kh * GROUP + g
                attn_u[:, hd * HEAD_DIM:(hd + 1) * HEAD_DIM] = o4[g * RB:(g + 1) * RB].astype(BF16)
        a = jnp.dot(attn_u[...], wa_ref[...], preferred_element_type=F32)

        t = o + lax.broadcasted_iota(I32, (RB, 1), 0)
        RW = RB + 2 * POOL_HALO
        parts = []
        for gi, w in enumerate(POOL_WINDOWS):
            cols = slice(gi * POOL_GC, (gi + 1) * POOL_GC)
            xw = xp_s[s, pl.ds(o, RW), cols]
            run = xw
            span = 1
            while span < w:
                run = run + pltpu.roll(run, span, 0)
                span *= 2
            if w // 2 > 1:
                run = pltpu.roll(run, RW - (w // 2 - 1), 0)
            tot = run[POOL_HALO:POOL_HALO + RB]
            cnt = (jnp.minimum(t + w // 2, L) - jnp.maximum(t - w // 2, 0)).astype(F32)
            parts.append(tot / cnt - xw[POOL_HALO:POOL_HALO + RB])
        dpool = jnp.concatenate(parts, axis=1).astype(BF16)
        pooled = jnp.dot(dpool, wpool_ref[...], preferred_element_type=F32) * pscale_ref[...]
        b = jnp.dot(pooled.astype(BF16), wb_ref[...], preferred_element_type=F32)

        gates = jnp.dot(h_s[pl.ds(r0, RB), :], win_ref[:, GATE_COL:IN_W], preferred_element_type=F32)
        merged = _sigmoid(gates[:, 0:D_MODEL]) * a + _sigmoid(gates[:, D_MODEL:2 * D_MODEL]) * b
        upd = jnp.dot(merged.astype(BF16), wo_ref[...], preferred_element_type=F32)
        xm = x_ref[pl.ds(r0, RB), :] + gt1 * upd
        xmid_ref[rows, :] = xm
        xm_s[slot, rows, :] = xm

    def moe_prep(u):
        rows = slice(u * RB, (u + 1) * RB)
        h2 = _rms(xm_s[1 - slot, rows, :]) * mod2_s[1 - slot, 0:1, :] + mod2_s[1 - slot, 1:2, :]
        hi = h2.astype(BF16)
        lo = (h2 - hi.astype(F32)).astype(BF16)
        l1 = jnp.dot(hi, wr_ref[...], preferred_element_type=F32)
        l2 = jnp.dot(lo, wr_ref[:, 0:LANES], preferred_element_type=F32)
        gate, bucket = _route(l1[:, 0:LANES] + l1[:, LANES:2 * LANES] + l2)
        groups = pl.ds(u * (RB // SUBLANES), RB // SUBLANES)
        _store_tiles(h2_ref.at[groups], _pack_bf16_pairs(h2))
        gate_ref[rows, :] = gate
        lane = lax.broadcasted_iota(I32, (RB, LANES), 1).astype(F32)
        oh_ref[rows, :] = jnp.where(lane == bucket, 1.0, 0.0).astype(BF16)

    mod2_s[slot, 0:1, :] = gain2
    mod2_s[slot, 1:2, :] = sh2
    for u in range(U):
        moe_prep(u)
    for u in range(U):
        mix(u)


def _mix(x2d, mod, mod_row, cache, rope_tabs, weights, *, S, L, emit_kv, blocks_per_step, cast=()):
    T = x2d.shape[0]
    TM = S * L
    P = cache[0].shape[1] if cache is not None else 0
    use_rope = rope_tabs is not None
    assert T % TM == 0 and L % ROW_BLOCK == 0
    assert not (use_rope or P) or S == 1
    Lk = P + L

    args = [x2d, mod]
    nrb = TM // ROW_BLOCK
    n_blocks = T // ROW_BLOCK
    step_rows = blocks_per_step * ROW_BLOCK
    steps_per_group = nrb // blocks_per_step
    n_mix_steps = n_blocks // blocks_per_step
    assert nrb % blocks_per_step == 0

    def mixed(s):
        return jnp.minimum(s, n_mix_steps - 1)

    def group(s):
        return mixed(s) // steps_per_group

    def prepared(s):
        return jnp.maximum(s - 1, 0)

    in_specs = [
        pl.BlockSpec((TM, D_MODEL), lambda s: (group(s), 0)),
        pl.BlockSpec((1, 6, D_MODEL), lambda s: (mod_row(group(s)), 0, 0)),
    ]
    if P:
        args += list(cache)
        in_specs += [pl.BlockSpec((1, P, KV_W), lambda s: (group(s), 0, 0))] * 2
    if use_rope:
        args += list(rope_tabs)
        in_specs += [_resident((L, HEAD_DIM))] * 3
    args += list(weights)
    in_specs += [_resident(w.shape) for w in weights]
    n_steps = T // TM
    cast_specs = []
    for w in cast:
        assert w.shape[0] % n_steps == 0
        blk = (w.shape[0] // n_steps,) + w.shape[1:]
        cast_specs.append(pl.BlockSpec(blk, lambda s, n=len(blk): (group(s),) + (0,) * (n - 1)))
    args += list(cast)
    in_specs += cast_specs

    out_shape = [jax.ShapeDtypeStruct((T, D_MODEL), F32), jax.ShapeDtypeStruct(_tiles_shape(T, PACKED_CHUNKS), U32),
                 jax.ShapeDtypeStruct((T, LANES), F32),
                 jax.ShapeDtypeStruct((T, LANES), BF16)]
    out_specs = [pl.BlockSpec((step_rows, D_MODEL), lambda s: (mixed(s), 0)),
                 _tiles_spec(step_rows, prepared, PACKED_CHUNKS),
                 pl.BlockSpec((step_rows, LANES), lambda s: (prepared(s), 0)),
                 pl.BlockSpec((step_rows, LANES), lambda s: (prepared(s), 0))]
    if emit_kv:
        out_shape += [jax.ShapeDtypeStruct((T * N_KV_HEADS, HEAD_DIM), F32)] * 2
        out_specs += [pl.BlockSpec((TM * N_KV_HEADS, HEAD_DIM), lambda s: (group(s), 0))] * 2
    out_shape += [jax.ShapeDtypeStruct(w.shape, BF16) for w in cast]
    out_specs += cast_specs

    scratch = [
        pltpu.VMEM((N_HEADS, TM, HEAD_DIM), BF16),
        pltpu.VMEM((S, Lk, KV_W), BF16),
        pltpu.VMEM((S, Lk, KV_W), BF16),
        pltpu.VMEM((S, L + 2 * POOL_HALO, POOL_W), F32),
        pltpu.VMEM((TM, D_MODEL), BF16),
        pltpu.VMEM((blocks_per_step, ROW_BLOCK, ATTN_W), BF16),
        pltpu.VMEM((2, step_rows, D_MODEL), F32),
        pltpu.VMEM((2, 2, D_MODEL), F32),
    ]
    kern = functools.partial(_mix_kernel, S=S, L=L, P=P, use_rope=use_rope, emit_kv=emit_kv,
                             n_cast=len(cast), n_blocks=n_blocks, U=blocks_per_step)
    return pl.pallas_call(
        kern,
        grid=(n_mix_steps + 1,),
        in_specs=in_specs,
        out_specs=out_specs,
        out_shape=out_shape,
        scratch_shapes=scratch,
        compiler_params=pltpu.CompilerParams(
            dimension_semantics=("arbitrary",), vmem_limit_bytes=V7X_VMEM_LIMIT_BYTES),
        name="mixer_rope" if use_rope else "mixer_ctx",
    )(*args)


def _plan_kernel(oh_ref, dest_ref, meta_ref, *, n_blocks):
    TB = TOKEN_BLOCK
    lane = lax.broadcasted_iota(I32, (SUBLANES, LANES), 1)

    def count(b, acc):
        oh = oh_ref[pl.ds(pl.multiple_of(b * TB, TB), TB), :].astype(F32)
        return acc + jnp.sum(oh, axis=0, keepdims=True)

    counts = lax.fori_loop(0, n_blocks, count, jnp.zeros((SUBLANES, LANES), F32))
    padded = jnp.ceil(counts * (1.0 / SORT_TILE)) * SORT_TILE
    ends = padded
    step = 1
    while step < LANES:
        ends = ends + jnp.where(lane >= step, pltpu.roll(ends, step, 1), 0.0)
        step *= 2
    starts = ends - padded

    tri = jnp.where(lax.broadcasted_iota(I32, (TB, TB), 1) < lax.broadcasted_iota(I32, (TB, TB), 0),
                    1.0, 0.0).astype(BF16)

    def place(b, seen):
        oh = oh_ref[pl.ds(pl.multiple_of(b * TB, TB), TB), :]
        ohf = oh.astype(F32)
        rank = jnp.dot(tri, oh, preferred_element_type=F32)
        base = (starts + seen)[0:1, :]
        d = jnp.sum(ohf * (rank + base), axis=1, keepdims=True)
        dest_ref[b] = _row(d).astype(I32)
        return seen + jnp.sum(ohf, axis=0, keepdims=True)

    lax.fori_loop(0, n_blocks, place, jnp.zeros((SUBLANES, LANES), F32))

    tile_row0 = lax.broadcasted_iota(I32, (LANES, LANES), 0).astype(F32) * SORT_TILE
    is_bucket = lax.broadcasted_iota(I32, (LANES, LANES), 1) < N_BUCKETS
    done = jnp.sum(jnp.where(is_bucket, jnp.where(ends[0:1, :] <= tile_row0, 1.0, 0.0), 0.0),
                   axis=1, keepdims=True)
    bkt = jnp.minimum(done, N_BUCKETS - 1.0)
    grp = (jnp.where(bkt >= PAIRS_PER_GROUP, 1.0, 0.0) + jnp.where(bkt >= 2 * PAIRS_PER_GROUP, 1.0, 0.0)
           + jnp.where(bkt >= 3 * PAIRS_PER_GROUP, 1.0, 0.0))
    pair = bkt - PAIRS_PER_GROUP * grp
    a = jnp.where(pair >= 3.0, 1.0, 0.0) + jnp.where(pair >= 5.0, 1.0, 0.0)
    b = pair - a * (7.0 - a) * 0.5 + a + 1.0
    e1 = EXP_PER_GROUP * grp + a
    e2 = EXP_PER_GROUP * grp + b
    meta = jnp.concatenate(
        [_row(e1), _row(e2), ends[0:1, :] * (1.0 / SORT_TILE), jnp.zeros((SUBLANES - 3, LANES), F32)], axis=0)
    meta_ref[...] = meta.astype(I32)


def _plan(onehot):
    T = onehot.shape[0]
    n_blocks = T // TOKEN_BLOCK
    dest, meta = pl.pallas_call(
        functools.partial(_plan_kernel, n_blocks=n_blocks),
        out_shape=[jax.ShapeDtypeStruct((n_blocks, 1, TOKEN_BLOCK), I32),
                   jax.ShapeDtypeStruct((SUBLANES, LANES), I32)],
        name="moe_plan",
    )(onehot)
    return dest.reshape(T), meta


def _sc_move_rows(src_v, table_hbm, out_hbm, lo, n_rows, idx_v, pieces_v, sem):
    chunks = pieces_v.shape[0] // SC_ROWS_PER_STEP
    lane = lax.iota(I32, SC_LANES)
    row_in_group = lane & (SUBLANES - 1)
    chunk_in_pair = lane >> 3
    rows_per_gather = SC_PIECES_PER_GATHER // chunks

    @pl.loop(0, n_rows // SC_ROWS_PER_STEP)
    def _(step):
        copies = []
        for g in range(SC_ROWS_PER_STEP // rows_per_gather):
            r0 = step * SC_ROWS_PER_STEP + g * rows_per_gather
            for v in range(SC_PIECES_PER_GATHER // SC_LANES):
                group, chunk0 = v // (chunks // 2), 2 * (v % (chunks // 2))
                tok = plsc.load_gather(src_v, [r0 + group * SUBLANES + row_in_group])
                piece = (tok >> 3) * (SUBLANES * chunks) + (chunk0 + chunk_in_pair) * SUBLANES + (tok & 7)
                idx_v[pl.ds(g * SC_PIECES_PER_GATHER + v * SC_LANES, SC_LANES)] = piece
            window = pl.ds(g * SC_PIECES_PER_GATHER, SC_PIECES_PER_GATHER)
            copies.append(pltpu.async_copy(table_hbm.at[idx_v.at[window]], pieces_v.at[window], sem))
        for cp in copies:
            cp.wait()
        first = pl.multiple_of((lo + step * SC_ROWS_PER_STEP) * chunks, SC_ROWS_PER_STEP * chunks)
        pltpu.sync_copy(pieces_v, out_hbm.at[pl.ds(first, SC_ROWS_PER_STEP * chunks)])


def _sc_scratch(chunks, dtype):
    return [pltpu.VMEM((SC_ROWS_PER_STEP * chunks,), I32), pltpu.VMEM((SC_ROWS_PER_STEP * chunks, LANES), dtype)]


def _sc_dispatch(h2_flat, gate_rows, dest, n_rows):
    T = dest.shape[0]
    per_worker = n_rows // SC_WORKERS
    rows_per_step = SC_ROWS_PER_STEP
    chunks = h2_flat.shape[0] // T
    assert n_rows % SC_WORKERS == 0 and per_worker % rows_per_step == 0 and T % SC_LANES == 0
    mesh = plsc.VectorSubcoreMesh(core_axis_name="c", subcore_axis_name="s")

    @functools.partial(
        pl.kernel, mesh=mesh,
        out_type=[jax.ShapeDtypeStruct((n_rows * chunks, LANES), h2_flat.dtype),
                  jax.ShapeDtypeStruct((n_rows, LANES), F32)],
        scratch_types=[pltpu.VMEM((T,), I32), pltpu.VMEM((per_worker,), I32)]
        + _sc_scratch(chunks, h2_flat.dtype)
        + [pltpu.VMEM((rows_per_step, LANES), F32), pltpu.SemaphoreType.DMA, pltpu.SemaphoreType.DMA],
        compiler_params=pltpu.CompilerParams(use_tc_tiling_on_sc=True, needs_layout_passes=False),
        name="sc_dispatch",
    )
    def dispatch(h2_hbm, gate_hbm, dest_hbm, out_h_hbm, out_g_hbm,
                 dest_v, src_v, idx_v, pieces_v, gates_v, sem_h, sem_g):
        worker = lax.axis_index("s") * SC_CORES + lax.axis_index("c")
        lo = worker * per_worker
        pltpu.sync_copy(dest_hbm, dest_v)

        @pl.loop(0, per_worker // SC_LANES)
        def _(j):
            j0 = pl.multiple_of(j * SC_LANES, SC_LANES)
            src_v[pl.ds(j0, SC_LANES)] = lax.rem(lo + j0 + lax.iota(I32, SC_LANES), T)

        @pl.loop(0, T // SC_LANES)
        def _(j):
            t0 = pl.multiple_of(j * SC_LANES, SC_LANES)
            d = dest_v[pl.ds(t0, SC_LANES)] - lo
            mine = (d >= 0) & (d < per_worker)
            plsc.store_scatter(src_v, [jnp.where(mine, d, 0)], t0 + lax.iota(I32, SC_LANES), mask=mine)

        @pl.loop(0, per_worker // rows_per_step)
        def _(j):
            off = pl.multiple_of(j * rows_per_step, rows_per_step)
            pltpu.async_copy(gate_hbm.at[src_v.at[pl.ds(off, rows_per_step)]], gates_v, sem_g).wait()
            pltpu.sync_copy(gates_v, out_g_hbm.at[pl.ds(lo + off, rows_per_step)])

        _sc_move_rows(src_v, h2_hbm, out_h_hbm, lo, per_worker, idx_v, pieces_v, sem_h)

    return dispatch(h2_flat, gate_rows, dest)


def _expert_kernel(e1s, e2s, n_used, x_ref, gv_ref, wg_hbm, wu_hbm, wd_hbm, o_ref, wup_s, wdn_s, cur_s, sems):
    groups = SORT_TILE // SUBLANES

    def pair_copies(t, slot):
        copies = []
        for half, e in enumerate((e1s[t], e2s[t])):
            lo = 2 * half * D_EXPERT
            copies += [
                pltpu.make_async_copy(wg_hbm.at[e], wup_s.at[slot, :, pl.ds(lo, D_EXPERT)], sems.at[slot]),
                pltpu.make_async_copy(wu_hbm.at[e], wup_s.at[slot, :, pl.ds(lo + D_EXPERT, D_EXPERT)],
                                      sems.at[slot]),
                pltpu.make_async_copy(wd_hbm.at[e], wdn_s.at[slot, pl.ds(half * D_EXPERT, D_EXPERT), :],
                                      sems.at[slot]),
            ]
        return copies

    def differs(t):
        prev = jnp.maximum(t - 1, 0)
        return (e1s[t] != e1s[prev]) | (e2s[t] != e2s[prev])

    @pl.when(pl.program_id(0) == 0)
    def _():
        cur_s[0] = 0
        for cp in pair_copies(0, 0):
            cp.start()
        for cp in pair_copies(0, 0):
            cp.wait()

    def one_tile(k, carry):
        t = pl.program_id(0) * EXPERT_TILES_PER_STEP + k
        e1 = e1s[t]
        e2 = e2s[t]
        cur = cur_s[0]
        nxt = jnp.minimum(t + 1, n_used[0] - 1)
        prefetch = (t + 1 < n_used[0]) & differs(nxt)
        rows = pl.ds(pl.multiple_of(k * groups, groups), groups)

        @pl.when(prefetch)
        def _():
            for cp in pair_copies(nxt, 1 - cur):
                cp.start()

        @pl.when(t < n_used[0])
        def _():
            x = _unpack_bf16_pairs(_load_tiles(x_ref.at[rows]))
            gv = gv_ref[pl.ds(pl.multiple_of(k * SORT_TILE, SORT_TILE), SORT_TILE), :]
            lane = lax.broadcasted_iota(I32, gv.shape, 1)
            h = jnp.dot(x, wup_s[cur], preferred_element_type=F32)
            hid = []
            for slot, e in enumerate((e1, e2)):
                ge = jnp.sum(jnp.where(lane == EXPERT_LANE0 + e, gv, 0.0), axis=-1, keepdims=True)
                hg = h[:, (2 * slot) * D_EXPERT:(2 * slot + 1) * D_EXPERT]
                hu = h[:, (2 * slot + 1) * D_EXPERT:(2 * slot + 2) * D_EXPERT]
                hid.append((hg * _sigmoid(hg) * hu * ge).astype(BF16))
            out = jnp.dot(jnp.concatenate(hid, axis=1), wdn_s[cur], preferred_element_type=F32)
            _store_tiles(o_ref.at[rows], _pack_bf16_pairs(out))

        @pl.when(prefetch)
        def _():
            for cp in pair_copies(nxt, 1 - cur):
                cp.wait()
            cur_s[0] = 1 - cur

        @pl.when(t >= n_used[0])
        def _():
            o_ref[rows] = jnp.zeros((groups,) + o_ref.shape[1:], U32)

        return carry

    lax.fori_loop(0, EXPERT_TILES_PER_STEP, one_tile, 0)


def _experts(sorted_h2, sorted_gates, meta, wg, wu, wd):
    n_tiles = sorted_h2.shape[0] * SUBLANES // SORT_TILE
    step_rows = SORT_TILE * EXPERT_TILES_PER_STEP
    assert n_tiles % EXPERT_TILES_PER_STEP == 0

    def last_used(i, e1, e2, nu):
        return jnp.minimum(i, (nu[0] - 1) // EXPERT_TILES_PER_STEP)

    return pl.pallas_call(
        _expert_kernel,
        grid_spec=pltpu.PrefetchScalarGridSpec(
            num_scalar_prefetch=3,
            grid=(n_tiles // EXPERT_TILES_PER_STEP,),
            in_specs=[
                _tiles_spec(step_rows, last_used, PACKED_CHUNKS),
                pl.BlockSpec((step_rows, LANES), lambda *a: (last_used(*a), 0)),
                pl.BlockSpec(memory_space=pl.ANY), pl.BlockSpec(memory_space=pl.ANY),
                pl.BlockSpec(memory_space=pl.ANY),
            ],
            out_specs=_tiles_spec(step_rows, lambda i, *_: i, PACKED_CHUNKS),
            scratch_shapes=[pltpu.VMEM((2, D_MODEL, 4 * D_EXPERT), BF16),
                            pltpu.VMEM((2, 2 * D_EXPERT, D_MODEL), BF16),
                            pltpu.SMEM((1,), I32), pltpu.SemaphoreType.DMA((2,))],
        ),
        out_shape=jax.ShapeDtypeStruct(_tiles_shape(n_tiles * SORT_TILE, PACKED_CHUNKS), U32),
        compiler_params=pltpu.CompilerParams(
            dimension_semantics=("arbitrary",), vmem_limit_bytes=V7X_VMEM_LIMIT_BYTES),
        name="moe_experts",
    )(meta[0, :n_tiles], meta[1, :n_tiles], meta[2, LANES - 1:LANES], sorted_h2, sorted_gates, wg, wu, wd)


def _sc_row_gather(table_flat, idx, chunks):
    n = idx.shape[0]
    per_worker = n // SC_WORKERS
    assert n % SC_WORKERS == 0 and per_worker % SC_ROWS_PER_STEP == 0
    mesh = plsc.VectorSubcoreMesh(core_axis_name="c", subcore_axis_name="s")

    @functools.partial(
        pl.kernel, mesh=mesh,
        out_type=jax.ShapeDtypeStruct((n * chunks, LANES), table_flat.dtype),
        scratch_types=[pltpu.VMEM((per_worker,), I32)] + _sc_scratch(chunks, table_flat.dtype)
        + [pltpu.SemaphoreType.DMA],
        compiler_params=pltpu.CompilerParams(use_tc_tiling_on_sc=True, needs_layout_passes=False),
        name="sc_row_gather",
    )
    def gather(table_hbm, idx_hbm, out_hbm, src_v, idx_v, pieces_v, sem):
        worker = lax.axis_index("s") * SC_CORES + lax.axis_index("c")
        lo = worker * per_worker
        pltpu.sync_copy(idx_hbm.at[pl.ds(lo, per_worker)], src_v)
        _sc_move_rows(src_v, table_hbm, out_hbm, lo, per_worker, idx_v, pieces_v, sem)

    return gather(table_flat, idx)


def _final_kernel(x_ref, moe_ref, mod_ref, gf_ref, o_ref):
    y = x_ref[...] + mod_ref[0, 5:6, :] * _unpack_bf16_pairs(_load_tiles(moe_ref)).astype(F32)
    o_ref[...] = _rms(y) * gf_ref[...]


def _final(xmid, moe_rows, mod, mod_row, gf):
    T = xmid.shape[0]
    return pl.pallas_call(
        _final_kernel,
        grid=(T // FINAL_BLOCK,),
        in_specs=[
            pl.BlockSpec((FINAL_BLOCK, D_MODEL), lambda i: (i, 0)),
            _tiles_spec(FINAL_BLOCK, lambda i: i, PACKED_CHUNKS),
            pl.BlockSpec((1, 6, D_MODEL), lambda i: (mod_row(i), 0, 0)),
            pl.BlockSpec((1, D_MODEL), lambda i: (0, 0)),
        ],
        out_specs=pl.BlockSpec((FINAL_BLOCK, D_MODEL), lambda i: (i, 0)),
        out_shape=jax.ShapeDtypeStruct((T, D_MODEL), F32),
        compiler_params=pltpu.CompilerParams(
            dimension_semantics=("arbitrary",), vmem_limit_bytes=V7X_VMEM_LIMIT_BYTES),
        name="moe_final",
    )(xmid, moe_rows, mod, gf)


def _flat(tiles):
    return tiles.reshape(-1, LANES)


def _moe_dispatch(h2_tiles, gate_rows, onehot):
    T = gate_rows.shape[0]
    n_tiles = T // SORT_TILE + N_BUCKETS
    n_rows = n_tiles * SORT_TILE
    assert n_tiles <= LANES and T % TOKEN_BLOCK == 0
    dest, meta = _plan(onehot)
    sorted_h2, sorted_gates = _sc_dispatch(_flat(h2_tiles), gate_rows, dest, n_rows)
    return sorted_h2.reshape(_tiles_shape(n_rows, PACKED_CHUNKS)), sorted_gates, dest, meta


def _moe_unpermute(moe_sorted_tiles, dest):
    chunks = moe_sorted_tiles.shape[1]
    return _sc_row_gather(_flat(moe_sorted_tiles), dest, chunks).reshape(_tiles_shape(dest.shape[0], chunks))


def _rope_tables(n_tokens):
    t = np.arange(n_tokens)
    row = (t // GRID_W).astype(np.float32)
    col = (t % GRID_W).astype(np.float32)
    freq = np.float32(ROPE_THETA) ** (-np.arange(ROPE_NF, dtype=np.float32) / np.float32(ROPE_NF))
    ang = np.concatenate([row[:, None] * freq] * 2 + [col[:, None] * freq] * 2, axis=-1)
    first = (np.arange(HEAD_DIM) % (2 * ROPE_NF)) < ROPE_NF
    sin = np.sin(ang)
    zero = np.float32(0.0)
    return (jnp.asarray(np.cos(ang)), jnp.asarray(np.where(first, -sin, zero)),
            jnp.asarray(np.where(first, zero, sin)))


def kernel(x_prompt, x_sample, cache_k, cache_v, c, c_ctx, norm1_g, norm2_g, w_ada, b_ada, w_in, q_norm_g, k_norm_g, w_pool, pool_scale, w_branch_a, w_branch_b, w_out, w_router_group, w_router_expert, w_exp_gate, w_exp_up, w_exp_down, final_norm_g):
    assert norm1_g.shape[0] == 1, "single-layer trunk"
    B, L_ctx, _ = x_prompt.shape
    Bs, L_lat, _ = x_sample.shape
    P = cache_k.shape[2]
    assert 1 + Bs <= COND_ROWS

    cond = jnp.concatenate([c_ctx[None, :], c, jnp.zeros((COND_ROWS - 1 - Bs, D_MODEL), F32)], axis=0)
    wpool_bd = jax.scipy.linalg.block_diag(*[w_pool[0, g] for g in range(len(POOL_WINDOWS))])
    mod, w_in_b, wpool_b, wa_b, wb_b, wo_b = _ada(
        cond, w_ada[0], b_ada[0][None, :],
        cast=(w_in[0], wpool_bd, w_branch_a[0], w_branch_b[0], w_out[0]))
    mod = mod.reshape(COND_ROWS, 6, D_MODEL)

    wr = jnp.concatenate([w_router_group[0], w_router_expert[0],
                          jnp.zeros((D_MODEL, LANES - N_EXP_GROUPS - N_EXPERTS), F32)], axis=1)
    wr_hi = wr.astype(BF16)
    wr_lo = (wr - wr_hi.astype(F32)).astype(BF16)
    mix_w = (norm1_g[0][None, :], w_in_b, q_norm_g[0][None, :], k_norm_g[0][None, :],
             wpool_b, pool_scale[0][None, :], wa_b, wb_b, wo_b,
             norm2_g[0][None, :], jnp.concatenate([wr_hi, wr_lo], axis=1))
    gf = final_norm_g[None, :]

    xp2 = x_prompt.reshape(B * L_ctx, D_MODEL)
    xmid_p, h2_p, gate_p, oh_p, knew, vnew, wg, wu, wd = _mix(
        xp2, mod, lambda i: 0, None, None, mix_w, S=2, L=L_ctx, emit_kv=True, blocks_per_step=2,
        cast=(w_exp_gate[0], w_exp_up[0], w_exp_down[0]))
    sh_p, sg_p, dest_p, meta_p = _moe_dispatch(h2_p, gate_p, oh_p)

    xs2 = x_sample.reshape(Bs * L_lat, D_MODEL)
    cache = (cache_k[:, 0].reshape(Bs, P, KV_W), cache_v[:, 0].reshape(Bs, P, KV_W))
    xmid_s, h2_s, gate_s, oh_s = _mix(xs2, mod, lambda i: 1 + i, cache, _rope_tables(L_lat), mix_w,
                                      S=1, L=L_lat, emit_kv=False, blocks_per_step=1)
    sh_s, sg_s, dest_s, meta_s = _moe_dispatch(h2_s, gate_s, oh_s)

    moe_p = _moe_unpermute(_experts(sh_p, sg_p, meta_p, wg, wu, wd), dest_p)
    moe_s = _moe_unpermute(_experts(sh_s, sg_s, meta_s, wg, wu, wd), dest_s)
    y_prompt = _final(xmid_p, moe_p, mod, lambda i: 0, gf)
    blocks_per_seq = L_lat // FINAL_BLOCK
    y_sample = _final(xmid_s, moe_s, mod, lambda i: 1 + i // blocks_per_seq, gf)

    return (y_prompt.reshape(B, L_ctx, D_MODEL), y_sample.reshape(Bs, L_lat, D_MODEL),
            knew.reshape(B, 1, L_ctx, N_KV_HEADS, HEAD_DIM), vnew.reshape(B, 1, L_ctx, N_KV_HEADS, HEAD_DIM))
```

```python
import functools

import numpy as np
import jax
import jax.numpy as jnp
from jax import lax
from jax.experimental import pallas as pl
from jax.experimental.pallas import tpu as pltpu
from jax.experimental.pallas import tpu_sc as plsc

F32 = jnp.float32
BF16 = jnp.bfloat16
I32 = jnp.int32
U32 = jnp.uint32

D_MODEL = 1024
HEAD_DIM = 128
N_HEADS = 8
N_KV_HEADS = 2
GROUP = N_HEADS // N_KV_HEADS
ATTN_W = N_HEADS * HEAD_DIM
KV_W = N_KV_HEADS * HEAD_DIM
POOL_WINDOWS = (2, 4, 8, 16)
POOL_GC = 128
POOL_W = POOL_GC * len(POOL_WINDOWS)
IN_W = ATTN_W + 2 * KV_W + POOL_W + 2 * D_MODEL
GATE_COL = ATTN_W + 2 * KV_W + POOL_W
GRID_W = 64
ROPE_THETA = 10000.0
ROPE_NF = HEAD_DIM // 4
N_EXP_GROUPS = 4
EXP_PER_GROUP = 4
N_EXPERTS = 16
D_EXPERT = 256
EPS = 1e-6

LANES = 128
SUBLANES = 8
COND_ROWS = SUBLANES
POOL_HALO = 8
ROW_BLOCK = 256
ADA_COLS = 768
EXPERT_LANE0 = N_EXP_GROUPS
PAIRS_PER_GROUP = EXP_PER_GROUP * (EXP_PER_GROUP - 1) // 2
N_BUCKETS = N_EXP_GROUPS * PAIRS_PER_GROUP
SORT_TILE = 256
EXPERT_TILES_PER_STEP = 4
TOKEN_BLOCK = 1024
FINAL_BLOCK = 1024
ROW_CHUNKS = D_MODEL // LANES
SC_CORES = 2
SC_SUBCORES = 16
SC_WORKERS = SC_CORES * SC_SUBCORES
SC_LANES = 16
SC_PIECES_PER_GATHER = 128
SC_ROWS_PER_STEP = 64
PACKED_CHUNKS = ROW_CHUNKS // 2
V7X_VMEM_LIMIT_BYTES = 56 * 1024 * 1024


def _sigmoid(x):
    return 1.0 / (1.0 + jnp.exp(-x))


def _rms(x):
    return x * lax.rsqrt(jnp.mean(x * x, axis=-1, keepdims=True) + EPS)


def _resident(shape):
    zeros = (0,) * len(shape)
    return pl.BlockSpec(shape, lambda i, *_: zeros, pipeline_mode=pl.Buffered(1))


def _tiles_shape(n, chunks=ROW_CHUNKS):
    return (n // SUBLANES, chunks, SUBLANES, LANES)


def _tiles_spec(n, block_index, chunks=ROW_CHUNKS):
    return pl.BlockSpec(_tiles_shape(n, chunks), lambda *a: (block_index(*a), 0, 0, 0))


def _store_tiles(ref, x):
    for c in range(ref.shape[1]):
        ref[:, c, :, :] = x[:, c * LANES:(c + 1) * LANES].reshape(x.shape[0] // SUBLANES, SUBLANES, LANES)


def _load_tiles(ref):
    n = ref.shape[0] * SUBLANES
    return jnp.concatenate([ref[:, c, :, :].reshape(n, LANES) for c in range(ref.shape[1])], axis=1)


def _pack_bf16_pairs(x):
    bits = pltpu.bitcast(x.astype(BF16).astype(F32), U32)
    w = x.shape[1] // 2
    return bits[:, :w] | (bits[:, w:] >> 16)


def _unpack_bf16_pairs(words):
    hi = pltpu.bitcast(words & jnp.uint32(0xFFFF0000), F32).astype(BF16)
    lo = pltpu.bitcast(words << 16, F32).astype(BF16)
    return jnp.concatenate([hi, lo], axis=1)


def _row(x):
    return jnp.transpose(jnp.broadcast_to(x, (x.shape[0], LANES)))[0:1, :]


def _ada_kernel(c_ref, w_ref, b_ref, *refs):
    n_cast = (len(refs) - 1) // 2
    c = c_ref[...]
    s = (c * _sigmoid(c)).astype(BF16)
    refs[n_cast][...] = jnp.dot(s, w_ref[...].astype(BF16), preferred_element_type=F32) + b_ref[...]
    for src, dst in zip(refs[:n_cast], refs[n_cast + 1:]):
        dst[...] = src[...].astype(BF16)


def _ada(cond, w_ada, b_ada, cast=()):
    n = w_ada.shape[1]
    n_steps = n // ADA_COLS
    cast_specs = []
    for w in cast:
        assert w.ndim == 2 and w.shape[0] % (n_steps * 2 * SUBLANES) == 0
        cast_specs.append(pl.BlockSpec((w.shape[0] // n_steps, w.shape[1]), lambda j: (j, 0)))
    return pl.pallas_call(
        _ada_kernel,
        grid=(n_steps,),
        in_specs=[
            pl.BlockSpec((COND_ROWS, D_MODEL), lambda j: (0, 0)),
            pl.BlockSpec((D_MODEL, ADA_COLS), lambda j: (0, j)),
            pl.BlockSpec((1, ADA_COLS), lambda j: (0, j)),
        ] + cast_specs,
        out_specs=[pl.BlockSpec((COND_ROWS, ADA_COLS), lambda j: (0, j))] + cast_specs,
        out_shape=[jax.ShapeDtypeStruct((COND_ROWS, n), F32)] + [jax.ShapeDtypeStruct(w.shape, BF16) for w in cast],
        name="ada_mod",
    )(cond, w_ada, b_ada, *cast)


def _route(logits):
    lane = lax.broadcasted_iota(I32, logits.shape, 1).astype(F32)
    neg = jnp.float32(-1e30)
    far = jnp.float32(LANES)
    is_g = lane < N_EXP_GROUPS
    gl = jnp.where(is_g, logits, neg)
    gmax = jnp.max(gl, axis=-1, keepdims=True)
    gsel = jnp.min(jnp.where(gl == gmax, lane, far), axis=-1, keepdims=True)
    psel = 1.0 / jnp.sum(jnp.where(is_g, jnp.exp(gl - gmax), 0.0), axis=-1, keepdims=True)
    e_lo = EXPERT_LANE0 + EXP_PER_GROUP * gsel
    el = jnp.where(lane >= e_lo, jnp.where(lane < e_lo + EXP_PER_GROUP, logits, neg), neg)
    v1 = jnp.max(el, axis=-1, keepdims=True)
    i1 = jnp.min(jnp.where(el == v1, lane, far), axis=-1, keepdims=True)
    el2 = jnp.where(lane == i1, neg, el)
    v2 = jnp.max(el2, axis=-1, keepdims=True)
    i2 = jnp.min(jnp.where(el2 == v2, jnp.where(lane == i1, far, lane), far), axis=-1, keepdims=True)
    e2 = jnp.exp(v2 - v1)
    w1 = psel / (1.0 + e2)
    w2 = psel * e2 / (1.0 + e2)
    gate = jnp.where(lane == i1, w1, jnp.where(lane == i2, w2, 0.0))
    a = jnp.minimum(i1, i2) - e_lo
    b = jnp.maximum(i1, i2) - e_lo
    pair = a * (7.0 - a) * 0.5 + (b - a - 1.0)
    return gate, gsel * PAIRS_PER_GROUP + pair


def _mix_kernel(*refs, S, L, P, use_rope, emit_kv, n_cast, n_blocks, U):
    it = iter(refs)
    x_ref = next(it)
    mod_ref = next(it)
    if P:
        ck_ref = next(it)
        cv_ref = next(it)
    if use_rope:
        cos_ref = next(it)
        sneg_ref = next(it)
        spos_ref = next(it)
    (g1_ref, win_ref, qg_ref, kg_ref, wpool_ref, pscale_ref, wa_ref, wb_ref, wo_ref,
     g2_ref, wr_ref) = (next(it) for _ in range(11))
    cast_in = [next(it) for _ in range(n_cast)]
    xmid_ref = next(it)
    h2_ref = next(it)
    gate_ref = next(it)
    oh_ref = next(it)
    if emit_kv:
        knew_ref = next(it)
        vnew_ref = next(it)
    cast_out = [next(it) for _ in range(n_cast)]
    q_s, k_s, v_s, xp_s, h_s, attn_s, xm_s, mod2_s = (next(it) for _ in range(8))

    TM = S * L
    RB = ROW_BLOCK
    nrb = TM // RB
    n_steps = n_blocks // U
    scale = HEAD_DIM ** -0.5
    step = pl.program_id(0)
    block0 = U * jnp.minimum(step, n_steps - 1)
    slot = step % 2

    sh1 = mod_ref[0, 0:1, :]
    gain1 = g1_ref[...] * (1.0 + mod_ref[0, 1:2, :])
    gt1 = mod_ref[0, 2:3, :]
    sh2 = mod_ref[0, 3:4, :]
    gain2 = g2_ref[...] * (1.0 + mod_ref[0, 4:5, :])
    qg = qg_ref[...]
    kg = kg_ref[...]

    def project(r, carry):
        r0 = pl.multiple_of(r * RB, RB)
        s = r0 // L
        o = pl.multiple_of(r0 % L, RB)
        hb = (_rms(x_ref[pl.ds(r0, RB), :]) * gain1 + sh1).astype(BF16)
        h_s[pl.ds(r0, RB), :] = hb
        p1 = jnp.dot(hb, win_ref[:, 0:GATE_COL], preferred_element_type=F32)
        if use_rope:
            cs = cos_ref[pl.ds(o, RB), :]
            sn = sneg_ref[pl.ds(o, RB), :]
            sp = spos_ref[pl.ds(o, RB), :]

        def rope(t):
            return (t * cs + pltpu.roll(t, HEAD_DIM - ROPE_NF, 1) * sn + pltpu.roll(t, ROPE_NF, 1) * sp)

        for hd in range(N_HEADS):
            qh = _rms(p1[:, hd * HEAD_DIM:(hd + 1) * HEAD_DIM]) * qg
            if use_rope:
                qh = rope(qh)
            q_s[hd, pl.ds(r0, RB), :] = qh.astype(BF16)
        for kh in range(N_KV_HEADS):
            c0 = ATTN_W + kh * HEAD_DIM
            kk = _rms(p1[:, c0:c0 + HEAD_DIM]) * kg
            if emit_kv:
                knew_ref[pl.ds(N_KV_HEADS * r0 + kh, RB, stride=N_KV_HEADS), :] = kk
            if use_rope:
                kk = rope(kk)
            k_s[s, pl.ds(P + o, RB), kh * HEAD_DIM:(kh + 1) * HEAD_DIM] = kk.astype(BF16)
        vv = p1[:, ATTN_W + KV_W:ATTN_W + 2 * KV_W]
        if emit_kv:
            for kh in range(N_KV_HEADS):
                vnew_ref[pl.ds(N_KV_HEADS * r0 + kh, RB, stride=N_KV_HEADS), :] = (
                    vv[:, kh * HEAD_DIM:(kh + 1) * HEAD_DIM])
        v_s[s, pl.ds(P + o, RB), :] = vv.astype(BF16)
        xp_s[s, pl.ds(POOL_HALO + o, RB), :] = p1[:, ATTN_W + 2 * KV_W:GATE_COL]
        return carry

    @pl.when(step == 0)
    def _():
        xm_s[1] = jnp.zeros((U * RB, D_MODEL), F32)
        mod2_s[1] = jnp.zeros((2, D_MODEL), F32)

    @pl.when((step < n_steps) & (step % (nrb // U) == 0))
    def _():
        if P:
            k_s[0, 0:P, :] = ck_ref[0].astype(BF16)
            v_s[0, 0:P, :] = cv_ref[0].astype(BF16)
        xp_s[:, 0:POOL_HALO, :] = jnp.zeros((S, POOL_HALO, POOL_W), F32)
        xp_s[:, L + POOL_HALO:L + 2 * POOL_HALO, :] = jnp.zeros((S, POOL_HALO, POOL_W), F32)
        lax.fori_loop(0, TM // RB, project, 0)
        for src, dst in zip(cast_in, cast_out):
            dst[...] = src[...].astype(BF16)

    def mix(u):
        r0 = pl.multiple_of(((block0 + u) % nrb) * RB, RB)
        s = r0 // L
        o = pl.multiple_of(r0 % L, RB)
        attn_u = attn_s.at[u]
        rows = slice(u * RB, (u + 1) * RB)

        for hd in range(N_HEADS):
            kh = hd // GROUP
            k = k_s[s, :, kh * HEAD_DIM:(kh + 1) * HEAD_DIM]
            v = v_s[s, :, kh * HEAD_DIM:(kh + 1) * HEAD_DIM]
            qh = q_s[hd, pl.ds(r0, RB), :]
            sc = lax.dot_general(qh, k, (((1,), (1,)), ((), ())), preferred_element_type=F32) * scale
            e = jnp.exp(sc - jnp.max(sc, axis=-1, keepdims=True))
            den = jnp.sum(e, axis=-1, keepdims=True)
            oh = jnp.dot(e.astype(BF16), v, preferred_element_type=F32) / den
            attn_u[:, hd * HEAD_DIM:(hd + 1) * HEAD_DIM] = oh.astype(BF16)
        a = jnp.dot(attn_u[...], wa_ref[...], preferred_element_type=F32)

        t = o + lax.broadcasted_iota(I32, (RB, 1), 0)
        RW = RB + 2 * POOL_HALO
        parts = []
        for gi, w in enumerate(POOL_WINDOWS):
            cols = slice(gi * POOL_GC, (gi + 1) * POOL_GC)
            xw = xp_s[s, pl.ds(o, RW), cols]
            run = xw
            span = 1
            while span < w:
                run = run + pltpu.roll(run, span, 0)
                span *= 2
            if w // 2 > 1:
                run = pltpu.roll(run, RW - (w // 2 - 1), 0)
            tot = run[POOL_HALO:POOL_HALO + RB]
            cnt = (jnp.minimum(t + w // 2, L) - jnp.maximum(t - w // 2, 0)).astype(F32)
            parts.append(tot / cnt - xw[POOL_HALO:POOL_HALO + RB])
        dpool = jnp.concatenate(parts, axis=1).astype(BF16)
        pooled = jnp.dot(dpool, wpool_ref[...], preferred_element_type=F32) * pscale_ref[...]
        b = jnp.dot(pooled.astype(BF16), wb_ref[...], preferred_element_type=F32)

        gates = jnp.dot(h_s[pl.ds(r0, RB), :], win_ref[:, GATE_COL:IN_W], preferred_element_type=F32)
        merged = _sigmoid(gates[:, 0:D_MODEL]) * a + _sigmoid(gates[:, D_MODEL:2 * D_MODEL]) * b
        upd = jnp.dot(merged.astype(BF16), wo_ref[...], preferred_element_type=F32)
        xm = x_ref[pl.ds(r0, RB), :] + gt1 * upd
        xmid_ref[rows, :] = xm
        xm_s[slot, rows, :] = xm

    def moe_prep(u):
        rows = slice(u * RB, (u + 1) * RB)
        h2 = _rms(xm_s[1 - slot, rows, :]) * mod2_s[1 - slot, 0:1, :] + mod2_s[1 - slot, 1:2, :]
        hi = h2.astype(BF16)
        lo = (h2 - hi.astype(F32)).astype(BF16)
        l1 = jnp.dot(hi, wr_ref[...], preferred_element_type=F32)
        l2 = jnp.dot(lo, wr_ref[:, 0:LANES], preferred_element_type=F32)
        gate, bucket = _route(l1[:, 0:LANES] + l1[:, LANES:2 * LANES] + l2)
        groups = pl.ds(u * (RB // SUBLANES), RB // SUBLANES)
        _store_tiles(h2_ref.at[groups], _pack_bf16_pairs(h2))
        gate_ref[rows, :] = gate
        lane = lax.broadcasted_iota(I32, (RB, LANES), 1).astype(F32)
        oh_ref[rows, :] = jnp.where(lane == bucket, 1.0, 0.0).astype(BF16)

    mod2_s[slot, 0:1, :] = gain2
    mod2_s[slot, 1:2, :] = sh2
    for u in range(U):
        moe_prep(u)
    for u in range(U):
        mix(u)


def _mix(x2d, mod, mod_row, cache, rope_tabs, weights, *, S, L, emit_kv, blocks_per_step, cast=()):
    T = x2d.shape[0]
    TM = S * L
    P = cache[0].shape[1] if cache is not None else 0
    use_rope = rope_tabs is not None
    assert T % TM == 0 and L % ROW_BLOCK == 0
    assert not (use_rope or P) or S == 1
    Lk = P + L

    args = [x2d, mod]
    nrb = TM // ROW_BLOCK
    n_blocks = T // ROW_BLOCK
    step_rows = blocks_per_step * ROW_BLOCK
    steps_per_group = nrb // blocks_per_step
    n_mix_steps = n_blocks // blocks_per_step
    assert nrb % blocks_per_step == 0

    def mixed(s):
        return jnp.minimum(s, n_mix_steps - 1)

    def group(s):
        return mixed(s) // steps_per_group

    def prepared(s):
        return jnp.maximum(s - 1, 0)

    in_specs = [
        pl.BlockSpec((TM, D_MODEL), lambda s: (group(s), 0)),
        pl.BlockSpec((1, 6, D_MODEL), lambda s: (mod_row(group(s)), 0, 0)),
    ]
    if P:
        args += list(cache)
        in_specs += [pl.BlockSpec((1, P, KV_W), lambda s: (group(s), 0, 0))] * 2
    if use_rope:
        args += list(rope_tabs)
        in_specs += [_resident((L, HEAD_DIM))] * 3
    args += list(weights)
    in_specs += [_resident(w.shape) for w in weights]
    n_steps = T // TM
    cast_specs = []
    for w in cast:
        assert w.shape[0] % n_steps == 0
        blk = (w.shape[0] // n_steps,) + w.shape[1:]
        cast_specs.append(pl.BlockSpec(blk, lambda s, n=len(blk): (group(s),) + (0,) * (n - 1)))
    args += list(cast)
    in_specs += cast_specs

    out_shape = [jax.ShapeDtypeStruct((T, D_MODEL), F32), jax.ShapeDtypeStruct(_tiles_shape(T, PACKED_CHUNKS), U32),
                 jax.ShapeDtypeStruct((T, LANES), F32),
                 jax.ShapeDtypeStruct((T, LANES), BF16)]
    out_specs = [pl.BlockSpec((step_rows, D_MODEL), lambda s: (mixed(s), 0)),
                 _tiles_spec(step_rows, prepared, PACKED_CHUNKS),
                 pl.BlockSpec((step_rows, LANES), lambda s: (prepared(s), 0)),
                 pl.BlockSpec((step_rows, LANES), lambda s: (prepared(s), 0))]
    if emit_kv:
        out_shape += [jax.ShapeDtypeStruct((T * N_KV_HEADS, HEAD_DIM), F32)] * 2
        out_specs += [pl.BlockSpec((TM * N_KV_HEADS, HEAD_DIM), lambda s: (group(s), 0))] * 2
    out_shape += [jax.ShapeDtypeStruct(w.shape, BF16) for w in cast]
    out_specs += cast_specs

    scratch = [
        pltpu.VMEM((N_HEADS, TM, HEAD_DIM), BF16),
        pltpu.VMEM((S, Lk, KV_W), BF16),
        pltpu.VMEM((S, Lk, KV_W), BF16),
        pltpu.VMEM((S, L + 2 * POOL_HALO, POOL_W), F32),
        pltpu.VMEM((TM, D_MODEL), BF16),
        pltpu.VMEM((blocks_per_step, ROW_BLOCK, ATTN_W), BF16),
        pltpu.VMEM((2, step_rows, D_MODEL), F32),
        pltpu.VMEM((2, 2, D_MODEL), F32),
    ]
    kern = functools.partial(_mix_kernel, S=S, L=L, P=P, use_rope=use_rope, emit_kv=emit_kv,
                             n_cast=len(cast), n_blocks=n_blocks, U=blocks_per_step)
    return pl.pallas_call(
        kern,
        grid=(n_mix_steps + 1,),
        in_specs=in_specs,
        out_specs=out_specs,
        out_shape=out_shape,
        scratch_shapes=scratch,
        compiler_params=pltpu.CompilerParams(
            dimension_semantics=("arbitrary",), vmem_limit_bytes=V7X_VMEM_LIMIT_BYTES),
        name="mixer_rope" if use_rope else "mixer_ctx",
    )(*args)


def _plan_kernel(oh_ref, dest_ref, meta_ref, *, n_blocks):
    TB = TOKEN_BLOCK
    lane = lax.broadcasted_iota(I32, (SUBLANES, LANES), 1)

    def count(b, acc):
        oh = oh_ref[pl.ds(pl.multiple_of(b * TB, TB), TB), :].astype(F32)
        return acc + jnp.sum(oh, axis=0, keepdims=True)

    counts = lax.fori_loop(0, n_blocks, count, jnp.zeros((SUBLANES, LANES), F32))
    padded = jnp.ceil(counts * (1.0 / SORT_TILE)) * SORT_TILE
    ends = padded
    step = 1
    while step < LANES:
        ends = ends + jnp.where(lane >= step, pltpu.roll(ends, step, 1), 0.0)
        step *= 2
    starts = ends - padded

    tri = jnp.where(lax.broadcasted_iota(I32, (TB, TB), 1) < lax.broadcasted_iota(I32, (TB, TB), 0),
                    1.0, 0.0).astype(BF16)

    def place(b, seen):
        oh = oh_ref[pl.ds(pl.multiple_of(b * TB, TB), TB), :]
        ohf = oh.astype(F32)
        rank = jnp.dot(tri, oh, preferred_element_type=F32)
        base = (starts + seen)[0:1, :]
        d = jnp.sum(ohf * (rank + base), axis=1, keepdims=True)
        dest_ref[b] = _row(d).astype(I32)
        return seen + jnp.sum(ohf, axis=0, keepdims=True)

    lax.fori_loop(0, n_blocks, place, jnp.zeros((SUBLANES, LANES), F32))

    tile_row0 = lax.broadcasted_iota(I32, (LANES, LANES), 0).astype(F32) * SORT_TILE
    is_bucket = lax.broadcasted_iota(I32, (LANES, LANES), 1) < N_BUCKETS
    done = jnp.sum(jnp.where(is_bucket, jnp.where(ends[0:1, :] <= tile_row0, 1.0, 0.0), 0.0),
                   axis=1, keepdims=True)
    bkt = jnp.minimum(done, N_BUCKETS - 1.0)
    grp = (jnp.where(bkt >= PAIRS_PER_GROUP, 1.0, 0.0) + jnp.where(bkt >= 2 * PAIRS_PER_GROUP, 1.0, 0.0)
           + jnp.where(bkt >= 3 * PAIRS_PER_GROUP, 1.0, 0.0))
    pair = bkt - PAIRS_PER_GROUP * grp
    a = jnp.where(pair >= 3.0, 1.0, 0.0) + jnp.where(pair >= 5.0, 1.0, 0.0)
    b = pair - a * (7.0 - a) * 0.5 + a + 1.0
    e1 = EXP_PER_GROUP * grp + a
    e2 = EXP_PER_GROUP * grp + b
    meta = jnp.concatenate(
        [_row(e1), _row(e2), ends[0:1, :] * (1.0 / SORT_TILE), jnp.zeros((SUBLANES - 3, LANES), F32)], axis=0)
    meta_ref[...] = meta.astype(I32)


def _plan(onehot):
    T = onehot.shape[0]
    n_blocks = T // TOKEN_BLOCK
    dest, meta = pl.pallas_call(
        functools.partial(_plan_kernel, n_blocks=n_blocks),
        out_shape=[jax.ShapeDtypeStruct((n_blocks, 1, TOKEN_BLOCK), I32),
                   jax.ShapeDtypeStruct((SUBLANES, LANES), I32)],
        name="moe_plan",
    )(onehot)
    return dest.reshape(T), meta


def _sc_move_rows(src_v, table_hbm, out_hbm, lo, n_rows, idx_v, pieces_v, sem):
    chunks = pieces_v.shape[0] // SC_ROWS_PER_STEP
    lane = lax.iota(I32, SC_LANES)
    row_in_group = lane & (SUBLANES - 1)
    chunk_in_pair = lane >> 3
    rows_per_gather = SC_PIECES_PER_GATHER // chunks

    @pl.loop(0, n_rows // SC_ROWS_PER_STEP)
    def _(step):
        copies = []
        for g in range(SC_ROWS_PER_STEP // rows_per_gather):
            r0 = step * SC_ROWS_PER_STEP + g * rows_per_gather
            for v in range(SC_PIECES_PER_GATHER // SC_LANES):
                group, chunk0 = v // (chunks // 2), 2 * (v % (chunks // 2))
                tok = plsc.load_gather(src_v, [r0 + group * SUBLANES + row_in_group])
                piece = (tok >> 3) * (SUBLANES * chunks) + (chunk0 + chunk_in_pair) * SUBLANES + (tok & 7)
                idx_v[pl.ds(g * SC_PIECES_PER_GATHER + v * SC_LANES, SC_LANES)] = piece
            window = pl.ds(g * SC_PIECES_PER_GATHER, SC_PIECES_PER_GATHER)
            copies.append(pltpu.async_copy(table_hbm.at[idx_v.at[window]], pieces_v.at[window], sem))
        for cp in copies:
            cp.wait()
        first = pl.multiple_of((lo + step * SC_ROWS_PER_STEP) * chunks, SC_ROWS_PER_STEP * chunks)
        pltpu.sync_copy(pieces_v, out_hbm.at[pl.ds(first, SC_ROWS_PER_STEP * chunks)])


def _sc_scratch(chunks, dtype):
    return [pltpu.VMEM((SC_ROWS_PER_STEP * chunks,), I32), pltpu.VMEM((SC_ROWS_PER_STEP * chunks, LANES), dtype)]


def _sc_dispatch(h2_flat, gate_rows, dest, n_rows):
    T = dest.shape[0]
    per_worker = n_rows // SC_WORKERS
    rows_per_step = SC_ROWS_PER_STEP
    chunks = h2_flat.shape[0] // T
    assert n_rows % SC_WORKERS == 0 and per_worker % rows_per_step == 0 and T % SC_LANES == 0
    mesh = plsc.VectorSubcoreMesh(core_axis_name="c", subcore_axis_name="s")

    @functools.partial(
        pl.kernel, mesh=mesh,
        out_type=[jax.ShapeDtypeStruct((n_rows * chunks, LANES), h2_flat.dtype),
                  jax.ShapeDtypeStruct((n_rows, LANES), F32)],
        scratch_types=[pltpu.VMEM((T,), I32), pltpu.VMEM((per_worker,), I32)]
        + _sc_scratch(chunks, h2_flat.dtype)
        + [pltpu.VMEM((rows_per_step, LANES), F32), pltpu.SemaphoreType.DMA, pltpu.SemaphoreType.DMA],
        compiler_params=pltpu.CompilerParams(use_tc_tiling_on_sc=True, needs_layout_passes=False),
        name="sc_dispatch",
    )
    def dispatch(h2_hbm, gate_hbm, dest_hbm, out_h_hbm, out_g_hbm,
                 dest_v, src_v, idx_v, pieces_v, gates_v, sem_h, sem_g):
        worker = lax.axis_index("s") * SC_CORES + lax.axis_index("c")
        lo = worker * per_worker
        pltpu.sync_copy(dest_hbm, dest_v)

        @pl.loop(0, per_worker // SC_LANES)
        def _(j):
            j0 = pl.multiple_of(j * SC_LANES, SC_LANES)
            src_v[pl.ds(j0, SC_LANES)] = lax.rem(lo + j0 + lax.iota(I32, SC_LANES), T)

        @pl.loop(0, T // SC_LANES)
        def _(j):
            t0 = pl.multiple_of(j * SC_LANES, SC_LANES)
            d = dest_v[pl.ds(t0, SC_LANES)] - lo
            mine = (d >= 0) & (d < per_worker)
            plsc.store_scatter(src_v, [jnp.where(mine, d, 0)], t0 + lax.iota(I32, SC_LANES), mask=mine)

        @pl.loop(0, per_worker // rows_per_step)
        def _(j):
            off = pl.multiple_of(j * rows_per_step, rows_per_step)
            pltpu.async_copy(gate_hbm.at[src_v.at[pl.ds(off, rows_per_step)]], gates_v, sem_g).wait()
            pltpu.sync_copy(gates_v, out_g_hbm.at[pl.ds(lo + off, rows_per_step)])

        _sc_move_rows(src_v, h2_hbm, out_h_hbm, lo, per_worker, idx_v, pieces_v, sem_h)

    return dispatch(h2_flat, gate_rows, dest)


def _expert_kernel(e1s, e2s, n_used, x_ref, gv_ref, wg_ref, wu_ref, wd_ref, o_ref, wup_s, wdn_s):
    groups = SORT_TILE // SUBLANES

    def one_tile(k, carry):
        t = pl.program_id(0) * EXPERT_TILES_PER_STEP + k
        e1 = e1s[t]
        e2 = e2s[t]
        prev = jnp.maximum(t - 1, 0)
        new_pair = (t == 0) | (e1 != e1s[prev]) | (e2 != e2s[prev])
        rows = pl.ds(pl.multiple_of(k * groups, groups), groups)

        @pl.when((t < n_used[0]) & new_pair)
        def _():
            for slot, e in enumerate((e1, e2)):
                wup_s[:, (2 * slot) * D_EXPERT:(2 * slot + 1) * D_EXPERT] = wg_ref[e]
                wup_s[:, (2 * slot + 1) * D_EXPERT:(2 * slot + 2) * D_EXPERT] = wu_ref[e]
                wdn_s[slot * D_EXPERT:(slot + 1) * D_EXPERT, :] = wd_ref[e]

        @pl.when(t < n_used[0])
        def _():
            x = _unpack_bf16_pairs(_load_tiles(x_ref.at[rows]))
            gv = gv_ref[pl.ds(pl.multiple_of(k * SORT_TILE, SORT_TILE), SORT_TILE), :]
            lane = lax.broadcasted_iota(I32, gv.shape, 1)
            h = jnp.dot(x, wup_s[...], preferred_element_type=F32)
            hid = []
            for slot, e in enumerate((e1, e2)):
                ge = jnp.sum(jnp.where(lane == EXPERT_LANE0 + e, gv, 0.0), axis=-1, keepdims=True)
                hg = h[:, (2 * slot) * D_EXPERT:(2 * slot + 1) * D_EXPERT]
                hu = h[:, (2 * slot + 1) * D_EXPERT:(2 * slot + 2) * D_EXPERT]
                hid.append((hg * _sigmoid(hg) * hu * ge).astype(BF16))
            out = jnp.dot(jnp.concatenate(hid, axis=1), wdn_s[...], preferred_element_type=F32)
            _store_tiles(o_ref.at[rows], _pack_bf16_pairs(out))

        @pl.when(t >= n_used[0])
        def _():
            o_ref[rows] = jnp.zeros((groups,) + o_ref.shape[1:], U32)

        return carry

    lax.fori_loop(0, EXPERT_TILES_PER_STEP, one_tile, 0)


def _experts(sorted_h2, sorted_gates, meta, wg, wu, wd):
    n_tiles = sorted_h2.shape[0] * SUBLANES // SORT_TILE
    step_rows = SORT_TILE * EXPERT_TILES_PER_STEP
    assert n_tiles % EXPERT_TILES_PER_STEP == 0

    def last_used(i, e1, e2, nu):
        return jnp.minimum(i, (nu[0] - 1) // EXPERT_TILES_PER_STEP)

    return pl.pallas_call(
        _expert_kernel,
        grid_spec=pltpu.PrefetchScalarGridSpec(
            num_scalar_prefetch=3,
            grid=(n_tiles // EXPERT_TILES_PER_STEP,),
            in_specs=[
                _tiles_spec(step_rows, last_used, PACKED_CHUNKS),
                pl.BlockSpec((step_rows, LANES), lambda *a: (last_used(*a), 0)),
                _resident(wg.shape), _resident(wu.shape), _resident(wd.shape),
            ],
            out_specs=_tiles_spec(step_rows, lambda i, *_: i, PACKED_CHUNKS),
            scratch_shapes=[pltpu.VMEM((D_MODEL, 4 * D_EXPERT), BF16), pltpu.VMEM((2 * D_EXPERT, D_MODEL), BF16)],
        ),
        out_shape=jax.ShapeDtypeStruct(_tiles_shape(n_tiles * SORT_TILE, PACKED_CHUNKS), U32),
        compiler_params=pltpu.CompilerParams(
            dimension_semantics=("arbitrary",), vmem_limit_bytes=V7X_VMEM_LIMIT_BYTES),
        name="moe_experts",
    )(meta[0, :n_tiles], meta[1, :n_tiles], meta[2, LANES - 1:LANES], sorted_h2, sorted_gates, wg, wu, wd)


def _sc_row_gather(table_flat, idx, chunks):
    n = idx.shape[0]
    per_worker = n // SC_WORKERS
    assert n % SC_WORKERS == 0 and per_worker % SC_ROWS_PER_STEP == 0
    mesh = plsc.VectorSubcoreMesh(core_axis_name="c", subcore_axis_name="s")

    @functools.partial(
        pl.kernel, mesh=mesh,
        out_type=jax.ShapeDtypeStruct((n * chunks, LANES), table_flat.dtype),
        scratch_types=[pltpu.VMEM((per_worker,), I32)] + _sc_scratch(chunks, table_flat.dtype)
        + [pltpu.SemaphoreType.DMA],
        compiler_params=pltpu.CompilerParams(use_tc_tiling_on_sc=True, needs_layout_passes=False),
        name="sc_row_gather",
    )
    def gather(table_hbm, idx_hbm, out_hbm, src_v, idx_v, pieces_v, sem):
        worker = lax.axis_index("s") * SC_CORES + lax.axis_index("c")
        lo = worker * per_worker
        pltpu.sync_copy(idx_hbm.at[pl.ds(lo, per_worker)], src_v)
        _sc_move_rows(src_v, table_hbm, out_hbm, lo, per_worker, idx_v, pieces_v, sem)

    return gather(table_flat, idx)


def _final_kernel(x_ref, moe_ref, mod_ref, gf_ref, o_ref):
    y = x_ref[...] + mod_ref[0, 5:6, :] * _unpack_bf16_pairs(_load_tiles(moe_ref)).astype(F32)
    o_ref[...] = _rms(y) * gf_ref[...]


def _final(xmid, moe_rows, mod, mod_row, gf):
    T = xmid.shape[0]
    return pl.pallas_call(
        _final_kernel,
        grid=(T // FINAL_BLOCK,),
        in_specs=[
            pl.BlockSpec((FINAL_BLOCK, D_MODEL), lambda i: (i, 0)),
            _tiles_spec(FINAL_BLOCK, lambda i: i, PACKED_CHUNKS),
            pl.BlockSpec((1, 6, D_MODEL), lambda i: (mod_row(i), 0, 0)),
            pl.BlockSpec((1, D_MODEL), lambda i: (0, 0)),
        ],
        out_specs=pl.BlockSpec((FINAL_BLOCK, D_MODEL), lambda i: (i, 0)),
        out_shape=jax.ShapeDtypeStruct((T, D_MODEL), F32),
        compiler_params=pltpu.CompilerParams(
            dimension_semantics=("arbitrary",), vmem_limit_bytes=V7X_VMEM_LIMIT_BYTES),
        name="moe_final",
    )(xmid, moe_rows, mod, gf)


def _flat(tiles):
    return tiles.reshape(-1, LANES)


def _moe_dispatch(h2_tiles, gate_rows, onehot):
    T = gate_rows.shape[0]
    n_tiles = T // SORT_TILE + N_BUCKETS
    n_rows = n_tiles * SORT_TILE
    assert n_tiles <= LANES and T % TOKEN_BLOCK == 0
    dest, meta = _plan(onehot)
    sorted_h2, sorted_gates = _sc_dispatch(_flat(h2_tiles), gate_rows, dest, n_rows)
    return sorted_h2.reshape(_tiles_shape(n_rows, PACKED_CHUNKS)), sorted_gates, dest, meta


def _moe_unpermute(moe_sorted_tiles, dest):
    chunks = moe_sorted_tiles.shape[1]
    return _sc_row_gather(_flat(moe_sorted_tiles), dest, chunks).reshape(_tiles_shape(dest.shape[0], chunks))


def _rope_tables(n_tokens):
    t = np.arange(n_tokens)
    row = (t // GRID_W).astype(np.float32)
    col = (t % GRID_W).astype(np.float32)
    freq = np.float32(ROPE_THETA) ** (-np.arange(ROPE_NF, dtype=np.float32) / np.float32(ROPE_NF))
    ang = np.concatenate([row[:, None] * freq] * 2 + [col[:, None] * freq] * 2, axis=-1)
    first = (np.arange(HEAD_DIM) % (2 * ROPE_NF)) < ROPE_NF
    sin = np.sin(ang)
    zero = np.float32(0.0)
    return (jnp.asarray(np.cos(ang)), jnp.asarray(np.where(first, -sin, zero)),
            jnp.asarray(np.where(first, zero, sin)))


def kernel(x_prompt, x_sample, cache_k, cache_v, c, c_ctx, norm1_g, norm2_g, w_ada, b_ada, w_in, q_norm_g, k_norm_g, w_pool, pool_scale, w_branch_a, w_branch_b, w_out, w_router_group, w_router_expert, w_exp_gate, w_exp_up, w_exp_down, final_norm_g):
    assert norm1_g.shape[0] == 1, "single-layer trunk"
    B, L_ctx, _ = x_prompt.shape
    Bs, L_lat, _ = x_sample.shape
    P = cache_k.shape[2]
    assert 1 + Bs <= COND_ROWS

    cond = jnp.concatenate([c_ctx[None, :], c, jnp.zeros((COND_ROWS - 1 - Bs, D_MODEL), F32)], axis=0)
    wpool_bd = jax.scipy.linalg.block_diag(*[w_pool[0, g] for g in range(len(POOL_WINDOWS))])
    mod, w_in_b, wpool_b, wa_b, wb_b, wo_b = _ada(
        cond, w_ada[0], b_ada[0][None, :],
        cast=(w_in[0], wpool_bd, w_branch_a[0], w_branch_b[0], w_out[0]))
    mod = mod.reshape(COND_ROWS, 6, D_MODEL)

    wr = jnp.concatenate([w_router_group[0], w_router_expert[0],
                          jnp.zeros((D_MODEL, LANES - N_EXP_GROUPS - N_EXPERTS), F32)], axis=1)
    wr_hi = wr.astype(BF16)
    wr_lo = (wr - wr_hi.astype(F32)).astype(BF16)
    mix_w = (norm1_g[0][None, :], w_in_b, q_norm_g[0][None, :], k_norm_g[0][None, :],
             wpool_b, pool_scale[0][None, :], wa_b, wb_b, wo_b,
             norm2_g[0][None, :], jnp.concatenate([wr_hi, wr_lo], axis=1))
    gf = final_norm_g[None, :]

    xp2 = x_prompt.reshape(B * L_ctx, D_MODEL)
    xmid_p, h2_p, gate_p, oh_p, knew, vnew, wg, wu, wd = _mix(
        xp2, mod, lambda i: 0, None, None, mix_w, S=2, L=L_ctx, emit_kv=True, blocks_per_step=2,
        cast=(w_exp_gate[0], w_exp_up[0], w_exp_down[0]))
    sh_p, sg_p, dest_p, meta_p = _moe_dispatch(h2_p, gate_p, oh_p)

    xs2 = x_sample.reshape(Bs * L_lat, D_MODEL)
    cache = (cache_k[:, 0].reshape(Bs, P, KV_W), cache_v[:, 0].reshape(Bs, P, KV_W))
    xmid_s, h2_s, gate_s, oh_s = _mix(xs2, mod, lambda i: 1 + i, cache, _rope_tables(L_lat), mix_w,
                                      S=1, L=L_lat, emit_kv=False, blocks_per_step=1)
    sh_s, sg_s, dest_s, meta_s = _moe_dispatch(h2_s, gate_s, oh_s)

    moe_p = _moe_unpermute(_experts(sh_p, sg_p, meta_p, wg, wu, wd), dest_p)
    moe_s = _moe_unpermute(_experts(sh_s, sg_s, meta_s, wg, wu, wd), dest_s)
    y_prompt = _final(xmid_p, moe_p, mod, lambda i: 0, gf)
    blocks_per_seq = L_lat // FINAL_BLOCK
    y_sample = _final(xmid_s, moe_s, mod, lambda i: 1 + i // blocks_per_seq, gf)

    return (y_prompt.reshape(B, L_ctx, D_MODEL), y_sample.reshape(Bs, L_lat, D_MODEL),
            knew.reshape(B, 1, L_ctx, N_KV_HEADS, HEAD_DIM), vnew.reshape(B, 1, L_ctx, N_KV_HEADS, HEAD_DIM))
```

```python
import functools

import numpy as np
import jax
import jax.numpy as jnp
from jax import lax
from jax.experimental import pallas as pl
from jax.experimental.pallas import tpu as pltpu
from jax.experimental.pallas import tpu_sc as plsc

F32 = jnp.float32
BF16 = jnp.bfloat16
I32 = jnp.int32
U32 = jnp.uint32

D_MODEL = 1024
HEAD_DIM = 128
N_HEADS = 8
N_KV_HEADS = 2
GROUP = N_HEADS // N_KV_HEADS
ATTN_W = N_HEADS * HEAD_DIM
KV_W = N_KV_HEADS * HEAD_DIM
POOL_WINDOWS = (2, 4, 8, 16)
POOL_GC = 128
POOL_W = POOL_GC * len(POOL_WINDOWS)
IN_W = ATTN_W + 2 * KV_W + POOL_W + 2 * D_MODEL
GATE_COL = ATTN_W + 2 * KV_W + POOL_W
GRID_W = 64
ROPE_THETA = 10000.0
ROPE_NF = HEAD_DIM // 4
N_EXP_GROUPS = 4
EXP_PER_GROUP = 4
N_EXPERTS = 16
D_EXPERT = 256
EPS = 1e-6
LOG2_E = 1.4426950408889634

LANES = 128
SUBLANES = 8
COND_ROWS = SUBLANES
POOL_HALO = 8
ROW_BLOCK = 256
ADA_COLS = 768
EXPERT_LANE0 = N_EXP_GROUPS
PAIRS_PER_GROUP = EXP_PER_GROUP * (EXP_PER_GROUP - 1) // 2
N_BUCKETS = N_EXP_GROUPS * PAIRS_PER_GROUP
SORT_TILE = 256
EXPERT_TILES_PER_STEP = 4
TOKEN_BLOCK = 1024
FINAL_BLOCK = 1024
ROW_CHUNKS = D_MODEL // LANES
SC_CORES = 2
SC_SUBCORES = 16
SC_WORKERS = SC_CORES * SC_SUBCORES
SC_LANES = 16
SC_PIECES_PER_GATHER = 128
SC_ROWS_PER_STEP = 64
PACKED_CHUNKS = ROW_CHUNKS // 2
V7X_VMEM_LIMIT_BYTES = 56 * 1024 * 1024


def _sigmoid(x):
    return 1.0 / (1.0 + jnp.exp(-x))


def _rms(x):
    return x * lax.rsqrt(jnp.mean(x * x, axis=-1, keepdims=True) + EPS)


def _resident(shape):
    zeros = (0,) * len(shape)
    return pl.BlockSpec(shape, lambda i, *_: zeros, pipeline_mode=pl.Buffered(1))


def _tiles_shape(n, chunks=ROW_CHUNKS):
    return (n // SUBLANES, chunks, SUBLANES, LANES)


def _tiles_spec(n, block_index, chunks=ROW_CHUNKS):
    return pl.BlockSpec(_tiles_shape(n, chunks), lambda *a: (block_index(*a), 0, 0, 0))


def _store_tiles(ref, x):
    for c in range(ref.shape[1]):
        ref[:, c, :, :] = x[:, c * LANES:(c + 1) * LANES].reshape(x.shape[0] // SUBLANES, SUBLANES, LANES)


def _load_tiles(ref):
    n = ref.shape[0] * SUBLANES
    return jnp.concatenate([ref[:, c, :, :].reshape(n, LANES) for c in range(ref.shape[1])], axis=1)


def _pack_bf16_pairs(x):
    bits = pltpu.bitcast(x.astype(BF16).astype(F32), U32)
    w = x.shape[1] // 2
    return bits[:, :w] | (bits[:, w:] >> 16)


def _unpack_bf16_pairs(words):
    hi = pltpu.bitcast(words & jnp.uint32(0xFFFF0000), F32).astype(BF16)
    lo = pltpu.bitcast(words << 16, F32).astype(BF16)
    return jnp.concatenate([hi, lo], axis=1)


def _row(x):
    return jnp.transpose(jnp.broadcast_to(x, (x.shape[0], LANES)))[0:1, :]


def _ada_kernel(c_ref, w_ref, b_ref, *refs):
    n_cast = (len(refs) - 1) // 2
    c = c_ref[...]
    s = (c * _sigmoid(c)).astype(BF16)
    refs[n_cast][...] = jnp.dot(s, w_ref[...].astype(BF16), preferred_element_type=F32) + b_ref[...]
    for src, dst in zip(refs[:n_cast], refs[n_cast + 1:]):
        dst[...] = src[...].astype(BF16)


def _ada(cond, w_ada, b_ada, cast=()):
    n = w_ada.shape[1]
    n_steps = n // ADA_COLS
    cast_specs = []
    for w in cast:
        assert w.ndim == 2 and w.shape[0] % (n_steps * 2 * SUBLANES) == 0
        cast_specs.append(pl.BlockSpec((w.shape[0] // n_steps, w.shape[1]), lambda j: (j, 0)))
    return pl.pallas_call(
        _ada_kernel,
        grid=(n_steps,),
        in_specs=[
            pl.BlockSpec((COND_ROWS, D_MODEL), lambda j: (0, 0)),
            pl.BlockSpec((D_MODEL, ADA_COLS), lambda j: (0, j)),
            pl.BlockSpec((1, ADA_COLS), lambda j: (0, j)),
        ] + cast_specs,
        out_specs=[pl.BlockSpec((COND_ROWS, ADA_COLS), lambda j: (0, j))] + cast_specs,
        out_shape=[jax.ShapeDtypeStruct((COND_ROWS, n), F32)] + [jax.ShapeDtypeStruct(w.shape, BF16) for w in cast],
        name="ada_mod",
    )(cond, w_ada, b_ada, *cast)


def _route(logits):
    lane = lax.broadcasted_iota(I32, logits.shape, 1).astype(F32)
    neg = jnp.float32(-1e30)
    far = jnp.float32(LANES)
    is_g = lane < N_EXP_GROUPS
    gl = jnp.where(is_g, logits, neg)
    gmax = jnp.max(gl, axis=-1, keepdims=True)
    gsel = jnp.min(jnp.where(gl == gmax, lane, far), axis=-1, keepdims=True)
    psel = 1.0 / jnp.sum(jnp.where(is_g, jnp.exp(gl - gmax), 0.0), axis=-1, keepdims=True)
    e_lo = EXPERT_LANE0 + EXP_PER_GROUP * gsel
    el = jnp.where(lane >= e_lo, jnp.where(lane < e_lo + EXP_PER_GROUP, logits, neg), neg)
    v1 = jnp.max(el, axis=-1, keepdims=True)
    i1 = jnp.min(jnp.where(el == v1, lane, far), axis=-1, keepdims=True)
    el2 = jnp.where(lane == i1, neg, el)
    v2 = jnp.max(el2, axis=-1, keepdims=True)
    i2 = jnp.min(jnp.where(el2 == v2, jnp.where(lane == i1, far, lane), far), axis=-1, keepdims=True)
    e2 = jnp.exp(v2 - v1)
    w1 = psel / (1.0 + e2)
    w2 = psel * e2 / (1.0 + e2)
    gate = jnp.where(lane == i1, w1, jnp.where(lane == i2, w2, 0.0))
    a = jnp.minimum(i1, i2) - e_lo
    b = jnp.maximum(i1, i2) - e_lo
    pair = a * (7.0 - a) * 0.5 + (b - a - 1.0)
    return gate, gsel * PAIRS_PER_GROUP + pair


def _mix_kernel(*refs, S, L, P, use_rope, emit_kv, n_cast, n_blocks, U):
    it = iter(refs)
    x_ref = next(it)
    mod_ref = next(it)
    if P:
        ck_ref = next(it)
        cv_ref = next(it)
    if use_rope:
        cos_ref = next(it)
        sneg_ref = next(it)
        spos_ref = next(it)
    (g1_ref, win_ref, qg_ref, kg_ref, wpool_ref, pscale_ref, wa_ref, wb_ref, wo_ref,
     g2_ref, wr_ref) = (next(it) for _ in range(11))
    cast_in = [next(it) for _ in range(n_cast)]
    xmid_ref = next(it)
    h2_ref = next(it)
    gate_ref = next(it)
    oh_ref = next(it)
    if emit_kv:
        knew_ref = next(it)
        vnew_ref = next(it)
    cast_out = [next(it) for _ in range(n_cast)]
    q_s, k_s, v_s, xp_s, h_s, attn_s, xm_s, mod2_s = (next(it) for _ in range(8))

    TM = S * L
    RB = ROW_BLOCK
    nrb = TM // RB
    n_steps = n_blocks // U
    score_gain = HEAD_DIM ** -0.5 * LOG2_E
    step = pl.program_id(0)
    block0 = U * jnp.minimum(step, n_steps - 1)
    slot = step % 2

    sh1 = mod_ref[0, 0:1, :]
    gain1 = g1_ref[...] * (1.0 + mod_ref[0, 1:2, :])
    gt1 = mod_ref[0, 2:3, :]
    sh2 = mod_ref[0, 3:4, :]
    gain2 = g2_ref[...] * (1.0 + mod_ref[0, 4:5, :])
    qg = qg_ref[...] * score_gain
    kg = kg_ref[...]

    def project(r, carry):
        r0 = pl.multiple_of(r * RB, RB)
        s = r0 // L
        o = pl.multiple_of(r0 % L, RB)
        hb = (_rms(x_ref[pl.ds(r0, RB), :]) * gain1 + sh1).astype(BF16)
        h_s[pl.ds(r0, RB), :] = hb
        p1 = jnp.dot(hb, win_ref[:, 0:GATE_COL], preferred_element_type=F32)
        if use_rope:
            cs = cos_ref[pl.ds(o, RB), :]
            sn = sneg_ref[pl.ds(o, RB), :]
            sp = spos_ref[pl.ds(o, RB), :]

        def rope(t):
            return (t * cs + pltpu.roll(t, HEAD_DIM - ROPE_NF, 1) * sn + pltpu.roll(t, ROPE_NF, 1) * sp)

        for hd in range(N_HEADS):
            qh = _rms(p1[:, hd * HEAD_DIM:(hd + 1) * HEAD_DIM]) * qg
            if use_rope:
                qh = rope(qh)
            q_s[hd, pl.ds(r0, RB), :] = qh.astype(BF16)
        for kh in range(N_KV_HEADS):
            c0 = ATTN_W + kh * HEAD_DIM
            kk = _rms(p1[:, c0:c0 + HEAD_DIM]) * kg
            if emit_kv:
                knew_ref[pl.ds(N_KV_HEADS * r0 + kh, RB, stride=N_KV_HEADS), :] = kk
            if use_rope:
                kk = rope(kk)
            k_s[s, pl.ds(P + o, RB), kh * HEAD_DIM:(kh + 1) * HEAD_DIM] = kk.astype(BF16)
        vv = p1[:, ATTN_W + KV_W:ATTN_W + 2 * KV_W]
        if emit_kv:
            for kh in range(N_KV_HEADS):
                vnew_ref[pl.ds(N_KV_HEADS * r0 + kh, RB, stride=N_KV_HEADS), :] = (
                    vv[:, kh * HEAD_DIM:(kh + 1) * HEAD_DIM])
        v_s[s, pl.ds(P + o, RB), :] = vv.astype(BF16)
        xp_s[s, pl.ds(POOL_HALO + o, RB), :] = p1[:, ATTN_W + 2 * KV_W:GATE_COL]
        return carry

    @pl.when(step == 0)
    def _():
        xm_s[1] = jnp.zeros((U * RB, D_MODEL), F32)
        mod2_s[1] = jnp.zeros((2, D_MODEL), F32)

    @pl.when((step < n_steps) & (step % (nrb // U) == 0))
    def _():
        if P:
            k_s[0, 0:P, :] = ck_ref[0].astype(BF16)
            v_s[0, 0:P, :] = cv_ref[0].astype(BF16)
        xp_s[:, 0:POOL_HALO, :] = jnp.zeros((S, POOL_HALO, POOL_W), F32)
        xp_s[:, L + POOL_HALO:L + 2 * POOL_HALO, :] = jnp.zeros((S, POOL_HALO, POOL_W), F32)
        lax.fori_loop(0, TM // RB, project, 0)
        for src, dst in zip(cast_in, cast_out):
            dst[...] = src[...].astype(BF16)

    def mix(u):
        r0 = pl.multiple_of(((block0 + u) % nrb) * RB, RB)
        s = r0 // L
        o = pl.multiple_of(r0 % L, RB)
        attn_u = attn_s.at[u]
        rows = slice(u * RB, (u + 1) * RB)

        for hd in range(N_HEADS):
            kh = hd // GROUP
            k = k_s[s, :, kh * HEAD_DIM:(kh + 1) * HEAD_DIM]
            v = v_s[s, :, kh * HEAD_DIM:(kh + 1) * HEAD_DIM]
            qh = q_s[hd, pl.ds(r0, RB), :]
            sc = lax.dot_general(qh, k, (((1,), (1,)), ((), ())), preferred_element_type=F32)
            e = jnp.exp2(sc - jnp.max(sc, axis=-1, keepdims=True))
            den = jnp.sum(e, axis=-1, keepdims=True)
            oh = jnp.dot(e.astype(BF16), v, preferred_element_type=F32) / den
            attn_u[:, hd * HEAD_DIM:(hd + 1) * HEAD_DIM] = oh.astype(BF16)
        a = jnp.dot(attn_u[...], wa_ref[...], preferred_element_type=F32)

        t = o + lax.broadcasted_iota(I32, (RB, 1), 0)
        RW = RB + 2 * POOL_HALO
        parts = []
        for gi, w in enumerate(POOL_WINDOWS):
            cols = slice(gi * POOL_GC, (gi + 1) * POOL_GC)
            xw = xp_s[s, pl.ds(o, RW), cols]
            run = xw
            span = 1
            while span < w:
                run = run + pltpu.roll(run, span, 0)
                span *= 2
            if w // 2 > 1:
                run = pltpu.roll(run, RW - (w // 2 - 1), 0)
            tot = run[POOL_HALO:POOL_HALO + RB]
            cnt = (jnp.minimum(t + w // 2, L) - jnp.maximum(t - w // 2, 0)).astype(F32)
            parts.append(tot / cnt - xw[POOL_HALO:POOL_HALO + RB])
        dpool = jnp.concatenate(parts, axis=1).astype(BF16)
        pooled = jnp.dot(dpool, wpool_ref[...], preferred_element_type=F32) * pscale_ref[...]
        b = jnp.dot(pooled.astype(BF16), wb_ref[...], preferred_element_type=F32)

        gates = jnp.dot(h_s[pl.ds(r0, RB), :], win_ref[:, GATE_COL:IN_W], preferred_element_type=F32)
        merged = _sigmoid(gates[:, 0:D_MODEL]) * a + _sigmoid(gates[:, D_MODEL:2 * D_MODEL]) * b
        upd = jnp.dot(merged.astype(BF16), wo_ref[...], preferred_element_type=F32)
        xm = x_ref[pl.ds(r0, RB), :] + gt1 * upd
        xmid_ref[rows, :] = xm
        xm_s[slot, rows, :] = xm

    def moe_prep(u):
        rows = slice(u * RB, (u + 1) * RB)
        h2 = _rms(xm_s[1 - slot, rows, :]) * mod2_s[1 - slot, 0:1, :] + mod2_s[1 - slot, 1:2, :]
        hi = h2.astype(BF16)
        lo = (h2 - hi.astype(F32)).astype(BF16)
        l1 = jnp.dot(hi, wr_ref[...], preferred_element_type=F32)
        l2 = jnp.dot(lo, wr_ref[:, 0:LANES], preferred_element_type=F32)
        gate, bucket = _route(l1[:, 0:LANES] + l1[:, LANES:2 * LANES] + l2)
        groups = pl.ds(u * (RB // SUBLANES), RB // SUBLANES)
        _store_tiles(h2_ref.at[groups], _pack_bf16_pairs(h2))
        gate_ref[rows, :] = gate
        lane = lax.broadcasted_iota(I32, (RB, LANES), 1).astype(F32)
        oh_ref[rows, :] = jnp.where(lane == bucket, 1.0, 0.0).astype(BF16)

    mod2_s[slot, 0:1, :] = gain2
    mod2_s[slot, 1:2, :] = sh2
    for u in range(U):
        moe_prep(u)
    for u in range(U):
        mix(u)


def _mix(x2d, mod, mod_row, cache, rope_tabs, weights, *, S, L, emit_kv, blocks_per_step, cast=()):
    T = x2d.shape[0]
    TM = S * L
    P = cache[0].shape[1] if cache is not None else 0
    use_rope = rope_tabs is not None
    assert T % TM == 0 and L % ROW_BLOCK == 0
    assert not (use_rope or P) or S == 1
    Lk = P + L

    args = [x2d, mod]
    nrb = TM // ROW_BLOCK
    n_blocks = T // ROW_BLOCK
    step_rows = blocks_per_step * ROW_BLOCK
    steps_per_group = nrb // blocks_per_step
    n_mix_steps = n_blocks // blocks_per_step
    assert nrb % blocks_per_step == 0

    def mixed(s):
        return jnp.minimum(s, n_mix_steps - 1)

    def group(s):
        return mixed(s) // steps_per_group

    def prepared(s):
        return jnp.maximum(s - 1, 0)

    in_specs = [
        pl.BlockSpec((TM, D_MODEL), lambda s: (group(s), 0)),
        pl.BlockSpec((1, 6, D_MODEL), lambda s: (mod_row(group(s)), 0, 0)),
    ]
    if P:
        args += list(cache)
        in_specs += [pl.BlockSpec((1, P, KV_W), lambda s: (group(s), 0, 0))] * 2
    if use_rope:
        args += list(rope_tabs)
        in_specs += [_resident((L, HEAD_DIM))] * 3
    args += list(weights)
    in_specs += [_resident(w.shape) for w in weights]
    n_steps = T // TM
    cast_specs = []
    for w in cast:
        assert w.shape[0] % n_steps == 0
        blk = (w.shape[0] // n_steps,) + w.shape[1:]
        cast_specs.append(pl.BlockSpec(blk, lambda s, n=len(blk): (group(s),) + (0,) * (n - 1)))
    args += list(cast)
    in_specs += cast_specs

    out_shape = [jax.ShapeDtypeStruct((T, D_MODEL), F32), jax.ShapeDtypeStruct(_tiles_shape(T, PACKED_CHUNKS), U32),
                 jax.ShapeDtypeStruct((T, LANES), F32),
                 jax.ShapeDtypeStruct((T, LANES), BF16)]
    out_specs = [pl.BlockSpec((step_rows, D_MODEL), lambda s: (mixed(s), 0)),
                 _tiles_spec(step_rows, prepared, PACKED_CHUNKS),
                 pl.BlockSpec((step_rows, LANES), lambda s: (prepared(s), 0)),
                 pl.BlockSpec((step_rows, LANES), lambda s: (prepared(s), 0))]
    if emit_kv:
        out_shape += [jax.ShapeDtypeStruct((T * N_KV_HEADS, HEAD_DIM), F32)] * 2
        out_specs += [pl.BlockSpec((TM * N_KV_HEADS, HEAD_DIM), lambda s: (group(s), 0))] * 2
    out_shape += [jax.ShapeDtypeStruct(w.shape, BF16) for w in cast]
    out_specs += cast_specs

    scratch = [
        pltpu.VMEM((N_HEADS, TM, HEAD_DIM), BF16),
        pltpu.VMEM((S, Lk, KV_W), BF16),
        pltpu.VMEM((S, Lk, KV_W), BF16),
        pltpu.VMEM((S, L + 2 * POOL_HALO, POOL_W), F32),
        pltpu.VMEM((TM, D_MODEL), BF16),
        pltpu.VMEM((blocks_per_step, ROW_BLOCK, ATTN_W), BF16),
        pltpu.VMEM((2, step_rows, D_MODEL), F32),
        pltpu.VMEM((2, 2, D_MODEL), F32),
    ]
    kern = functools.partial(_mix_kernel, S=S, L=L, P=P, use_rope=use_rope, emit_kv=emit_kv,
                             n_cast=len(cast), n_blocks=n_blocks, U=blocks_per_step)
    return pl.pallas_call(
        kern,
        grid=(n_mix_steps + 1,),
        in_specs=in_specs,
        out_specs=out_specs,
        out_shape=out_shape,
        scratch_shapes=scratch,
        compiler_params=pltpu.CompilerParams(
            dimension_semantics=("arbitrary",), vmem_limit_bytes=V7X_VMEM_LIMIT_BYTES),
        name="mixer_rope" if use_rope else "mixer_ctx",
    )(*args)


def _plan_kernel(oh_ref, dest_ref, meta_ref, *, n_blocks):
    TB = TOKEN_BLOCK
    lane = lax.broadcasted_iota(I32, (SUBLANES, LANES), 1)

    def count(b, acc):
        oh = oh_ref[pl.ds(pl.multiple_of(b * TB, TB), TB), :].astype(F32)
        return acc + jnp.sum(oh, axis=0, keepdims=True)

    counts = lax.fori_loop(0, n_blocks, count, jnp.zeros((SUBLANES, LANES), F32))
    padded = jnp.ceil(counts * (1.0 / SORT_TILE)) * SORT_TILE
    ends = padded
    step = 1
    while step < LANES:
        ends = ends + jnp.where(lane >= step, pltpu.roll(ends, step, 1), 0.0)
        step *= 2
    starts = ends - padded

    tri = jnp.where(lax.broadcasted_iota(I32, (TB, TB), 1) < lax.broadcasted_iota(I32, (TB, TB), 0),
                    1.0, 0.0).astype(BF16)

    def place(b, seen):
        oh = oh_ref[pl.ds(pl.multiple_of(b * TB, TB), TB), :]
        ohf = oh.astype(F32)
        rank = jnp.dot(tri, oh, preferred_element_type=F32)
        base = (starts + seen)[0:1, :]
        d = jnp.sum(ohf * (rank + base), axis=1, keepdims=True)
        dest_ref[b] = _row(d).astype(I32)
        return seen + jnp.sum(ohf, axis=0, keepdims=True)

    lax.fori_loop(0, n_blocks, place, jnp.zeros((SUBLANES, LANES), F32))

    tile_row0 = lax.broadcasted_iota(I32, (LANES, LANES), 0).astype(F32) * SORT_TILE
    is_bucket = lax.broadcasted_iota(I32, (LANES, LANES), 1) < N_BUCKETS
    done = jnp.sum(jnp.where(is_bucket, jnp.where(ends[0:1, :] <= tile_row0, 1.0, 0.0), 0.0),
                   axis=1, keepdims=True)
    bkt = jnp.minimum(done, N_BUCKETS - 1.0)
    grp = (jnp.where(bkt >= PAIRS_PER_GROUP, 1.0, 0.0) + jnp.where(bkt >= 2 * PAIRS_PER_GROUP, 1.0, 0.0)
           + jnp.where(bkt >= 3 * PAIRS_PER_GROUP, 1.0, 0.0))
    pair = bkt - PAIRS_PER_GROUP * grp
    a = jnp.where(pair >= 3.0, 1.0, 0.0) + jnp.where(pair >= 5.0, 1.0, 0.0)
    b = pair - a * (7.0 - a) * 0.5 + a + 1.0
    e1 = EXP_PER_GROUP * grp + a
    e2 = EXP_PER_GROUP * grp + b
    meta = jnp.concatenate(
        [_row(e1), _row(e2), ends[0:1, :] * (1.0 / SORT_TILE), jnp.zeros((SUBLANES - 3, LANES), F32)], axis=0)
    meta_ref[...] = meta.astype(I32)


def _plan(onehot):
    T = onehot.shape[0]
    n_blocks = T // TOKEN_BLOCK
    dest, meta = pl.pallas_call(
        functools.partial(_plan_kernel, n_blocks=n_blocks),
        out_shape=[jax.ShapeDtypeStruct((n_blocks, 1, TOKEN_BLOCK), I32),
                   jax.ShapeDtypeStruct((SUBLANES, LANES), I32)],
        name="moe_plan",
    )(onehot)
    return dest.reshape(T), meta


def _sc_move_rows(src_v, table_hbm, out_hbm, lo, n_rows, idx_v, pieces_v, sem):
    chunks = pieces_v.shape[0] // SC_ROWS_PER_STEP
    lane = lax.iota(I32, SC_LANES)
    row_in_group = lane & (SUBLANES - 1)
    chunk_in_pair = lane >> 3
    rows_per_gather = SC_PIECES_PER_GATHER // chunks

    @pl.loop(0, n_rows // SC_ROWS_PER_STEP)
    def _(step):
        copies = []
        for g in range(SC_ROWS_PER_STEP // rows_per_gather):
            r0 = step * SC_ROWS_PER_STEP + g * rows_per_gather
            for v in range(SC_PIECES_PER_GATHER // SC_LANES):
                group, chunk0 = v // (chunks // 2), 2 * (v % (chunks // 2))
                tok = plsc.load_gather(src_v, [r0 + group * SUBLANES + row_in_group])
                piece = (tok >> 3) * (SUBLANES * chunks) + (chunk0 + chunk_in_pair) * SUBLANES + (tok & 7)
                idx_v[pl.ds(g * SC_PIECES_PER_GATHER + v * SC_LANES, SC_LANES)] = piece
            window = pl.ds(g * SC_PIECES_PER_GATHER, SC_PIECES_PER_GATHER)
            copies.append(pltpu.async_copy(table_hbm.at[idx_v.at[window]], pieces_v.at[window], sem))
        for cp in copies:
            cp.wait()
        first = pl.multiple_of((lo + step * SC_ROWS_PER_STEP) * chunks, SC_ROWS_PER_STEP * chunks)
        pltpu.sync_copy(pieces_v, out_hbm.at[pl.ds(first, SC_ROWS_PER_STEP * chunks)])


def _sc_scratch(chunks, dtype):
    return [pltpu.VMEM((SC_ROWS_PER_STEP * chunks,), I32), pltpu.VMEM((SC_ROWS_PER_STEP * chunks, LANES), dtype)]


def _sc_dispatch(h2_flat, gate_rows, dest, n_rows):
    T = dest.shape[0]
    per_worker = n_rows // SC_WORKERS
    rows_per_step = SC_ROWS_PER_STEP
    chunks = h2_flat.shape[0] // T
    assert n_rows % SC_WORKERS == 0 and per_worker % rows_per_step == 0 and T % SC_LANES == 0
    mesh = plsc.VectorSubcoreMesh(core_axis_name="c", subcore_axis_name="s")

    @functools.partial(
        pl.kernel, mesh=mesh,
        out_type=[jax.ShapeDtypeStruct((n_rows * chunks, LANES), h2_flat.dtype),
                  jax.ShapeDtypeStruct((n_rows, LANES), F32)],
        scratch_types=[pltpu.VMEM((T,), I32), pltpu.VMEM((per_worker,), I32)]
        + _sc_scratch(chunks, h2_flat.dtype)
        + [pltpu.VMEM((rows_per_step, LANES), F32), pltpu.SemaphoreType.DMA, pltpu.SemaphoreType.DMA],
        compiler_params=pltpu.CompilerParams(use_tc_tiling_on_sc=True, needs_layout_passes=False),
        name="sc_dispatch",
    )
    def dispatch(h2_hbm, gate_hbm, dest_hbm, out_h_hbm, out_g_hbm,
                 dest_v, src_v, idx_v, pieces_v, gates_v, sem_h, sem_g):
        worker = lax.axis_index("s") * SC_CORES + lax.axis_index("c")
        lo = worker * per_worker
        pltpu.sync_copy(dest_hbm, dest_v)

        @pl.loop(0, per_worker // SC_LANES)
        def _(j):
            j0 = pl.multiple_of(j * SC_LANES, SC_LANES)
            src_v[pl.ds(j0, SC_LANES)] = lax.rem(lo + j0 + lax.iota(I32, SC_LANES), T)

        @pl.loop(0, T // SC_LANES)
        def _(j):
            t0 = pl.multiple_of(j * SC_LANES, SC_LANES)
            d = dest_v[pl.ds(t0, SC_LANES)] - lo
            mine = (d >= 0) & (d < per_worker)
            plsc.store_scatter(src_v, [jnp.where(mine, d, 0)], t0 + lax.iota(I32, SC_LANES), mask=mine)

        @pl.loop(0, per_worker // rows_per_step)
        def _(j):
            off = pl.multiple_of(j * rows_per_step, rows_per_step)
            pltpu.async_copy(gate_hbm.at[src_v.at[pl.ds(off, rows_per_step)]], gates_v, sem_g).wait()
            pltpu.sync_copy(gates_v, out_g_hbm.at[pl.ds(lo + off, rows_per_step)])

        _sc_move_rows(src_v, h2_hbm, out_h_hbm, lo, per_worker, idx_v, pieces_v, sem_h)

    return dispatch(h2_flat, gate_rows, dest)


def _expert_kernel(e1s, e2s, n_used, x_ref, gv_ref, wg_ref, wu_ref, wd_ref, o_ref, wup_s, wdn_s):
    groups = SORT_TILE // SUBLANES

    def one_tile(k, carry):
        t = pl.program_id(0) * EXPERT_TILES_PER_STEP + k
        e1 = e1s[t]
        e2 = e2s[t]
        prev = jnp.maximum(t - 1, 0)
        new_pair = (t == 0) | (e1 != e1s[prev]) | (e2 != e2s[prev])
        rows = pl.ds(pl.multiple_of(k * groups, groups), groups)

        @pl.when((t < n_used[0]) & new_pair)
        def _():
            for slot, e in enumerate((e1, e2)):
                wup_s[:, (2 * slot) * D_EXPERT:(2 * slot + 1) * D_EXPERT] = wg_ref[e]
                wup_s[:, (2 * slot + 1) * D_EXPERT:(2 * slot + 2) * D_EXPERT] = wu_ref[e]
                wdn_s[slot * D_EXPERT:(slot + 1) * D_EXPERT, :] = wd_ref[e]

        @pl.when(t < n_used[0])
        def _():
            x = _unpack_bf16_pairs(_load_tiles(x_ref.at[rows]))
            gv = gv_ref[pl.ds(pl.multiple_of(k * SORT_TILE, SORT_TILE), SORT_TILE), :]
            lane = lax.broadcasted_iota(I32, gv.shape, 1)
            h = jnp.dot(x, wup_s[...], preferred_element_type=F32)
            hid = []
            for slot, e in enumerate((e1, e2)):
                ge = jnp.sum(jnp.where(lane == EXPERT_LANE0 + e, gv, 0.0), axis=-1, keepdims=True)
                hg = h[:, (2 * slot) * D_EXPERT:(2 * slot + 1) * D_EXPERT]
                hu = h[:, (2 * slot + 1) * D_EXPERT:(2 * slot + 2) * D_EXPERT]
                hid.append((hg * _sigmoid(hg) * hu * ge).astype(BF16))
            out = jnp.dot(jnp.concatenate(hid, axis=1), wdn_s[...], preferred_element_type=F32)
            _store_tiles(o_ref.at[rows], _pack_bf16_pairs(out))

        @pl.when(t >= n_used[0])
        def _():
            o_ref[rows] = jnp.zeros((groups,) + o_ref.shape[1:], U32)

        return carry

    lax.fori_loop(0, EXPERT_TILES_PER_STEP, one_tile, 0)


def _experts(sorted_h2, sorted_gates, meta, wg, wu, wd):
    n_tiles = sorted_h2.shape[0] * SUBLANES // SORT_TILE
    step_rows = SORT_TILE * EXPERT_TILES_PER_STEP
    assert n_tiles % EXPERT_TILES_PER_STEP == 0

    def last_used(i, e1, e2, nu):
        return jnp.minimum(i, (nu[0] - 1) // EXPERT_TILES_PER_STEP)

    return pl.pallas_call(
        _expert_kernel,
        grid_spec=pltpu.PrefetchScalarGridSpec(
            num_scalar_prefetch=3,
            grid=(n_tiles // EXPERT_TILES_PER_STEP,),
            in_specs=[
                _tiles_spec(step_rows, last_used, PACKED_CHUNKS),
                pl.BlockSpec((step_rows, LANES), lambda *a: (last_used(*a), 0)),
                _resident(wg.shape), _resident(wu.shape), _resident(wd.shape),
            ],
            out_specs=_tiles_spec(step_rows, lambda i, *_: i, PACKED_CHUNKS),
            scratch_shapes=[pltpu.VMEM((D_MODEL, 4 * D_EXPERT), BF16), pltpu.VMEM((2 * D_EXPERT, D_MODEL), BF16)],
        ),
        out_shape=jax.ShapeDtypeStruct(_tiles_shape(n_tiles * SORT_TILE, PACKED_CHUNKS), U32),
        compiler_params=pltpu.CompilerParams(
            dimension_semantics=("arbitrary",), vmem_limit_bytes=V7X_VMEM_LIMIT_BYTES),
        name="moe_experts",
    )(meta[0, :n_tiles], meta[1, :n_tiles], meta[2, LANES - 1:LANES], sorted_h2, sorted_gates, wg, wu, wd)


def _sc_row_gather(table_flat, idx, chunks):
    n = idx.shape[0]
    per_worker = n // SC_WORKERS
    assert n % SC_WORKERS == 0 and per_worker % SC_ROWS_PER_STEP == 0
    mesh = plsc.VectorSubcoreMesh(core_axis_name="c", subcore_axis_name="s")

    @functools.partial(
        pl.kernel, mesh=mesh,
        out_type=jax.ShapeDtypeStruct((n * chunks, LANES), table_flat.dtype),
        scratch_types=[pltpu.VMEM((per_worker,), I32)] + _sc_scratch(chunks, table_flat.dtype)
        + [pltpu.SemaphoreType.DMA],
        compiler_params=pltpu.CompilerParams(use_tc_tiling_on_sc=True, needs_layout_passes=False),
        name="sc_row_gather",
    )
    def gather(table_hbm, idx_hbm, out_hbm, src_v, idx_v, pieces_v, sem):
        worker = lax.axis_index("s") * SC_CORES + lax.axis_index("c")
        lo = worker * per_worker
        pltpu.sync_copy(idx_hbm.at[pl.ds(lo, per_worker)], src_v)
        _sc_move_rows(src_v, table_hbm, out_hbm, lo, per_worker, idx_v, pieces_v, sem)

    return gather(table_flat, idx)


def _final_kernel(x_ref, moe_ref, mod_ref, gf_ref, o_ref):
    y = x_ref[...] + mod_ref[0, 5:6, :] * _unpack_bf16_pairs(_load_tiles(moe_ref)).astype(F32)
    o_ref[...] = _rms(y) * gf_ref[...]


def _final(xmid, moe_rows, mod, mod_row, gf):
    T = xmid.shape[0]
    return pl.pallas_call(
        _final_kernel,
        grid=(T // FINAL_BLOCK,),
        in_specs=[
            pl.BlockSpec((FINAL_BLOCK, D_MODEL), lambda i: (i, 0)),
            _tiles_spec(FINAL_BLOCK, lambda i: i, PACKED_CHUNKS),
            pl.BlockSpec((1, 6, D_MODEL), lambda i: (mod_row(i), 0, 0)),
            pl.BlockSpec((1, D_MODEL), lambda i: (0, 0)),
        ],
        out_specs=pl.BlockSpec((FINAL_BLOCK, D_MODEL), lambda i: (i, 0)),
        out_shape=jax.ShapeDtypeStruct((T, D_MODEL), F32),
        compiler_params=pltpu.CompilerParams(
            dimension_semantics=("arbitrary",), vmem_limit_bytes=V7X_VMEM_LIMIT_BYTES),
        name="moe_final",
    )(xmid, moe_rows, mod, gf)


def _flat(tiles):
    return tiles.reshape(-1, LANES)


def _moe_dispatch(h2_tiles, gate_rows, onehot):
    T = gate_rows.shape[0]
    n_tiles = T // SORT_TILE + N_BUCKETS
    n_rows = n_tiles * SORT_TILE
    assert n_tiles <= LANES and T % TOKEN_BLOCK == 0
    dest, meta = _plan(onehot)
    sorted_h2, sorted_gates = _sc_dispatch(_flat(h2_tiles), gate_rows, dest, n_rows)
    return sorted_h2.reshape(_tiles_shape(n_rows, PACKED_CHUNKS)), sorted_gates, dest, meta


def _moe_unpermute(moe_sorted_tiles, dest):
    chunks = moe_sorted_tiles.shape[1]
    return _sc_row_gather(_flat(moe_sorted_tiles), dest, chunks).reshape(_tiles_shape(dest.shape[0], chunks))


def _rope_tables(n_tokens):
    t = np.arange(n_tokens)
    row = (t // GRID_W).astype(np.float32)
    col = (t % GRID_W).astype(np.float32)
    freq = np.float32(ROPE_THETA) ** (-np.arange(ROPE_NF, dtype=np.float32) / np.float32(ROPE_NF))
    ang = np.concatenate([row[:, None] * freq] * 2 + [col[:, None] * freq] * 2, axis=-1)
    first = (np.arange(HEAD_DIM) % (2 * ROPE_NF)) < ROPE_NF
    sin = np.sin(ang)
    zero = np.float32(0.0)
    return (jnp.asarray(np.cos(ang)), jnp.asarray(np.where(first, -sin, zero)),
            jnp.asarray(np.where(first, zero, sin)))


def kernel(x_prompt, x_sample, cache_k, cache_v, c, c_ctx, norm1_g, norm2_g, w_ada, b_ada, w_in, q_norm_g, k_norm_g, w_pool, pool_scale, w_branch_a, w_branch_b, w_out, w_router_group, w_router_expert, w_exp_gate, w_exp_up, w_exp_down, final_norm_g):
    assert norm1_g.shape[0] == 1, "single-layer trunk"
    B, L_ctx, _ = x_prompt.shape
    Bs, L_lat, _ = x_sample.shape
    P = cache_k.shape[2]
    assert 1 + Bs <= COND_ROWS

    cond = jnp.concatenate([c_ctx[None, :], c, jnp.zeros((COND_ROWS - 1 - Bs, D_MODEL), F32)], axis=0)
    wpool_bd = jax.scipy.linalg.block_diag(*[w_pool[0, g] for g in range(len(POOL_WINDOWS))])
    mod, w_in_b, wpool_b, wa_b, wb_b, wo_b = _ada(
        cond, w_ada[0], b_ada[0][None, :],
        cast=(w_in[0], wpool_bd, w_branch_a[0], w_branch_b[0], w_out[0]))
    mod = mod.reshape(COND_ROWS, 6, D_MODEL)

    wr = jnp.concatenate([w_router_group[0], w_router_expert[0],
                          jnp.zeros((D_MODEL, LANES - N_EXP_GROUPS - N_EXPERTS), F32)], axis=1)
    wr_hi = wr.astype(BF16)
    wr_lo = (wr - wr_hi.astype(F32)).astype(BF16)
    mix_w = (norm1_g[0][None, :], w_in_b, q_norm_g[0][None, :], k_norm_g[0][None, :],
             wpool_b, pool_scale[0][None, :], wa_b, wb_b, wo_b,
             norm2_g[0][None, :], jnp.concatenate([wr_hi, wr_lo], axis=1))
    gf = final_norm_g[None, :]

    xp2 = x_prompt.reshape(B * L_ctx, D_MODEL)
    xmid_p, h2_p, gate_p, oh_p, knew, vnew, wg, wu, wd = _mix(
        xp2, mod, lambda i: 0, None, None, mix_w, S=2, L=L_ctx, emit_kv=True, blocks_per_step=2,
        cast=(w_exp_gate[0], w_exp_up[0], w_exp_down[0]))
    sh_p, sg_p, dest_p, meta_p = _moe_dispatch(h2_p, gate_p, oh_p)

    xs2 = x_sample.reshape(Bs * L_lat, D_MODEL)
    cache = (cache_k[:, 0].reshape(Bs, P, KV_W), cache_v[:, 0].reshape(Bs, P, KV_W))
    xmid_s, h2_s, gate_s, oh_s = _mix(xs2, mod, lambda i: 1 + i, cache, _rope_tables(L_lat), mix_w,
                                      S=1, L=L_lat, emit_kv=False, blocks_per_step=1)
    sh_s, sg_s, dest_s, meta_s = _moe_dispatch(h2_s, gate_s, oh_s)

    moe_p = _moe_unpermute(_experts(sh_p, sg_p, meta_p, wg, wu, wd), dest_p)
    moe_s = _moe_unpermute(_experts(sh_s, sg_s, meta_s, wg, wu, wd), dest_s)
    y_prompt = _final(xmid_p, moe_p, mod, lambda i: 0, gf)
    blocks_per_seq = L_lat // FINAL_BLOCK
    y_sample = _final(xmid_s, moe_s, mod, lambda i: 1 + i // blocks_per_seq, gf)

    return (y_prompt.reshape(B, L_ctx, D_MODEL), y_sample.reshape(Bs, L_lat, D_MODEL),
            knew.reshape(B, 1, L_ctx, N_KV_HEADS, HEAD_DIM), vnew.reshape(B, 1, L_ctx, N_KV_HEADS, HEAD_DIM))
```

```python
import functools

import numpy as np
import jax
import jax.numpy as jnp
from jax import lax
from jax.experimental import pallas as pl
from jax.experimental.pallas import tpu as pltpu
from jax.experimental.pallas import tpu_sc as plsc

F32 = jnp.float32
BF16 = jnp.bfloat16
I32 = jnp.int32
U32 = jnp.uint32

D_MODEL = 1024
HEAD_DIM = 128
N_HEADS = 8
N_KV_HEADS = 2
GROUP = N_HEADS // N_KV_HEADS
ATTN_W = N_HEADS * HEAD_DIM
KV_W = N_KV_HEADS * HEAD_DIM
POOL_WINDOWS = (2, 4, 8, 16)
POOL_GC = 128
POOL_W = POOL_GC * len(POOL_WINDOWS)
IN_W = ATTN_W + 2 * KV_W + POOL_W + 2 * D_MODEL
GATE_COL = ATTN_W + 2 * KV_W + POOL_W
GRID_W = 64
ROPE_THETA = 10000.0
ROPE_NF = HEAD_DIM // 4
N_EXP_GROUPS = 4
EXP_PER_GROUP = 4
N_EXPERTS = 16
D_EXPERT = 256
EPS = 1e-6
LOG2_E = 1.4426950408889634

LANES = 128
SUBLANES = 8
COND_ROWS = SUBLANES
POOL_HALO = 8
ROW_BLOCK = 256
ADA_COLS = 768
EXPERT_LANE0 = N_EXP_GROUPS
PAIRS_PER_GROUP = EXP_PER_GROUP * (EXP_PER_GROUP - 1) // 2
N_BUCKETS = N_EXP_GROUPS * PAIRS_PER_GROUP
SORT_TILE = 256
EXPERT_TILES_PER_STEP = 8
TOKEN_BLOCK = 1024
FINAL_BLOCK = 1024
ROW_CHUNKS = D_MODEL // LANES
SC_CORES = 2
SC_SUBCORES = 16
SC_WORKERS = SC_CORES * SC_SUBCORES
SC_LANES = 16
SC_PIECES_PER_GATHER = 128
SC_ROWS_PER_STEP = 64
PACKED_CHUNKS = ROW_CHUNKS // 2
V7X_VMEM_LIMIT_BYTES = 56 * 1024 * 1024


def _sigmoid(x):
    return 1.0 / (1.0 + jnp.exp(-x))


def _rms(x):
    return x * lax.rsqrt(jnp.mean(x * x, axis=-1, keepdims=True) + EPS)


def _resident(shape):
    zeros = (0,) * len(shape)
    return pl.BlockSpec(shape, lambda i, *_: zeros, pipeline_mode=pl.Buffered(1))


def _tiles_shape(n, chunks=ROW_CHUNKS):
    return (n // SUBLANES, chunks, SUBLANES, LANES)


def _tiles_spec(n, block_index, chunks=ROW_CHUNKS):
    return pl.BlockSpec(_tiles_shape(n, chunks), lambda *a: (block_index(*a), 0, 0, 0))


def _store_tiles(ref, x):
    for c in range(ref.shape[1]):
        ref[:, c, :, :] = x[:, c * LANES:(c + 1) * LANES].reshape(x.shape[0] // SUBLANES, SUBLANES, LANES)


def _load_tiles(ref):
    n = ref.shape[0] * SUBLANES
    return jnp.concatenate([ref[:, c, :, :].reshape(n, LANES) for c in range(ref.shape[1])], axis=1)


def _pack_bf16_pairs(x):
    bits = pltpu.bitcast(x.astype(BF16).astype(F32), U32)
    w = x.shape[1] // 2
    return bits[:, :w] | (bits[:, w:] >> 16)


def _unpack_bf16_pairs(words):
    hi = pltpu.bitcast(words & jnp.uint32(0xFFFF0000), F32).astype(BF16)
    lo = pltpu.bitcast(words << 16, F32).astype(BF16)
    return jnp.concatenate([hi, lo], axis=1)


def _row(x):
    return jnp.transpose(jnp.broadcast_to(x, (x.shape[0], LANES)))[0:1, :]


def _ada_kernel(c_ref, w_ref, b_ref, *refs):
    n_cast = (len(refs) - 1) // 2
    c = c_ref[...]
    s = (c * _sigmoid(c)).astype(BF16)
    refs[n_cast][...] = jnp.dot(s, w_ref[...].astype(BF16), preferred_element_type=F32) + b_ref[...]
    for src, dst in zip(refs[:n_cast], refs[n_cast + 1:]):
        dst[...] = src[...].astype(BF16)


def _ada(cond, w_ada, b_ada, cast=()):
    n = w_ada.shape[1]
    n_steps = n // ADA_COLS
    cast_specs = []
    for w in cast:
        assert w.ndim == 2 and w.shape[0] % (n_steps * 2 * SUBLANES) == 0
        cast_specs.append(pl.BlockSpec((w.shape[0] // n_steps, w.shape[1]), lambda j: (j, 0)))
    return pl.pallas_call(
        _ada_kernel,
        grid=(n_steps,),
        in_specs=[
            pl.BlockSpec((COND_ROWS, D_MODEL), lambda j: (0, 0)),
            pl.BlockSpec((D_MODEL, ADA_COLS), lambda j: (0, j)),
            pl.BlockSpec((1, ADA_COLS), lambda j: (0, j)),
        ] + cast_specs,
        out_specs=[pl.BlockSpec((COND_ROWS, ADA_COLS), lambda j: (0, j))] + cast_specs,
        out_shape=[jax.ShapeDtypeStruct((COND_ROWS, n), F32)] + [jax.ShapeDtypeStruct(w.shape, BF16) for w in cast],
        name="ada_mod",
    )(cond, w_ada, b_ada, *cast)


def _route(logits):
    lane = lax.broadcasted_iota(I32, logits.shape, 1).astype(F32)
    neg = jnp.float32(-1e30)
    far = jnp.float32(LANES)
    is_g = lane < N_EXP_GROUPS
    gl = jnp.where(is_g, logits, neg)
    gmax = jnp.max(gl, axis=-1, keepdims=True)
    gsel = jnp.min(jnp.where(gl == gmax, lane, far), axis=-1, keepdims=True)
    psel = 1.0 / jnp.sum(jnp.where(is_g, jnp.exp(gl - gmax), 0.0), axis=-1, keepdims=True)
    e_lo = EXPERT_LANE0 + EXP_PER_GROUP * gsel
    el = jnp.where(lane >= e_lo, jnp.where(lane < e_lo + EXP_PER_GROUP, logits, neg), neg)
    v1 = jnp.max(el, axis=-1, keepdims=True)
    i1 = jnp.min(jnp.where(el == v1, lane, far), axis=-1, keepdims=True)
    el2 = jnp.where(lane == i1, neg, el)
    v2 = jnp.max(el2, axis=-1, keepdims=True)
    i2 = jnp.min(jnp.where(el2 == v2, jnp.where(lane == i1, far, lane), far), axis=-1, keepdims=True)
    e2 = jnp.exp(v2 - v1)
    w1 = psel / (1.0 + e2)
    w2 = psel * e2 / (1.0 + e2)
    gate = jnp.where(lane == i1, w1, jnp.where(lane == i2, w2, 0.0))
    a = jnp.minimum(i1, i2) - e_lo
    b = jnp.maximum(i1, i2) - e_lo
    pair = a * (7.0 - a) * 0.5 + (b - a - 1.0)
    return gate, gsel * PAIRS_PER_GROUP + pair


def _mix_kernel(*refs, S, L, P, use_rope, emit_kv, n_cast, n_blocks, U):
    it = iter(refs)
    x_ref = next(it)
    mod_ref = next(it)
    if P:
        ck_ref = next(it)
        cv_ref = next(it)
    if use_rope:
        cos_ref = next(it)
        sneg_ref = next(it)
        spos_ref = next(it)
    (g1_ref, win_ref, qg_ref, kg_ref, wpool_ref, pscale_ref, wa_ref, wb_ref, wo_ref,
     g2_ref, wr_ref) = (next(it) for _ in range(11))
    cast_in = [next(it) for _ in range(n_cast)]
    xmid_ref = next(it)
    h2_ref = next(it)
    gate_ref = next(it)
    oh_ref = next(it)
    if emit_kv:
        knew_ref = next(it)
        vnew_ref = next(it)
    cast_out = [next(it) for _ in range(n_cast)]
    q_s, k_s, v_s, xp_s, h_s, attn_s, xm_s, mod2_s = (next(it) for _ in range(8))

    TM = S * L
    RB = ROW_BLOCK
    nrb = TM // RB
    n_steps = n_blocks // U
    score_gain = HEAD_DIM ** -0.5 * LOG2_E
    step = pl.program_id(0)
    block0 = U * jnp.minimum(step, n_steps - 1)
    slot = step % 2

    sh1 = mod_ref[0, 0:1, :]
    gain1 = g1_ref[...] * (1.0 + mod_ref[0, 1:2, :])
    gt1 = mod_ref[0, 2:3, :]
    sh2 = mod_ref[0, 3:4, :]
    gain2 = g2_ref[...] * (1.0 + mod_ref[0, 4:5, :])
    qg = qg_ref[...] * score_gain
    kg = kg_ref[...]

    def project(r, carry):
        r0 = pl.multiple_of(r * RB, RB)
        s = r0 // L
        o = pl.multiple_of(r0 % L, RB)
        hb = (_rms(x_ref[pl.ds(r0, RB), :]) * gain1 + sh1).astype(BF16)
        h_s[pl.ds(r0, RB), :] = hb
        p1 = jnp.dot(hb, win_ref[:, 0:GATE_COL], preferred_element_type=F32)
        if use_rope:
            cs = cos_ref[pl.ds(o, RB), :]
            sn = sneg_ref[pl.ds(o, RB), :]
            sp = spos_ref[pl.ds(o, RB), :]

        def rope(t):
            return (t * cs + pltpu.roll(t, HEAD_DIM - ROPE_NF, 1) * sn + pltpu.roll(t, ROPE_NF, 1) * sp)

        for hd in range(N_HEADS):
            qh = _rms(p1[:, hd * HEAD_DIM:(hd + 1) * HEAD_DIM]) * qg
            if use_rope:
                qh = rope(qh)
            q_s[hd, pl.ds(r0, RB), :] = qh.astype(BF16)
        for kh in range(N_KV_HEADS):
            c0 = ATTN_W + kh * HEAD_DIM
            kk = _rms(p1[:, c0:c0 + HEAD_DIM]) * kg
            if emit_kv:
                knew_ref[pl.ds(N_KV_HEADS * r0 + kh, RB, stride=N_KV_HEADS), :] = kk
            if use_rope:
                kk = rope(kk)
            k_s[s, pl.ds(P + o, RB), kh * HEAD_DIM:(kh + 1) * HEAD_DIM] = kk.astype(BF16)
        vv = p1[:, ATTN_W + KV_W:ATTN_W + 2 * KV_W]
        if emit_kv:
            for kh in range(N_KV_HEADS):
                vnew_ref[pl.ds(N_KV_HEADS * r0 + kh, RB, stride=N_KV_HEADS), :] = (
                    vv[:, kh * HEAD_DIM:(kh + 1) * HEAD_DIM])
        v_s[s, pl.ds(P + o, RB), :] = vv.astype(BF16)
        xp_s[s, pl.ds(POOL_HALO + o, RB), :] = p1[:, ATTN_W + 2 * KV_W:GATE_COL]
        return carry

    @pl.when(step == 0)
    def _():
        xm_s[1] = jnp.zeros((U * RB, D_MODEL), F32)
        mod2_s[1] = jnp.zeros((2, D_MODEL), F32)

    @pl.when((step < n_steps) & (step % (nrb // U) == 0))
    def _():
        if P:
            k_s[0, 0:P, :] = ck_ref[0].astype(BF16)
            v_s[0, 0:P, :] = cv_ref[0].astype(BF16)
        xp_s[:, 0:POOL_HALO, :] = jnp.zeros((S, POOL_HALO, POOL_W), F32)
        xp_s[:, L + POOL_HALO:L + 2 * POOL_HALO, :] = jnp.zeros((S, POOL_HALO, POOL_W), F32)
        lax.fori_loop(0, TM // RB, project, 0)
        for src, dst in zip(cast_in, cast_out):
            dst[...] = src[...].astype(BF16)

    def mix(u):
        r0 = pl.multiple_of(((block0 + u) % nrb) * RB, RB)
        s = r0 // L
        o = pl.multiple_of(r0 % L, RB)
        attn_u = attn_s.at[u]
        rows = slice(u * RB, (u + 1) * RB)

        for hd in range(N_HEADS):
            kh = hd // GROUP
            k = k_s[s, :, kh * HEAD_DIM:(kh + 1) * HEAD_DIM]
            v = v_s[s, :, kh * HEAD_DIM:(kh + 1) * HEAD_DIM]
            qh = q_s[hd, pl.ds(r0, RB), :]
            sc = lax.dot_general(qh, k, (((1,), (1,)), ((), ())), preferred_element_type=F32)
            e = jnp.exp2(sc - jnp.max(sc, axis=-1, keepdims=True))
            den = jnp.sum(e, axis=-1, keepdims=True)
            oh = jnp.dot(e.astype(BF16), v, preferred_element_type=F32) / den
            attn_u[:, hd * HEAD_DIM:(hd + 1) * HEAD_DIM] = oh.astype(BF16)
        a = jnp.dot(attn_u[...], wa_ref[...], preferred_element_type=F32)

        t = o + lax.broadcasted_iota(I32, (RB, 1), 0)
        RW = RB + 2 * POOL_HALO
        parts = []
        for gi, w in enumerate(POOL_WINDOWS):
            cols = slice(gi * POOL_GC, (gi + 1) * POOL_GC)
            xw = xp_s[s, pl.ds(o, RW), cols]
            run = xw
            span = 1
            while span < w:
                run = run + pltpu.roll(run, span, 0)
                span *= 2
            if w // 2 > 1:
                run = pltpu.roll(run, RW - (w // 2 - 1), 0)
            tot = run[POOL_HALO:POOL_HALO + RB]
            cnt = (jnp.minimum(t + w // 2, L) - jnp.maximum(t - w // 2, 0)).astype(F32)
            parts.append(tot / cnt - xw[POOL_HALO:POOL_HALO + RB])
        dpool = jnp.concatenate(parts, axis=1).astype(BF16)
        pooled = jnp.dot(dpool, wpool_ref[...], preferred_element_type=F32) * pscale_ref[...]
        b = jnp.dot(pooled.astype(BF16), wb_ref[...], preferred_element_type=F32)

        gates = jnp.dot(h_s[pl.ds(r0, RB), :], win_ref[:, GATE_COL:IN_W], preferred_element_type=F32)
        merged = _sigmoid(gates[:, 0:D_MODEL]) * a + _sigmoid(gates[:, D_MODEL:2 * D_MODEL]) * b
        upd = jnp.dot(merged.astype(BF16), wo_ref[...], preferred_element_type=F32)
        xm = x_ref[pl.ds(r0, RB), :] + gt1 * upd
        xmid_ref[rows, :] = xm
        xm_s[slot, rows, :] = xm

    def moe_prep(u):
        rows = slice(u * RB, (u + 1) * RB)
        h2 = _rms(xm_s[1 - slot, rows, :]) * mod2_s[1 - slot, 0:1, :] + mod2_s[1 - slot, 1:2, :]
        hi = h2.astype(BF16)
        lo = (h2 - hi.astype(F32)).astype(BF16)
        l1 = jnp.dot(hi, wr_ref[...], preferred_element_type=F32)
        l2 = jnp.dot(lo, wr_ref[:, 0:LANES], preferred_element_type=F32)
        gate, bucket = _route(l1[:, 0:LANES] + l1[:, LANES:2 * LANES] + l2)
        groups = pl.ds(u * (RB // SUBLANES), RB // SUBLANES)
        _store_tiles(h2_ref.at[groups], _pack_bf16_pairs(h2))
        gate_ref[rows, :] = gate
        lane = lax.broadcasted_iota(I32, (RB, LANES), 1).astype(F32)
        oh_ref[rows, :] = jnp.where(lane == bucket, 1.0, 0.0).astype(BF16)

    mod2_s[slot, 0:1, :] = gain2
    mod2_s[slot, 1:2, :] = sh2
    for u in range(U):
        moe_prep(u)
    for u in range(U):
        mix(u)


def _mix(x2d, mod, mod_row, cache, rope_tabs, weights, *, S, L, emit_kv, blocks_per_step, cast=()):
    T = x2d.shape[0]
    TM = S * L
    P = cache[0].shape[1] if cache is not None else 0
    use_rope = rope_tabs is not None
    assert T % TM == 0 and L % ROW_BLOCK == 0
    assert not (use_rope or P) or S == 1
    Lk = P + L

    args = [x2d, mod]
    nrb = TM // ROW_BLOCK
    n_blocks = T // ROW_BLOCK
    step_rows = blocks_per_step * ROW_BLOCK
    steps_per_group = nrb // blocks_per_step
    n_mix_steps = n_blocks // blocks_per_step
    assert nrb % blocks_per_step == 0

    def mixed(s):
        return jnp.minimum(s, n_mix_steps - 1)

    def group(s):
        return mixed(s) // steps_per_group

    def prepared(s):
        return jnp.maximum(s - 1, 0)

    in_specs = [
        pl.BlockSpec((TM, D_MODEL), lambda s: (group(s), 0)),
        pl.BlockSpec((1, 6, D_MODEL), lambda s: (mod_row(group(s)), 0, 0)),
    ]
    if P:
        args += list(cache)
        in_specs += [pl.BlockSpec((1, P, KV_W), lambda s: (group(s), 0, 0))] * 2
    if use_rope:
        args += list(rope_tabs)
        in_specs += [_resident((L, HEAD_DIM))] * 3
    args += list(weights)
    in_specs += [_resident(w.shape) for w in weights]
    n_steps = T // TM
    cast_specs = []
    for w in cast:
        assert w.shape[0] % n_steps == 0
        blk = (w.shape[0] // n_steps,) + w.shape[1:]
        cast_specs.append(pl.BlockSpec(blk, lambda s, n=len(blk): (group(s),) + (0,) * (n - 1)))
    args += list(cast)
    in_specs += cast_specs

    out_shape = [jax.ShapeDtypeStruct((T, D_MODEL), F32), jax.ShapeDtypeStruct(_tiles_shape(T, PACKED_CHUNKS), U32),
                 jax.ShapeDtypeStruct((T, LANES), F32),
                 jax.ShapeDtypeStruct((T, LANES), BF16)]
    out_specs = [pl.BlockSpec((step_rows, D_MODEL), lambda s: (mixed(s), 0)),
                 _tiles_spec(step_rows, prepared, PACKED_CHUNKS),
                 pl.BlockSpec((step_rows, LANES), lambda s: (prepared(s), 0)),
                 pl.BlockSpec((step_rows, LANES), lambda s: (prepared(s), 0))]
    if emit_kv:
        out_shape += [jax.ShapeDtypeStruct((T * N_KV_HEADS, HEAD_DIM), F32)] * 2
        out_specs += [pl.BlockSpec((TM * N_KV_HEADS, HEAD_DIM), lambda s: (group(s), 0))] * 2
    out_shape += [jax.ShapeDtypeStruct(w.shape, BF16) for w in cast]
    out_specs += cast_specs

    scratch = [
        pltpu.VMEM((N_HEADS, TM, HEAD_DIM), BF16),
        pltpu.VMEM((S, Lk, KV_W), BF16),
        pltpu.VMEM((S, Lk, KV_W), BF16),
        pltpu.VMEM((S, L + 2 * POOL_HALO, POOL_W), F32),
        pltpu.VMEM((TM, D_MODEL), BF16),
        pltpu.VMEM((blocks_per_step, ROW_BLOCK, ATTN_W), BF16),
        pltpu.VMEM((2, step_rows, D_MODEL), F32),
        pltpu.VMEM((2, 2, D_MODEL), F32),
    ]
    kern = functools.partial(_mix_kernel, S=S, L=L, P=P, use_rope=use_rope, emit_kv=emit_kv,
                             n_cast=len(cast), n_blocks=n_blocks, U=blocks_per_step)
    return pl.pallas_call(
        kern,
        grid=(n_mix_steps + 1,),
        in_specs=in_specs,
        out_specs=out_specs,
        out_shape=out_shape,
        scratch_shapes=scratch,
        compiler_params=pltpu.CompilerParams(
            dimension_semantics=("arbitrary",), vmem_limit_bytes=V7X_VMEM_LIMIT_BYTES),
        name="mixer_rope" if use_rope else "mixer_ctx",
    )(*args)


def _plan_kernel(oh_ref, dest_ref, meta_ref, *, n_blocks):
    TB = TOKEN_BLOCK
    lane = lax.broadcasted_iota(I32, (SUBLANES, LANES), 1)

    def count(b, acc):
        oh = oh_ref[pl.ds(pl.multiple_of(b * TB, TB), TB), :].astype(F32)
        return acc + jnp.sum(oh, axis=0, keepdims=True)

    counts = lax.fori_loop(0, n_blocks, count, jnp.zeros((SUBLANES, LANES), F32))
    padded = jnp.ceil(counts * (1.0 / SORT_TILE)) * SORT_TILE
    ends = padded
    step = 1
    while step < LANES:
        ends = ends + jnp.where(lane >= step, pltpu.roll(ends, step, 1), 0.0)
        step *= 2
    starts = ends - padded

    tri = jnp.where(lax.broadcasted_iota(I32, (TB, TB), 1) < lax.broadcasted_iota(I32, (TB, TB), 0),
                    1.0, 0.0).astype(BF16)

    def place(b, seen):
        oh = oh_ref[pl.ds(pl.multiple_of(b * TB, TB), TB), :]
        ohf = oh.astype(F32)
        rank = jnp.dot(tri, oh, preferred_element_type=F32)
        base = (starts + seen)[0:1, :]
        d = jnp.sum(ohf * (rank + base), axis=1, keepdims=True)
        dest_ref[b] = _row(d).astype(I32)
        return seen + jnp.sum(ohf, axis=0, keepdims=True)

    lax.fori_loop(0, n_blocks, place, jnp.zeros((SUBLANES, LANES), F32))

    tile_row0 = lax.broadcasted_iota(I32, (LANES, LANES), 0).astype(F32) * SORT_TILE
    is_bucket = lax.broadcasted_iota(I32, (LANES, LANES), 1) < N_BUCKETS
    done = jnp.sum(jnp.where(is_bucket, jnp.where(ends[0:1, :] <= tile_row0, 1.0, 0.0), 0.0),
                   axis=1, keepdims=True)
    bkt = jnp.minimum(done, N_BUCKETS - 1.0)
    grp = (jnp.where(bkt >= PAIRS_PER_GROUP, 1.0, 0.0) + jnp.where(bkt >= 2 * PAIRS_PER_GROUP, 1.0, 0.0)
           + jnp.where(bkt >= 3 * PAIRS_PER_GROUP, 1.0, 0.0))
    pair = bkt - PAIRS_PER_GROUP * grp
    a = jnp.where(pair >= 3.0, 1.0, 0.0) + jnp.where(pair >= 5.0, 1.0, 0.0)
    b = pair - a * (7.0 - a) * 0.5 + a + 1.0
    e1 = EXP_PER_GROUP * grp + a
    e2 = EXP_PER_GROUP * grp + b
    meta = jnp.concatenate(
        [_row(e1), _row(e2), ends[0:1, :] * (1.0 / SORT_TILE), jnp.zeros((SUBLANES - 3, LANES), F32)], axis=0)
    meta_ref[...] = meta.astype(I32)


def _plan(onehot):
    T = onehot.shape[0]
    n_blocks = T // TOKEN_BLOCK
    dest, meta = pl.pallas_call(
        functools.partial(_plan_kernel, n_blocks=n_blocks),
        out_shape=[jax.ShapeDtypeStruct((n_blocks, 1, TOKEN_BLOCK), I32),
                   jax.ShapeDtypeStruct((SUBLANES, LANES), I32)],
        name="moe_plan",
    )(onehot)
    return dest.reshape(T), meta


def _sc_move_rows(src_v, table_hbm, out_hbm, lo, n_rows, idx_v, pieces_v, sem):
    chunks = pieces_v.shape[0] // SC_ROWS_PER_STEP
    lane = lax.iota(I32, SC_LANES)
    row_in_group = lane & (SUBLANES - 1)
    chunk_in_pair = lane >> 3
    rows_per_gather = SC_PIECES_PER_GATHER // chunks

    @pl.loop(0, n_rows // SC_ROWS_PER_STEP)
    def _(step):
        copies = []
        for g in range(SC_ROWS_PER_STEP // rows_per_gather):
            r0 = step * SC_ROWS_PER_STEP + g * rows_per_gather
            for v in range(SC_PIECES_PER_GATHER // SC_LANES):
                group, chunk0 = v // (chunks // 2), 2 * (v % (chunks // 2))
                tok = plsc.load_gather(src_v, [r0 + group * SUBLANES + row_in_group])
                piece = (tok >> 3) * (SUBLANES * chunks) + (chunk0 + chunk_in_pair) * SUBLANES + (tok & 7)
                idx_v[pl.ds(g * SC_PIECES_PER_GATHER + v * SC_LANES, SC_LANES)] = piece
            window = pl.ds(g * SC_PIECES_PER_GATHER, SC_PIECES_PER_GATHER)
            copies.append(pltpu.async_copy(table_hbm.at[idx_v.at[window]], pieces_v.at[window], sem))
        for cp in copies:
            cp.wait()
        first = pl.multiple_of((lo + step * SC_ROWS_PER_STEP) * chunks, SC_ROWS_PER_STEP * chunks)
        pltpu.sync_copy(pieces_v, out_hbm.at[pl.ds(first, SC_ROWS_PER_STEP * chunks)])


def _sc_scratch(chunks, dtype):
    return [pltpu.VMEM((SC_ROWS_PER_STEP * chunks,), I32), pltpu.VMEM((SC_ROWS_PER_STEP * chunks, LANES), dtype)]


def _sc_dispatch(h2_flat, gate_rows, dest, n_rows):
    T = dest.shape[0]
    per_worker = n_rows // SC_WORKERS
    rows_per_step = SC_ROWS_PER_STEP
    chunks = h2_flat.shape[0] // T
    assert n_rows % SC_WORKERS == 0 and per_worker % rows_per_step == 0 and T % SC_LANES == 0
    mesh = plsc.VectorSubcoreMesh(core_axis_name="c", subcore_axis_name="s")

    @functools.partial(
        pl.kernel, mesh=mesh,
        out_type=[jax.ShapeDtypeStruct((n_rows * chunks, LANES), h2_flat.dtype),
                  jax.ShapeDtypeStruct((n_rows, LANES), F32)],
        scratch_types=[pltpu.VMEM((T,), I32), pltpu.VMEM((per_worker,), I32)]
        + _sc_scratch(chunks, h2_flat.dtype)
        + [pltpu.VMEM((rows_per_step, LANES), F32), pltpu.SemaphoreType.DMA, pltpu.SemaphoreType.DMA],
        compiler_params=pltpu.CompilerParams(use_tc_tiling_on_sc=True, needs_layout_passes=False),
        name="sc_dispatch",
    )
    def dispatch(h2_hbm, gate_hbm, dest_hbm, out_h_hbm, out_g_hbm,
                 dest_v, src_v, idx_v, pieces_v, gates_v, sem_h, sem_g):
        worker = lax.axis_index("s") * SC_CORES + lax.axis_index("c")
        lo = worker * per_worker
        pltpu.sync_copy(dest_hbm, dest_v)

        @pl.loop(0, per_worker // SC_LANES)
        def _(j):
            j0 = pl.multiple_of(j * SC_LANES, SC_LANES)
            src_v[pl.ds(j0, SC_LANES)] = lax.rem(lo + j0 + lax.iota(I32, SC_LANES), T)

        @pl.loop(0, T // SC_LANES)
        def _(j):
            t0 = pl.multiple_of(j * SC_LANES, SC_LANES)
            d = dest_v[pl.ds(t0, SC_LANES)] - lo
            mine = (d >= 0) & (d < per_worker)
            plsc.store_scatter(src_v, [jnp.where(mine, d, 0)], t0 + lax.iota(I32, SC_LANES), mask=mine)

        @pl.loop(0, per_worker // rows_per_step)
        def _(j):
            off = pl.multiple_of(j * rows_per_step, rows_per_step)
            pltpu.async_copy(gate_hbm.at[src_v.at[pl.ds(off, rows_per_step)]], gates_v, sem_g).wait()
            pltpu.sync_copy(gates_v, out_g_hbm.at[pl.ds(lo + off, rows_per_step)])

        _sc_move_rows(src_v, h2_hbm, out_h_hbm, lo, per_worker, idx_v, pieces_v, sem_h)

    return dispatch(h2_flat, gate_rows, dest)


def _expert_kernel(e1s, e2s, n_used, x_ref, gv_ref, wg_ref, wu_ref, wd_ref, o_ref, wup_s, wdn_s):
    groups = SORT_TILE // SUBLANES

    def one_tile(k, carry):
        t = pl.program_id(0) * EXPERT_TILES_PER_STEP + k
        e1 = e1s[t]
        e2 = e2s[t]
        prev = jnp.maximum(t - 1, 0)
        new_pair = (t == 0) | (e1 != e1s[prev]) | (e2 != e2s[prev])
        rows = pl.ds(pl.multiple_of(k * groups, groups), groups)

        @pl.when((t < n_used[0]) & new_pair)
        def _():
            for slot, e in enumerate((e1, e2)):
                wup_s[:, (2 * slot) * D_EXPERT:(2 * slot + 1) * D_EXPERT] = wg_ref[e]
                wup_s[:, (2 * slot + 1) * D_EXPERT:(2 * slot + 2) * D_EXPERT] = wu_ref[e]
                wdn_s[slot * D_EXPERT:(slot + 1) * D_EXPERT, :] = wd_ref[e]

        @pl.when(t < n_used[0])
        def _():
            x = _unpack_bf16_pairs(_load_tiles(x_ref.at[rows]))
            gv = gv_ref[pl.ds(pl.multiple_of(k * SORT_TILE, SORT_TILE), SORT_TILE), :]
            lane = lax.broadcasted_iota(I32, gv.shape, 1)
            h = jnp.dot(x, wup_s[...], preferred_element_type=F32)
            hid = []
            for slot, e in enumerate((e1, e2)):
                ge = jnp.sum(jnp.where(lane == EXPERT_LANE0 + e, gv, 0.0), axis=-1, keepdims=True)
                hg = h[:, (2 * slot) * D_EXPERT:(2 * slot + 1) * D_EXPERT]
                hu = h[:, (2 * slot + 1) * D_EXPERT:(2 * slot + 2) * D_EXPERT]
                hid.append((hg * _sigmoid(hg) * hu * ge).astype(BF16))
            out = jnp.dot(jnp.concatenate(hid, axis=1), wdn_s[...], preferred_element_type=F32)
            _store_tiles(o_ref.at[rows], _pack_bf16_pairs(out))

        @pl.when(t >= n_used[0])
        def _():
            o_ref[rows] = jnp.zeros((groups,) + o_ref.shape[1:], U32)

        return carry

    lax.fori_loop(0, EXPERT_TILES_PER_STEP, one_tile, 0)


def _experts(sorted_h2, sorted_gates, meta, wg, wu, wd):
    n_tiles = sorted_h2.shape[0] * SUBLANES // SORT_TILE
    step_rows = SORT_TILE * EXPERT_TILES_PER_STEP
    assert n_tiles % EXPERT_TILES_PER_STEP == 0

    def last_used(i, e1, e2, nu):
        return jnp.minimum(i, (nu[0] - 1) // EXPERT_TILES_PER_STEP)

    return pl.pallas_call(
        _expert_kernel,
        grid_spec=pltpu.PrefetchScalarGridSpec(
            num_scalar_prefetch=3,
            grid=(n_tiles // EXPERT_TILES_PER_STEP,),
            in_specs=[
                _tiles_spec(step_rows, last_used, PACKED_CHUNKS),
                pl.BlockSpec((step_rows, LANES), lambda *a: (last_used(*a), 0)),
                _resident(wg.shape), _resident(wu.shape), _resident(wd.shape),
            ],
            out_specs=_tiles_spec(step_rows, lambda i, *_: i, PACKED_CHUNKS),
            scratch_shapes=[pltpu.VMEM((D_MODEL, 4 * D_EXPERT), BF16), pltpu.VMEM((2 * D_EXPERT, D_MODEL), BF16)],
        ),
        out_shape=jax.ShapeDtypeStruct(_tiles_shape(n_tiles * SORT_TILE, PACKED_CHUNKS), U32),
        compiler_params=pltpu.CompilerParams(
            dimension_semantics=("arbitrary",), vmem_limit_bytes=V7X_VMEM_LIMIT_BYTES),
        name="moe_experts",
    )(meta[0, :n_tiles], meta[1, :n_tiles], meta[2, LANES - 1:LANES], sorted_h2, sorted_gates, wg, wu, wd)


def _sc_row_gather(table_flat, idx, chunks):
    n = idx.shape[0]
    per_worker = n // SC_WORKERS
    assert n % SC_WORKERS == 0 and per_worker % SC_ROWS_PER_STEP == 0
    mesh = plsc.VectorSubcoreMesh(core_axis_name="c", subcore_axis_name="s")

    @functools.partial(
        pl.kernel, mesh=mesh,
        out_type=jax.ShapeDtypeStruct((n * chunks, LANES), table_flat.dtype),
        scratch_types=[pltpu.VMEM((per_worker,), I32)] + _sc_scratch(chunks, table_flat.dtype)
        + [pltpu.SemaphoreType.DMA],
        compiler_params=pltpu.CompilerParams(use_tc_tiling_on_sc=True, needs_layout_passes=False),
        name="sc_row_gather",
    )
    def gather(table_hbm, idx_hbm, out_hbm, src_v, idx_v, pieces_v, sem):
        worker = lax.axis_index("s") * SC_CORES + lax.axis_index("c")
        lo = worker * per_worker
        pltpu.sync_copy(idx_hbm.at[pl.ds(lo, per_worker)], src_v)
        _sc_move_rows(src_v, table_hbm, out_hbm, lo, per_worker, idx_v, pieces_v, sem)

    return gather(table_flat, idx)


def _final_kernel(x_ref, moe_ref, mod_ref, gf_ref, o_ref):
    y = x_ref[...] + mod_ref[0, 5:6, :] * _unpack_bf16_pairs(_load_tiles(moe_ref)).astype(F32)
    o_ref[...] = _rms(y) * gf_ref[...]


def _final(xmid, moe_rows, mod, mod_row, gf):
    T = xmid.shape[0]
    return pl.pallas_call(
        _final_kernel,
        grid=(T // FINAL_BLOCK,),
        in_specs=[
            pl.BlockSpec((FINAL_BLOCK, D_MODEL), lambda i: (i, 0)),
            _tiles_spec(FINAL_BLOCK, lambda i: i, PACKED_CHUNKS),
            pl.BlockSpec((1, 6, D_MODEL), lambda i: (mod_row(i), 0, 0)),
            pl.BlockSpec((1, D_MODEL), lambda i: (0, 0)),
        ],
        out_specs=pl.BlockSpec((FINAL_BLOCK, D_MODEL), lambda i: (i, 0)),
        out_shape=jax.ShapeDtypeStruct((T, D_MODEL), F32),
        compiler_params=pltpu.CompilerParams(
            dimension_semantics=("arbitrary",), vmem_limit_bytes=V7X_VMEM_LIMIT_BYTES),
        name="moe_final",
    )(xmid, moe_rows, mod, gf)


def _flat(tiles):
    return tiles.reshape(-1, LANES)


def _moe_dispatch(h2_tiles, gate_rows, onehot):
    T = gate_rows.shape[0]
    n_tiles = T // SORT_TILE + N_BUCKETS
    n_rows = n_tiles * SORT_TILE
    assert n_tiles <= LANES and T % TOKEN_BLOCK == 0
    dest, meta = _plan(onehot)
    sorted_h2, sorted_gates = _sc_dispatch(_flat(h2_tiles), gate_rows, dest, n_rows)
    return sorted_h2.reshape(_tiles_shape(n_rows, PACKED_CHUNKS)), sorted_gates, dest, meta


def _moe_unpermute(moe_sorted_tiles, dest):
    chunks = moe_sorted_tiles.shape[1]
    return _sc_row_gather(_flat(moe_sorted_tiles), dest, chunks).reshape(_tiles_shape(dest.shape[0], chunks))


def _rope_tables(n_tokens):
    t = np.arange(n_tokens)
    row = (t // GRID_W).astype(np.float32)
    col = (t % GRID_W).astype(np.float32)
    freq = np.float32(ROPE_THETA) ** (-np.arange(ROPE_NF, dtype=np.float32) / np.float32(ROPE_NF))
    ang = np.concatenate([row[:, None] * freq] * 2 + [col[:, None] * freq] * 2, axis=-1)
    first = (np.arange(HEAD_DIM) % (2 * ROPE_NF)) < ROPE_NF
    sin = np.sin(ang)
    zero = np.float32(0.0)
    return (jnp.asarray(np.cos(ang)), jnp.asarray(np.where(first, -sin, zero)),
            jnp.asarray(np.where(first, zero, sin)))


def kernel(x_prompt, x_sample, cache_k, cache_v, c, c_ctx, norm1_g, norm2_g, w_ada, b_ada, w_in, q_norm_g, k_norm_g, w_pool, pool_scale, w_branch_a, w_branch_b, w_out, w_router_group, w_router_expert, w_exp_gate, w_exp_up, w_exp_down, final_norm_g):
    assert norm1_g.shape[0] == 1, "single-layer trunk"
    B, L_ctx, _ = x_prompt.shape
    Bs, L_lat, _ = x_sample.shape
    P = cache_k.shape[2]
    assert 1 + Bs <= COND_ROWS

    cond = jnp.concatenate([c_ctx[None, :], c, jnp.zeros((COND_ROWS - 1 - Bs, D_MODEL), F32)], axis=0)
    wpool_bd = jax.scipy.linalg.block_diag(*[w_pool[0, g] for g in range(len(POOL_WINDOWS))])
    mod, w_in_b, wpool_b, wa_b, wb_b, wo_b = _ada(
        cond, w_ada[0], b_ada[0][None, :],
        cast=(w_in[0], wpool_bd, w_branch_a[0], w_branch_b[0], w_out[0]))
    mod = mod.reshape(COND_ROWS, 6, D_MODEL)

    wr = jnp.concatenate([w_router_group[0], w_router_expert[0],
                          jnp.zeros((D_MODEL, LANES - N_EXP_GROUPS - N_EXPERTS), F32)], axis=1)
    wr_hi = wr.astype(BF16)
    wr_lo = (wr - wr_hi.astype(F32)).astype(BF16)
    mix_w = (norm1_g[0][None, :], w_in_b, q_norm_g[0][None, :], k_norm_g[0][None, :],
             wpool_b, pool_scale[0][None, :], wa_b, wb_b, wo_b,
             norm2_g[0][None, :], jnp.concatenate([wr_hi, wr_lo], axis=1))
    gf = final_norm_g[None, :]

    xp2 = x_prompt.reshape(B * L_ctx, D_MODEL)
    xmid_p, h2_p, gate_p, oh_p, knew, vnew, wg, wu, wd = _mix(
        xp2, mod, lambda i: 0, None, None, mix_w, S=2, L=L_ctx, emit_kv=True, blocks_per_step=2,
        cast=(w_exp_gate[0], w_exp_up[0], w_exp_down[0]))
    sh_p, sg_p, dest_p, meta_p = _moe_dispatch(h2_p, gate_p, oh_p)

    xs2 = x_sample.reshape(Bs * L_lat, D_MODEL)
    cache = (cache_k[:, 0].reshape(Bs, P, KV_W), cache_v[:, 0].reshape(Bs, P, KV_W))
    xmid_s, h2_s, gate_s, oh_s = _mix(xs2, mod, lambda i: 1 + i, cache, _rope_tables(L_lat), mix_w,
                                      S=1, L=L_lat, emit_kv=False, blocks_per_step=1)
    sh_s, sg_s, dest_s, meta_s = _moe_dispatch(h2_s, gate_s, oh_s)

    moe_p = _moe_unpermute(_experts(sh_p, sg_p, meta_p, wg, wu, wd), dest_p)
    moe_s = _moe_unpermute(_experts(sh_s, sg_s, meta_s, wg, wu, wd), dest_s)
    y_prompt = _final(xmid_p, moe_p, mod, lambda i: 0, gf)
    blocks_per_seq = L_lat // FINAL_BLOCK
    y_sample = _final(xmid_s, moe_s, mod, lambda i: 1 + i // blocks_per_seq, gf)

    return (y_prompt.reshape(B, L_ctx, D_MODEL), y_sample.reshape(Bs, L_lat, D_MODEL),
            knew.reshape(B, 1, L_ctx, N_KV_HEADS, HEAD_DIM), vnew.reshape(B, 1, L_ctx, N_KV_HEADS, HEAD_DIM))
```

```python
import functools

import numpy as np
import jax
import jax.numpy as jnp
from jax import lax
from jax.experimental import pallas as pl
from jax.experimental.pallas import tpu as pltpu
from jax.experimental.pallas import tpu_sc as plsc

F32 = jnp.float32
BF16 = jnp.bfloat16
I32 = jnp.int32
U32 = jnp.uint32

D_MODEL = 1024
HEAD_DIM = 128
N_HEADS = 8
N_KV_HEADS = 2
GROUP = N_HEADS // N_KV_HEADS
ATTN_W = N_HEADS * HEAD_DIM
KV_W = N_KV_HEADS * HEAD_DIM
POOL_WINDOWS = (2, 4, 8, 16)
POOL_GC = 128
POOL_W = POOL_GC * len(POOL_WINDOWS)
IN_W = ATTN_W + 2 * KV_W + POOL_W + 2 * D_MODEL
GATE_COL = ATTN_W + 2 * KV_W + POOL_W
GRID_W = 64
ROPE_THETA = 10000.0
ROPE_NF = HEAD_DIM // 4
N_EXP_GROUPS = 4
EXP_PER_GROUP = 4
N_EXPERTS = 16
D_EXPERT = 256
EPS = 1e-6
LOG2_E = 1.4426950408889634

LANES = 128
SUBLANES = 8
COND_ROWS = SUBLANES
POOL_HALO = 8
ROW_BLOCK = 256
ADA_COLS = 768
EXPERT_LANE0 = N_EXP_GROUPS
PAIRS_PER_GROUP = EXP_PER_GROUP * (EXP_PER_GROUP - 1) // 2
N_BUCKETS = N_EXP_GROUPS * PAIRS_PER_GROUP
SORT_TILE = 256
EXPERT_TILES_PER_STEP = 8
TOKEN_BLOCK = 1024
FINAL_BLOCK = 1024
ROW_CHUNKS = D_MODEL // LANES
SC_CORES = 2
SC_SUBCORES = 16
SC_WORKERS = SC_CORES * SC_SUBCORES
SC_LANES = 16
SC_PIECES_PER_GATHER = 128
SC_ROWS_PER_STEP = 64
PACKED_CHUNKS = ROW_CHUNKS // 2
V7X_VMEM_LIMIT_BYTES = 56 * 1024 * 1024


def _sigmoid(x):
    return 1.0 / (1.0 + jnp.exp(-x))


def _rms(x):
    return x * lax.rsqrt(jnp.mean(x * x, axis=-1, keepdims=True) + EPS)


def _resident(shape):
    zeros = (0,) * len(shape)
    return pl.BlockSpec(shape, lambda i, *_: zeros, pipeline_mode=pl.Buffered(1))


def _tiles_shape(n, chunks=ROW_CHUNKS):
    return (n // SUBLANES, chunks, SUBLANES, LANES)


def _tiles_spec(n, block_index, chunks=ROW_CHUNKS):
    return pl.BlockSpec(_tiles_shape(n, chunks), lambda *a: (block_index(*a), 0, 0, 0))


def _store_tiles(ref, x):
    for c in range(ref.shape[1]):
        ref[:, c, :, :] = x[:, c * LANES:(c + 1) * LANES].reshape(x.shape[0] // SUBLANES, SUBLANES, LANES)


def _load_tiles(ref):
    n = ref.shape[0] * SUBLANES
    return jnp.concatenate([ref[:, c, :, :].reshape(n, LANES) for c in range(ref.shape[1])], axis=1)


def _pack_bf16_pairs(x):
    bits = pltpu.bitcast(x.astype(BF16).astype(F32), U32)
    w = x.shape[1] // 2
    return bits[:, :w] | (bits[:, w:] >> 16)


def _unpack_bf16_pairs(words):
    hi = pltpu.bitcast(words & jnp.uint32(0xFFFF0000), F32).astype(BF16)
    lo = pltpu.bitcast(words << 16, F32).astype(BF16)
    return jnp.concatenate([hi, lo], axis=1)


def _row(x):
    return jnp.transpose(jnp.broadcast_to(x, (x.shape[0], LANES)))[0:1, :]


def _ada_kernel(c_ref, w_ref, b_ref, *refs):
    n_cast = (len(refs) - 1) // 2
    c = c_ref[...]
    s = (c * _sigmoid(c)).astype(BF16)
    refs[n_cast][...] = jnp.dot(s, w_ref[...].astype(BF16), preferred_element_type=F32) + b_ref[...]
    for src, dst in zip(refs[:n_cast], refs[n_cast + 1:]):
        dst[...] = src[...].astype(BF16)


def _ada(cond, w_ada, b_ada, cast=()):
    n = w_ada.shape[1]
    n_steps = n // ADA_COLS
    cast_specs = []
    for w in cast:
        assert w.ndim == 2 and w.shape[0] % (n_steps * 2 * SUBLANES) == 0
        cast_specs.append(pl.BlockSpec((w.shape[0] // n_steps, w.shape[1]), lambda j: (j, 0)))
    return pl.pallas_call(
        _ada_kernel,
        grid=(n_steps,),
        in_specs=[
            pl.BlockSpec((COND_ROWS, D_MODEL), lambda j: (0, 0)),
            pl.BlockSpec((D_MODEL, ADA_COLS), lambda j: (0, j)),
            pl.BlockSpec((1, ADA_COLS), lambda j: (0, j)),
        ] + cast_specs,
        out_specs=[pl.BlockSpec((COND_ROWS, ADA_COLS), lambda j: (0, j))] + cast_specs,
        out_shape=[jax.ShapeDtypeStruct((COND_ROWS, n), F32)] + [jax.ShapeDtypeStruct(w.shape, BF16) for w in cast],
        name="ada_mod",
    )(cond, w_ada, b_ada, *cast)


def _route(logits):
    lane = lax.broadcasted_iota(I32, logits.shape, 1).astype(F32)
    neg = jnp.float32(-1e30)
    far = jnp.float32(LANES)
    is_g = lane < N_EXP_GROUPS
    gl = jnp.where(is_g, logits, neg)
    gmax = jnp.max(gl, axis=-1, keepdims=True)
    gsel = jnp.min(jnp.where(gl == gmax, lane, far), axis=-1, keepdims=True)
    psel = 1.0 / jnp.sum(jnp.where(is_g, jnp.exp(gl - gmax), 0.0), axis=-1, keepdims=True)
    e_lo = EXPERT_LANE0 + EXP_PER_GROUP * gsel
    el = jnp.where(lane >= e_lo, jnp.where(lane < e_lo + EXP_PER_GROUP, logits, neg), neg)
    v1 = jnp.max(el, axis=-1, keepdims=True)
    i1 = jnp.min(jnp.where(el == v1, lane, far), axis=-1, keepdims=True)
    el2 = jnp.where(lane == i1, neg, el)
    v2 = jnp.max(el2, axis=-1, keepdims=True)
    i2 = jnp.min(jnp.where(el2 == v2, jnp.where(lane == i1, far, lane), far), axis=-1, keepdims=True)
    e2 = jnp.exp(v2 - v1)
    w1 = psel / (1.0 + e2)
    w2 = psel * e2 / (1.0 + e2)
    gate = jnp.where(lane == i1, w1, jnp.where(lane == i2, w2, 0.0))
    a = jnp.minimum(i1, i2) - e_lo
    b = jnp.maximum(i1, i2) - e_lo
    pair = a * (7.0 - a) * 0.5 + (b - a - 1.0)
    return gate, gsel * PAIRS_PER_GROUP + pair


def _mix_kernel(*refs, S, L, P, use_rope, emit_kv, n_cast, n_blocks, U):
    it = iter(refs)
    x_ref = next(it)
    mod_ref = next(it)
    if P:
        ck_ref = next(it)
        cv_ref = next(it)
    if use_rope:
        cos_ref = next(it)
        sneg_ref = next(it)
        spos_ref = next(it)
    (g1_ref, win_ref, qg_ref, kg_ref, wpool_ref, pscale_ref, wa_ref, wb_ref, wo_ref,
     g2_ref, wr_ref) = (next(it) for _ in range(11))
    cast_in = [[next(it) for _ in range(n)] for n in n_cast]
    xmid_ref = next(it)
    h2_ref = next(it)
    gate_ref = next(it)
    oh_ref = next(it)
    if emit_kv:
        knew_ref = next(it)
        vnew_ref = next(it)
    cast_out = [next(it) for _ in n_cast]
    q_s, k_s, v_s, xp_s, h_s, attn_s, xm_s, mod2_s = (next(it) for _ in range(8))

    TM = S * L
    RB = ROW_BLOCK
    nrb = TM // RB
    n_steps = n_blocks // U
    score_gain = HEAD_DIM ** -0.5 * LOG2_E
    step = pl.program_id(0)
    block0 = U * jnp.minimum(step, n_steps - 1)
    slot = step % 2

    sh1 = mod_ref[0, 0:1, :]
    gain1 = g1_ref[...] * (1.0 + mod_ref[0, 1:2, :])
    gt1 = mod_ref[0, 2:3, :]
    sh2 = mod_ref[0, 3:4, :]
    gain2 = g2_ref[...] * (1.0 + mod_ref[0, 4:5, :])
    qg = qg_ref[...] * score_gain
    kg = kg_ref[...]

    def project(r, carry):
        r0 = pl.multiple_of(r * RB, RB)
        s = r0 // L
        o = pl.multiple_of(r0 % L, RB)
        hb = (_rms(x_ref[pl.ds(r0, RB), :]) * gain1 + sh1).astype(BF16)
        h_s[pl.ds(r0, RB), :] = hb
        p1 = jnp.dot(hb, win_ref[:, 0:GATE_COL], preferred_element_type=F32)
        if use_rope:
            cs = cos_ref[pl.ds(o, RB), :]
            sn = sneg_ref[pl.ds(o, RB), :]
            sp = spos_ref[pl.ds(o, RB), :]

        def rope(t):
            return (t * cs + pltpu.roll(t, HEAD_DIM - ROPE_NF, 1) * sn + pltpu.roll(t, ROPE_NF, 1) * sp)

        for hd in range(N_HEADS):
            qh = _rms(p1[:, hd * HEAD_DIM:(hd + 1) * HEAD_DIM]) * qg
            if use_rope:
                qh = rope(qh)
            q_s[hd, pl.ds(r0, RB), :] = qh.astype(BF16)
        for kh in range(N_KV_HEADS):
            c0 = ATTN_W + kh * HEAD_DIM
            kk = _rms(p1[:, c0:c0 + HEAD_DIM]) * kg
            if emit_kv:
                knew_ref[pl.ds(N_KV_HEADS * r0 + kh, RB, stride=N_KV_HEADS), :] = kk
            if use_rope:
                kk = rope(kk)
            k_s[s, pl.ds(P + o, RB), kh * HEAD_DIM:(kh + 1) * HEAD_DIM] = kk.astype(BF16)
        vv = p1[:, ATTN_W + KV_W:ATTN_W + 2 * KV_W]
        if emit_kv:
            for kh in range(N_KV_HEADS):
                vnew_ref[pl.ds(N_KV_HEADS * r0 + kh, RB, stride=N_KV_HEADS), :] = (
                    vv[:, kh * HEAD_DIM:(kh + 1) * HEAD_DIM])
        v_s[s, pl.ds(P + o, RB), :] = vv.astype(BF16)
        xp_s[s, pl.ds(POOL_HALO + o, RB), :] = p1[:, ATTN_W + 2 * KV_W:GATE_COL]
        return carry

    @pl.when(step == 0)
    def _():
        xm_s[1] = jnp.zeros((U * RB, D_MODEL), F32)
        mod2_s[1] = jnp.zeros((2, D_MODEL), F32)

    @pl.when((step < n_steps) & (step % (nrb // U) == 0))
    def _():
        if P:
            k_s[0, 0:P, :] = ck_ref[0].astype(BF16)
            v_s[0, 0:P, :] = cv_ref[0].astype(BF16)
        xp_s[:, 0:POOL_HALO, :] = jnp.zeros((S, POOL_HALO, POOL_W), F32)
        xp_s[:, L + POOL_HALO:L + 2 * POOL_HALO, :] = jnp.zeros((S, POOL_HALO, POOL_W), F32)
        lax.fori_loop(0, TM // RB, project, 0)
        for srcs, dst in zip(cast_in, cast_out):
            col = 0
            for src in srcs:
                dst[..., col:col + src.shape[-1]] = src[...].astype(BF16)
                col += src.shape[-1]

    def mix(u):
        r0 = pl.multiple_of(((block0 + u) % nrb) * RB, RB)
        s = r0 // L
        o = pl.multiple_of(r0 % L, RB)
        attn_u = attn_s.at[u]
        rows = slice(u * RB, (u + 1) * RB)

        for hd in range(N_HEADS):
            kh = hd // GROUP
            k = k_s[s, :, kh * HEAD_DIM:(kh + 1) * HEAD_DIM]
            v = v_s[s, :, kh * HEAD_DIM:(kh + 1) * HEAD_DIM]
            qh = q_s[hd, pl.ds(r0, RB), :]
            sc = lax.dot_general(qh, k, (((1,), (1,)), ((), ())), preferred_element_type=F32)
            e = jnp.exp2(sc - jnp.max(sc, axis=-1, keepdims=True))
            den = jnp.sum(e, axis=-1, keepdims=True)
            oh = jnp.dot(e.astype(BF16), v, preferred_element_type=F32) / den
            attn_u[:, hd * HEAD_DIM:(hd + 1) * HEAD_DIM] = oh.astype(BF16)
        a = jnp.dot(attn_u[...], wa_ref[...], preferred_element_type=F32)

        t = o + lax.broadcasted_iota(I32, (RB, 1), 0)
        RW = RB + 2 * POOL_HALO
        parts = []
        for gi, w in enumerate(POOL_WINDOWS):
            cols = slice(gi * POOL_GC, (gi + 1) * POOL_GC)
            xw = xp_s[s, pl.ds(o, RW), cols]
            run = xw
            span = 1
            while span < w:
                run = run + pltpu.roll(run, span, 0)
                span *= 2
            if w // 2 > 1:
                run = pltpu.roll(run, RW - (w // 2 - 1), 0)
            tot = run[POOL_HALO:POOL_HALO + RB]
            cnt = (jnp.minimum(t + w // 2, L) - jnp.maximum(t - w // 2, 0)).astype(F32)
            parts.append(tot / cnt - xw[POOL_HALO:POOL_HALO + RB])
        dpool = jnp.concatenate(parts, axis=1).astype(BF16)
        pooled = jnp.dot(dpool, wpool_ref[...], preferred_element_type=F32) * pscale_ref[...]
        b = jnp.dot(pooled.astype(BF16), wb_ref[...], preferred_element_type=F32)

        gates = jnp.dot(h_s[pl.ds(r0, RB), :], win_ref[:, GATE_COL:IN_W], preferred_element_type=F32)
        merged = _sigmoid(gates[:, 0:D_MODEL]) * a + _sigmoid(gates[:, D_MODEL:2 * D_MODEL]) * b
        upd = jnp.dot(merged.astype(BF16), wo_ref[...], preferred_element_type=F32)
        xm = x_ref[pl.ds(r0, RB), :] + gt1 * upd
        xmid_ref[rows, :] = xm
        xm_s[slot, rows, :] = xm

    def moe_prep(u):
        rows = slice(u * RB, (u + 1) * RB)
        h2 = _rms(xm_s[1 - slot, rows, :]) * mod2_s[1 - slot, 0:1, :] + mod2_s[1 - slot, 1:2, :]
        hi = h2.astype(BF16)
        lo = (h2 - hi.astype(F32)).astype(BF16)
        l1 = jnp.dot(hi, wr_ref[...], preferred_element_type=F32)
        l2 = jnp.dot(lo, wr_ref[:, 0:LANES], preferred_element_type=F32)
        gate, bucket = _route(l1[:, 0:LANES] + l1[:, LANES:2 * LANES] + l2)
        groups = pl.ds(u * (RB // SUBLANES), RB // SUBLANES)
        _store_tiles(h2_ref.at[groups], _pack_bf16_pairs(h2))
        gate_ref[rows, :] = gate
        lane = lax.broadcasted_iota(I32, (RB, LANES), 1).astype(F32)
        oh_ref[rows, :] = jnp.where(lane == bucket, 1.0, 0.0).astype(BF16)

    mod2_s[slot, 0:1, :] = gain2
    mod2_s[slot, 1:2, :] = sh2
    for u in range(U):
        moe_prep(u)
    for u in range(U):
        mix(u)


def _mix(x2d, mod, mod_row, cache, rope_tabs, weights, *, S, L, emit_kv, blocks_per_step, cast=()):
    T = x2d.shape[0]
    TM = S * L
    P = cache[0].shape[1] if cache is not None else 0
    use_rope = rope_tabs is not None
    assert T % TM == 0 and L % ROW_BLOCK == 0
    assert not (use_rope or P) or S == 1
    Lk = P + L

    args = [x2d, mod]
    nrb = TM // ROW_BLOCK
    n_blocks = T // ROW_BLOCK
    step_rows = blocks_per_step * ROW_BLOCK
    steps_per_group = nrb // blocks_per_step
    n_mix_steps = n_blocks // blocks_per_step
    assert nrb % blocks_per_step == 0

    def mixed(s):
        return jnp.minimum(s, n_mix_steps - 1)

    def group(s):
        return mixed(s) // steps_per_group

    def prepared(s):
        return jnp.maximum(s - 1, 0)

    in_specs = [
        pl.BlockSpec((TM, D_MODEL), lambda s: (group(s), 0)),
        pl.BlockSpec((1, 6, D_MODEL), lambda s: (mod_row(group(s)), 0, 0)),
    ]
    if P:
        args += list(cache)
        in_specs += [pl.BlockSpec((1, P, KV_W), lambda s: (group(s), 0, 0))] * 2
    if use_rope:
        args += list(rope_tabs)
        in_specs += [_resident((L, HEAD_DIM))] * 3
    args += list(weights)
    in_specs += [_resident(w.shape) for w in weights]
    n_steps = T // TM
    def per_group(shape):
        assert shape[0] % n_steps == 0
        blk = (shape[0] // n_steps,) + shape[1:]
        return pl.BlockSpec(blk, lambda s, n=len(blk): (group(s),) + (0,) * (n - 1))

    cast_out_shapes = [ws[0].shape[:-1] + (sum(w.shape[-1] for w in ws),) for ws in cast]
    for ws in cast:
        args += list(ws)
        in_specs += [per_group(w.shape) for w in ws]

    out_shape = [jax.ShapeDtypeStruct((T, D_MODEL), F32), jax.ShapeDtypeStruct(_tiles_shape(T, PACKED_CHUNKS), U32),
                 jax.ShapeDtypeStruct((T, LANES), F32),
                 jax.ShapeDtypeStruct((T, LANES), BF16)]
    out_specs = [pl.BlockSpec((step_rows, D_MODEL), lambda s: (mixed(s), 0)),
                 _tiles_spec(step_rows, prepared, PACKED_CHUNKS),
                 pl.BlockSpec((step_rows, LANES), lambda s: (prepared(s), 0)),
                 pl.BlockSpec((step_rows, LANES), lambda s: (prepared(s), 0))]
    if emit_kv:
        out_shape += [jax.ShapeDtypeStruct((T * N_KV_HEADS, HEAD_DIM), F32)] * 2
        out_specs += [pl.BlockSpec((TM * N_KV_HEADS, HEAD_DIM), lambda s: (group(s), 0))] * 2
    out_shape += [jax.ShapeDtypeStruct(shp, BF16) for shp in cast_out_shapes]
    out_specs += [per_group(shp) for shp in cast_out_shapes]

    scratch = [
        pltpu.VMEM((N_HEADS, TM, HEAD_DIM), BF16),
        pltpu.VMEM((S, Lk, KV_W), BF16),
        pltpu.VMEM((S, Lk, KV_W), BF16),
        pltpu.VMEM((S, L + 2 * POOL_HALO, POOL_W), F32),
        pltpu.VMEM((TM, D_MODEL), BF16),
        pltpu.VMEM((blocks_per_step, ROW_BLOCK, ATTN_W), BF16),
        pltpu.VMEM((2, step_rows, D_MODEL), F32),
        pltpu.VMEM((2, 2, D_MODEL), F32),
    ]
    kern = functools.partial(_mix_kernel, S=S, L=L, P=P, use_rope=use_rope, emit_kv=emit_kv,
                             n_cast=tuple(len(ws) for ws in cast), n_blocks=n_blocks, U=blocks_per_step)
    return pl.pallas_call(
        kern,
        grid=(n_mix_steps + 1,),
        in_specs=in_specs,
        out_specs=out_specs,
        out_shape=out_shape,
        scratch_shapes=scratch,
        compiler_params=pltpu.CompilerParams(
            dimension_semantics=("arbitrary",), vmem_limit_bytes=V7X_VMEM_LIMIT_BYTES),
        name="mixer_rope" if use_rope else "mixer_ctx",
    )(*args)


def _plan_kernel(oh_ref, dest_ref, meta_ref, *, n_blocks):
    TB = TOKEN_BLOCK
    lane = lax.broadcasted_iota(I32, (SUBLANES, LANES), 1)

    def count(b, acc):
        oh = oh_ref[pl.ds(pl.multiple_of(b * TB, TB), TB), :].astype(F32)
        return acc + jnp.sum(oh, axis=0, keepdims=True)

    counts = lax.fori_loop(0, n_blocks, count, jnp.zeros((SUBLANES, LANES), F32))
    padded = jnp.ceil(counts * (1.0 / SORT_TILE)) * SORT_TILE
    ends = padded
    step = 1
    while step < LANES:
        ends = ends + jnp.where(lane >= step, pltpu.roll(ends, step, 1), 0.0)
        step *= 2
    starts = ends - padded

    tri = jnp.where(lax.broadcasted_iota(I32, (TB, TB), 1) < lax.broadcasted_iota(I32, (TB, TB), 0),
                    1.0, 0.0).astype(BF16)

    def place(b, seen):
        oh = oh_ref[pl.ds(pl.multiple_of(b * TB, TB), TB), :]
        ohf = oh.astype(F32)
        rank = jnp.dot(tri, oh, preferred_element_type=F32)
        base = (starts + seen)[0:1, :]
        d = jnp.sum(ohf * (rank + base), axis=1, keepdims=True)
        dest_ref[b] = _row(d).astype(I32)
        return seen + jnp.sum(ohf, axis=0, keepdims=True)

    lax.fori_loop(0, n_blocks, place, jnp.zeros((SUBLANES, LANES), F32))

    tile_row0 = lax.broadcasted_iota(I32, (LANES, LANES), 0).astype(F32) * SORT_TILE
    is_bucket = lax.broadcasted_iota(I32, (LANES, LANES), 1) < N_BUCKETS
    done = jnp.sum(jnp.where(is_bucket, jnp.where(ends[0:1, :] <= tile_row0, 1.0, 0.0), 0.0),
                   axis=1, keepdims=True)
    bkt = jnp.minimum(done, N_BUCKETS - 1.0)
    grp = (jnp.where(bkt >= PAIRS_PER_GROUP, 1.0, 0.0) + jnp.where(bkt >= 2 * PAIRS_PER_GROUP, 1.0, 0.0)
           + jnp.where(bkt >= 3 * PAIRS_PER_GROUP, 1.0, 0.0))
    pair = bkt - PAIRS_PER_GROUP * grp
    a = jnp.where(pair >= 3.0, 1.0, 0.0) + jnp.where(pair >= 5.0, 1.0, 0.0)
    b = pair - a * (7.0 - a) * 0.5 + a + 1.0
    e1 = EXP_PER_GROUP * grp + a
    e2 = EXP_PER_GROUP * grp + b
    meta = jnp.concatenate(
        [_row(e1), _row(e2), ends[0:1, :] * (1.0 / SORT_TILE), jnp.zeros((SUBLANES - 3, LANES), F32)], axis=0)
    meta_ref[...] = meta.astype(I32)


def _plan(onehot):
    T = onehot.shape[0]
    n_blocks = T // TOKEN_BLOCK
    dest, meta = pl.pallas_call(
        functools.partial(_plan_kernel, n_blocks=n_blocks),
        out_shape=[jax.ShapeDtypeStruct((n_blocks, 1, TOKEN_BLOCK), I32),
                   jax.ShapeDtypeStruct((SUBLANES, LANES), I32)],
        name="moe_plan",
    )(onehot)
    return dest.reshape(T), meta


def _sc_move_rows(src_v, table_hbm, out_hbm, lo, n_rows, idx_v, pieces_v, sem):
    chunks = pieces_v.shape[0] // SC_ROWS_PER_STEP
    lane = lax.iota(I32, SC_LANES)
    row_in_group = lane & (SUBLANES - 1)
    chunk_in_pair = lane >> 3
    rows_per_gather = SC_PIECES_PER_GATHER // chunks

    @pl.loop(0, n_rows // SC_ROWS_PER_STEP)
    def _(step):
        copies = []
        for g in range(SC_ROWS_PER_STEP // rows_per_gather):
            r0 = step * SC_ROWS_PER_STEP + g * rows_per_gather
            for v in range(SC_PIECES_PER_GATHER // SC_LANES):
                group, chunk0 = v // (chunks // 2), 2 * (v % (chunks // 2))
                tok = plsc.load_gather(src_v, [r0 + group * SUBLANES + row_in_group])
                piece = (tok >> 3) * (SUBLANES * chunks) + (chunk0 + chunk_in_pair) * SUBLANES + (tok & 7)
                idx_v[pl.ds(g * SC_PIECES_PER_GATHER + v * SC_LANES, SC_LANES)] = piece
            window = pl.ds(g * SC_PIECES_PER_GATHER, SC_PIECES_PER_GATHER)
            copies.append(pltpu.async_copy(table_hbm.at[idx_v.at[window]], pieces_v.at[window], sem))
        for cp in copies:
            cp.wait()
        first = pl.multiple_of((lo + step * SC_ROWS_PER_STEP) * chunks, SC_ROWS_PER_STEP * chunks)
        pltpu.sync_copy(pieces_v, out_hbm.at[pl.ds(first, SC_ROWS_PER_STEP * chunks)])


def _sc_scratch(chunks, dtype):
    return [pltpu.VMEM((SC_ROWS_PER_STEP * chunks,), I32), pltpu.VMEM((SC_ROWS_PER_STEP * chunks, LANES), dtype)]


def _sc_dispatch(h2_flat, gate_rows, dest, n_rows):
    T = dest.shape[0]
    per_worker = n_rows // SC_WORKERS
    rows_per_step = SC_ROWS_PER_STEP
    chunks = h2_flat.shape[0] // T
    assert n_rows % SC_WORKERS == 0 and per_worker % rows_per_step == 0 and T % SC_LANES == 0
    mesh = plsc.VectorSubcoreMesh(core_axis_name="c", subcore_axis_name="s")

    @functools.partial(
        pl.kernel, mesh=mesh,
        out_type=[jax.ShapeDtypeStruct((n_rows * chunks, LANES), h2_flat.dtype),
                  jax.ShapeDtypeStruct((n_rows, LANES), F32)],
        scratch_types=[pltpu.VMEM((T,), I32), pltpu.VMEM((per_worker,), I32)]
        + _sc_scratch(chunks, h2_flat.dtype)
        + [pltpu.VMEM((rows_per_step, LANES), F32), pltpu.SemaphoreType.DMA, pltpu.SemaphoreType.DMA],
        compiler_params=pltpu.CompilerParams(use_tc_tiling_on_sc=True, needs_layout_passes=False),
        name="sc_dispatch",
    )
    def dispatch(h2_hbm, gate_hbm, dest_hbm, out_h_hbm, out_g_hbm,
                 dest_v, src_v, idx_v, pieces_v, gates_v, sem_h, sem_g):
        worker = lax.axis_index("s") * SC_CORES + lax.axis_index("c")
        lo = worker * per_worker
        pltpu.sync_copy(dest_hbm, dest_v)

        @pl.loop(0, per_worker // SC_LANES)
        def _(j):
            j0 = pl.multiple_of(j * SC_LANES, SC_LANES)
            src_v[pl.ds(j0, SC_LANES)] = lax.rem(lo + j0 + lax.iota(I32, SC_LANES), T)

        @pl.loop(0, T // SC_LANES)
        def _(j):
            t0 = pl.multiple_of(j * SC_LANES, SC_LANES)
            d = dest_v[pl.ds(t0, SC_LANES)] - lo
            mine = (d >= 0) & (d < per_worker)
            plsc.store_scatter(src_v, [jnp.where(mine, d, 0)], t0 + lax.iota(I32, SC_LANES), mask=mine)

        @pl.loop(0, per_worker // rows_per_step)
        def _(j):
            off = pl.multiple_of(j * rows_per_step, rows_per_step)
            pltpu.async_copy(gate_hbm.at[src_v.at[pl.ds(off, rows_per_step)]], gates_v, sem_g).wait()
            pltpu.sync_copy(gates_v, out_g_hbm.at[pl.ds(lo + off, rows_per_step)])

        _sc_move_rows(src_v, h2_hbm, out_h_hbm, lo, per_worker, idx_v, pieces_v, sem_h)

    return dispatch(h2_flat, gate_rows, dest)


def _expert_kernel(e1s, e2s, n_used, x_ref, gv_ref, wgu_ref, wd_ref, o_ref):
    groups = SORT_TILE // SUBLANES

    def one_tile(k, carry):
        t = pl.program_id(0) * EXPERT_TILES_PER_STEP + k
        rows = pl.ds(pl.multiple_of(k * groups, groups), groups)

        @pl.when(t < n_used[0])
        def _():
            x = _unpack_bf16_pairs(_load_tiles(x_ref.at[rows]))
            gv = gv_ref[pl.ds(pl.multiple_of(k * SORT_TILE, SORT_TILE), SORT_TILE), :]
            lane = lax.broadcasted_iota(I32, gv.shape, 1)
            out = None
            for e in (e1s[t], e2s[t]):
                ge = jnp.sum(jnp.where(lane == EXPERT_LANE0 + e, gv, 0.0), axis=-1, keepdims=True)
                h = jnp.dot(x, wgu_ref[e], preferred_element_type=F32)
                hg = h[:, 0:D_EXPERT]
                hid = (hg * _sigmoid(hg) * h[:, D_EXPERT:2 * D_EXPERT] * ge).astype(BF16)
                y = jnp.dot(hid, wd_ref[e], preferred_element_type=F32)
                out = y if out is None else out + y
            _store_tiles(o_ref.at[rows], _pack_bf16_pairs(out))

        @pl.when(t >= n_used[0])
        def _():
            o_ref[rows] = jnp.zeros((groups,) + o_ref.shape[1:], U32)

        return carry

    lax.fori_loop(0, EXPERT_TILES_PER_STEP, one_tile, 0)


def _experts(sorted_h2, sorted_gates, meta, wgu, wd):
    n_tiles = sorted_h2.shape[0] * SUBLANES // SORT_TILE
    step_rows = SORT_TILE * EXPERT_TILES_PER_STEP
    assert n_tiles % EXPERT_TILES_PER_STEP == 0

    def last_used(i, e1, e2, nu):
        return jnp.minimum(i, (nu[0] - 1) // EXPERT_TILES_PER_STEP)

    return pl.pallas_call(
        _expert_kernel,
        grid_spec=pltpu.PrefetchScalarGridSpec(
            num_scalar_prefetch=3,
            grid=(n_tiles // EXPERT_TILES_PER_STEP,),
            in_specs=[
                _tiles_spec(step_rows, last_used, PACKED_CHUNKS),
                pl.BlockSpec((step_rows, LANES), lambda *a: (last_used(*a), 0)),
                _resident(wgu.shape), _resident(wd.shape),
            ],
            out_specs=_tiles_spec(step_rows, lambda i, *_: i, PACKED_CHUNKS),
        ),
        out_shape=jax.ShapeDtypeStruct(_tiles_shape(n_tiles * SORT_TILE, PACKED_CHUNKS), U32),
        compiler_params=pltpu.CompilerParams(
            dimension_semantics=("arbitrary",), vmem_limit_bytes=V7X_VMEM_LIMIT_BYTES),
        name="moe_experts",
    )(meta[0, :n_tiles], meta[1, :n_tiles], meta[2, LANES - 1:LANES], sorted_h2, sorted_gates, wgu, wd)


def _sc_row_gather(table_flat, idx, chunks):
    n = idx.shape[0]
    per_worker = n // SC_WORKERS
    assert n % SC_WORKERS == 0 and per_worker % SC_ROWS_PER_STEP == 0
    mesh = plsc.VectorSubcoreMesh(core_axis_name="c", subcore_axis_name="s")

    @functools.partial(
        pl.kernel, mesh=mesh,
        out_type=jax.ShapeDtypeStruct((n * chunks, LANES), table_flat.dtype),
        scratch_types=[pltpu.VMEM((per_worker,), I32)] + _sc_scratch(chunks, table_flat.dtype)
        + [pltpu.SemaphoreType.DMA],
        compiler_params=pltpu.CompilerParams(use_tc_tiling_on_sc=True, needs_layout_passes=False),
        name="sc_row_gather",
    )
    def gather(table_hbm, idx_hbm, out_hbm, src_v, idx_v, pieces_v, sem):
        worker = lax.axis_index("s") * SC_CORES + lax.axis_index("c")
        lo = worker * per_worker
        pltpu.sync_copy(idx_hbm.at[pl.ds(lo, per_worker)], src_v)
        _sc_move_rows(src_v, table_hbm, out_hbm, lo, per_worker, idx_v, pieces_v, sem)

    return gather(table_flat, idx)


def _final_kernel(x_ref, moe_ref, mod_ref, gf_ref, o_ref):
    y = x_ref[...] + mod_ref[0, 5:6, :] * _unpack_bf16_pairs(_load_tiles(moe_ref)).astype(F32)
    o_ref[...] = _rms(y) * gf_ref[...]


def _final(xmid, moe_rows, mod, mod_row, gf):
    T = xmid.shape[0]
    return pl.pallas_call(
        _final_kernel,
        grid=(T // FINAL_BLOCK,),
        in_specs=[
            pl.BlockSpec((FINAL_BLOCK, D_MODEL), lambda i: (i, 0)),
            _tiles_spec(FINAL_BLOCK, lambda i: i, PACKED_CHUNKS),
            pl.BlockSpec((1, 6, D_MODEL), lambda i: (mod_row(i), 0, 0)),
            pl.BlockSpec((1, D_MODEL), lambda i: (0, 0)),
        ],
        out_specs=pl.BlockSpec((FINAL_BLOCK, D_MODEL), lambda i: (i, 0)),
        out_shape=jax.ShapeDtypeStruct((T, D_MODEL), F32),
        compiler_params=pltpu.CompilerParams(
            dimension_semantics=("arbitrary",), vmem_limit_bytes=V7X_VMEM_LIMIT_BYTES),
        name="moe_final",
    )(xmid, moe_rows, mod, gf)


def _flat(tiles):
    return tiles.reshape(-1, LANES)


def _moe_dispatch(h2_tiles, gate_rows, onehot):
    T = gate_rows.shape[0]
    n_tiles = T // SORT_TILE + N_BUCKETS
    n_rows = n_tiles * SORT_TILE
    assert n_tiles <= LANES and T % TOKEN_BLOCK == 0
    dest, meta = _plan(onehot)
    sorted_h2, sorted_gates = _sc_dispatch(_flat(h2_tiles), gate_rows, dest, n_rows)
    return sorted_h2.reshape(_tiles_shape(n_rows, PACKED_CHUNKS)), sorted_gates, dest, meta


def _moe_unpermute(moe_sorted_tiles, dest):
    chunks = moe_sorted_tiles.shape[1]
    return _sc_row_gather(_flat(moe_sorted_tiles), dest, chunks).reshape(_tiles_shape(dest.shape[0], chunks))


def _rope_tables(n_tokens):
    t = np.arange(n_tokens)
    row = (t // GRID_W).astype(np.float32)
    col = (t % GRID_W).astype(np.float32)
    freq = np.float32(ROPE_THETA) ** (-np.arange(ROPE_NF, dtype=np.float32) / np.float32(ROPE_NF))
    ang = np.concatenate([row[:, None] * freq] * 2 + [col[:, None] * freq] * 2, axis=-1)
    first = (np.arange(HEAD_DIM) % (2 * ROPE_NF)) < ROPE_NF
    sin = np.sin(ang)
    zero = np.float32(0.0)
    return (jnp.asarray(np.cos(ang)), jnp.asarray(np.where(first, -sin, zero)),
            jnp.asarray(np.where(first, zero, sin)))


def kernel(x_prompt, x_sample, cache_k, cache_v, c, c_ctx, norm1_g, norm2_g, w_ada, b_ada, w_in, q_norm_g, k_norm_g, w_pool, pool_scale, w_branch_a, w_branch_b, w_out, w_router_group, w_router_expert, w_exp_gate, w_exp_up, w_exp_down, final_norm_g):
    assert norm1_g.shape[0] == 1, "single-layer trunk"
    B, L_ctx, _ = x_prompt.shape
    Bs, L_lat, _ = x_sample.shape
    P = cache_k.shape[2]
    assert 1 + Bs <= COND_ROWS

    cond = jnp.concatenate([c_ctx[None, :], c, jnp.zeros((COND_ROWS - 1 - Bs, D_MODEL), F32)], axis=0)
    wpool_bd = jax.scipy.linalg.block_diag(*[w_pool[0, g] for g in range(len(POOL_WINDOWS))])
    mod, w_in_b, wpool_b, wa_b, wb_b, wo_b = _ada(
        cond, w_ada[0], b_ada[0][None, :],
        cast=(w_in[0], wpool_bd, w_branch_a[0], w_branch_b[0], w_out[0]))
    mod = mod.reshape(COND_ROWS, 6, D_MODEL)

    wr = jnp.concatenate([w_router_group[0], w_router_expert[0],
                          jnp.zeros((D_MODEL, LANES - N_EXP_GROUPS - N_EXPERTS), F32)], axis=1)
    wr_hi = wr.astype(BF16)
    wr_lo = (wr - wr_hi.astype(F32)).astype(BF16)
    mix_w = (norm1_g[0][None, :], w_in_b, q_norm_g[0][None, :], k_norm_g[0][None, :],
             wpool_b, pool_scale[0][None, :], wa_b, wb_b, wo_b,
             norm2_g[0][None, :], jnp.concatenate([wr_hi, wr_lo], axis=1))
    gf = final_norm_g[None, :]

    xp2 = x_prompt.reshape(B * L_ctx, D_MODEL)
    xmid_p, h2_p, gate_p, oh_p, knew, vnew, wgu, wd = _mix(
        xp2, mod, lambda i: 0, None, None, mix_w, S=2, L=L_ctx, emit_kv=True, blocks_per_step=2,
        cast=((w_exp_gate[0], w_exp_up[0]), (w_exp_down[0],)))
    sh_p, sg_p, dest_p, meta_p = _moe_dispatch(h2_p, gate_p, oh_p)

    xs2 = x_sample.reshape(Bs * L_lat, D_MODEL)
    cache = (cache_k[:, 0].reshape(Bs, P, KV_W), cache_v[:, 0].reshape(Bs, P, KV_W))
    xmid_s, h2_s, gate_s, oh_s = _mix(xs2, mod, lambda i: 1 + i, cache, _rope_tables(L_lat), mix_w,
                                      S=1, L=L_lat, emit_kv=False, blocks_per_step=1)
    sh_s, sg_s, dest_s, meta_s = _moe_dispatch(h2_s, gate_s, oh_s)

    moe_p = _moe_unpermute(_experts(sh_p, sg_p, meta_p, wgu, wd), dest_p)
    moe_s = _moe_unpermute(_experts(sh_s, sg_s, meta_s, wgu, wd), dest_s)
    y_prompt = _final(xmid_p, moe_p, mod, lambda i: 0, gf)
    blocks_per_seq = L_lat // FINAL_BLOCK
    y_sample = _final(xmid_s, moe_s, mod, lambda i: 1 + i // blocks_per_seq, gf)

    return (y_prompt.reshape(B, L_ctx, D_MODEL), y_sample.reshape(Bs, L_lat, D_MODEL),
            knew.reshape(B, 1, L_ctx, N_KV_HEADS, HEAD_DIM), vnew.reshape(B, 1, L_ctx, N_KV_HEADS, HEAD_DIM))
```

```python
import functools

import numpy as np
import jax
import jax.numpy as jnp
from jax import lax
from jax.experimental import pallas as pl
from jax.experimental.pallas import tpu as pltpu
from jax.experimental.pallas import tpu_sc as plsc

F32 = jnp.float32
BF16 = jnp.bfloat16
I32 = jnp.int32
U32 = jnp.uint32

D_MODEL = 1024
HEAD_DIM = 128
N_HEADS = 8
N_KV_HEADS = 2
GROUP = N_HEADS // N_KV_HEADS
ATTN_W = N_HEADS * HEAD_DIM
KV_W = N_KV_HEADS * HEAD_DIM
POOL_WINDOWS = (2, 4, 8, 16)
POOL_GC = 128
POOL_W = POOL_GC * len(POOL_WINDOWS)
IN_W = ATTN_W + 2 * KV_W + POOL_W + 2 * D_MODEL
GATE_COL = ATTN_W + 2 * KV_W + POOL_W
GRID_W = 64
ROPE_THETA = 10000.0
ROPE_NF = HEAD_DIM // 4
N_EXP_GROUPS = 4
EXP_PER_GROUP = 4
N_EXPERTS = 16
D_EXPERT = 256
EPS = 1e-6
LOG2_E = 1.4426950408889634

LANES = 128
SUBLANES = 8
COND_ROWS = SUBLANES
POOL_HALO = 8
ROW_BLOCK = 256
ADA_COLS = 768
EXPERT_LANE0 = N_EXP_GROUPS
PAIRS_PER_GROUP = EXP_PER_GROUP * (EXP_PER_GROUP - 1) // 2
N_BUCKETS = N_EXP_GROUPS * PAIRS_PER_GROUP
SORT_TILE = 256
EXPERT_TILES_PER_STEP = 8
TOKEN_BLOCK = 1024
FINAL_BLOCK = 1024
ROW_CHUNKS = D_MODEL // LANES
SC_CORES = 2
SC_SUBCORES = 16
SC_WORKERS = SC_CORES * SC_SUBCORES
SC_LANES = 16
SC_PIECES_PER_GATHER = 128
SC_ROWS_PER_STEP = 64
PACKED_CHUNKS = ROW_CHUNKS // 2
V7X_VMEM_LIMIT_BYTES = 56 * 1024 * 1024


def _sigmoid(x):
    return 1.0 / (1.0 + jnp.exp(-x))


def _rms(x):
    return x * lax.rsqrt(jnp.mean(x * x, axis=-1, keepdims=True) + EPS)


def _resident(shape):
    zeros = (0,) * len(shape)
    return pl.BlockSpec(shape, lambda i, *_: zeros, pipeline_mode=pl.Buffered(1))


def _tiles_shape(n, chunks=ROW_CHUNKS):
    return (n // SUBLANES, chunks, SUBLANES, LANES)


def _tiles_spec(n, block_index, chunks=ROW_CHUNKS):
    return pl.BlockSpec(_tiles_shape(n, chunks), lambda *a: (block_index(*a), 0, 0, 0))


def _store_tiles(ref, x):
    for c in range(ref.shape[1]):
        ref[:, c, :, :] = x[:, c * LANES:(c + 1) * LANES].reshape(x.shape[0] // SUBLANES, SUBLANES, LANES)


def _load_tiles(ref):
    n = ref.shape[0] * SUBLANES
    return jnp.concatenate([ref[:, c, :, :].reshape(n, LANES) for c in range(ref.shape[1])], axis=1)


def _pack_bf16_pairs(x):
    bits = pltpu.bitcast(x.astype(BF16).astype(F32), U32)
    w = x.shape[1] // 2
    return bits[:, :w] | (bits[:, w:] >> 16)


def _unpack_bf16_pairs(words):
    hi = pltpu.bitcast(words & jnp.uint32(0xFFFF0000), F32).astype(BF16)
    lo = pltpu.bitcast(words << 16, F32).astype(BF16)
    return jnp.concatenate([hi, lo], axis=1)


def _row(x):
    return jnp.transpose(jnp.broadcast_to(x, (x.shape[0], LANES)))[0:1, :]


def _ada_kernel(c_ref, w_ref, b_ref, *refs):
    n_cast = (len(refs) - 1) // 2
    c = c_ref[...]
    s = (c * _sigmoid(c)).astype(BF16)
    refs[n_cast][...] = jnp.dot(s, w_ref[...].astype(BF16), preferred_element_type=F32) + b_ref[...]
    for src, dst in zip(refs[:n_cast], refs[n_cast + 1:]):
        dst[...] = src[...].astype(BF16)


def _ada(cond, w_ada, b_ada, cast=()):
    n = w_ada.shape[1]
    n_steps = n // ADA_COLS
    cast_specs = []
    for w in cast:
        assert w.ndim == 2 and w.shape[0] % (n_steps * 2 * SUBLANES) == 0
        cast_specs.append(pl.BlockSpec((w.shape[0] // n_steps, w.shape[1]), lambda j: (j, 0)))
    return pl.pallas_call(
        _ada_kernel,
        grid=(n_steps,),
        in_specs=[
            pl.BlockSpec((COND_ROWS, D_MODEL), lambda j: (0, 0)),
            pl.BlockSpec((D_MODEL, ADA_COLS), lambda j: (0, j)),
            pl.BlockSpec((1, ADA_COLS), lambda j: (0, j)),
        ] + cast_specs,
        out_specs=[pl.BlockSpec((COND_ROWS, ADA_COLS), lambda j: (0, j))] + cast_specs,
        out_shape=[jax.ShapeDtypeStruct((COND_ROWS, n), F32)] + [jax.ShapeDtypeStruct(w.shape, BF16) for w in cast],
        name="ada_mod",
    )(cond, w_ada, b_ada, *cast)


def _route(logits):
    lane = lax.broadcasted_iota(I32, logits.shape, 1).astype(F32)
    neg = jnp.float32(-1e30)
    far = jnp.float32(LANES)
    is_g = lane < N_EXP_GROUPS
    gl = jnp.where(is_g, logits, neg)
    gmax = jnp.max(gl, axis=-1, keepdims=True)
    gsel = jnp.min(jnp.where(gl == gmax, lane, far), axis=-1, keepdims=True)
    psel = 1.0 / jnp.sum(jnp.where(is_g, jnp.exp(gl - gmax), 0.0), axis=-1, keepdims=True)
    e_lo = EXPERT_LANE0 + EXP_PER_GROUP * gsel
    el = jnp.where(lane >= e_lo, jnp.where(lane < e_lo + EXP_PER_GROUP, logits, neg), neg)
    v1 = jnp.max(el, axis=-1, keepdims=True)
    i1 = jnp.min(jnp.where(el == v1, lane, far), axis=-1, keepdims=True)
    el2 = jnp.where(lane == i1, neg, el)
    v2 = jnp.max(el2, axis=-1, keepdims=True)
    i2 = jnp.min(jnp.where(el2 == v2, jnp.where(lane == i1, far, lane), far), axis=-1, keepdims=True)
    e2 = jnp.exp(v2 - v1)
    w1 = psel / (1.0 + e2)
    w2 = psel * e2 / (1.0 + e2)
    gate = jnp.where(lane == i1, w1, jnp.where(lane == i2, w2, 0.0))
    a = jnp.minimum(i1, i2) - e_lo
    b = jnp.maximum(i1, i2) - e_lo
    pair = a * (7.0 - a) * 0.5 + (b - a - 1.0)
    return gate, gsel * PAIRS_PER_GROUP + pair


def _mix_kernel(*refs, S, L, P, use_rope, emit_kv, n_cast, n_blocks, U):
    it = iter(refs)
    x_ref = next(it)
    mod_ref = next(it)
    if P:
        ck_ref = next(it)
        cv_ref = next(it)
    if use_rope:
        cos_ref = next(it)
        sneg_ref = next(it)
        spos_ref = next(it)
    (g1_ref, win_ref, qg_ref, kg_ref, wpool_ref, pscale_ref, wa_ref, wb_ref, wo_ref,
     g2_ref, wr_ref) = (next(it) for _ in range(11))
    cast_in = [[next(it) for _ in range(n)] for n in n_cast]
    xmid_ref = next(it)
    h2_ref = next(it)
    gate_ref = next(it)
    oh_ref = next(it)
    if emit_kv:
        knew_ref = next(it)
        vnew_ref = next(it)
    cast_out = [next(it) for _ in n_cast]
    q_s, k_s, v_s, xp_s, h_s, attn_s, xm_s, mod2_s = (next(it) for _ in range(8))

    TM = S * L
    RB = ROW_BLOCK
    nrb = TM // RB
    n_steps = n_blocks // U
    score_gain = HEAD_DIM ** -0.5 * LOG2_E
    step = pl.program_id(0)
    block0 = U * jnp.minimum(step, n_steps - 1)
    slot = step % 2

    sh1 = mod_ref[0, 0:1, :]
    gain1 = g1_ref[...] * (1.0 + mod_ref[0, 1:2, :])
    gt1 = mod_ref[0, 2:3, :]
    sh2 = mod_ref[0, 3:4, :]
    gain2 = g2_ref[...] * (1.0 + mod_ref[0, 4:5, :])
    qg = qg_ref[...] * score_gain
    kg = kg_ref[...]

    def project(r, carry):
        r0 = pl.multiple_of(r * RB, RB)
        s = r0 // L
        o = pl.multiple_of(r0 % L, RB)
        hb = (_rms(x_ref[pl.ds(r0, RB), :]) * gain1 + sh1).astype(BF16)
        h_s[pl.ds(r0, RB), :] = hb
        p1 = jnp.dot(hb, win_ref[:, 0:GATE_COL], preferred_element_type=F32)
        if use_rope:
            cs = cos_ref[pl.ds(o, RB), :]
            sn = sneg_ref[pl.ds(o, RB), :]
            sp = spos_ref[pl.ds(o, RB), :]

        def rope(t):
            return (t * cs + pltpu.roll(t, HEAD_DIM - ROPE_NF, 1) * sn + pltpu.roll(t, ROPE_NF, 1) * sp)

        for hd in range(N_HEADS):
            qh = _rms(p1[:, hd * HEAD_DIM:(hd + 1) * HEAD_DIM]) * qg
            if use_rope:
                qh = rope(qh)
            q_s[hd, pl.ds(r0, RB), :] = qh.astype(BF16)
        for kh in range(N_KV_HEADS):
            c0 = ATTN_W + kh * HEAD_DIM
            kk = _rms(p1[:, c0:c0 + HEAD_DIM]) * kg
            if emit_kv:
                knew_ref[pl.ds(N_KV_HEADS * r0 + kh, RB, stride=N_KV_HEADS), :] = kk
            if use_rope:
                kk = rope(kk)
            k_s[s, pl.ds(P + o, RB), kh * HEAD_DIM:(kh + 1) * HEAD_DIM] = kk.astype(BF16)
        vv = p1[:, ATTN_W + KV_W:ATTN_W + 2 * KV_W]
        if emit_kv:
            for kh in range(N_KV_HEADS):
                vnew_ref[pl.ds(N_KV_HEADS * r0 + kh, RB, stride=N_KV_HEADS), :] = (
                    vv[:, kh * HEAD_DIM:(kh + 1) * HEAD_DIM])
        v_s[s, pl.ds(P + o, RB), :] = vv.astype(BF16)
        xp_s[s, pl.ds(POOL_HALO + o, RB), :] = p1[:, ATTN_W + 2 * KV_W:GATE_COL]
        return carry

    @pl.when(step == 0)
    def _():
        xm_s[1] = jnp.zeros((U * RB, D_MODEL), F32)
        mod2_s[1] = jnp.zeros((2, D_MODEL), F32)

    @pl.when((step < n_steps) & (step % (nrb // U) == 0))
    def _():
        if P:
            k_s[0, 0:P, :] = ck_ref[0].astype(BF16)
            v_s[0, 0:P, :] = cv_ref[0].astype(BF16)
        xp_s[:, 0:POOL_HALO, :] = jnp.zeros((S, POOL_HALO, POOL_W), F32)
        xp_s[:, L + POOL_HALO:L + 2 * POOL_HALO, :] = jnp.zeros((S, POOL_HALO, POOL_W), F32)
        lax.fori_loop(0, TM // RB, project, 0)
        for srcs, dst in zip(cast_in, cast_out):
            col = 0
            for src in srcs:
                dst[..., col:col + src.shape[-1]] = src[...].astype(BF16)
                col += src.shape[-1]

    def mix(u):
        r0 = pl.multiple_of(((block0 + u) % nrb) * RB, RB)
        s = r0 // L
        o = pl.multiple_of(r0 % L, RB)
        attn_u = attn_s.at[u]
        rows = slice(u * RB, (u + 1) * RB)

        for hd in range(N_HEADS):
            kh = hd // GROUP
            k = k_s[s, :, kh * HEAD_DIM:(kh + 1) * HEAD_DIM]
            v = v_s[s, :, kh * HEAD_DIM:(kh + 1) * HEAD_DIM]
            qh = q_s[hd, pl.ds(r0, RB), :]
            sc = lax.dot_general(qh, k, (((1,), (1,)), ((), ())), preferred_element_type=F32)
            e = jnp.exp2(sc - jnp.max(sc, axis=-1, keepdims=True))
            den = jnp.sum(e, axis=-1, keepdims=True)
            oh = jnp.dot(e.astype(BF16), v, preferred_element_type=F32) / den
            attn_u[:, hd * HEAD_DIM:(hd + 1) * HEAD_DIM] = oh.astype(BF16)
        a = jnp.dot(attn_u[...], wa_ref[...], preferred_element_type=F32)

        t = o + lax.broadcasted_iota(I32, (RB, 1), 0)
        RW = RB + 2 * POOL_HALO
        parts = []
        for gi, w in enumerate(POOL_WINDOWS):
            cols = slice(gi * POOL_GC, (gi + 1) * POOL_GC)
            xw = xp_s[s, pl.ds(o, RW), cols]
            run = xw
            span = 1
            while span < w:
                run = run + pltpu.roll(run, span, 0)
                span *= 2
            if w // 2 > 1:
                run = pltpu.roll(run, RW - (w // 2 - 1), 0)
            tot = run[POOL_HALO:POOL_HALO + RB]
            cnt = (jnp.minimum(t + w // 2, L) - jnp.maximum(t - w // 2, 0)).astype(F32)
            parts.append(tot / cnt - xw[POOL_HALO:POOL_HALO + RB])
        dpool = jnp.concatenate(parts, axis=1).astype(BF16)
        pooled = jnp.dot(dpool, wpool_ref[...], preferred_element_type=F32) * pscale_ref[...]
        b = jnp.dot(pooled.astype(BF16), wb_ref[...], preferred_element_type=F32)

        gates = jnp.dot(h_s[pl.ds(r0, RB), :], win_ref[:, GATE_COL:IN_W], preferred_element_type=F32)
        merged = _sigmoid(gates[:, 0:D_MODEL]) * a + _sigmoid(gates[:, D_MODEL:2 * D_MODEL]) * b
        upd = jnp.dot(merged.astype(BF16), wo_ref[...], preferred_element_type=F32)
        xm = x_ref[pl.ds(r0, RB), :] + gt1 * upd
        xmid_ref[rows, :] = xm
        xm_s[slot, rows, :] = xm

    def moe_prep(u):
        rows = slice(u * RB, (u + 1) * RB)
        h2 = _rms(xm_s[1 - slot, rows, :]) * mod2_s[1 - slot, 0:1, :] + mod2_s[1 - slot, 1:2, :]
        hi = h2.astype(BF16)
        lo = (h2 - hi.astype(F32)).astype(BF16)
        l1 = jnp.dot(hi, wr_ref[...], preferred_element_type=F32)
        l2 = jnp.dot(lo, wr_ref[:, 0:LANES], preferred_element_type=F32)
        gate, bucket = _route(l1[:, 0:LANES] + l1[:, LANES:2 * LANES] + l2)
        groups = pl.ds(u * (RB // SUBLANES), RB // SUBLANES)
        _store_tiles(h2_ref.at[groups], _pack_bf16_pairs(h2))
        gate_ref[rows, :] = gate
        lane = lax.broadcasted_iota(I32, (RB, LANES), 1).astype(F32)
        oh_ref[rows, :] = jnp.where(lane == bucket, 1.0, 0.0).astype(BF16)

    mod2_s[slot, 0:1, :] = gain2
    mod2_s[slot, 1:2, :] = sh2
    for u in range(U):
        moe_prep(u)
    for u in range(U):
        mix(u)


def _mix(x2d, mod, mod_row, cache, rope_tabs, weights, *, S, L, emit_kv, blocks_per_step, cast=()):
    T = x2d.shape[0]
    TM = S * L
    P = cache[0].shape[1] if cache is not None else 0
    use_rope = rope_tabs is not None
    assert T % TM == 0 and L % ROW_BLOCK == 0
    assert not (use_rope or P) or S == 1
    Lk = P + L

    args = [x2d, mod]
    nrb = TM // ROW_BLOCK
    n_blocks = T // ROW_BLOCK
    step_rows = blocks_per_step * ROW_BLOCK
    steps_per_group = nrb // blocks_per_step
    n_mix_steps = n_blocks // blocks_per_step
    assert nrb % blocks_per_step == 0

    def mixed(s):
        return jnp.minimum(s, n_mix_steps - 1)

    def group(s):
        return mixed(s) // steps_per_group

    def prepared(s):
        return jnp.maximum(s - 1, 0)

    in_specs = [
        pl.BlockSpec((TM, D_MODEL), lambda s: (group(s), 0)),
        pl.BlockSpec((1, 6, D_MODEL), lambda s: (mod_row(group(s)), 0, 0)),
    ]
    if P:
        args += list(cache)
        in_specs += [pl.BlockSpec((1, P, KV_W), lambda s: (group(s), 0, 0))] * 2
    if use_rope:
        args += list(rope_tabs)
        in_specs += [_resident((L, HEAD_DIM))] * 3
    args += list(weights)
    in_specs += [_resident(w.shape) for w in weights]
    n_steps = T // TM
    def per_group(shape):
        assert shape[0] % n_steps == 0
        blk = (shape[0] // n_steps,) + shape[1:]
        return pl.BlockSpec(blk, lambda s, n=len(blk): (group(s),) + (0,) * (n - 1))

    cast_out_shapes = [ws[0].shape[:-1] + (sum(w.shape[-1] for w in ws),) for ws in cast]
    for ws in cast:
        args += list(ws)
        in_specs += [per_group(w.shape) for w in ws]

    out_shape = [jax.ShapeDtypeStruct((T, D_MODEL), F32), jax.ShapeDtypeStruct(_tiles_shape(T, PACKED_CHUNKS), U32),
                 jax.ShapeDtypeStruct((T, LANES), F32),
                 jax.ShapeDtypeStruct((T, LANES), BF16)]
    out_specs = [pl.BlockSpec((step_rows, D_MODEL), lambda s: (mixed(s), 0)),
                 _tiles_spec(step_rows, prepared, PACKED_CHUNKS),
                 pl.BlockSpec((step_rows, LANES), lambda s: (prepared(s), 0)),
                 pl.BlockSpec((step_rows, LANES), lambda s: (prepared(s), 0))]
    if emit_kv:
        out_shape += [jax.ShapeDtypeStruct((T * N_KV_HEADS, HEAD_DIM), F32)] * 2
        out_specs += [pl.BlockSpec((TM * N_KV_HEADS, HEAD_DIM), lambda s: (group(s), 0))] * 2
    out_shape += [jax.ShapeDtypeStruct(shp, BF16) for shp in cast_out_shapes]
    out_specs += [per_group(shp) for shp in cast_out_shapes]

    scratch = [
        pltpu.VMEM((N_HEADS, TM, HEAD_DIM), BF16),
        pltpu.VMEM((S, Lk, KV_W), BF16),
        pltpu.VMEM((S, Lk, KV_W), BF16),
        pltpu.VMEM((S, L + 2 * POOL_HALO, POOL_W), F32),
        pltpu.VMEM((TM, D_MODEL), BF16),
        pltpu.VMEM((blocks_per_step, ROW_BLOCK, ATTN_W), BF16),
        pltpu.VMEM((2, step_rows, D_MODEL), F32),
        pltpu.VMEM((2, 2, D_MODEL), F32),
    ]
    kern = functools.partial(_mix_kernel, S=S, L=L, P=P, use_rope=use_rope, emit_kv=emit_kv,
                             n_cast=tuple(len(ws) for ws in cast), n_blocks=n_blocks, U=blocks_per_step)
    return pl.pallas_call(
        kern,
        grid=(n_mix_steps + 1,),
        in_specs=in_specs,
        out_specs=out_specs,
        out_shape=out_shape,
        scratch_shapes=scratch,
        compiler_params=pltpu.CompilerParams(
            dimension_semantics=("arbitrary",), vmem_limit_bytes=V7X_VMEM_LIMIT_BYTES),
        name="mixer_rope" if use_rope else "mixer_ctx",
    )(*args)


def _plan_kernel(oh_ref, dest_ref, meta_ref, *, n_blocks):
    TB = TOKEN_BLOCK
    lane = lax.broadcasted_iota(I32, (SUBLANES, LANES), 1)

    def count(b, acc):
        oh = oh_ref[pl.ds(pl.multiple_of(b * TB, TB), TB), :].astype(F32)
        return acc + jnp.sum(oh, axis=0, keepdims=True)

    counts = lax.fori_loop(0, n_blocks, count, jnp.zeros((SUBLANES, LANES), F32))
    padded = jnp.ceil(counts * (1.0 / SORT_TILE)) * SORT_TILE
    ends = padded
    step = 1
    while step < LANES:
        ends = ends + jnp.where(lane >= step, pltpu.roll(ends, step, 1), 0.0)
        step *= 2
    starts = ends - padded

    tri = jnp.where(lax.broadcasted_iota(I32, (TB, TB), 1) < lax.broadcasted_iota(I32, (TB, TB), 0),
                    1.0, 0.0).astype(BF16)

    def place(b, seen):
        oh = oh_ref[pl.ds(pl.multiple_of(b * TB, TB), TB), :]
        ohf = oh.astype(F32)
        rank = jnp.dot(tri, oh, preferred_element_type=F32)
        base = (starts + seen)[0:1, :]
        d = jnp.sum(ohf * (rank + base), axis=1, keepdims=True)
        dest_ref[b] = _row(d).astype(I32)
        return seen + jnp.sum(ohf, axis=0, keepdims=True)

    lax.fori_loop(0, n_blocks, place, jnp.zeros((SUBLANES, LANES), F32))

    tile_row0 = lax.broadcasted_iota(I32, (LANES, LANES), 0).astype(F32) * SORT_TILE
    is_bucket = lax.broadcasted_iota(I32, (LANES, LANES), 1) < N_BUCKETS
    done = jnp.sum(jnp.where(is_bucket, jnp.where(ends[0:1, :] <= tile_row0, 1.0, 0.0), 0.0),
                   axis=1, keepdims=True)
    bkt = jnp.minimum(done, N_BUCKETS - 1.0)
    grp = (jnp.where(bkt >= PAIRS_PER_GROUP, 1.0, 0.0) + jnp.where(bkt >= 2 * PAIRS_PER_GROUP, 1.0, 0.0)
           + jnp.where(bkt >= 3 * PAIRS_PER_GROUP, 1.0, 0.0))
    pair = bkt - PAIRS_PER_GROUP * grp
    a = jnp.where(pair >= 3.0, 1.0, 0.0) + jnp.where(pair >= 5.0, 1.0, 0.0)
    b = pair - a * (7.0 - a) * 0.5 + a + 1.0
    e1 = EXP_PER_GROUP * grp + a
    e2 = EXP_PER_GROUP * grp + b
    meta = jnp.concatenate(
        [_row(e1), _row(e2), ends[0:1, :] * (1.0 / SORT_TILE), jnp.zeros((SUBLANES - 3, LANES), F32)], axis=0)
    meta_ref[...] = meta.astype(I32)


def _plan(onehot):
    T = onehot.shape[0]
    n_blocks = T // TOKEN_BLOCK
    dest, meta = pl.pallas_call(
        functools.partial(_plan_kernel, n_blocks=n_blocks),
        out_shape=[jax.ShapeDtypeStruct((n_blocks, 1, TOKEN_BLOCK), I32),
                   jax.ShapeDtypeStruct((SUBLANES, LANES), I32)],
        name="moe_plan",
    )(onehot)
    return dest.reshape(T), meta


def _sc_move_rows(src_v, table_hbm, out_hbm, lo, n_rows, idx_v, pieces_v, sem):
    chunks = pieces_v.shape[0] // SC_ROWS_PER_STEP
    lane = lax.iota(I32, SC_LANES)
    row_in_group = lane & (SUBLANES - 1)
    chunk_in_pair = lane >> 3
    rows_per_gather = SC_PIECES_PER_GATHER // chunks

    @pl.loop(0, n_rows // SC_ROWS_PER_STEP)
    def _(step):
        copies = []
        for g in range(SC_ROWS_PER_STEP // rows_per_gather):
            r0 = step * SC_ROWS_PER_STEP + g * rows_per_gather
            for v in range(SC_PIECES_PER_GATHER // SC_LANES):
                group, chunk0 = v // (chunks // 2), 2 * (v % (chunks // 2))
                tok = plsc.load_gather(src_v, [r0 + group * SUBLANES + row_in_group])
                piece = (tok >> 3) * (SUBLANES * chunks) + (chunk0 + chunk_in_pair) * SUBLANES + (tok & 7)
                idx_v[pl.ds(g * SC_PIECES_PER_GATHER + v * SC_LANES, SC_LANES)] = piece
            window = pl.ds(g * SC_PIECES_PER_GATHER, SC_PIECES_PER_GATHER)
            copies.append(pltpu.async_copy(table_hbm.at[idx_v.at[window]], pieces_v.at[window], sem))
        for cp in copies:
            cp.wait()
        first = pl.multiple_of((lo + step * SC_ROWS_PER_STEP) * chunks, SC_ROWS_PER_STEP * chunks)
        pltpu.sync_copy(pieces_v, out_hbm.at[pl.ds(first, SC_ROWS_PER_STEP * chunks)])


def _sc_scratch(chunks, dtype):
    return [pltpu.VMEM((SC_ROWS_PER_STEP * chunks,), I32), pltpu.VMEM((SC_ROWS_PER_STEP * chunks, LANES), dtype)]


def _sc_dispatch(h2_flat, gate_rows, dest, n_rows):
    T = dest.shape[0]
    per_worker = n_rows // SC_WORKERS
    rows_per_step = SC_ROWS_PER_STEP
    chunks = h2_flat.shape[0] // T
    assert n_rows % SC_WORKERS == 0 and per_worker % rows_per_step == 0 and T % SC_LANES == 0
    mesh = plsc.VectorSubcoreMesh(core_axis_name="c", subcore_axis_name="s")

    @functools.partial(
        pl.kernel, mesh=mesh,
        out_type=[jax.ShapeDtypeStruct((n_rows * chunks, LANES), h2_flat.dtype),
                  jax.ShapeDtypeStruct((n_rows, LANES), F32)],
        scratch_types=[pltpu.VMEM((T,), I32), pltpu.VMEM((per_worker,), I32)]
        + _sc_scratch(chunks, h2_flat.dtype)
        + [pltpu.VMEM((rows_per_step, LANES), F32), pltpu.SemaphoreType.DMA, pltpu.SemaphoreType.DMA],
        compiler_params=pltpu.CompilerParams(use_tc_tiling_on_sc=True, needs_layout_passes=False),
        name="sc_dispatch",
    )
    def dispatch(h2_hbm, gate_hbm, dest_hbm, out_h_hbm, out_g_hbm,
                 dest_v, src_v, idx_v, pieces_v, gates_v, sem_h, sem_g):
        worker = lax.axis_index("s") * SC_CORES + lax.axis_index("c")
        lo = worker * per_worker
        pltpu.sync_copy(dest_hbm, dest_v)

        @pl.loop(0, per_worker // SC_LANES)
        def _(j):
            j0 = pl.multiple_of(j * SC_LANES, SC_LANES)
            src_v[pl.ds(j0, SC_LANES)] = lax.rem(lo + j0 + lax.iota(I32, SC_LANES), T)

        @pl.loop(0, T // SC_LANES)
        def _(j):
            t0 = pl.multiple_of(j * SC_LANES, SC_LANES)
            d = dest_v[pl.ds(t0, SC_LANES)] - lo
            mine = (d >= 0) & (d < per_worker)
            plsc.store_scatter(src_v, [jnp.where(mine, d, 0)], t0 + lax.iota(I32, SC_LANES), mask=mine)

        @pl.loop(0, per_worker // rows_per_step)
        def _(j):
            off = pl.multiple_of(j * rows_per_step, rows_per_step)
            pltpu.async_copy(gate_hbm.at[src_v.at[pl.ds(off, rows_per_step)]], gates_v, sem_g).wait()
            pltpu.sync_copy(gates_v, out_g_hbm.at[pl.ds(lo + off, rows_per_step)])

        _sc_move_rows(src_v, h2_hbm, out_h_hbm, lo, per_worker, idx_v, pieces_v, sem_h)

    return dispatch(h2_flat, gate_rows, dest)


def _expert_kernel(e1s, e2s, n_used, x_ref, gv_ref, wgu_hbm, wd_hbm, o_ref, wgu_s, wd_s, loaded_s, sems):
    groups = SORT_TILE // SUBLANES
    step = pl.program_id(0)

    def weight_copies(e):
        return (pltpu.make_async_copy(wgu_hbm.at[e], wgu_s.at[e], sems.at[e]),
                pltpu.make_async_copy(wd_hbm.at[e], wd_s.at[e], sems.at[e]))

    def ensure_loaded(e):
        @pl.when(loaded_s[e] == 0)
        def _():
            for cp in weight_copies(e):
                cp.wait()
            loaded_s[e] = 1

    @pl.when(step == 0)
    def _():
        for e in range(N_EXPERTS):
            loaded_s[e] = 0
            for cp in weight_copies(e):
                cp.start()

    def one_tile(k, carry):
        t = step * EXPERT_TILES_PER_STEP + k
        rows = pl.ds(pl.multiple_of(k * groups, groups), groups)

        @pl.when(t < n_used[0])
        def _():
            x = _unpack_bf16_pairs(_load_tiles(x_ref.at[rows]))
            gv = gv_ref[pl.ds(pl.multiple_of(k * SORT_TILE, SORT_TILE), SORT_TILE), :]
            lane = lax.broadcasted_iota(I32, gv.shape, 1)
            out = None
            for e in (e1s[t], e2s[t]):
                ensure_loaded(e)
                ge = jnp.sum(jnp.where(lane == EXPERT_LANE0 + e, gv, 0.0), axis=-1, keepdims=True)
                h = jnp.dot(x, wgu_s[e], preferred_element_type=F32)
                hg = h[:, 0:D_EXPERT]
                hid = (hg * _sigmoid(hg) * h[:, D_EXPERT:2 * D_EXPERT] * ge).astype(BF16)
                y = jnp.dot(hid, wd_s[e], preferred_element_type=F32)
                out = y if out is None else out + y
            _store_tiles(o_ref.at[rows], _pack_bf16_pairs(out))

        @pl.when(t >= n_used[0])
        def _():
            o_ref[rows] = jnp.zeros((groups,) + o_ref.shape[1:], U32)

        return carry

    lax.fori_loop(0, EXPERT_TILES_PER_STEP, one_tile, 0)

    @pl.when(step == pl.num_programs(0) - 1)
    def _():
        for e in range(N_EXPERTS):
            ensure_loaded(e)


def _experts(sorted_h2, sorted_gates, meta, wgu, wd):
    n_tiles = sorted_h2.shape[0] * SUBLANES // SORT_TILE
    step_rows = SORT_TILE * EXPERT_TILES_PER_STEP
    assert n_tiles % EXPERT_TILES_PER_STEP == 0

    def last_used(i, e1, e2, nu):
        return jnp.minimum(i, (nu[0] - 1) // EXPERT_TILES_PER_STEP)

    return pl.pallas_call(
        _expert_kernel,
        grid_spec=pltpu.PrefetchScalarGridSpec(
            num_scalar_prefetch=3,
            grid=(n_tiles // EXPERT_TILES_PER_STEP,),
            in_specs=[
                _tiles_spec(step_rows, last_used, PACKED_CHUNKS),
                pl.BlockSpec((step_rows, LANES), lambda *a: (last_used(*a), 0)),
                pl.BlockSpec(memory_space=pl.ANY), pl.BlockSpec(memory_space=pl.ANY),
            ],
            out_specs=_tiles_spec(step_rows, lambda i, *_: i, PACKED_CHUNKS),
            scratch_shapes=[pltpu.VMEM(wgu.shape, BF16), pltpu.VMEM(wd.shape, BF16),
                            pltpu.SMEM((N_EXPERTS,), I32), pltpu.SemaphoreType.DMA((N_EXPERTS,))],
        ),
        out_shape=jax.ShapeDtypeStruct(_tiles_shape(n_tiles * SORT_TILE, PACKED_CHUNKS), U32),
        compiler_params=pltpu.CompilerParams(
            dimension_semantics=("arbitrary",), vmem_limit_bytes=V7X_VMEM_LIMIT_BYTES),
        name="moe_experts",
    )(meta[0, :n_tiles], meta[1, :n_tiles], meta[2, LANES - 1:LANES], sorted_h2, sorted_gates, wgu, wd)


def _sc_row_gather(table_flat, idx, chunks):
    n = idx.shape[0]
    per_worker = n // SC_WORKERS
    assert n % SC_WORKERS == 0 and per_worker % SC_ROWS_PER_STEP == 0
    mesh = plsc.VectorSubcoreMesh(core_axis_name="c", subcore_axis_name="s")

    @functools.partial(
        pl.kernel, mesh=mesh,
        out_type=jax.ShapeDtypeStruct((n * chunks, LANES), table_flat.dtype),
        scratch_types=[pltpu.VMEM((per_worker,), I32)] + _sc_scratch(chunks, table_flat.dtype)
        + [pltpu.SemaphoreType.DMA],
        compiler_params=pltpu.CompilerParams(use_tc_tiling_on_sc=True, needs_layout_passes=False),
        name="sc_row_gather",
    )
    def gather(table_hbm, idx_hbm, out_hbm, src_v, idx_v, pieces_v, sem):
        worker = lax.axis_index("s") * SC_CORES + lax.axis_index("c")
        lo = worker * per_worker
        pltpu.sync_copy(idx_hbm.at[pl.ds(lo, per_worker)], src_v)
        _sc_move_rows(src_v, table_hbm, out_hbm, lo, per_worker, idx_v, pieces_v, sem)

    return gather(table_flat, idx)


def _final_kernel(x_ref, moe_ref, mod_ref, gf_ref, o_ref):
    y = x_ref[...] + mod_ref[0, 5:6, :] * _unpack_bf16_pairs(_load_tiles(moe_ref)).astype(F32)
    o_ref[...] = _rms(y) * gf_ref[...]


def _final(xmid, moe_rows, mod, mod_row, gf):
    T = xmid.shape[0]
    return pl.pallas_call(
        _final_kernel,
        grid=(T // FINAL_BLOCK,),
        in_specs=[
            pl.BlockSpec((FINAL_BLOCK, D_MODEL), lambda i: (i, 0)),
            _tiles_spec(FINAL_BLOCK, lambda i: i, PACKED_CHUNKS),
            pl.BlockSpec((1, 6, D_MODEL), lambda i: (mod_row(i), 0, 0)),
            pl.BlockSpec((1, D_MODEL), lambda i: (0, 0)),
        ],
        out_specs=pl.BlockSpec((FINAL_BLOCK, D_MODEL), lambda i: (i, 0)),
        out_shape=jax.ShapeDtypeStruct((T, D_MODEL), F32),
        compiler_params=pltpu.CompilerParams(
            dimension_semantics=("arbitrary",), vmem_limit_bytes=V7X_VMEM_LIMIT_BYTES),
        name="moe_final",
    )(xmid, moe_rows, mod, gf)


def _flat(tiles):
    return tiles.reshape(-1, LANES)


def _moe_dispatch(h2_tiles, gate_rows, onehot):
    T = gate_rows.shape[0]
    n_tiles = T // SORT_TILE + N_BUCKETS
    n_rows = n_tiles * SORT_TILE
    assert n_tiles <= LANES and T % TOKEN_BLOCK == 0
    dest, meta = _plan(onehot)
    sorted_h2, sorted_gates = _sc_dispatch(_flat(h2_tiles), gate_rows, dest, n_rows)
    return sorted_h2.reshape(_tiles_shape(n_rows, PACKED_CHUNKS)), sorted_gates, dest, meta


def _moe_unpermute(moe_sorted_tiles, dest):
    chunks = moe_sorted_tiles.shape[1]
    return _sc_row_gather(_flat(moe_sorted_tiles), dest, chunks).reshape(_tiles_shape(dest.shape[0], chunks))


def _rope_tables(n_tokens):
    t = np.arange(n_tokens)
    row = (t // GRID_W).astype(np.float32)
    col = (t % GRID_W).astype(np.float32)
    freq = np.float32(ROPE_THETA) ** (-np.arange(ROPE_NF, dtype=np.float32) / np.float32(ROPE_NF))
    ang = np.concatenate([row[:, None] * freq] * 2 + [col[:, None] * freq] * 2, axis=-1)
    first = (np.arange(HEAD_DIM) % (2 * ROPE_NF)) < ROPE_NF
    sin = np.sin(ang)
    zero = np.float32(0.0)
    return (jnp.asarray(np.cos(ang)), jnp.asarray(np.where(first, -sin, zero)),
            jnp.asarray(np.where(first, zero, sin)))


def kernel(x_prompt, x_sample, cache_k, cache_v, c, c_ctx, norm1_g, norm2_g, w_ada, b_ada, w_in, q_norm_g, k_norm_g, w_pool, pool_scale, w_branch_a, w_branch_b, w_out, w_router_group, w_router_expert, w_exp_gate, w_exp_up, w_exp_down, final_norm_g):
    assert norm1_g.shape[0] == 1, "single-layer trunk"
    B, L_ctx, _ = x_prompt.shape
    Bs, L_lat, _ = x_sample.shape
    P = cache_k.shape[2]
    assert 1 + Bs <= COND_ROWS

    cond = jnp.concatenate([c_ctx[None, :], c, jnp.zeros((COND_ROWS - 1 - Bs, D_MODEL), F32)], axis=0)
    wpool_bd = jax.scipy.linalg.block_diag(*[w_pool[0, g] for g in range(len(POOL_WINDOWS))])
    mod, w_in_b, wpool_b, wa_b, wb_b, wo_b = _ada(
        cond, w_ada[0], b_ada[0][None, :],
        cast=(w_in[0], wpool_bd, w_branch_a[0], w_branch_b[0], w_out[0]))
    mod = mod.reshape(COND_ROWS, 6, D_MODEL)

    wr = jnp.concatenate([w_router_group[0], w_router_expert[0],
                          jnp.zeros((D_MODEL, LANES - N_EXP_GROUPS - N_EXPERTS), F32)], axis=1)
    wr_hi = wr.astype(BF16)
    wr_lo = (wr - wr_hi.astype(F32)).astype(BF16)
    mix_w = (norm1_g[0][None, :], w_in_b, q_norm_g[0][None, :], k_norm_g[0][None, :],
             wpool_b, pool_scale[0][None, :], wa_b, wb_b, wo_b,
             norm2_g[0][None, :], jnp.concatenate([wr_hi, wr_lo], axis=1))
    gf = final_norm_g[None, :]

    xp2 = x_prompt.reshape(B * L_ctx, D_MODEL)
    xmid_p, h2_p, gate_p, oh_p, knew, vnew, wgu, wd = _mix(
        xp2, mod, lambda i: 0, None, None, mix_w, S=2, L=L_ctx, emit_kv=True, blocks_per_step=2,
        cast=((w_exp_gate[0], w_exp_up[0]), (w_exp_down[0],)))
    sh_p, sg_p, dest_p, meta_p = _moe_dispatch(h2_p, gate_p, oh_p)

    xs2 = x_sample.reshape(Bs * L_lat, D_MODEL)
    cache = (cache_k[:, 0].reshape(Bs, P, KV_W), cache_v[:, 0].reshape(Bs, P, KV_W))
    xmid_s, h2_s, gate_s, oh_s = _mix(xs2, mod, lambda i: 1 + i, cache, _rope_tables(L_lat), mix_w,
                                      S=1, L=L_lat, emit_kv=False, blocks_per_step=1)
    sh_s, sg_s, dest_s, meta_s = _moe_dispatch(h2_s, gate_s, oh_s)

    moe_p = _moe_unpermute(_experts(sh_p, sg_p, meta_p, wgu, wd), dest_p)
    moe_s = _moe_unpermute(_experts(sh_s, sg_s, meta_s, wgu, wd), dest_s)
    y_prompt = _final(xmid_p, moe_p, mod, lambda i: 0, gf)
    blocks_per_seq = L_lat // FINAL_BLOCK
    y_sample = _final(xmid_s, moe_s, mod, lambda i: 1 + i // blocks_per_seq, gf)

    return (y_prompt.reshape(B, L_ctx, D_MODEL), y_sample.reshape(Bs, L_lat, D_MODEL),
            knew.reshape(B, 1, L_ctx, N_KV_HEADS, HEAD_DIM), vnew.reshape(B, 1, L_ctx, N_KV_HEADS, HEAD_DIM))
```

```python
import functools

import numpy as np
import jax
import jax.numpy as jnp
from jax import lax
from jax.experimental import pallas as pl
from jax.experimental.pallas import tpu as pltpu
from jax.experimental.pallas import tpu_sc as plsc

F32 = jnp.float32
BF16 = jnp.bfloat16
I32 = jnp.int32
U32 = jnp.uint32

D_MODEL = 1024
HEAD_DIM = 128
N_HEADS = 8
N_KV_HEADS = 2
GROUP = N_HEADS // N_KV_HEADS
ATTN_W = N_HEADS * HEAD_DIM
KV_W = N_KV_HEADS * HEAD_DIM
POOL_WINDOWS = (2, 4, 8, 16)
POOL_GC = 128
POOL_W = POOL_GC * len(POOL_WINDOWS)
IN_W = ATTN_W + 2 * KV_W + POOL_W + 2 * D_MODEL
GATE_COL = ATTN_W + 2 * KV_W + POOL_W
GRID_W = 64
ROPE_THETA = 10000.0
ROPE_NF = HEAD_DIM // 4
N_EXP_GROUPS = 4
EXP_PER_GROUP = 4
N_EXPERTS = 16
D_EXPERT = 256
EPS = 1e-6
LOG2_E = 1.4426950408889634

LANES = 128
SUBLANES = 8
COND_ROWS = SUBLANES
POOL_HALO = 8
ROW_BLOCK = 256
ADA_COLS = 768
EXPERT_LANE0 = N_EXP_GROUPS
PAIRS_PER_GROUP = EXP_PER_GROUP * (EXP_PER_GROUP - 1) // 2
N_BUCKETS = N_EXP_GROUPS * PAIRS_PER_GROUP
SORT_TILE = 512
EXPERT_TILES_PER_STEP = 4
TOKEN_BLOCK = 1024
FINAL_BLOCK = 1024
ROW_CHUNKS = D_MODEL // LANES
SC_CORES = 2
SC_SUBCORES = 16
SC_WORKERS = SC_CORES * SC_SUBCORES
SC_LANES = 16
SC_PIECES_PER_GATHER = 128
SC_ROWS_PER_STEP = 64
PACKED_CHUNKS = ROW_CHUNKS // 2
V7X_VMEM_LIMIT_BYTES = 56 * 1024 * 1024


def _sigmoid(x):
    return 1.0 / (1.0 + jnp.exp(-x))


def _rms(x):
    return x * lax.rsqrt(jnp.mean(x * x, axis=-1, keepdims=True) + EPS)


def _resident(shape):
    zeros = (0,) * len(shape)
    return pl.BlockSpec(shape, lambda i, *_: zeros, pipeline_mode=pl.Buffered(1))


def _tiles_shape(n, chunks=ROW_CHUNKS):
    return (n // SUBLANES, chunks, SUBLANES, LANES)


def _tiles_spec(n, block_index, chunks=ROW_CHUNKS):
    return pl.BlockSpec(_tiles_shape(n, chunks), lambda *a: (block_index(*a), 0, 0, 0))


def _store_tiles(ref, x):
    for c in range(ref.shape[1]):
        ref[:, c, :, :] = x[:, c * LANES:(c + 1) * LANES].reshape(x.shape[0] // SUBLANES, SUBLANES, LANES)


def _load_tiles(ref):
    n = ref.shape[0] * SUBLANES
    return jnp.concatenate([ref[:, c, :, :].reshape(n, LANES) for c in range(ref.shape[1])], axis=1)


def _pack_bf16_pairs(x):
    bits = pltpu.bitcast(x.astype(BF16).astype(F32), U32)
    w = x.shape[1] // 2
    return bits[:, :w] | (bits[:, w:] >> 16)


def _unpack_bf16_pairs(words):
    hi = pltpu.bitcast(words & jnp.uint32(0xFFFF0000), F32).astype(BF16)
    lo = pltpu.bitcast(words << 16, F32).astype(BF16)
    return jnp.concatenate([hi, lo], axis=1)


def _row(x):
    return jnp.transpose(jnp.broadcast_to(x, (x.shape[0], LANES)))[0:1, :]


def _ada_kernel(c_ref, w_ref, b_ref, *refs):
    n_cast = (len(refs) - 1) // 2
    c = c_ref[...]
    s = (c * _sigmoid(c)).astype(BF16)
    refs[n_cast][...] = jnp.dot(s, w_ref[...].astype(BF16), preferred_element_type=F32) + b_ref[...]
    for src, dst in zip(refs[:n_cast], refs[n_cast + 1:]):
        dst[...] = src[...].astype(BF16)


def _ada(cond, w_ada, b_ada, cast=()):
    n = w_ada.shape[1]
    n_steps = n // ADA_COLS
    cast_specs = []
    for w in cast:
        assert w.ndim == 2 and w.shape[0] % (n_steps * 2 * SUBLANES) == 0
        cast_specs.append(pl.BlockSpec((w.shape[0] // n_steps, w.shape[1]), lambda j: (j, 0)))
    return pl.pallas_call(
        _ada_kernel,
        grid=(n_steps,),
        in_specs=[
            pl.BlockSpec((COND_ROWS, D_MODEL), lambda j: (0, 0)),
            pl.BlockSpec((D_MODEL, ADA_COLS), lambda j: (0, j)),
            pl.BlockSpec((1, ADA_COLS), lambda j: (0, j)),
        ] + cast_specs,
        out_specs=[pl.BlockSpec((COND_ROWS, ADA_COLS), lambda j: (0, j))] + cast_specs,
        out_shape=[jax.ShapeDtypeStruct((COND_ROWS, n), F32)] + [jax.ShapeDtypeStruct(w.shape, BF16) for w in cast],
        name="ada_mod",
    )(cond, w_ada, b_ada, *cast)


def _route(logits):
    lane = lax.broadcasted_iota(I32, logits.shape, 1).astype(F32)
    neg = jnp.float32(-1e30)
    far = jnp.float32(LANES)
    is_g = lane < N_EXP_GROUPS
    gl = jnp.where(is_g, logits, neg)
    gmax = jnp.max(gl, axis=-1, keepdims=True)
    gsel = jnp.min(jnp.where(gl == gmax, lane, far), axis=-1, keepdims=True)
    psel = 1.0 / jnp.sum(jnp.where(is_g, jnp.exp(gl - gmax), 0.0), axis=-1, keepdims=True)
    e_lo = EXPERT_LANE0 + EXP_PER_GROUP * gsel
    el = jnp.where(lane >= e_lo, jnp.where(lane < e_lo + EXP_PER_GROUP, logits, neg), neg)
    v1 = jnp.max(el, axis=-1, keepdims=True)
    i1 = jnp.min(jnp.where(el == v1, lane, far), axis=-1, keepdims=True)
    el2 = jnp.where(lane == i1, neg, el)
    v2 = jnp.max(el2, axis=-1, keepdims=True)
    i2 = jnp.min(jnp.where(el2 == v2, jnp.where(lane == i1, far, lane), far), axis=-1, keepdims=True)
    e2 = jnp.exp(v2 - v1)
    w1 = psel / (1.0 + e2)
    w2 = psel * e2 / (1.0 + e2)
    gate = jnp.where(lane == i1, w1, jnp.where(lane == i2, w2, 0.0))
    a = jnp.minimum(i1, i2) - e_lo
    b = jnp.maximum(i1, i2) - e_lo
    pair = a * (7.0 - a) * 0.5 + (b - a - 1.0)
    return gate, gsel * PAIRS_PER_GROUP + pair


def _mix_kernel(*refs, S, L, P, use_rope, emit_kv, n_cast, n_blocks, U):
    it = iter(refs)
    x_ref = next(it)
    mod_ref = next(it)
    if P:
        ck_ref = next(it)
        cv_ref = next(it)
    if use_rope:
        cos_ref = next(it)
        sneg_ref = next(it)
        spos_ref = next(it)
    (g1_ref, win_ref, qg_ref, kg_ref, wpool_ref, pscale_ref, wa_ref, wb_ref, wo_ref,
     g2_ref, wr_ref) = (next(it) for _ in range(11))
    cast_in = [[next(it) for _ in range(n)] for n in n_cast]
    xmid_ref = next(it)
    h2_ref = next(it)
    gate_ref = next(it)
    oh_ref = next(it)
    if emit_kv:
        knew_ref = next(it)
        vnew_ref = next(it)
    cast_out = [next(it) for _ in n_cast]
    q_s, k_s, v_s, xp_s, h_s, attn_s, xm_s, mod2_s = (next(it) for _ in range(8))

    TM = S * L
    RB = ROW_BLOCK
    nrb = TM // RB
    n_steps = n_blocks // U
    score_gain = HEAD_DIM ** -0.5 * LOG2_E
    step = pl.program_id(0)
    block0 = U * jnp.minimum(step, n_steps - 1)
    slot = step % 2

    sh1 = mod_ref[0, 0:1, :]
    gain1 = g1_ref[...] * (1.0 + mod_ref[0, 1:2, :])
    gt1 = mod_ref[0, 2:3, :]
    sh2 = mod_ref[0, 3:4, :]
    gain2 = g2_ref[...] * (1.0 + mod_ref[0, 4:5, :])
    qg = qg_ref[...] * score_gain
    kg = kg_ref[...]

    def project(r, carry):
        r0 = pl.multiple_of(r * RB, RB)
        s = r0 // L
        o = pl.multiple_of(r0 % L, RB)
        hb = (_rms(x_ref[pl.ds(r0, RB), :]) * gain1 + sh1).astype(BF16)
        h_s[pl.ds(r0, RB), :] = hb
        p1 = jnp.dot(hb, win_ref[:, 0:GATE_COL], preferred_element_type=F32)
        if use_rope:
            cs = cos_ref[pl.ds(o, RB), :]
            sn = sneg_ref[pl.ds(o, RB), :]
            sp = spos_ref[pl.ds(o, RB), :]

        def rope(t):
            return (t * cs + pltpu.roll(t, HEAD_DIM - ROPE_NF, 1) * sn + pltpu.roll(t, ROPE_NF, 1) * sp)

        for hd in range(N_HEADS):
            qh = _rms(p1[:, hd * HEAD_DIM:(hd + 1) * HEAD_DIM]) * qg
            if use_rope:
                qh = rope(qh)
            q_s[hd, pl.ds(r0, RB), :] = qh.astype(BF16)
        for kh in range(N_KV_HEADS):
            c0 = ATTN_W + kh * HEAD_DIM
            kk = _rms(p1[:, c0:c0 + HEAD_DIM]) * kg
            if emit_kv:
                knew_ref[pl.ds(N_KV_HEADS * r0 + kh, RB, stride=N_KV_HEADS), :] = kk
            if use_rope:
                kk = rope(kk)
            k_s[s, pl.ds(P + o, RB), kh * HEAD_DIM:(kh + 1) * HEAD_DIM] = kk.astype(BF16)
        vv = p1[:, ATTN_W + KV_W:ATTN_W + 2 * KV_W]
        if emit_kv:
            for kh in range(N_KV_HEADS):
                vnew_ref[pl.ds(N_KV_HEADS * r0 + kh, RB, stride=N_KV_HEADS), :] = (
                    vv[:, kh * HEAD_DIM:(kh + 1) * HEAD_DIM])
        v_s[s, pl.ds(P + o, RB), :] = vv.astype(BF16)
        xp_s[s, pl.ds(POOL_HALO + o, RB), :] = p1[:, ATTN_W + 2 * KV_W:GATE_COL]
        return carry

    @pl.when(step == 0)
    def _():
        xm_s[1] = jnp.zeros((U * RB, D_MODEL), F32)
        mod2_s[1] = jnp.zeros((2, D_MODEL), F32)

    @pl.when((step < n_steps) & (step % (nrb // U) == 0))
    def _():
        if P:
            k_s[0, 0:P, :] = ck_ref[0].astype(BF16)
            v_s[0, 0:P, :] = cv_ref[0].astype(BF16)
        xp_s[:, 0:POOL_HALO, :] = jnp.zeros((S, POOL_HALO, POOL_W), F32)
        xp_s[:, L + POOL_HALO:L + 2 * POOL_HALO, :] = jnp.zeros((S, POOL_HALO, POOL_W), F32)
        lax.fori_loop(0, TM // RB, project, 0)
        for srcs, dst in zip(cast_in, cast_out):
            col = 0
            for src in srcs:
                dst[..., col:col + src.shape[-1]] = src[...].astype(BF16)
                col += src.shape[-1]

    def mix(u):
        r0 = pl.multiple_of(((block0 + u) % nrb) * RB, RB)
        s = r0 // L
        o = pl.multiple_of(r0 % L, RB)
        attn_u = attn_s.at[u]
        rows = slice(u * RB, (u + 1) * RB)

        for hd in range(N_HEADS):
            kh = hd // GROUP
            k = k_s[s, :, kh * HEAD_DIM:(kh + 1) * HEAD_DIM]
            v = v_s[s, :, kh * HEAD_DIM:(kh + 1) * HEAD_DIM]
            qh = q_s[hd, pl.ds(r0, RB), :]
            sc = lax.dot_general(qh, k, (((1,), (1,)), ((), ())), preferred_element_type=F32)
            e = jnp.exp2(sc - jnp.max(sc, axis=-1, keepdims=True))
            den = jnp.sum(e, axis=-1, keepdims=True)
            oh = jnp.dot(e.astype(BF16), v, preferred_element_type=F32) / den
            attn_u[:, hd * HEAD_DIM:(hd + 1) * HEAD_DIM] = oh.astype(BF16)
        a = jnp.dot(attn_u[...], wa_ref[...], preferred_element_type=F32)

        t = o + lax.broadcasted_iota(I32, (RB, 1), 0)
        RW = RB + 2 * POOL_HALO
        parts = []
        for gi, w in enumerate(POOL_WINDOWS):
            cols = slice(gi * POOL_GC, (gi + 1) * POOL_GC)
            xw = xp_s[s, pl.ds(o, RW), cols]
            run = xw
            span = 1
            while span < w:
                run = run + pltpu.roll(run, span, 0)
                span *= 2
            if w // 2 > 1:
                run = pltpu.roll(run, RW - (w // 2 - 1), 0)
            tot = run[POOL_HALO:POOL_HALO + RB]
            cnt = (jnp.minimum(t + w // 2, L) - jnp.maximum(t - w // 2, 0)).astype(F32)
            parts.append(tot / cnt - xw[POOL_HALO:POOL_HALO + RB])
        dpool = jnp.concatenate(parts, axis=1).astype(BF16)
        pooled = jnp.dot(dpool, wpool_ref[...], preferred_element_type=F32) * pscale_ref[...]
        b = jnp.dot(pooled.astype(BF16), wb_ref[...], preferred_element_type=F32)

        gates = jnp.dot(h_s[pl.ds(r0, RB), :], win_ref[:, GATE_COL:IN_W], preferred_element_type=F32)
        merged = _sigmoid(gates[:, 0:D_MODEL]) * a + _sigmoid(gates[:, D_MODEL:2 * D_MODEL]) * b
        upd = jnp.dot(merged.astype(BF16), wo_ref[...], preferred_element_type=F32)
        xm = x_ref[pl.ds(r0, RB), :] + gt1 * upd
        xmid_ref[rows, :] = xm
        xm_s[slot, rows, :] = xm

    def moe_prep(u):
        rows = slice(u * RB, (u + 1) * RB)
        h2 = _rms(xm_s[1 - slot, rows, :]) * mod2_s[1 - slot, 0:1, :] + mod2_s[1 - slot, 1:2, :]
        hi = h2.astype(BF16)
        lo = (h2 - hi.astype(F32)).astype(BF16)
        l1 = jnp.dot(hi, wr_ref[...], preferred_element_type=F32)
        l2 = jnp.dot(lo, wr_ref[:, 0:LANES], preferred_element_type=F32)
        gate, bucket = _route(l1[:, 0:LANES] + l1[:, LANES:2 * LANES] + l2)
        groups = pl.ds(u * (RB // SUBLANES), RB // SUBLANES)
        _store_tiles(h2_ref.at[groups], _pack_bf16_pairs(h2))
        gate_ref[rows, :] = gate
        lane = lax.broadcasted_iota(I32, (RB, LANES), 1).astype(F32)
        oh_ref[rows, :] = jnp.where(lane == bucket, 1.0, 0.0).astype(BF16)

    mod2_s[slot, 0:1, :] = gain2
    mod2_s[slot, 1:2, :] = sh2
    for u in range(U):
        moe_prep(u)
    for u in range(U):
        mix(u)


def _mix(x2d, mod, mod_row, cache, rope_tabs, weights, *, S, L, emit_kv, blocks_per_step, cast=()):
    T = x2d.shape[0]
    TM = S * L
    P = cache[0].shape[1] if cache is not None else 0
    use_rope = rope_tabs is not None
    assert T % TM == 0 and L % ROW_BLOCK == 0
    assert not (use_rope or P) or S == 1
    Lk = P + L

    args = [x2d, mod]
    nrb = TM // ROW_BLOCK
    n_blocks = T // ROW_BLOCK
    step_rows = blocks_per_step * ROW_BLOCK
    steps_per_group = nrb // blocks_per_step
    n_mix_steps = n_blocks // blocks_per_step
    assert nrb % blocks_per_step == 0

    def mixed(s):
        return jnp.minimum(s, n_mix_steps - 1)

    def group(s):
        return mixed(s) // steps_per_group

    def prepared(s):
        return jnp.maximum(s - 1, 0)

    in_specs = [
        pl.BlockSpec((TM, D_MODEL), lambda s: (group(s), 0)),
        pl.BlockSpec((1, 6, D_MODEL), lambda s: (mod_row(group(s)), 0, 0)),
    ]
    if P:
        args += list(cache)
        in_specs += [pl.BlockSpec((1, P, KV_W), lambda s: (group(s), 0, 0))] * 2
    if use_rope:
        args += list(rope_tabs)
        in_specs += [_resident((L, HEAD_DIM))] * 3
    args += list(weights)
    in_specs += [_resident(w.shape) for w in weights]
    n_steps = T // TM
    def per_group(shape):
        assert shape[0] % n_steps == 0
        blk = (shape[0] // n_steps,) + shape[1:]
        return pl.BlockSpec(blk, lambda s, n=len(blk): (group(s),) + (0,) * (n - 1))

    cast_out_shapes = [ws[0].shape[:-1] + (sum(w.shape[-1] for w in ws),) for ws in cast]
    for ws in cast:
        args += list(ws)
        in_specs += [per_group(w.shape) for w in ws]

    out_shape = [jax.ShapeDtypeStruct((T, D_MODEL), F32), jax.ShapeDtypeStruct(_tiles_shape(T, PACKED_CHUNKS), U32),
                 jax.ShapeDtypeStruct((T, LANES), F32),
                 jax.ShapeDtypeStruct((T, LANES), BF16)]
    out_specs = [pl.BlockSpec((step_rows, D_MODEL), lambda s: (mixed(s), 0)),
                 _tiles_spec(step_rows, prepared, PACKED_CHUNKS),
                 pl.BlockSpec((step_rows, LANES), lambda s: (prepared(s), 0)),
                 pl.BlockSpec((step_rows, LANES), lambda s: (prepared(s), 0))]
    if emit_kv:
        out_shape += [jax.ShapeDtypeStruct((T * N_KV_HEADS, HEAD_DIM), F32)] * 2
        out_specs += [pl.BlockSpec((TM * N_KV_HEADS, HEAD_DIM), lambda s: (group(s), 0))] * 2
    out_shape += [jax.ShapeDtypeStruct(shp, BF16) for shp in cast_out_shapes]
    out_specs += [per_group(shp) for shp in cast_out_shapes]

    scratch = [
        pltpu.VMEM((N_HEADS, TM, HEAD_DIM), BF16),
        pltpu.VMEM((S, Lk, KV_W), BF16),
        pltpu.VMEM((S, Lk, KV_W), BF16),
        pltpu.VMEM((S, L + 2 * POOL_HALO, POOL_W), F32),
        pltpu.VMEM((TM, D_MODEL), BF16),
        pltpu.VMEM((blocks_per_step, ROW_BLOCK, ATTN_W), BF16),
        pltpu.VMEM((2, step_rows, D_MODEL), F32),
        pltpu.VMEM((2, 2, D_MODEL), F32),
    ]
    kern = functools.partial(_mix_kernel, S=S, L=L, P=P, use_rope=use_rope, emit_kv=emit_kv,
                             n_cast=tuple(len(ws) for ws in cast), n_blocks=n_blocks, U=blocks_per_step)
    return pl.pallas_call(
        kern,
        grid=(n_mix_steps + 1,),
        in_specs=in_specs,
        out_specs=out_specs,
        out_shape=out_shape,
        scratch_shapes=scratch,
        compiler_params=pltpu.CompilerParams(
            dimension_semantics=("arbitrary",), vmem_limit_bytes=V7X_VMEM_LIMIT_BYTES),
        name="mixer_rope" if use_rope else "mixer_ctx",
    )(*args)


def _plan_kernel(oh_ref, dest_ref, meta_ref, *, n_blocks):
    TB = TOKEN_BLOCK
    lane = lax.broadcasted_iota(I32, (SUBLANES, LANES), 1)

    def count(b, acc):
        oh = oh_ref[pl.ds(pl.multiple_of(b * TB, TB), TB), :].astype(F32)
        return acc + jnp.sum(oh, axis=0, keepdims=True)

    counts = lax.fori_loop(0, n_blocks, count, jnp.zeros((SUBLANES, LANES), F32))
    padded = jnp.ceil(counts * (1.0 / SORT_TILE)) * SORT_TILE
    ends = padded
    step = 1
    while step < LANES:
        ends = ends + jnp.where(lane >= step, pltpu.roll(ends, step, 1), 0.0)
        step *= 2
    starts = ends - padded

    tri = jnp.where(lax.broadcasted_iota(I32, (TB, TB), 1) < lax.broadcasted_iota(I32, (TB, TB), 0),
                    1.0, 0.0).astype(BF16)

    def place(b, seen):
        oh = oh_ref[pl.ds(pl.multiple_of(b * TB, TB), TB), :]
        ohf = oh.astype(F32)
        rank = jnp.dot(tri, oh, preferred_element_type=F32)
        base = (starts + seen)[0:1, :]
        d = jnp.sum(ohf * (rank + base), axis=1, keepdims=True)
        dest_ref[b] = _row(d).astype(I32)
        return seen + jnp.sum(ohf, axis=0, keepdims=True)

    lax.fori_loop(0, n_blocks, place, jnp.zeros((SUBLANES, LANES), F32))

    tile_row0 = lax.broadcasted_iota(I32, (LANES, LANES), 0).astype(F32) * SORT_TILE
    is_bucket = lax.broadcasted_iota(I32, (LANES, LANES), 1) < N_BUCKETS
    done = jnp.sum(jnp.where(is_bucket, jnp.where(ends[0:1, :] <= tile_row0, 1.0, 0.0), 0.0),
                   axis=1, keepdims=True)
    bkt = jnp.minimum(done, N_BUCKETS - 1.0)
    grp = (jnp.where(bkt >= PAIRS_PER_GROUP, 1.0, 0.0) + jnp.where(bkt >= 2 * PAIRS_PER_GROUP, 1.0, 0.0)
           + jnp.where(bkt >= 3 * PAIRS_PER_GROUP, 1.0, 0.0))
    pair = bkt - PAIRS_PER_GROUP * grp
    a = jnp.where(pair >= 3.0, 1.0, 0.0) + jnp.where(pair >= 5.0, 1.0, 0.0)
    b = pair - a * (7.0 - a) * 0.5 + a + 1.0
    e1 = EXP_PER_GROUP * grp + a
    e2 = EXP_PER_GROUP * grp + b
    meta = jnp.concatenate(
        [_row(e1), _row(e2), ends[0:1, :] * (1.0 / SORT_TILE), jnp.zeros((SUBLANES - 3, LANES), F32)], axis=0)
    meta_ref[...] = meta.astype(I32)


def _plan(onehot):
    T = onehot.shape[0]
    n_blocks = T // TOKEN_BLOCK
    dest, meta = pl.pallas_call(
        functools.partial(_plan_kernel, n_blocks=n_blocks),
        out_shape=[jax.ShapeDtypeStruct((n_blocks, 1, TOKEN_BLOCK), I32),
                   jax.ShapeDtypeStruct((SUBLANES, LANES), I32)],
        name="moe_plan",
    )(onehot)
    return dest.reshape(T), meta


def _sc_move_rows(src_v, table_hbm, out_hbm, lo, n_rows, idx_v, pieces_v, sem):
    chunks = pieces_v.shape[0] // SC_ROWS_PER_STEP
    lane = lax.iota(I32, SC_LANES)
    row_in_group = lane & (SUBLANES - 1)
    chunk_in_pair = lane >> 3
    rows_per_gather = SC_PIECES_PER_GATHER // chunks

    @pl.loop(0, n_rows // SC_ROWS_PER_STEP)
    def _(step):
        copies = []
        for g in range(SC_ROWS_PER_STEP // rows_per_gather):
            r0 = step * SC_ROWS_PER_STEP + g * rows_per_gather
            for v in range(SC_PIECES_PER_GATHER // SC_LANES):
                group, chunk0 = v // (chunks // 2), 2 * (v % (chunks // 2))
                tok = plsc.load_gather(src_v, [r0 + group * SUBLANES + row_in_group])
                piece = (tok >> 3) * (SUBLANES * chunks) + (chunk0 + chunk_in_pair) * SUBLANES + (tok & 7)
                idx_v[pl.ds(g * SC_PIECES_PER_GATHER + v * SC_LANES, SC_LANES)] = piece
            window = pl.ds(g * SC_PIECES_PER_GATHER, SC_PIECES_PER_GATHER)
            copies.append(pltpu.async_copy(table_hbm.at[idx_v.at[window]], pieces_v.at[window], sem))
        for cp in copies:
            cp.wait()
        first = pl.multiple_of((lo + step * SC_ROWS_PER_STEP) * chunks, SC_ROWS_PER_STEP * chunks)
        pltpu.sync_copy(pieces_v, out_hbm.at[pl.ds(first, SC_ROWS_PER_STEP * chunks)])


def _sc_scratch(chunks, dtype):
    return [pltpu.VMEM((SC_ROWS_PER_STEP * chunks,), I32), pltpu.VMEM((SC_ROWS_PER_STEP * chunks, LANES), dtype)]


def _sc_dispatch(h2_flat, gate_rows, dest, n_rows):
    T = dest.shape[0]
    per_worker = n_rows // SC_WORKERS
    rows_per_step = SC_ROWS_PER_STEP
    chunks = h2_flat.shape[0] // T
    assert n_rows % SC_WORKERS == 0 and per_worker % rows_per_step == 0 and T % SC_LANES == 0
    mesh = plsc.VectorSubcoreMesh(core_axis_name="c", subcore_axis_name="s")

    @functools.partial(
        pl.kernel, mesh=mesh,
        out_type=[jax.ShapeDtypeStruct((n_rows * chunks, LANES), h2_flat.dtype),
                  jax.ShapeDtypeStruct((n_rows, LANES), F32)],
        scratch_types=[pltpu.VMEM((T,), I32), pltpu.VMEM((per_worker,), I32)]
        + _sc_scratch(chunks, h2_flat.dtype)
        + [pltpu.VMEM((rows_per_step, LANES), F32), pltpu.SemaphoreType.DMA, pltpu.SemaphoreType.DMA],
        compiler_params=pltpu.CompilerParams(use_tc_tiling_on_sc=True, needs_layout_passes=False),
        name="sc_dispatch",
    )
    def dispatch(h2_hbm, gate_hbm, dest_hbm, out_h_hbm, out_g_hbm,
                 dest_v, src_v, idx_v, pieces_v, gates_v, sem_h, sem_g):
        worker = lax.axis_index("s") * SC_CORES + lax.axis_index("c")
        lo = worker * per_worker
        pltpu.sync_copy(dest_hbm, dest_v)

        @pl.loop(0, per_worker // SC_LANES)
        def _(j):
            j0 = pl.multiple_of(j * SC_LANES, SC_LANES)
            src_v[pl.ds(j0, SC_LANES)] = lax.rem(lo + j0 + lax.iota(I32, SC_LANES), T)

        @pl.loop(0, T // SC_LANES)
        def _(j):
            t0 = pl.multiple_of(j * SC_LANES, SC_LANES)
            d = dest_v[pl.ds(t0, SC_LANES)] - lo
            mine = (d >= 0) & (d < per_worker)
            plsc.store_scatter(src_v, [jnp.where(mine, d, 0)], t0 + lax.iota(I32, SC_LANES), mask=mine)

        @pl.loop(0, per_worker // rows_per_step)
        def _(j):
            off = pl.multiple_of(j * rows_per_step, rows_per_step)
            pltpu.async_copy(gate_hbm.at[src_v.at[pl.ds(off, rows_per_step)]], gates_v, sem_g).wait()
            pltpu.sync_copy(gates_v, out_g_hbm.at[pl.ds(lo + off, rows_per_step)])

        _sc_move_rows(src_v, h2_hbm, out_h_hbm, lo, per_worker, idx_v, pieces_v, sem_h)

    return dispatch(h2_flat, gate_rows, dest)


def _expert_kernel(e1s, e2s, n_used, x_ref, gv_ref, wgu_ref, wd_ref, o_ref):
    groups = SORT_TILE // SUBLANES

    def one_tile(k, carry):
        t = pl.program_id(0) * EXPERT_TILES_PER_STEP + k
        rows = pl.ds(pl.multiple_of(k * groups, groups), groups)

        @pl.when(t < n_used[0])
        def _():
            x = _unpack_bf16_pairs(_load_tiles(x_ref.at[rows]))
            gv = gv_ref[pl.ds(pl.multiple_of(k * SORT_TILE, SORT_TILE), SORT_TILE), :]
            lane = lax.broadcasted_iota(I32, gv.shape, 1)
            out = None
            for e in (e1s[t], e2s[t]):
                ge = jnp.sum(jnp.where(lane == EXPERT_LANE0 + e, gv, 0.0), axis=-1, keepdims=True)
                h = jnp.dot(x, wgu_ref[e], preferred_element_type=F32)
                hg = h[:, 0:D_EXPERT]
                hid = (hg * _sigmoid(hg) * h[:, D_EXPERT:2 * D_EXPERT] * ge).astype(BF16)
                y = jnp.dot(hid, wd_ref[e], preferred_element_type=F32)
                out = y if out is None else out + y
            _store_tiles(o_ref.at[rows], _pack_bf16_pairs(out))

        @pl.when(t >= n_used[0])
        def _():
            o_ref[rows] = jnp.zeros((groups,) + o_ref.shape[1:], U32)

        return carry

    lax.fori_loop(0, EXPERT_TILES_PER_STEP, one_tile, 0)


def _experts(sorted_h2, sorted_gates, meta, wgu, wd):
    n_tiles = sorted_h2.shape[0] * SUBLANES // SORT_TILE
    step_rows = SORT_TILE * EXPERT_TILES_PER_STEP
    assert n_tiles % EXPERT_TILES_PER_STEP == 0

    def last_used(i, e1, e2, nu):
        return jnp.minimum(i, (nu[0] - 1) // EXPERT_TILES_PER_STEP)

    return pl.pallas_call(
        _expert_kernel,
        grid_spec=pltpu.PrefetchScalarGridSpec(
            num_scalar_prefetch=3,
            grid=(n_tiles // EXPERT_TILES_PER_STEP,),
            in_specs=[
                _tiles_spec(step_rows, last_used, PACKED_CHUNKS),
                pl.BlockSpec((step_rows, LANES), lambda *a: (last_used(*a), 0)),
                _resident(wgu.shape), _resident(wd.shape),
            ],
            out_specs=_tiles_spec(step_rows, lambda i, *_: i, PACKED_CHUNKS),
        ),
        out_shape=jax.ShapeDtypeStruct(_tiles_shape(n_tiles * SORT_TILE, PACKED_CHUNKS), U32),
        compiler_params=pltpu.CompilerParams(
            dimension_semantics=("arbitrary",), vmem_limit_bytes=V7X_VMEM_LIMIT_BYTES),
        name="moe_experts",
    )(meta[0, :n_tiles], meta[1, :n_tiles], meta[2, LANES - 1:LANES], sorted_h2, sorted_gates, wgu, wd)


def _sc_row_gather(table_flat, idx, chunks):
    n = idx.shape[0]
    per_worker = n // SC_WORKERS
    assert n % SC_WORKERS == 0 and per_worker % SC_ROWS_PER_STEP == 0
    mesh = plsc.VectorSubcoreMesh(core_axis_name="c", subcore_axis_name="s")

    @functools.partial(
        pl.kernel, mesh=mesh,
        out_type=jax.ShapeDtypeStruct((n * chunks, LANES), table_flat.dtype),
        scratch_types=[pltpu.VMEM((per_worker,), I32)] + _sc_scratch(chunks, table_flat.dtype)
        + [pltpu.SemaphoreType.DMA],
        compiler_params=pltpu.CompilerParams(use_tc_tiling_on_sc=True, needs_layout_passes=False),
        name="sc_row_gather",
    )
    def gather(table_hbm, idx_hbm, out_hbm, src_v, idx_v, pieces_v, sem):
        worker = lax.axis_index("s") * SC_CORES + lax.axis_index("c")
        lo = worker * per_worker
        pltpu.sync_copy(idx_hbm.at[pl.ds(lo, per_worker)], src_v)
        _sc_move_rows(src_v, table_hbm, out_hbm, lo, per_worker, idx_v, pieces_v, sem)

    return gather(table_flat, idx)


def _final_kernel(x_ref, moe_ref, mod_ref, gf_ref, o_ref):
    y = x_ref[...] + mod_ref[0, 5:6, :] * _unpack_bf16_pairs(_load_tiles(moe_ref)).astype(F32)
    o_ref[...] = _rms(y) * gf_ref[...]


def _final(xmid, moe_rows, mod, mod_row, gf):
    T = xmid.shape[0]
    return pl.pallas_call(
        _final_kernel,
        grid=(T // FINAL_BLOCK,),
        in_specs=[
            pl.BlockSpec((FINAL_BLOCK, D_MODEL), lambda i: (i, 0)),
            _tiles_spec(FINAL_BLOCK, lambda i: i, PACKED_CHUNKS),
            pl.BlockSpec((1, 6, D_MODEL), lambda i: (mod_row(i), 0, 0)),
            pl.BlockSpec((1, D_MODEL), lambda i: (0, 0)),
        ],
        out_specs=pl.BlockSpec((FINAL_BLOCK, D_MODEL), lambda i: (i, 0)),
        out_shape=jax.ShapeDtypeStruct((T, D_MODEL), F32),
        compiler_params=pltpu.CompilerParams(
            dimension_semantics=("arbitrary",), vmem_limit_bytes=V7X_VMEM_LIMIT_BYTES),
        name="moe_final",
    )(xmid, moe_rows, mod, gf)


def _flat(tiles):
    return tiles.reshape(-1, LANES)


def _moe_dispatch(h2_tiles, gate_rows, onehot):
    T = gate_rows.shape[0]
    n_tiles = T // SORT_TILE + N_BUCKETS
    n_rows = n_tiles * SORT_TILE
    assert n_tiles <= LANES and T % TOKEN_BLOCK == 0
    dest, meta = _plan(onehot)
    sorted_h2, sorted_gates = _sc_dispatch(_flat(h2_tiles), gate_rows, dest, n_rows)
    return sorted_h2.reshape(_tiles_shape(n_rows, PACKED_CHUNKS)), sorted_gates, dest, meta


def _moe_unpermute(moe_sorted_tiles, dest):
    chunks = moe_sorted_tiles.shape[1]
    return _sc_row_gather(_flat(moe_sorted_tiles), dest, chunks).reshape(_tiles_shape(dest.shape[0], chunks))


def _rope_tables(n_tokens):
    t = np.arange(n_tokens)
    row = (t // GRID_W).astype(np.float32)
    col = (t % GRID_W).astype(np.float32)
    freq = np.float32(ROPE_THETA) ** (-np.arange(ROPE_NF, dtype=np.float32) / np.float32(ROPE_NF))
    ang = np.concatenate([row[:, None] * freq] * 2 + [col[:, None] * freq] * 2, axis=-1)
    first = (np.arange(HEAD_DIM) % (2 * ROPE_NF)) < ROPE_NF
    sin = np.sin(ang)
    zero = np.float32(0.0)
    return (jnp.asarray(np.cos(ang)), jnp.asarray(np.where(first, -sin, zero)),
            jnp.asarray(np.where(first, zero, sin)))


def kernel(x_prompt, x_sample, cache_k, cache_v, c, c_ctx, norm1_g, norm2_g, w_ada, b_ada, w_in, q_norm_g, k_norm_g, w_pool, pool_scale, w_branch_a, w_branch_b, w_out, w_router_group, w_router_expert, w_exp_gate, w_exp_up, w_exp_down, final_norm_g):
    assert norm1_g.shape[0] == 1, "single-layer trunk"
    B, L_ctx, _ = x_prompt.shape
    Bs, L_lat, _ = x_sample.shape
    P = cache_k.shape[2]
    assert 1 + Bs <= COND_ROWS

    cond = jnp.concatenate([c_ctx[None, :], c, jnp.zeros((COND_ROWS - 1 - Bs, D_MODEL), F32)], axis=0)
    wpool_bd = jax.scipy.linalg.block_diag(*[w_pool[0, g] for g in range(len(POOL_WINDOWS))])
    mod, w_in_b, wpool_b, wa_b, wb_b, wo_b = _ada(
        cond, w_ada[0], b_ada[0][None, :],
        cast=(w_in[0], wpool_bd, w_branch_a[0], w_branch_b[0], w_out[0]))
    mod = mod.reshape(COND_ROWS, 6, D_MODEL)

    wr = jnp.concatenate([w_router_group[0], w_router_expert[0],
                          jnp.zeros((D_MODEL, LANES - N_EXP_GROUPS - N_EXPERTS), F32)], axis=1)
    wr_hi = wr.astype(BF16)
    wr_lo = (wr - wr_hi.astype(F32)).astype(BF16)
    mix_w = (norm1_g[0][None, :], w_in_b, q_norm_g[0][None, :], k_norm_g[0][None, :],
             wpool_b, pool_scale[0][None, :], wa_b, wb_b, wo_b,
             norm2_g[0][None, :], jnp.concatenate([wr_hi, wr_lo], axis=1))
    gf = final_norm_g[None, :]

    xp2 = x_prompt.reshape(B * L_ctx, D_MODEL)
    xmid_p, h2_p, gate_p, oh_p, knew, vnew, wgu, wd = _mix(
        xp2, mod, lambda i: 0, None, None, mix_w, S=2, L=L_ctx, emit_kv=True, blocks_per_step=2,
        cast=((w_exp_gate[0], w_exp_up[0]), (w_exp_down[0],)))
    sh_p, sg_p, dest_p, meta_p = _moe_dispatch(h2_p, gate_p, oh_p)

    xs2 = x_sample.reshape(Bs * L_lat, D_MODEL)
    cache = (cache_k[:, 0].reshape(Bs, P, KV_W), cache_v[:, 0].reshape(Bs, P, KV_W))
    xmid_s, h2_s, gate_s, oh_s = _mix(xs2, mod, lambda i: 1 + i, cache, _rope_tables(L_lat), mix_w,
                                      S=1, L=L_lat, emit_kv=False, blocks_per_step=1)
    sh_s, sg_s, dest_s, meta_s = _moe_dispatch(h2_s, gate_s, oh_s)

    moe_p = _moe_unpermute(_experts(sh_p, sg_p, meta_p, wgu, wd), dest_p)
    moe_s = _moe_unpermute(_experts(sh_s, sg_s, meta_s, wgu, wd), dest_s)
    y_prompt = _final(xmid_p, moe_p, mod, lambda i: 0, gf)
    blocks_per_seq = L_lat // FINAL_BLOCK
    y_sample = _final(xmid_s, moe_s, mod, lambda i: 1 + i // blocks_per_seq, gf)

    return (y_prompt.reshape(B, L_ctx, D_MODEL), y_sample.reshape(Bs, L_lat, D_MODEL),
            knew.reshape(B, 1, L_ctx, N_KV_HEADS, HEAD_DIM), vnew.reshape(B, 1, L_ctx, N_KV_HEADS, HEAD_DIM))
```

```python
import functools

import numpy as np
import jax
import jax.numpy as jnp
from jax import lax
from jax.experimental import pallas as pl
from jax.experimental.pallas import tpu as pltpu
from jax.experimental.pallas import tpu_sc as plsc

F32 = jnp.float32
BF16 = jnp.bfloat16
I32 = jnp.int32
U32 = jnp.uint32

D_MODEL = 1024
HEAD_DIM = 128
N_HEADS = 8
N_KV_HEADS = 2
GROUP = N_HEADS // N_KV_HEADS
ATTN_W = N_HEADS * HEAD_DIM
KV_W = N_KV_HEADS * HEAD_DIM
POOL_WINDOWS = (2, 4, 8, 16)
POOL_GC = 128
POOL_W = POOL_GC * len(POOL_WINDOWS)
IN_W = ATTN_W + 2 * KV_W + POOL_W + 2 * D_MODEL
GATE_COL = ATTN_W + 2 * KV_W + POOL_W
GRID_W = 64
ROPE_THETA = 10000.0
ROPE_NF = HEAD_DIM // 4
N_EXP_GROUPS = 4
EXP_PER_GROUP = 4
N_EXPERTS = 16
D_EXPERT = 256
EPS = 1e-6
LOG2_E = 1.4426950408889634

LANES = 128
SUBLANES = 8
COND_ROWS = SUBLANES
POOL_HALO = 8
ROW_BLOCK = 256
ADA_COLS = 768
EXPERT_LANE0 = N_EXP_GROUPS
PAIRS_PER_GROUP = EXP_PER_GROUP * (EXP_PER_GROUP - 1) // 2
N_BUCKETS = N_EXP_GROUPS * PAIRS_PER_GROUP
SORT_TILE = 256
EXPERT_TILES_PER_STEP = 8
TOKEN_BLOCK = 1024
FINAL_BLOCK = 1024
ROW_CHUNKS = D_MODEL // LANES
SC_CORES = 2
SC_SUBCORES = 16
SC_WORKERS = SC_CORES * SC_SUBCORES
SC_LANES = 16
SC_PIECES_PER_GATHER = 128
SC_ROWS_PER_STEP = 64
PACKED_CHUNKS = ROW_CHUNKS // 2
V7X_VMEM_LIMIT_BYTES = 56 * 1024 * 1024


def _sigmoid(x):
    return 1.0 / (1.0 + jnp.exp(-x))


def _rms(x):
    return x * lax.rsqrt(jnp.mean(x * x, axis=-1, keepdims=True) + EPS)


def _resident(shape):
    zeros = (0,) * len(shape)
    return pl.BlockSpec(shape, lambda i, *_: zeros, pipeline_mode=pl.Buffered(1))


def _tiles_shape(n, chunks=ROW_CHUNKS):
    return (n // SUBLANES, chunks, SUBLANES, LANES)


def _tiles_spec(n, block_index, chunks=ROW_CHUNKS):
    return pl.BlockSpec(_tiles_shape(n, chunks), lambda *a: (block_index(*a), 0, 0, 0))


def _store_tiles(ref, x):
    for c in range(ref.shape[1]):
        ref[:, c, :, :] = x[:, c * LANES:(c + 1) * LANES].reshape(x.shape[0] // SUBLANES, SUBLANES, LANES)


def _load_tiles(ref):
    n = ref.shape[0] * SUBLANES
    return jnp.concatenate([ref[:, c, :, :].reshape(n, LANES) for c in range(ref.shape[1])], axis=1)


def _pack_bf16_pairs(x):
    bits = pltpu.bitcast(x.astype(BF16).astype(F32), U32)
    w = x.shape[1] // 2
    return bits[:, :w] | (bits[:, w:] >> 16)


def _unpack_bf16_pairs(words):
    hi = pltpu.bitcast(words & jnp.uint32(0xFFFF0000), F32).astype(BF16)
    lo = pltpu.bitcast(words << 16, F32).astype(BF16)
    return jnp.concatenate([hi, lo], axis=1)


def _row(x):
    return jnp.transpose(jnp.broadcast_to(x, (x.shape[0], LANES)))[0:1, :]


def _ada_kernel(c_ref, w_ref, b_ref, *refs):
    n_cast = (len(refs) - 1) // 2
    c = c_ref[...]
    s = (c * _sigmoid(c)).astype(BF16)
    refs[n_cast][...] = jnp.dot(s, w_ref[...].astype(BF16), preferred_element_type=F32) + b_ref[...]
    for src, dst in zip(refs[:n_cast], refs[n_cast + 1:]):
        dst[...] = src[...].astype(BF16)


def _ada(cond, w_ada, b_ada, cast=()):
    n = w_ada.shape[1]
    n_steps = n // ADA_COLS
    cast_specs = []
    for w in cast:
        assert w.ndim == 2 and w.shape[0] % (n_steps * 2 * SUBLANES) == 0
        cast_specs.append(pl.BlockSpec((w.shape[0] // n_steps, w.shape[1]), lambda j: (j, 0)))
    return pl.pallas_call(
        _ada_kernel,
        grid=(n_steps,),
        in_specs=[
            pl.BlockSpec((COND_ROWS, D_MODEL), lambda j: (0, 0)),
            pl.BlockSpec((D_MODEL, ADA_COLS), lambda j: (0, j)),
            pl.BlockSpec((1, ADA_COLS), lambda j: (0, j)),
        ] + cast_specs,
        out_specs=[pl.BlockSpec((COND_ROWS, ADA_COLS), lambda j: (0, j))] + cast_specs,
        out_shape=[jax.ShapeDtypeStruct((COND_ROWS, n), F32)] + [jax.ShapeDtypeStruct(w.shape, BF16) for w in cast],
        name="ada_mod",
    )(cond, w_ada, b_ada, *cast)


def _route(logits):
    lane = lax.broadcasted_iota(I32, logits.shape, 1).astype(F32)
    neg = jnp.float32(-1e30)
    far = jnp.float32(LANES)
    is_g = lane < N_EXP_GROUPS
    gl = jnp.where(is_g, logits, neg)
    gmax = jnp.max(gl, axis=-1, keepdims=True)
    gsel = jnp.min(jnp.where(gl == gmax, lane, far), axis=-1, keepdims=True)
    psel = 1.0 / jnp.sum(jnp.where(is_g, jnp.exp(gl - gmax), 0.0), axis=-1, keepdims=True)
    e_lo = EXPERT_LANE0 + EXP_PER_GROUP * gsel
    el = jnp.where(lane >= e_lo, jnp.where(lane < e_lo + EXP_PER_GROUP, logits, neg), neg)
    v1 = jnp.max(el, axis=-1, keepdims=True)
    i1 = jnp.min(jnp.where(el == v1, lane, far), axis=-1, keepdims=True)
    el2 = jnp.where(lane == i1, neg, el)
    v2 = jnp.max(el2, axis=-1, keepdims=True)
    i2 = jnp.min(jnp.where(el2 == v2, jnp.where(lane == i1, far, lane), far), axis=-1, keepdims=True)
    e2 = jnp.exp(v2 - v1)
    w1 = psel / (1.0 + e2)
    w2 = psel * e2 / (1.0 + e2)
    gate = jnp.where(lane == i1, w1, jnp.where(lane == i2, w2, 0.0))
    a = jnp.minimum(i1, i2) - e_lo
    b = jnp.maximum(i1, i2) - e_lo
    pair = a * (7.0 - a) * 0.5 + (b - a - 1.0)
    return gate, gsel * PAIRS_PER_GROUP + pair


def _mix_kernel(*refs, S, L, P, use_rope, emit_kv, n_cast, n_blocks, U):
    it = iter(refs)
    x_ref = next(it)
    mod_ref = next(it)
    if P:
        ck_ref = next(it)
        cv_ref = next(it)
    if use_rope:
        cos_ref = next(it)
        sneg_ref = next(it)
        spos_ref = next(it)
    (g1_ref, win_ref, qg_ref, kg_ref, wpool_ref, pscale_ref, wa_ref, wb_ref, wo_ref,
     g2_ref, wr_ref) = (next(it) for _ in range(11))
    cast_in = [[next(it) for _ in range(n)] for n in n_cast]
    xmid_ref = next(it)
    h2_ref = next(it)
    gate_ref = next(it)
    oh_ref = next(it)
    if emit_kv:
        knew_ref = next(it)
        vnew_ref = next(it)
    cast_out = [next(it) for _ in n_cast]
    q_s, k_s, v_s, xp_s, h_s, attn_s, xm_s, mod2_s = (next(it) for _ in range(8))

    TM = S * L
    RB = ROW_BLOCK
    nrb = TM // RB
    n_steps = n_blocks // U
    score_gain = HEAD_DIM ** -0.5 * LOG2_E
    step = pl.program_id(0)
    block0 = U * jnp.minimum(step, n_steps - 1)
    slot = step % 2

    sh1 = mod_ref[0, 0:1, :]
    gain1 = g1_ref[...] * (1.0 + mod_ref[0, 1:2, :])
    gt1 = mod_ref[0, 2:3, :]
    sh2 = mod_ref[0, 3:4, :]
    gain2 = g2_ref[...] * (1.0 + mod_ref[0, 4:5, :])
    qg = qg_ref[...] * score_gain
    kg = kg_ref[...]

    def project(r, carry):
        r0 = pl.multiple_of(r * RB, RB)
        s = r0 // L
        o = pl.multiple_of(r0 % L, RB)
        hb = (_rms(x_ref[pl.ds(r0, RB), :]) * gain1 + sh1).astype(BF16)
        h_s[pl.ds(r0, RB), :] = hb
        p1 = jnp.dot(hb, win_ref[:, 0:GATE_COL], preferred_element_type=F32)
        if use_rope:
            cs = cos_ref[pl.ds(o, RB), :]
            sn = sneg_ref[pl.ds(o, RB), :]
            sp = spos_ref[pl.ds(o, RB), :]

        def rope(t):
            return (t * cs + pltpu.roll(t, HEAD_DIM - ROPE_NF, 1) * sn + pltpu.roll(t, ROPE_NF, 1) * sp)

        for hd in range(N_HEADS):
            qh = _rms(p1[:, hd * HEAD_DIM:(hd + 1) * HEAD_DIM]) * qg
            if use_rope:
                qh = rope(qh)
            q_s[hd, pl.ds(r0, RB), :] = qh.astype(BF16)
        for kh in range(N_KV_HEADS):
            c0 = ATTN_W + kh * HEAD_DIM
            kk = _rms(p1[:, c0:c0 + HEAD_DIM]) * kg
            if emit_kv:
                knew_ref[pl.ds(N_KV_HEADS * r0 + kh, RB, stride=N_KV_HEADS), :] = kk
            if use_rope:
                kk = rope(kk)
            k_s[s, pl.ds(P + o, RB), kh * HEAD_DIM:(kh + 1) * HEAD_DIM] = kk.astype(BF16)
        vv = p1[:, ATTN_W + KV_W:ATTN_W + 2 * KV_W]
        if emit_kv:
            for kh in range(N_KV_HEADS):
                vnew_ref[pl.ds(N_KV_HEADS * r0 + kh, RB, stride=N_KV_HEADS), :] = (
                    vv[:, kh * HEAD_DIM:(kh + 1) * HEAD_DIM])
        v_s[s, pl.ds(P + o, RB), :] = vv.astype(BF16)
        xp_s[s, pl.ds(POOL_HALO + o, RB), :] = p1[:, ATTN_W + 2 * KV_W:GATE_COL]
        return carry

    @pl.when(step == 0)
    def _():
        xm_s[1] = jnp.zeros((U * RB, D_MODEL), F32)
        mod2_s[1] = jnp.zeros((2, D_MODEL), F32)

    @pl.when((step < n_steps) & (step % (nrb // U) == 0))
    def _():
        if P:
            for kh in range(N_KV_HEADS):
                cols = slice(kh * HEAD_DIM, (kh + 1) * HEAD_DIM)
                k_s[0, 0:P, cols] = ck_ref[pl.ds(kh, P, stride=N_KV_HEADS), :].astype(BF16)
                v_s[0, 0:P, cols] = cv_ref[pl.ds(kh, P, stride=N_KV_HEADS), :].astype(BF16)
        xp_s[:, 0:POOL_HALO, :] = jnp.zeros((S, POOL_HALO, POOL_W), F32)
        xp_s[:, L + POOL_HALO:L + 2 * POOL_HALO, :] = jnp.zeros((S, POOL_HALO, POOL_W), F32)
        lax.fori_loop(0, TM // RB, project, 0)
        for srcs, dst in zip(cast_in, cast_out):
            col = 0
            for src in srcs:
                dst[..., col:col + src.shape[-1]] = src[...].astype(BF16)
                col += src.shape[-1]

    def mix(u):
        r0 = pl.multiple_of(((block0 + u) % nrb) * RB, RB)
        s = r0 // L
        o = pl.multiple_of(r0 % L, RB)
        attn_u = attn_s.at[u]
        rows = slice(u * RB, (u + 1) * RB)

        for hd in range(N_HEADS):
            kh = hd // GROUP
            k = k_s[s, :, kh * HEAD_DIM:(kh + 1) * HEAD_DIM]
            v = v_s[s, :, kh * HEAD_DIM:(kh + 1) * HEAD_DIM]
            qh = q_s[hd, pl.ds(r0, RB), :]
            sc = lax.dot_general(qh, k, (((1,), (1,)), ((), ())), preferred_element_type=F32)
            e = jnp.exp2(sc - jnp.max(sc, axis=-1, keepdims=True))
            den = jnp.sum(e, axis=-1, keepdims=True)
            oh = jnp.dot(e.astype(BF16), v, preferred_element_type=F32) / den
            attn_u[:, hd * HEAD_DIM:(hd + 1) * HEAD_DIM] = oh.astype(BF16)
        a = jnp.dot(attn_u[...], wa_ref[...], preferred_element_type=F32)

        t = o + lax.broadcasted_iota(I32, (RB, 1), 0)
        RW = RB + 2 * POOL_HALO
        parts = []
        for gi, w in enumerate(POOL_WINDOWS):
            cols = slice(gi * POOL_GC, (gi + 1) * POOL_GC)
            xw = xp_s[s, pl.ds(o, RW), cols]
            run = xw
            span = 1
            while span < w:
                run = run + pltpu.roll(run, span, 0)
                span *= 2
            if w // 2 > 1:
                run = pltpu.roll(run, RW - (w // 2 - 1), 0)
            tot = run[POOL_HALO:POOL_HALO + RB]
            cnt = (jnp.minimum(t + w // 2, L) - jnp.maximum(t - w // 2, 0)).astype(F32)
            parts.append(tot / cnt - xw[POOL_HALO:POOL_HALO + RB])
        dpool = jnp.concatenate(parts, axis=1).astype(BF16)
        pooled = jnp.dot(dpool, wpool_ref[...], preferred_element_type=F32) * pscale_ref[...]
        b = jnp.dot(pooled.astype(BF16), wb_ref[...], preferred_element_type=F32)

        gates = jnp.dot(h_s[pl.ds(r0, RB), :], win_ref[:, GATE_COL:IN_W], preferred_element_type=F32)
        merged = _sigmoid(gates[:, 0:D_MODEL]) * a + _sigmoid(gates[:, D_MODEL:2 * D_MODEL]) * b
        upd = jnp.dot(merged.astype(BF16), wo_ref[...], preferred_element_type=F32)
        xm = x_ref[pl.ds(r0, RB), :] + gt1 * upd
        xmid_ref[rows, :] = xm
        xm_s[slot, rows, :] = xm

    def moe_prep(u):
        rows = slice(u * RB, (u + 1) * RB)
        h2 = _rms(xm_s[1 - slot, rows, :]) * mod2_s[1 - slot, 0:1, :] + mod2_s[1 - slot, 1:2, :]
        hi = h2.astype(BF16)
        lo = (h2 - hi.astype(F32)).astype(BF16)
        l1 = jnp.dot(hi, wr_ref[...], preferred_element_type=F32)
        l2 = jnp.dot(lo, wr_ref[:, 0:LANES], preferred_element_type=F32)
        gate, bucket = _route(l1[:, 0:LANES] + l1[:, LANES:2 * LANES] + l2)
        groups = pl.ds(u * (RB // SUBLANES), RB // SUBLANES)
        _store_tiles(h2_ref.at[groups], _pack_bf16_pairs(h2))
        gate_ref[rows, :] = gate
        lane = lax.broadcasted_iota(I32, (RB, LANES), 1).astype(F32)
        oh_ref[rows, :] = jnp.where(lane == bucket, 1.0, 0.0).astype(BF16)

    mod2_s[slot, 0:1, :] = gain2
    mod2_s[slot, 1:2, :] = sh2
    for u in range(U):
        moe_prep(u)
    for u in range(U):
        mix(u)


def _mix(x2d, mod, mod_row, cache, rope_tabs, weights, *, S, L, emit_kv, blocks_per_step, cast=()):
    T = x2d.shape[0]
    TM = S * L
    P = cache[0].shape[0] // (T // L * N_KV_HEADS) if cache is not None else 0
    use_rope = rope_tabs is not None
    assert T % TM == 0 and L % ROW_BLOCK == 0
    assert not (use_rope or P) or S == 1
    Lk = P + L

    args = [x2d, mod]
    nrb = TM // ROW_BLOCK
    n_blocks = T // ROW_BLOCK
    step_rows = blocks_per_step * ROW_BLOCK
    steps_per_group = nrb // blocks_per_step
    n_mix_steps = n_blocks // blocks_per_step
    assert nrb % blocks_per_step == 0

    def mixed(s):
        return jnp.minimum(s, n_mix_steps - 1)

    def group(s):
        return mixed(s) // steps_per_group

    def prepared(s):
        return jnp.maximum(s - 1, 0)

    in_specs = [
        pl.BlockSpec((TM, D_MODEL), lambda s: (group(s), 0)),
        pl.BlockSpec((1, 6, D_MODEL), lambda s: (mod_row(group(s)), 0, 0)),
    ]
    if P:
        args += list(cache)
        in_specs += [pl.BlockSpec((P * N_KV_HEADS, HEAD_DIM), lambda s: (group(s), 0))] * 2
    if use_rope:
        args += list(rope_tabs)
        in_specs += [_resident((L, HEAD_DIM))] * 3
    args += list(weights)
    in_specs += [_resident(w.shape) for w in weights]
    n_steps = T // TM
    def per_group(shape):
        assert shape[0] % n_steps == 0
        blk = (shape[0] // n_steps,) + shape[1:]
        return pl.BlockSpec(blk, lambda s, n=len(blk): (group(s),) + (0,) * (n - 1))

    cast_out_shapes = [ws[0].shape[:-1] + (sum(w.shape[-1] for w in ws),) for ws in cast]
    for ws in cast:
        args += list(ws)
        in_specs += [per_group(w.shape) for w in ws]

    out_shape = [jax.ShapeDtypeStruct((T, D_MODEL), F32), jax.ShapeDtypeStruct(_tiles_shape(T, PACKED_CHUNKS), U32),
                 jax.ShapeDtypeStruct((T, LANES), F32),
                 jax.ShapeDtypeStruct((T, LANES), BF16)]
    out_specs = [pl.BlockSpec((step_rows, D_MODEL), lambda s: (mixed(s), 0)),
                 _tiles_spec(step_rows, prepared, PACKED_CHUNKS),
                 pl.BlockSpec((step_rows, LANES), lambda s: (prepared(s), 0)),
                 pl.BlockSpec((step_rows, LANES), lambda s: (prepared(s), 0))]
    if emit_kv:
        out_shape += [jax.ShapeDtypeStruct((T * N_KV_HEADS, HEAD_DIM), F32)] * 2
        out_specs += [pl.BlockSpec((TM * N_KV_HEADS, HEAD_DIM), lambda s: (group(s), 0))] * 2
    out_shape += [jax.ShapeDtypeStruct(shp, BF16) for shp in cast_out_shapes]
    out_specs += [per_group(shp) for shp in cast_out_shapes]

    scratch = [
        pltpu.VMEM((N_HEADS, TM, HEAD_DIM), BF16),
        pltpu.VMEM((S, Lk, KV_W), BF16),
        pltpu.VMEM((S, Lk, KV_W), BF16),
        pltpu.VMEM((S, L + 2 * POOL_HALO, POOL_W), F32),
        pltpu.VMEM((TM, D_MODEL), BF16),
        pltpu.VMEM((blocks_per_step, ROW_BLOCK, ATTN_W), BF16),
        pltpu.VMEM((2, step_rows, D_MODEL), F32),
        pltpu.VMEM((2, 2, D_MODEL), F32),
    ]
    kern = functools.partial(_mix_kernel, S=S, L=L, P=P, use_rope=use_rope, emit_kv=emit_kv,
                             n_cast=tuple(len(ws) for ws in cast), n_blocks=n_blocks, U=blocks_per_step)
    return pl.pallas_call(
        kern,
        grid=(n_mix_steps + 1,),
        in_specs=in_specs,
        out_specs=out_specs,
        out_shape=out_shape,
        scratch_shapes=scratch,
        compiler_params=pltpu.CompilerParams(
            dimension_semantics=("arbitrary",), vmem_limit_bytes=V7X_VMEM_LIMIT_BYTES),
        name="mixer_rope" if use_rope else "mixer_ctx",
    )(*args)


def _plan_kernel(oh_ref, dest_ref, meta_ref, *, n_blocks):
    TB = TOKEN_BLOCK
    lane = lax.broadcasted_iota(I32, (SUBLANES, LANES), 1)

    def count(b, acc):
        oh = oh_ref[pl.ds(pl.multiple_of(b * TB, TB), TB), :].astype(F32)
        return acc + jnp.sum(oh, axis=0, keepdims=True)

    counts = lax.fori_loop(0, n_blocks, count, jnp.zeros((SUBLANES, LANES), F32))
    padded = jnp.ceil(counts * (1.0 / SORT_TILE)) * SORT_TILE
    ends = padded
    step = 1
    while step < LANES:
        ends = ends + jnp.where(lane >= step, pltpu.roll(ends, step, 1), 0.0)
        step *= 2
    starts = ends - padded

    tri = jnp.where(lax.broadcasted_iota(I32, (TB, TB), 1) < lax.broadcasted_iota(I32, (TB, TB), 0),
                    1.0, 0.0).astype(BF16)

    def place(b, seen):
        oh = oh_ref[pl.ds(pl.multiple_of(b * TB, TB), TB), :]
        ohf = oh.astype(F32)
        rank = jnp.dot(tri, oh, preferred_element_type=F32)
        base = (starts + seen)[0:1, :]
        d = jnp.sum(ohf * (rank + base), axis=1, keepdims=True)
        dest_ref[b] = _row(d).astype(I32)
        return seen + jnp.sum(ohf, axis=0, keepdims=True)

    lax.fori_loop(0, n_blocks, place, jnp.zeros((SUBLANES, LANES), F32))

    tile_row0 = lax.broadcasted_iota(I32, (LANES, LANES), 0).astype(F32) * SORT_TILE
    is_bucket = lax.broadcasted_iota(I32, (LANES, LANES), 1) < N_BUCKETS
    done = jnp.sum(jnp.where(is_bucket, jnp.where(ends[0:1, :] <= tile_row0, 1.0, 0.0), 0.0),
                   axis=1, keepdims=True)
    bkt = jnp.minimum(done, N_BUCKETS - 1.0)
    grp = (jnp.where(bkt >= PAIRS_PER_GROUP, 1.0, 0.0) + jnp.where(bkt >= 2 * PAIRS_PER_GROUP, 1.0, 0.0)
           + jnp.where(bkt >= 3 * PAIRS_PER_GROUP, 1.0, 0.0))
    pair = bkt - PAIRS_PER_GROUP * grp
    a = jnp.where(pair >= 3.0, 1.0, 0.0) + jnp.where(pair >= 5.0, 1.0, 0.0)
    b = pair - a * (7.0 - a) * 0.5 + a + 1.0
    e1 = EXP_PER_GROUP * grp + a
    e2 = EXP_PER_GROUP * grp + b
    meta = jnp.concatenate(
        [_row(e1), _row(e2), ends[0:1, :] * (1.0 / SORT_TILE), jnp.zeros((SUBLANES - 3, LANES), F32)], axis=0)
    meta_ref[...] = meta.astype(I32)


def _plan(onehot):
    T = onehot.shape[0]
    n_blocks = T // TOKEN_BLOCK
    dest, meta = pl.pallas_call(
        functools.partial(_plan_kernel, n_blocks=n_blocks),
        out_shape=[jax.ShapeDtypeStruct((n_blocks, 1, TOKEN_BLOCK), I32),
                   jax.ShapeDtypeStruct((SUBLANES, LANES), I32)],
        name="moe_plan",
    )(onehot)
    return dest.reshape(T), meta


def _sc_move_rows(src_v, table_hbm, out_hbm, lo, n_rows, idx_v, pieces_v, sem):
    chunks = pieces_v.shape[0] // SC_ROWS_PER_STEP
    lane = lax.iota(I32, SC_LANES)
    row_in_group = lane & (SUBLANES - 1)
    chunk_in_pair = lane >> 3
    rows_per_gather = SC_PIECES_PER_GATHER // chunks

    @pl.loop(0, n_rows // SC_ROWS_PER_STEP)
    def _(step):
        copies = []
        for g in range(SC_ROWS_PER_STEP // rows_per_gather):
            r0 = step * SC_ROWS_PER_STEP + g * rows_per_gather
            for v in range(SC_PIECES_PER_GATHER // SC_LANES):
                group, chunk0 = v // (chunks // 2), 2 * (v % (chunks // 2))
                tok = plsc.load_gather(src_v, [r0 + group * SUBLANES + row_in_group])
                piece = (tok >> 3) * (SUBLANES * chunks) + (chunk0 + chunk_in_pair) * SUBLANES + (tok & 7)
                idx_v[pl.ds(g * SC_PIECES_PER_GATHER + v * SC_LANES, SC_LANES)] = piece
            window = pl.ds(g * SC_PIECES_PER_GATHER, SC_PIECES_PER_GATHER)
            copies.append(pltpu.async_copy(table_hbm.at[idx_v.at[window]], pieces_v.at[window], sem))
        for cp in copies:
            cp.wait()
        first = pl.multiple_of((lo + step * SC_ROWS_PER_STEP) * chunks, SC_ROWS_PER_STEP * chunks)
        pltpu.sync_copy(pieces_v, out_hbm.at[pl.ds(first, SC_ROWS_PER_STEP * chunks)])


def _sc_scratch(chunks, dtype):
    return [pltpu.VMEM((SC_ROWS_PER_STEP * chunks,), I32), pltpu.VMEM((SC_ROWS_PER_STEP * chunks, LANES), dtype)]


def _sc_dispatch(h2_flat, gate_rows, dest, n_rows):
    T = dest.shape[0]
    per_worker = n_rows // SC_WORKERS
    rows_per_step = SC_ROWS_PER_STEP
    chunks = h2_flat.shape[0] // T
    assert n_rows % SC_WORKERS == 0 and per_worker % rows_per_step == 0 and T % SC_LANES == 0
    mesh = plsc.VectorSubcoreMesh(core_axis_name="c", subcore_axis_name="s")

    @functools.partial(
        pl.kernel, mesh=mesh,
        out_type=[jax.ShapeDtypeStruct((n_rows * chunks, LANES), h2_flat.dtype),
                  jax.ShapeDtypeStruct((n_rows, LANES), F32)],
        scratch_types=[pltpu.VMEM((T,), I32), pltpu.VMEM((per_worker,), I32)]
        + _sc_scratch(chunks, h2_flat.dtype)
        + [pltpu.VMEM((rows_per_step, LANES), F32), pltpu.SemaphoreType.DMA, pltpu.SemaphoreType.DMA],
        compiler_params=pltpu.CompilerParams(use_tc_tiling_on_sc=True, needs_layout_passes=False),
        name="sc_dispatch",
    )
    def dispatch(h2_hbm, gate_hbm, dest_hbm, out_h_hbm, out_g_hbm,
                 dest_v, src_v, idx_v, pieces_v, gates_v, sem_h, sem_g):
        worker = lax.axis_index("s") * SC_CORES + lax.axis_index("c")
        lo = worker * per_worker
        pltpu.sync_copy(dest_hbm, dest_v)

        @pl.loop(0, per_worker // SC_LANES)
        def _(j):
            j0 = pl.multiple_of(j * SC_LANES, SC_LANES)
            src_v[pl.ds(j0, SC_LANES)] = lax.rem(lo + j0 + lax.iota(I32, SC_LANES), T)

        @pl.loop(0, T // SC_LANES)
        def _(j):
            t0 = pl.multiple_of(j * SC_LANES, SC_LANES)
            d = dest_v[pl.ds(t0, SC_LANES)] - lo
            mine = (d >= 0) & (d < per_worker)
            plsc.store_scatter(src_v, [jnp.where(mine, d, 0)], t0 + lax.iota(I32, SC_LANES), mask=mine)

        @pl.loop(0, per_worker // rows_per_step)
        def _(j):
            off = pl.multiple_of(j * rows_per_step, rows_per_step)
            pltpu.async_copy(gate_hbm.at[src_v.at[pl.ds(off, rows_per_step)]], gates_v, sem_g).wait()
            pltpu.sync_copy(gates_v, out_g_hbm.at[pl.ds(lo + off, rows_per_step)])

        _sc_move_rows(src_v, h2_hbm, out_h_hbm, lo, per_worker, idx_v, pieces_v, sem_h)

    return dispatch(h2_flat, gate_rows, dest)


def _expert_kernel(meta, x_ref, gv_ref, wgu_ref, wd_ref, o_ref):
    groups = SORT_TILE // SUBLANES
    n_used = meta[2, LANES - 1]

    def one_tile(k, carry):
        t = pl.program_id(0) * EXPERT_TILES_PER_STEP + k
        rows = pl.ds(pl.multiple_of(k * groups, groups), groups)

        @pl.when(t < n_used)
        def _():
            x = _unpack_bf16_pairs(_load_tiles(x_ref.at[rows]))
            gv = gv_ref[pl.ds(pl.multiple_of(k * SORT_TILE, SORT_TILE), SORT_TILE), :]
            lane = lax.broadcasted_iota(I32, gv.shape, 1)
            out = None
            for e in (meta[0, t], meta[1, t]):
                ge = jnp.sum(jnp.where(lane == EXPERT_LANE0 + e, gv, 0.0), axis=-1, keepdims=True)
                h = jnp.dot(x, wgu_ref[e], preferred_element_type=F32)
                hg = h[:, 0:D_EXPERT]
                hid = (hg * _sigmoid(hg) * h[:, D_EXPERT:2 * D_EXPERT] * ge).astype(BF16)
                y = jnp.dot(hid, wd_ref[e], preferred_element_type=F32)
                out = y if out is None else out + y
            _store_tiles(o_ref.at[rows], _pack_bf16_pairs(out))

        @pl.when(t >= n_used)
        def _():
            o_ref[rows] = jnp.zeros((groups,) + o_ref.shape[1:], U32)

        return carry

    lax.fori_loop(0, EXPERT_TILES_PER_STEP, one_tile, 0)


def _experts(sorted_h2, sorted_gates, meta, wgu, wd):
    n_tiles = sorted_h2.shape[0] * SUBLANES // SORT_TILE
    step_rows = SORT_TILE * EXPERT_TILES_PER_STEP
    assert n_tiles % EXPERT_TILES_PER_STEP == 0

    def last_used(i, meta):
        return jnp.minimum(i, (meta[2, LANES - 1] - 1) // EXPERT_TILES_PER_STEP)

    return pl.pallas_call(
        _expert_kernel,
        grid_spec=pltpu.PrefetchScalarGridSpec(
            num_scalar_prefetch=1,
            grid=(n_tiles // EXPERT_TILES_PER_STEP,),
            in_specs=[
                _tiles_spec(step_rows, last_used, PACKED_CHUNKS),
                pl.BlockSpec((step_rows, LANES), lambda *a: (last_used(*a), 0)),
                _resident(wgu.shape), _resident(wd.shape),
            ],
            out_specs=_tiles_spec(step_rows, lambda i, *_: i, PACKED_CHUNKS),
        ),
        out_shape=jax.ShapeDtypeStruct(_tiles_shape(n_tiles * SORT_TILE, PACKED_CHUNKS), U32),
        compiler_params=pltpu.CompilerParams(
            dimension_semantics=("arbitrary",), vmem_limit_bytes=V7X_VMEM_LIMIT_BYTES),
        name="moe_experts",
    )(meta, sorted_h2, sorted_gates, wgu, wd)


def _sc_row_gather(table_flat, idx, chunks):
    n = idx.shape[0]
    per_worker = n // SC_WORKERS
    assert n % SC_WORKERS == 0 and per_worker % SC_ROWS_PER_STEP == 0
    mesh = plsc.VectorSubcoreMesh(core_axis_name="c", subcore_axis_name="s")

    @functools.partial(
        pl.kernel, mesh=mesh,
        out_type=jax.ShapeDtypeStruct((n * chunks, LANES), table_flat.dtype),
        scratch_types=[pltpu.VMEM((per_worker,), I32)] + _sc_scratch(chunks, table_flat.dtype)
        + [pltpu.SemaphoreType.DMA],
        compiler_params=pltpu.CompilerParams(use_tc_tiling_on_sc=True, needs_layout_passes=False),
        name="sc_row_gather",
    )
    def gather(table_hbm, idx_hbm, out_hbm, src_v, idx_v, pieces_v, sem):
        worker = lax.axis_index("s") * SC_CORES + lax.axis_index("c")
        lo = worker * per_worker
        pltpu.sync_copy(idx_hbm.at[pl.ds(lo, per_worker)], src_v)
        _sc_move_rows(src_v, table_hbm, out_hbm, lo, per_worker, idx_v, pieces_v, sem)

    return gather(table_flat, idx)


def _final_kernel(x_ref, moe_ref, mod_ref, gf_ref, o_ref):
    y = x_ref[...] + mod_ref[0, 5:6, :] * _unpack_bf16_pairs(_load_tiles(moe_ref)).astype(F32)
    o_ref[...] = _rms(y) * gf_ref[...]


def _final(xmid, moe_rows, mod, mod_row, gf):
    T = xmid.shape[0]
    return pl.pallas_call(
        _final_kernel,
        grid=(T // FINAL_BLOCK,),
        in_specs=[
            pl.BlockSpec((FINAL_BLOCK, D_MODEL), lambda i: (i, 0)),
            _tiles_spec(FINAL_BLOCK, lambda i: i, PACKED_CHUNKS),
            pl.BlockSpec((1, 6, D_MODEL), lambda i: (mod_row(i), 0, 0)),
            pl.BlockSpec((1, D_MODEL), lambda i: (0, 0)),
        ],
        out_specs=pl.BlockSpec((FINAL_BLOCK, D_MODEL), lambda i: (i, 0)),
        out_shape=jax.ShapeDtypeStruct((T, D_MODEL), F32),
        compiler_params=pltpu.CompilerParams(
            dimension_semantics=("arbitrary",), vmem_limit_bytes=V7X_VMEM_LIMIT_BYTES),
        name="moe_final",
    )(xmid, moe_rows, mod, gf)


def _flat(tiles):
    return tiles.reshape(-1, LANES)


def _moe_dispatch(h2_tiles, gate_rows, onehot):
    T = gate_rows.shape[0]
    n_tiles = T // SORT_TILE + N_BUCKETS
    n_rows = n_tiles * SORT_TILE
    assert n_tiles <= LANES and T % TOKEN_BLOCK == 0
    dest, meta = _plan(onehot)
    sorted_h2, sorted_gates = _sc_dispatch(_flat(h2_tiles), gate_rows, dest, n_rows)
    return sorted_h2.reshape(_tiles_shape(n_rows, PACKED_CHUNKS)), sorted_gates, dest, meta


def _moe_unpermute(moe_sorted_tiles, dest):
    chunks = moe_sorted_tiles.shape[1]
    return _sc_row_gather(_flat(moe_sorted_tiles), dest, chunks).reshape(_tiles_shape(dest.shape[0], chunks))


def _rope_tables(n_tokens):
    t = np.arange(n_tokens)
    row = (t // GRID_W).astype(np.float32)
    col = (t % GRID_W).astype(np.float32)
    freq = np.float32(ROPE_THETA) ** (-np.arange(ROPE_NF, dtype=np.float32) / np.float32(ROPE_NF))
    ang = np.concatenate([row[:, None] * freq] * 2 + [col[:, None] * freq] * 2, axis=-1)
    first = (np.arange(HEAD_DIM) % (2 * ROPE_NF)) < ROPE_NF
    sin = np.sin(ang)
    zero = np.float32(0.0)
    return (jnp.asarray(np.cos(ang)), jnp.asarray(np.where(first, -sin, zero)),
            jnp.asarray(np.where(first, zero, sin)))


def kernel(x_prompt, x_sample, cache_k, cache_v, c, c_ctx, norm1_g, norm2_g, w_ada, b_ada, w_in, q_norm_g, k_norm_g, w_pool, pool_scale, w_branch_a, w_branch_b, w_out, w_router_group, w_router_expert, w_exp_gate, w_exp_up, w_exp_down, final_norm_g):
    assert norm1_g.shape[0] == 1, "single-layer trunk"
    B, L_ctx, _ = x_prompt.shape
    Bs, L_lat, _ = x_sample.shape
    P = cache_k.shape[2]
    assert 1 + Bs <= COND_ROWS

    cond = jnp.concatenate([c_ctx[None, :], c, jnp.zeros((COND_ROWS - 1 - Bs, D_MODEL), F32)], axis=0)
    wpool_bd = jax.scipy.linalg.block_diag(*[w_pool[0, g] for g in range(len(POOL_WINDOWS))])
    mod, w_in_b, wpool_b, wa_b, wb_b, wo_b = _ada(
        cond, w_ada[0], b_ada[0][None, :],
        cast=(w_in[0], wpool_bd, w_branch_a[0], w_branch_b[0], w_out[0]))
    mod = mod.reshape(COND_ROWS, 6, D_MODEL)

    wr = jnp.concatenate([w_router_group[0], w_router_expert[0],
                          jnp.zeros((D_MODEL, LANES - N_EXP_GROUPS - N_EXPERTS), F32)], axis=1)
    wr_hi = wr.astype(BF16)
    wr_lo = (wr - wr_hi.astype(F32)).astype(BF16)
    mix_w = (norm1_g[0][None, :], w_in_b, q_norm_g[0][None, :], k_norm_g[0][None, :],
             wpool_b, pool_scale[0][None, :], wa_b, wb_b, wo_b,
             norm2_g[0][None, :], jnp.concatenate([wr_hi, wr_lo], axis=1))
    gf = final_norm_g[None, :]

    xp2 = x_prompt.reshape(B * L_ctx, D_MODEL)
    xmid_p, h2_p, gate_p, oh_p, knew, vnew, wgu, wd = _mix(
        xp2, mod, lambda i: 0, None, None, mix_w, S=2, L=L_ctx, emit_kv=True, blocks_per_step=2,
        cast=((w_exp_gate[0], w_exp_up[0]), (w_exp_down[0],)))
    sh_p, sg_p, dest_p, meta_p = _moe_dispatch(h2_p, gate_p, oh_p)

    xs2 = x_sample.reshape(Bs * L_lat, D_MODEL)
    cache = (cache_k.reshape(Bs * P * N_KV_HEADS, HEAD_DIM), cache_v.reshape(Bs * P * N_KV_HEADS, HEAD_DIM))
    xmid_s, h2_s, gate_s, oh_s = _mix(xs2, mod, lambda i: 1 + i, cache, _rope_tables(L_lat), mix_w,
                                      S=1, L=L_lat, emit_kv=False, blocks_per_step=1)
    sh_s, sg_s, dest_s, meta_s = _moe_dispatch(h2_s, gate_s, oh_s)

    moe_p = _moe_unpermute(_experts(sh_p, sg_p, meta_p, wgu, wd), dest_p)
    moe_s = _moe_unpermute(_experts(sh_s, sg_s, meta_s, wgu, wd), dest_s)
    y_prompt = _final(xmid_p, moe_p, mod, lambda i: 0, gf)
    blocks_per_seq = L_lat // FINAL_BLOCK
    y_sample = _final(xmid_s, moe_s, mod, lambda i: 1 + i // blocks_per_seq, gf)

    return (y_prompt.reshape(B, L_ctx, D_MODEL), y_sample.reshape(Bs, L_lat, D_MODEL),
            knew.reshape(B, 1, L_ctx, N_KV_HEADS, HEAD_DIM), vnew.reshape(B, 1, L_ctx, N_KV_HEADS, HEAD_DIM))
```

```python
import functools

import numpy as np
import jax
import jax.numpy as jnp
from jax import lax
from jax.experimental import pallas as pl
from jax.experimental.pallas import tpu as pltpu
from jax.experimental.pallas import tpu_sc as plsc

F32 = jnp.float32
BF16 = jnp.bfloat16
I32 = jnp.int32
U32 = jnp.uint32

D_MODEL = 1024
HEAD_DIM = 128
N_HEADS = 8
N_KV_HEADS = 2
GROUP = N_HEADS // N_KV_HEADS
ATTN_W = N_HEADS * HEAD_DIM
KV_W = N_KV_HEADS * HEAD_DIM
POOL_WINDOWS = (2, 4, 8, 16)
POOL_GC = 128
POOL_W = POOL_GC * len(POOL_WINDOWS)
IN_W = ATTN_W + 2 * KV_W + POOL_W + 2 * D_MODEL
GATE_COL = ATTN_W + 2 * KV_W + POOL_W
GRID_W = 64
ROPE_THETA = 10000.0
ROPE_NF = HEAD_DIM // 4
N_EXP_GROUPS = 4
EXP_PER_GROUP = 4
N_EXPERTS = 16
D_EXPERT = 256
EPS = 1e-6
LOG2_E = 1.4426950408889634

LANES = 128
SUBLANES = 8
COND_ROWS = SUBLANES
POOL_HALO = 8
ROW_BLOCK = 256
ADA_COLS = 768
EXPERT_LANE0 = N_EXP_GROUPS
PAIRS_PER_GROUP = EXP_PER_GROUP * (EXP_PER_GROUP - 1) // 2
N_BUCKETS = N_EXP_GROUPS * PAIRS_PER_GROUP
SORT_TILE = 256
EXPERT_TILES_PER_STEP = 8
TOKEN_BLOCK = 1024
FINAL_BLOCK = 1024
ROW_CHUNKS = D_MODEL // LANES
SC_CORES = 2
SC_SUBCORES = 16
SC_WORKERS = SC_CORES * SC_SUBCORES
SC_LANES = 16
SC_PIECES_PER_GATHER = 128
SC_ROWS_PER_STEP = 64
PACKED_CHUNKS = ROW_CHUNKS // 2
V7X_VMEM_LIMIT_BYTES = 56 * 1024 * 1024


def _sigmoid(x):
    return 1.0 / (1.0 + jnp.exp(-x))


def _rms(x):
    return x * lax.rsqrt(jnp.mean(x * x, axis=-1, keepdims=True) + EPS)


def _resident(shape):
    zeros = (0,) * len(shape)
    return pl.BlockSpec(shape, lambda i, *_: zeros, pipeline_mode=pl.Buffered(1))


def _tiles_shape(n, chunks=ROW_CHUNKS):
    return (n // SUBLANES, chunks, SUBLANES, LANES)


def _tiles_spec(n, block_index, chunks=ROW_CHUNKS):
    return pl.BlockSpec(_tiles_shape(n, chunks), lambda *a: (block_index(*a), 0, 0, 0))


def _store_tiles(ref, x):
    for c in range(ref.shape[1]):
        ref[:, c, :, :] = x[:, c * LANES:(c + 1) * LANES].reshape(x.shape[0] // SUBLANES, SUBLANES, LANES)


def _load_tiles(ref):
    n = ref.shape[0] * SUBLANES
    return jnp.concatenate([ref[:, c, :, :].reshape(n, LANES) for c in range(ref.shape[1])], axis=1)


def _pack_bf16_pairs(x):
    bits = pltpu.bitcast(x.astype(BF16).astype(F32), U32)
    w = x.shape[1] // 2
    return bits[:, :w] | (bits[:, w:] >> 16)


def _unpack_bf16_pairs(words):
    hi = pltpu.bitcast(words & jnp.uint32(0xFFFF0000), F32).astype(BF16)
    lo = pltpu.bitcast(words << 16, F32).astype(BF16)
    return jnp.concatenate([hi, lo], axis=1)


def _row(x):
    return jnp.transpose(jnp.broadcast_to(x, (x.shape[0], LANES)))[0:1, :]


def _ada_kernel(c_ref, w_ref, b_ref, *refs):
    n_cast = (len(refs) - 1) // 2
    c = c_ref[...]
    s = (c * _sigmoid(c)).astype(BF16)
    refs[n_cast][...] = jnp.dot(s, w_ref[...].astype(BF16), preferred_element_type=F32) + b_ref[...]
    for src, dst in zip(refs[:n_cast], refs[n_cast + 1:]):
        dst[...] = src[...].astype(BF16)


def _ada(cond, w_ada, b_ada, cast=()):
    n = w_ada.shape[1]
    n_steps = n // ADA_COLS
    cast_specs = []
    for w in cast:
        assert w.ndim == 2 and w.shape[0] % (n_steps * 2 * SUBLANES) == 0
        cast_specs.append(pl.BlockSpec((w.shape[0] // n_steps, w.shape[1]), lambda j: (j, 0)))
    return pl.pallas_call(
        _ada_kernel,
        grid=(n_steps,),
        in_specs=[
            pl.BlockSpec((COND_ROWS, D_MODEL), lambda j: (0, 0)),
            pl.BlockSpec((D_MODEL, ADA_COLS), lambda j: (0, j)),
            pl.BlockSpec((1, ADA_COLS), lambda j: (0, j)),
        ] + cast_specs,
        out_specs=[pl.BlockSpec((COND_ROWS, ADA_COLS), lambda j: (0, j))] + cast_specs,
        out_shape=[jax.ShapeDtypeStruct((COND_ROWS, n), F32)] + [jax.ShapeDtypeStruct(w.shape, BF16) for w in cast],
        name="ada_mod",
    )(cond, w_ada, b_ada, *cast)


def _route(logits):
    lane = lax.broadcasted_iota(I32, logits.shape, 1).astype(F32)
    neg = jnp.float32(-1e30)
    far = jnp.float32(LANES)
    is_g = lane < N_EXP_GROUPS
    gl = jnp.where(is_g, logits, neg)
    gmax = jnp.max(gl, axis=-1, keepdims=True)
    gsel = jnp.min(jnp.where(gl == gmax, lane, far), axis=-1, keepdims=True)
    psel = 1.0 / jnp.sum(jnp.where(is_g, jnp.exp(gl - gmax), 0.0), axis=-1, keepdims=True)
    e_lo = EXPERT_LANE0 + EXP_PER_GROUP * gsel
    el = jnp.where(lane >= e_lo, jnp.where(lane < e_lo + EXP_PER_GROUP, logits, neg), neg)
    v1 = jnp.max(el, axis=-1, keepdims=True)
    i1 = jnp.min(jnp.where(el == v1, lane, far), axis=-1, keepdims=True)
    el2 = jnp.where(lane == i1, neg, el)
    v2 = jnp.max(el2, axis=-1, keepdims=True)
    i2 = jnp.min(jnp.where(el2 == v2, jnp.where(lane == i1, far, lane), far), axis=-1, keepdims=True)
    e2 = jnp.exp(v2 - v1)
    w1 = psel / (1.0 + e2)
    w2 = psel * e2 / (1.0 + e2)
    gate = jnp.where(lane == i1, w1, jnp.where(lane == i2, w2, 0.0))
    a = jnp.minimum(i1, i2) - e_lo
    b = jnp.maximum(i1, i2) - e_lo
    pair = a * (7.0 - a) * 0.5 + (b - a - 1.0)
    return gate, gsel * PAIRS_PER_GROUP + pair


def _mix_kernel(*refs, S, L, P, use_rope, emit_kv, n_cast, n_blocks, U):
    it = iter(refs)
    x_ref = next(it)
    mod_ref = next(it)
    if P:
        ck_ref = next(it)
        cv_ref = next(it)
    if use_rope:
        cos_ref = next(it)
        sneg_ref = next(it)
        spos_ref = next(it)
    (g1_ref, win_ref, qg_ref, kg_ref, wpool_ref, pscale_ref, wa_ref, wb_ref, wo_ref,
     g2_ref, wr_ref) = (next(it) for _ in range(11))
    cast_in = [[next(it) for _ in range(n)] for n in n_cast]
    xmid_ref = next(it)
    h2_ref = next(it)
    gate_ref = next(it)
    oh_ref = next(it)
    if emit_kv:
        knew_ref = next(it)
        vnew_ref = next(it)
    cast_out = [next(it) for _ in n_cast]
    q_s, k_s, v_s, xp_s, h_s, attn_s, xm_s, mod2_s = (next(it) for _ in range(8))

    TM = S * L
    RB = ROW_BLOCK
    nrb = TM // RB
    n_steps = n_blocks // U
    score_gain = HEAD_DIM ** -0.5 * LOG2_E
    step = pl.program_id(0)
    block0 = U * jnp.minimum(step, n_steps - 1)
    slot = step % 2

    sh1 = mod_ref[0, 0:1, :]
    gain1 = g1_ref[...] * (1.0 + mod_ref[0, 1:2, :])
    gt1 = mod_ref[0, 2:3, :]
    sh2 = mod_ref[0, 3:4, :]
    gain2 = g2_ref[...] * (1.0 + mod_ref[0, 4:5, :])
    qg = qg_ref[...] * score_gain
    kg = kg_ref[...]

    def project(r, carry):
        r0 = pl.multiple_of(r * RB, RB)
        s = r0 // L
        o = pl.multiple_of(r0 % L, RB)
        hb = (_rms(x_ref[pl.ds(r0, RB), :]) * gain1 + sh1).astype(BF16)
        h_s[pl.ds(r0, RB), :] = hb
        p1 = jnp.dot(hb, win_ref[:, 0:GATE_COL], preferred_element_type=F32)
        if use_rope:
            cs = cos_ref[pl.ds(o, RB), :]
            sn = sneg_ref[pl.ds(o, RB), :]
            sp = spos_ref[pl.ds(o, RB), :]

        def rope(t):
            return (t * cs + pltpu.roll(t, HEAD_DIM - ROPE_NF, 1) * sn + pltpu.roll(t, ROPE_NF, 1) * sp)

        for hd in range(N_HEADS):
            qh = _rms(p1[:, hd * HEAD_DIM:(hd + 1) * HEAD_DIM]) * qg
            if use_rope:
                qh = rope(qh)
            q_s[hd, pl.ds(r0, RB), :] = qh.astype(BF16)
        for kh in range(N_KV_HEADS):
            c0 = ATTN_W + kh * HEAD_DIM
            kk = _rms(p1[:, c0:c0 + HEAD_DIM]) * kg
            if emit_kv:
                knew_ref[pl.ds(N_KV_HEADS * r0 + kh, RB, stride=N_KV_HEADS), :] = kk
            if use_rope:
                kk = rope(kk)
            k_s[s, pl.ds(P + o, RB), kh * HEAD_DIM:(kh + 1) * HEAD_DIM] = kk.astype(BF16)
        vv = p1[:, ATTN_W + KV_W:ATTN_W + 2 * KV_W]
        if emit_kv:
            for kh in range(N_KV_HEADS):
                vnew_ref[pl.ds(N_KV_HEADS * r0 + kh, RB, stride=N_KV_HEADS), :] = (
                    vv[:, kh * HEAD_DIM:(kh + 1) * HEAD_DIM])
        v_s[s, pl.ds(P + o, RB), :] = vv.astype(BF16)
        xp_s[s, pl.ds(POOL_HALO + o, RB), :] = p1[:, ATTN_W + 2 * KV_W:GATE_COL]
        return carry

    @pl.when(step == 0)
    def _():
        xm_s[1] = jnp.zeros((U * RB, D_MODEL), F32)
        mod2_s[1] = jnp.zeros((2, D_MODEL), F32)

    @pl.when((step < n_steps) & (step % (nrb // U) == 0))
    def _():
        if P:
            for kh in range(N_KV_HEADS):
                cols = slice(kh * HEAD_DIM, (kh + 1) * HEAD_DIM)
                k_s[0, 0:P, cols] = ck_ref[pl.ds(kh, P, stride=N_KV_HEADS), :].astype(BF16)
                v_s[0, 0:P, cols] = cv_ref[pl.ds(kh, P, stride=N_KV_HEADS), :].astype(BF16)
        xp_s[:, 0:POOL_HALO, :] = jnp.zeros((S, POOL_HALO, POOL_W), F32)
        xp_s[:, L + POOL_HALO:L + 2 * POOL_HALO, :] = jnp.zeros((S, POOL_HALO, POOL_W), F32)
        lax.fori_loop(0, TM // RB, project, 0)
        for srcs, dst in zip(cast_in, cast_out):
            col = 0
            for src in srcs:
                dst[..., col:col + src.shape[-1]] = src[...].astype(BF16)
                col += src.shape[-1]

    def mix(u):
        r0 = pl.multiple_of(((block0 + u) % nrb) * RB, RB)
        s = r0 // L
        o = pl.multiple_of(r0 % L, RB)
        attn_u = attn_s.at[u]
        rows = slice(u * RB, (u + 1) * RB)

        for hd in range(N_HEADS):
            kh = hd // GROUP
            k = k_s[s, :, kh * HEAD_DIM:(kh + 1) * HEAD_DIM]
            v = v_s[s, :, kh * HEAD_DIM:(kh + 1) * HEAD_DIM]
            qh = q_s[hd, pl.ds(r0, RB), :]
            sc = lax.dot_general(qh, k, (((1,), (1,)), ((), ())), preferred_element_type=F32)
            e = jnp.exp2(sc - jnp.max(sc, axis=-1, keepdims=True))
            den = jnp.sum(e, axis=-1, keepdims=True)
            oh = jnp.dot(e.astype(BF16), v, preferred_element_type=F32) / den
            attn_u[:, hd * HEAD_DIM:(hd + 1) * HEAD_DIM] = oh.astype(BF16)
        a = jnp.dot(attn_u[...], wa_ref[...], preferred_element_type=F32)

        t = o + lax.broadcasted_iota(I32, (RB, 1), 0)
        RW = RB + 2 * POOL_HALO
        parts = []
        for gi, w in enumerate(POOL_WINDOWS):
            cols = slice(gi * POOL_GC, (gi + 1) * POOL_GC)
            xw = xp_s[s, pl.ds(o, RW), cols]
            run = xw
            span = 1
            while span < w:
                run = run + pltpu.roll(run, span, 0)
                span *= 2
            if w // 2 > 1:
                run = pltpu.roll(run, RW - (w // 2 - 1), 0)
            tot = run[POOL_HALO:POOL_HALO + RB]
            cnt = (jnp.minimum(t + w // 2, L) - jnp.maximum(t - w // 2, 0)).astype(F32)
            parts.append(tot / cnt - xw[POOL_HALO:POOL_HALO + RB])
        dpool = jnp.concatenate(parts, axis=1).astype(BF16)
        pooled = jnp.dot(dpool, wpool_ref[...], preferred_element_type=F32) * pscale_ref[...]
        b = jnp.dot(pooled.astype(BF16), wb_ref[...], preferred_element_type=F32)

        gates = jnp.dot(h_s[pl.ds(r0, RB), :], win_ref[:, GATE_COL:IN_W], preferred_element_type=F32)
        merged = _sigmoid(gates[:, 0:D_MODEL]) * a + _sigmoid(gates[:, D_MODEL:2 * D_MODEL]) * b
        upd = jnp.dot(merged.astype(BF16), wo_ref[...], preferred_element_type=F32)
        xm = x_ref[pl.ds(r0, RB), :] + gt1 * upd
        xmid_ref[rows, :] = xm
        xm_s[slot, rows, :] = xm

    def moe_prep(u):
        rows = slice(u * RB, (u + 1) * RB)
        h2 = _rms(xm_s[1 - slot, rows, :]) * mod2_s[1 - slot, 0:1, :] + mod2_s[1 - slot, 1:2, :]
        hi = h2.astype(BF16)
        lo = (h2 - hi.astype(F32)).astype(BF16)
        l1 = jnp.dot(hi, wr_ref[...], preferred_element_type=F32)
        l2 = jnp.dot(lo, wr_ref[:, 0:LANES], preferred_element_type=F32)
        gate, bucket = _route(l1[:, 0:LANES] + l1[:, LANES:2 * LANES] + l2)
        groups = pl.ds(u * (RB // SUBLANES), RB // SUBLANES)
        _store_tiles(h2_ref.at[groups], _pack_bf16_pairs(h2))
        gate_ref[rows, :] = gate
        lane = lax.broadcasted_iota(I32, (RB, LANES), 1).astype(F32)
        oh_ref[rows, :] = jnp.where(lane == bucket, 1.0, 0.0).astype(BF16)

    mod2_s[slot, 0:1, :] = gain2
    mod2_s[slot, 1:2, :] = sh2
    for u in range(U):
        moe_prep(u)
    for u in range(U):
        mix(u)


def _mix(x2d, mod, mod_row, cache, rope_tabs, weights, *, S, L, emit_kv, blocks_per_step, cast=()):
    T = x2d.shape[0]
    TM = S * L
    P = cache[0].shape[0] // (T // L * N_KV_HEADS) if cache is not None else 0
    use_rope = rope_tabs is not None
    assert T % TM == 0 and L % ROW_BLOCK == 0
    assert not (use_rope or P) or S == 1
    Lk = P + L

    args = [x2d, mod]
    nrb = TM // ROW_BLOCK
    n_blocks = T // ROW_BLOCK
    step_rows = blocks_per_step * ROW_BLOCK
    steps_per_group = nrb // blocks_per_step
    n_mix_steps = n_blocks // blocks_per_step
    assert nrb % blocks_per_step == 0

    def mixed(s):
        return jnp.minimum(s, n_mix_steps - 1)

    def group(s):
        return mixed(s) // steps_per_group

    def prepared(s):
        return jnp.maximum(s - 1, 0)

    in_specs = [
        pl.BlockSpec((TM, D_MODEL), lambda s: (group(s), 0)),
        pl.BlockSpec((1, 6, D_MODEL), lambda s: (mod_row(group(s)), 0, 0)),
    ]
    if P:
        args += list(cache)
        in_specs += [pl.BlockSpec((P * N_KV_HEADS, HEAD_DIM), lambda s: (group(s), 0))] * 2
    if use_rope:
        args += list(rope_tabs)
        in_specs += [_resident((L, HEAD_DIM))] * 3
    args += list(weights)
    in_specs += [_resident(w.shape) for w in weights]
    n_steps = T // TM
    def per_group(shape):
        assert shape[0] % n_steps == 0
        blk = (shape[0] // n_steps,) + shape[1:]
        return pl.BlockSpec(blk, lambda s, n=len(blk): (group(s),) + (0,) * (n - 1))

    cast_out_shapes = [ws[0].shape[:-1] + (sum(w.shape[-1] for w in ws),) for ws in cast]
    for ws in cast:
        args += list(ws)
        in_specs += [per_group(w.shape) for w in ws]

    out_shape = [jax.ShapeDtypeStruct((T, D_MODEL), F32), jax.ShapeDtypeStruct(_tiles_shape(T, PACKED_CHUNKS), U32),
                 jax.ShapeDtypeStruct((T, LANES), F32),
                 jax.ShapeDtypeStruct((T, LANES), BF16)]
    out_specs = [pl.BlockSpec((step_rows, D_MODEL), lambda s: (mixed(s), 0)),
                 _tiles_spec(step_rows, prepared, PACKED_CHUNKS),
                 pl.BlockSpec((step_rows, LANES), lambda s: (prepared(s), 0)),
                 pl.BlockSpec((step_rows, LANES), lambda s: (prepared(s), 0))]
    if emit_kv:
        out_shape += [jax.ShapeDtypeStruct((T * N_KV_HEADS, HEAD_DIM), F32)] * 2
        out_specs += [pl.BlockSpec((TM * N_KV_HEADS, HEAD_DIM), lambda s: (group(s), 0))] * 2
    out_shape += [jax.ShapeDtypeStruct(shp, BF16) for shp in cast_out_shapes]
    out_specs += [per_group(shp) for shp in cast_out_shapes]

    scratch = [
        pltpu.VMEM((N_HEADS, TM, HEAD_DIM), BF16),
        pltpu.VMEM((S, Lk, KV_W), BF16),
        pltpu.VMEM((S, Lk, KV_W), BF16),
        pltpu.VMEM((S, L + 2 * POOL_HALO, POOL_W), F32),
        pltpu.VMEM((TM, D_MODEL), BF16),
        pltpu.VMEM((blocks_per_step, ROW_BLOCK, ATTN_W), BF16),
        pltpu.VMEM((2, step_rows, D_MODEL), F32),
        pltpu.VMEM((2, 2, D_MODEL), F32),
    ]
    kern = functools.partial(_mix_kernel, S=S, L=L, P=P, use_rope=use_rope, emit_kv=emit_kv,
                             n_cast=tuple(len(ws) for ws in cast), n_blocks=n_blocks, U=blocks_per_step)
    return pl.pallas_call(
        kern,
        grid=(n_mix_steps + 1,),
        in_specs=in_specs,
        out_specs=out_specs,
        out_shape=out_shape,
        scratch_shapes=scratch,
        compiler_params=pltpu.CompilerParams(
            dimension_semantics=("arbitrary",), vmem_limit_bytes=V7X_VMEM_LIMIT_BYTES),
        name="mixer_rope" if use_rope else "mixer_ctx",
    )(*args)


def _plan_kernel(oh_ref, dest_ref, meta_ref, *, n_blocks):
    TB = TOKEN_BLOCK
    lane = lax.broadcasted_iota(I32, (SUBLANES, LANES), 1)

    def count(b, acc):
        oh = oh_ref[pl.ds(pl.multiple_of(b * TB, TB), TB), :].astype(F32)
        return acc + jnp.sum(oh, axis=0, keepdims=True)

    counts = lax.fori_loop(0, n_blocks, count, jnp.zeros((SUBLANES, LANES), F32))
    padded = jnp.ceil(counts * (1.0 / SORT_TILE)) * SORT_TILE
    ends = padded
    step = 1
    while step < LANES:
        ends = ends + jnp.where(lane >= step, pltpu.roll(ends, step, 1), 0.0)
        step *= 2
    starts = ends - padded

    tri = jnp.where(lax.broadcasted_iota(I32, (TB, TB), 1) < lax.broadcasted_iota(I32, (TB, TB), 0),
                    1.0, 0.0).astype(BF16)

    def place(b, seen):
        oh = oh_ref[pl.ds(pl.multiple_of(b * TB, TB), TB), :]
        ohf = oh.astype(F32)
        rank = jnp.dot(tri, oh, preferred_element_type=F32)
        base = (starts + seen)[0:1, :]
        d = jnp.sum(ohf * (rank + base), axis=1, keepdims=True)
        dest_ref[b] = _row(d).astype(I32)
        return seen + jnp.sum(ohf, axis=0, keepdims=True)

    lax.fori_loop(0, n_blocks, place, jnp.zeros((SUBLANES, LANES), F32))

    tile_row0 = lax.broadcasted_iota(I32, (LANES, LANES), 0).astype(F32) * SORT_TILE
    is_bucket = lax.broadcasted_iota(I32, (LANES, LANES), 1) < N_BUCKETS
    done = jnp.sum(jnp.where(is_bucket, jnp.where(ends[0:1, :] <= tile_row0, 1.0, 0.0), 0.0),
                   axis=1, keepdims=True)
    bkt = jnp.minimum(done, N_BUCKETS - 1.0)
    grp = (jnp.where(bkt >= PAIRS_PER_GROUP, 1.0, 0.0) + jnp.where(bkt >= 2 * PAIRS_PER_GROUP, 1.0, 0.0)
           + jnp.where(bkt >= 3 * PAIRS_PER_GROUP, 1.0, 0.0))
    pair = bkt - PAIRS_PER_GROUP * grp
    a = jnp.where(pair >= 3.0, 1.0, 0.0) + jnp.where(pair >= 5.0, 1.0, 0.0)
    b = pair - a * (7.0 - a) * 0.5 + a + 1.0
    e1 = EXP_PER_GROUP * grp + a
    e2 = EXP_PER_GROUP * grp + b
    meta = jnp.concatenate(
        [_row(e1), _row(e2), ends[0:1, :] * (1.0 / SORT_TILE), jnp.zeros((SUBLANES - 3, LANES), F32)], axis=0)
    meta_ref[...] = meta.astype(I32)


def _plan(onehot):
    T = onehot.shape[0]
    n_blocks = T // TOKEN_BLOCK
    dest, meta = pl.pallas_call(
        functools.partial(_plan_kernel, n_blocks=n_blocks),
        out_shape=[jax.ShapeDtypeStruct((n_blocks, 1, TOKEN_BLOCK), I32),
                   jax.ShapeDtypeStruct((SUBLANES, LANES), I32)],
        name="moe_plan",
    )(onehot)
    return dest.reshape(T), meta


def _sc_move_rows(src_v, table_hbm, out_hbm, lo, n_rows, idx_v, pieces_v, sem):
    chunks = pieces_v.shape[0] // SC_ROWS_PER_STEP
    lane = lax.iota(I32, SC_LANES)
    row_in_group = lane & (SUBLANES - 1)
    chunk_in_pair = lane >> 3
    rows_per_gather = SC_PIECES_PER_GATHER // chunks

    @pl.loop(0, n_rows // SC_ROWS_PER_STEP)
    def _(step):
        copies = []
        for g in range(SC_ROWS_PER_STEP // rows_per_gather):
            r0 = step * SC_ROWS_PER_STEP + g * rows_per_gather
            for v in range(SC_PIECES_PER_GATHER // SC_LANES):
                group, chunk0 = v // (chunks // 2), 2 * (v % (chunks // 2))
                tok = plsc.load_gather(src_v, [r0 + group * SUBLANES + row_in_group])
                piece = (tok >> 3) * (SUBLANES * chunks) + (chunk0 + chunk_in_pair) * SUBLANES + (tok & 7)
                idx_v[pl.ds(g * SC_PIECES_PER_GATHER + v * SC_LANES, SC_LANES)] = piece
            window = pl.ds(g * SC_PIECES_PER_GATHER, SC_PIECES_PER_GATHER)
            copies.append(pltpu.async_copy(table_hbm.at[idx_v.at[window]], pieces_v.at[window], sem))
        for cp in copies:
            cp.wait()
        first = pl.multiple_of((lo + step * SC_ROWS_PER_STEP) * chunks, SC_ROWS_PER_STEP * chunks)
        pltpu.sync_copy(pieces_v, out_hbm.at[pl.ds(first, SC_ROWS_PER_STEP * chunks)])


def _sc_scratch(chunks, dtype):
    return [pltpu.VMEM((SC_ROWS_PER_STEP * chunks,), I32), pltpu.VMEM((SC_ROWS_PER_STEP * chunks, LANES), dtype)]


def _sc_dispatch(h2_flat, gate_rows, dest, n_rows):
    T = dest.shape[0]
    per_worker = n_rows // SC_WORKERS
    rows_per_step = SC_ROWS_PER_STEP
    chunks = h2_flat.shape[0] // T
    assert n_rows % SC_WORKERS == 0 and per_worker % rows_per_step == 0 and T % SC_LANES == 0
    mesh = plsc.VectorSubcoreMesh(core_axis_name="c", subcore_axis_name="s")

    @functools.partial(
        pl.kernel, mesh=mesh,
        out_type=[jax.ShapeDtypeStruct((n_rows * chunks, LANES), h2_flat.dtype),
                  jax.ShapeDtypeStruct((n_rows, LANES), F32)],
        scratch_types=[pltpu.VMEM((T,), I32), pltpu.VMEM((per_worker,), I32)]
        + _sc_scratch(chunks, h2_flat.dtype)
        + [pltpu.VMEM((rows_per_step, LANES), F32), pltpu.SemaphoreType.DMA, pltpu.SemaphoreType.DMA],
        compiler_params=pltpu.CompilerParams(use_tc_tiling_on_sc=True, needs_layout_passes=False),
        name="sc_dispatch",
    )
    def dispatch(h2_hbm, gate_hbm, dest_hbm, out_h_hbm, out_g_hbm,
                 dest_v, src_v, idx_v, pieces_v, gates_v, sem_h, sem_g):
        worker = lax.axis_index("s") * SC_CORES + lax.axis_index("c")
        lo = worker * per_worker
        pltpu.sync_copy(dest_hbm, dest_v)

        @pl.loop(0, per_worker // SC_LANES)
        def _(j):
            j0 = pl.multiple_of(j * SC_LANES, SC_LANES)
            src_v[pl.ds(j0, SC_LANES)] = lax.rem(lo + j0 + lax.iota(I32, SC_LANES), T)

        @pl.loop(0, T // SC_LANES)
        def _(j):
            t0 = pl.multiple_of(j * SC_LANES, SC_LANES)
            d = dest_v[pl.ds(t0, SC_LANES)] - lo
            mine = (d >= 0) & (d < per_worker)
            plsc.store_scatter(src_v, [jnp.where(mine, d, 0)], t0 + lax.iota(I32, SC_LANES), mask=mine)

        @pl.loop(0, per_worker // rows_per_step)
        def _(j):
            off = pl.multiple_of(j * rows_per_step, rows_per_step)
            pltpu.async_copy(gate_hbm.at[src_v.at[pl.ds(off, rows_per_step)]], gates_v, sem_g).wait()
            pltpu.sync_copy(gates_v, out_g_hbm.at[pl.ds(lo + off, rows_per_step)])

        _sc_move_rows(src_v, h2_hbm, out_h_hbm, lo, per_worker, idx_v, pieces_v, sem_h)

    return dispatch(h2_flat, gate_rows, dest)


def _expert_kernel(meta, x_ref, gv_ref, wgu_ref, wd_ref, o_ref):
    groups = SORT_TILE // SUBLANES
    n_used = meta[2, LANES - 1]

    def one_tile(k, carry):
        t = pl.program_id(0) * EXPERT_TILES_PER_STEP + k
        rows = pl.ds(pl.multiple_of(k * groups, groups), groups)

        @pl.when(t < n_used)
        def _():
            x = _unpack_bf16_pairs(_load_tiles(x_ref.at[rows]))
            gv = gv_ref[pl.ds(pl.multiple_of(k * SORT_TILE, SORT_TILE), SORT_TILE), :]
            lane = lax.broadcasted_iota(I32, gv.shape, 1)
            out = None
            for e in (meta[0, t], meta[1, t]):
                ge = jnp.sum(jnp.where(lane == EXPERT_LANE0 + e, gv, 0.0), axis=-1, keepdims=True)
                h = jnp.dot(x, wgu_ref[e], preferred_element_type=F32)
                hg = h[:, 0:D_EXPERT]
                hid = (hg * _sigmoid(hg) * h[:, D_EXPERT:2 * D_EXPERT] * ge).astype(BF16)
                y = jnp.dot(hid, wd_ref[e], preferred_element_type=F32)
                out = y if out is None else out + y
            _store_tiles(o_ref.at[rows], _pack_bf16_pairs(out))

        @pl.when(t >= n_used)
        def _():
            o_ref[rows] = jnp.zeros((groups,) + o_ref.shape[1:], U32)

        return carry

    lax.fori_loop(0, EXPERT_TILES_PER_STEP, one_tile, 0)


def _experts(sorted_h2, sorted_gates, meta, wgu, wd):
    n_tiles = sorted_h2.shape[0] * SUBLANES // SORT_TILE
    step_rows = SORT_TILE * EXPERT_TILES_PER_STEP
    assert n_tiles % EXPERT_TILES_PER_STEP == 0

    def last_used(i, meta):
        return jnp.minimum(i, (meta[2, LANES - 1] - 1) // EXPERT_TILES_PER_STEP)

    return pl.pallas_call(
        _expert_kernel,
        grid_spec=pltpu.PrefetchScalarGridSpec(
            num_scalar_prefetch=1,
            grid=(n_tiles // EXPERT_TILES_PER_STEP,),
            in_specs=[
                _tiles_spec(step_rows, last_used, PACKED_CHUNKS),
                pl.BlockSpec((step_rows, LANES), lambda *a: (last_used(*a), 0)),
                _resident(wgu.shape), _resident(wd.shape),
            ],
            out_specs=_tiles_spec(step_rows, lambda i, *_: i, PACKED_CHUNKS),
        ),
        out_shape=jax.ShapeDtypeStruct(_tiles_shape(n_tiles * SORT_TILE, PACKED_CHUNKS), U32),
        compiler_params=pltpu.CompilerParams(
            dimension_semantics=("arbitrary",), vmem_limit_bytes=V7X_VMEM_LIMIT_BYTES),
        name="moe_experts",
    )(meta, sorted_h2, sorted_gates, wgu, wd)


def _sc_row_gather(table_flat, idx, chunks):
    n = idx.shape[0]
    per_worker = n // SC_WORKERS
    assert n % SC_WORKERS == 0 and per_worker % SC_ROWS_PER_STEP == 0
    mesh = plsc.VectorSubcoreMesh(core_axis_name="c", subcore_axis_name="s")

    @functools.partial(
        pl.kernel, mesh=mesh,
        out_type=jax.ShapeDtypeStruct((n * chunks, LANES), table_flat.dtype),
        scratch_types=[pltpu.VMEM((per_worker,), I32)] + _sc_scratch(chunks, table_flat.dtype)
        + [pltpu.SemaphoreType.DMA],
        compiler_params=pltpu.CompilerParams(use_tc_tiling_on_sc=True, needs_layout_passes=False),
        name="sc_row_gather",
    )
    def gather(table_hbm, idx_hbm, out_hbm, src_v, idx_v, pieces_v, sem):
        worker = lax.axis_index("s") * SC_CORES + lax.axis_index("c")
        lo = worker * per_worker
        pltpu.sync_copy(idx_hbm.at[pl.ds(lo, per_worker)], src_v)
        _sc_move_rows(src_v, table_hbm, out_hbm, lo, per_worker, idx_v, pieces_v, sem)

    return gather(table_flat, idx)


def _final_kernel(x_ref, moe_ref, mod_ref, gf_ref, o_ref):
    y = x_ref[...] + mod_ref[0, 5:6, :] * _unpack_bf16_pairs(_load_tiles(moe_ref)).astype(F32)
    o_ref[...] = _rms(y) * gf_ref[...]


def _final(xmid, moe_rows, mod, mod_row, gf):
    T = xmid.shape[0]
    return pl.pallas_call(
        _final_kernel,
        grid=(T // FINAL_BLOCK,),
        in_specs=[
            pl.BlockSpec((FINAL_BLOCK, D_MODEL), lambda i: (i, 0)),
            _tiles_spec(FINAL_BLOCK, lambda i: i, PACKED_CHUNKS),
            pl.BlockSpec((1, 6, D_MODEL), lambda i: (mod_row(i), 0, 0)),
            pl.BlockSpec((1, D_MODEL), lambda i: (0, 0)),
        ],
        out_specs=pl.BlockSpec((FINAL_BLOCK, D_MODEL), lambda i: (i, 0)),
        out_shape=jax.ShapeDtypeStruct((T, D_MODEL), F32),
        compiler_params=pltpu.CompilerParams(
            dimension_semantics=("arbitrary",), vmem_limit_bytes=V7X_VMEM_LIMIT_BYTES),
        name="moe_final",
    )(xmid, moe_rows, mod, gf)


def _flat(tiles):
    return tiles.reshape(-1, LANES)


def _moe_dispatch(h2_tiles, gate_rows, onehot):
    T = gate_rows.shape[0]
    n_tiles = T // SORT_TILE + N_BUCKETS
    n_rows = n_tiles * SORT_TILE
    assert n_tiles <= LANES and T % TOKEN_BLOCK == 0
    dest, meta = _plan(onehot)
    sorted_h2, sorted_gates = _sc_dispatch(_flat(h2_tiles), gate_rows, dest, n_rows)
    return sorted_h2.reshape(_tiles_shape(n_rows, PACKED_CHUNKS)), sorted_gates, dest, meta


def _moe_unpermute(moe_sorted_tiles, dest):
    chunks = moe_sorted_tiles.shape[1]
    return _sc_row_gather(_flat(moe_sorted_tiles), dest, chunks).reshape(_tiles_shape(dest.shape[0], chunks))


def _rope_tables(n_tokens):
    t = np.arange(n_tokens)
    row = (t // GRID_W).astype(np.float32)
    col = (t % GRID_W).astype(np.float32)
    freq = np.float32(ROPE_THETA) ** (-np.arange(ROPE_NF, dtype=np.float32) / np.float32(ROPE_NF))
    ang = np.concatenate([row[:, None] * freq] * 2 + [col[:, None] * freq] * 2, axis=-1)
    first = (np.arange(HEAD_DIM) % (2 * ROPE_NF)) < ROPE_NF
    sin = np.sin(ang)
    zero = np.float32(0.0)
    return (jnp.asarray(np.cos(ang)), jnp.asarray(np.where(first, -sin, zero)),
            jnp.asarray(np.where(first, zero, sin)))


def kernel(x_prompt, x_sample, cache_k, cache_v, c, c_ctx, norm1_g, norm2_g, w_ada, b_ada, w_in, q_norm_g, k_norm_g, w_pool, pool_scale, w_branch_a, w_branch_b, w_out, w_router_group, w_router_expert, w_exp_gate, w_exp_up, w_exp_down, final_norm_g):
    assert norm1_g.shape[0] == 1, "single-layer trunk"
    B, L_ctx, _ = x_prompt.shape
    Bs, L_lat, _ = x_sample.shape
    P = cache_k.shape[2]
    assert 1 + Bs <= COND_ROWS

    cond = jnp.concatenate([c_ctx[None, :], c, jnp.zeros((COND_ROWS - 1 - Bs, D_MODEL), F32)], axis=0)
    wpool_bd = jax.scipy.linalg.block_diag(*[w_pool[0, g] for g in range(len(POOL_WINDOWS))])
    mod, w_in_b, wpool_b, wa_b, wb_b, wo_b = _ada(
        cond, w_ada[0], b_ada[0][None, :],
        cast=(w_in[0], wpool_bd, w_branch_a[0], w_branch_b[0], w_out[0]))
    mod = mod.reshape(COND_ROWS, 6, D_MODEL)

    wr = jnp.concatenate([w_router_group[0], w_router_expert[0],
                          jnp.zeros((D_MODEL, LANES - N_EXP_GROUPS - N_EXPERTS), F32)], axis=1)
    wr_hi = wr.astype(BF16)
    wr_lo = (wr - wr_hi.astype(F32)).astype(BF16)
    mix_w = (norm1_g[0][None, :], w_in_b, q_norm_g[0][None, :], k_norm_g[0][None, :],
             wpool_b, pool_scale[0][None, :], wa_b, wb_b, wo_b,
             norm2_g[0][None, :], jnp.concatenate([wr_hi, wr_lo], axis=1))
    gf = final_norm_g[None, :]

    xp2 = x_prompt.reshape(B * L_ctx, D_MODEL)
    xmid_p, h2_p, gate_p, oh_p, knew, vnew, wgu, wd = _mix(
        xp2, mod, lambda i: 0, None, None, mix_w, S=2, L=L_ctx, emit_kv=True, blocks_per_step=2,
        cast=((w_exp_gate[0], w_exp_up[0]), (w_exp_down[0],)))
    sh_p, sg_p, dest_p, meta_p = _moe_dispatch(h2_p, gate_p, oh_p)

    xs2 = x_sample.reshape(Bs * L_lat, D_MODEL)
    cache = (cache_k.reshape(Bs * P * N_KV_HEADS, HEAD_DIM), cache_v.reshape(Bs * P * N_KV_HEADS, HEAD_DIM))
    xmid_s, h2_s, gate_s, oh_s = _mix(xs2, mod, lambda i: 1 + i, cache, _rope_tables(L_lat), mix_w,
                                      S=1, L=L_lat, emit_kv=False, blocks_per_step=2)
    sh_s, sg_s, dest_s, meta_s = _moe_dispatch(h2_s, gate_s, oh_s)

    moe_p = _moe_unpermute(_experts(sh_p, sg_p, meta_p, wgu, wd), dest_p)
    moe_s = _moe_unpermute(_experts(sh_s, sg_s, meta_s, wgu, wd), dest_s)
    y_prompt = _final(xmid_p, moe_p, mod, lambda i: 0, gf)
    blocks_per_seq = L_lat // FINAL_BLOCK
    y_sample = _final(xmid_s, moe_s, mod, lambda i: 1 + i // blocks_per_seq, gf)

    return (y_prompt.reshape(B, L_ctx, D_MODEL), y_sample.reshape(Bs, L_lat, D_MODEL),
            knew.reshape(B, 1, L_ctx, N_KV_HEADS, HEAD_DIM), vnew.reshape(B, 1, L_ctx, N_KV_HEADS, HEAD_DIM))
```

```python
import functools

import numpy as np
import jax
import jax.numpy as jnp
from jax import lax
from jax.experimental import pallas as pl
from jax.experimental.pallas import tpu as pltpu
from jax.experimental.pallas import tpu_sc as plsc

F32 = jnp.float32
BF16 = jnp.bfloat16
I32 = jnp.int32
U32 = jnp.uint32

D_MODEL = 1024
HEAD_DIM = 128
N_HEADS = 8
N_KV_HEADS = 2
GROUP = N_HEADS // N_KV_HEADS
ATTN_W = N_HEADS * HEAD_DIM
KV_W = N_KV_HEADS * HEAD_DIM
POOL_WINDOWS = (2, 4, 8, 16)
POOL_GC = 128
POOL_W = POOL_GC * len(POOL_WINDOWS)
IN_W = ATTN_W + 2 * KV_W + POOL_W + 2 * D_MODEL
GATE_COL = ATTN_W + 2 * KV_W + POOL_W
GRID_W = 64
ROPE_THETA = 10000.0
ROPE_NF = HEAD_DIM // 4
N_EXP_GROUPS = 4
EXP_PER_GROUP = 4
N_EXPERTS = 16
D_EXPERT = 256
EPS = 1e-6
LOG2_E = 1.4426950408889634

LANES = 128
SUBLANES = 8
COND_ROWS = SUBLANES
POOL_HALO = 8
ROW_BLOCK = 256
ADA_COLS = 768
EXPERT_LANE0 = N_EXP_GROUPS
PAIRS_PER_GROUP = EXP_PER_GROUP * (EXP_PER_GROUP - 1) // 2
N_BUCKETS = N_EXP_GROUPS * PAIRS_PER_GROUP
EXPERT_STEP_ROWS = 1536
TOKEN_BLOCK = 1024
FINAL_BLOCK = 1024
ROW_CHUNKS = D_MODEL // LANES
SC_CORES = 2
SC_SUBCORES = 16
SC_WORKERS = SC_CORES * SC_SUBCORES
SC_LANES = 16
SC_PIECES_PER_GATHER = 128
SC_ROWS_PER_STEP = 64
PACKED_CHUNKS = ROW_CHUNKS // 2
V7X_VMEM_LIMIT_BYTES = 56 * 1024 * 1024


def _sigmoid(x):
    return 1.0 / (1.0 + jnp.exp(-x))


def _rms(x):
    return x * lax.rsqrt(jnp.mean(x * x, axis=-1, keepdims=True) + EPS)


def _resident(shape):
    zeros = (0,) * len(shape)
    return pl.BlockSpec(shape, lambda i, *_: zeros, pipeline_mode=pl.Buffered(1))


def _tiles_shape(n, chunks=ROW_CHUNKS):
    return (n // SUBLANES, chunks, SUBLANES, LANES)


def _tiles_spec(n, block_index, chunks=ROW_CHUNKS):
    return pl.BlockSpec(_tiles_shape(n, chunks), lambda *a: (block_index(*a), 0, 0, 0))


def _store_tiles(ref, x):
    for c in range(ref.shape[1]):
        ref[:, c, :, :] = x[:, c * LANES:(c + 1) * LANES].reshape(x.shape[0] // SUBLANES, SUBLANES, LANES)


def _load_tiles(ref):
    n = ref.shape[0] * SUBLANES
    return jnp.concatenate([ref[:, c, :, :].reshape(n, LANES) for c in range(ref.shape[1])], axis=1)


def _pack_bf16_pairs(x):
    bits = pltpu.bitcast(x.astype(BF16).astype(F32), U32)
    w = x.shape[1] // 2
    return bits[:, :w] | (bits[:, w:] >> 16)


def _unpack_bf16_pairs(words):
    hi = pltpu.bitcast(words & jnp.uint32(0xFFFF0000), F32).astype(BF16)
    lo = pltpu.bitcast(words << 16, F32).astype(BF16)
    return jnp.concatenate([hi, lo], axis=1)


def _row(x):
    return jnp.transpose(jnp.broadcast_to(x, (x.shape[0], LANES)))[0:1, :]


def _ada_kernel(c_ref, w_ref, b_ref, *refs):
    n_cast = (len(refs) - 1) // 2
    c = c_ref[...]
    s = (c * _sigmoid(c)).astype(BF16)
    refs[n_cast][...] = jnp.dot(s, w_ref[...].astype(BF16), preferred_element_type=F32) + b_ref[...]
    for src, dst in zip(refs[:n_cast], refs[n_cast + 1:]):
        dst[...] = src[...].astype(BF16)


def _ada(cond, w_ada, b_ada, cast=()):
    n = w_ada.shape[1]
    n_steps = n // ADA_COLS
    cast_specs = []
    for w in cast:
        assert w.ndim == 2 and w.shape[0] % (n_steps * 2 * SUBLANES) == 0
        cast_specs.append(pl.BlockSpec((w.shape[0] // n_steps, w.shape[1]), lambda j: (j, 0)))
    return pl.pallas_call(
        _ada_kernel,
        grid=(n_steps,),
        in_specs=[
            pl.BlockSpec((COND_ROWS, D_MODEL), lambda j: (0, 0)),
            pl.BlockSpec((D_MODEL, ADA_COLS), lambda j: (0, j)),
            pl.BlockSpec((1, ADA_COLS), lambda j: (0, j)),
        ] + cast_specs,
        out_specs=[pl.BlockSpec((COND_ROWS, ADA_COLS), lambda j: (0, j))] + cast_specs,
        out_shape=[jax.ShapeDtypeStruct((COND_ROWS, n), F32)] + [jax.ShapeDtypeStruct(w.shape, BF16) for w in cast],
        name="ada_mod",
    )(cond, w_ada, b_ada, *cast)


def _route(logits):
    lane = lax.broadcasted_iota(I32, logits.shape, 1).astype(F32)
    neg = jnp.float32(-1e30)
    far = jnp.float32(LANES)
    is_g = lane < N_EXP_GROUPS
    gl = jnp.where(is_g, logits, neg)
    gmax = jnp.max(gl, axis=-1, keepdims=True)
    gsel = jnp.min(jnp.where(gl == gmax, lane, far), axis=-1, keepdims=True)
    psel = 1.0 / jnp.sum(jnp.where(is_g, jnp.exp(gl - gmax), 0.0), axis=-1, keepdims=True)
    e_lo = EXPERT_LANE0 + EXP_PER_GROUP * gsel
    el = jnp.where(lane >= e_lo, jnp.where(lane < e_lo + EXP_PER_GROUP, logits, neg), neg)
    v1 = jnp.max(el, axis=-1, keepdims=True)
    i1 = jnp.min(jnp.where(el == v1, lane, far), axis=-1, keepdims=True)
    el2 = jnp.where(lane == i1, neg, el)
    v2 = jnp.max(el2, axis=-1, keepdims=True)
    i2 = jnp.min(jnp.where(el2 == v2, jnp.where(lane == i1, far, lane), far), axis=-1, keepdims=True)
    e2 = jnp.exp(v2 - v1)
    w1 = psel / (1.0 + e2)
    w2 = psel * e2 / (1.0 + e2)
    gate = jnp.where(lane == i1, w1, jnp.where(lane == i2, w2, 0.0))
    a = jnp.minimum(i1, i2) - e_lo
    b = jnp.maximum(i1, i2) - e_lo
    pair = a * (7.0 - a) * 0.5 + (b - a - 1.0)
    return gate, gsel * PAIRS_PER_GROUP + pair


def _mix_kernel(*refs, S, L, P, use_rope, emit_kv, n_cast, n_blocks, U):
    it = iter(refs)
    x_ref = next(it)
    mod_ref = next(it)
    if P:
        ck_ref = next(it)
        cv_ref = next(it)
    if use_rope:
        cos_ref = next(it)
        sneg_ref = next(it)
        spos_ref = next(it)
    (g1_ref, win_ref, qg_ref, kg_ref, wpool_ref, pscale_ref, wa_ref, wb_ref, wo_ref,
     g2_ref, wr_ref) = (next(it) for _ in range(11))
    cast_in = [[next(it) for _ in range(n)] for n in n_cast]
    xmid_ref = next(it)
    h2_ref = next(it)
    gate_ref = next(it)
    oh_ref = next(it)
    if emit_kv:
        knew_ref = next(it)
        vnew_ref = next(it)
    cast_out = [next(it) for _ in n_cast]
    q_s, k_s, v_s, xp_s, h_s, attn_s, xm_s, mod2_s = (next(it) for _ in range(8))

    TM = S * L
    RB = ROW_BLOCK
    nrb = TM // RB
    n_steps = n_blocks // U
    score_gain = HEAD_DIM ** -0.5 * LOG2_E
    step = pl.program_id(0)
    block0 = U * jnp.minimum(step, n_steps - 1)
    slot = step % 2

    sh1 = mod_ref[0, 0:1, :]
    gain1 = g1_ref[...] * (1.0 + mod_ref[0, 1:2, :])
    gt1 = mod_ref[0, 2:3, :]
    sh2 = mod_ref[0, 3:4, :]
    gain2 = g2_ref[...] * (1.0 + mod_ref[0, 4:5, :])
    qg = qg_ref[...] * score_gain
    kg = kg_ref[...]

    def project(r, carry):
        r0 = pl.multiple_of(r * RB, RB)
        s = r0 // L
        o = pl.multiple_of(r0 % L, RB)
        hb = (_rms(x_ref[pl.ds(r0, RB), :]) * gain1 + sh1).astype(BF16)
        h_s[pl.ds(r0, RB), :] = hb
        p1 = jnp.dot(hb, win_ref[:, 0:GATE_COL], preferred_element_type=F32)
        if use_rope:
            cs = cos_ref[pl.ds(o, RB), :]
            sn = sneg_ref[pl.ds(o, RB), :]
            sp = spos_ref[pl.ds(o, RB), :]

        def rope(t):
            return (t * cs + pltpu.roll(t, HEAD_DIM - ROPE_NF, 1) * sn + pltpu.roll(t, ROPE_NF, 1) * sp)

        for hd in range(N_HEADS):
            qh = _rms(p1[:, hd * HEAD_DIM:(hd + 1) * HEAD_DIM]) * qg
            if use_rope:
                qh = rope(qh)
            q_s[hd, pl.ds(r0, RB), :] = qh.astype(BF16)
        for kh in range(N_KV_HEADS):
            c0 = ATTN_W + kh * HEAD_DIM
            kk = _rms(p1[:, c0:c0 + HEAD_DIM]) * kg
            if emit_kv:
                knew_ref[pl.ds(N_KV_HEADS * r0 + kh, RB, stride=N_KV_HEADS), :] = kk
            if use_rope:
                kk = rope(kk)
            k_s[s, pl.ds(P + o, RB), kh * HEAD_DIM:(kh + 1) * HEAD_DIM] = kk.astype(BF16)
        vv = p1[:, ATTN_W + KV_W:ATTN_W + 2 * KV_W]
        if emit_kv:
            for kh in range(N_KV_HEADS):
                vnew_ref[pl.ds(N_KV_HEADS * r0 + kh, RB, stride=N_KV_HEADS), :] = (
                    vv[:, kh * HEAD_DIM:(kh + 1) * HEAD_DIM])
        v_s[s, pl.ds(P + o, RB), :] = vv.astype(BF16)
        xp_s[s, pl.ds(POOL_HALO + o, RB), :] = p1[:, ATTN_W + 2 * KV_W:GATE_COL]
        return carry

    @pl.when(step == 0)
    def _():
        xm_s[1] = jnp.zeros((U * RB, D_MODEL), F32)
        mod2_s[1] = jnp.zeros((2, D_MODEL), F32)

    @pl.when((step < n_steps) & (step % (nrb // U) == 0))
    def _():
        if P:
            for kh in range(N_KV_HEADS):
                cols = slice(kh * HEAD_DIM, (kh + 1) * HEAD_DIM)
                k_s[0, 0:P, cols] = ck_ref[pl.ds(kh, P, stride=N_KV_HEADS), :].astype(BF16)
                v_s[0, 0:P, cols] = cv_ref[pl.ds(kh, P, stride=N_KV_HEADS), :].astype(BF16)
        xp_s[:, 0:POOL_HALO, :] = jnp.zeros((S, POOL_HALO, POOL_W), F32)
        xp_s[:, L + POOL_HALO:L + 2 * POOL_HALO, :] = jnp.zeros((S, POOL_HALO, POOL_W), F32)
        lax.fori_loop(0, TM // RB, project, 0)
        for srcs, dst in zip(cast_in, cast_out):
            col = 0
            for src in srcs:
                dst[..., col:col + src.shape[-1]] = src[...].astype(BF16)
                col += src.shape[-1]

    def mix(u):
        r0 = pl.multiple_of(((block0 + u) % nrb) * RB, RB)
        s = r0 // L
        o = pl.multiple_of(r0 % L, RB)
        attn_u = attn_s.at[u]
        rows = slice(u * RB, (u + 1) * RB)

        for hd in range(N_HEADS):
            kh = hd // GROUP
            k = k_s[s, :, kh * HEAD_DIM:(kh + 1) * HEAD_DIM]
            v = v_s[s, :, kh * HEAD_DIM:(kh + 1) * HEAD_DIM]
            qh = q_s[hd, pl.ds(r0, RB), :]
            sc = lax.dot_general(qh, k, (((1,), (1,)), ((), ())), preferred_element_type=F32)
            e = jnp.exp2(sc - jnp.max(sc, axis=-1, keepdims=True))
            den = jnp.sum(e, axis=-1, keepdims=True)
            oh = jnp.dot(e.astype(BF16), v, preferred_element_type=F32) / den
            attn_u[:, hd * HEAD_DIM:(hd + 1) * HEAD_DIM] = oh.astype(BF16)
        a = jnp.dot(attn_u[...], wa_ref[...], preferred_element_type=F32)

        t = o + lax.broadcasted_iota(I32, (RB, 1), 0)
        RW = RB + 2 * POOL_HALO
        parts = []
        for gi, w in enumerate(POOL_WINDOWS):
            cols = slice(gi * POOL_GC, (gi + 1) * POOL_GC)
            xw = xp_s[s, pl.ds(o, RW), cols]
            run = xw
            span = 1
            while span < w:
                run = run + pltpu.roll(run, span, 0)
                span *= 2
            if w // 2 > 1:
                run = pltpu.roll(run, RW - (w // 2 - 1), 0)
            tot = run[POOL_HALO:POOL_HALO + RB]
            cnt = (jnp.minimum(t + w // 2, L) - jnp.maximum(t - w // 2, 0)).astype(F32)
            parts.append(tot / cnt - xw[POOL_HALO:POOL_HALO + RB])
        dpool = jnp.concatenate(parts, axis=1).astype(BF16)
        pooled = jnp.dot(dpool, wpool_ref[...], preferred_element_type=F32) * pscale_ref[...]
        b = jnp.dot(pooled.astype(BF16), wb_ref[...], preferred_element_type=F32)

        gates = jnp.dot(h_s[pl.ds(r0, RB), :], win_ref[:, GATE_COL:IN_W], preferred_element_type=F32)
        merged = _sigmoid(gates[:, 0:D_MODEL]) * a + _sigmoid(gates[:, D_MODEL:2 * D_MODEL]) * b
        upd = jnp.dot(merged.astype(BF16), wo_ref[...], preferred_element_type=F32)
        xm = x_ref[pl.ds(r0, RB), :] + gt1 * upd
        xmid_ref[rows, :] = xm
        xm_s[slot, rows, :] = xm

    def moe_prep(u):
        rows = slice(u * RB, (u + 1) * RB)
        h2 = _rms(xm_s[1 - slot, rows, :]) * mod2_s[1 - slot, 0:1, :] + mod2_s[1 - slot, 1:2, :]
        hi = h2.astype(BF16)
        lo = (h2 - hi.astype(F32)).astype(BF16)
        l1 = jnp.dot(hi, wr_ref[...], preferred_element_type=F32)
        l2 = jnp.dot(lo, wr_ref[:, 0:LANES], preferred_element_type=F32)
        gate, bucket = _route(l1[:, 0:LANES] + l1[:, LANES:2 * LANES] + l2)
        groups = pl.ds(u * (RB // SUBLANES), RB // SUBLANES)
        _store_tiles(h2_ref.at[groups], _pack_bf16_pairs(h2))
        gate_ref[rows, :] = gate
        lane = lax.broadcasted_iota(I32, (RB, LANES), 1).astype(F32)
        oh_ref[rows, :] = jnp.where(lane == bucket, 1.0, 0.0).astype(BF16)

    mod2_s[slot, 0:1, :] = gain2
    mod2_s[slot, 1:2, :] = sh2
    for u in range(U):
        moe_prep(u)
    for u in range(U):
        mix(u)


def _mix(x2d, mod, mod_row, cache, rope_tabs, weights, *, S, L, emit_kv, blocks_per_step, cast=()):
    T = x2d.shape[0]
    TM = S * L
    P = cache[0].shape[0] // (T // L * N_KV_HEADS) if cache is not None else 0
    use_rope = rope_tabs is not None
    assert T % TM == 0 and L % ROW_BLOCK == 0
    assert not (use_rope or P) or S == 1
    Lk = P + L

    args = [x2d, mod]
    nrb = TM // ROW_BLOCK
    n_blocks = T // ROW_BLOCK
    step_rows = blocks_per_step * ROW_BLOCK
    steps_per_group = nrb // blocks_per_step
    n_mix_steps = n_blocks // blocks_per_step
    assert nrb % blocks_per_step == 0

    def mixed(s):
        return jnp.minimum(s, n_mix_steps - 1)

    def group(s):
        return mixed(s) // steps_per_group

    def prepared(s):
        return jnp.maximum(s - 1, 0)

    in_specs = [
        pl.BlockSpec((TM, D_MODEL), lambda s: (group(s), 0)),
        pl.BlockSpec((1, 6, D_MODEL), lambda s: (mod_row(group(s)), 0, 0)),
    ]
    if P:
        args += list(cache)
        in_specs += [pl.BlockSpec((P * N_KV_HEADS, HEAD_DIM), lambda s: (group(s), 0))] * 2
    if use_rope:
        args += list(rope_tabs)
        in_specs += [_resident((L, HEAD_DIM))] * 3
    args += list(weights)
    in_specs += [_resident(w.shape) for w in weights]
    n_steps = T // TM
    def per_group(shape):
        assert shape[0] % n_steps == 0
        blk = (shape[0] // n_steps,) + shape[1:]
        return pl.BlockSpec(blk, lambda s, n=len(blk): (group(s),) + (0,) * (n - 1))

    cast_out_shapes = [ws[0].shape[:-1] + (sum(w.shape[-1] for w in ws),) for ws in cast]
    for ws in cast:
        args += list(ws)
        in_specs += [per_group(w.shape) for w in ws]

    out_shape = [jax.ShapeDtypeStruct((T, D_MODEL), F32), jax.ShapeDtypeStruct(_tiles_shape(T, PACKED_CHUNKS), U32),
                 jax.ShapeDtypeStruct((T, LANES), F32),
                 jax.ShapeDtypeStruct((T, LANES), BF16)]
    out_specs = [pl.BlockSpec((step_rows, D_MODEL), lambda s: (mixed(s), 0)),
                 _tiles_spec(step_rows, prepared, PACKED_CHUNKS),
                 pl.BlockSpec((step_rows, LANES), lambda s: (prepared(s), 0)),
                 pl.BlockSpec((step_rows, LANES), lambda s: (prepared(s), 0))]
    if emit_kv:
        out_shape += [jax.ShapeDtypeStruct((T * N_KV_HEADS, HEAD_DIM), F32)] * 2
        out_specs += [pl.BlockSpec((TM * N_KV_HEADS, HEAD_DIM), lambda s: (group(s), 0))] * 2
    out_shape += [jax.ShapeDtypeStruct(shp, BF16) for shp in cast_out_shapes]
    out_specs += [per_group(shp) for shp in cast_out_shapes]

    scratch = [
        pltpu.VMEM((N_HEADS, TM, HEAD_DIM), BF16),
        pltpu.VMEM((S, Lk, KV_W), BF16),
        pltpu.VMEM((S, Lk, KV_W), BF16),
        pltpu.VMEM((S, L + 2 * POOL_HALO, POOL_W), F32),
        pltpu.VMEM((TM, D_MODEL), BF16),
        pltpu.VMEM((blocks_per_step, ROW_BLOCK, ATTN_W), BF16),
        pltpu.VMEM((2, step_rows, D_MODEL), F32),
        pltpu.VMEM((2, 2, D_MODEL), F32),
    ]
    kern = functools.partial(_mix_kernel, S=S, L=L, P=P, use_rope=use_rope, emit_kv=emit_kv,
                             n_cast=tuple(len(ws) for ws in cast), n_blocks=n_blocks, U=blocks_per_step)
    return pl.pallas_call(
        kern,
        grid=(n_mix_steps + 1,),
        in_specs=in_specs,
        out_specs=out_specs,
        out_shape=out_shape,
        scratch_shapes=scratch,
        compiler_params=pltpu.CompilerParams(
            dimension_semantics=("arbitrary",), vmem_limit_bytes=V7X_VMEM_LIMIT_BYTES),
        name="mixer_rope" if use_rope else "mixer_ctx",
    )(*args)


def _plan_kernel(oh_ref, dest_ref, meta_ref, *, n_blocks, tile):
    TB = TOKEN_BLOCK
    lane = lax.broadcasted_iota(I32, (SUBLANES, LANES), 1)

    def count(b, acc):
        oh = oh_ref[pl.ds(pl.multiple_of(b * TB, TB), TB), :].astype(F32)
        return acc + jnp.sum(oh, axis=0, keepdims=True)

    counts = lax.fori_loop(0, n_blocks, count, jnp.zeros((SUBLANES, LANES), F32))
    padded = jnp.floor((counts + (tile - 0.5)) * (1.0 / tile)) * tile
    ends = padded
    step = 1
    while step < LANES:
        ends = ends + jnp.where(lane >= step, pltpu.roll(ends, step, 1), 0.0)
        step *= 2
    starts = ends - padded

    tri = jnp.where(lax.broadcasted_iota(I32, (TB, TB), 1) < lax.broadcasted_iota(I32, (TB, TB), 0),
                    1.0, 0.0).astype(BF16)

    def place(b, seen):
        oh = oh_ref[pl.ds(pl.multiple_of(b * TB, TB), TB), :]
        ohf = oh.astype(F32)
        rank = jnp.dot(tri, oh, preferred_element_type=F32)
        base = (starts + seen)[0:1, :]
        d = jnp.sum(ohf * (rank + base), axis=1, keepdims=True)
        dest_ref[b] = _row(d).astype(I32)
        return seen + jnp.sum(ohf, axis=0, keepdims=True)

    lax.fori_loop(0, n_blocks, place, jnp.zeros((SUBLANES, LANES), F32))

    tile_row0 = lax.broadcasted_iota(I32, (LANES, LANES), 0).astype(F32) * tile
    is_bucket = lax.broadcasted_iota(I32, (LANES, LANES), 1) < N_BUCKETS
    done = jnp.sum(jnp.where(is_bucket, jnp.where(ends[0:1, :] <= tile_row0, 1.0, 0.0), 0.0),
                   axis=1, keepdims=True)
    bkt = jnp.minimum(done, N_BUCKETS - 1.0)
    grp = (jnp.where(bkt >= PAIRS_PER_GROUP, 1.0, 0.0) + jnp.where(bkt >= 2 * PAIRS_PER_GROUP, 1.0, 0.0)
           + jnp.where(bkt >= 3 * PAIRS_PER_GROUP, 1.0, 0.0))
    pair = bkt - PAIRS_PER_GROUP * grp
    a = jnp.where(pair >= 3.0, 1.0, 0.0) + jnp.where(pair >= 5.0, 1.0, 0.0)
    b = pair - a * (7.0 - a) * 0.5 + a + 1.0
    e1 = EXP_PER_GROUP * grp + a
    e2 = EXP_PER_GROUP * grp + b
    meta = jnp.concatenate(
        [_row(e1), _row(e2), jnp.floor(ends[0:1, :] * (1.0 / tile) + 0.5),
         jnp.zeros((SUBLANES - 3, LANES), F32)], axis=0)
    meta_ref[...] = meta.astype(I32)


def _plan(onehot, tile):
    T = onehot.shape[0]
    n_blocks = T // TOKEN_BLOCK
    dest, meta = pl.pallas_call(
        functools.partial(_plan_kernel, n_blocks=n_blocks, tile=tile),
        out_shape=[jax.ShapeDtypeStruct((n_blocks, 1, TOKEN_BLOCK), I32),
                   jax.ShapeDtypeStruct((SUBLANES, LANES), I32)],
        name="moe_plan",
    )(onehot)
    return dest.reshape(T), meta


def _sc_move_rows(src_v, table_hbm, out_hbm, lo, n_rows, idx_v, pieces_v, sem):
    chunks = pieces_v.shape[0] // SC_ROWS_PER_STEP
    lane = lax.iota(I32, SC_LANES)
    row_in_group = lane & (SUBLANES - 1)
    chunk_in_pair = lane >> 3
    rows_per_gather = SC_PIECES_PER_GATHER // chunks

    @pl.loop(0, n_rows // SC_ROWS_PER_STEP)
    def _(step):
        copies = []
        for g in range(SC_ROWS_PER_STEP // rows_per_gather):
            r0 = step * SC_ROWS_PER_STEP + g * rows_per_gather
            for v in range(SC_PIECES_PER_GATHER // SC_LANES):
                group, chunk0 = v // (chunks // 2), 2 * (v % (chunks // 2))
                tok = plsc.load_gather(src_v, [r0 + group * SUBLANES + row_in_group])
                piece = (tok >> 3) * (SUBLANES * chunks) + (chunk0 + chunk_in_pair) * SUBLANES + (tok & 7)
                idx_v[pl.ds(g * SC_PIECES_PER_GATHER + v * SC_LANES, SC_LANES)] = piece
            window = pl.ds(g * SC_PIECES_PER_GATHER, SC_PIECES_PER_GATHER)
            copies.append(pltpu.async_copy(table_hbm.at[idx_v.at[window]], pieces_v.at[window], sem))
        for cp in copies:
            cp.wait()
        first = pl.multiple_of((lo + step * SC_ROWS_PER_STEP) * chunks, SC_ROWS_PER_STEP * chunks)
        pltpu.sync_copy(pieces_v, out_hbm.at[pl.ds(first, SC_ROWS_PER_STEP * chunks)])


def _sc_scratch(chunks, dtype):
    return [pltpu.VMEM((SC_ROWS_PER_STEP * chunks,), I32), pltpu.VMEM((SC_ROWS_PER_STEP * chunks, LANES), dtype)]


def _sc_dispatch(h2_flat, gate_rows, dest, n_rows):
    T = dest.shape[0]
    per_worker = n_rows // SC_WORKERS
    rows_per_step = SC_ROWS_PER_STEP
    chunks = h2_flat.shape[0] // T
    assert n_rows % SC_WORKERS == 0 and per_worker % rows_per_step == 0 and T % SC_LANES == 0
    mesh = plsc.VectorSubcoreMesh(core_axis_name="c", subcore_axis_name="s")

    @functools.partial(
        pl.kernel, mesh=mesh,
        out_type=[jax.ShapeDtypeStruct((n_rows * chunks, LANES), h2_flat.dtype),
                  jax.ShapeDtypeStruct((n_rows, LANES), F32)],
        scratch_types=[pltpu.VMEM((T,), I32), pltpu.VMEM((per_worker,), I32)]
        + _sc_scratch(chunks, h2_flat.dtype)
        + [pltpu.VMEM((rows_per_step, LANES), F32), pltpu.SemaphoreType.DMA, pltpu.SemaphoreType.DMA],
        compiler_params=pltpu.CompilerParams(use_tc_tiling_on_sc=True, needs_layout_passes=False),
        name="sc_dispatch",
    )
    def dispatch(h2_hbm, gate_hbm, dest_hbm, out_h_hbm, out_g_hbm,
                 dest_v, src_v, idx_v, pieces_v, gates_v, sem_h, sem_g):
        worker = lax.axis_index("s") * SC_CORES + lax.axis_index("c")
        lo = worker * per_worker
        pltpu.sync_copy(dest_hbm, dest_v)

        @pl.loop(0, per_worker // SC_LANES)
        def _(j):
            j0 = pl.multiple_of(j * SC_LANES, SC_LANES)
            src_v[pl.ds(j0, SC_LANES)] = lax.rem(lo + j0 + lax.iota(I32, SC_LANES), T)

        @pl.loop(0, T // SC_LANES)
        def _(j):
            t0 = pl.multiple_of(j * SC_LANES, SC_LANES)
            d = dest_v[pl.ds(t0, SC_LANES)] - lo
            mine = (d >= 0) & (d < per_worker)
            plsc.store_scatter(src_v, [jnp.where(mine, d, 0)], t0 + lax.iota(I32, SC_LANES), mask=mine)

        @pl.loop(0, per_worker // rows_per_step)
        def _(j):
            off = pl.multiple_of(j * rows_per_step, rows_per_step)
            pltpu.async_copy(gate_hbm.at[src_v.at[pl.ds(off, rows_per_step)]], gates_v, sem_g).wait()
            pltpu.sync_copy(gates_v, out_g_hbm.at[pl.ds(lo + off, rows_per_step)])

        _sc_move_rows(src_v, h2_hbm, out_h_hbm, lo, per_worker, idx_v, pieces_v, sem_h)

    return dispatch(h2_flat, gate_rows, dest)


def _expert_kernel(meta, x_ref, gv_ref, wgu_ref, wd_ref, o_ref, *, tile, per_step):
    groups = tile // SUBLANES
    n_used = meta[2, LANES - 1]

    def one_tile(k, carry):
        t = pl.program_id(0) * per_step + k
        rows = pl.ds(pl.multiple_of(k * groups, groups), groups)

        @pl.when(t < n_used)
        def _():
            x = _unpack_bf16_pairs(_load_tiles(x_ref.at[rows]))
            gv = gv_ref[pl.ds(pl.multiple_of(k * tile, SUBLANES), tile), :]
            lane = lax.broadcasted_iota(I32, gv.shape, 1)
            out = None
            for e in (meta[0, t], meta[1, t]):
                ge = jnp.sum(jnp.where(lane == EXPERT_LANE0 + e, gv, 0.0), axis=-1, keepdims=True)
                h = jnp.dot(x, wgu_ref[e], preferred_element_type=F32)
                hg = h[:, 0:D_EXPERT]
                hid = (hg * _sigmoid(hg) * h[:, D_EXPERT:2 * D_EXPERT] * ge).astype(BF16)
                y = jnp.dot(hid, wd_ref[e], preferred_element_type=F32)
                out = y if out is None else out + y
            _store_tiles(o_ref.at[rows], _pack_bf16_pairs(out))

        @pl.when(t >= n_used)
        def _():
            o_ref[rows] = jnp.zeros((groups,) + o_ref.shape[1:], U32)

        return carry

    lax.fori_loop(0, per_step, one_tile, 0)


def _experts(sorted_h2, sorted_gates, meta, wgu, wd, tiling):
    tile, per_step, n_tiles = tiling
    step_rows = tile * per_step
    assert n_tiles * tile == sorted_h2.shape[0] * SUBLANES and n_tiles % per_step == 0

    def last_used(i, meta):
        return jnp.minimum(i, (meta[2, LANES - 1] - 1) // per_step)

    return pl.pallas_call(
        functools.partial(_expert_kernel, tile=tile, per_step=per_step),
        grid_spec=pltpu.PrefetchScalarGridSpec(
            num_scalar_prefetch=1,
            grid=(n_tiles // per_step,),
            in_specs=[
                _tiles_spec(step_rows, last_used, PACKED_CHUNKS),
                pl.BlockSpec((step_rows, LANES), lambda *a: (last_used(*a), 0)),
                _resident(wgu.shape), _resident(wd.shape),
            ],
            out_specs=_tiles_spec(step_rows, lambda i, *_: i, PACKED_CHUNKS),
        ),
        out_shape=jax.ShapeDtypeStruct(_tiles_shape(n_tiles * tile, PACKED_CHUNKS), U32),
        compiler_params=pltpu.CompilerParams(
            dimension_semantics=("arbitrary",), vmem_limit_bytes=V7X_VMEM_LIMIT_BYTES),
        name="moe_experts",
    )(meta, sorted_h2, sorted_gates, wgu, wd)


def _sc_row_gather(table_flat, idx, chunks):
    n = idx.shape[0]
    per_worker = n // SC_WORKERS
    assert n % SC_WORKERS == 0 and per_worker % SC_ROWS_PER_STEP == 0
    mesh = plsc.VectorSubcoreMesh(core_axis_name="c", subcore_axis_name="s")

    @functools.partial(
        pl.kernel, mesh=mesh,
        out_type=jax.ShapeDtypeStruct((n * chunks, LANES), table_flat.dtype),
        scratch_types=[pltpu.VMEM((per_worker,), I32)] + _sc_scratch(chunks, table_flat.dtype)
        + [pltpu.SemaphoreType.DMA],
        compiler_params=pltpu.CompilerParams(use_tc_tiling_on_sc=True, needs_layout_passes=False),
        name="sc_row_gather",
    )
    def gather(table_hbm, idx_hbm, out_hbm, src_v, idx_v, pieces_v, sem):
        worker = lax.axis_index("s") * SC_CORES + lax.axis_index("c")
        lo = worker * per_worker
        pltpu.sync_copy(idx_hbm.at[pl.ds(lo, per_worker)], src_v)
        _sc_move_rows(src_v, table_hbm, out_hbm, lo, per_worker, idx_v, pieces_v, sem)

    return gather(table_flat, idx)


def _final_kernel(x_ref, moe_ref, mod_ref, gf_ref, o_ref):
    y = x_ref[...] + mod_ref[0, 5:6, :] * _unpack_bf16_pairs(_load_tiles(moe_ref)).astype(F32)
    o_ref[...] = _rms(y) * gf_ref[...]


def _final(xmid, moe_rows, mod, mod_row, gf):
    T = xmid.shape[0]
    return pl.pallas_call(
        _final_kernel,
        grid=(T // FINAL_BLOCK,),
        in_specs=[
            pl.BlockSpec((FINAL_BLOCK, D_MODEL), lambda i: (i, 0)),
            _tiles_spec(FINAL_BLOCK, lambda i: i, PACKED_CHUNKS),
            pl.BlockSpec((1, 6, D_MODEL), lambda i: (mod_row(i), 0, 0)),
            pl.BlockSpec((1, D_MODEL), lambda i: (0, 0)),
        ],
        out_specs=pl.BlockSpec((FINAL_BLOCK, D_MODEL), lambda i: (i, 0)),
        out_shape=jax.ShapeDtypeStruct((T, D_MODEL), F32),
        compiler_params=pltpu.CompilerParams(
            dimension_semantics=("arbitrary",), vmem_limit_bytes=V7X_VMEM_LIMIT_BYTES),
        name="moe_final",
    )(xmid, moe_rows, mod, gf)


def _flat(tiles):
    return tiles.reshape(-1, LANES)


def _expert_tiling(T):
    tile = -(-(T * 9) // (8 * N_BUCKETS * 64)) * 64
    per_step = max(1, EXPERT_STEP_ROWS // tile)
    n_tiles = (T + N_BUCKETS * (tile - 1)) // tile
    while n_tiles % per_step or (n_tiles * tile) % (SC_WORKERS * SC_ROWS_PER_STEP):
        n_tiles += 1
    return tile, per_step, n_tiles


def _moe_dispatch(h2_tiles, gate_rows, onehot, tiling):
    T = gate_rows.shape[0]
    tile, _, n_tiles = tiling
    n_rows = n_tiles * tile
    assert n_tiles <= LANES and T % TOKEN_BLOCK == 0
    dest, meta = _plan(onehot, tile)
    sorted_h2, sorted_gates = _sc_dispatch(_flat(h2_tiles), gate_rows, dest, n_rows)
    return sorted_h2.reshape(_tiles_shape(n_rows, PACKED_CHUNKS)), sorted_gates, dest, meta


def _moe_unpermute(moe_sorted_tiles, dest):
    chunks = moe_sorted_tiles.shape[1]
    return _sc_row_gather(_flat(moe_sorted_tiles), dest, chunks).reshape(_tiles_shape(dest.shape[0], chunks))


def _rope_tables(n_tokens):
    t = np.arange(n_tokens)
    row = (t // GRID_W).astype(np.float32)
    col = (t % GRID_W).astype(np.float32)
    freq = np.float32(ROPE_THETA) ** (-np.arange(ROPE_NF, dtype=np.float32) / np.float32(ROPE_NF))
    ang = np.concatenate([row[:, None] * freq] * 2 + [col[:, None] * freq] * 2, axis=-1)
    first = (np.arange(HEAD_DIM) % (2 * ROPE_NF)) < ROPE_NF
    sin = np.sin(ang)
    zero = np.float32(0.0)
    return (jnp.asarray(np.cos(ang)), jnp.asarray(np.where(first, -sin, zero)),
            jnp.asarray(np.where(first, zero, sin)))


def kernel(x_prompt, x_sample, cache_k, cache_v, c, c_ctx, norm1_g, norm2_g, w_ada, b_ada, w_in, q_norm_g, k_norm_g, w_pool, pool_scale, w_branch_a, w_branch_b, w_out, w_router_group, w_router_expert, w_exp_gate, w_exp_up, w_exp_down, final_norm_g):
    assert norm1_g.shape[0] == 1, "single-layer trunk"
    B, L_ctx, _ = x_prompt.shape
    Bs, L_lat, _ = x_sample.shape
    P = cache_k.shape[2]
    assert 1 + Bs <= COND_ROWS

    cond = jnp.concatenate([c_ctx[None, :], c, jnp.zeros((COND_ROWS - 1 - Bs, D_MODEL), F32)], axis=0)
    wpool_bd = jax.scipy.linalg.block_diag(*[w_pool[0, g] for g in range(len(POOL_WINDOWS))])
    mod, w_in_b, wpool_b, wa_b, wb_b, wo_b = _ada(
        cond, w_ada[0], b_ada[0][None, :],
        cast=(w_in[0], wpool_bd, w_branch_a[0], w_branch_b[0], w_out[0]))
    mod = mod.reshape(COND_ROWS, 6, D_MODEL)

    wr = jnp.concatenate([w_router_group[0], w_router_expert[0],
                          jnp.zeros((D_MODEL, LANES - N_EXP_GROUPS - N_EXPERTS), F32)], axis=1)
    wr_hi = wr.astype(BF16)
    wr_lo = (wr - wr_hi.astype(F32)).astype(BF16)
    mix_w = (norm1_g[0][None, :], w_in_b, q_norm_g[0][None, :], k_norm_g[0][None, :],
             wpool_b, pool_scale[0][None, :], wa_b, wb_b, wo_b,
             norm2_g[0][None, :], jnp.concatenate([wr_hi, wr_lo], axis=1))
    gf = final_norm_g[None, :]

    xp2 = x_prompt.reshape(B * L_ctx, D_MODEL)
    xmid_p, h2_p, gate_p, oh_p, knew, vnew, wgu, wd = _mix(
        xp2, mod, lambda i: 0, None, None, mix_w, S=2, L=L_ctx, emit_kv=True, blocks_per_step=2,
        cast=((w_exp_gate[0], w_exp_up[0]), (w_exp_down[0],)))
    tiling_p = _expert_tiling(B * L_ctx)
    sh_p, sg_p, dest_p, meta_p = _moe_dispatch(h2_p, gate_p, oh_p, tiling_p)

    xs2 = x_sample.reshape(Bs * L_lat, D_MODEL)
    cache = (cache_k.reshape(Bs * P * N_KV_HEADS, HEAD_DIM), cache_v.reshape(Bs * P * N_KV_HEADS, HEAD_DIM))
    xmid_s, h2_s, gate_s, oh_s = _mix(xs2, mod, lambda i: 1 + i, cache, _rope_tables(L_lat), mix_w,
                                      S=1, L=L_lat, emit_kv=False, blocks_per_step=1)
    tiling_s = _expert_tiling(Bs * L_lat)
    sh_s, sg_s, dest_s, meta_s = _moe_dispatch(h2_s, gate_s, oh_s, tiling_s)

    moe_p = _moe_unpermute(_experts(sh_p, sg_p, meta_p, wgu, wd, tiling_p), dest_p)
    moe_s = _moe_unpermute(_experts(sh_s, sg_s, meta_s, wgu, wd, tiling_s), dest_s)
    y_prompt = _final(xmid_p, moe_p, mod, lambda i: 0, gf)
    blocks_per_seq = L_lat // FINAL_BLOCK
    y_sample = _final(xmid_s, moe_s, mod, lambda i: 1 + i // blocks_per_seq, gf)

    return (y_prompt.reshape(B, L_ctx, D_MODEL), y_sample.reshape(Bs, L_lat, D_MODEL),
            knew.reshape(B, 1, L_ctx, N_KV_HEADS, HEAD_DIM), vnew.reshape(B, 1, L_ctx, N_KV_HEADS, HEAD_DIM))
```

```python
import functools

import numpy as np
import jax
import jax.numpy as jnp
from jax import lax
from jax.experimental import pallas as pl
from jax.experimental.pallas import tpu as pltpu
from jax.experimental.pallas import tpu_sc as plsc

F32 = jnp.float32
BF16 = jnp.bfloat16
I32 = jnp.int32
U32 = jnp.uint32

D_MODEL = 1024
HEAD_DIM = 128
N_HEADS = 8
N_KV_HEADS = 2
GROUP = N_HEADS // N_KV_HEADS
ATTN_W = N_HEADS * HEAD_DIM
KV_W = N_KV_HEADS * HEAD_DIM
POOL_WINDOWS = (2, 4, 8, 16)
POOL_GC = 128
POOL_W = POOL_GC * len(POOL_WINDOWS)
IN_W = ATTN_W + 2 * KV_W + POOL_W + 2 * D_MODEL
GATE_COL = ATTN_W + 2 * KV_W + POOL_W
GRID_W = 64
ROPE_THETA = 10000.0
ROPE_NF = HEAD_DIM // 4
N_EXP_GROUPS = 4
EXP_PER_GROUP = 4
N_EXPERTS = 16
D_EXPERT = 256
EPS = 1e-6
LOG2_E = 1.4426950408889634

LANES = 128
SUBLANES = 8
COND_ROWS = SUBLANES
POOL_HALO = 8
ROW_BLOCK = 256
ADA_COLS = 768
EXPERT_LANE0 = N_EXP_GROUPS
PAIRS_PER_GROUP = EXP_PER_GROUP * (EXP_PER_GROUP - 1) // 2
N_BUCKETS = N_EXP_GROUPS * PAIRS_PER_GROUP
EXPERT_STEP_ROWS = 1536
WEIGHT_PARTS = 4
TOKEN_BLOCK = 1024
FINAL_BLOCK = 1024
ROW_CHUNKS = D_MODEL // LANES
SC_CORES = 2
SC_SUBCORES = 16
SC_WORKERS = SC_CORES * SC_SUBCORES
SC_LANES = 16
SC_PIECES_PER_GATHER = 128
SC_ROWS_PER_STEP = 64
PACKED_CHUNKS = ROW_CHUNKS // 2
V7X_VMEM_LIMIT_BYTES = 56 * 1024 * 1024


def _sigmoid(x):
    return 1.0 / (1.0 + jnp.exp(-x))


def _rms(x):
    return x * lax.rsqrt(jnp.mean(x * x, axis=-1, keepdims=True) + EPS)


def _resident(shape):
    zeros = (0,) * len(shape)
    return pl.BlockSpec(shape, lambda i, *_: zeros, pipeline_mode=pl.Buffered(1))


def _resident_parts(shape, axis, parts):
    block = tuple(n // parts if d == axis else n for d, n in enumerate(shape))

    def spec(p):
        index = tuple(p if d == axis else 0 for d in range(len(shape)))
        return pl.BlockSpec(block, lambda i, *_: index, pipeline_mode=pl.Buffered(1))

    return [spec(p) for p in range(parts)]


def _tiles_shape(n, chunks=ROW_CHUNKS):
    return (n // SUBLANES, chunks, SUBLANES, LANES)


def _tiles_spec(n, block_index, chunks=ROW_CHUNKS):
    return pl.BlockSpec(_tiles_shape(n, chunks), lambda *a: (block_index(*a), 0, 0, 0))


def _store_tiles(ref, x):
    for c in range(ref.shape[1]):
        ref[:, c, :, :] = x[:, c * LANES:(c + 1) * LANES].reshape(x.shape[0] // SUBLANES, SUBLANES, LANES)


def _load_tiles(ref):
    n = ref.shape[0] * SUBLANES
    return jnp.concatenate([ref[:, c, :, :].reshape(n, LANES) for c in range(ref.shape[1])], axis=1)


def _pack_bf16_pairs(x):
    bits = pltpu.bitcast(x.astype(BF16).astype(F32), U32)
    w = x.shape[1] // 2
    return bits[:, :w] | (bits[:, w:] >> 16)


def _unpack_bf16_pairs(words):
    hi = pltpu.bitcast(words & jnp.uint32(0xFFFF0000), F32).astype(BF16)
    lo = pltpu.bitcast(words << 16, F32).astype(BF16)
    return jnp.concatenate([hi, lo], axis=1)


def _row(x):
    return jnp.transpose(jnp.broadcast_to(x, (x.shape[0], LANES)))[0:1, :]


def _ada_kernel(c_ref, w_ref, b_ref, *refs):
    n_cast = (len(refs) - 1) // 2
    c = c_ref[...]
    s = (c * _sigmoid(c)).astype(BF16)
    refs[n_cast][...] = jnp.dot(s, w_ref[...].astype(BF16), preferred_element_type=F32) + b_ref[...]
    for src, dst in zip(refs[:n_cast], refs[n_cast + 1:]):
        dst[...] = src[...].astype(BF16)


def _ada(cond, w_ada, b_ada, cast=()):
    n = w_ada.shape[1]
    n_steps = n // ADA_COLS
    cast_specs = []
    for w in cast:
        assert w.ndim == 2 and w.shape[0] % (n_steps * 2 * SUBLANES) == 0
        cast_specs.append(pl.BlockSpec((w.shape[0] // n_steps, w.shape[1]), lambda j: (j, 0)))
    return pl.pallas_call(
        _ada_kernel,
        grid=(n_steps,),
        in_specs=[
            pl.BlockSpec((COND_ROWS, D_MODEL), lambda j: (0, 0)),
            pl.BlockSpec((D_MODEL, ADA_COLS), lambda j: (0, j)),
            pl.BlockSpec((1, ADA_COLS), lambda j: (0, j)),
        ] + cast_specs,
        out_specs=[pl.BlockSpec((COND_ROWS, ADA_COLS), lambda j: (0, j))] + cast_specs,
        out_shape=[jax.ShapeDtypeStruct((COND_ROWS, n), F32)] + [jax.ShapeDtypeStruct(w.shape, BF16) for w in cast],
        name="ada_mod",
    )(cond, w_ada, b_ada, *cast)


def _route(logits):
    lane = lax.broadcasted_iota(I32, logits.shape, 1).astype(F32)
    neg = jnp.float32(-1e30)
    far = jnp.float32(LANES)
    is_g = lane < N_EXP_GROUPS
    gl = jnp.where(is_g, logits, neg)
    gmax = jnp.max(gl, axis=-1, keepdims=True)
    gsel = jnp.min(jnp.where(gl == gmax, lane, far), axis=-1, keepdims=True)
    psel = 1.0 / jnp.sum(jnp.where(is_g, jnp.exp(gl - gmax), 0.0), axis=-1, keepdims=True)
    e_lo = EXPERT_LANE0 + EXP_PER_GROUP * gsel
    el = jnp.where(lane >= e_lo, jnp.where(lane < e_lo + EXP_PER_GROUP, logits, neg), neg)
    v1 = jnp.max(el, axis=-1, keepdims=True)
    i1 = jnp.min(jnp.where(el == v1, lane, far), axis=-1, keepdims=True)
    el2 = jnp.where(lane == i1, neg, el)
    v2 = jnp.max(el2, axis=-1, keepdims=True)
    i2 = jnp.min(jnp.where(el2 == v2, jnp.where(lane == i1, far, lane), far), axis=-1, keepdims=True)
    e2 = jnp.exp(v2 - v1)
    w1 = psel / (1.0 + e2)
    w2 = psel * e2 / (1.0 + e2)
    gate = jnp.where(lane == i1, w1, jnp.where(lane == i2, w2, 0.0))
    a = jnp.minimum(i1, i2) - e_lo
    b = jnp.maximum(i1, i2) - e_lo
    pair = a * (7.0 - a) * 0.5 + (b - a - 1.0)
    return gate, gsel * PAIRS_PER_GROUP + pair


def _mix_kernel(*refs, S, L, P, use_rope, emit_kv, n_cast, n_blocks, U):
    it = iter(refs)
    x_ref = next(it)
    mod_ref = next(it)
    if P:
        ck_ref = next(it)
        cv_ref = next(it)
    if use_rope:
        cos_ref = next(it)
        sneg_ref = next(it)
        spos_ref = next(it)
    (g1_ref, win_ref, qg_ref, kg_ref, wpool_ref, pscale_ref, wa_ref, wb_ref, wo_ref,
     g2_ref, wr_ref) = (next(it) for _ in range(11))
    cast_in = [[next(it) for _ in range(n)] for n in n_cast]
    xmid_ref = next(it)
    h2_ref = next(it)
    gate_ref = next(it)
    oh_ref = next(it)
    if emit_kv:
        knew_ref = next(it)
        vnew_ref = next(it)
    cast_out = [next(it) for _ in n_cast]
    q_s, k_s, v_s, xp_s, h_s, attn_s, xm_s, mod2_s = (next(it) for _ in range(8))

    TM = S * L
    RB = ROW_BLOCK
    nrb = TM // RB
    n_steps = n_blocks // U
    score_gain = HEAD_DIM ** -0.5 * LOG2_E
    step = pl.program_id(0)
    block0 = U * jnp.minimum(step, n_steps - 1)
    slot = step % 2

    sh1 = mod_ref[0, 0:1, :]
    gain1 = g1_ref[...] * (1.0 + mod_ref[0, 1:2, :])
    gt1 = mod_ref[0, 2:3, :]
    sh2 = mod_ref[0, 3:4, :]
    gain2 = g2_ref[...] * (1.0 + mod_ref[0, 4:5, :])
    qg = qg_ref[...] * score_gain
    kg = kg_ref[...]

    def project(r, carry):
        r0 = pl.multiple_of(r * RB, RB)
        s = r0 // L
        o = pl.multiple_of(r0 % L, RB)
        hb = (_rms(x_ref[pl.ds(r0, RB), :]) * gain1 + sh1).astype(BF16)
        h_s[pl.ds(r0, RB), :] = hb
        p1 = jnp.dot(hb, win_ref[:, 0:GATE_COL], preferred_element_type=F32)
        if use_rope:
            cs = cos_ref[pl.ds(o, RB), :]
            sn = sneg_ref[pl.ds(o, RB), :]
            sp = spos_ref[pl.ds(o, RB), :]

        def rope(t):
            return (t * cs + pltpu.roll(t, HEAD_DIM - ROPE_NF, 1) * sn + pltpu.roll(t, ROPE_NF, 1) * sp)

        for hd in range(N_HEADS):
            qh = _rms(p1[:, hd * HEAD_DIM:(hd + 1) * HEAD_DIM]) * qg
            if use_rope:
                qh = rope(qh)
            q_s[hd, pl.ds(r0, RB), :] = qh.astype(BF16)
        for kh in range(N_KV_HEADS):
            c0 = ATTN_W + kh * HEAD_DIM
            kk = _rms(p1[:, c0:c0 + HEAD_DIM]) * kg
            if emit_kv:
                knew_ref[pl.ds(N_KV_HEADS * r0 + kh, RB, stride=N_KV_HEADS), :] = kk
            if use_rope:
                kk = rope(kk)
            k_s[s, pl.ds(P + o, RB), kh * HEAD_DIM:(kh + 1) * HEAD_DIM] = kk.astype(BF16)
        vv = p1[:, ATTN_W + KV_W:ATTN_W + 2 * KV_W]
        if emit_kv:
            for kh in range(N_KV_HEADS):
                vnew_ref[pl.ds(N_KV_HEADS * r0 + kh, RB, stride=N_KV_HEADS), :] = (
                    vv[:, kh * HEAD_DIM:(kh + 1) * HEAD_DIM])
        v_s[s, pl.ds(P + o, RB), :] = vv.astype(BF16)
        xp_s[s, pl.ds(POOL_HALO + o, RB), :] = p1[:, ATTN_W + 2 * KV_W:GATE_COL]
        return carry

    @pl.when(step == 0)
    def _():
        xm_s[1] = jnp.zeros((U * RB, D_MODEL), F32)
        mod2_s[1] = jnp.zeros((2, D_MODEL), F32)

    @pl.when((step < n_steps) & (step % (nrb // U) == 0))
    def _():
        if P:
            for kh in range(N_KV_HEADS):
                cols = slice(kh * HEAD_DIM, (kh + 1) * HEAD_DIM)
                k_s[0, 0:P, cols] = ck_ref[pl.ds(kh, P, stride=N_KV_HEADS), :].astype(BF16)
                v_s[0, 0:P, cols] = cv_ref[pl.ds(kh, P, stride=N_KV_HEADS), :].astype(BF16)
        xp_s[:, 0:POOL_HALO, :] = jnp.zeros((S, POOL_HALO, POOL_W), F32)
        xp_s[:, L + POOL_HALO:L + 2 * POOL_HALO, :] = jnp.zeros((S, POOL_HALO, POOL_W), F32)
        lax.fori_loop(0, TM // RB, project, 0)
        for srcs, dst in zip(cast_in, cast_out):
            col = 0
            for src in srcs:
                dst[..., col:col + src.shape[-1]] = src[...].astype(BF16)
                col += src.shape[-1]

    def mix(u):
        r0 = pl.multiple_of(((block0 + u) % nrb) * RB, RB)
        s = r0 // L
        o = pl.multiple_of(r0 % L, RB)
        attn_u = attn_s.at[u]
        rows = slice(u * RB, (u + 1) * RB)

        for hd in range(N_HEADS):
            kh = hd // GROUP
            k = k_s[s, :, kh * HEAD_DIM:(kh + 1) * HEAD_DIM]
            v = v_s[s, :, kh * HEAD_DIM:(kh + 1) * HEAD_DIM]
            qh = q_s[hd, pl.ds(r0, RB), :]
            sc = lax.dot_general(qh, k, (((1,), (1,)), ((), ())), preferred_element_type=F32)
            e = jnp.exp2(sc - jnp.max(sc, axis=-1, keepdims=True))
            den = jnp.sum(e, axis=-1, keepdims=True)
            oh = jnp.dot(e.astype(BF16), v, preferred_element_type=F32) / den
            attn_u[:, hd * HEAD_DIM:(hd + 1) * HEAD_DIM] = oh.astype(BF16)
        a = jnp.dot(attn_u[...], wa_ref[...], preferred_element_type=F32)

        t = o + lax.broadcasted_iota(I32, (RB, 1), 0)
        RW = RB + 2 * POOL_HALO
        parts = []
        for gi, w in enumerate(POOL_WINDOWS):
            cols = slice(gi * POOL_GC, (gi + 1) * POOL_GC)
            xw = xp_s[s, pl.ds(o, RW), cols]
            run = xw
            span = 1
            while span < w:
                run = run + pltpu.roll(run, span, 0)
                span *= 2
            if w // 2 > 1:
                run = pltpu.roll(run, RW - (w // 2 - 1), 0)
            tot = run[POOL_HALO:POOL_HALO + RB]
            cnt = (jnp.minimum(t + w // 2, L) - jnp.maximum(t - w // 2, 0)).astype(F32)
            parts.append(tot / cnt - xw[POOL_HALO:POOL_HALO + RB])
        dpool = jnp.concatenate(parts, axis=1).astype(BF16)
        pooled = jnp.dot(dpool, wpool_ref[...], preferred_element_type=F32) * pscale_ref[...]
        b = jnp.dot(pooled.astype(BF16), wb_ref[...], preferred_element_type=F32)

        gates = jnp.dot(h_s[pl.ds(r0, RB), :], win_ref[:, GATE_COL:IN_W], preferred_element_type=F32)
        merged = _sigmoid(gates[:, 0:D_MODEL]) * a + _sigmoid(gates[:, D_MODEL:2 * D_MODEL]) * b
        upd = jnp.dot(merged.astype(BF16), wo_ref[...], preferred_element_type=F32)
        xm = x_ref[pl.ds(r0, RB), :] + gt1 * upd
        xmid_ref[rows, :] = xm
        xm_s[slot, rows, :] = xm

    def moe_prep(u):
        rows = slice(u * RB, (u + 1) * RB)
        h2 = _rms(xm_s[1 - slot, rows, :]) * mod2_s[1 - slot, 0:1, :] + mod2_s[1 - slot, 1:2, :]
        hi = h2.astype(BF16)
        lo = (h2 - hi.astype(F32)).astype(BF16)
        l1 = jnp.dot(hi, wr_ref[...], preferred_element_type=F32)
        l2 = jnp.dot(lo, wr_ref[:, 0:LANES], preferred_element_type=F32)
        gate, bucket = _route(l1[:, 0:LANES] + l1[:, LANES:2 * LANES] + l2)
        groups = pl.ds(u * (RB // SUBLANES), RB // SUBLANES)
        _store_tiles(h2_ref.at[groups], _pack_bf16_pairs(h2))
        gate_ref[rows, :] = gate
        lane = lax.broadcasted_iota(I32, (RB, LANES), 1).astype(F32)
        oh_ref[rows, :] = jnp.where(lane == bucket, 1.0, 0.0).astype(BF16)

    mod2_s[slot, 0:1, :] = gain2
    mod2_s[slot, 1:2, :] = sh2
    for u in range(U):
        moe_prep(u)
    for u in range(U):
        mix(u)


def _mix(x2d, mod, mod_row, cache, rope_tabs, weights, *, S, L, emit_kv, blocks_per_step, cast=()):
    T = x2d.shape[0]
    TM = S * L
    P = cache[0].shape[0] // (T // L * N_KV_HEADS) if cache is not None else 0
    use_rope = rope_tabs is not None
    assert T % TM == 0 and L % ROW_BLOCK == 0
    assert not (use_rope or P) or S == 1
    Lk = P + L

    args = [x2d, mod]
    nrb = TM // ROW_BLOCK
    n_blocks = T // ROW_BLOCK
    step_rows = blocks_per_step * ROW_BLOCK
    steps_per_group = nrb // blocks_per_step
    n_mix_steps = n_blocks // blocks_per_step
    assert nrb % blocks_per_step == 0

    def mixed(s):
        return jnp.minimum(s, n_mix_steps - 1)

    def group(s):
        return mixed(s) // steps_per_group

    def prepared(s):
        return jnp.maximum(s - 1, 0)

    in_specs = [
        pl.BlockSpec((TM, D_MODEL), lambda s: (group(s), 0)),
        pl.BlockSpec((1, 6, D_MODEL), lambda s: (mod_row(group(s)), 0, 0)),
    ]
    if P:
        args += list(cache)
        in_specs += [pl.BlockSpec((P * N_KV_HEADS, HEAD_DIM), lambda s: (group(s), 0))] * 2
    if use_rope:
        args += list(rope_tabs)
        in_specs += [_resident((L, HEAD_DIM))] * 3
    args += list(weights)
    in_specs += [_resident(w.shape) for w in weights]
    n_steps = T // TM
    def per_group(shape):
        assert shape[0] % n_steps == 0
        blk = (shape[0] // n_steps,) + shape[1:]
        return pl.BlockSpec(blk, lambda s, n=len(blk): (group(s),) + (0,) * (n - 1))

    cast_out_shapes = [ws[0].shape[:-1] + (sum(w.shape[-1] for w in ws),) for ws in cast]
    for ws in cast:
        args += list(ws)
        in_specs += [per_group(w.shape) for w in ws]

    out_shape = [jax.ShapeDtypeStruct((T, D_MODEL), F32), jax.ShapeDtypeStruct(_tiles_shape(T, PACKED_CHUNKS), U32),
                 jax.ShapeDtypeStruct((T, LANES), F32),
                 jax.ShapeDtypeStruct((T, LANES), BF16)]
    out_specs = [pl.BlockSpec((step_rows, D_MODEL), lambda s: (mixed(s), 0)),
                 _tiles_spec(step_rows, prepared, PACKED_CHUNKS),
                 pl.BlockSpec((step_rows, LANES), lambda s: (prepared(s), 0)),
                 pl.BlockSpec((step_rows, LANES), lambda s: (prepared(s), 0))]
    if emit_kv:
        out_shape += [jax.ShapeDtypeStruct((T * N_KV_HEADS, HEAD_DIM), F32)] * 2
        out_specs += [pl.BlockSpec((TM * N_KV_HEADS, HEAD_DIM), lambda s: (group(s), 0))] * 2
    out_shape += [jax.ShapeDtypeStruct(shp, BF16) for shp in cast_out_shapes]
    out_specs += [per_group(shp) for shp in cast_out_shapes]

    scratch = [
        pltpu.VMEM((N_HEADS, TM, HEAD_DIM), BF16),
        pltpu.VMEM((S, Lk, KV_W), BF16),
        pltpu.VMEM((S, Lk, KV_W), BF16),
        pltpu.VMEM((S, L + 2 * POOL_HALO, POOL_W), F32),
        pltpu.VMEM((TM, D_MODEL), BF16),
        pltpu.VMEM((blocks_per_step, ROW_BLOCK, ATTN_W), BF16),
        pltpu.VMEM((2, step_rows, D_MODEL), F32),
        pltpu.VMEM((2, 2, D_MODEL), F32),
    ]
    kern = functools.partial(_mix_kernel, S=S, L=L, P=P, use_rope=use_rope, emit_kv=emit_kv,
                             n_cast=tuple(len(ws) for ws in cast), n_blocks=n_blocks, U=blocks_per_step)
    return pl.pallas_call(
        kern,
        grid=(n_mix_steps + 1,),
        in_specs=in_specs,
        out_specs=out_specs,
        out_shape=out_shape,
        scratch_shapes=scratch,
        compiler_params=pltpu.CompilerParams(
            dimension_semantics=("arbitrary",), vmem_limit_bytes=V7X_VMEM_LIMIT_BYTES),
        name="mixer_rope" if use_rope else "mixer_ctx",
    )(*args)


def _plan_kernel(oh_ref, dest_ref, meta_ref, *, n_blocks, tile):
    TB = TOKEN_BLOCK
    lane = lax.broadcasted_iota(I32, (SUBLANES, LANES), 1)

    def count(b, acc):
        oh = oh_ref[pl.ds(pl.multiple_of(b * TB, TB), TB), :].astype(F32)
        return acc + jnp.sum(oh, axis=0, keepdims=True)

    counts = lax.fori_loop(0, n_blocks, count, jnp.zeros((SUBLANES, LANES), F32))
    padded = jnp.floor((counts + (tile - 0.5)) * (1.0 / tile)) * tile
    ends = padded
    step = 1
    while step < LANES:
        ends = ends + jnp.where(lane >= step, pltpu.roll(ends, step, 1), 0.0)
        step *= 2
    starts = ends - padded

    tri = jnp.where(lax.broadcasted_iota(I32, (TB, TB), 1) < lax.broadcasted_iota(I32, (TB, TB), 0),
                    1.0, 0.0).astype(BF16)

    def place(b, seen):
        oh = oh_ref[pl.ds(pl.multiple_of(b * TB, TB), TB), :]
        ohf = oh.astype(F32)
        rank = jnp.dot(tri, oh, preferred_element_type=F32)
        base = (starts + seen)[0:1, :]
        d = jnp.sum(ohf * (rank + base), axis=1, keepdims=True)
        dest_ref[b] = _row(d).astype(I32)
        return seen + jnp.sum(ohf, axis=0, keepdims=True)

    lax.fori_loop(0, n_blocks, place, jnp.zeros((SUBLANES, LANES), F32))

    tile_row0 = lax.broadcasted_iota(I32, (LANES, LANES), 0).astype(F32) * tile
    is_bucket = lax.broadcasted_iota(I32, (LANES, LANES), 1) < N_BUCKETS
    done = jnp.sum(jnp.where(is_bucket, jnp.where(ends[0:1, :] <= tile_row0, 1.0, 0.0), 0.0),
                   axis=1, keepdims=True)
    bkt = jnp.minimum(done, N_BUCKETS - 1.0)
    grp = (jnp.where(bkt >= PAIRS_PER_GROUP, 1.0, 0.0) + jnp.where(bkt >= 2 * PAIRS_PER_GROUP, 1.0, 0.0)
           + jnp.where(bkt >= 3 * PAIRS_PER_GROUP, 1.0, 0.0))
    pair = bkt - PAIRS_PER_GROUP * grp
    a = jnp.where(pair >= 3.0, 1.0, 0.0) + jnp.where(pair >= 5.0, 1.0, 0.0)
    b = pair - a * (7.0 - a) * 0.5 + a + 1.0
    e1 = EXP_PER_GROUP * grp + a
    e2 = EXP_PER_GROUP * grp + b
    meta = jnp.concatenate(
        [_row(e1), _row(e2), jnp.floor(ends[0:1, :] * (1.0 / tile) + 0.5),
         jnp.zeros((SUBLANES - 3, LANES), F32)], axis=0)
    meta_ref[...] = meta.astype(I32)


def _plan(onehot, tile):
    T = onehot.shape[0]
    n_blocks = T // TOKEN_BLOCK
    dest, meta = pl.pallas_call(
        functools.partial(_plan_kernel, n_blocks=n_blocks, tile=tile),
        out_shape=[jax.ShapeDtypeStruct((n_blocks, 1, TOKEN_BLOCK), I32),
                   jax.ShapeDtypeStruct((SUBLANES, LANES), I32)],
        name="moe_plan",
    )(onehot)
    return dest.reshape(T), meta


def _sc_move_rows(src_v, table_hbm, out_hbm, lo, n_rows, idx_v, pieces_v, sem):
    chunks = pieces_v.shape[0] // SC_ROWS_PER_STEP
    lane = lax.iota(I32, SC_LANES)
    row_in_group = lane & (SUBLANES - 1)
    chunk_in_pair = lane >> 3
    rows_per_gather = SC_PIECES_PER_GATHER // chunks

    @pl.loop(0, n_rows // SC_ROWS_PER_STEP)
    def _(step):
        copies = []
        for g in range(SC_ROWS_PER_STEP // rows_per_gather):
            r0 = step * SC_ROWS_PER_STEP + g * rows_per_gather
            for v in range(SC_PIECES_PER_GATHER // SC_LANES):
                group, chunk0 = v // (chunks // 2), 2 * (v % (chunks // 2))
                tok = plsc.load_gather(src_v, [r0 + group * SUBLANES + row_in_group])
                piece = (tok >> 3) * (SUBLANES * chunks) + (chunk0 + chunk_in_pair) * SUBLANES + (tok & 7)
                idx_v[pl.ds(g * SC_PIECES_PER_GATHER + v * SC_LANES, SC_LANES)] = piece
            window = pl.ds(g * SC_PIECES_PER_GATHER, SC_PIECES_PER_GATHER)
            copies.append(pltpu.async_copy(table_hbm.at[idx_v.at[window]], pieces_v.at[window], sem))
        for cp in copies:
            cp.wait()
        first = pl.multiple_of((lo + step * SC_ROWS_PER_STEP) * chunks, SC_ROWS_PER_STEP * chunks)
        pltpu.sync_copy(pieces_v, out_hbm.at[pl.ds(first, SC_ROWS_PER_STEP * chunks)])


def _sc_scratch(chunks, dtype):
    return [pltpu.VMEM((SC_ROWS_PER_STEP * chunks,), I32), pltpu.VMEM((SC_ROWS_PER_STEP * chunks, LANES), dtype)]


def _sc_dispatch(h2_flat, gate_rows, dest, n_rows):
    T = dest.shape[0]
    per_worker = n_rows // SC_WORKERS
    rows_per_step = SC_ROWS_PER_STEP
    chunks = h2_flat.shape[0] // T
    assert n_rows % SC_WORKERS == 0 and per_worker % rows_per_step == 0 and T % SC_LANES == 0
    mesh = plsc.VectorSubcoreMesh(core_axis_name="c", subcore_axis_name="s")

    @functools.partial(
        pl.kernel, mesh=mesh,
        out_type=[jax.ShapeDtypeStruct((n_rows * chunks, LANES), h2_flat.dtype),
                  jax.ShapeDtypeStruct((n_rows, LANES), F32)],
        scratch_types=[pltpu.VMEM((T,), I32), pltpu.VMEM((per_worker,), I32)]
        + _sc_scratch(chunks, h2_flat.dtype)
        + [pltpu.VMEM((rows_per_step, LANES), F32), pltpu.SemaphoreType.DMA, pltpu.SemaphoreType.DMA],
        compiler_params=pltpu.CompilerParams(use_tc_tiling_on_sc=True, needs_layout_passes=False),
        name="sc_dispatch",
    )
    def dispatch(h2_hbm, gate_hbm, dest_hbm, out_h_hbm, out_g_hbm,
                 dest_v, src_v, idx_v, pieces_v, gates_v, sem_h, sem_g):
        worker = lax.axis_index("s") * SC_CORES + lax.axis_index("c")
        lo = worker * per_worker
        pltpu.sync_copy(dest_hbm, dest_v)

        @pl.loop(0, per_worker // SC_LANES)
        def _(j):
            j0 = pl.multiple_of(j * SC_LANES, SC_LANES)
            src_v[pl.ds(j0, SC_LANES)] = lax.rem(lo + j0 + lax.iota(I32, SC_LANES), T)

        @pl.loop(0, T // SC_LANES)
        def _(j):
            t0 = pl.multiple_of(j * SC_LANES, SC_LANES)
            d = dest_v[pl.ds(t0, SC_LANES)] - lo
            mine = (d >= 0) & (d < per_worker)
            plsc.store_scatter(src_v, [jnp.where(mine, d, 0)], t0 + lax.iota(I32, SC_LANES), mask=mine)

        @pl.loop(0, per_worker // rows_per_step)
        def _(j):
            off = pl.multiple_of(j * rows_per_step, rows_per_step)
            pltpu.async_copy(gate_hbm.at[src_v.at[pl.ds(off, rows_per_step)]], gates_v, sem_g).wait()
            pltpu.sync_copy(gates_v, out_g_hbm.at[pl.ds(lo + off, rows_per_step)])

        _sc_move_rows(src_v, h2_hbm, out_h_hbm, lo, per_worker, idx_v, pieces_v, sem_h)

    return dispatch(h2_flat, gate_rows, dest)


def _expert_kernel(meta, x_ref, gv_ref, *refs, tile, per_step):
    wgu_refs, wd_refs, o_ref = refs[:WEIGHT_PARTS], refs[WEIGHT_PARTS:2 * WEIGHT_PARTS], refs[-1]
    k_part = D_MODEL // WEIGHT_PARTS
    groups = tile // SUBLANES
    n_used = meta[2, LANES - 1]

    def one_tile(k, carry):
        t = pl.program_id(0) * per_step + k
        rows = pl.ds(pl.multiple_of(k * groups, groups), groups)

        @pl.when(t < n_used)
        def _():
            x = _unpack_bf16_pairs(_load_tiles(x_ref.at[rows]))
            gv = gv_ref[pl.ds(pl.multiple_of(k * tile, SUBLANES), tile), :]
            lane = lax.broadcasted_iota(I32, gv.shape, 1)
            out = None
            for e in (meta[0, t], meta[1, t]):
                ge = jnp.sum(jnp.where(lane == EXPERT_LANE0 + e, gv, 0.0), axis=-1, keepdims=True)
                h = None
                for p, w_ref in enumerate(wgu_refs):
                    part = jnp.dot(x[:, p * k_part:(p + 1) * k_part], w_ref[e], preferred_element_type=F32)
                    h = part if h is None else h + part
                hg = h[:, 0:D_EXPERT]
                hid = (hg * _sigmoid(hg) * h[:, D_EXPERT:2 * D_EXPERT] * ge).astype(BF16)
                y = jnp.concatenate(
                    [jnp.dot(hid, w_ref[e], preferred_element_type=F32) for w_ref in wd_refs], axis=1)
                out = y if out is None else out + y
            _store_tiles(o_ref.at[rows], _pack_bf16_pairs(out))

        @pl.when(t >= n_used)
        def _():
            o_ref[rows] = jnp.zeros((groups,) + o_ref.shape[1:], U32)

        return carry

    lax.fori_loop(0, per_step, one_tile, 0)


def _experts(sorted_h2, sorted_gates, meta, wgu, wd, tiling):
    tile, per_step, n_tiles = tiling
    step_rows = tile * per_step
    assert n_tiles * tile == sorted_h2.shape[0] * SUBLANES and n_tiles % per_step == 0

    def last_used(i, meta):
        return jnp.minimum(i, (meta[2, LANES - 1] - 1) // per_step)

    return pl.pallas_call(
        functools.partial(_expert_kernel, tile=tile, per_step=per_step),
        grid_spec=pltpu.PrefetchScalarGridSpec(
            num_scalar_prefetch=1,
            grid=(n_tiles // per_step,),
            in_specs=[
                _tiles_spec(step_rows, last_used, PACKED_CHUNKS),
                pl.BlockSpec((step_rows, LANES), lambda *a: (last_used(*a), 0)),
            ] + _resident_parts(wgu.shape, 1, WEIGHT_PARTS) + _resident_parts(wd.shape, 2, WEIGHT_PARTS),
            out_specs=_tiles_spec(step_rows, lambda i, *_: i, PACKED_CHUNKS),
        ),
        out_shape=jax.ShapeDtypeStruct(_tiles_shape(n_tiles * tile, PACKED_CHUNKS), U32),
        compiler_params=pltpu.CompilerParams(
            dimension_semantics=("arbitrary",), vmem_limit_bytes=V7X_VMEM_LIMIT_BYTES),
        name="moe_experts",
    )(meta, sorted_h2, sorted_gates, *([wgu] * WEIGHT_PARTS), *([wd] * WEIGHT_PARTS))


def _sc_row_gather(table_flat, idx, chunks):
    n = idx.shape[0]
    per_worker = n // SC_WORKERS
    assert n % SC_WORKERS == 0 and per_worker % SC_ROWS_PER_STEP == 0
    mesh = plsc.VectorSubcoreMesh(core_axis_name="c", subcore_axis_name="s")

    @functools.partial(
        pl.kernel, mesh=mesh,
        out_type=jax.ShapeDtypeStruct((n * chunks, LANES), table_flat.dtype),
        scratch_types=[pltpu.VMEM((per_worker,), I32)] + _sc_scratch(chunks, table_flat.dtype)
        + [pltpu.SemaphoreType.DMA],
        compiler_params=pltpu.CompilerParams(use_tc_tiling_on_sc=True, needs_layout_passes=False),
        name="sc_row_gather",
    )
    def gather(table_hbm, idx_hbm, out_hbm, src_v, idx_v, pieces_v, sem):
        worker = lax.axis_index("s") * SC_CORES + lax.axis_index("c")
        lo = worker * per_worker
        pltpu.sync_copy(idx_hbm.at[pl.ds(lo, per_worker)], src_v)
        _sc_move_rows(src_v, table_hbm, out_hbm, lo, per_worker, idx_v, pieces_v, sem)

    return gather(table_flat, idx)


def _final_kernel(x_ref, moe_ref, mod_ref, gf_ref, o_ref):
    y = x_ref[...] + mod_ref[0, 5:6, :] * _unpack_bf16_pairs(_load_tiles(moe_ref)).astype(F32)
    o_ref[...] = _rms(y) * gf_ref[...]


def _final(xmid, moe_rows, mod, mod_row, gf):
    T = xmid.shape[0]
    return pl.pallas_call(
        _final_kernel,
        grid=(T // FINAL_BLOCK,),
        in_specs=[
            pl.BlockSpec((FINAL_BLOCK, D_MODEL), lambda i: (i, 0)),
            _tiles_spec(FINAL_BLOCK, lambda i: i, PACKED_CHUNKS),
            pl.BlockSpec((1, 6, D_MODEL), lambda i: (mod_row(i), 0, 0)),
            pl.BlockSpec((1, D_MODEL), lambda i: (0, 0)),
        ],
        out_specs=pl.BlockSpec((FINAL_BLOCK, D_MODEL), lambda i: (i, 0)),
        out_shape=jax.ShapeDtypeStruct((T, D_MODEL), F32),
        compiler_params=pltpu.CompilerParams(
            dimension_semantics=("arbitrary",), vmem_limit_bytes=V7X_VMEM_LIMIT_BYTES),
        name="moe_final",
    )(xmid, moe_rows, mod, gf)


def _flat(tiles):
    return tiles.reshape(-1, LANES)


def _expert_tiling(T):
    tile = -(-(T * 9) // (8 * N_BUCKETS * 64)) * 64
    per_step = max(1, EXPERT_STEP_ROWS // tile)
    n_tiles = (T + N_BUCKETS * (tile - 1)) // tile
    while n_tiles % per_step or (n_tiles * tile) % (SC_WORKERS * SC_ROWS_PER_STEP):
        n_tiles += 1
    return tile, per_step, n_tiles


def _moe_dispatch(h2_tiles, gate_rows, onehot, tiling):
    T = gate_rows.shape[0]
    tile, _, n_tiles = tiling
    n_rows = n_tiles * tile
    assert n_tiles <= LANES and T % TOKEN_BLOCK == 0
    dest, meta = _plan(onehot, tile)
    sorted_h2, sorted_gates = _sc_dispatch(_flat(h2_tiles), gate_rows, dest, n_rows)
    return sorted_h2.reshape(_tiles_shape(n_rows, PACKED_CHUNKS)), sorted_gates, dest, meta


def _moe_unpermute(moe_sorted_tiles, dest):
    chunks = moe_sorted_tiles.shape[1]
    return _sc_row_gather(_flat(moe_sorted_tiles), dest, chunks).reshape(_tiles_shape(dest.shape[0], chunks))


def _rope_tables(n_tokens):
    t = np.arange(n_tokens)
    row = (t // GRID_W).astype(np.float32)
    col = (t % GRID_W).astype(np.float32)
    freq = np.float32(ROPE_THETA) ** (-np.arange(ROPE_NF, dtype=np.float32) / np.float32(ROPE_NF))
    ang = np.concatenate([row[:, None] * freq] * 2 + [col[:, None] * freq] * 2, axis=-1)
    first = (np.arange(HEAD_DIM) % (2 * ROPE_NF)) < ROPE_NF
    sin = np.sin(ang)
    zero = np.float32(0.0)
    return (jnp.asarray(np.cos(ang)), jnp.asarray(np.where(first, -sin, zero)),
            jnp.asarray(np.where(first, zero, sin)))


def kernel(x_prompt, x_sample, cache_k, cache_v, c, c_ctx, norm1_g, norm2_g, w_ada, b_ada, w_in, q_norm_g, k_norm_g, w_pool, pool_scale, w_branch_a, w_branch_b, w_out, w_router_group, w_router_expert, w_exp_gate, w_exp_up, w_exp_down, final_norm_g):
    assert norm1_g.shape[0] == 1, "single-layer trunk"
    B, L_ctx, _ = x_prompt.shape
    Bs, L_lat, _ = x_sample.shape
    P = cache_k.shape[2]
    assert 1 + Bs <= COND_ROWS

    cond = jnp.concatenate([c_ctx[None, :], c, jnp.zeros((COND_ROWS - 1 - Bs, D_MODEL), F32)], axis=0)
    wpool_bd = jax.scipy.linalg.block_diag(*[w_pool[0, g] for g in range(len(POOL_WINDOWS))])
    mod, w_in_b, wpool_b, wa_b, wb_b, wo_b = _ada(
        cond, w_ada[0], b_ada[0][None, :],
        cast=(w_in[0], wpool_bd, w_branch_a[0], w_branch_b[0], w_out[0]))
    mod = mod.reshape(COND_ROWS, 6, D_MODEL)

    wr = jnp.concatenate([w_router_group[0], w_router_expert[0],
                          jnp.zeros((D_MODEL, LANES - N_EXP_GROUPS - N_EXPERTS), F32)], axis=1)
    wr_hi = wr.astype(BF16)
    wr_lo = (wr - wr_hi.astype(F32)).astype(BF16)
    mix_w = (norm1_g[0][None, :], w_in_b, q_norm_g[0][None, :], k_norm_g[0][None, :],
             wpool_b, pool_scale[0][None, :], wa_b, wb_b, wo_b,
             norm2_g[0][None, :], jnp.concatenate([wr_hi, wr_lo], axis=1))
    gf = final_norm_g[None, :]

    xp2 = x_prompt.reshape(B * L_ctx, D_MODEL)
    xmid_p, h2_p, gate_p, oh_p, knew, vnew, wgu, wd = _mix(
        xp2, mod, lambda i: 0, None, None, mix_w, S=2, L=L_ctx, emit_kv=True, blocks_per_step=2,
        cast=((w_exp_gate[0], w_exp_up[0]), (w_exp_down[0],)))
    tiling_p = _expert_tiling(B * L_ctx)
    sh_p, sg_p, dest_p, meta_p = _moe_dispatch(h2_p, gate_p, oh_p, tiling_p)

    xs2 = x_sample.reshape(Bs * L_lat, D_MODEL)
    cache = (cache_k.reshape(Bs * P * N_KV_HEADS, HEAD_DIM), cache_v.reshape(Bs * P * N_KV_HEADS, HEAD_DIM))
    xmid_s, h2_s, gate_s, oh_s = _mix(xs2, mod, lambda i: 1 + i, cache, _rope_tables(L_lat), mix_w,
                                      S=1, L=L_lat, emit_kv=False, blocks_per_step=1)
    tiling_s = _expert_tiling(Bs * L_lat)
    sh_s, sg_s, dest_s, meta_s = _moe_dispatch(h2_s, gate_s, oh_s, tiling_s)

    moe_p = _moe_unpermute(_experts(sh_p, sg_p, meta_p, wgu, wd, tiling_p), dest_p)
    moe_s = _moe_unpermute(_experts(sh_s, sg_s, meta_s, wgu, wd, tiling_s), dest_s)
    y_prompt = _final(xmid_p, moe_p, mod, lambda i: 0, gf)
    blocks_per_seq = L_lat // FINAL_BLOCK
    y_sample = _final(xmid_s, moe_s, mod, lambda i: 1 + i // blocks_per_seq, gf)

    return (y_prompt.reshape(B, L_ctx, D_MODEL), y_sample.reshape(Bs, L_lat, D_MODEL),
            knew.reshape(B, 1, L_ctx, N_KV_HEADS, HEAD_DIM), vnew.reshape(B, 1, L_ctx, N_KV_HEADS, HEAD_DIM))
```

```python
import functools

import numpy as np
import jax
import jax.numpy as jnp
from jax import lax
from jax.experimental import pallas as pl
from jax.experimental.pallas import tpu as pltpu
from jax.experimental.pallas import tpu_sc as plsc

F32 = jnp.float32
BF16 = jnp.bfloat16
I32 = jnp.int32
U32 = jnp.uint32

D_MODEL = 1024
HEAD_DIM = 128
N_HEADS = 8
N_KV_HEADS = 2
GROUP = N_HEADS // N_KV_HEADS
ATTN_W = N_HEADS * HEAD_DIM
KV_W = N_KV_HEADS * HEAD_DIM
POOL_WINDOWS = (2, 4, 8, 16)
POOL_GC = 128
POOL_W = POOL_GC * len(POOL_WINDOWS)
IN_W = ATTN_W + 2 * KV_W + POOL_W + 2 * D_MODEL
GATE_COL = ATTN_W + 2 * KV_W + POOL_W
GRID_W = 64
ROPE_THETA = 10000.0
ROPE_NF = HEAD_DIM // 4
N_EXP_GROUPS = 4
EXP_PER_GROUP = 4
N_EXPERTS = 16
D_EXPERT = 256
EPS = 1e-6
LOG2_E = 1.4426950408889634

LANES = 128
SUBLANES = 8
COND_ROWS = SUBLANES
POOL_HALO = 8
ROW_BLOCK = 256
ADA_COLS = 768
EXPERT_LANE0 = N_EXP_GROUPS
PAIRS_PER_GROUP = EXP_PER_GROUP * (EXP_PER_GROUP - 1) // 2
N_BUCKETS = N_EXP_GROUPS * PAIRS_PER_GROUP
EXPERT_STEP_ROWS = 1536
TOKEN_BLOCK = 1024
FINAL_BLOCK = 1024
ROW_CHUNKS = D_MODEL // LANES
SC_CORES = 2
SC_SUBCORES = 16
SC_WORKERS = SC_CORES * SC_SUBCORES
SC_LANES = 16
SC_PIECES_PER_GATHER = 128
SC_ROWS_PER_STEP = 64
PACKED_CHUNKS = ROW_CHUNKS // 2
V7X_VMEM_LIMIT_BYTES = 56 * 1024 * 1024


def _sigmoid(x):
    return 1.0 / (1.0 + jnp.exp(-x))


def _rms(x):
    return x * lax.rsqrt(jnp.mean(x * x, axis=-1, keepdims=True) + EPS)


def _resident(shape):
    zeros = (0,) * len(shape)
    return pl.BlockSpec(shape, lambda i, *_: zeros, pipeline_mode=pl.Buffered(1))


def _tiles_shape(n, chunks=ROW_CHUNKS):
    return (n // SUBLANES, chunks, SUBLANES, LANES)


def _tiles_spec(n, block_index, chunks=ROW_CHUNKS):
    return pl.BlockSpec(_tiles_shape(n, chunks), lambda *a: (block_index(*a), 0, 0, 0))


def _store_tiles(ref, x):
    for c in range(ref.shape[1]):
        ref[:, c, :, :] = x[:, c * LANES:(c + 1) * LANES].reshape(x.shape[0] // SUBLANES, SUBLANES, LANES)


def _load_tiles(ref):
    n = ref.shape[0] * SUBLANES
    return jnp.concatenate([ref[:, c, :, :].reshape(n, LANES) for c in range(ref.shape[1])], axis=1)


def _pack_bf16_pairs(x):
    bits = pltpu.bitcast(x.astype(BF16).astype(F32), U32)
    w = x.shape[1] // 2
    return bits[:, :w] | (bits[:, w:] >> 16)


def _unpack_bf16_pairs(words):
    hi = pltpu.bitcast(words & jnp.uint32(0xFFFF0000), F32).astype(BF16)
    lo = pltpu.bitcast(words << 16, F32).astype(BF16)
    return jnp.concatenate([hi, lo], axis=1)


def _row(x):
    return jnp.transpose(jnp.broadcast_to(x, (x.shape[0], LANES)))[0:1, :]


def _ada_kernel(c_ref, w_ref, b_ref, *refs):
    n_cast = (len(refs) - 1) // 2
    c = c_ref[...]
    s = (c * _sigmoid(c)).astype(BF16)
    refs[n_cast][...] = jnp.dot(s, w_ref[...].astype(BF16), preferred_element_type=F32) + b_ref[...]
    for src, dst in zip(refs[:n_cast], refs[n_cast + 1:]):
        dst[...] = src[...].astype(BF16)


def _ada(cond, w_ada, b_ada, cast=()):
    n = w_ada.shape[1]
    n_steps = n // ADA_COLS
    cast_specs = []
    for w in cast:
        assert w.ndim == 2 and w.shape[0] % (n_steps * 2 * SUBLANES) == 0
        cast_specs.append(pl.BlockSpec((w.shape[0] // n_steps, w.shape[1]), lambda j: (j, 0)))
    return pl.pallas_call(
        _ada_kernel,
        grid=(n_steps,),
        in_specs=[
            pl.BlockSpec((COND_ROWS, D_MODEL), lambda j: (0, 0)),
            pl.BlockSpec((D_MODEL, ADA_COLS), lambda j: (0, j)),
            pl.BlockSpec((1, ADA_COLS), lambda j: (0, j)),
        ] + cast_specs,
        out_specs=[pl.BlockSpec((COND_ROWS, ADA_COLS), lambda j: (0, j))] + cast_specs,
        out_shape=[jax.ShapeDtypeStruct((COND_ROWS, n), F32)] + [jax.ShapeDtypeStruct(w.shape, BF16) for w in cast],
        name="ada_mod",
    )(cond, w_ada, b_ada, *cast)


def _route(logits):
    lane = lax.broadcasted_iota(I32, logits.shape, 1).astype(F32)
    neg = jnp.float32(-1e30)
    far = jnp.float32(LANES)
    is_g = lane < N_EXP_GROUPS
    gl = jnp.where(is_g, logits, neg)
    gmax = jnp.max(gl, axis=-1, keepdims=True)
    gsel = jnp.min(jnp.where(gl == gmax, lane, far), axis=-1, keepdims=True)
    psel = 1.0 / jnp.sum(jnp.where(is_g, jnp.exp(gl - gmax), 0.0), axis=-1, keepdims=True)
    e_lo = EXPERT_LANE0 + EXP_PER_GROUP * gsel
    el = jnp.where(lane >= e_lo, jnp.where(lane < e_lo + EXP_PER_GROUP, logits, neg), neg)
    v1 = jnp.max(el, axis=-1, keepdims=True)
    i1 = jnp.min(jnp.where(el == v1, lane, far), axis=-1, keepdims=True)
    el2 = jnp.where(lane == i1, neg, el)
    v2 = jnp.max(el2, axis=-1, keepdims=True)
    i2 = jnp.min(jnp.where(el2 == v2, jnp.where(lane == i1, far, lane), far), axis=-1, keepdims=True)
    e2 = jnp.exp(v2 - v1)
    w1 = psel / (1.0 + e2)
    w2 = psel * e2 / (1.0 + e2)
    gate = jnp.where(lane == i1, w1, jnp.where(lane == i2, w2, 0.0))
    a = jnp.minimum(i1, i2) - e_lo
    b = jnp.maximum(i1, i2) - e_lo
    pair = a * (7.0 - a) * 0.5 + (b - a - 1.0)
    return gate, gsel * PAIRS_PER_GROUP + pair


def _mix_kernel(*refs, S, L, P, use_rope, emit_kv, n_cast, n_blocks, U):
    it = iter(refs)
    x_ref = next(it)
    mod_ref = next(it)
    if P:
        ck_ref = next(it)
        cv_ref = next(it)
    if use_rope:
        cos_ref = next(it)
        sneg_ref = next(it)
        spos_ref = next(it)
    (g1_ref, win_ref, qg_ref, kg_ref, wpool_ref, pscale_ref, wa_ref, wb_ref, wo_ref,
     g2_ref, wr_ref) = (next(it) for _ in range(11))
    cast_in = [[next(it) for _ in range(n)] for n in n_cast]
    xmid_ref = next(it)
    h2_ref = next(it)
    gate_ref = next(it)
    oh_ref = next(it)
    if emit_kv:
        knew_ref = next(it)
        vnew_ref = next(it)
    cast_out = [next(it) for _ in n_cast]
    q_s, k_s, v_s, xp_s, h_s, attn_s, xm_s, mod2_s = (next(it) for _ in range(8))

    TM = S * L
    RB = ROW_BLOCK
    nrb = TM // RB
    n_steps = n_blocks // U
    score_gain = HEAD_DIM ** -0.5 * LOG2_E
    step = pl.program_id(0)
    block0 = U * jnp.minimum(step, n_steps - 1)
    slot = step % 2

    sh1 = mod_ref[0, 0:1, :]
    gain1 = g1_ref[...] * (1.0 + mod_ref[0, 1:2, :])
    gt1 = mod_ref[0, 2:3, :]
    sh2 = mod_ref[0, 3:4, :]
    gain2 = g2_ref[...] * (1.0 + mod_ref[0, 4:5, :])
    qg = qg_ref[...] * score_gain
    kg = kg_ref[...]

    def project(r, carry):
        r0 = pl.multiple_of(r * RB, RB)
        s = r0 // L
        o = pl.multiple_of(r0 % L, RB)
        hb = (_rms(x_ref[pl.ds(r0, RB), :]) * gain1 + sh1).astype(BF16)
        h_s[pl.ds(r0, RB), :] = hb
        p1 = jnp.dot(hb, win_ref[:, 0:GATE_COL], preferred_element_type=F32)
        if use_rope:
            cs = cos_ref[pl.ds(o, RB), :]
            sn = sneg_ref[pl.ds(o, RB), :]
            sp = spos_ref[pl.ds(o, RB), :]

        def rope(t):
            return (t * cs + pltpu.roll(t, HEAD_DIM - ROPE_NF, 1) * sn + pltpu.roll(t, ROPE_NF, 1) * sp)

        for hd in range(N_HEADS):
            qh = _rms(p1[:, hd * HEAD_DIM:(hd + 1) * HEAD_DIM]) * qg
            if use_rope:
                qh = rope(qh)
            q_s[hd, pl.ds(r0, RB), :] = qh.astype(BF16)
        for kh in range(N_KV_HEADS):
            c0 = ATTN_W + kh * HEAD_DIM
            kk = _rms(p1[:, c0:c0 + HEAD_DIM]) * kg
            if emit_kv:
                knew_ref[pl.ds(N_KV_HEADS * r0 + kh, RB, stride=N_KV_HEADS), :] = kk
            if use_rope:
                kk = rope(kk)
            k_s[s, pl.ds(P + o, RB), kh * HEAD_DIM:(kh + 1) * HEAD_DIM] = kk.astype(BF16)
        vv = p1[:, ATTN_W + KV_W:ATTN_W + 2 * KV_W]
        if emit_kv:
            for kh in range(N_KV_HEADS):
                vnew_ref[pl.ds(N_KV_HEADS * r0 + kh, RB, stride=N_KV_HEADS), :] = (
                    vv[:, kh * HEAD_DIM:(kh + 1) * HEAD_DIM])
        v_s[s, pl.ds(P + o, RB), :] = vv.astype(BF16)
        xp_s[s, pl.ds(POOL_HALO + o, RB), :] = p1[:, ATTN_W + 2 * KV_W:GATE_COL]
        return carry

    @pl.when(step == 0)
    def _():
        xm_s[1] = jnp.zeros((U * RB, D_MODEL), F32)
        mod2_s[1] = jnp.zeros((2, D_MODEL), F32)

    @pl.when((step < n_steps) & (step % (nrb // U) == 0))
    def _():
        if P:
            for kh in range(N_KV_HEADS):
                cols = slice(kh * HEAD_DIM, (kh + 1) * HEAD_DIM)
                k_s[0, 0:P, cols] = ck_ref[pl.ds(kh, P, stride=N_KV_HEADS), :].astype(BF16)
                v_s[0, 0:P, cols] = cv_ref[pl.ds(kh, P, stride=N_KV_HEADS), :].astype(BF16)
        xp_s[:, 0:POOL_HALO, :] = jnp.zeros((S, POOL_HALO, POOL_W), F32)
        xp_s[:, L + POOL_HALO:L + 2 * POOL_HALO, :] = jnp.zeros((S, POOL_HALO, POOL_W), F32)
        lax.fori_loop(0, TM // RB, project, 0)
        for srcs, dst in zip(cast_in, cast_out):
            col = 0
            for src in srcs:
                dst[..., col:col + src.shape[-1]] = src[...].astype(BF16)
                col += src.shape[-1]

    def mix(u):
        r0 = pl.multiple_of(((block0 + u) % nrb) * RB, RB)
        s = r0 // L
        o = pl.multiple_of(r0 % L, RB)
        attn_u = attn_s.at[u]
        rows = slice(u * RB, (u + 1) * RB)

        for hd in range(N_HEADS):
            kh = hd // GROUP
            k = k_s[s, :, kh * HEAD_DIM:(kh + 1) * HEAD_DIM]
            v = v_s[s, :, kh * HEAD_DIM:(kh + 1) * HEAD_DIM]
            qh = q_s[hd, pl.ds(r0, RB), :]
            sc = lax.dot_general(qh, k, (((1,), (1,)), ((), ())), preferred_element_type=F32)
            e = jnp.exp2(sc - jnp.max(sc, axis=-1, keepdims=True))
            den = jnp.sum(e, axis=-1, keepdims=True)
            oh = jnp.dot(e.astype(BF16), v, preferred_element_type=F32) / den
            attn_u[:, hd * HEAD_DIM:(hd + 1) * HEAD_DIM] = oh.astype(BF16)
        a = jnp.dot(attn_u[...], wa_ref[...], preferred_element_type=F32)

        t = o + lax.broadcasted_iota(I32, (RB, 1), 0)
        RW = RB + 2 * POOL_HALO
        parts = []
        for gi, w in enumerate(POOL_WINDOWS):
            cols = slice(gi * POOL_GC, (gi + 1) * POOL_GC)
            xw = xp_s[s, pl.ds(o, RW), cols]
            run = xw
            span = 1
            while span < w:
                run = run + pltpu.roll(run, span, 0)
                span *= 2
            if w // 2 > 1:
                run = pltpu.roll(run, RW - (w // 2 - 1), 0)
            tot = run[POOL_HALO:POOL_HALO + RB]
            cnt = (jnp.minimum(t + w // 2, L) - jnp.maximum(t - w // 2, 0)).astype(F32)
            parts.append(tot / cnt - xw[POOL_HALO:POOL_HALO + RB])
        dpool = jnp.concatenate(parts, axis=1).astype(BF16)
        pooled = jnp.dot(dpool, wpool_ref[...], preferred_element_type=F32) * pscale_ref[...]
        b = jnp.dot(pooled.astype(BF16), wb_ref[...], preferred_element_type=F32)

        gates = jnp.dot(h_s[pl.ds(r0, RB), :], win_ref[:, GATE_COL:IN_W], preferred_element_type=F32)
        merged = _sigmoid(gates[:, 0:D_MODEL]) * a + _sigmoid(gates[:, D_MODEL:2 * D_MODEL]) * b
        upd = jnp.dot(merged.astype(BF16), wo_ref[...], preferred_element_type=F32)
        xm = x_ref[pl.ds(r0, RB), :] + gt1 * upd
        xmid_ref[rows, :] = xm
        xm_s[slot, rows, :] = xm

    def moe_prep(u):
        rows = slice(u * RB, (u + 1) * RB)
        h2 = _rms(xm_s[1 - slot, rows, :]) * mod2_s[1 - slot, 0:1, :] + mod2_s[1 - slot, 1:2, :]
        hi = h2.astype(BF16)
        lo = (h2 - hi.astype(F32)).astype(BF16)
        l1 = jnp.dot(hi, wr_ref[...], preferred_element_type=F32)
        l2 = jnp.dot(lo, wr_ref[:, 0:LANES], preferred_element_type=F32)
        gate, bucket = _route(l1[:, 0:LANES] + l1[:, LANES:2 * LANES] + l2)
        groups = pl.ds(u * (RB // SUBLANES), RB // SUBLANES)
        _store_tiles(h2_ref.at[groups], _pack_bf16_pairs(h2))
        gate_ref[rows, :] = gate
        lane = lax.broadcasted_iota(I32, (RB, LANES), 1).astype(F32)
        oh_ref[rows, :] = jnp.where(lane == bucket, 1.0, 0.0).astype(BF16)

    mod2_s[slot, 0:1, :] = gain2
    mod2_s[slot, 1:2, :] = sh2
    for u in range(U):
        moe_prep(u)
    for u in range(U):
        mix(u)


def _mix(x2d, mod, mod_row, cache, rope_tabs, weights, *, S, L, emit_kv, blocks_per_step, cast=()):
    T = x2d.shape[0]
    TM = S * L
    P = cache[0].shape[0] // (T // L * N_KV_HEADS) if cache is not None else 0
    use_rope = rope_tabs is not None
    assert T % TM == 0 and L % ROW_BLOCK == 0
    assert not (use_rope or P) or S == 1
    Lk = P + L

    args = [x2d, mod]
    nrb = TM // ROW_BLOCK
    n_blocks = T // ROW_BLOCK
    step_rows = blocks_per_step * ROW_BLOCK
    steps_per_group = nrb // blocks_per_step
    n_mix_steps = n_blocks // blocks_per_step
    assert nrb % blocks_per_step == 0

    def mixed(s):
        return jnp.minimum(s, n_mix_steps - 1)

    def group(s):
        return mixed(s) // steps_per_group

    def prepared(s):
        return jnp.maximum(s - 1, 0)

    in_specs = [
        pl.BlockSpec((TM, D_MODEL), lambda s: (group(s), 0)),
        pl.BlockSpec((1, 6, D_MODEL), lambda s: (mod_row(group(s)), 0, 0)),
    ]
    if P:
        args += list(cache)
        in_specs += [pl.BlockSpec((P * N_KV_HEADS, HEAD_DIM), lambda s: (group(s), 0))] * 2
    if use_rope:
        args += list(rope_tabs)
        in_specs += [_resident((L, HEAD_DIM))] * 3
    args += list(weights)
    in_specs += [_resident(w.shape) for w in weights]
    n_steps = T // TM
    def per_group(shape):
        assert shape[0] % n_steps == 0
        blk = (shape[0] // n_steps,) + shape[1:]
        return pl.BlockSpec(blk, lambda s, n=len(blk): (group(s),) + (0,) * (n - 1))

    cast_out_shapes = [ws[0].shape[:-1] + (sum(w.shape[-1] for w in ws),) for ws in cast]
    for ws in cast:
        args += list(ws)
        in_specs += [per_group(w.shape) for w in ws]

    out_shape = [jax.ShapeDtypeStruct((T, D_MODEL), F32), jax.ShapeDtypeStruct(_tiles_shape(T, PACKED_CHUNKS), U32),
                 jax.ShapeDtypeStruct((T, LANES), F32),
                 jax.ShapeDtypeStruct((T, LANES), BF16)]
    out_specs = [pl.BlockSpec((step_rows, D_MODEL), lambda s: (mixed(s), 0)),
                 _tiles_spec(step_rows, prepared, PACKED_CHUNKS),
                 pl.BlockSpec((step_rows, LANES), lambda s: (prepared(s), 0)),
                 pl.BlockSpec((step_rows, LANES), lambda s: (prepared(s), 0))]
    if emit_kv:
        out_shape += [jax.ShapeDtypeStruct((T * N_KV_HEADS, HEAD_DIM), F32)] * 2
        out_specs += [pl.BlockSpec((TM * N_KV_HEADS, HEAD_DIM), lambda s: (group(s), 0))] * 2
    out_shape += [jax.ShapeDtypeStruct(shp, BF16) for shp in cast_out_shapes]
    out_specs += [per_group(shp) for shp in cast_out_shapes]

    scratch = [
        pltpu.VMEM((N_HEADS, TM, HEAD_DIM), BF16),
        pltpu.VMEM((S, Lk, KV_W), BF16),
        pltpu.VMEM((S, Lk, KV_W), BF16),
        pltpu.VMEM((S, L + 2 * POOL_HALO, POOL_W), F32),
        pltpu.VMEM((TM, D_MODEL), BF16),
        pltpu.VMEM((blocks_per_step, ROW_BLOCK, ATTN_W), BF16),
        pltpu.VMEM((2, step_rows, D_MODEL), F32),
        pltpu.VMEM((2, 2, D_MODEL), F32),
    ]
    kern = functools.partial(_mix_kernel, S=S, L=L, P=P, use_rope=use_rope, emit_kv=emit_kv,
                             n_cast=tuple(len(ws) for ws in cast), n_blocks=n_blocks, U=blocks_per_step)
    return pl.pallas_call(
        kern,
        grid=(n_mix_steps + 1,),
        in_specs=in_specs,
        out_specs=out_specs,
        out_shape=out_shape,
        scratch_shapes=scratch,
        compiler_params=pltpu.CompilerParams(
            dimension_semantics=("arbitrary",), vmem_limit_bytes=V7X_VMEM_LIMIT_BYTES),
        name="mixer_rope" if use_rope else "mixer_ctx",
    )(*args)


def _plan_kernel(oh_ref, dest_ref, meta_ref, *, n_blocks, tile):
    TB = TOKEN_BLOCK
    lane = lax.broadcasted_iota(I32, (SUBLANES, LANES), 1)

    def count(b, acc):
        oh = oh_ref[pl.ds(pl.multiple_of(b * TB, TB), TB), :].astype(F32)
        return acc + jnp.sum(oh, axis=0, keepdims=True)

    counts = lax.fori_loop(0, n_blocks, count, jnp.zeros((SUBLANES, LANES), F32))
    padded = jnp.floor((counts + (tile - 0.5)) * (1.0 / tile)) * tile
    ends = padded
    step = 1
    while step < LANES:
        ends = ends + jnp.where(lane >= step, pltpu.roll(ends, step, 1), 0.0)
        step *= 2
    starts = ends - padded

    tri = jnp.where(lax.broadcasted_iota(I32, (TB, TB), 1) < lax.broadcasted_iota(I32, (TB, TB), 0),
                    1.0, 0.0).astype(BF16)

    def place(b, seen):
        oh = oh_ref[pl.ds(pl.multiple_of(b * TB, TB), TB), :]
        ohf = oh.astype(F32)
        rank = jnp.dot(tri, oh, preferred_element_type=F32)
        base = (starts + seen)[0:1, :]
        d = jnp.sum(ohf * (rank + base), axis=1, keepdims=True)
        dest_ref[b] = _row(d).astype(I32)
        return seen + jnp.sum(ohf, axis=0, keepdims=True)

    lax.fori_loop(0, n_blocks, place, jnp.zeros((SUBLANES, LANES), F32))

    tile_row0 = lax.broadcasted_iota(I32, (LANES, LANES), 0).astype(F32) * tile
    is_bucket = lax.broadcasted_iota(I32, (LANES, LANES), 1) < N_BUCKETS
    done = jnp.sum(jnp.where(is_bucket, jnp.where(ends[0:1, :] <= tile_row0, 1.0, 0.0), 0.0),
                   axis=1, keepdims=True)
    bkt = jnp.minimum(done, N_BUCKETS - 1.0)
    grp = (jnp.where(bkt >= PAIRS_PER_GROUP, 1.0, 0.0) + jnp.where(bkt >= 2 * PAIRS_PER_GROUP, 1.0, 0.0)
           + jnp.where(bkt >= 3 * PAIRS_PER_GROUP, 1.0, 0.0))
    pair = bkt - PAIRS_PER_GROUP * grp
    a = jnp.where(pair >= 3.0, 1.0, 0.0) + jnp.where(pair >= 5.0, 1.0, 0.0)
    b = pair - a * (7.0 - a) * 0.5 + a + 1.0
    e1 = EXP_PER_GROUP * grp + a
    e2 = EXP_PER_GROUP * grp + b
    meta = jnp.concatenate(
        [_row(e1), _row(e2), jnp.floor(ends[0:1, :] * (1.0 / tile) + 0.5),
         jnp.zeros((SUBLANES - 3, LANES), F32)], axis=0)
    meta_ref[...] = meta.astype(I32)


def _plan(onehot, tile):
    T = onehot.shape[0]
    n_blocks = T // TOKEN_BLOCK
    dest, meta = pl.pallas_call(
        functools.partial(_plan_kernel, n_blocks=n_blocks, tile=tile),
        out_shape=[jax.ShapeDtypeStruct((n_blocks, 1, TOKEN_BLOCK), I32),
                   jax.ShapeDtypeStruct((SUBLANES, LANES), I32)],
        name="moe_plan",
    )(onehot)
    return dest.reshape(T), meta


def _sc_move_rows(src_v, table_hbm, out_hbm, lo, n_rows, idx_v, pieces_v, sem):
    chunks = pieces_v.shape[0] // SC_ROWS_PER_STEP
    lane = lax.iota(I32, SC_LANES)
    row_in_group = lane & (SUBLANES - 1)
    chunk_in_pair = lane >> 3
    rows_per_gather = SC_PIECES_PER_GATHER // chunks

    @pl.loop(0, n_rows // SC_ROWS_PER_STEP)
    def _(step):
        copies = []
        for g in range(SC_ROWS_PER_STEP // rows_per_gather):
            r0 = step * SC_ROWS_PER_STEP + g * rows_per_gather
            for v in range(SC_PIECES_PER_GATHER // SC_LANES):
                group, chunk0 = v // (chunks // 2), 2 * (v % (chunks // 2))
                tok = plsc.load_gather(src_v, [r0 + group * SUBLANES + row_in_group])
                piece = (tok >> 3) * (SUBLANES * chunks) + (chunk0 + chunk_in_pair) * SUBLANES + (tok & 7)
                idx_v[pl.ds(g * SC_PIECES_PER_GATHER + v * SC_LANES, SC_LANES)] = piece
            window = pl.ds(g * SC_PIECES_PER_GATHER, SC_PIECES_PER_GATHER)
            copies.append(pltpu.async_copy(table_hbm.at[idx_v.at[window]], pieces_v.at[window], sem))
        for cp in copies:
            cp.wait()
        first = pl.multiple_of((lo + step * SC_ROWS_PER_STEP) * chunks, SC_ROWS_PER_STEP * chunks)
        pltpu.sync_copy(pieces_v, out_hbm.at[pl.ds(first, SC_ROWS_PER_STEP * chunks)])


def _sc_scratch(chunks, dtype):
    return [pltpu.VMEM((SC_ROWS_PER_STEP * chunks,), I32), pltpu.VMEM((SC_ROWS_PER_STEP * chunks, LANES), dtype)]


def _sc_dispatch(h2_flat, gate_rows, dest, n_rows):
    T = dest.shape[0]
    per_worker = n_rows // SC_WORKERS
    rows_per_step = SC_ROWS_PER_STEP
    chunks = h2_flat.shape[0] // T
    assert n_rows % SC_WORKERS == 0 and per_worker % rows_per_step == 0 and T % SC_LANES == 0
    mesh = plsc.VectorSubcoreMesh(core_axis_name="c", subcore_axis_name="s")

    @functools.partial(
        pl.kernel, mesh=mesh,
        out_type=[jax.ShapeDtypeStruct((n_rows * chunks, LANES), h2_flat.dtype),
                  jax.ShapeDtypeStruct((n_rows, LANES), F32)],
        scratch_types=[pltpu.VMEM((T,), I32), pltpu.VMEM((per_worker,), I32)]
        + _sc_scratch(chunks, h2_flat.dtype)
        + [pltpu.VMEM((rows_per_step, LANES), F32), pltpu.SemaphoreType.DMA, pltpu.SemaphoreType.DMA],
        compiler_params=pltpu.CompilerParams(use_tc_tiling_on_sc=True, needs_layout_passes=False),
        name="sc_dispatch",
    )
    def dispatch(h2_hbm, gate_hbm, dest_hbm, out_h_hbm, out_g_hbm,
                 dest_v, src_v, idx_v, pieces_v, gates_v, sem_h, sem_g):
        worker = lax.axis_index("s") * SC_CORES + lax.axis_index("c")
        lo = worker * per_worker
        pltpu.sync_copy(dest_hbm, dest_v)

        @pl.loop(0, per_worker // SC_LANES)
        def _(j):
            j0 = pl.multiple_of(j * SC_LANES, SC_LANES)
            src_v[pl.ds(j0, SC_LANES)] = lax.rem(lo + j0 + lax.iota(I32, SC_LANES), T)

        @pl.loop(0, T // SC_LANES)
        def _(j):
            t0 = pl.multiple_of(j * SC_LANES, SC_LANES)
            d = dest_v[pl.ds(t0, SC_LANES)] - lo
            mine = (d >= 0) & (d < per_worker)
            plsc.store_scatter(src_v, [jnp.where(mine, d, 0)], t0 + lax.iota(I32, SC_LANES), mask=mine)

        @pl.loop(0, per_worker // rows_per_step)
        def _(j):
            off = pl.multiple_of(j * rows_per_step, rows_per_step)
            pltpu.async_copy(gate_hbm.at[src_v.at[pl.ds(off, rows_per_step)]], gates_v, sem_g).wait()
            pltpu.sync_copy(gates_v, out_g_hbm.at[pl.ds(lo + off, rows_per_step)])

        _sc_move_rows(src_v, h2_hbm, out_h_hbm, lo, per_worker, idx_v, pieces_v, sem_h)

    return dispatch(h2_flat, gate_rows, dest)


def _expert_kernel(meta, x_ref, gv_ref, wgu_ref, wd_ref, o_ref, *, tile, per_step):
    groups = tile // SUBLANES
    n_used = meta[2, LANES - 1]

    def one_tile(k, carry):
        t = pl.program_id(0) * per_step + k
        rows = pl.ds(pl.multiple_of(k * groups, groups), groups)

        @pl.when(t < n_used)
        def _():
            x = _unpack_bf16_pairs(_load_tiles(x_ref.at[rows]))
            gv = gv_ref[pl.ds(pl.multiple_of(k * tile, SUBLANES), tile), :]
            lane = lax.broadcasted_iota(I32, gv.shape, 1)
            out = None
            for e in (meta[0, t], meta[1, t]):
                ge = jnp.sum(jnp.where(lane == EXPERT_LANE0 + e, gv, 0.0), axis=-1, keepdims=True)
                h = jnp.dot(x, wgu_ref[e], preferred_element_type=F32)
                hg = h[:, 0:D_EXPERT]
                hid = (hg * _sigmoid(hg) * h[:, D_EXPERT:2 * D_EXPERT] * ge).astype(BF16)
                y = jnp.dot(hid, wd_ref[e], preferred_element_type=F32)
                out = y if out is None else out + y
            _store_tiles(o_ref.at[rows], _pack_bf16_pairs(out))

        return carry

    lax.fori_loop(0, per_step, one_tile, 0)


def _experts(sorted_h2, sorted_gates, meta, wgu, wd, tiling):
    tile, per_step, n_tiles = tiling
    step_rows = tile * per_step
    assert n_tiles * tile == sorted_h2.shape[0] * SUBLANES and n_tiles % per_step == 0

    def last_used(i, meta):
        return jnp.minimum(i, (meta[2, LANES - 1] - 1) // per_step)

    return pl.pallas_call(
        functools.partial(_expert_kernel, tile=tile, per_step=per_step),
        grid_spec=pltpu.PrefetchScalarGridSpec(
            num_scalar_prefetch=1,
            grid=(n_tiles // per_step,),
            in_specs=[
                _tiles_spec(step_rows, last_used, PACKED_CHUNKS),
                pl.BlockSpec((step_rows, LANES), lambda *a: (last_used(*a), 0)),
                _resident(wgu.shape), _resident(wd.shape),
            ],
            out_specs=_tiles_spec(step_rows, last_used, PACKED_CHUNKS),
        ),
        out_shape=jax.ShapeDtypeStruct(_tiles_shape(n_tiles * tile, PACKED_CHUNKS), U32),
        compiler_params=pltpu.CompilerParams(
            dimension_semantics=("arbitrary",), vmem_limit_bytes=V7X_VMEM_LIMIT_BYTES),
        name="moe_experts",
    )(meta, sorted_h2, sorted_gates, wgu, wd)


def _sc_row_gather(table_flat, idx, chunks):
    n = idx.shape[0]
    per_worker = n // SC_WORKERS
    assert n % SC_WORKERS == 0 and per_worker % SC_ROWS_PER_STEP == 0
    mesh = plsc.VectorSubcoreMesh(core_axis_name="c", subcore_axis_name="s")

    @functools.partial(
        pl.kernel, mesh=mesh,
        out_type=jax.ShapeDtypeStruct((n * chunks, LANES), table_flat.dtype),
        scratch_types=[pltpu.VMEM((per_worker,), I32)] + _sc_scratch(chunks, table_flat.dtype)
        + [pltpu.SemaphoreType.DMA],
        compiler_params=pltpu.CompilerParams(use_tc_tiling_on_sc=True, needs_layout_passes=False),
        name="sc_row_gather",
    )
    def gather(table_hbm, idx_hbm, out_hbm, src_v, idx_v, pieces_v, sem):
        worker = lax.axis_index("s") * SC_CORES + lax.axis_index("c")
        lo = worker * per_worker
        pltpu.sync_copy(idx_hbm.at[pl.ds(lo, per_worker)], src_v)
        _sc_move_rows(src_v, table_hbm, out_hbm, lo, per_worker, idx_v, pieces_v, sem)

    return gather(table_flat, idx)


def _final_kernel(x_ref, moe_ref, mod_ref, gf_ref, o_ref):
    y = x_ref[...] + mod_ref[0, 5:6, :] * _unpack_bf16_pairs(_load_tiles(moe_ref)).astype(F32)
    o_ref[...] = _rms(y) * gf_ref[...]


def _final(xmid, moe_rows, mod, mod_row, gf):
    T = xmid.shape[0]
    return pl.pallas_call(
        _final_kernel,
        grid=(T // FINAL_BLOCK,),
        in_specs=[
            pl.BlockSpec((FINAL_BLOCK, D_MODEL), lambda i: (i, 0)),
            _tiles_spec(FINAL_BLOCK, lambda i: i, PACKED_CHUNKS),
            pl.BlockSpec((1, 6, D_MODEL), lambda i: (mod_row(i), 0, 0)),
            pl.BlockSpec((1, D_MODEL), lambda i: (0, 0)),
        ],
        out_specs=pl.BlockSpec((FINAL_BLOCK, D_MODEL), lambda i: (i, 0)),
        out_shape=jax.ShapeDtypeStruct((T, D_MODEL), F32),
        compiler_params=pltpu.CompilerParams(
            dimension_semantics=("arbitrary",), vmem_limit_bytes=V7X_VMEM_LIMIT_BYTES),
        name="moe_final",
    )(xmid, moe_rows, mod, gf)


def _flat(tiles):
    return tiles.reshape(-1, LANES)


def _expert_tiling(T):
    tile = -(-(T * 9) // (8 * N_BUCKETS * 64)) * 64
    per_step = max(1, EXPERT_STEP_ROWS // tile)
    n_tiles = (T + N_BUCKETS * (tile - 1)) // tile
    while n_tiles % per_step or (n_tiles * tile) % (SC_WORKERS * SC_ROWS_PER_STEP):
        n_tiles += 1
    return tile, per_step, n_tiles


def _moe_dispatch(h2_tiles, gate_rows, onehot, tiling):
    T = gate_rows.shape[0]
    tile, _, n_tiles = tiling
    n_rows = n_tiles * tile
    assert n_tiles <= LANES and T % TOKEN_BLOCK == 0
    dest, meta = _plan(onehot, tile)
    sorted_h2, sorted_gates = _sc_dispatch(_flat(h2_tiles), gate_rows, dest, n_rows)
    return sorted_h2.reshape(_tiles_shape(n_rows, PACKED_CHUNKS)), sorted_gates, dest, meta


def _moe_unpermute(moe_sorted_tiles, dest):
    chunks = moe_sorted_tiles.shape[1]
    return _sc_row_gather(_flat(moe_sorted_tiles), dest, chunks).reshape(_tiles_shape(dest.shape[0], chunks))


def _rope_tables(n_tokens):
    t = np.arange(n_tokens)
    row = (t // GRID_W).astype(np.float32)
    col = (t % GRID_W).astype(np.float32)
    freq = np.float32(ROPE_THETA) ** (-np.arange(ROPE_NF, dtype=np.float32) / np.float32(ROPE_NF))
    ang = np.concatenate([row[:, None] * freq] * 2 + [col[:, None] * freq] * 2, axis=-1)
    first = (np.arange(HEAD_DIM) % (2 * ROPE_NF)) < ROPE_NF
    sin = np.sin(ang)
    zero = np.float32(0.0)
    return (jnp.asarray(np.cos(ang)), jnp.asarray(np.where(first, -sin, zero)),
            jnp.asarray(np.where(first, zero, sin)))


def kernel(x_prompt, x_sample, cache_k, cache_v, c, c_ctx, norm1_g, norm2_g, w_ada, b_ada, w_in, q_norm_g, k_norm_g, w_pool, pool_scale, w_branch_a, w_branch_b, w_out, w_router_group, w_router_expert, w_exp_gate, w_exp_up, w_exp_down, final_norm_g):
    assert norm1_g.shape[0] == 1, "single-layer trunk"
    B, L_ctx, _ = x_prompt.shape
    Bs, L_lat, _ = x_sample.shape
    P = cache_k.shape[2]
    assert 1 + Bs <= COND_ROWS

    cond = jnp.concatenate([c_ctx[None, :], c, jnp.zeros((COND_ROWS - 1 - Bs, D_MODEL), F32)], axis=0)
    wpool_bd = jax.scipy.linalg.block_diag(*[w_pool[0, g] for g in range(len(POOL_WINDOWS))])
    mod, w_in_b, wpool_b, wa_b, wb_b, wo_b = _ada(
        cond, w_ada[0], b_ada[0][None, :],
        cast=(w_in[0], wpool_bd, w_branch_a[0], w_branch_b[0], w_out[0]))
    mod = mod.reshape(COND_ROWS, 6, D_MODEL)

    wr = jnp.concatenate([w_router_group[0], w_router_expert[0],
                          jnp.zeros((D_MODEL, LANES - N_EXP_GROUPS - N_EXPERTS), F32)], axis=1)
    wr_hi = wr.astype(BF16)
    wr_lo = (wr - wr_hi.astype(F32)).astype(BF16)
    mix_w = (norm1_g[0][None, :], w_in_b, q_norm_g[0][None, :], k_norm_g[0][None, :],
             wpool_b, pool_scale[0][None, :], wa_b, wb_b, wo_b,
             norm2_g[0][None, :], jnp.concatenate([wr_hi, wr_lo], axis=1))
    gf = final_norm_g[None, :]

    xp2 = x_prompt.reshape(B * L_ctx, D_MODEL)
    xmid_p, h2_p, gate_p, oh_p, knew, vnew, wgu, wd = _mix(
        xp2, mod, lambda i: 0, None, None, mix_w, S=2, L=L_ctx, emit_kv=True, blocks_per_step=2,
        cast=((w_exp_gate[0], w_exp_up[0]), (w_exp_down[0],)))
    tiling_p = _expert_tiling(B * L_ctx)
    sh_p, sg_p, dest_p, meta_p = _moe_dispatch(h2_p, gate_p, oh_p, tiling_p)

    xs2 = x_sample.reshape(Bs * L_lat, D_MODEL)
    cache = (cache_k.reshape(Bs * P * N_KV_HEADS, HEAD_DIM), cache_v.reshape(Bs * P * N_KV_HEADS, HEAD_DIM))
    xmid_s, h2_s, gate_s, oh_s = _mix(xs2, mod, lambda i: 1 + i, cache, _rope_tables(L_lat), mix_w,
                                      S=1, L=L_lat, emit_kv=False, blocks_per_step=1)
    tiling_s = _expert_tiling(Bs * L_lat)
    sh_s, sg_s, dest_s, meta_s = _moe_dispatch(h2_s, gate_s, oh_s, tiling_s)

    moe_p = _moe_unpermute(_experts(sh_p, sg_p, meta_p, wgu, wd, tiling_p), dest_p)
    moe_s = _moe_unpermute(_experts(sh_s, sg_s, meta_s, wgu, wd, tiling_s), dest_s)
    y_prompt = _final(xmid_p, moe_p, mod, lambda i: 0, gf)
    blocks_per_seq = L_lat // FINAL_BLOCK
    y_sample = _final(xmid_s, moe_s, mod, lambda i: 1 + i // blocks_per_seq, gf)

    return (y_prompt.reshape(B, L_ctx, D_MODEL), y_sample.reshape(Bs, L_lat, D_MODEL),
            knew.reshape(B, 1, L_ctx, N_KV_HEADS, HEAD_DIM), vnew.reshape(B, 1, L_ctx, N_KV_HEADS, HEAD_DIM))
```

```python
import functools

import numpy as np
import jax
import jax.numpy as jnp
from jax import lax
from jax.experimental import pallas as pl
from jax.experimental.pallas import tpu as pltpu
from jax.experimental.pallas import tpu_sc as plsc

F32 = jnp.float32
BF16 = jnp.bfloat16
I32 = jnp.int32
U32 = jnp.uint32

D_MODEL = 1024
HEAD_DIM = 128
N_HEADS = 8
N_KV_HEADS = 2
GROUP = N_HEADS // N_KV_HEADS
ATTN_W = N_HEADS * HEAD_DIM
KV_W = N_KV_HEADS * HEAD_DIM
POOL_WINDOWS = (2, 4, 8, 16)
POOL_GC = 128
POOL_W = POOL_GC * len(POOL_WINDOWS)
IN_W = ATTN_W + 2 * KV_W + POOL_W + 2 * D_MODEL
GATE_COL = ATTN_W + 2 * KV_W + POOL_W
GRID_W = 64
ROPE_THETA = 10000.0
ROPE_NF = HEAD_DIM // 4
N_EXP_GROUPS = 4
EXP_PER_GROUP = 4
N_EXPERTS = 16
D_EXPERT = 256
EPS = 1e-6
LOG2_E = 1.4426950408889634

LANES = 128
SUBLANES = 8
COND_ROWS = SUBLANES
POOL_HALO = 8
ROW_BLOCK = 256
ADA_COLS = 768
EXPERT_LANE0 = N_EXP_GROUPS
PAIRS_PER_GROUP = EXP_PER_GROUP * (EXP_PER_GROUP - 1) // 2
N_BUCKETS = N_EXP_GROUPS * PAIRS_PER_GROUP
EXPERT_STEP_ROWS = 1536
TOKEN_BLOCK = 1024
FINAL_BLOCK = 1024
ROW_CHUNKS = D_MODEL // LANES
SC_CORES = 2
SC_SUBCORES = 16
SC_WORKERS = SC_CORES * SC_SUBCORES
SC_LANES = 16
SC_PIECES_PER_GATHER = 128
SC_ROWS_PER_STEP = 64
PACKED_CHUNKS = ROW_CHUNKS // 2
V7X_VMEM_LIMIT_BYTES = 56 * 1024 * 1024


def _sigmoid(x):
    return 1.0 / (1.0 + jnp.exp(-x))


def _rms(x):
    return x * lax.rsqrt(jnp.mean(x * x, axis=-1, keepdims=True) + EPS)


def _resident(shape):
    zeros = (0,) * len(shape)
    return pl.BlockSpec(shape, lambda i, *_: zeros, pipeline_mode=pl.Buffered(1))


def _tiles_shape(n, chunks=ROW_CHUNKS):
    return (n // SUBLANES, chunks, SUBLANES, LANES)


def _tiles_spec(n, block_index, chunks=ROW_CHUNKS):
    return pl.BlockSpec(_tiles_shape(n, chunks), lambda *a: (block_index(*a), 0, 0, 0))


def _store_tiles(ref, x):
    for c in range(ref.shape[1]):
        ref[:, c, :, :] = x[:, c * LANES:(c + 1) * LANES].reshape(x.shape[0] // SUBLANES, SUBLANES, LANES)


def _load_tiles(ref):
    n = ref.shape[0] * SUBLANES
    return jnp.concatenate([ref[:, c, :, :].reshape(n, LANES) for c in range(ref.shape[1])], axis=1)


def _pack_bf16_pairs(x):
    bits = pltpu.bitcast(x.astype(BF16).astype(F32), U32)
    w = x.shape[1] // 2
    return bits[:, :w] | (bits[:, w:] >> 16)


def _unpack_bf16_pairs(words):
    hi = pltpu.bitcast(words & jnp.uint32(0xFFFF0000), F32).astype(BF16)
    lo = pltpu.bitcast(words << 16, F32).astype(BF16)
    return jnp.concatenate([hi, lo], axis=1)


def _row(x):
    return jnp.transpose(jnp.broadcast_to(x, (x.shape[0], LANES)))[0:1, :]


def _ada_kernel(c_ref, w_ref, b_ref, *refs):
    n_cast = (len(refs) - 1) // 2
    c = c_ref[...]
    s = (c * _sigmoid(c)).astype(BF16)
    refs[n_cast][...] = jnp.dot(s, w_ref[...].astype(BF16), preferred_element_type=F32) + b_ref[...]
    for src, dst in zip(refs[:n_cast], refs[n_cast + 1:]):
        dst[...] = src[...].astype(BF16)


def _ada(cond, w_ada, b_ada, cast=()):
    n = w_ada.shape[1]
    n_steps = n // ADA_COLS
    cast_specs = []
    for w in cast:
        assert w.ndim == 2 and w.shape[0] % (n_steps * 2 * SUBLANES) == 0
        cast_specs.append(pl.BlockSpec((w.shape[0] // n_steps, w.shape[1]), lambda j: (j, 0)))
    return pl.pallas_call(
        _ada_kernel,
        grid=(n_steps,),
        in_specs=[
            pl.BlockSpec((COND_ROWS, D_MODEL), lambda j: (0, 0)),
            pl.BlockSpec((D_MODEL, ADA_COLS), lambda j: (0, j)),
            pl.BlockSpec((1, ADA_COLS), lambda j: (0, j)),
        ] + cast_specs,
        out_specs=[pl.BlockSpec((COND_ROWS, ADA_COLS), lambda j: (0, j))] + cast_specs,
        out_shape=[jax.ShapeDtypeStruct((COND_ROWS, n), F32)] + [jax.ShapeDtypeStruct(w.shape, BF16) for w in cast],
        name="ada_mod",
    )(cond, w_ada, b_ada, *cast)


def _route(logits):
    lane = lax.broadcasted_iota(I32, logits.shape, 1).astype(F32)
    neg = jnp.float32(-1e30)
    far = jnp.float32(LANES)
    is_g = lane < N_EXP_GROUPS
    gl = jnp.where(is_g, logits, neg)
    gmax = jnp.max(gl, axis=-1, keepdims=True)
    gsel = jnp.min(jnp.where(gl == gmax, lane, far), axis=-1, keepdims=True)
    psel = 1.0 / jnp.sum(jnp.where(is_g, jnp.exp(gl - gmax), 0.0), axis=-1, keepdims=True)
    e_lo = EXPERT_LANE0 + EXP_PER_GROUP * gsel
    el = jnp.where(lane >= e_lo, jnp.where(lane < e_lo + EXP_PER_GROUP, logits, neg), neg)
    v1 = jnp.max(el, axis=-1, keepdims=True)
    i1 = jnp.min(jnp.where(el == v1, lane, far), axis=-1, keepdims=True)
    el2 = jnp.where(lane == i1, neg, el)
    v2 = jnp.max(el2, axis=-1, keepdims=True)
    i2 = jnp.min(jnp.where(el2 == v2, jnp.where(lane == i1, far, lane), far), axis=-1, keepdims=True)
    e2 = jnp.exp(v2 - v1)
    w1 = psel / (1.0 + e2)
    w2 = psel * e2 / (1.0 + e2)
    gate = jnp.where(lane == i1, w1, jnp.where(lane == i2, w2, 0.0))
    a = jnp.minimum(i1, i2) - e_lo
    b = jnp.maximum(i1, i2) - e_lo
    pair = a * (7.0 - a) * 0.5 + (b - a - 1.0)
    return gate, gsel * PAIRS_PER_GROUP + pair


def _mix_kernel(*refs, S, L, P, use_rope, emit_kv, n_cast, n_blocks, U):
    it = iter(refs)
    x_ref = next(it)
    mod_ref = next(it)
    if P:
        ck_ref = next(it)
        cv_ref = next(it)
    if use_rope:
        cos_ref = next(it)
        sneg_ref = next(it)
        spos_ref = next(it)
    (g1_ref, win_ref, qg_ref, kg_ref, wpool_ref, pscale_ref, wa_ref, wb_ref, wo_ref,
     g2_ref, wr_ref) = (next(it) for _ in range(11))
    cast_in = [[next(it) for _ in range(n)] for n in n_cast]
    xmid_ref = next(it)
    h2_ref = next(it)
    gate_ref = next(it)
    oh_ref = next(it)
    if emit_kv:
        knew_ref = next(it)
        vnew_ref = next(it)
    cast_out = [next(it) for _ in n_cast]
    q_s, k_s, v_s, xp_s, h_s, attn_s, xm_s, mod2_s = (next(it) for _ in range(8))

    TM = S * L
    RB = ROW_BLOCK
    nrb = TM // RB
    n_steps = n_blocks // U
    score_gain = HEAD_DIM ** -0.5 * LOG2_E
    step = pl.program_id(0)
    block0 = U * jnp.minimum(step, n_steps - 1)
    slot = step % 2

    sh1 = mod_ref[0, 0:1, :]
    gain1 = g1_ref[...] * (1.0 + mod_ref[0, 1:2, :])
    gt1 = mod_ref[0, 2:3, :]
    sh2 = mod_ref[0, 3:4, :]
    gain2 = g2_ref[...] * (1.0 + mod_ref[0, 4:5, :])
    qg = qg_ref[...] * score_gain
    kg = kg_ref[...]

    def project(r, carry):
        r0 = pl.multiple_of(r * RB, RB)
        s = r0 // L
        o = pl.multiple_of(r0 % L, RB)
        hb = (_rms(x_ref[pl.ds(r0, RB), :]) * gain1 + sh1).astype(BF16)
        h_s[pl.ds(r0, RB), :] = hb
        p1 = jnp.dot(hb, win_ref[:, 0:GATE_COL], preferred_element_type=F32)
        if use_rope:
            cs = cos_ref[pl.ds(o, RB), :]
            sn = sneg_ref[pl.ds(o, RB), :]
            sp = spos_ref[pl.ds(o, RB), :]

        def rope(t):
            return (t * cs + pltpu.roll(t, HEAD_DIM - ROPE_NF, 1) * sn + pltpu.roll(t, ROPE_NF, 1) * sp)

        for hd in range(N_HEADS):
            qh = _rms(p1[:, hd * HEAD_DIM:(hd + 1) * HEAD_DIM]) * qg
            if use_rope:
                qh = rope(qh)
            q_s[hd, pl.ds(r0, RB), :] = qh.astype(BF16)
        for kh in range(N_KV_HEADS):
            c0 = ATTN_W + kh * HEAD_DIM
            kk = _rms(p1[:, c0:c0 + HEAD_DIM]) * kg
            if emit_kv:
                knew_ref[pl.ds(N_KV_HEADS * r0 + kh, RB, stride=N_KV_HEADS), :] = kk
            if use_rope:
                kk = rope(kk)
            k_s[s, pl.ds(P + o, RB), kh * HEAD_DIM:(kh + 1) * HEAD_DIM] = kk.astype(BF16)
        vv = p1[:, ATTN_W + KV_W:ATTN_W + 2 * KV_W]
        if emit_kv:
            for kh in range(N_KV_HEADS):
                vnew_ref[pl.ds(N_KV_HEADS * r0 + kh, RB, stride=N_KV_HEADS), :] = (
                    vv[:, kh * HEAD_DIM:(kh + 1) * HEAD_DIM])
        v_s[s, pl.ds(P + o, RB), :] = vv.astype(BF16)
        xp_s[s, pl.ds(POOL_HALO + o, RB), :] = p1[:, ATTN_W + 2 * KV_W:GATE_COL]
        return carry

    @pl.when(step == 0)
    def _():
        xm_s[1] = jnp.zeros((U * RB, D_MODEL), F32)
        mod2_s[1] = jnp.zeros((2, D_MODEL), F32)

    @pl.when((step < n_steps) & (step % (nrb // U) == 0))
    def _():
        if P:
            for kh in range(N_KV_HEADS):
                cols = slice(kh * HEAD_DIM, (kh + 1) * HEAD_DIM)
                k_s[0, 0:P, cols] = ck_ref[pl.ds(kh, P, stride=N_KV_HEADS), :].astype(BF16)
                v_s[0, 0:P, cols] = cv_ref[pl.ds(kh, P, stride=N_KV_HEADS), :].astype(BF16)
        xp_s[:, 0:POOL_HALO, :] = jnp.zeros((S, POOL_HALO, POOL_W), F32)
        xp_s[:, L + POOL_HALO:L + 2 * POOL_HALO, :] = jnp.zeros((S, POOL_HALO, POOL_W), F32)
        lax.fori_loop(0, TM // RB, project, 0)
        for srcs, dst in zip(cast_in, cast_out):
            col = 0
            for src in srcs:
                dst[..., col:col + src.shape[-1]] = src[...].astype(BF16)
                col += src.shape[-1]

    def mix(u):
        r0 = pl.multiple_of(((block0 + u) % nrb) * RB, RB)
        s = r0 // L
        o = pl.multiple_of(r0 % L, RB)
        attn_u = attn_s.at[u]
        rows = slice(u * RB, (u + 1) * RB)

        for hd in range(N_HEADS):
            kh = hd // GROUP
            k = k_s[s, :, kh * HEAD_DIM:(kh + 1) * HEAD_DIM]
            v = v_s[s, :, kh * HEAD_DIM:(kh + 1) * HEAD_DIM]
            qh = q_s[hd, pl.ds(r0, RB), :]
            sc = lax.dot_general(qh, k, (((1,), (1,)), ((), ())), preferred_element_type=F32)
            e = jnp.exp2(sc - jnp.max(sc, axis=-1, keepdims=True))
            den = jnp.sum(e, axis=-1, keepdims=True)
            oh = jnp.dot(e.astype(BF16), v, preferred_element_type=F32) / den
            attn_u[:, hd * HEAD_DIM:(hd + 1) * HEAD_DIM] = oh.astype(BF16)
        a = jnp.dot(attn_u[...], wa_ref[...], preferred_element_type=F32)

        t = o + lax.broadcasted_iota(I32, (RB, 1), 0)
        RW = RB + 2 * POOL_HALO
        parts = []
        for gi, w in enumerate(POOL_WINDOWS):
            cols = slice(gi * POOL_GC, (gi + 1) * POOL_GC)
            xw = xp_s[s, pl.ds(o, RW), cols]
            run = xw
            span = 1
            while span < w:
                run = run + pltpu.roll(run, span, 0)
                span *= 2
            if w // 2 > 1:
                run = pltpu.roll(run, RW - (w // 2 - 1), 0)
            tot = run[POOL_HALO:POOL_HALO + RB]
            cnt = (jnp.minimum(t + w // 2, L) - jnp.maximum(t - w // 2, 0)).astype(F32)
            parts.append(tot / cnt - xw[POOL_HALO:POOL_HALO + RB])
        dpool = jnp.concatenate(parts, axis=1).astype(BF16)
        pooled = jnp.dot(dpool, wpool_ref[...], preferred_element_type=F32) * pscale_ref[...]
        b = jnp.dot(pooled.astype(BF16), wb_ref[...], preferred_element_type=F32)

        gates = jnp.dot(h_s[pl.ds(r0, RB), :], win_ref[:, GATE_COL:IN_W], preferred_element_type=F32)
        merged = _sigmoid(gates[:, 0:D_MODEL]) * a + _sigmoid(gates[:, D_MODEL:2 * D_MODEL]) * b
        upd = jnp.dot(merged.astype(BF16), wo_ref[...], preferred_element_type=F32)
        xm = x_ref[pl.ds(r0, RB), :] + gt1 * upd
        xmid_ref[rows, :] = xm
        xm_s[slot, rows, :] = xm

    def moe_prep(u):
        rows = slice(u * RB, (u + 1) * RB)
        h2 = _rms(xm_s[1 - slot, rows, :]) * mod2_s[1 - slot, 0:1, :] + mod2_s[1 - slot, 1:2, :]
        hi = h2.astype(BF16)
        lo = (h2 - hi.astype(F32)).astype(BF16)
        l1 = jnp.dot(hi, wr_ref[...], preferred_element_type=F32)
        l2 = jnp.dot(lo, wr_ref[:, 0:LANES], preferred_element_type=F32)
        gate, bucket = _route(l1[:, 0:LANES] + l1[:, LANES:2 * LANES] + l2)
        groups = pl.ds(u * (RB // SUBLANES), RB // SUBLANES)
        _store_tiles(h2_ref.at[groups], _pack_bf16_pairs(h2))
        gate_ref[rows, :] = gate
        lane = lax.broadcasted_iota(I32, (RB, LANES), 1).astype(F32)
        oh_ref[rows, :] = jnp.where(lane == bucket, 1.0, 0.0).astype(BF16)

    mod2_s[slot, 0:1, :] = gain2
    mod2_s[slot, 1:2, :] = sh2
    for u in range(U):
        moe_prep(u)
    for u in range(U):
        mix(u)


def _mix(x2d, mod, mod_row, cache, rope_tabs, weights, *, S, L, emit_kv, blocks_per_step, cast=()):
    T = x2d.shape[0]
    TM = S * L
    P = cache[0].shape[0] // (T // L * N_KV_HEADS) if cache is not None else 0
    use_rope = rope_tabs is not None
    assert T % TM == 0 and L % ROW_BLOCK == 0
    assert not (use_rope or P) or S == 1
    Lk = P + L

    args = [x2d, mod]
    nrb = TM // ROW_BLOCK
    n_blocks = T // ROW_BLOCK
    step_rows = blocks_per_step * ROW_BLOCK
    steps_per_group = nrb // blocks_per_step
    n_mix_steps = n_blocks // blocks_per_step
    assert nrb % blocks_per_step == 0

    def mixed(s):
        return jnp.minimum(s, n_mix_steps - 1)

    def group(s):
        return mixed(s) // steps_per_group

    def prepared(s):
        return jnp.maximum(s - 1, 0)

    in_specs = [
        pl.BlockSpec((TM, D_MODEL), lambda s: (group(s), 0)),
        pl.BlockSpec((1, 6, D_MODEL), lambda s: (mod_row(group(s)), 0, 0)),
    ]
    if P:
        args += list(cache)
        in_specs += [pl.BlockSpec((P * N_KV_HEADS, HEAD_DIM), lambda s: (group(s), 0))] * 2
    if use_rope:
        args += list(rope_tabs)
        in_specs += [_resident((L, HEAD_DIM))] * 3
    args += list(weights)
    in_specs += [_resident(w.shape) for w in weights]
    n_steps = T // TM
    def per_group(shape):
        assert shape[0] % n_steps == 0
        blk = (shape[0] // n_steps,) + shape[1:]
        return pl.BlockSpec(blk, lambda s, n=len(blk): (group(s),) + (0,) * (n - 1))

    cast_out_shapes = [ws[0].shape[:-1] + (sum(w.shape[-1] for w in ws),) for ws in cast]
    for ws in cast:
        args += list(ws)
        in_specs += [per_group(w.shape) for w in ws]

    out_shape = [jax.ShapeDtypeStruct((T, D_MODEL), F32), jax.ShapeDtypeStruct(_tiles_shape(T, PACKED_CHUNKS), U32),
                 jax.ShapeDtypeStruct((T, LANES), F32),
                 jax.ShapeDtypeStruct((T, LANES), BF16)]
    out_specs = [pl.BlockSpec((step_rows, D_MODEL), lambda s: (mixed(s), 0)),
                 _tiles_spec(step_rows, prepared, PACKED_CHUNKS),
                 pl.BlockSpec((step_rows, LANES), lambda s: (prepared(s), 0)),
                 pl.BlockSpec((step_rows, LANES), lambda s: (prepared(s), 0))]
    if emit_kv:
        out_shape += [jax.ShapeDtypeStruct((T * N_KV_HEADS, HEAD_DIM), F32)] * 2
        out_specs += [pl.BlockSpec((TM * N_KV_HEADS, HEAD_DIM), lambda s: (group(s), 0))] * 2
    out_shape += [jax.ShapeDtypeStruct(shp, BF16) for shp in cast_out_shapes]
    out_specs += [per_group(shp) for shp in cast_out_shapes]

    scratch = [
        pltpu.VMEM((N_HEADS, TM, HEAD_DIM), BF16),
        pltpu.VMEM((S, Lk, KV_W), BF16),
        pltpu.VMEM((S, Lk, KV_W), BF16),
        pltpu.VMEM((S, L + 2 * POOL_HALO, POOL_W), F32),
        pltpu.VMEM((TM, D_MODEL), BF16),
        pltpu.VMEM((blocks_per_step, ROW_BLOCK, ATTN_W), BF16),
        pltpu.VMEM((2, step_rows, D_MODEL), F32),
        pltpu.VMEM((2, 2, D_MODEL), F32),
    ]
    kern = functools.partial(_mix_kernel, S=S, L=L, P=P, use_rope=use_rope, emit_kv=emit_kv,
                             n_cast=tuple(len(ws) for ws in cast), n_blocks=n_blocks, U=blocks_per_step)
    return pl.pallas_call(
        kern,
        grid=(n_mix_steps + 1,),
        in_specs=in_specs,
        out_specs=out_specs,
        out_shape=out_shape,
        scratch_shapes=scratch,
        compiler_params=pltpu.CompilerParams(
            dimension_semantics=("arbitrary",), vmem_limit_bytes=V7X_VMEM_LIMIT_BYTES),
        name="mixer_rope" if use_rope else "mixer_ctx",
    )(*args)


def _plan_kernel(oh_ref, dest_ref, meta_ref, *, n_blocks, tile):
    TB = TOKEN_BLOCK
    lane = lax.broadcasted_iota(I32, (SUBLANES, LANES), 1)

    def count(b, acc):
        oh = oh_ref[pl.ds(pl.multiple_of(b * TB, TB), TB), :].astype(F32)
        return acc + jnp.sum(oh, axis=0, keepdims=True)

    counts = lax.fori_loop(0, n_blocks, count, jnp.zeros((SUBLANES, LANES), F32))
    padded = jnp.floor((counts + (tile - 0.5)) * (1.0 / tile)) * tile
    ends = padded
    step = 1
    while step < LANES:
        ends = ends + jnp.where(lane >= step, pltpu.roll(ends, step, 1), 0.0)
        step *= 2
    starts = ends - padded

    tri = jnp.where(lax.broadcasted_iota(I32, (TB, TB), 1) < lax.broadcasted_iota(I32, (TB, TB), 0),
                    1.0, 0.0).astype(BF16)

    def place(b, seen):
        oh = oh_ref[pl.ds(pl.multiple_of(b * TB, TB), TB), :]
        ohf = oh.astype(F32)
        rank = jnp.dot(tri, oh, preferred_element_type=F32)
        base = (starts + seen)[0:1, :]
        d = jnp.sum(ohf * (rank + base), axis=1, keepdims=True)
        dest_ref[b] = _row(d).astype(I32)
        return seen + jnp.sum(ohf, axis=0, keepdims=True)

    lax.fori_loop(0, n_blocks, place, jnp.zeros((SUBLANES, LANES), F32))

    tile_row0 = lax.broadcasted_iota(I32, (LANES, LANES), 0).astype(F32) * tile
    is_bucket = lax.broadcasted_iota(I32, (LANES, LANES), 1) < N_BUCKETS
    done = jnp.sum(jnp.where(is_bucket, jnp.where(ends[0:1, :] <= tile_row0, 1.0, 0.0), 0.0),
                   axis=1, keepdims=True)
    bkt = jnp.minimum(done, N_BUCKETS - 1.0)
    grp = (jnp.where(bkt >= PAIRS_PER_GROUP, 1.0, 0.0) + jnp.where(bkt >= 2 * PAIRS_PER_GROUP, 1.0, 0.0)
           + jnp.where(bkt >= 3 * PAIRS_PER_GROUP, 1.0, 0.0))
    pair = bkt - PAIRS_PER_GROUP * grp
    a = jnp.where(pair >= 3.0, 1.0, 0.0) + jnp.where(pair >= 5.0, 1.0, 0.0)
    b = pair - a * (7.0 - a) * 0.5 + a + 1.0
    e1 = EXP_PER_GROUP * grp + a
    e2 = EXP_PER_GROUP * grp + b
    meta = jnp.concatenate(
        [_row(e1), _row(e2), jnp.floor(ends[0:1, :] * (1.0 / tile) + 0.5),
         jnp.zeros((SUBLANES - 3, LANES), F32)], axis=0)
    meta_ref[...] = meta.astype(I32)


def _plan(onehot, tile):
    T = onehot.shape[0]
    n_blocks = T // TOKEN_BLOCK
    dest, meta = pl.pallas_call(
        functools.partial(_plan_kernel, n_blocks=n_blocks, tile=tile),
        out_shape=[jax.ShapeDtypeStruct((n_blocks, 1, TOKEN_BLOCK), I32),
                   jax.ShapeDtypeStruct((SUBLANES, LANES), I32)],
        name="moe_plan",
    )(onehot)
    return dest.reshape(T), meta


def _sc_move_rows(src_v, table_hbm, out_hbm, lo, n_rows, idx_v, pieces_v, sem):
    chunks = pieces_v.shape[0] // SC_ROWS_PER_STEP
    lane = lax.iota(I32, SC_LANES)
    row_in_group = lane & (SUBLANES - 1)
    chunk_in_pair = lane >> 3
    rows_per_gather = SC_PIECES_PER_GATHER // chunks

    @pl.loop(0, n_rows // SC_ROWS_PER_STEP)
    def _(step):
        copies = []
        for g in range(SC_ROWS_PER_STEP // rows_per_gather):
            r0 = step * SC_ROWS_PER_STEP + g * rows_per_gather
            for v in range(SC_PIECES_PER_GATHER // SC_LANES):
                group, chunk0 = v // (chunks // 2), 2 * (v % (chunks // 2))
                tok = plsc.load_gather(src_v, [r0 + group * SUBLANES + row_in_group])
                piece = (tok >> 3) * (SUBLANES * chunks) + (chunk0 + chunk_in_pair) * SUBLANES + (tok & 7)
                idx_v[pl.ds(g * SC_PIECES_PER_GATHER + v * SC_LANES, SC_LANES)] = piece
            window = pl.ds(g * SC_PIECES_PER_GATHER, SC_PIECES_PER_GATHER)
            copies.append(pltpu.async_copy(table_hbm.at[idx_v.at[window]], pieces_v.at[window], sem))
        for cp in copies:
            cp.wait()
        first = pl.multiple_of((lo + step * SC_ROWS_PER_STEP) * chunks, SC_ROWS_PER_STEP * chunks)
        pltpu.sync_copy(pieces_v, out_hbm.at[pl.ds(first, SC_ROWS_PER_STEP * chunks)])


def _sc_scratch(chunks, dtype):
    return [pltpu.VMEM((SC_ROWS_PER_STEP * chunks,), I32), pltpu.VMEM((SC_ROWS_PER_STEP * chunks, LANES), dtype)]


def _sc_dispatch(h2_flat, gate_rows, dest, n_rows):
    T = dest.shape[0]
    per_worker = n_rows // SC_WORKERS
    rows_per_step = SC_ROWS_PER_STEP
    chunks = h2_flat.shape[0] // T
    assert n_rows % SC_WORKERS == 0 and per_worker % rows_per_step == 0 and T % SC_LANES == 0
    mesh = plsc.VectorSubcoreMesh(core_axis_name="c", subcore_axis_name="s")

    @functools.partial(
        pl.kernel, mesh=mesh,
        out_type=[jax.ShapeDtypeStruct((n_rows * chunks, LANES), h2_flat.dtype),
                  jax.ShapeDtypeStruct((n_rows, LANES), F32)],
        scratch_types=[pltpu.VMEM((T,), I32), pltpu.VMEM((per_worker,), I32)]
        + _sc_scratch(chunks, h2_flat.dtype)
        + [pltpu.VMEM((rows_per_step, LANES), F32), pltpu.SemaphoreType.DMA, pltpu.SemaphoreType.DMA],
        compiler_params=pltpu.CompilerParams(use_tc_tiling_on_sc=True, needs_layout_passes=False),
        name="sc_dispatch",
    )
    def dispatch(h2_hbm, gate_hbm, dest_hbm, out_h_hbm, out_g_hbm,
                 dest_v, src_v, idx_v, pieces_v, gates_v, sem_h, sem_g):
        worker = lax.axis_index("s") * SC_CORES + lax.axis_index("c")
        lo = worker * per_worker
        pltpu.sync_copy(dest_hbm, dest_v)

        @pl.loop(0, per_worker // SC_LANES)
        def _(j):
            j0 = pl.multiple_of(j * SC_LANES, SC_LANES)
            src_v[pl.ds(j0, SC_LANES)] = lax.rem(lo + j0 + lax.iota(I32, SC_LANES), T)

        @pl.loop(0, T // SC_LANES)
        def _(j):
            t0 = pl.multiple_of(j * SC_LANES, SC_LANES)
            d = dest_v[pl.ds(t0, SC_LANES)] - lo
            mine = (d >= 0) & (d < per_worker)
            plsc.store_scatter(src_v, [jnp.where(mine, d, 0)], t0 + lax.iota(I32, SC_LANES), mask=mine)

        @pl.loop(0, per_worker // rows_per_step)
        def _(j):
            off = pl.multiple_of(j * rows_per_step, rows_per_step)
            pltpu.async_copy(gate_hbm.at[src_v.at[pl.ds(off, rows_per_step)]], gates_v, sem_g).wait()
            pltpu.sync_copy(gates_v, out_g_hbm.at[pl.ds(lo + off, rows_per_step)])

        _sc_move_rows(src_v, h2_hbm, out_h_hbm, lo, per_worker, idx_v, pieces_v, sem_h)

    return dispatch(h2_flat, gate_rows, dest)


def _expert_kernel(meta, x_ref, gv_ref, wgu_hbm, wd_hbm, o_ref, wgu_ref, wd_ref, ready_s, sems, *, tile, per_step):
    groups = tile // SUBLANES
    n_used = meta[2, LANES - 1]
    step = pl.program_id(0)

    def weight_copies(g):
        experts = pl.ds(g * EXP_PER_GROUP, EXP_PER_GROUP)
        return (pltpu.make_async_copy(wgu_hbm.at[experts], wgu_ref.at[experts], sems.at[g]),
                pltpu.make_async_copy(wd_hbm.at[experts], wd_ref.at[experts], sems.at[g]))

    def land_through(last_group):
        landed = ready_s[0]
        for g in range(N_EXP_GROUPS):
            @pl.when((g >= landed) & (g <= last_group))
            def _():
                for cp in weight_copies(g):
                    cp.wait()
                if g + 1 < N_EXP_GROUPS:
                    for cp in weight_copies(g + 1):
                        cp.start()
        ready_s[0] = jnp.maximum(landed, last_group + 1)

    @pl.when(step == 0)
    def _():
        ready_s[0] = 0
        for cp in weight_copies(0):
            cp.start()

    def one_tile(k, carry):
        t = step * per_step + k
        rows = pl.ds(pl.multiple_of(k * groups, groups), groups)

        @pl.when(t < n_used)
        def _():
            land_through(meta[0, t] // EXP_PER_GROUP)
            x = _unpack_bf16_pairs(_load_tiles(x_ref.at[rows]))
            gv = gv_ref[pl.ds(pl.multiple_of(k * tile, SUBLANES), tile), :]
            lane = lax.broadcasted_iota(I32, gv.shape, 1)
            out = None
            for e in (meta[0, t], meta[1, t]):
                ge = jnp.sum(jnp.where(lane == EXPERT_LANE0 + e, gv, 0.0), axis=-1, keepdims=True)
                h = jnp.dot(x, wgu_ref[e], preferred_element_type=F32)
                hg = h[:, 0:D_EXPERT]
                hid = (hg * _sigmoid(hg) * h[:, D_EXPERT:2 * D_EXPERT] * ge).astype(BF16)
                y = jnp.dot(hid, wd_ref[e], preferred_element_type=F32)
                out = y if out is None else out + y
            _store_tiles(o_ref.at[rows], _pack_bf16_pairs(out))

        return carry

    lax.fori_loop(0, per_step, one_tile, 0)

    @pl.when(step == pl.num_programs(0) - 1)
    def _():
        land_through(N_EXP_GROUPS - 1)


def _experts(sorted_h2, sorted_gates, meta, wgu, wd, tiling):
    tile, per_step, n_tiles = tiling
    step_rows = tile * per_step
    assert n_tiles * tile == sorted_h2.shape[0] * SUBLANES and n_tiles % per_step == 0

    def last_used(i, meta):
        return jnp.minimum(i, (meta[2, LANES - 1] - 1) // per_step)

    return pl.pallas_call(
        functools.partial(_expert_kernel, tile=tile, per_step=per_step),
        grid_spec=pltpu.PrefetchScalarGridSpec(
            num_scalar_prefetch=1,
            grid=(n_tiles // per_step,),
            in_specs=[
                _tiles_spec(step_rows, last_used, PACKED_CHUNKS),
                pl.BlockSpec((step_rows, LANES), lambda *a: (last_used(*a), 0)),
                pl.BlockSpec(memory_space=pl.ANY), pl.BlockSpec(memory_space=pl.ANY),
            ],
            out_specs=_tiles_spec(step_rows, last_used, PACKED_CHUNKS),
            scratch_shapes=[pltpu.VMEM(wgu.shape, wgu.dtype), pltpu.VMEM(wd.shape, wd.dtype),
                            pltpu.SMEM((1,), I32), pltpu.SemaphoreType.DMA((N_EXP_GROUPS,))],
        ),
        out_shape=jax.ShapeDtypeStruct(_tiles_shape(n_tiles * tile, PACKED_CHUNKS), U32),
        compiler_params=pltpu.CompilerParams(
            dimension_semantics=("arbitrary",), vmem_limit_bytes=V7X_VMEM_LIMIT_BYTES),
        name="moe_experts",
    )(meta, sorted_h2, sorted_gates, wgu, wd)


def _sc_row_gather(table_flat, idx, chunks):
    n = idx.shape[0]
    per_worker = n // SC_WORKERS
    assert n % SC_WORKERS == 0 and per_worker % SC_ROWS_PER_STEP == 0
    mesh = plsc.VectorSubcoreMesh(core_axis_name="c", subcore_axis_name="s")

    @functools.partial(
        pl.kernel, mesh=mesh,
        out_type=jax.ShapeDtypeStruct((n * chunks, LANES), table_flat.dtype),
        scratch_types=[pltpu.VMEM((per_worker,), I32)] + _sc_scratch(chunks, table_flat.dtype)
        + [pltpu.SemaphoreType.DMA],
        compiler_params=pltpu.CompilerParams(use_tc_tiling_on_sc=True, needs_layout_passes=False),
        name="sc_row_gather",
    )
    def gather(table_hbm, idx_hbm, out_hbm, src_v, idx_v, pieces_v, sem):
        worker = lax.axis_index("s") * SC_CORES + lax.axis_index("c")
        lo = worker * per_worker
        pltpu.sync_copy(idx_hbm.at[pl.ds(lo, per_worker)], src_v)
        _sc_move_rows(src_v, table_hbm, out_hbm, lo, per_worker, idx_v, pieces_v, sem)

    return gather(table_flat, idx)


def _final_kernel(x_ref, moe_ref, mod_ref, gf_ref, o_ref):
    y = x_ref[...] + mod_ref[0, 5:6, :] * _unpack_bf16_pairs(_load_tiles(moe_ref)).astype(F32)
    o_ref[...] = _rms(y) * gf_ref[...]


def _final(xmid, moe_rows, mod, mod_row, gf):
    T = xmid.shape[0]
    return pl.pallas_call(
        _final_kernel,
        grid=(T // FINAL_BLOCK,),
        in_specs=[
            pl.BlockSpec((FINAL_BLOCK, D_MODEL), lambda i: (i, 0)),
            _tiles_spec(FINAL_BLOCK, lambda i: i, PACKED_CHUNKS),
            pl.BlockSpec((1, 6, D_MODEL), lambda i: (mod_row(i), 0, 0)),
            pl.BlockSpec((1, D_MODEL), lambda i: (0, 0)),
        ],
        out_specs=pl.BlockSpec((FINAL_BLOCK, D_MODEL), lambda i: (i, 0)),
        out_shape=jax.ShapeDtypeStruct((T, D_MODEL), F32),
        compiler_params=pltpu.CompilerParams(
            dimension_semantics=("arbitrary",), vmem_limit_bytes=V7X_VMEM_LIMIT_BYTES),
        name="moe_final",
    )(xmid, moe_rows, mod, gf)


def _flat(tiles):
    return tiles.reshape(-1, LANES)


def _expert_tiling(T):
    tile = -(-(T * 9) // (8 * N_BUCKETS * 64)) * 64
    per_step = max(1, EXPERT_STEP_ROWS // tile)
    n_tiles = (T + N_BUCKETS * (tile - 1)) // tile
    while n_tiles % per_step or (n_tiles * tile) % (SC_WORKERS * SC_ROWS_PER_STEP):
        n_tiles += 1
    return tile, per_step, n_tiles


def _moe_dispatch(h2_tiles, gate_rows, onehot, tiling):
    T = gate_rows.shape[0]
    tile, _, n_tiles = tiling
    n_rows = n_tiles * tile
    assert n_tiles <= LANES and T % TOKEN_BLOCK == 0
    dest, meta = _plan(onehot, tile)
    sorted_h2, sorted_gates = _sc_dispatch(_flat(h2_tiles), gate_rows, dest, n_rows)
    return sorted_h2.reshape(_tiles_shape(n_rows, PACKED_CHUNKS)), sorted_gates, dest, meta


def _moe_unpermute(moe_sorted_tiles, dest):
    chunks = moe_sorted_tiles.shape[1]
    return _sc_row_gather(_flat(moe_sorted_tiles), dest, chunks).reshape(_tiles_shape(dest.shape[0], chunks))


def _rope_tables(n_tokens):
    t = np.arange(n_tokens)
    row = (t // GRID_W).astype(np.float32)
    col = (t % GRID_W).astype(np.float32)
    freq = np.float32(ROPE_THETA) ** (-np.arange(ROPE_NF, dtype=np.float32) / np.float32(ROPE_NF))
    ang = np.concatenate([row[:, None] * freq] * 2 + [col[:, None] * freq] * 2, axis=-1)
    first = (np.arange(HEAD_DIM) % (2 * ROPE_NF)) < ROPE_NF
    sin = np.sin(ang)
    zero = np.float32(0.0)
    return (jnp.asarray(np.cos(ang)), jnp.asarray(np.where(first, -sin, zero)),
            jnp.asarray(np.where(first, zero, sin)))


def kernel(x_prompt, x_sample, cache_k, cache_v, c, c_ctx, norm1_g, norm2_g, w_ada, b_ada, w_in, q_norm_g, k_norm_g, w_pool, pool_scale, w_branch_a, w_branch_b, w_out, w_router_group, w_router_expert, w_exp_gate, w_exp_up, w_exp_down, final_norm_g):
    assert norm1_g.shape[0] == 1, "single-layer trunk"
    B, L_ctx, _ = x_prompt.shape
    Bs, L_lat, _ = x_sample.shape
    P = cache_k.shape[2]
    assert 1 + Bs <= COND_ROWS

    cond = jnp.concatenate([c_ctx[None, :], c, jnp.zeros((COND_ROWS - 1 - Bs, D_MODEL), F32)], axis=0)
    wpool_bd = jax.scipy.linalg.block_diag(*[w_pool[0, g] for g in range(len(POOL_WINDOWS))])
    mod, w_in_b, wpool_b, wa_b, wb_b, wo_b = _ada(
        cond, w_ada[0], b_ada[0][None, :],
        cast=(w_in[0], wpool_bd, w_branch_a[0], w_branch_b[0], w_out[0]))
    mod = mod.reshape(COND_ROWS, 6, D_MODEL)

    wr = jnp.concatenate([w_router_group[0], w_router_expert[0],
                          jnp.zeros((D_MODEL, LANES - N_EXP_GROUPS - N_EXPERTS), F32)], axis=1)
    wr_hi = wr.astype(BF16)
    wr_lo = (wr - wr_hi.astype(F32)).astype(BF16)
    mix_w = (norm1_g[0][None, :], w_in_b, q_norm_g[0][None, :], k_norm_g[0][None, :],
             wpool_b, pool_scale[0][None, :], wa_b, wb_b, wo_b,
             norm2_g[0][None, :], jnp.concatenate([wr_hi, wr_lo], axis=1))
    gf = final_norm_g[None, :]

    xp2 = x_prompt.reshape(B * L_ctx, D_MODEL)
    xmid_p, h2_p, gate_p, oh_p, knew, vnew, wgu, wd = _mix(
        xp2, mod, lambda i: 0, None, None, mix_w, S=2, L=L_ctx, emit_kv=True, blocks_per_step=2,
        cast=((w_exp_gate[0], w_exp_up[0]), (w_exp_down[0],)))
    tiling_p = _expert_tiling(B * L_ctx)
    sh_p, sg_p, dest_p, meta_p = _moe_dispatch(h2_p, gate_p, oh_p, tiling_p)

    xs2 = x_sample.reshape(Bs * L_lat, D_MODEL)
    cache = (cache_k.reshape(Bs * P * N_KV_HEADS, HEAD_DIM), cache_v.reshape(Bs * P * N_KV_HEADS, HEAD_DIM))
    xmid_s, h2_s, gate_s, oh_s = _mix(xs2, mod, lambda i: 1 + i, cache, _rope_tables(L_lat), mix_w,
                                      S=1, L=L_lat, emit_kv=False, blocks_per_step=1)
    tiling_s = _expert_tiling(Bs * L_lat)
    sh_s, sg_s, dest_s, meta_s = _moe_dispatch(h2_s, gate_s, oh_s, tiling_s)

    moe_p = _moe_unpermute(_experts(sh_p, sg_p, meta_p, wgu, wd, tiling_p), dest_p)
    moe_s = _moe_unpermute(_experts(sh_s, sg_s, meta_s, wgu, wd, tiling_s), dest_s)
    y_prompt = _final(xmid_p, moe_p, mod, lambda i: 0, gf)
    blocks_per_seq = L_lat // FINAL_BLOCK
    y_sample = _final(xmid_s, moe_s, mod, lambda i: 1 + i // blocks_per_seq, gf)

    return (y_prompt.reshape(B, L_ctx, D_MODEL), y_sample.reshape(Bs, L_lat, D_MODEL),
            knew.reshape(B, 1, L_ctx, N_KV_HEADS, HEAD_DIM), vnew.reshape(B, 1, L_ctx, N_KV_HEADS, HEAD_DIM))
```

```python
import functools

import numpy as np
import jax
import jax.numpy as jnp
from jax import lax
from jax.experimental import pallas as pl
from jax.experimental.pallas import tpu as pltpu
from jax.experimental.pallas import tpu_sc as plsc

F32 = jnp.float32
BF16 = jnp.bfloat16
I32 = jnp.int32
U32 = jnp.uint32

D_MODEL = 1024
HEAD_DIM = 128
N_HEADS = 8
N_KV_HEADS = 2
GROUP = N_HEADS // N_KV_HEADS
ATTN_W = N_HEADS * HEAD_DIM
KV_W = N_KV_HEADS * HEAD_DIM
POOL_WINDOWS = (2, 4, 8, 16)
POOL_GC = 128
POOL_W = POOL_GC * len(POOL_WINDOWS)
IN_W = ATTN_W + 2 * KV_W + POOL_W + 2 * D_MODEL
GATE_COL = ATTN_W + 2 * KV_W + POOL_W
GRID_W = 64
ROPE_THETA = 10000.0
ROPE_NF = HEAD_DIM // 4
N_EXP_GROUPS = 4
EXP_PER_GROUP = 4
N_EXPERTS = 16
D_EXPERT = 256
EPS = 1e-6
LOG2_E = 1.4426950408889634

LANES = 128
SUBLANES = 8
COND_ROWS = SUBLANES
POOL_HALO = 8
ROW_BLOCK = 256
ADA_COLS = 768
EXPERT_LANE0 = N_EXP_GROUPS
PAIRS_PER_GROUP = EXP_PER_GROUP * (EXP_PER_GROUP - 1) // 2
N_BUCKETS = N_EXP_GROUPS * PAIRS_PER_GROUP
EXPERT_STEP_ROWS = 1536
TOKEN_BLOCK = 256
FINAL_BLOCK = 1024
ROW_CHUNKS = D_MODEL // LANES
SC_CORES = 2
SC_SUBCORES = 16
SC_WORKERS = SC_CORES * SC_SUBCORES
SC_LANES = 16
SC_PIECES_PER_GATHER = 128
SC_ROWS_PER_STEP = 64
PACKED_CHUNKS = ROW_CHUNKS // 2
V7X_VMEM_LIMIT_BYTES = 56 * 1024 * 1024


def _sigmoid(x):
    return 1.0 / (1.0 + jnp.exp(-x))


def _rms(x):
    return x * lax.rsqrt(jnp.mean(x * x, axis=-1, keepdims=True) + EPS)


def _resident(shape):
    zeros = (0,) * len(shape)
    return pl.BlockSpec(shape, lambda i, *_: zeros, pipeline_mode=pl.Buffered(1))


def _tiles_shape(n, chunks=ROW_CHUNKS):
    return (n // SUBLANES, chunks, SUBLANES, LANES)


def _tiles_spec(n, block_index, chunks=ROW_CHUNKS):
    return pl.BlockSpec(_tiles_shape(n, chunks), lambda *a: (block_index(*a), 0, 0, 0))


def _store_tiles(ref, x):
    for c in range(ref.shape[1]):
        ref[:, c, :, :] = x[:, c * LANES:(c + 1) * LANES].reshape(x.shape[0] // SUBLANES, SUBLANES, LANES)


def _load_tiles(ref):
    n = ref.shape[0] * SUBLANES
    return jnp.concatenate([ref[:, c, :, :].reshape(n, LANES) for c in range(ref.shape[1])], axis=1)


def _pack_bf16_pairs(x):
    bits = pltpu.bitcast(x.astype(BF16).astype(F32), U32)
    w = x.shape[1] // 2
    return bits[:, :w] | (bits[:, w:] >> 16)


def _unpack_bf16_pairs(words):
    hi = pltpu.bitcast(words & jnp.uint32(0xFFFF0000), F32).astype(BF16)
    lo = pltpu.bitcast(words << 16, F32).astype(BF16)
    return jnp.concatenate([hi, lo], axis=1)


def _row(x):
    return jnp.transpose(jnp.broadcast_to(x, (x.shape[0], LANES)))[0:1, :]


def _ada_kernel(c_ref, w_ref, b_ref, *refs):
    n_cast = (len(refs) - 1) // 2
    c = c_ref[...]
    s = (c * _sigmoid(c)).astype(BF16)
    refs[n_cast][...] = jnp.dot(s, w_ref[...].astype(BF16), preferred_element_type=F32) + b_ref[...]
    for src, dst in zip(refs[:n_cast], refs[n_cast + 1:]):
        dst[...] = src[...].astype(BF16)


def _ada(cond, w_ada, b_ada, cast=()):
    n = w_ada.shape[1]
    n_steps = n // ADA_COLS
    cast_specs = []
    for w in cast:
        assert w.ndim == 2 and w.shape[0] % (n_steps * 2 * SUBLANES) == 0
        cast_specs.append(pl.BlockSpec((w.shape[0] // n_steps, w.shape[1]), lambda j: (j, 0)))
    return pl.pallas_call(
        _ada_kernel,
        grid=(n_steps,),
        in_specs=[
            pl.BlockSpec((COND_ROWS, D_MODEL), lambda j: (0, 0)),
            pl.BlockSpec((D_MODEL, ADA_COLS), lambda j: (0, j)),
            pl.BlockSpec((1, ADA_COLS), lambda j: (0, j)),
        ] + cast_specs,
        out_specs=[pl.BlockSpec((COND_ROWS, ADA_COLS), lambda j: (0, j))] + cast_specs,
        out_shape=[jax.ShapeDtypeStruct((COND_ROWS, n), F32)] + [jax.ShapeDtypeStruct(w.shape, BF16) for w in cast],
        name="ada_mod",
    )(cond, w_ada, b_ada, *cast)


def _route(logits):
    lane = lax.broadcasted_iota(I32, logits.shape, 1).astype(F32)
    neg = jnp.float32(-1e30)
    far = jnp.float32(LANES)
    is_g = lane < N_EXP_GROUPS
    gl = jnp.where(is_g, logits, neg)
    gmax = jnp.max(gl, axis=-1, keepdims=True)
    gsel = jnp.min(jnp.where(gl == gmax, lane, far), axis=-1, keepdims=True)
    psel = 1.0 / jnp.sum(jnp.where(is_g, jnp.exp(gl - gmax), 0.0), axis=-1, keepdims=True)
    e_lo = EXPERT_LANE0 + EXP_PER_GROUP * gsel
    el = jnp.where(lane >= e_lo, jnp.where(lane < e_lo + EXP_PER_GROUP, logits, neg), neg)
    v1 = jnp.max(el, axis=-1, keepdims=True)
    i1 = jnp.min(jnp.where(el == v1, lane, far), axis=-1, keepdims=True)
    el2 = jnp.where(lane == i1, neg, el)
    v2 = jnp.max(el2, axis=-1, keepdims=True)
    i2 = jnp.min(jnp.where(el2 == v2, jnp.where(lane == i1, far, lane), far), axis=-1, keepdims=True)
    e2 = jnp.exp(v2 - v1)
    w1 = psel / (1.0 + e2)
    w2 = psel * e2 / (1.0 + e2)
    gate = jnp.where(lane == i1, w1, jnp.where(lane == i2, w2, 0.0))
    a = jnp.minimum(i1, i2) - e_lo
    b = jnp.maximum(i1, i2) - e_lo
    pair = a * (7.0 - a) * 0.5 + (b - a - 1.0)
    return gate, gsel * PAIRS_PER_GROUP + pair


def _mix_kernel(*refs, S, L, P, use_rope, emit_kv, n_cast, n_blocks, U):
    it = iter(refs)
    x_ref = next(it)
    mod_ref = next(it)
    if P:
        ck_ref = next(it)
        cv_ref = next(it)
    if use_rope:
        cos_ref = next(it)
        sneg_ref = next(it)
        spos_ref = next(it)
    (g1_ref, win_ref, qg_ref, kg_ref, wpool_ref, pscale_ref, wa_ref, wb_ref, wo_ref,
     g2_ref, wr_ref) = (next(it) for _ in range(11))
    cast_in = [[next(it) for _ in range(n)] for n in n_cast]
    xmid_ref = next(it)
    h2_ref = next(it)
    gate_ref = next(it)
    oh_ref = next(it)
    if emit_kv:
        knew_ref = next(it)
        vnew_ref = next(it)
    cast_out = [next(it) for _ in n_cast]
    q_s, k_s, v_s, xp_s, h_s, attn_s, xm_s, mod2_s = (next(it) for _ in range(8))

    TM = S * L
    RB = ROW_BLOCK
    nrb = TM // RB
    n_steps = n_blocks // U
    score_gain = HEAD_DIM ** -0.5 * LOG2_E
    step = pl.program_id(0)
    block0 = U * jnp.minimum(step, n_steps - 1)
    slot = step % 2

    sh1 = mod_ref[0, 0:1, :]
    gain1 = g1_ref[...] * (1.0 + mod_ref[0, 1:2, :])
    gt1 = mod_ref[0, 2:3, :]
    sh2 = mod_ref[0, 3:4, :]
    gain2 = g2_ref[...] * (1.0 + mod_ref[0, 4:5, :])
    qg = qg_ref[...] * score_gain
    kg = kg_ref[...]

    def project(r, carry):
        r0 = pl.multiple_of(r * RB, RB)
        s = r0 // L
        o = pl.multiple_of(r0 % L, RB)
        hb = (_rms(x_ref[pl.ds(r0, RB), :]) * gain1 + sh1).astype(BF16)
        h_s[pl.ds(r0, RB), :] = hb
        p1 = jnp.dot(hb, win_ref[:, 0:GATE_COL], preferred_element_type=F32)
        if use_rope:
            cs = cos_ref[pl.ds(o, RB), :]
            sn = sneg_ref[pl.ds(o, RB), :]
            sp = spos_ref[pl.ds(o, RB), :]

        def rope(t):
            return (t * cs + pltpu.roll(t, HEAD_DIM - ROPE_NF, 1) * sn + pltpu.roll(t, ROPE_NF, 1) * sp)

        for hd in range(N_HEADS):
            qh = _rms(p1[:, hd * HEAD_DIM:(hd + 1) * HEAD_DIM]) * qg
            if use_rope:
                qh = rope(qh)
            q_s[hd, pl.ds(r0, RB), :] = qh.astype(BF16)
        for kh in range(N_KV_HEADS):
            c0 = ATTN_W + kh * HEAD_DIM
            kk = _rms(p1[:, c0:c0 + HEAD_DIM]) * kg
            if emit_kv:
                knew_ref[pl.ds(N_KV_HEADS * r0 + kh, RB, stride=N_KV_HEADS), :] = kk
            if use_rope:
                kk = rope(kk)
            k_s[s, pl.ds(P + o, RB), kh * HEAD_DIM:(kh + 1) * HEAD_DIM] = kk.astype(BF16)
        vv = p1[:, ATTN_W + KV_W:ATTN_W + 2 * KV_W]
        if emit_kv:
            for kh in range(N_KV_HEADS):
                vnew_ref[pl.ds(N_KV_HEADS * r0 + kh, RB, stride=N_KV_HEADS), :] = (
                    vv[:, kh * HEAD_DIM:(kh + 1) * HEAD_DIM])
        v_s[s, pl.ds(P + o, RB), :] = vv.astype(BF16)
        xp_s[s, pl.ds(POOL_HALO + o, RB), :] = p1[:, ATTN_W + 2 * KV_W:GATE_COL]
        return carry

    @pl.when(step == 0)
    def _():
        xm_s[1] = jnp.zeros((U * RB, D_MODEL), F32)
        mod2_s[1] = jnp.zeros((2, D_MODEL), F32)

    @pl.when((step < n_steps) & (step % (nrb // U) == 0))
    def _():
        if P:
            for kh in range(N_KV_HEADS):
                cols = slice(kh * HEAD_DIM, (kh + 1) * HEAD_DIM)
                k_s[0, 0:P, cols] = ck_ref[pl.ds(kh, P, stride=N_KV_HEADS), :].astype(BF16)
                v_s[0, 0:P, cols] = cv_ref[pl.ds(kh, P, stride=N_KV_HEADS), :].astype(BF16)
        xp_s[:, 0:POOL_HALO, :] = jnp.zeros((S, POOL_HALO, POOL_W), F32)
        xp_s[:, L + POOL_HALO:L + 2 * POOL_HALO, :] = jnp.zeros((S, POOL_HALO, POOL_W), F32)
        lax.fori_loop(0, TM // RB, project, 0)
        for srcs, dst in zip(cast_in, cast_out):
            col = 0
            for src in srcs:
                dst[..., col:col + src.shape[-1]] = src[...].astype(BF16)
                col += src.shape[-1]

    def mix(u):
        r0 = pl.multiple_of(((block0 + u) % nrb) * RB, RB)
        s = r0 // L
        o = pl.multiple_of(r0 % L, RB)
        attn_u = attn_s.at[u]
        rows = slice(u * RB, (u + 1) * RB)

        for hd in range(N_HEADS):
            kh = hd // GROUP
            k = k_s[s, :, kh * HEAD_DIM:(kh + 1) * HEAD_DIM]
            v = v_s[s, :, kh * HEAD_DIM:(kh + 1) * HEAD_DIM]
            qh = q_s[hd, pl.ds(r0, RB), :]
            sc = lax.dot_general(qh, k, (((1,), (1,)), ((), ())), preferred_element_type=F32)
            e = jnp.exp2(sc - jnp.max(sc, axis=-1, keepdims=True))
            den = jnp.sum(e, axis=-1, keepdims=True)
            oh = jnp.dot(e.astype(BF16), v, preferred_element_type=F32) / den
            attn_u[:, hd * HEAD_DIM:(hd + 1) * HEAD_DIM] = oh.astype(BF16)
        a = jnp.dot(attn_u[...], wa_ref[...], preferred_element_type=F32)

        t = o + lax.broadcasted_iota(I32, (RB, 1), 0)
        RW = RB + 2 * POOL_HALO
        parts = []
        for gi, w in enumerate(POOL_WINDOWS):
            cols = slice(gi * POOL_GC, (gi + 1) * POOL_GC)
            xw = xp_s[s, pl.ds(o, RW), cols]
            run = xw
            span = 1
            while span < w:
                run = run + pltpu.roll(run, span, 0)
                span *= 2
            if w // 2 > 1:
                run = pltpu.roll(run, RW - (w // 2 - 1), 0)
            tot = run[POOL_HALO:POOL_HALO + RB]
            cnt = (jnp.minimum(t + w // 2, L) - jnp.maximum(t - w // 2, 0)).astype(F32)
            parts.append(tot / cnt - xw[POOL_HALO:POOL_HALO + RB])
        dpool = jnp.concatenate(parts, axis=1).astype(BF16)
        pooled = jnp.dot(dpool, wpool_ref[...], preferred_element_type=F32) * pscale_ref[...]
        b = jnp.dot(pooled.astype(BF16), wb_ref[...], preferred_element_type=F32)

        gates = jnp.dot(h_s[pl.ds(r0, RB), :], win_ref[:, GATE_COL:IN_W], preferred_element_type=F32)
        merged = _sigmoid(gates[:, 0:D_MODEL]) * a + _sigmoid(gates[:, D_MODEL:2 * D_MODEL]) * b
        upd = jnp.dot(merged.astype(BF16), wo_ref[...], preferred_element_type=F32)
        xm = x_ref[pl.ds(r0, RB), :] + gt1 * upd
        xmid_ref[rows, :] = xm
        xm_s[slot, rows, :] = xm

    def moe_prep(u):
        rows = slice(u * RB, (u + 1) * RB)
        h2 = _rms(xm_s[1 - slot, rows, :]) * mod2_s[1 - slot, 0:1, :] + mod2_s[1 - slot, 1:2, :]
        hi = h2.astype(BF16)
        lo = (h2 - hi.astype(F32)).astype(BF16)
        l1 = jnp.dot(hi, wr_ref[...], preferred_element_type=F32)
        l2 = jnp.dot(lo, wr_ref[:, 0:LANES], preferred_element_type=F32)
        gate, bucket = _route(l1[:, 0:LANES] + l1[:, LANES:2 * LANES] + l2)
        groups = pl.ds(u * (RB // SUBLANES), RB // SUBLANES)
        _store_tiles(h2_ref.at[groups], _pack_bf16_pairs(h2))
        gate_ref[rows, :] = gate
        lane = lax.broadcasted_iota(I32, (RB, LANES), 1).astype(F32)
        oh_ref[rows, :] = jnp.where(lane == bucket, 1.0, 0.0).astype(BF16)

    mod2_s[slot, 0:1, :] = gain2
    mod2_s[slot, 1:2, :] = sh2
    for u in range(U):
        moe_prep(u)
    for u in range(U):
        mix(u)


def _mix(x2d, mod, mod_row, cache, rope_tabs, weights, *, S, L, emit_kv, blocks_per_step, cast=()):
    T = x2d.shape[0]
    TM = S * L
    P = cache[0].shape[0] // (T // L * N_KV_HEADS) if cache is not None else 0
    use_rope = rope_tabs is not None
    assert T % TM == 0 and L % ROW_BLOCK == 0
    assert not (use_rope or P) or S == 1
    Lk = P + L

    args = [x2d, mod]
    nrb = TM // ROW_BLOCK
    n_blocks = T // ROW_BLOCK
    step_rows = blocks_per_step * ROW_BLOCK
    steps_per_group = nrb // blocks_per_step
    n_mix_steps = n_blocks // blocks_per_step
    assert nrb % blocks_per_step == 0

    def mixed(s):
        return jnp.minimum(s, n_mix_steps - 1)

    def group(s):
        return mixed(s) // steps_per_group

    def prepared(s):
        return jnp.maximum(s - 1, 0)

    in_specs = [
        pl.BlockSpec((TM, D_MODEL), lambda s: (group(s), 0)),
        pl.BlockSpec((1, 6, D_MODEL), lambda s: (mod_row(group(s)), 0, 0)),
    ]
    if P:
        args += list(cache)
        in_specs += [pl.BlockSpec((P * N_KV_HEADS, HEAD_DIM), lambda s: (group(s), 0))] * 2
    if use_rope:
        args += list(rope_tabs)
        in_specs += [_resident((L, HEAD_DIM))] * 3
    args += list(weights)
    in_specs += [_resident(w.shape) for w in weights]
    n_steps = T // TM
    def per_group(shape):
        assert shape[0] % n_steps == 0
        blk = (shape[0] // n_steps,) + shape[1:]
        return pl.BlockSpec(blk, lambda s, n=len(blk): (group(s),) + (0,) * (n - 1))

    cast_out_shapes = [ws[0].shape[:-1] + (sum(w.shape[-1] for w in ws),) for ws in cast]
    for ws in cast:
        args += list(ws)
        in_specs += [per_group(w.shape) for w in ws]

    out_shape = [jax.ShapeDtypeStruct((T, D_MODEL), F32), jax.ShapeDtypeStruct(_tiles_shape(T, PACKED_CHUNKS), U32),
                 jax.ShapeDtypeStruct((T, LANES), F32),
                 jax.ShapeDtypeStruct((T, LANES), BF16)]
    out_specs = [pl.BlockSpec((step_rows, D_MODEL), lambda s: (mixed(s), 0)),
                 _tiles_spec(step_rows, prepared, PACKED_CHUNKS),
                 pl.BlockSpec((step_rows, LANES), lambda s: (prepared(s), 0)),
                 pl.BlockSpec((step_rows, LANES), lambda s: (prepared(s), 0))]
    if emit_kv:
        out_shape += [jax.ShapeDtypeStruct((T * N_KV_HEADS, HEAD_DIM), F32)] * 2
        out_specs += [pl.BlockSpec((TM * N_KV_HEADS, HEAD_DIM), lambda s: (group(s), 0))] * 2
    out_shape += [jax.ShapeDtypeStruct(shp, BF16) for shp in cast_out_shapes]
    out_specs += [per_group(shp) for shp in cast_out_shapes]

    scratch = [
        pltpu.VMEM((N_HEADS, TM, HEAD_DIM), BF16),
        pltpu.VMEM((S, Lk, KV_W), BF16),
        pltpu.VMEM((S, Lk, KV_W), BF16),
        pltpu.VMEM((S, L + 2 * POOL_HALO, POOL_W), F32),
        pltpu.VMEM((TM, D_MODEL), BF16),
        pltpu.VMEM((blocks_per_step, ROW_BLOCK, ATTN_W), BF16),
        pltpu.VMEM((2, step_rows, D_MODEL), F32),
        pltpu.VMEM((2, 2, D_MODEL), F32),
    ]
    kern = functools.partial(_mix_kernel, S=S, L=L, P=P, use_rope=use_rope, emit_kv=emit_kv,
                             n_cast=tuple(len(ws) for ws in cast), n_blocks=n_blocks, U=blocks_per_step)
    return pl.pallas_call(
        kern,
        grid=(n_mix_steps + 1,),
        in_specs=in_specs,
        out_specs=out_specs,
        out_shape=out_shape,
        scratch_shapes=scratch,
        compiler_params=pltpu.CompilerParams(
            dimension_semantics=("arbitrary",), vmem_limit_bytes=V7X_VMEM_LIMIT_BYTES),
        name="mixer_rope" if use_rope else "mixer_ctx",
    )(*args)


def _plan_kernel(oh_ref, dest_ref, meta_ref, *, n_blocks, tile):
    TB = TOKEN_BLOCK
    lane = lax.broadcasted_iota(I32, (SUBLANES, LANES), 1)

    def count(b, acc):
        oh = oh_ref[pl.ds(pl.multiple_of(b * TB, TB), TB), :].astype(F32)
        return acc + jnp.sum(oh, axis=0, keepdims=True)

    counts = lax.fori_loop(0, n_blocks, count, jnp.zeros((SUBLANES, LANES), F32))
    padded = jnp.floor((counts + (tile - 0.5)) * (1.0 / tile)) * tile
    ends = padded
    step = 1
    while step < LANES:
        ends = ends + jnp.where(lane >= step, pltpu.roll(ends, step, 1), 0.0)
        step *= 2
    starts = ends - padded

    tri = jnp.where(lax.broadcasted_iota(I32, (TB, TB), 1) < lax.broadcasted_iota(I32, (TB, TB), 0),
                    1.0, 0.0).astype(BF16)

    def place(b, seen):
        oh = oh_ref[pl.ds(pl.multiple_of(b * TB, TB), TB), :]
        ohf = oh.astype(F32)
        rank = jnp.dot(tri, oh, preferred_element_type=F32)
        base = (starts + seen)[0:1, :]
        d = jnp.sum(ohf * (rank + base), axis=1, keepdims=True)
        dest_ref[b] = _row(d).astype(I32)
        return seen + jnp.sum(ohf, axis=0, keepdims=True)

    lax.fori_loop(0, n_blocks, place, jnp.zeros((SUBLANES, LANES), F32))

    tile_row0 = lax.broadcasted_iota(I32, (LANES, LANES), 0).astype(F32) * tile
    is_bucket = lax.broadcasted_iota(I32, (LANES, LANES), 1) < N_BUCKETS
    done = jnp.sum(jnp.where(is_bucket, jnp.where(ends[0:1, :] <= tile_row0, 1.0, 0.0), 0.0),
                   axis=1, keepdims=True)
    bkt = jnp.minimum(done, N_BUCKETS - 1.0)
    grp = (jnp.where(bkt >= PAIRS_PER_GROUP, 1.0, 0.0) + jnp.where(bkt >= 2 * PAIRS_PER_GROUP, 1.0, 0.0)
           + jnp.where(bkt >= 3 * PAIRS_PER_GROUP, 1.0, 0.0))
    pair = bkt - PAIRS_PER_GROUP * grp
    a = jnp.where(pair >= 3.0, 1.0, 0.0) + jnp.where(pair >= 5.0, 1.0, 0.0)
    b = pair - a * (7.0 - a) * 0.5 + a + 1.0
    e1 = EXP_PER_GROUP * grp + a
    e2 = EXP_PER_GROUP * grp + b
    meta = jnp.concatenate(
        [_row(e1), _row(e2), jnp.floor(ends[0:1, :] * (1.0 / tile) + 0.5),
         jnp.zeros((SUBLANES - 3, LANES), F32)], axis=0)
    meta_ref[...] = meta.astype(I32)


def _plan(onehot, tile):
    T = onehot.shape[0]
    n_blocks = T // TOKEN_BLOCK
    dest, meta = pl.pallas_call(
        functools.partial(_plan_kernel, n_blocks=n_blocks, tile=tile),
        out_shape=[jax.ShapeDtypeStruct((n_blocks, 1, TOKEN_BLOCK), I32),
                   jax.ShapeDtypeStruct((SUBLANES, LANES), I32)],
        name="moe_plan",
    )(onehot)
    return dest.reshape(T), meta


def _sc_move_rows(src_v, table_hbm, out_hbm, lo, n_rows, idx_v, pieces_v, sem):
    chunks = pieces_v.shape[0] // SC_ROWS_PER_STEP
    lane = lax.iota(I32, SC_LANES)
    row_in_group = lane & (SUBLANES - 1)
    chunk_in_pair = lane >> 3
    rows_per_gather = SC_PIECES_PER_GATHER // chunks

    @pl.loop(0, n_rows // SC_ROWS_PER_STEP)
    def _(step):
        copies = []
        for g in range(SC_ROWS_PER_STEP // rows_per_gather):
            r0 = step * SC_ROWS_PER_STEP + g * rows_per_gather
            for v in range(SC_PIECES_PER_GATHER // SC_LANES):
                group, chunk0 = v // (chunks // 2), 2 * (v % (chunks // 2))
                tok = plsc.load_gather(src_v, [r0 + group * SUBLANES + row_in_group])
                piece = (tok >> 3) * (SUBLANES * chunks) + (chunk0 + chunk_in_pair) * SUBLANES + (tok & 7)
                idx_v[pl.ds(g * SC_PIECES_PER_GATHER + v * SC_LANES, SC_LANES)] = piece
            window = pl.ds(g * SC_PIECES_PER_GATHER, SC_PIECES_PER_GATHER)
            copies.append(pltpu.async_copy(table_hbm.at[idx_v.at[window]], pieces_v.at[window], sem))
        for cp in copies:
            cp.wait()
        first = pl.multiple_of((lo + step * SC_ROWS_PER_STEP) * chunks, SC_ROWS_PER_STEP * chunks)
        pltpu.sync_copy(pieces_v, out_hbm.at[pl.ds(first, SC_ROWS_PER_STEP * chunks)])


def _sc_scratch(chunks, dtype):
    return [pltpu.VMEM((SC_ROWS_PER_STEP * chunks,), I32), pltpu.VMEM((SC_ROWS_PER_STEP * chunks, LANES), dtype)]


def _sc_dispatch(h2_flat, gate_rows, dest, n_rows):
    T = dest.shape[0]
    per_worker = n_rows // SC_WORKERS
    rows_per_step = SC_ROWS_PER_STEP
    chunks = h2_flat.shape[0] // T
    assert n_rows % SC_WORKERS == 0 and per_worker % rows_per_step == 0 and T % SC_LANES == 0
    mesh = plsc.VectorSubcoreMesh(core_axis_name="c", subcore_axis_name="s")

    @functools.partial(
        pl.kernel, mesh=mesh,
        out_type=[jax.ShapeDtypeStruct((n_rows * chunks, LANES), h2_flat.dtype),
                  jax.ShapeDtypeStruct((n_rows, LANES), F32)],
        scratch_types=[pltpu.VMEM((T,), I32), pltpu.VMEM((per_worker,), I32)]
        + _sc_scratch(chunks, h2_flat.dtype)
        + [pltpu.VMEM((rows_per_step, LANES), F32), pltpu.SemaphoreType.DMA, pltpu.SemaphoreType.DMA],
        compiler_params=pltpu.CompilerParams(use_tc_tiling_on_sc=True, needs_layout_passes=False),
        name="sc_dispatch",
    )
    def dispatch(h2_hbm, gate_hbm, dest_hbm, out_h_hbm, out_g_hbm,
                 dest_v, src_v, idx_v, pieces_v, gates_v, sem_h, sem_g):
        worker = lax.axis_index("s") * SC_CORES + lax.axis_index("c")
        lo = worker * per_worker
        pltpu.sync_copy(dest_hbm, dest_v)

        @pl.loop(0, per_worker // SC_LANES)
        def _(j):
            j0 = pl.multiple_of(j * SC_LANES, SC_LANES)
            src_v[pl.ds(j0, SC_LANES)] = lax.rem(lo + j0 + lax.iota(I32, SC_LANES), T)

        @pl.loop(0, T // SC_LANES)
        def _(j):
            t0 = pl.multiple_of(j * SC_LANES, SC_LANES)
            d = dest_v[pl.ds(t0, SC_LANES)] - lo
            mine = (d >= 0) & (d < per_worker)
            plsc.store_scatter(src_v, [jnp.where(mine, d, 0)], t0 + lax.iota(I32, SC_LANES), mask=mine)

        @pl.loop(0, per_worker // rows_per_step)
        def _(j):
            off = pl.multiple_of(j * rows_per_step, rows_per_step)
            pltpu.async_copy(gate_hbm.at[src_v.at[pl.ds(off, rows_per_step)]], gates_v, sem_g).wait()
            pltpu.sync_copy(gates_v, out_g_hbm.at[pl.ds(lo + off, rows_per_step)])

        _sc_move_rows(src_v, h2_hbm, out_h_hbm, lo, per_worker, idx_v, pieces_v, sem_h)

    return dispatch(h2_flat, gate_rows, dest)


def _expert_kernel(meta, x_ref, gv_ref, wgu_hbm, wd_hbm, o_ref, wgu_ref, wd_ref, ready_s, sems, *, tile, per_step):
    groups = tile // SUBLANES
    n_used = meta[2, LANES - 1]
    step = pl.program_id(0)

    def weight_copies(g):
        experts = pl.ds(g * EXP_PER_GROUP, EXP_PER_GROUP)
        return (pltpu.make_async_copy(wgu_hbm.at[experts], wgu_ref.at[experts], sems.at[g]),
                pltpu.make_async_copy(wd_hbm.at[experts], wd_ref.at[experts], sems.at[g]))

    def land_through(last_group):
        landed = ready_s[0]
        for g in range(N_EXP_GROUPS):
            @pl.when((g >= landed) & (g <= last_group))
            def _():
                for cp in weight_copies(g):
                    cp.wait()
                if g + 1 < N_EXP_GROUPS:
                    for cp in weight_copies(g + 1):
                        cp.start()
        ready_s[0] = jnp.maximum(landed, last_group + 1)

    @pl.when(step == 0)
    def _():
        ready_s[0] = 0
        for cp in weight_copies(0):
            cp.start()

    def one_tile(k, carry):
        t = step * per_step + k
        rows = pl.ds(pl.multiple_of(k * groups, groups), groups)

        @pl.when(t < n_used)
        def _():
            land_through(meta[0, t] // EXP_PER_GROUP)
            x = _unpack_bf16_pairs(_load_tiles(x_ref.at[rows]))
            gv = gv_ref[pl.ds(pl.multiple_of(k * tile, SUBLANES), tile), :]
            lane = lax.broadcasted_iota(I32, gv.shape, 1)
            out = None
            for e in (meta[0, t], meta[1, t]):
                ge = jnp.sum(jnp.where(lane == EXPERT_LANE0 + e, gv, 0.0), axis=-1, keepdims=True)
                h = jnp.dot(x, wgu_ref[e], preferred_element_type=F32)
                hg = h[:, 0:D_EXPERT]
                hid = (hg * _sigmoid(hg) * h[:, D_EXPERT:2 * D_EXPERT] * ge).astype(BF16)
                y = jnp.dot(hid, wd_ref[e], preferred_element_type=F32)
                out = y if out is None else out + y
            _store_tiles(o_ref.at[rows], _pack_bf16_pairs(out))

        return carry

    lax.fori_loop(0, per_step, one_tile, 0)

    @pl.when(step == pl.num_programs(0) - 1)
    def _():
        land_through(N_EXP_GROUPS - 1)


def _experts(sorted_h2, sorted_gates, meta, wgu, wd, tiling):
    tile, per_step, n_tiles = tiling
    step_rows = tile * per_step
    assert n_tiles * tile == sorted_h2.shape[0] * SUBLANES and n_tiles % per_step == 0

    def last_used(i, meta):
        return jnp.minimum(i, (meta[2, LANES - 1] - 1) // per_step)

    return pl.pallas_call(
        functools.partial(_expert_kernel, tile=tile, per_step=per_step),
        grid_spec=pltpu.PrefetchScalarGridSpec(
            num_scalar_prefetch=1,
            grid=(n_tiles // per_step,),
            in_specs=[
                _tiles_spec(step_rows, last_used, PACKED_CHUNKS),
                pl.BlockSpec((step_rows, LANES), lambda *a: (last_used(*a), 0)),
                pl.BlockSpec(memory_space=pl.ANY), pl.BlockSpec(memory_space=pl.ANY),
            ],
            out_specs=_tiles_spec(step_rows, last_used, PACKED_CHUNKS),
            scratch_shapes=[pltpu.VMEM(wgu.shape, wgu.dtype), pltpu.VMEM(wd.shape, wd.dtype),
                            pltpu.SMEM((1,), I32), pltpu.SemaphoreType.DMA((N_EXP_GROUPS,))],
        ),
        out_shape=jax.ShapeDtypeStruct(_tiles_shape(n_tiles * tile, PACKED_CHUNKS), U32),
        compiler_params=pltpu.CompilerParams(
            dimension_semantics=("arbitrary",), vmem_limit_bytes=V7X_VMEM_LIMIT_BYTES),
        name="moe_experts",
    )(meta, sorted_h2, sorted_gates, wgu, wd)


def _sc_row_gather(table_flat, idx, chunks):
    n = idx.shape[0]
    per_worker = n // SC_WORKERS
    assert n % SC_WORKERS == 0 and per_worker % SC_ROWS_PER_STEP == 0
    mesh = plsc.VectorSubcoreMesh(core_axis_name="c", subcore_axis_name="s")

    @functools.partial(
        pl.kernel, mesh=mesh,
        out_type=jax.ShapeDtypeStruct((n * chunks, LANES), table_flat.dtype),
        scratch_types=[pltpu.VMEM((per_worker,), I32)] + _sc_scratch(chunks, table_flat.dtype)
        + [pltpu.SemaphoreType.DMA],
        compiler_params=pltpu.CompilerParams(use_tc_tiling_on_sc=True, needs_layout_passes=False),
        name="sc_row_gather",
    )
    def gather(table_hbm, idx_hbm, out_hbm, src_v, idx_v, pieces_v, sem):
        worker = lax.axis_index("s") * SC_CORES + lax.axis_index("c")
        lo = worker * per_worker
        pltpu.sync_copy(idx_hbm.at[pl.ds(lo, per_worker)], src_v)
        _sc_move_rows(src_v, table_hbm, out_hbm, lo, per_worker, idx_v, pieces_v, sem)

    return gather(table_flat, idx)


def _final_kernel(x_ref, moe_ref, mod_ref, gf_ref, o_ref):
    y = x_ref[...] + mod_ref[0, 5:6, :] * _unpack_bf16_pairs(_load_tiles(moe_ref)).astype(F32)
    o_ref[...] = _rms(y) * gf_ref[...]


def _final(xmid, moe_rows, mod, mod_row, gf):
    T = xmid.shape[0]
    return pl.pallas_call(
        _final_kernel,
        grid=(T // FINAL_BLOCK,),
        in_specs=[
            pl.BlockSpec((FINAL_BLOCK, D_MODEL), lambda i: (i, 0)),
            _tiles_spec(FINAL_BLOCK, lambda i: i, PACKED_CHUNKS),
            pl.BlockSpec((1, 6, D_MODEL), lambda i: (mod_row(i), 0, 0)),
            pl.BlockSpec((1, D_MODEL), lambda i: (0, 0)),
        ],
        out_specs=pl.BlockSpec((FINAL_BLOCK, D_MODEL), lambda i: (i, 0)),
        out_shape=jax.ShapeDtypeStruct((T, D_MODEL), F32),
        compiler_params=pltpu.CompilerParams(
            dimension_semantics=("arbitrary",), vmem_limit_bytes=V7X_VMEM_LIMIT_BYTES),
        name="moe_final",
    )(xmid, moe_rows, mod, gf)


def _flat(tiles):
    return tiles.reshape(-1, LANES)


def _expert_tiling(T):
    tile = -(-(T * 9) // (8 * N_BUCKETS * 64)) * 64
    per_step = max(1, EXPERT_STEP_ROWS // tile)
    n_tiles = (T + N_BUCKETS * (tile - 1)) // tile
    while n_tiles % per_step or (n_tiles * tile) % (SC_WORKERS * SC_ROWS_PER_STEP):
        n_tiles += 1
    return tile, per_step, n_tiles


def _moe_dispatch(h2_tiles, gate_rows, onehot, tiling):
    T = gate_rows.shape[0]
    tile, _, n_tiles = tiling
    n_rows = n_tiles * tile
    assert n_tiles <= LANES and T % TOKEN_BLOCK == 0
    dest, meta = _plan(onehot, tile)
    sorted_h2, sorted_gates = _sc_dispatch(_flat(h2_tiles), gate_rows, dest, n_rows)
    return sorted_h2.reshape(_tiles_shape(n_rows, PACKED_CHUNKS)), sorted_gates, dest, meta


def _moe_unpermute(moe_sorted_tiles, dest):
    chunks = moe_sorted_tiles.shape[1]
    return _sc_row_gather(_flat(moe_sorted_tiles), dest, chunks).reshape(_tiles_shape(dest.shape[0], chunks))


def _rope_tables(n_tokens):
    t = np.arange(n_tokens)
    row = (t // GRID_W).astype(np.float32)
    col = (t % GRID_W).astype(np.float32)
    freq = np.float32(ROPE_THETA) ** (-np.arange(ROPE_NF, dtype=np.float32) / np.float32(ROPE_NF))
    ang = np.concatenate([row[:, None] * freq] * 2 + [col[:, None] * freq] * 2, axis=-1)
    first = (np.arange(HEAD_DIM) % (2 * ROPE_NF)) < ROPE_NF
    sin = np.sin(ang)
    zero = np.float32(0.0)
    return (jnp.asarray(np.cos(ang)), jnp.asarray(np.where(first, -sin, zero)),
            jnp.asarray(np.where(first, zero, sin)))


def kernel(x_prompt, x_sample, cache_k, cache_v, c, c_ctx, norm1_g, norm2_g, w_ada, b_ada, w_in, q_norm_g, k_norm_g, w_pool, pool_scale, w_branch_a, w_branch_b, w_out, w_router_group, w_router_expert, w_exp_gate, w_exp_up, w_exp_down, final_norm_g):
    assert norm1_g.shape[0] == 1, "single-layer trunk"
    B, L_ctx, _ = x_prompt.shape
    Bs, L_lat, _ = x_sample.shape
    P = cache_k.shape[2]
    assert 1 + Bs <= COND_ROWS

    cond = jnp.concatenate([c_ctx[None, :], c, jnp.zeros((COND_ROWS - 1 - Bs, D_MODEL), F32)], axis=0)
    wpool_bd = jax.scipy.linalg.block_diag(*[w_pool[0, g] for g in range(len(POOL_WINDOWS))])
    mod, w_in_b, wpool_b, wa_b, wb_b, wo_b = _ada(
        cond, w_ada[0], b_ada[0][None, :],
        cast=(w_in[0], wpool_bd, w_branch_a[0], w_branch_b[0], w_out[0]))
    mod = mod.reshape(COND_ROWS, 6, D_MODEL)

    wr = jnp.concatenate([w_router_group[0], w_router_expert[0],
                          jnp.zeros((D_MODEL, LANES - N_EXP_GROUPS - N_EXPERTS), F32)], axis=1)
    wr_hi = wr.astype(BF16)
    wr_lo = (wr - wr_hi.astype(F32)).astype(BF16)
    mix_w = (norm1_g[0][None, :], w_in_b, q_norm_g[0][None, :], k_norm_g[0][None, :],
             wpool_b, pool_scale[0][None, :], wa_b, wb_b, wo_b,
             norm2_g[0][None, :], jnp.concatenate([wr_hi, wr_lo], axis=1))
    gf = final_norm_g[None, :]

    xp2 = x_prompt.reshape(B * L_ctx, D_MODEL)
    xmid_p, h2_p, gate_p, oh_p, knew, vnew, wgu, wd = _mix(
        xp2, mod, lambda i: 0, None, None, mix_w, S=2, L=L_ctx, emit_kv=True, blocks_per_step=2,
        cast=((w_exp_gate[0], w_exp_up[0]), (w_exp_down[0],)))
    tiling_p = _expert_tiling(B * L_ctx)
    sh_p, sg_p, dest_p, meta_p = _moe_dispatch(h2_p, gate_p, oh_p, tiling_p)

    xs2 = x_sample.reshape(Bs * L_lat, D_MODEL)
    cache = (cache_k.reshape(Bs * P * N_KV_HEADS, HEAD_DIM), cache_v.reshape(Bs * P * N_KV_HEADS, HEAD_DIM))
    xmid_s, h2_s, gate_s, oh_s = _mix(xs2, mod, lambda i: 1 + i, cache, _rope_tables(L_lat), mix_w,
                                      S=1, L=L_lat, emit_kv=False, blocks_per_step=1)
    tiling_s = _expert_tiling(Bs * L_lat)
    sh_s, sg_s, dest_s, meta_s = _moe_dispatch(h2_s, gate_s, oh_s, tiling_s)

    moe_p = _moe_unpermute(_experts(sh_p, sg_p, meta_p, wgu, wd, tiling_p), dest_p)
    moe_s = _moe_unpermute(_experts(sh_s, sg_s, meta_s, wgu, wd, tiling_s), dest_s)
    y_prompt = _final(xmid_p, moe_p, mod, lambda i: 0, gf)
    blocks_per_seq = L_lat // FINAL_BLOCK
    y_sample = _final(xmid_s, moe_s, mod, lambda i: 1 + i // blocks_per_seq, gf)

    return (y_prompt.reshape(B, L_ctx, D_MODEL), y_sample.reshape(Bs, L_lat, D_MODEL),
            knew.reshape(B, 1, L_ctx, N_KV_HEADS, HEAD_DIM), vnew.reshape(B, 1, L_ctx, N_KV_HEADS, HEAD_DIM))
```

```python
import functools

import numpy as np
import jax
import jax.numpy as jnp
from jax import lax
from jax.experimental import pallas as pl
from jax.experimental.pallas import tpu as pltpu
from jax.experimental.pallas import tpu_sc as plsc

F32 = jnp.float32
BF16 = jnp.bfloat16
I32 = jnp.int32
U32 = jnp.uint32

D_MODEL = 1024
HEAD_DIM = 128
N_HEADS = 8
N_KV_HEADS = 2
GROUP = N_HEADS // N_KV_HEADS
ATTN_W = N_HEADS * HEAD_DIM
KV_W = N_KV_HEADS * HEAD_DIM
POOL_WINDOWS = (2, 4, 8, 16)
POOL_GC = 128
POOL_W = POOL_GC * len(POOL_WINDOWS)
IN_W = ATTN_W + 2 * KV_W + POOL_W + 2 * D_MODEL
GATE_COL = ATTN_W + 2 * KV_W + POOL_W
GRID_W = 64
ROPE_THETA = 10000.0
ROPE_NF = HEAD_DIM // 4
N_EXP_GROUPS = 4
EXP_PER_GROUP = 4
N_EXPERTS = 16
D_EXPERT = 256
EPS = 1e-6
LOG2_E = 1.4426950408889634

LANES = 128
SUBLANES = 8
COND_ROWS = SUBLANES
POOL_HALO = 8
ROW_BLOCK = 256
ADA_COLS = 768
EXPERT_LANE0 = N_EXP_GROUPS
PAIRS_PER_GROUP = EXP_PER_GROUP * (EXP_PER_GROUP - 1) // 2
N_BUCKETS = N_EXP_GROUPS * PAIRS_PER_GROUP
EXPERT_STEP_ROWS = 1536
TOKEN_BLOCK = 1024
FINAL_BLOCK = 1024
ROW_CHUNKS = D_MODEL // LANES
SC_CORES = 2
SC_SUBCORES = 16
SC_WORKERS = SC_CORES * SC_SUBCORES
SC_LANES = 16
SC_PIECES_PER_GATHER = 128
SC_ROWS_PER_STEP = 64
PACKED_CHUNKS = ROW_CHUNKS // 2
V7X_VMEM_LIMIT_BYTES = 56 * 1024 * 1024


def _sigmoid(x):
    return 1.0 / (1.0 + jnp.exp(-x))


def _rms(x):
    return x * lax.rsqrt(jnp.mean(x * x, axis=-1, keepdims=True) + EPS)


def _resident(shape):
    zeros = (0,) * len(shape)
    return pl.BlockSpec(shape, lambda i, *_: zeros, pipeline_mode=pl.Buffered(1))


def _tiles_shape(n, chunks=ROW_CHUNKS):
    return (n // SUBLANES, chunks, SUBLANES, LANES)


def _tiles_spec(n, block_index, chunks=ROW_CHUNKS):
    return pl.BlockSpec(_tiles_shape(n, chunks), lambda *a: (block_index(*a), 0, 0, 0))


def _store_tiles(ref, x):
    for c in range(ref.shape[1]):
        ref[:, c, :, :] = x[:, c * LANES:(c + 1) * LANES].reshape(x.shape[0] // SUBLANES, SUBLANES, LANES)


def _load_tiles(ref):
    n = ref.shape[0] * SUBLANES
    return jnp.concatenate([ref[:, c, :, :].reshape(n, LANES) for c in range(ref.shape[1])], axis=1)


def _pack_bf16_pairs(x):
    bits = pltpu.bitcast(x.astype(BF16).astype(F32), U32)
    w = x.shape[1] // 2
    return bits[:, :w] | (bits[:, w:] >> 16)


def _unpack_bf16_pairs(words):
    hi = pltpu.bitcast(words & jnp.uint32(0xFFFF0000), F32).astype(BF16)
    lo = pltpu.bitcast(words << 16, F32).astype(BF16)
    return jnp.concatenate([hi, lo], axis=1)


def _row(x):
    return jnp.transpose(jnp.broadcast_to(x, (x.shape[0], LANES)))[0:1, :]


def _ada_kernel(c_ref, w_ref, b_ref, *refs):
    n_cast = (len(refs) - 1) // 2
    c = c_ref[...]
    s = (c * _sigmoid(c)).astype(BF16)
    refs[n_cast][...] = jnp.dot(s, w_ref[...].astype(BF16), preferred_element_type=F32) + b_ref[...]
    for src, dst in zip(refs[:n_cast], refs[n_cast + 1:]):
        dst[...] = src[...].astype(BF16)


def _ada(cond, w_ada, b_ada, cast=()):
    n = w_ada.shape[1]
    n_steps = n // ADA_COLS
    cast_specs = []
    for w in cast:
        assert w.ndim == 2 and w.shape[0] % (n_steps * 2 * SUBLANES) == 0
        cast_specs.append(pl.BlockSpec((w.shape[0] // n_steps, w.shape[1]), lambda j: (j, 0)))
    return pl.pallas_call(
        _ada_kernel,
        grid=(n_steps,),
        in_specs=[
            pl.BlockSpec((COND_ROWS, D_MODEL), lambda j: (0, 0)),
            pl.BlockSpec((D_MODEL, ADA_COLS), lambda j: (0, j)),
            pl.BlockSpec((1, ADA_COLS), lambda j: (0, j)),
        ] + cast_specs,
        out_specs=[pl.BlockSpec((COND_ROWS, ADA_COLS), lambda j: (0, j))] + cast_specs,
        out_shape=[jax.ShapeDtypeStruct((COND_ROWS, n), F32)] + [jax.ShapeDtypeStruct(w.shape, BF16) for w in cast],
        name="ada_mod",
    )(cond, w_ada, b_ada, *cast)


def _route(logits):
    lane = lax.broadcasted_iota(I32, logits.shape, 1).astype(F32)
    neg = jnp.float32(-1e30)
    far = jnp.float32(LANES)
    is_g = lane < N_EXP_GROUPS
    gl = jnp.where(is_g, logits, neg)
    gmax = jnp.max(gl, axis=-1, keepdims=True)
    gsel = jnp.min(jnp.where(gl == gmax, lane, far), axis=-1, keepdims=True)
    psel = 1.0 / jnp.sum(jnp.where(is_g, jnp.exp(gl - gmax), 0.0), axis=-1, keepdims=True)
    e_lo = EXPERT_LANE0 + EXP_PER_GROUP * gsel
    el = jnp.where(lane >= e_lo, jnp.where(lane < e_lo + EXP_PER_GROUP, logits, neg), neg)
    v1 = jnp.max(el, axis=-1, keepdims=True)
    i1 = jnp.min(jnp.where(el == v1, lane, far), axis=-1, keepdims=True)
    el2 = jnp.where(lane == i1, neg, el)
    v2 = jnp.max(el2, axis=-1, keepdims=True)
    i2 = jnp.min(jnp.where(el2 == v2, jnp.where(lane == i1, far, lane), far), axis=-1, keepdims=True)
    e2 = jnp.exp(v2 - v1)
    w1 = psel / (1.0 + e2)
    w2 = psel * e2 / (1.0 + e2)
    gate = jnp.where(lane == i1, w1, jnp.where(lane == i2, w2, 0.0))
    a = jnp.minimum(i1, i2) - e_lo
    b = jnp.maximum(i1, i2) - e_lo
    pair = a * (7.0 - a) * 0.5 + (b - a - 1.0)
    return gate, gsel * PAIRS_PER_GROUP + pair


def _mix_kernel(*refs, S, L, P, use_rope, emit_kv, n_cast, n_blocks, U):
    it = iter(refs)
    x_ref = next(it)
    mod_ref = next(it)
    if P:
        ck_ref = next(it)
        cv_ref = next(it)
    if use_rope:
        cos_ref = next(it)
        sneg_ref = next(it)
        spos_ref = next(it)
    (g1_ref, win_ref, qg_ref, kg_ref, wpool_ref, pscale_ref, wa_ref, wb_ref, wo_ref,
     g2_ref, wr_ref) = (next(it) for _ in range(11))
    cast_in = [[next(it) for _ in range(n)] for n in n_cast]
    xmid_ref = next(it)
    h2_ref = next(it)
    gate_ref = next(it)
    oh_ref = next(it)
    if emit_kv:
        knew_ref = next(it)
        vnew_ref = next(it)
    cast_out = [next(it) for _ in n_cast]
    q_s, k_s, v_s, xp_s, h_s, attn_s, xm_s, mod2_s = (next(it) for _ in range(8))

    TM = S * L
    RB = ROW_BLOCK
    nrb = TM // RB
    n_steps = n_blocks // U
    score_gain = HEAD_DIM ** -0.5 * LOG2_E
    step = pl.program_id(0)
    block0 = U * jnp.minimum(step, n_steps - 1)
    slot = step % 2

    sh1 = mod_ref[0, 0:1, :]
    gain1 = g1_ref[...] * (1.0 + mod_ref[0, 1:2, :])
    gt1 = mod_ref[0, 2:3, :]
    sh2 = mod_ref[0, 3:4, :]
    gain2 = g2_ref[...] * (1.0 + mod_ref[0, 4:5, :])
    qg = qg_ref[...] * score_gain
    kg = kg_ref[...]

    def project(r, carry):
        r0 = pl.multiple_of(r * RB, RB)
        s = r0 // L
        o = pl.multiple_of(r0 % L, RB)
        hb = (_rms(x_ref[pl.ds(r0, RB), :]) * gain1 + sh1).astype(BF16)
        h_s[pl.ds(r0, RB), :] = hb
        p1 = jnp.dot(hb, win_ref[:, 0:GATE_COL], preferred_element_type=F32)
        if use_rope:
            cs = cos_ref[pl.ds(o, RB), :]
            sn = sneg_ref[pl.ds(o, RB), :]
            sp = spos_ref[pl.ds(o, RB), :]

        def rope(t):
            return (t * cs + pltpu.roll(t, HEAD_DIM - ROPE_NF, 1) * sn + pltpu.roll(t, ROPE_NF, 1) * sp)

        for hd in range(N_HEADS):
            qh = _rms(p1[:, hd * HEAD_DIM:(hd + 1) * HEAD_DIM]) * qg
            if use_rope:
                qh = rope(qh)
            q_s[hd, pl.ds(r0, RB), :] = qh.astype(BF16)
        for kh in range(N_KV_HEADS):
            c0 = ATTN_W + kh * HEAD_DIM
            kk = _rms(p1[:, c0:c0 + HEAD_DIM]) * kg
            if emit_kv:
                knew_ref[pl.ds(N_KV_HEADS * r0 + kh, RB, stride=N_KV_HEADS), :] = kk
            if use_rope:
                kk = rope(kk)
            k_s[s, pl.ds(P + o, RB), kh * HEAD_DIM:(kh + 1) * HEAD_DIM] = kk.astype(BF16)
        vv = p1[:, ATTN_W + KV_W:ATTN_W + 2 * KV_W]
        if emit_kv:
            for kh in range(N_KV_HEADS):
                vnew_ref[pl.ds(N_KV_HEADS * r0 + kh, RB, stride=N_KV_HEADS), :] = (
                    vv[:, kh * HEAD_DIM:(kh + 1) * HEAD_DIM])
        v_s[s, pl.ds(P + o, RB), :] = vv.astype(BF16)
        xp_s[s, pl.ds(POOL_HALO + o, RB), :] = p1[:, ATTN_W + 2 * KV_W:GATE_COL]
        return carry

    @pl.when(step == 0)
    def _():
        xm_s[1] = jnp.zeros((U * RB, D_MODEL), F32)
        mod2_s[1] = jnp.zeros((2, D_MODEL), F32)

    @pl.when((step < n_steps) & (step % (nrb // U) == 0))
    def _():
        if P:
            for kh in range(N_KV_HEADS):
                cols = slice(kh * HEAD_DIM, (kh + 1) * HEAD_DIM)
                k_s[0, 0:P, cols] = ck_ref[pl.ds(kh, P, stride=N_KV_HEADS), :].astype(BF16)
                v_s[0, 0:P, cols] = cv_ref[pl.ds(kh, P, stride=N_KV_HEADS), :].astype(BF16)
        xp_s[:, 0:POOL_HALO, :] = jnp.zeros((S, POOL_HALO, POOL_W), F32)
        xp_s[:, L + POOL_HALO:L + 2 * POOL_HALO, :] = jnp.zeros((S, POOL_HALO, POOL_W), F32)
        lax.fori_loop(0, TM // RB, project, 0)
        for srcs, dst in zip(cast_in, cast_out):
            col = 0
            for src in srcs:
                dst[..., col:col + src.shape[-1]] = src[...].astype(BF16)
                col += src.shape[-1]

    def mix(u):
        r0 = pl.multiple_of(((block0 + u) % nrb) * RB, RB)
        s = r0 // L
        o = pl.multiple_of(r0 % L, RB)
        attn_u = attn_s.at[u]
        rows = slice(u * RB, (u + 1) * RB)

        for hd in range(N_HEADS):
            kh = hd // GROUP
            k = k_s[s, :, kh * HEAD_DIM:(kh + 1) * HEAD_DIM]
            v = v_s[s, :, kh * HEAD_DIM:(kh + 1) * HEAD_DIM]
            qh = q_s[hd, pl.ds(r0, RB), :]
            sc = lax.dot_general(qh, k, (((1,), (1,)), ((), ())), preferred_element_type=F32)
            e = jnp.exp2(sc - jnp.max(sc, axis=-1, keepdims=True))
            den = jnp.sum(e, axis=-1, keepdims=True)
            oh = jnp.dot(e.astype(BF16), v, preferred_element_type=F32) / den
            attn_u[:, hd * HEAD_DIM:(hd + 1) * HEAD_DIM] = oh.astype(BF16)
        a = jnp.dot(attn_u[...], wa_ref[...], preferred_element_type=F32)

        t = o + lax.broadcasted_iota(I32, (RB, 1), 0)
        RW = RB + 2 * POOL_HALO
        parts = []
        for gi, w in enumerate(POOL_WINDOWS):
            cols = slice(gi * POOL_GC, (gi + 1) * POOL_GC)
            xw = xp_s[s, pl.ds(o, RW), cols]
            run = xw
            span = 1
            while span < w:
                run = run + pltpu.roll(run, span, 0)
                span *= 2
            if w // 2 > 1:
                run = pltpu.roll(run, RW - (w // 2 - 1), 0)
            tot = run[POOL_HALO:POOL_HALO + RB]
            cnt = (jnp.minimum(t + w // 2, L) - jnp.maximum(t - w // 2, 0)).astype(F32)
            parts.append(tot / cnt - xw[POOL_HALO:POOL_HALO + RB])
        dpool = jnp.concatenate(parts, axis=1).astype(BF16)
        pooled = jnp.dot(dpool, wpool_ref[...], preferred_element_type=F32) * pscale_ref[...]
        b = jnp.dot(pooled.astype(BF16), wb_ref[...], preferred_element_type=F32)

        gates = jnp.dot(h_s[pl.ds(r0, RB), :], win_ref[:, GATE_COL:IN_W], preferred_element_type=F32)
        merged = _sigmoid(gates[:, 0:D_MODEL]) * a + _sigmoid(gates[:, D_MODEL:2 * D_MODEL]) * b
        upd = jnp.dot(merged.astype(BF16), wo_ref[...], preferred_element_type=F32)
        xm = x_ref[pl.ds(r0, RB), :] + gt1 * upd
        xmid_ref[rows, :] = xm
        xm_s[slot, rows, :] = xm

    def moe_prep(u):
        rows = slice(u * RB, (u + 1) * RB)
        h2 = _rms(xm_s[1 - slot, rows, :]) * mod2_s[1 - slot, 0:1, :] + mod2_s[1 - slot, 1:2, :]
        hi = h2.astype(BF16)
        lo = (h2 - hi.astype(F32)).astype(BF16)
        l1 = jnp.dot(hi, wr_ref[...], preferred_element_type=F32)
        l2 = jnp.dot(lo, wr_ref[:, 0:LANES], preferred_element_type=F32)
        gate, bucket = _route(l1[:, 0:LANES] + l1[:, LANES:2 * LANES] + l2)
        groups = pl.ds(u * (RB // SUBLANES), RB // SUBLANES)
        _store_tiles(h2_ref.at[groups], _pack_bf16_pairs(h2))
        gate_ref[rows, :] = gate
        lane = lax.broadcasted_iota(I32, (RB, LANES), 1).astype(F32)
        oh_ref[rows, :] = jnp.where(lane == bucket, 1.0, 0.0).astype(BF16)

    mod2_s[slot, 0:1, :] = gain2
    mod2_s[slot, 1:2, :] = sh2

    @pl.when(step < n_steps)
    def _():
        for u in range(U):
            moe_prep(u)
        for u in range(U):
            mix(u)

    @pl.when(step == n_steps)
    def _():
        for u in range(U):
            moe_prep(u)


def _mix(x2d, mod, mod_row, cache, rope_tabs, weights, *, S, L, emit_kv, blocks_per_step, cast=()):
    T = x2d.shape[0]
    TM = S * L
    P = cache[0].shape[0] // (T // L * N_KV_HEADS) if cache is not None else 0
    use_rope = rope_tabs is not None
    assert T % TM == 0 and L % ROW_BLOCK == 0
    assert not (use_rope or P) or S == 1
    Lk = P + L

    args = [x2d, mod]
    nrb = TM // ROW_BLOCK
    n_blocks = T // ROW_BLOCK
    step_rows = blocks_per_step * ROW_BLOCK
    steps_per_group = nrb // blocks_per_step
    n_mix_steps = n_blocks // blocks_per_step
    assert nrb % blocks_per_step == 0

    def mixed(s):
        return jnp.minimum(s, n_mix_steps - 1)

    def group(s):
        return mixed(s) // steps_per_group

    def prepared(s):
        return jnp.maximum(s - 1, 0)

    in_specs = [
        pl.BlockSpec((TM, D_MODEL), lambda s: (group(s), 0)),
        pl.BlockSpec((1, 6, D_MODEL), lambda s: (mod_row(group(s)), 0, 0)),
    ]
    if P:
        args += list(cache)
        in_specs += [pl.BlockSpec((P * N_KV_HEADS, HEAD_DIM), lambda s: (group(s), 0))] * 2
    if use_rope:
        args += list(rope_tabs)
        in_specs += [_resident((L, HEAD_DIM))] * 3
    args += list(weights)
    in_specs += [_resident(w.shape) for w in weights]
    n_steps = T // TM
    def per_group(shape):
        assert shape[0] % n_steps == 0
        blk = (shape[0] // n_steps,) + shape[1:]
        return pl.BlockSpec(blk, lambda s, n=len(blk): (group(s),) + (0,) * (n - 1))

    cast_out_shapes = [ws[0].shape[:-1] + (sum(w.shape[-1] for w in ws),) for ws in cast]
    for ws in cast:
        args += list(ws)
        in_specs += [per_group(w.shape) for w in ws]

    out_shape = [jax.ShapeDtypeStruct((T, D_MODEL), F32), jax.ShapeDtypeStruct(_tiles_shape(T, PACKED_CHUNKS), U32),
                 jax.ShapeDtypeStruct((T, LANES), F32),
                 jax.ShapeDtypeStruct((T, LANES), BF16)]
    out_specs = [pl.BlockSpec((step_rows, D_MODEL), lambda s: (mixed(s), 0)),
                 _tiles_spec(step_rows, prepared, PACKED_CHUNKS),
                 pl.BlockSpec((step_rows, LANES), lambda s: (prepared(s), 0)),
                 pl.BlockSpec((step_rows, LANES), lambda s: (prepared(s), 0))]
    if emit_kv:
        out_shape += [jax.ShapeDtypeStruct((T * N_KV_HEADS, HEAD_DIM), F32)] * 2
        out_specs += [pl.BlockSpec((TM * N_KV_HEADS, HEAD_DIM), lambda s: (group(s), 0))] * 2
    out_shape += [jax.ShapeDtypeStruct(shp, BF16) for shp in cast_out_shapes]
    out_specs += [per_group(shp) for shp in cast_out_shapes]

    scratch = [
        pltpu.VMEM((N_HEADS, TM, HEAD_DIM), BF16),
        pltpu.VMEM((S, Lk, KV_W), BF16),
        pltpu.VMEM((S, Lk, KV_W), BF16),
        pltpu.VMEM((S, L + 2 * POOL_HALO, POOL_W), F32),
        pltpu.VMEM((TM, D_MODEL), BF16),
        pltpu.VMEM((blocks_per_step, ROW_BLOCK, ATTN_W), BF16),
        pltpu.VMEM((2, step_rows, D_MODEL), F32),
        pltpu.VMEM((2, 2, D_MODEL), F32),
    ]
    kern = functools.partial(_mix_kernel, S=S, L=L, P=P, use_rope=use_rope, emit_kv=emit_kv,
                             n_cast=tuple(len(ws) for ws in cast), n_blocks=n_blocks, U=blocks_per_step)
    return pl.pallas_call(
        kern,
        grid=(n_mix_steps + 1,),
        in_specs=in_specs,
        out_specs=out_specs,
        out_shape=out_shape,
        scratch_shapes=scratch,
        compiler_params=pltpu.CompilerParams(
            dimension_semantics=("arbitrary",), vmem_limit_bytes=V7X_VMEM_LIMIT_BYTES),
        name="mixer_rope" if use_rope else "mixer_ctx",
    )(*args)


def _plan_kernel(oh_ref, dest_ref, meta_ref, *, n_blocks, tile):
    TB = TOKEN_BLOCK
    lane = lax.broadcasted_iota(I32, (SUBLANES, LANES), 1)

    def count(b, acc):
        oh = oh_ref[pl.ds(pl.multiple_of(b * TB, TB), TB), :].astype(F32)
        return acc + jnp.sum(oh, axis=0, keepdims=True)

    counts = lax.fori_loop(0, n_blocks, count, jnp.zeros((SUBLANES, LANES), F32))
    padded = jnp.floor((counts + (tile - 0.5)) * (1.0 / tile)) * tile
    ends = padded
    step = 1
    while step < LANES:
        ends = ends + jnp.where(lane >= step, pltpu.roll(ends, step, 1), 0.0)
        step *= 2
    starts = ends - padded

    tri = jnp.where(lax.broadcasted_iota(I32, (TB, TB), 1) < lax.broadcasted_iota(I32, (TB, TB), 0),
                    1.0, 0.0).astype(BF16)

    def place(b, seen):
        oh = oh_ref[pl.ds(pl.multiple_of(b * TB, TB), TB), :]
        ohf = oh.astype(F32)
        rank = jnp.dot(tri, oh, preferred_element_type=F32)
        base = (starts + seen)[0:1, :]
        d = jnp.sum(ohf * (rank + base), axis=1, keepdims=True)
        dest_ref[b] = _row(d).astype(I32)
        return seen + jnp.sum(ohf, axis=0, keepdims=True)

    lax.fori_loop(0, n_blocks, place, jnp.zeros((SUBLANES, LANES), F32))

    tile_row0 = lax.broadcasted_iota(I32, (LANES, LANES), 0).astype(F32) * tile
    is_bucket = lax.broadcasted_iota(I32, (LANES, LANES), 1) < N_BUCKETS
    done = jnp.sum(jnp.where(is_bucket, jnp.where(ends[0:1, :] <= tile_row0, 1.0, 0.0), 0.0),
                   axis=1, keepdims=True)
    bkt = jnp.minimum(done, N_BUCKETS - 1.0)
    grp = (jnp.where(bkt >= PAIRS_PER_GROUP, 1.0, 0.0) + jnp.where(bkt >= 2 * PAIRS_PER_GROUP, 1.0, 0.0)
           + jnp.where(bkt >= 3 * PAIRS_PER_GROUP, 1.0, 0.0))
    pair = bkt - PAIRS_PER_GROUP * grp
    a = jnp.where(pair >= 3.0, 1.0, 0.0) + jnp.where(pair >= 5.0, 1.0, 0.0)
    b = pair - a * (7.0 - a) * 0.5 + a + 1.0
    e1 = EXP_PER_GROUP * grp + a
    e2 = EXP_PER_GROUP * grp + b
    meta = jnp.concatenate(
        [_row(e1), _row(e2), jnp.floor(ends[0:1, :] * (1.0 / tile) + 0.5),
         jnp.zeros((SUBLANES - 3, LANES), F32)], axis=0)
    meta_ref[...] = meta.astype(I32)


def _plan(onehot, tile):
    T = onehot.shape[0]
    n_blocks = T // TOKEN_BLOCK
    dest, meta = pl.pallas_call(
        functools.partial(_plan_kernel, n_blocks=n_blocks, tile=tile),
        out_shape=[jax.ShapeDtypeStruct((n_blocks, 1, TOKEN_BLOCK), I32),
                   jax.ShapeDtypeStruct((SUBLANES, LANES), I32)],
        name="moe_plan",
    )(onehot)
    return dest.reshape(T), meta


def _sc_move_rows(src_v, table_hbm, out_hbm, lo, n_rows, idx_v, pieces_v, sem):
    chunks = pieces_v.shape[0] // SC_ROWS_PER_STEP
    lane = lax.iota(I32, SC_LANES)
    row_in_group = lane & (SUBLANES - 1)
    chunk_in_pair = lane >> 3
    rows_per_gather = SC_PIECES_PER_GATHER // chunks

    @pl.loop(0, n_rows // SC_ROWS_PER_STEP)
    def _(step):
        copies = []
        for g in range(SC_ROWS_PER_STEP // rows_per_gather):
            r0 = step * SC_ROWS_PER_STEP + g * rows_per_gather
            for v in range(SC_PIECES_PER_GATHER // SC_LANES):
                group, chunk0 = v // (chunks // 2), 2 * (v % (chunks // 2))
                tok = plsc.load_gather(src_v, [r0 + group * SUBLANES + row_in_group])
                piece = (tok >> 3) * (SUBLANES * chunks) + (chunk0 + chunk_in_pair) * SUBLANES + (tok & 7)
                idx_v[pl.ds(g * SC_PIECES_PER_GATHER + v * SC_LANES, SC_LANES)] = piece
            window = pl.ds(g * SC_PIECES_PER_GATHER, SC_PIECES_PER_GATHER)
            copies.append(pltpu.async_copy(table_hbm.at[idx_v.at[window]], pieces_v.at[window], sem))
        for cp in copies:
            cp.wait()
        first = pl.multiple_of((lo + step * SC_ROWS_PER_STEP) * chunks, SC_ROWS_PER_STEP * chunks)
        pltpu.sync_copy(pieces_v, out_hbm.at[pl.ds(first, SC_ROWS_PER_STEP * chunks)])


def _sc_scratch(chunks, dtype):
    return [pltpu.VMEM((SC_ROWS_PER_STEP * chunks,), I32), pltpu.VMEM((SC_ROWS_PER_STEP * chunks, LANES), dtype)]


def _sc_dispatch(h2_flat, gate_rows, dest, n_rows):
    T = dest.shape[0]
    per_worker = n_rows // SC_WORKERS
    rows_per_step = SC_ROWS_PER_STEP
    chunks = h2_flat.shape[0] // T
    assert n_rows % SC_WORKERS == 0 and per_worker % rows_per_step == 0 and T % SC_LANES == 0
    mesh = plsc.VectorSubcoreMesh(core_axis_name="c", subcore_axis_name="s")

    @functools.partial(
        pl.kernel, mesh=mesh,
        out_type=[jax.ShapeDtypeStruct((n_rows * chunks, LANES), h2_flat.dtype),
                  jax.ShapeDtypeStruct((n_rows, LANES), F32)],
        scratch_types=[pltpu.VMEM((T,), I32), pltpu.VMEM((per_worker,), I32)]
        + _sc_scratch(chunks, h2_flat.dtype)
        + [pltpu.VMEM((rows_per_step, LANES), F32), pltpu.SemaphoreType.DMA, pltpu.SemaphoreType.DMA],
        compiler_params=pltpu.CompilerParams(use_tc_tiling_on_sc=True, needs_layout_passes=False),
        name="sc_dispatch",
    )
    def dispatch(h2_hbm, gate_hbm, dest_hbm, out_h_hbm, out_g_hbm,
                 dest_v, src_v, idx_v, pieces_v, gates_v, sem_h, sem_g):
        worker = lax.axis_index("s") * SC_CORES + lax.axis_index("c")
        lo = worker * per_worker
        pltpu.sync_copy(dest_hbm, dest_v)

        @pl.loop(0, per_worker // SC_LANES)
        def _(j):
            j0 = pl.multiple_of(j * SC_LANES, SC_LANES)
            src_v[pl.ds(j0, SC_LANES)] = lax.rem(lo + j0 + lax.iota(I32, SC_LANES), T)

        @pl.loop(0, T // SC_LANES)
        def _(j):
            t0 = pl.multiple_of(j * SC_LANES, SC_LANES)
            d = dest_v[pl.ds(t0, SC_LANES)] - lo
            mine = (d >= 0) & (d < per_worker)
            plsc.store_scatter(src_v, [jnp.where(mine, d, 0)], t0 + lax.iota(I32, SC_LANES), mask=mine)

        @pl.loop(0, per_worker // rows_per_step)
        def _(j):
            off = pl.multiple_of(j * rows_per_step, rows_per_step)
            pltpu.async_copy(gate_hbm.at[src_v.at[pl.ds(off, rows_per_step)]], gates_v, sem_g).wait()
            pltpu.sync_copy(gates_v, out_g_hbm.at[pl.ds(lo + off, rows_per_step)])

        _sc_move_rows(src_v, h2_hbm, out_h_hbm, lo, per_worker, idx_v, pieces_v, sem_h)

    return dispatch(h2_flat, gate_rows, dest)


def _expert_kernel(meta, x_ref, gv_ref, wgu_hbm, wd_hbm, o_ref, wgu_ref, wd_ref, ready_s, sems, *, tile, per_step):
    groups = tile // SUBLANES
    n_used = meta[2, LANES - 1]
    step = pl.program_id(0)

    def weight_copies(g):
        experts = pl.ds(g * EXP_PER_GROUP, EXP_PER_GROUP)
        return (pltpu.make_async_copy(wgu_hbm.at[experts], wgu_ref.at[experts], sems.at[g]),
                pltpu.make_async_copy(wd_hbm.at[experts], wd_ref.at[experts], sems.at[g]))

    def land_through(last_group):
        landed = ready_s[0]
        for g in range(N_EXP_GROUPS):
            @pl.when((g >= landed) & (g <= last_group))
            def _():
                for cp in weight_copies(g):
                    cp.wait()
                if g + 1 < N_EXP_GROUPS:
                    for cp in weight_copies(g + 1):
                        cp.start()
        ready_s[0] = jnp.maximum(landed, last_group + 1)

    @pl.when(step == 0)
    def _():
        ready_s[0] = 0
        for cp in weight_copies(0):
            cp.start()

    def one_tile(k, carry):
        t = step * per_step + k
        rows = pl.ds(pl.multiple_of(k * groups, groups), groups)

        @pl.when(t < n_used)
        def _():
            land_through(meta[0, t] // EXP_PER_GROUP)
            x = _unpack_bf16_pairs(_load_tiles(x_ref.at[rows]))
            gv = gv_ref[pl.ds(pl.multiple_of(k * tile, SUBLANES), tile), :]
            lane = lax.broadcasted_iota(I32, gv.shape, 1)
            out = None
            for e in (meta[0, t], meta[1, t]):
                ge = jnp.sum(jnp.where(lane == EXPERT_LANE0 + e, gv, 0.0), axis=-1, keepdims=True)
                h = jnp.dot(x, wgu_ref[e], preferred_element_type=F32)
                hg = h[:, 0:D_EXPERT]
                hid = (hg * _sigmoid(hg) * h[:, D_EXPERT:2 * D_EXPERT] * ge).astype(BF16)
                y = jnp.dot(hid, wd_ref[e], preferred_element_type=F32)
                out = y if out is None else out + y
            _store_tiles(o_ref.at[rows], _pack_bf16_pairs(out))

        return carry

    lax.fori_loop(0, per_step, one_tile, 0)

    @pl.when(step == pl.num_programs(0) - 1)
    def _():
        land_through(N_EXP_GROUPS - 1)


def _experts(sorted_h2, sorted_gates, meta, wgu, wd, tiling):
    tile, per_step, n_tiles = tiling
    step_rows = tile * per_step
    assert n_tiles * tile == sorted_h2.shape[0] * SUBLANES and n_tiles % per_step == 0

    def last_used(i, meta):
        return jnp.minimum(i, (meta[2, LANES - 1] - 1) // per_step)

    return pl.pallas_call(
        functools.partial(_expert_kernel, tile=tile, per_step=per_step),
        grid_spec=pltpu.PrefetchScalarGridSpec(
            num_scalar_prefetch=1,
            grid=(n_tiles // per_step,),
            in_specs=[
                _tiles_spec(step_rows, last_used, PACKED_CHUNKS),
                pl.BlockSpec((step_rows, LANES), lambda *a: (last_used(*a), 0)),
                pl.BlockSpec(memory_space=pl.ANY), pl.BlockSpec(memory_space=pl.ANY),
            ],
            out_specs=_tiles_spec(step_rows, last_used, PACKED_CHUNKS),
            scratch_shapes=[pltpu.VMEM(wgu.shape, wgu.dtype), pltpu.VMEM(wd.shape, wd.dtype),
                            pltpu.SMEM((1,), I32), pltpu.SemaphoreType.DMA((N_EXP_GROUPS,))],
        ),
        out_shape=jax.ShapeDtypeStruct(_tiles_shape(n_tiles * tile, PACKED_CHUNKS), U32),
        compiler_params=pltpu.CompilerParams(
            dimension_semantics=("arbitrary",), vmem_limit_bytes=V7X_VMEM_LIMIT_BYTES),
        name="moe_experts",
    )(meta, sorted_h2, sorted_gates, wgu, wd)


def _sc_row_gather(table_flat, idx, chunks):
    n = idx.shape[0]
    per_worker = n // SC_WORKERS
    assert n % SC_WORKERS == 0 and per_worker % SC_ROWS_PER_STEP == 0
    mesh = plsc.VectorSubcoreMesh(core_axis_name="c", subcore_axis_name="s")

    @functools.partial(
        pl.kernel, mesh=mesh,
        out_type=jax.ShapeDtypeStruct((n * chunks, LANES), table_flat.dtype),
        scratch_types=[pltpu.VMEM((per_worker,), I32)] + _sc_scratch(chunks, table_flat.dtype)
        + [pltpu.SemaphoreType.DMA],
        compiler_params=pltpu.CompilerParams(use_tc_tiling_on_sc=True, needs_layout_passes=False),
        name="sc_row_gather",
    )
    def gather(table_hbm, idx_hbm, out_hbm, src_v, idx_v, pieces_v, sem):
        worker = lax.axis_index("s") * SC_CORES + lax.axis_index("c")
        lo = worker * per_worker
        pltpu.sync_copy(idx_hbm.at[pl.ds(lo, per_worker)], src_v)
        _sc_move_rows(src_v, table_hbm, out_hbm, lo, per_worker, idx_v, pieces_v, sem)

    return gather(table_flat, idx)


def _final_kernel(x_ref, moe_ref, mod_ref, gf_ref, o_ref):
    y = x_ref[...] + mod_ref[0, 5:6, :] * _unpack_bf16_pairs(_load_tiles(moe_ref)).astype(F32)
    o_ref[...] = _rms(y) * gf_ref[...]


def _final(xmid, moe_rows, mod, mod_row, gf):
    T = xmid.shape[0]
    return pl.pallas_call(
        _final_kernel,
        grid=(T // FINAL_BLOCK,),
        in_specs=[
            pl.BlockSpec((FINAL_BLOCK, D_MODEL), lambda i: (i, 0)),
            _tiles_spec(FINAL_BLOCK, lambda i: i, PACKED_CHUNKS),
            pl.BlockSpec((1, 6, D_MODEL), lambda i: (mod_row(i), 0, 0)),
            pl.BlockSpec((1, D_MODEL), lambda i: (0, 0)),
        ],
        out_specs=pl.BlockSpec((FINAL_BLOCK, D_MODEL), lambda i: (i, 0)),
        out_shape=jax.ShapeDtypeStruct((T, D_MODEL), F32),
        compiler_params=pltpu.CompilerParams(
            dimension_semantics=("arbitrary",), vmem_limit_bytes=V7X_VMEM_LIMIT_BYTES),
        name="moe_final",
    )(xmid, moe_rows, mod, gf)


def _flat(tiles):
    return tiles.reshape(-1, LANES)


def _expert_tiling(T):
    tile = -(-(T * 9) // (8 * N_BUCKETS * 64)) * 64
    per_step = max(1, EXPERT_STEP_ROWS // tile)
    n_tiles = (T + N_BUCKETS * (tile - 1)) // tile
    while n_tiles % per_step or (n_tiles * tile) % (SC_WORKERS * SC_ROWS_PER_STEP):
        n_tiles += 1
    return tile, per_step, n_tiles


def _moe_dispatch(h2_tiles, gate_rows, onehot, tiling):
    T = gate_rows.shape[0]
    tile, _, n_tiles = tiling
    n_rows = n_tiles * tile
    assert n_tiles <= LANES and T % TOKEN_BLOCK == 0
    dest, meta = _plan(onehot, tile)
    sorted_h2, sorted_gates = _sc_dispatch(_flat(h2_tiles), gate_rows, dest, n_rows)
    return sorted_h2.reshape(_tiles_shape(n_rows, PACKED_CHUNKS)), sorted_gates, dest, meta


def _moe_unpermute(moe_sorted_tiles, dest):
    chunks = moe_sorted_tiles.shape[1]
    return _sc_row_gather(_flat(moe_sorted_tiles), dest, chunks).reshape(_tiles_shape(dest.shape[0], chunks))


def _rope_tables(n_tokens):
    t = np.arange(n_tokens)
    row = (t // GRID_W).astype(np.float32)
    col = (t % GRID_W).astype(np.float32)
    freq = np.float32(ROPE_THETA) ** (-np.arange(ROPE_NF, dtype=np.float32) / np.float32(ROPE_NF))
    ang = np.concatenate([row[:, None] * freq] * 2 + [col[:, None] * freq] * 2, axis=-1)
    first = (np.arange(HEAD_DIM) % (2 * ROPE_NF)) < ROPE_NF
    sin = np.sin(ang)
    zero = np.float32(0.0)
    return (jnp.asarray(np.cos(ang)), jnp.asarray(np.where(first, -sin, zero)),
            jnp.asarray(np.where(first, zero, sin)))


def kernel(x_prompt, x_sample, cache_k, cache_v, c, c_ctx, norm1_g, norm2_g, w_ada, b_ada, w_in, q_norm_g, k_norm_g, w_pool, pool_scale, w_branch_a, w_branch_b, w_out, w_router_group, w_router_expert, w_exp_gate, w_exp_up, w_exp_down, final_norm_g):
    assert norm1_g.shape[0] == 1, "single-layer trunk"
    B, L_ctx, _ = x_prompt.shape
    Bs, L_lat, _ = x_sample.shape
    P = cache_k.shape[2]
    assert 1 + Bs <= COND_ROWS

    cond = jnp.concatenate([c_ctx[None, :], c, jnp.zeros((COND_ROWS - 1 - Bs, D_MODEL), F32)], axis=0)
    wpool_bd = jax.scipy.linalg.block_diag(*[w_pool[0, g] for g in range(len(POOL_WINDOWS))])
    mod, w_in_b, wpool_b, wa_b, wb_b, wo_b = _ada(
        cond, w_ada[0], b_ada[0][None, :],
        cast=(w_in[0], wpool_bd, w_branch_a[0], w_branch_b[0], w_out[0]))
    mod = mod.reshape(COND_ROWS, 6, D_MODEL)

    wr = jnp.concatenate([w_router_group[0], w_router_expert[0],
                          jnp.zeros((D_MODEL, LANES - N_EXP_GROUPS - N_EXPERTS), F32)], axis=1)
    wr_hi = wr.astype(BF16)
    wr_lo = (wr - wr_hi.astype(F32)).astype(BF16)
    mix_w = (norm1_g[0][None, :], w_in_b, q_norm_g[0][None, :], k_norm_g[0][None, :],
             wpool_b, pool_scale[0][None, :], wa_b, wb_b, wo_b,
             norm2_g[0][None, :], jnp.concatenate([wr_hi, wr_lo], axis=1))
    gf = final_norm_g[None, :]

    xp2 = x_prompt.reshape(B * L_ctx, D_MODEL)
    xmid_p, h2_p, gate_p, oh_p, knew, vnew, wgu, wd = _mix(
        xp2, mod, lambda i: 0, None, None, mix_w, S=2, L=L_ctx, emit_kv=True, blocks_per_step=2,
        cast=((w_exp_gate[0], w_exp_up[0]), (w_exp_down[0],)))
    tiling_p = _expert_tiling(B * L_ctx)
    sh_p, sg_p, dest_p, meta_p = _moe_dispatch(h2_p, gate_p, oh_p, tiling_p)

    xs2 = x_sample.reshape(Bs * L_lat, D_MODEL)
    cache = (cache_k.reshape(Bs * P * N_KV_HEADS, HEAD_DIM), cache_v.reshape(Bs * P * N_KV_HEADS, HEAD_DIM))
    xmid_s, h2_s, gate_s, oh_s = _mix(xs2, mod, lambda i: 1 + i, cache, _rope_tables(L_lat), mix_w,
                                      S=1, L=L_lat, emit_kv=False, blocks_per_step=1)
    tiling_s = _expert_tiling(Bs * L_lat)
    sh_s, sg_s, dest_s, meta_s = _moe_dispatch(h2_s, gate_s, oh_s, tiling_s)

    moe_p = _moe_unpermute(_experts(sh_p, sg_p, meta_p, wgu, wd, tiling_p), dest_p)
    moe_s = _moe_unpermute(_experts(sh_s, sg_s, meta_s, wgu, wd, tiling_s), dest_s)
    y_prompt = _final(xmid_p, moe_p, mod, lambda i: 0, gf)
    blocks_per_seq = L_lat // FINAL_BLOCK
    y_sample = _final(xmid_s, moe_s, mod, lambda i: 1 + i // blocks_per_seq, gf)

    return (y_prompt.reshape(B, L_ctx, D_MODEL), y_sample.reshape(Bs, L_lat, D_MODEL),
            knew.reshape(B, 1, L_ctx, N_KV_HEADS, HEAD_DIM), vnew.reshape(B, 1, L_ctx, N_KV_HEADS, HEAD_DIM))
```

```python
import functools

import numpy as np
import jax
import jax.numpy as jnp
from jax import lax
from jax.experimental import pallas as pl
from jax.experimental.pallas import tpu as pltpu
from jax.experimental.pallas import tpu_sc as plsc

F32 = jnp.float32
BF16 = jnp.bfloat16
I32 = jnp.int32
U32 = jnp.uint32

D_MODEL = 1024
HEAD_DIM = 128
N_HEADS = 8
N_KV_HEADS = 2
GROUP = N_HEADS // N_KV_HEADS
ATTN_W = N_HEADS * HEAD_DIM
KV_W = N_KV_HEADS * HEAD_DIM
POOL_WINDOWS = (2, 4, 8, 16)
POOL_GC = 128
POOL_W = POOL_GC * len(POOL_WINDOWS)
IN_W = ATTN_W + 2 * KV_W + POOL_W + 2 * D_MODEL
GATE_COL = ATTN_W + 2 * KV_W + POOL_W
GRID_W = 64
ROPE_THETA = 10000.0
ROPE_NF = HEAD_DIM // 4
N_EXP_GROUPS = 4
EXP_PER_GROUP = 4
N_EXPERTS = 16
D_EXPERT = 256
EPS = 1e-6
LOG2_E = 1.4426950408889634

LANES = 128
SUBLANES = 8
COND_ROWS = SUBLANES
POOL_HALO = 8
ROW_BLOCK = 256
ADA_COLS = 768
EXPERT_LANE0 = N_EXP_GROUPS
PAIRS_PER_GROUP = EXP_PER_GROUP * (EXP_PER_GROUP - 1) // 2
N_BUCKETS = N_EXP_GROUPS * PAIRS_PER_GROUP
EXPERT_STEP_ROWS = 1536
TOKEN_BLOCK = 1024
FINAL_BLOCK = 1024
ROW_CHUNKS = D_MODEL // LANES
SC_CORES = 2
SC_SUBCORES = 16
SC_WORKERS = SC_CORES * SC_SUBCORES
SC_LANES = 16
SC_PIECES_PER_GATHER = 128
SC_ROWS_PER_STEP = 64
PACKED_CHUNKS = ROW_CHUNKS // 2
V7X_VMEM_LIMIT_BYTES = 56 * 1024 * 1024


def _sigmoid(x):
    return 1.0 / (1.0 + jnp.exp(-x))


def _rms(x):
    return x * lax.rsqrt(jnp.mean(x * x, axis=-1, keepdims=True) + EPS)


def _resident(shape):
    zeros = (0,) * len(shape)
    return pl.BlockSpec(shape, lambda i, *_: zeros, pipeline_mode=pl.Buffered(1))


def _tiles_shape(n, chunks=ROW_CHUNKS):
    return (n // SUBLANES, chunks, SUBLANES, LANES)


def _tiles_spec(n, block_index, chunks=ROW_CHUNKS):
    return pl.BlockSpec(_tiles_shape(n, chunks), lambda *a: (block_index(*a), 0, 0, 0))


def _store_tiles(ref, x):
    for c in range(ref.shape[1]):
        ref[:, c, :, :] = x[:, c * LANES:(c + 1) * LANES].reshape(x.shape[0] // SUBLANES, SUBLANES, LANES)


def _load_tiles(ref):
    n = ref.shape[0] * SUBLANES
    return jnp.concatenate([ref[:, c, :, :].reshape(n, LANES) for c in range(ref.shape[1])], axis=1)


def _pack_bf16_pairs(x):
    bits = pltpu.bitcast(x.astype(BF16).astype(F32), U32)
    w = x.shape[1] // 2
    return bits[:, :w] | (bits[:, w:] >> 16)


def _unpack_bf16_pairs(words):
    hi = pltpu.bitcast(words & jnp.uint32(0xFFFF0000), F32).astype(BF16)
    lo = pltpu.bitcast(words << 16, F32).astype(BF16)
    return jnp.concatenate([hi, lo], axis=1)


def _row(x):
    return jnp.transpose(jnp.broadcast_to(x, (x.shape[0], LANES)))[0:1, :]


def _ada_kernel(c_ref, w_ref, b_ref, *refs, steps_per_pool_group):
    n_cast = len(refs) // 2 - 1
    c = c_ref[...]
    s = (c * _sigmoid(c)).astype(BF16)
    refs[n_cast + 1][...] = jnp.dot(s, w_ref[...].astype(BF16), preferred_element_type=F32) + b_ref[...]
    for src, dst in zip(refs[:n_cast], refs[n_cast + 2:]):
        dst[...] = src[...].astype(BF16)
    pool_src, pool_dst = refs[n_cast], refs[-1]
    wide = jnp.concatenate([pool_src[0]] * len(POOL_WINDOWS), axis=1)
    lane_group = lax.broadcasted_iota(I32, wide.shape, 1) // POOL_GC
    pool_dst[...] = jnp.where(lane_group == pl.program_id(0) // steps_per_pool_group, wide, 0.0).astype(BF16)


def _ada(cond, w_ada, b_ada, w_pool, cast=()):
    n = w_ada.shape[1]
    n_steps = n // ADA_COLS
    cast_specs = []
    for w in cast:
        assert w.ndim == 2 and w.shape[0] % (n_steps * 2 * SUBLANES) == 0
        cast_specs.append(pl.BlockSpec((w.shape[0] // n_steps, w.shape[1]), lambda j: (j, 0)))
    pool_rows = POOL_W // n_steps
    spg = POOL_GC // pool_rows
    assert w_pool.shape == (len(POOL_WINDOWS), POOL_GC, POOL_GC) and POOL_GC % pool_rows == 0
    assert pool_rows % (2 * SUBLANES) == 0
    pool_in = pl.BlockSpec((1, pool_rows, POOL_GC), lambda j: (j // spg, j % spg, 0))
    pool_out = pl.BlockSpec((pool_rows, POOL_W), lambda j: (j, 0))
    return pl.pallas_call(
        functools.partial(_ada_kernel, steps_per_pool_group=spg),
        grid=(n_steps,),
        in_specs=[
            pl.BlockSpec((COND_ROWS, D_MODEL), lambda j: (0, 0)),
            pl.BlockSpec((D_MODEL, ADA_COLS), lambda j: (0, j)),
            pl.BlockSpec((1, ADA_COLS), lambda j: (0, j)),
        ] + cast_specs + [pool_in],
        out_specs=[pl.BlockSpec((COND_ROWS, ADA_COLS), lambda j: (0, j))] + cast_specs + [pool_out],
        out_shape=[jax.ShapeDtypeStruct((COND_ROWS, n), F32)] + [jax.ShapeDtypeStruct(w.shape, BF16) for w in cast]
        + [jax.ShapeDtypeStruct((POOL_W, POOL_W), BF16)],
        name="ada_mod",
    )(cond, w_ada, b_ada, *cast, w_pool)


def _route(logits):
    lane = lax.broadcasted_iota(I32, logits.shape, 1).astype(F32)
    neg = jnp.float32(-1e30)
    far = jnp.float32(LANES)
    is_g = lane < N_EXP_GROUPS
    gl = jnp.where(is_g, logits, neg)
    gmax = jnp.max(gl, axis=-1, keepdims=True)
    gsel = jnp.min(jnp.where(gl == gmax, lane, far), axis=-1, keepdims=True)
    psel = 1.0 / jnp.sum(jnp.where(is_g, jnp.exp(gl - gmax), 0.0), axis=-1, keepdims=True)
    e_lo = EXPERT_LANE0 + EXP_PER_GROUP * gsel
    el = jnp.where(lane >= e_lo, jnp.where(lane < e_lo + EXP_PER_GROUP, logits, neg), neg)
    v1 = jnp.max(el, axis=-1, keepdims=True)
    i1 = jnp.min(jnp.where(el == v1, lane, far), axis=-1, keepdims=True)
    el2 = jnp.where(lane == i1, neg, el)
    v2 = jnp.max(el2, axis=-1, keepdims=True)
    i2 = jnp.min(jnp.where(el2 == v2, jnp.where(lane == i1, far, lane), far), axis=-1, keepdims=True)
    e2 = jnp.exp(v2 - v1)
    w1 = psel / (1.0 + e2)
    w2 = psel * e2 / (1.0 + e2)
    gate = jnp.where(lane == i1, w1, jnp.where(lane == i2, w2, 0.0))
    a = jnp.minimum(i1, i2) - e_lo
    b = jnp.maximum(i1, i2) - e_lo
    pair = a * (7.0 - a) * 0.5 + (b - a - 1.0)
    return gate, gsel * PAIRS_PER_GROUP + pair


def _mix_kernel(*refs, S, L, P, use_rope, emit_kv, n_cast, n_blocks, U, mod_row):
    it = iter(refs)
    x_ref = next(it)
    mod_ref = next(it)
    if P:
        ck_ref = next(it)
        cv_ref = next(it)
    if use_rope:
        cos_ref = next(it)
        sneg_ref = next(it)
        spos_ref = next(it)
    (g1_ref, win_ref, qg_ref, kg_ref, wpool_ref, pscale_ref, wa_ref, wb_ref, wo_ref,
     g2_ref, wr_ref) = (next(it) for _ in range(11))
    cast_in = [[next(it) for _ in range(n)] for n in n_cast]
    xmid_ref = next(it)
    h2_ref = next(it)
    gate_ref = next(it)
    oh_ref = next(it)
    if emit_kv:
        knew_ref = next(it)
        vnew_ref = next(it)
    cast_out = [next(it) for _ in n_cast]
    q_s, k_s, v_s, xp_s, h_s, attn_s, xm_s, mod2_s = (next(it) for _ in range(8))

    TM = S * L
    RB = ROW_BLOCK
    nrb = TM // RB
    n_steps = n_blocks // U
    score_gain = HEAD_DIM ** -0.5 * LOG2_E
    step = pl.program_id(0)
    block0 = U * jnp.minimum(step, n_steps - 1)
    slot = step % 2

    mod_at = pl.ds(mod_row(jnp.minimum(step, n_steps - 1) // (nrb // U)), 1)
    sh1 = mod_ref[mod_at, 0:D_MODEL]
    gain1 = g1_ref[...] * (1.0 + mod_ref[mod_at, D_MODEL:2 * D_MODEL])
    gt1 = mod_ref[mod_at, 2 * D_MODEL:3 * D_MODEL]
    sh2 = mod_ref[mod_at, 3 * D_MODEL:4 * D_MODEL]
    gain2 = g2_ref[...] * (1.0 + mod_ref[mod_at, 4 * D_MODEL:5 * D_MODEL])
    qg = qg_ref[...] * score_gain
    kg = kg_ref[...]

    def project(r, carry):
        r0 = pl.multiple_of(r * RB, RB)
        s = r0 // L
        o = pl.multiple_of(r0 % L, RB)
        hb = (_rms(x_ref[pl.ds(r0, RB), :]) * gain1 + sh1).astype(BF16)
        h_s[pl.ds(r0, RB), :] = hb
        p1 = jnp.dot(hb, win_ref[:, 0:GATE_COL], preferred_element_type=F32)
        if use_rope:
            cs = cos_ref[pl.ds(o, RB), :]
            sn = sneg_ref[pl.ds(o, RB), :]
            sp = spos_ref[pl.ds(o, RB), :]

        def rope(t):
            return (t * cs + pltpu.roll(t, HEAD_DIM - ROPE_NF, 1) * sn + pltpu.roll(t, ROPE_NF, 1) * sp)

        for hd in range(N_HEADS):
            qh = _rms(p1[:, hd * HEAD_DIM:(hd + 1) * HEAD_DIM]) * qg
            if use_rope:
                qh = rope(qh)
            q_s[hd, pl.ds(r0, RB), :] = qh.astype(BF16)
        for kh in range(N_KV_HEADS):
            c0 = ATTN_W + kh * HEAD_DIM
            kk = _rms(p1[:, c0:c0 + HEAD_DIM]) * kg
            if emit_kv:
                knew_ref[pl.ds(N_KV_HEADS * r0 + kh, RB, stride=N_KV_HEADS), :] = kk
            if use_rope:
                kk = rope(kk)
            k_s[s, pl.ds(P + o, RB), kh * HEAD_DIM:(kh + 1) * HEAD_DIM] = kk.astype(BF16)
        vv = p1[:, ATTN_W + KV_W:ATTN_W + 2 * KV_W]
        if emit_kv:
            for kh in range(N_KV_HEADS):
                vnew_ref[pl.ds(N_KV_HEADS * r0 + kh, RB, stride=N_KV_HEADS), :] = (
                    vv[:, kh * HEAD_DIM:(kh + 1) * HEAD_DIM])
        v_s[s, pl.ds(P + o, RB), :] = vv.astype(BF16)
        xp_s[s, pl.ds(POOL_HALO + o, RB), :] = p1[:, ATTN_W + 2 * KV_W:GATE_COL]
        return carry

    @pl.when(step == 0)
    def _():
        xm_s[1] = jnp.zeros((U * RB, D_MODEL), F32)
        mod2_s[1] = jnp.zeros((2, D_MODEL), F32)

    @pl.when((step < n_steps) & (step % (nrb // U) == 0))
    def _():
        if P:
            for kh in range(N_KV_HEADS):
                cols = slice(kh * HEAD_DIM, (kh + 1) * HEAD_DIM)
                k_s[0, 0:P, cols] = ck_ref[pl.ds(kh, P, stride=N_KV_HEADS), :].astype(BF16)
                v_s[0, 0:P, cols] = cv_ref[pl.ds(kh, P, stride=N_KV_HEADS), :].astype(BF16)
        xp_s[:, 0:POOL_HALO, :] = jnp.zeros((S, POOL_HALO, POOL_W), F32)
        xp_s[:, L + POOL_HALO:L + 2 * POOL_HALO, :] = jnp.zeros((S, POOL_HALO, POOL_W), F32)
        lax.fori_loop(0, TM // RB, project, 0)
        for srcs, dst in zip(cast_in, cast_out):
            col = 0
            for src in srcs:
                dst[..., col:col + src.shape[-1]] = src[...].astype(BF16)
                col += src.shape[-1]

    def mix(u):
        r0 = pl.multiple_of(((block0 + u) % nrb) * RB, RB)
        s = r0 // L
        o = pl.multiple_of(r0 % L, RB)
        attn_u = attn_s.at[u]
        rows = slice(u * RB, (u + 1) * RB)

        for hd in range(N_HEADS):
            kh = hd // GROUP
            k = k_s[s, :, kh * HEAD_DIM:(kh + 1) * HEAD_DIM]
            v = v_s[s, :, kh * HEAD_DIM:(kh + 1) * HEAD_DIM]
            qh = q_s[hd, pl.ds(r0, RB), :]
            sc = lax.dot_general(qh, k, (((1,), (1,)), ((), ())), preferred_element_type=F32)
            e = jnp.exp2(sc - jnp.max(sc, axis=-1, keepdims=True))
            den = jnp.sum(e, axis=-1, keepdims=True)
            oh = jnp.dot(e.astype(BF16), v, preferred_element_type=F32) / den
            attn_u[:, hd * HEAD_DIM:(hd + 1) * HEAD_DIM] = oh.astype(BF16)
        a = jnp.dot(attn_u[...], wa_ref[...], preferred_element_type=F32)

        t = o + lax.broadcasted_iota(I32, (RB, 1), 0)
        RW = RB + 2 * POOL_HALO
        parts = []
        for gi, w in enumerate(POOL_WINDOWS):
            cols = slice(gi * POOL_GC, (gi + 1) * POOL_GC)
            xw = xp_s[s, pl.ds(o, RW), cols]
            run = xw
            span = 1
            while span < w:
                run = run + pltpu.roll(run, span, 0)
                span *= 2
            if w // 2 > 1:
                run = pltpu.roll(run, RW - (w // 2 - 1), 0)
            tot = run[POOL_HALO:POOL_HALO + RB]
            cnt = (jnp.minimum(t + w // 2, L) - jnp.maximum(t - w // 2, 0)).astype(F32)
            parts.append(tot / cnt - xw[POOL_HALO:POOL_HALO + RB])
        dpool = jnp.concatenate(parts, axis=1).astype(BF16)
        pooled = jnp.dot(dpool, wpool_ref[...], preferred_element_type=F32) * pscale_ref[...]
        b = jnp.dot(pooled.astype(BF16), wb_ref[...], preferred_element_type=F32)

        gates = jnp.dot(h_s[pl.ds(r0, RB), :], win_ref[:, GATE_COL:IN_W], preferred_element_type=F32)
        merged = _sigmoid(gates[:, 0:D_MODEL]) * a + _sigmoid(gates[:, D_MODEL:2 * D_MODEL]) * b
        upd = jnp.dot(merged.astype(BF16), wo_ref[...], preferred_element_type=F32)
        xm = x_ref[pl.ds(r0, RB), :] + gt1 * upd
        xmid_ref[rows, :] = xm
        xm_s[slot, rows, :] = xm

    def moe_prep(u):
        rows = slice(u * RB, (u + 1) * RB)
        h2 = _rms(xm_s[1 - slot, rows, :]) * mod2_s[1 - slot, 0:1, :] + mod2_s[1 - slot, 1:2, :]
        hi = h2.astype(BF16)
        lo = (h2 - hi.astype(F32)).astype(BF16)
        l1 = jnp.dot(hi, wr_ref[...], preferred_element_type=F32)
        l2 = jnp.dot(lo, wr_ref[:, 0:LANES], preferred_element_type=F32)
        gate, bucket = _route(l1[:, 0:LANES] + l1[:, LANES:2 * LANES] + l2)
        groups = pl.ds(u * (RB // SUBLANES), RB // SUBLANES)
        _store_tiles(h2_ref.at[groups], _pack_bf16_pairs(h2))
        gate_ref[rows, :] = gate
        lane = lax.broadcasted_iota(I32, (RB, LANES), 1).astype(F32)
        oh_ref[rows, :] = jnp.where(lane == bucket, 1.0, 0.0).astype(BF16)

    mod2_s[slot, 0:1, :] = gain2
    mod2_s[slot, 1:2, :] = sh2

    @pl.when(step < n_steps)
    def _():
        for u in range(U):
            moe_prep(u)
        for u in range(U):
            mix(u)

    @pl.when(step == n_steps)
    def _():
        for u in range(U):
            moe_prep(u)


def _mix(x2d, mod, mod_row, cache, rope_tabs, weights, *, S, L, emit_kv, blocks_per_step, cast=()):
    T = x2d.shape[0]
    TM = S * L
    P = cache[0].shape[0] // (T // L * N_KV_HEADS) if cache is not None else 0
    use_rope = rope_tabs is not None
    assert T % TM == 0 and L % ROW_BLOCK == 0
    assert not (use_rope or P) or S == 1
    Lk = P + L

    args = [x2d, mod]
    nrb = TM // ROW_BLOCK
    n_blocks = T // ROW_BLOCK
    step_rows = blocks_per_step * ROW_BLOCK
    steps_per_group = nrb // blocks_per_step
    n_mix_steps = n_blocks // blocks_per_step
    assert nrb % blocks_per_step == 0

    def mixed(s):
        return jnp.minimum(s, n_mix_steps - 1)

    def group(s):
        return mixed(s) // steps_per_group

    def prepared(s):
        return jnp.maximum(s - 1, 0)

    in_specs = [
        pl.BlockSpec((TM, D_MODEL), lambda s: (group(s), 0)),
        _resident(mod.shape),
    ]
    if P:
        args += list(cache)
        in_specs += [pl.BlockSpec((P * N_KV_HEADS, HEAD_DIM), lambda s: (group(s), 0))] * 2
    if use_rope:
        args += list(rope_tabs)
        in_specs += [_resident((L, HEAD_DIM))] * 3
    args += list(weights)
    in_specs += [_resident(w.shape) for w in weights]
    n_steps = T // TM
    def per_group(shape):
        assert shape[0] % n_steps == 0
        blk = (shape[0] // n_steps,) + shape[1:]
        return pl.BlockSpec(blk, lambda s, n=len(blk): (group(s),) + (0,) * (n - 1))

    cast_out_shapes = [ws[0].shape[:-1] + (sum(w.shape[-1] for w in ws),) for ws in cast]
    for ws in cast:
        args += list(ws)
        in_specs += [per_group(w.shape) for w in ws]

    out_shape = [jax.ShapeDtypeStruct((T, D_MODEL), F32), jax.ShapeDtypeStruct(_tiles_shape(T, PACKED_CHUNKS), U32),
                 jax.ShapeDtypeStruct((T, LANES), F32),
                 jax.ShapeDtypeStruct((T, LANES), BF16)]
    out_specs = [pl.BlockSpec((step_rows, D_MODEL), lambda s: (mixed(s), 0)),
                 _tiles_spec(step_rows, prepared, PACKED_CHUNKS),
                 pl.BlockSpec((step_rows, LANES), lambda s: (prepared(s), 0)),
                 pl.BlockSpec((step_rows, LANES), lambda s: (prepared(s), 0))]
    if emit_kv:
        out_shape += [jax.ShapeDtypeStruct((T * N_KV_HEADS, HEAD_DIM), F32)] * 2
        out_specs += [pl.BlockSpec((TM * N_KV_HEADS, HEAD_DIM), lambda s: (group(s), 0))] * 2
    out_shape += [jax.ShapeDtypeStruct(shp, BF16) for shp in cast_out_shapes]
    out_specs += [per_group(shp) for shp in cast_out_shapes]

    scratch = [
        pltpu.VMEM((N_HEADS, TM, HEAD_DIM), BF16),
        pltpu.VMEM((S, Lk, KV_W), BF16),
        pltpu.VMEM((S, Lk, KV_W), BF16),
        pltpu.VMEM((S, L + 2 * POOL_HALO, POOL_W), F32),
        pltpu.VMEM((TM, D_MODEL), BF16),
        pltpu.VMEM((blocks_per_step, ROW_BLOCK, ATTN_W), BF16),
        pltpu.VMEM((2, step_rows, D_MODEL), F32),
        pltpu.VMEM((2, 2, D_MODEL), F32),
    ]
    kern = functools.partial(_mix_kernel, S=S, L=L, P=P, use_rope=use_rope, emit_kv=emit_kv,
                             n_cast=tuple(len(ws) for ws in cast), n_blocks=n_blocks, U=blocks_per_step,
                             mod_row=mod_row)
    return pl.pallas_call(
        kern,
        grid=(n_mix_steps + 1,),
        in_specs=in_specs,
        out_specs=out_specs,
        out_shape=out_shape,
        scratch_shapes=scratch,
        compiler_params=pltpu.CompilerParams(
            dimension_semantics=("arbitrary",), vmem_limit_bytes=V7X_VMEM_LIMIT_BYTES),
        name="mixer_rope" if use_rope else "mixer_ctx",
    )(*args)


def _plan_kernel(oh_ref, dest_ref, meta_ref, *, n_blocks, tile):
    TB = TOKEN_BLOCK
    lane = lax.broadcasted_iota(I32, (SUBLANES, LANES), 1)

    def count(b, acc):
        oh = oh_ref[pl.ds(pl.multiple_of(b * TB, TB), TB), :].astype(F32)
        return acc + jnp.sum(oh, axis=0, keepdims=True)

    counts = lax.fori_loop(0, n_blocks, count, jnp.zeros((SUBLANES, LANES), F32))
    padded = jnp.floor((counts + (tile - 0.5)) * (1.0 / tile)) * tile
    ends = padded
    step = 1
    while step < LANES:
        ends = ends + jnp.where(lane >= step, pltpu.roll(ends, step, 1), 0.0)
        step *= 2
    starts = ends - padded

    tri = jnp.where(lax.broadcasted_iota(I32, (TB, TB), 1) < lax.broadcasted_iota(I32, (TB, TB), 0),
                    1.0, 0.0).astype(BF16)

    def place(b, seen):
        oh = oh_ref[pl.ds(pl.multiple_of(b * TB, TB), TB), :]
        ohf = oh.astype(F32)
        rank = jnp.dot(tri, oh, preferred_element_type=F32)
        base = (starts + seen)[0:1, :]
        d = jnp.sum(ohf * (rank + base), axis=1, keepdims=True)
        dest_ref[b] = _row(d).astype(I32)
        return seen + jnp.sum(ohf, axis=0, keepdims=True)

    lax.fori_loop(0, n_blocks, place, jnp.zeros((SUBLANES, LANES), F32))

    tile_row0 = lax.broadcasted_iota(I32, (LANES, LANES), 0).astype(F32) * tile
    is_bucket = lax.broadcasted_iota(I32, (LANES, LANES), 1) < N_BUCKETS
    done = jnp.sum(jnp.where(is_bucket, jnp.where(ends[0:1, :] <= tile_row0, 1.0, 0.0), 0.0),
                   axis=1, keepdims=True)
    bkt = jnp.minimum(done, N_BUCKETS - 1.0)
    grp = (jnp.where(bkt >= PAIRS_PER_GROUP, 1.0, 0.0) + jnp.where(bkt >= 2 * PAIRS_PER_GROUP, 1.0, 0.0)
           + jnp.where(bkt >= 3 * PAIRS_PER_GROUP, 1.0, 0.0))
    pair = bkt - PAIRS_PER_GROUP * grp
    a = jnp.where(pair >= 3.0, 1.0, 0.0) + jnp.where(pair >= 5.0, 1.0, 0.0)
    b = pair - a * (7.0 - a) * 0.5 + a + 1.0
    e1 = EXP_PER_GROUP * grp + a
    e2 = EXP_PER_GROUP * grp + b
    meta = jnp.concatenate(
        [_row(e1), _row(e2), jnp.floor(ends[0:1, :] * (1.0 / tile) + 0.5),
         jnp.zeros((SUBLANES - 3, LANES), F32)], axis=0)
    meta_ref[...] = meta.astype(I32)


def _plan(onehot, tile):
    T = onehot.shape[0]
    n_blocks = T // TOKEN_BLOCK
    dest, meta = pl.pallas_call(
        functools.partial(_plan_kernel, n_blocks=n_blocks, tile=tile),
        out_shape=[jax.ShapeDtypeStruct((n_blocks, 1, TOKEN_BLOCK), I32),
                   jax.ShapeDtypeStruct((SUBLANES, LANES), I32)],
        name="moe_plan",
    )(onehot)
    return dest.reshape(T), meta


def _sc_move_rows(src_v, table_hbm, out_hbm, lo, n_rows, idx_v, pieces_v, sem):
    chunks = pieces_v.shape[0] // SC_ROWS_PER_STEP
    lane = lax.iota(I32, SC_LANES)
    row_in_group = lane & (SUBLANES - 1)
    chunk_in_pair = lane >> 3
    rows_per_gather = SC_PIECES_PER_GATHER // chunks

    @pl.loop(0, n_rows // SC_ROWS_PER_STEP)
    def _(step):
        copies = []
        for g in range(SC_ROWS_PER_STEP // rows_per_gather):
            r0 = step * SC_ROWS_PER_STEP + g * rows_per_gather
            for v in range(SC_PIECES_PER_GATHER // SC_LANES):
                group, chunk0 = v // (chunks // 2), 2 * (v % (chunks // 2))
                tok = plsc.load_gather(src_v, [r0 + group * SUBLANES + row_in_group])
                piece = (tok >> 3) * (SUBLANES * chunks) + (chunk0 + chunk_in_pair) * SUBLANES + (tok & 7)
                idx_v[pl.ds(g * SC_PIECES_PER_GATHER + v * SC_LANES, SC_LANES)] = piece
            window = pl.ds(g * SC_PIECES_PER_GATHER, SC_PIECES_PER_GATHER)
            copies.append(pltpu.async_copy(table_hbm.at[idx_v.at[window]], pieces_v.at[window], sem))
        for cp in copies:
            cp.wait()
        first = pl.multiple_of((lo + step * SC_ROWS_PER_STEP) * chunks, SC_ROWS_PER_STEP * chunks)
        pltpu.sync_copy(pieces_v, out_hbm.at[pl.ds(first, SC_ROWS_PER_STEP * chunks)])


def _sc_scratch(chunks, dtype):
    return [pltpu.VMEM((SC_ROWS_PER_STEP * chunks,), I32), pltpu.VMEM((SC_ROWS_PER_STEP * chunks, LANES), dtype)]


def _sc_dispatch(h2_flat, gate_rows, dest, n_rows):
    T = dest.shape[0]
    per_worker = n_rows // SC_WORKERS
    rows_per_step = SC_ROWS_PER_STEP
    chunks = h2_flat.shape[0] // T
    assert n_rows % SC_WORKERS == 0 and per_worker % rows_per_step == 0 and T % SC_LANES == 0
    mesh = plsc.VectorSubcoreMesh(core_axis_name="c", subcore_axis_name="s")

    @functools.partial(
        pl.kernel, mesh=mesh,
        out_type=[jax.ShapeDtypeStruct((n_rows * chunks, LANES), h2_flat.dtype),
                  jax.ShapeDtypeStruct((n_rows, LANES), F32)],
        scratch_types=[pltpu.VMEM((T,), I32), pltpu.VMEM((per_worker,), I32)]
        + _sc_scratch(chunks, h2_flat.dtype)
        + [pltpu.VMEM((rows_per_step, LANES), F32), pltpu.SemaphoreType.DMA, pltpu.SemaphoreType.DMA],
        compiler_params=pltpu.CompilerParams(use_tc_tiling_on_sc=True, needs_layout_passes=False),
        name="sc_dispatch",
    )
    def dispatch(h2_hbm, gate_hbm, dest_hbm, out_h_hbm, out_g_hbm,
                 dest_v, src_v, idx_v, pieces_v, gates_v, sem_h, sem_g):
        worker = lax.axis_index("s") * SC_CORES + lax.axis_index("c")
        lo = worker * per_worker
        pltpu.sync_copy(dest_hbm, dest_v)

        @pl.loop(0, per_worker // SC_LANES)
        def _(j):
            j0 = pl.multiple_of(j * SC_LANES, SC_LANES)
            src_v[pl.ds(j0, SC_LANES)] = lax.rem(lo + j0 + lax.iota(I32, SC_LANES), T)

        @pl.loop(0, T // SC_LANES)
        def _(j):
            t0 = pl.multiple_of(j * SC_LANES, SC_LANES)
            d = dest_v[pl.ds(t0, SC_LANES)] - lo
            mine = (d >= 0) & (d < per_worker)
            plsc.store_scatter(src_v, [jnp.where(mine, d, 0)], t0 + lax.iota(I32, SC_LANES), mask=mine)

        @pl.loop(0, per_worker // rows_per_step)
        def _(j):
            off = pl.multiple_of(j * rows_per_step, rows_per_step)
            pltpu.async_copy(gate_hbm.at[src_v.at[pl.ds(off, rows_per_step)]], gates_v, sem_g).wait()
            pltpu.sync_copy(gates_v, out_g_hbm.at[pl.ds(lo + off, rows_per_step)])

        _sc_move_rows(src_v, h2_hbm, out_h_hbm, lo, per_worker, idx_v, pieces_v, sem_h)

    return dispatch(h2_flat, gate_rows, dest)


def _expert_kernel(meta, x_ref, gv_ref, wgu_hbm, wd_hbm, o_ref, wgu_ref, wd_ref, ready_s, sems, *, tile, per_step):
    groups = tile // SUBLANES
    n_used = meta[2, LANES - 1]
    step = pl.program_id(0)

    def weight_copies(g):
        experts = pl.ds(g * EXP_PER_GROUP, EXP_PER_GROUP)
        return (pltpu.make_async_copy(wgu_hbm.at[experts], wgu_ref.at[experts], sems.at[g]),
                pltpu.make_async_copy(wd_hbm.at[experts], wd_ref.at[experts], sems.at[g]))

    def land_through(last_group):
        landed = ready_s[0]
        for g in range(N_EXP_GROUPS):
            @pl.when((g >= landed) & (g <= last_group))
            def _():
                for cp in weight_copies(g):
                    cp.wait()
                if g + 1 < N_EXP_GROUPS:
                    for cp in weight_copies(g + 1):
                        cp.start()
        ready_s[0] = jnp.maximum(landed, last_group + 1)

    @pl.when(step == 0)
    def _():
        ready_s[0] = 0
        for cp in weight_copies(0):
            cp.start()

    def one_tile(k, carry):
        t = step * per_step + k
        rows = pl.ds(pl.multiple_of(k * groups, groups), groups)

        @pl.when(t < n_used)
        def _():
            land_through(meta[0, t] // EXP_PER_GROUP)
            x = _unpack_bf16_pairs(_load_tiles(x_ref.at[rows]))
            gv = gv_ref[pl.ds(pl.multiple_of(k * tile, SUBLANES), tile), :]
            lane = lax.broadcasted_iota(I32, gv.shape, 1)
            out = None
            for e in (meta[0, t], meta[1, t]):
                ge = jnp.sum(jnp.where(lane == EXPERT_LANE0 + e, gv, 0.0), axis=-1, keepdims=True)
                h = jnp.dot(x, wgu_ref[e], preferred_element_type=F32)
                hg = h[:, 0:D_EXPERT]
                hid = (hg * _sigmoid(hg) * h[:, D_EXPERT:2 * D_EXPERT] * ge).astype(BF16)
                y = jnp.dot(hid, wd_ref[e], preferred_element_type=F32)
                out = y if out is None else out + y
            _store_tiles(o_ref.at[rows], _pack_bf16_pairs(out))

        return carry

    lax.fori_loop(0, per_step, one_tile, 0)

    @pl.when(step == pl.num_programs(0) - 1)
    def _():
        land_through(N_EXP_GROUPS - 1)


def _experts(sorted_h2, sorted_gates, meta, wgu, wd, tiling):
    tile, per_step, n_tiles = tiling
    step_rows = tile * per_step
    assert n_tiles * tile == sorted_h2.shape[0] * SUBLANES and n_tiles % per_step == 0

    def last_used(i, meta):
        return jnp.minimum(i, (meta[2, LANES - 1] - 1) // per_step)

    return pl.pallas_call(
        functools.partial(_expert_kernel, tile=tile, per_step=per_step),
        grid_spec=pltpu.PrefetchScalarGridSpec(
            num_scalar_prefetch=1,
            grid=(n_tiles // per_step,),
            in_specs=[
                _tiles_spec(step_rows, last_used, PACKED_CHUNKS),
                pl.BlockSpec((step_rows, LANES), lambda *a: (last_used(*a), 0)),
                pl.BlockSpec(memory_space=pl.ANY), pl.BlockSpec(memory_space=pl.ANY),
            ],
            out_specs=_tiles_spec(step_rows, last_used, PACKED_CHUNKS),
            scratch_shapes=[pltpu.VMEM(wgu.shape, wgu.dtype), pltpu.VMEM(wd.shape, wd.dtype),
                            pltpu.SMEM((1,), I32), pltpu.SemaphoreType.DMA((N_EXP_GROUPS,))],
        ),
        out_shape=jax.ShapeDtypeStruct(_tiles_shape(n_tiles * tile, PACKED_CHUNKS), U32),
        compiler_params=pltpu.CompilerParams(
            dimension_semantics=("arbitrary",), vmem_limit_bytes=V7X_VMEM_LIMIT_BYTES),
        name="moe_experts",
    )(meta, sorted_h2, sorted_gates, wgu, wd)


def _sc_row_gather(table_flat, idx, chunks):
    n = idx.shape[0]
    per_worker = n // SC_WORKERS
    assert n % SC_WORKERS == 0 and per_worker % SC_ROWS_PER_STEP == 0
    mesh = plsc.VectorSubcoreMesh(core_axis_name="c", subcore_axis_name="s")

    @functools.partial(
        pl.kernel, mesh=mesh,
        out_type=jax.ShapeDtypeStruct((n * chunks, LANES), table_flat.dtype),
        scratch_types=[pltpu.VMEM((per_worker,), I32)] + _sc_scratch(chunks, table_flat.dtype)
        + [pltpu.SemaphoreType.DMA],
        compiler_params=pltpu.CompilerParams(use_tc_tiling_on_sc=True, needs_layout_passes=False),
        name="sc_row_gather",
    )
    def gather(table_hbm, idx_hbm, out_hbm, src_v, idx_v, pieces_v, sem):
        worker = lax.axis_index("s") * SC_CORES + lax.axis_index("c")
        lo = worker * per_worker
        pltpu.sync_copy(idx_hbm.at[pl.ds(lo, per_worker)], src_v)
        _sc_move_rows(src_v, table_hbm, out_hbm, lo, per_worker, idx_v, pieces_v, sem)

    return gather(table_flat, idx)


def _final_kernel(x_ref, moe_ref, gt2_ref, gf_ref, o_ref, *, mod_row):
    gt2 = gt2_ref[pl.ds(mod_row(pl.program_id(0)), 1), :]
    y = x_ref[...] + gt2 * _unpack_bf16_pairs(_load_tiles(moe_ref)).astype(F32)
    o_ref[...] = _rms(y) * gf_ref[...]


def _final(xmid, moe_rows, mod, mod_row, gf):
    T = xmid.shape[0]
    return pl.pallas_call(
        functools.partial(_final_kernel, mod_row=mod_row),
        grid=(T // FINAL_BLOCK,),
        in_specs=[
            pl.BlockSpec((FINAL_BLOCK, D_MODEL), lambda i: (i, 0)),
            _tiles_spec(FINAL_BLOCK, lambda i: i, PACKED_CHUNKS),
            pl.BlockSpec((COND_ROWS, D_MODEL), lambda i: (0, 5)),
            pl.BlockSpec((1, D_MODEL), lambda i: (0, 0)),
        ],
        out_specs=pl.BlockSpec((FINAL_BLOCK, D_MODEL), lambda i: (i, 0)),
        out_shape=jax.ShapeDtypeStruct((T, D_MODEL), F32),
        compiler_params=pltpu.CompilerParams(
            dimension_semantics=("arbitrary",), vmem_limit_bytes=V7X_VMEM_LIMIT_BYTES),
        name="moe_final",
    )(xmid, moe_rows, mod, gf)


def _flat(tiles):
    return tiles.reshape(-1, LANES)


def _expert_tiling(T):
    tile = -(-(T * 9) // (8 * N_BUCKETS * 64)) * 64
    per_step = max(1, EXPERT_STEP_ROWS // tile)
    n_tiles = (T + N_BUCKETS * (tile - 1)) // tile
    while n_tiles % per_step or (n_tiles * tile) % (SC_WORKERS * SC_ROWS_PER_STEP):
        n_tiles += 1
    return tile, per_step, n_tiles


def _moe_dispatch(h2_tiles, gate_rows, onehot, tiling):
    T = gate_rows.shape[0]
    tile, _, n_tiles = tiling
    n_rows = n_tiles * tile
    assert n_tiles <= LANES and T % TOKEN_BLOCK == 0
    dest, meta = _plan(onehot, tile)
    sorted_h2, sorted_gates = _sc_dispatch(_flat(h2_tiles), gate_rows, dest, n_rows)
    return sorted_h2.reshape(_tiles_shape(n_rows, PACKED_CHUNKS)), sorted_gates, dest, meta


def _moe_unpermute(moe_sorted_tiles, dest):
    chunks = moe_sorted_tiles.shape[1]
    return _sc_row_gather(_flat(moe_sorted_tiles), dest, chunks).reshape(_tiles_shape(dest.shape[0], chunks))


def _rope_tables(n_tokens):
    t = np.arange(n_tokens)
    row = (t // GRID_W).astype(np.float32)
    col = (t % GRID_W).astype(np.float32)
    freq = np.float32(ROPE_THETA) ** (-np.arange(ROPE_NF, dtype=np.float32) / np.float32(ROPE_NF))
    ang = np.concatenate([row[:, None] * freq] * 2 + [col[:, None] * freq] * 2, axis=-1)
    first = (np.arange(HEAD_DIM) % (2 * ROPE_NF)) < ROPE_NF
    sin = np.sin(ang)
    zero = np.float32(0.0)
    return (jnp.asarray(np.cos(ang)), jnp.asarray(np.where(first, -sin, zero)),
            jnp.asarray(np.where(first, zero, sin)))


def kernel(x_prompt, x_sample, cache_k, cache_v, c, c_ctx, norm1_g, norm2_g, w_ada, b_ada, w_in, q_norm_g, k_norm_g, w_pool, pool_scale, w_branch_a, w_branch_b, w_out, w_router_group, w_router_expert, w_exp_gate, w_exp_up, w_exp_down, final_norm_g):
    assert norm1_g.shape[0] == 1, "single-layer trunk"
    B, L_ctx, _ = x_prompt.shape
    Bs, L_lat, _ = x_sample.shape
    P = cache_k.shape[2]
    assert 1 + Bs <= COND_ROWS

    cond = jnp.concatenate([c_ctx[None, :], c, jnp.zeros((COND_ROWS - 1 - Bs, D_MODEL), F32)], axis=0)
    mod, w_in_b, wa_b, wb_b, wo_b, wpool_b = _ada(
        cond, w_ada[0], b_ada[0][None, :], w_pool[0],
        cast=(w_in[0], w_branch_a[0], w_branch_b[0], w_out[0]))

    wr = jnp.concatenate([w_router_group[0], w_router_expert[0],
                          jnp.zeros((D_MODEL, LANES - N_EXP_GROUPS - N_EXPERTS), F32)], axis=1)
    wr_hi = wr.astype(BF16)
    wr_lo = (wr - wr_hi.astype(F32)).astype(BF16)
    mix_w = (norm1_g[0][None, :], w_in_b, q_norm_g[0][None, :], k_norm_g[0][None, :],
             wpool_b, pool_scale[0][None, :], wa_b, wb_b, wo_b,
             norm2_g[0][None, :], jnp.concatenate([wr_hi, wr_lo], axis=1))
    gf = final_norm_g[None, :]

    xp2 = x_prompt.reshape(B * L_ctx, D_MODEL)
    xmid_p, h2_p, gate_p, oh_p, knew, vnew, wgu, wd = _mix(
        xp2, mod, lambda i: 0, None, None, mix_w, S=2, L=L_ctx, emit_kv=True, blocks_per_step=2,
        cast=((w_exp_gate[0], w_exp_up[0]), (w_exp_down[0],)))
    tiling_p = _expert_tiling(B * L_ctx)
    sh_p, sg_p, dest_p, meta_p = _moe_dispatch(h2_p, gate_p, oh_p, tiling_p)

    xs2 = x_sample.reshape(Bs * L_lat, D_MODEL)
    cache = (cache_k.reshape(Bs * P * N_KV_HEADS, HEAD_DIM), cache_v.reshape(Bs * P * N_KV_HEADS, HEAD_DIM))
    xmid_s, h2_s, gate_s, oh_s = _mix(xs2, mod, lambda i: 1 + i, cache, _rope_tables(L_lat), mix_w,
                                      S=1, L=L_lat, emit_kv=False, blocks_per_step=1)
    tiling_s = _expert_tiling(Bs * L_lat)
    sh_s, sg_s, dest_s, meta_s = _moe_dispatch(h2_s, gate_s, oh_s, tiling_s)

    moe_p = _moe_unpermute(_experts(sh_p, sg_p, meta_p, wgu, wd, tiling_p), dest_p)
    moe_s = _moe_unpermute(_experts(sh_s, sg_s, meta_s, wgu, wd, tiling_s), dest_s)
    y_prompt = _final(xmid_p, moe_p, mod, lambda i: 0, gf)
    blocks_per_seq = L_lat // FINAL_BLOCK
    y_sample = _final(xmid_s, moe_s, mod, lambda i: 1 + i // blocks_per_seq, gf)

    return (y_prompt.reshape(B, L_ctx, D_MODEL), y_sample.reshape(Bs, L_lat, D_MODEL),
            knew.reshape(B, 1, L_ctx, N_KV_HEADS, HEAD_DIM), vnew.reshape(B, 1, L_ctx, N_KV_HEADS, HEAD_DIM))
```

```python
import functools

import numpy as np
import jax
import jax.numpy as jnp
from jax import lax
from jax.experimental import pallas as pl
from jax.experimental.pallas import tpu as pltpu
from jax.experimental.pallas import tpu_sc as plsc

F32 = jnp.float32
BF16 = jnp.bfloat16
I32 = jnp.int32
U32 = jnp.uint32

D_MODEL = 1024
HEAD_DIM = 128
N_HEADS = 8
N_KV_HEADS = 2
GROUP = N_HEADS // N_KV_HEADS
ATTN_W = N_HEADS * HEAD_DIM
KV_W = N_KV_HEADS * HEAD_DIM
POOL_WINDOWS = (2, 4, 8, 16)
POOL_GC = 128
POOL_W = POOL_GC * len(POOL_WINDOWS)
IN_W = ATTN_W + 2 * KV_W + POOL_W + 2 * D_MODEL
GATE_COL = ATTN_W + 2 * KV_W + POOL_W
GRID_W = 64
ROPE_THETA = 10000.0
ROPE_NF = HEAD_DIM // 4
N_EXP_GROUPS = 4
EXP_PER_GROUP = 4
N_EXPERTS = 16
D_EXPERT = 256
EPS = 1e-6
LOG2_E = 1.4426950408889634

LANES = 128
SUBLANES = 8
COND_ROWS = SUBLANES
POOL_HALO = 8
ROW_BLOCK = 256
ADA_COLS = 768
EXPERT_LANE0 = N_EXP_GROUPS
PAIRS_PER_GROUP = EXP_PER_GROUP * (EXP_PER_GROUP - 1) // 2
N_BUCKETS = N_EXP_GROUPS * PAIRS_PER_GROUP
EXPERT_STEP_ROWS = 1536
TOKEN_BLOCK = 1024
FINAL_BLOCK = 1024
ROW_CHUNKS = D_MODEL // LANES
SC_CORES = 2
SC_SUBCORES = 16
SC_WORKERS = SC_CORES * SC_SUBCORES
SC_LANES = 16
SC_PIECES_PER_GATHER = 128
SC_ROWS_PER_STEP = 64
PACKED_CHUNKS = ROW_CHUNKS // 2
V7X_VMEM_LIMIT_BYTES = 56 * 1024 * 1024


def _sigmoid(x):
    return 1.0 / (1.0 + jnp.exp(-x))


def _rms(x):
    return x * lax.rsqrt(jnp.mean(x * x, axis=-1, keepdims=True) + EPS)


def _resident(shape):
    zeros = (0,) * len(shape)
    return pl.BlockSpec(shape, lambda i, *_: zeros, pipeline_mode=pl.Buffered(1))


def _tiles_shape(n, chunks=ROW_CHUNKS):
    return (n // SUBLANES, chunks, SUBLANES, LANES)


def _tiles_spec(n, block_index, chunks=ROW_CHUNKS):
    return pl.BlockSpec(_tiles_shape(n, chunks), lambda *a: (block_index(*a), 0, 0, 0))


def _store_tiles(ref, x):
    for c in range(ref.shape[1]):
        ref[:, c, :, :] = x[:, c * LANES:(c + 1) * LANES].reshape(x.shape[0] // SUBLANES, SUBLANES, LANES)


def _load_tiles(ref):
    n = ref.shape[0] * SUBLANES
    return jnp.concatenate([ref[:, c, :, :].reshape(n, LANES) for c in range(ref.shape[1])], axis=1)


def _pack_bf16_pairs(x):
    bits = pltpu.bitcast(x.astype(BF16).astype(F32), U32)
    w = x.shape[1] // 2
    return bits[:, :w] | (bits[:, w:] >> 16)


def _unpack_bf16_pairs(words):
    hi = pltpu.bitcast(words & jnp.uint32(0xFFFF0000), F32).astype(BF16)
    lo = pltpu.bitcast(words << 16, F32).astype(BF16)
    return jnp.concatenate([hi, lo], axis=1)


def _row(x):
    return jnp.transpose(jnp.broadcast_to(x, (x.shape[0], LANES)))[0:1, :]


def _ada_kernel(c_ref, w_ref, b_ref, *refs, steps_per_pool_group):
    n_cast = len(refs) // 2 - 1
    c = c_ref[...]
    s = (c * _sigmoid(c)).astype(BF16)
    refs[n_cast + 1][...] = jnp.dot(s, w_ref[...].astype(BF16), preferred_element_type=F32) + b_ref[...]
    for src, dst in zip(refs[:n_cast], refs[n_cast + 2:]):
        dst[...] = src[...].astype(BF16)
    pool_src, pool_dst = refs[n_cast], refs[-1]
    wide = jnp.concatenate([pool_src[0]] * len(POOL_WINDOWS), axis=1)
    lane_group = lax.broadcasted_iota(I32, wide.shape, 1) // POOL_GC
    pool_dst[...] = jnp.where(lane_group == pl.program_id(0) // steps_per_pool_group, wide, 0.0).astype(BF16)


def _ada(cond, w_ada, b_ada, w_pool, cast=()):
    n = w_ada.shape[1]
    n_steps = n // ADA_COLS
    cast_specs = []
    for w in cast:
        assert w.ndim == 2 and w.shape[0] % (n_steps * 2 * SUBLANES) == 0
        cast_specs.append(pl.BlockSpec((w.shape[0] // n_steps, w.shape[1]), lambda j: (j, 0)))
    pool_rows = POOL_W // n_steps
    spg = POOL_GC // pool_rows
    assert w_pool.shape == (len(POOL_WINDOWS), POOL_GC, POOL_GC) and POOL_GC % pool_rows == 0
    assert pool_rows % (2 * SUBLANES) == 0
    pool_in = pl.BlockSpec((1, pool_rows, POOL_GC), lambda j: (j // spg, j % spg, 0))
    pool_out = pl.BlockSpec((pool_rows, POOL_W), lambda j: (j, 0))
    return pl.pallas_call(
        functools.partial(_ada_kernel, steps_per_pool_group=spg),
        grid=(n_steps,),
        in_specs=[
            pl.BlockSpec((COND_ROWS, D_MODEL), lambda j: (0, 0)),
            pl.BlockSpec((D_MODEL, ADA_COLS), lambda j: (0, j)),
            pl.BlockSpec((1, ADA_COLS), lambda j: (0, j)),
        ] + cast_specs + [pool_in],
        out_specs=[pl.BlockSpec((COND_ROWS, ADA_COLS), lambda j: (0, j))] + cast_specs + [pool_out],
        out_shape=[jax.ShapeDtypeStruct((COND_ROWS, n), F32)] + [jax.ShapeDtypeStruct(w.shape, BF16) for w in cast]
        + [jax.ShapeDtypeStruct((POOL_W, POOL_W), BF16)],
        name="ada_mod",
    )(cond, w_ada, b_ada, *cast, w_pool)


def _route(logits):
    lane = lax.broadcasted_iota(I32, logits.shape, 1).astype(F32)
    neg = jnp.float32(-1e30)
    far = jnp.float32(LANES)
    is_g = lane < N_EXP_GROUPS
    gl = jnp.where(is_g, logits, neg)
    gmax = jnp.max(gl, axis=-1, keepdims=True)
    gsel = jnp.min(jnp.where(gl == gmax, lane, far), axis=-1, keepdims=True)
    psel = 1.0 / jnp.sum(jnp.where(is_g, jnp.exp(gl - gmax), 0.0), axis=-1, keepdims=True)
    e_lo = EXPERT_LANE0 + EXP_PER_GROUP * gsel
    el = jnp.where(lane >= e_lo, jnp.where(lane < e_lo + EXP_PER_GROUP, logits, neg), neg)
    v1 = jnp.max(el, axis=-1, keepdims=True)
    i1 = jnp.min(jnp.where(el == v1, lane, far), axis=-1, keepdims=True)
    el2 = jnp.where(lane == i1, neg, el)
    v2 = jnp.max(el2, axis=-1, keepdims=True)
    i2 = jnp.min(jnp.where(el2 == v2, jnp.where(lane == i1, far, lane), far), axis=-1, keepdims=True)
    e2 = jnp.exp(v2 - v1)
    w1 = psel / (1.0 + e2)
    w2 = psel * e2 / (1.0 + e2)
    gate = jnp.where(lane == i1, w1, jnp.where(lane == i2, w2, 0.0))
    a = jnp.minimum(i1, i2) - e_lo
    b = jnp.maximum(i1, i2) - e_lo
    pair = a * (7.0 - a) * 0.5 + (b - a - 1.0)
    return gate, gsel * PAIRS_PER_GROUP + pair


def _mix_kernel(*refs, S, L, P, use_rope, emit_kv, n_cast, n_blocks, U, mod_row):
    it = iter(refs)
    x_ref = next(it)
    mod_ref = next(it)
    if P:
        ck_ref = next(it)
        cv_ref = next(it)
    if use_rope:
        cos_ref = next(it)
        sneg_ref = next(it)
        spos_ref = next(it)
    (g1_ref, win_ref, qg_ref, kg_ref, wpool_ref, pscale_ref, wa_ref, wb_ref, wo_ref,
     g2_ref, wr_ref) = (next(it) for _ in range(11))
    cast_in = [[next(it) for _ in range(n)] for n in n_cast]
    xmid_ref = next(it)
    h2_ref = next(it)
    gate_ref = next(it)
    oh_ref = next(it)
    if emit_kv:
        knew_ref = next(it)
        vnew_ref = next(it)
    cast_out = [next(it) for _ in n_cast]
    q_s, k_s, v_s, xp_s, h_s, attn_s, xm_s, mod2_s = (next(it) for _ in range(8))

    TM = S * L
    RB = ROW_BLOCK
    nrb = TM // RB
    n_steps = n_blocks // U
    score_gain = HEAD_DIM ** -0.5 * LOG2_E
    step = pl.program_id(0)
    block0 = U * jnp.minimum(step, n_steps - 1)
    slot = step % 2

    mod_at = pl.ds(mod_row(jnp.minimum(step, n_steps - 1) // (nrb // U)), 1)
    sh1 = mod_ref[mod_at, 0:D_MODEL]
    gain1 = g1_ref[...] * (1.0 + mod_ref[mod_at, D_MODEL:2 * D_MODEL])
    gt1 = mod_ref[mod_at, 2 * D_MODEL:3 * D_MODEL]
    sh2 = mod_ref[mod_at, 3 * D_MODEL:4 * D_MODEL]
    gain2 = g2_ref[...] * (1.0 + mod_ref[mod_at, 4 * D_MODEL:5 * D_MODEL])
    qg = qg_ref[...] * score_gain
    kg = kg_ref[...]

    def project(r, carry):
        r0 = pl.multiple_of(r * RB, RB)
        s = r0 // L
        o = pl.multiple_of(r0 % L, RB)
        hb = (_rms(x_ref[pl.ds(r0, RB), :]) * gain1 + sh1).astype(BF16)
        h_s[pl.ds(r0, RB), :] = hb
        p1 = jnp.dot(hb, win_ref[:, 0:GATE_COL], preferred_element_type=F32)
        if use_rope:
            cs = cos_ref[pl.ds(o, RB), :]
            sn = sneg_ref[pl.ds(o, RB), :]
            sp = spos_ref[pl.ds(o, RB), :]

        def rope(t):
            return (t * cs + pltpu.roll(t, HEAD_DIM - ROPE_NF, 1) * sn + pltpu.roll(t, ROPE_NF, 1) * sp)

        for hd in range(N_HEADS):
            qh = _rms(p1[:, hd * HEAD_DIM:(hd + 1) * HEAD_DIM]) * qg
            if use_rope:
                qh = rope(qh)
            q_s[hd, pl.ds(r0, RB), :] = qh.astype(BF16)
        for kh in range(N_KV_HEADS):
            c0 = ATTN_W + kh * HEAD_DIM
            kk = _rms(p1[:, c0:c0 + HEAD_DIM]) * kg
            if emit_kv:
                knew_ref[pl.ds(N_KV_HEADS * r0 + kh, RB, stride=N_KV_HEADS), :] = kk
            if use_rope:
                kk = rope(kk)
            k_s[s, pl.ds(P + o, RB), kh * HEAD_DIM:(kh + 1) * HEAD_DIM] = kk.astype(BF16)
        vv = p1[:, ATTN_W + KV_W:ATTN_W + 2 * KV_W]
        if emit_kv:
            for kh in range(N_KV_HEADS):
                vnew_ref[pl.ds(N_KV_HEADS * r0 + kh, RB, stride=N_KV_HEADS), :] = (
                    vv[:, kh * HEAD_DIM:(kh + 1) * HEAD_DIM])
        v_s[s, pl.ds(P + o, RB), :] = vv.astype(BF16)
        xp_s[s, pl.ds(POOL_HALO + o, RB), :] = p1[:, ATTN_W + 2 * KV_W:GATE_COL]
        return carry

    @pl.when(step == 0)
    def _():
        xm_s[1] = jnp.zeros((U * RB, D_MODEL), F32)
        mod2_s[1] = jnp.zeros((2, D_MODEL), F32)

    @pl.when((step < n_steps) & (step % (nrb // U) == 0))
    def _():
        if P:
            for kh in range(N_KV_HEADS):
                cols = slice(kh * HEAD_DIM, (kh + 1) * HEAD_DIM)
                k_s[0, 0:P, cols] = ck_ref[pl.ds(kh, P, stride=N_KV_HEADS), :].astype(BF16)
                v_s[0, 0:P, cols] = cv_ref[pl.ds(kh, P, stride=N_KV_HEADS), :].astype(BF16)
        xp_s[:, 0:POOL_HALO, :] = jnp.zeros((S, POOL_HALO, POOL_W), F32)
        xp_s[:, L + POOL_HALO:L + 2 * POOL_HALO, :] = jnp.zeros((S, POOL_HALO, POOL_W), F32)
        lax.fori_loop(0, TM // RB, project, 0)
        for srcs, dst in zip(cast_in, cast_out):
            col = 0
            for src in srcs:
                dst[..., col:col + src.shape[-1]] = src[...].astype(BF16)
                col += src.shape[-1]

    def mix(u):
        r0 = pl.multiple_of(((block0 + u) % nrb) * RB, RB)
        s = r0 // L
        o = pl.multiple_of(r0 % L, RB)
        attn_u = attn_s.at[u]
        rows = slice(u * RB, (u + 1) * RB)

        for hd in range(N_HEADS):
            kh = hd // GROUP
            k = k_s[s, :, kh * HEAD_DIM:(kh + 1) * HEAD_DIM]
            v = v_s[s, :, kh * HEAD_DIM:(kh + 1) * HEAD_DIM]
            qh = q_s[hd, pl.ds(r0, RB), :]
            sc = lax.dot_general(qh, k, (((1,), (1,)), ((), ())), preferred_element_type=F32)
            e = jnp.exp2(sc - jnp.max(sc, axis=-1, keepdims=True))
            den = jnp.sum(e, axis=-1, keepdims=True)
            oh = jnp.dot(e.astype(BF16), v, preferred_element_type=F32) / den
            attn_u[:, hd * HEAD_DIM:(hd + 1) * HEAD_DIM] = oh.astype(BF16)
        a = jnp.dot(attn_u[...], wa_ref[...], preferred_element_type=F32)

        t = o + lax.broadcasted_iota(I32, (RB, 1), 0)
        RW = RB + 2 * POOL_HALO
        parts = []
        for gi, w in enumerate(POOL_WINDOWS):
            cols = slice(gi * POOL_GC, (gi + 1) * POOL_GC)
            xw = xp_s[s, pl.ds(o, RW), cols]
            run = xw
            span = 1
            while span < w:
                run = run + pltpu.roll(run, span, 0)
                span *= 2
            if w // 2 > 1:
                run = pltpu.roll(run, RW - (w // 2 - 1), 0)
            tot = run[POOL_HALO:POOL_HALO + RB]
            cnt = (jnp.minimum(t + w // 2, L) - jnp.maximum(t - w // 2, 0)).astype(F32)
            parts.append(tot / cnt - xw[POOL_HALO:POOL_HALO + RB])
        dpool = jnp.concatenate(parts, axis=1).astype(BF16)
        pooled = jnp.dot(dpool, wpool_ref[...], preferred_element_type=F32) * pscale_ref[...]
        b = jnp.dot(pooled.astype(BF16), wb_ref[...], preferred_element_type=F32)

        gates = jnp.dot(h_s[pl.ds(r0, RB), :], win_ref[:, GATE_COL:IN_W], preferred_element_type=F32)
        merged = _sigmoid(gates[:, 0:D_MODEL]) * a + _sigmoid(gates[:, D_MODEL:2 * D_MODEL]) * b
        upd = jnp.dot(merged.astype(BF16), wo_ref[...], preferred_element_type=F32)
        xm = x_ref[pl.ds(r0, RB), :] + gt1 * upd
        xmid_ref[rows, :] = xm
        xm_s[slot, rows, :] = xm

    def moe_prep(u):
        rows = slice(u * RB, (u + 1) * RB)
        h2 = _rms(xm_s[1 - slot, rows, :]) * mod2_s[1 - slot, 0:1, :] + mod2_s[1 - slot, 1:2, :]
        hi = h2.astype(BF16)
        lo = (h2 - hi.astype(F32)).astype(BF16)
        l1 = jnp.dot(hi, wr_ref[...], preferred_element_type=F32)
        l2 = jnp.dot(lo, wr_ref[:, 0:LANES], preferred_element_type=F32)
        gate, bucket = _route(l1[:, 0:LANES] + l1[:, LANES:2 * LANES] + l2)
        groups = pl.ds(u * (RB // SUBLANES), RB // SUBLANES)
        _store_tiles(h2_ref.at[groups], _pack_bf16_pairs(h2))
        gate_ref[rows, :] = gate
        lane = lax.broadcasted_iota(I32, (RB, LANES), 1).astype(F32)
        oh_ref[rows, :] = jnp.where(lane == bucket, 1.0, 0.0).astype(BF16)

    mod2_s[slot, 0:1, :] = gain2
    mod2_s[slot, 1:2, :] = sh2

    @pl.when(step < n_steps)
    def _():
        for u in range(U):
            moe_prep(u)
        for u in range(U):
            mix(u)

    @pl.when(step == n_steps)
    def _():
        for u in range(U):
            moe_prep(u)


def _mix(x2d, mod, mod_row, cache, rope_tabs, weights, *, S, L, emit_kv, blocks_per_step, cast=()):
    T = x2d.shape[0]
    TM = S * L
    P = cache[0].shape[0] // (T // L * N_KV_HEADS) if cache is not None else 0
    use_rope = rope_tabs is not None
    assert T % TM == 0 and L % ROW_BLOCK == 0
    assert not (use_rope or P) or S == 1
    Lk = P + L

    args = [x2d, mod]
    nrb = TM // ROW_BLOCK
    n_blocks = T // ROW_BLOCK
    step_rows = blocks_per_step * ROW_BLOCK
    steps_per_group = nrb // blocks_per_step
    n_mix_steps = n_blocks // blocks_per_step
    assert nrb % blocks_per_step == 0

    def mixed(s):
        return jnp.minimum(s, n_mix_steps - 1)

    def group(s):
        return mixed(s) // steps_per_group

    def prepared(s):
        return jnp.maximum(s - 1, 0)

    in_specs = [
        pl.BlockSpec((TM, D_MODEL), lambda s: (group(s), 0)),
        _resident(mod.shape),
    ]
    if P:
        args += list(cache)
        in_specs += [pl.BlockSpec((P * N_KV_HEADS, HEAD_DIM), lambda s: (group(s), 0))] * 2
    if use_rope:
        args += list(rope_tabs)
        in_specs += [_resident((L, HEAD_DIM))] * 3
    args += list(weights)
    in_specs += [_resident(w.shape) for w in weights]
    n_steps = T // TM
    def per_group(shape):
        assert shape[0] % n_steps == 0
        blk = (shape[0] // n_steps,) + shape[1:]
        return pl.BlockSpec(blk, lambda s, n=len(blk): (group(s),) + (0,) * (n - 1))

    cast_out_shapes = [ws[0].shape[:-1] + (sum(w.shape[-1] for w in ws),) for ws in cast]
    for ws in cast:
        args += list(ws)
        in_specs += [per_group(w.shape) for w in ws]

    out_shape = [jax.ShapeDtypeStruct((T, D_MODEL), F32), jax.ShapeDtypeStruct(_tiles_shape(T, PACKED_CHUNKS), U32),
                 jax.ShapeDtypeStruct((T, LANES), F32),
                 jax.ShapeDtypeStruct((T, LANES), BF16)]
    out_specs = [pl.BlockSpec((step_rows, D_MODEL), lambda s: (mixed(s), 0)),
                 _tiles_spec(step_rows, prepared, PACKED_CHUNKS),
                 pl.BlockSpec((step_rows, LANES), lambda s: (prepared(s), 0)),
                 pl.BlockSpec((step_rows, LANES), lambda s: (prepared(s), 0))]
    if emit_kv:
        out_shape += [jax.ShapeDtypeStruct((T * N_KV_HEADS, HEAD_DIM), F32)] * 2
        out_specs += [pl.BlockSpec((TM * N_KV_HEADS, HEAD_DIM), lambda s: (group(s), 0))] * 2
    out_shape += [jax.ShapeDtypeStruct(shp, BF16) for shp in cast_out_shapes]
    out_specs += [per_group(shp) for shp in cast_out_shapes]

    scratch = [
        pltpu.VMEM((N_HEADS, TM, HEAD_DIM), BF16),
        pltpu.VMEM((S, Lk, KV_W), BF16),
        pltpu.VMEM((S, Lk, KV_W), BF16),
        pltpu.VMEM((S, L + 2 * POOL_HALO, POOL_W), F32),
        pltpu.VMEM((TM, D_MODEL), BF16),
        pltpu.VMEM((blocks_per_step, ROW_BLOCK, ATTN_W), BF16),
        pltpu.VMEM((2, step_rows, D_MODEL), F32),
        pltpu.VMEM((2, 2, D_MODEL), F32),
    ]
    kern = functools.partial(_mix_kernel, S=S, L=L, P=P, use_rope=use_rope, emit_kv=emit_kv,
                             n_cast=tuple(len(ws) for ws in cast), n_blocks=n_blocks, U=blocks_per_step,
                             mod_row=mod_row)
    return pl.pallas_call(
        kern,
        grid=(n_mix_steps + 1,),
        in_specs=in_specs,
        out_specs=out_specs,
        out_shape=out_shape,
        scratch_shapes=scratch,
        compiler_params=pltpu.CompilerParams(
            dimension_semantics=("arbitrary",), vmem_limit_bytes=V7X_VMEM_LIMIT_BYTES),
        name="mixer_rope" if use_rope else "mixer_ctx",
    )(*args)


def _plan_kernel(oh_ref, dest_ref, meta_ref, *, n_blocks, tile):
    TB = TOKEN_BLOCK
    lane = lax.broadcasted_iota(I32, (SUBLANES, LANES), 1)

    def count(b, acc):
        oh = oh_ref[pl.ds(pl.multiple_of(b * TB, TB), TB), :].astype(F32)
        return acc + jnp.sum(oh, axis=0, keepdims=True)

    counts = lax.fori_loop(0, n_blocks, count, jnp.zeros((SUBLANES, LANES), F32))
    padded = jnp.floor((counts + (tile - 0.5)) * (1.0 / tile)) * tile
    ends = padded
    step = 1
    while step < LANES:
        ends = ends + jnp.where(lane >= step, pltpu.roll(ends, step, 1), 0.0)
        step *= 2
    starts = ends - padded

    tri = jnp.where(lax.broadcasted_iota(I32, (TB, TB), 1) < lax.broadcasted_iota(I32, (TB, TB), 0),
                    1.0, 0.0).astype(BF16)
    ones = jnp.ones((2 * SUBLANES, LANES), BF16)
    nt = (((1,), (1,)), ((), ()))

    def place(b, seen):
        oh = oh_ref[pl.ds(pl.multiple_of(b * TB, TB), TB), :]
        ohf = oh.astype(F32)
        rank = jnp.dot(tri, oh, preferred_element_type=F32)
        base = (starts + seen)[0:1, :]
        v = ohf * (rank + base)
        hi = jnp.floor(v * (1.0 / LANES))
        lo = v - hi * LANES
        d = (lax.dot_general(ones, hi.astype(BF16), nt, preferred_element_type=F32) * LANES
             + lax.dot_general(ones, lo.astype(BF16), nt, preferred_element_type=F32))
        dest_ref[b] = d[0:1, :].astype(I32)
        return seen + jnp.sum(ohf, axis=0, keepdims=True)

    lax.fori_loop(0, n_blocks, place, jnp.zeros((SUBLANES, LANES), F32))

    tile_row0 = lax.broadcasted_iota(I32, (LANES, LANES), 0).astype(F32) * tile
    is_bucket = lax.broadcasted_iota(I32, (LANES, LANES), 1) < N_BUCKETS
    done = jnp.sum(jnp.where(is_bucket, jnp.where(ends[0:1, :] <= tile_row0, 1.0, 0.0), 0.0),
                   axis=1, keepdims=True)
    bkt = jnp.minimum(done, N_BUCKETS - 1.0)
    grp = (jnp.where(bkt >= PAIRS_PER_GROUP, 1.0, 0.0) + jnp.where(bkt >= 2 * PAIRS_PER_GROUP, 1.0, 0.0)
           + jnp.where(bkt >= 3 * PAIRS_PER_GROUP, 1.0, 0.0))
    pair = bkt - PAIRS_PER_GROUP * grp
    a = jnp.where(pair >= 3.0, 1.0, 0.0) + jnp.where(pair >= 5.0, 1.0, 0.0)
    b = pair - a * (7.0 - a) * 0.5 + a + 1.0
    e1 = EXP_PER_GROUP * grp + a
    e2 = EXP_PER_GROUP * grp + b
    meta = jnp.concatenate(
        [_row(e1), _row(e2), jnp.floor(ends[0:1, :] * (1.0 / tile) + 0.5),
         jnp.zeros((SUBLANES - 3, LANES), F32)], axis=0)
    meta_ref[...] = meta.astype(I32)


def _plan(onehot, tile):
    T = onehot.shape[0]
    n_blocks = T // TOKEN_BLOCK
    dest, meta = pl.pallas_call(
        functools.partial(_plan_kernel, n_blocks=n_blocks, tile=tile),
        out_shape=[jax.ShapeDtypeStruct((n_blocks, 1, TOKEN_BLOCK), I32),
                   jax.ShapeDtypeStruct((SUBLANES, LANES), I32)],
        name="moe_plan",
    )(onehot)
    return dest.reshape(T), meta


def _sc_move_rows(src_v, table_hbm, out_hbm, lo, n_rows, idx_v, pieces_v, sem):
    chunks = pieces_v.shape[0] // SC_ROWS_PER_STEP
    lane = lax.iota(I32, SC_LANES)
    row_in_group = lane & (SUBLANES - 1)
    chunk_in_pair = lane >> 3
    rows_per_gather = SC_PIECES_PER_GATHER // chunks

    @pl.loop(0, n_rows // SC_ROWS_PER_STEP)
    def _(step):
        copies = []
        for g in range(SC_ROWS_PER_STEP // rows_per_gather):
            r0 = step * SC_ROWS_PER_STEP + g * rows_per_gather
            for v in range(SC_PIECES_PER_GATHER // SC_LANES):
                group, chunk0 = v // (chunks // 2), 2 * (v % (chunks // 2))
                tok = plsc.load_gather(src_v, [r0 + group * SUBLANES + row_in_group])
                piece = (tok >> 3) * (SUBLANES * chunks) + (chunk0 + chunk_in_pair) * SUBLANES + (tok & 7)
                idx_v[pl.ds(g * SC_PIECES_PER_GATHER + v * SC_LANES, SC_LANES)] = piece
            window = pl.ds(g * SC_PIECES_PER_GATHER, SC_PIECES_PER_GATHER)
            copies.append(pltpu.async_copy(table_hbm.at[idx_v.at[window]], pieces_v.at[window], sem))
        for cp in copies:
            cp.wait()
        first = pl.multiple_of((lo + step * SC_ROWS_PER_STEP) * chunks, SC_ROWS_PER_STEP * chunks)
        pltpu.sync_copy(pieces_v, out_hbm.at[pl.ds(first, SC_ROWS_PER_STEP * chunks)])


def _sc_scratch(chunks, dtype):
    return [pltpu.VMEM((SC_ROWS_PER_STEP * chunks,), I32), pltpu.VMEM((SC_ROWS_PER_STEP * chunks, LANES), dtype)]


def _sc_dispatch(h2_flat, gate_rows, dest, n_rows):
    T = dest.shape[0]
    per_worker = n_rows // SC_WORKERS
    rows_per_step = SC_ROWS_PER_STEP
    chunks = h2_flat.shape[0] // T
    assert n_rows % SC_WORKERS == 0 and per_worker % rows_per_step == 0 and T % SC_LANES == 0
    mesh = plsc.VectorSubcoreMesh(core_axis_name="c", subcore_axis_name="s")

    @functools.partial(
        pl.kernel, mesh=mesh,
        out_type=[jax.ShapeDtypeStruct((n_rows * chunks, LANES), h2_flat.dtype),
                  jax.ShapeDtypeStruct((n_rows, LANES), F32)],
        scratch_types=[pltpu.VMEM((T,), I32), pltpu.VMEM((per_worker,), I32)]
        + _sc_scratch(chunks, h2_flat.dtype)
        + [pltpu.VMEM((rows_per_step, LANES), F32), pltpu.SemaphoreType.DMA, pltpu.SemaphoreType.DMA],
        compiler_params=pltpu.CompilerParams(use_tc_tiling_on_sc=True, needs_layout_passes=False),
        name="sc_dispatch",
    )
    def dispatch(h2_hbm, gate_hbm, dest_hbm, out_h_hbm, out_g_hbm,
                 dest_v, src_v, idx_v, pieces_v, gates_v, sem_h, sem_g):
        worker = lax.axis_index("s") * SC_CORES + lax.axis_index("c")
        lo = worker * per_worker
        pltpu.sync_copy(dest_hbm, dest_v)

        @pl.loop(0, per_worker // SC_LANES)
        def _(j):
            j0 = pl.multiple_of(j * SC_LANES, SC_LANES)
            src_v[pl.ds(j0, SC_LANES)] = lax.rem(lo + j0 + lax.iota(I32, SC_LANES), T)

        @pl.loop(0, T // SC_LANES)
        def _(j):
            t0 = pl.multiple_of(j * SC_LANES, SC_LANES)
            d = dest_v[pl.ds(t0, SC_LANES)] - lo
            mine = (d >= 0) & (d < per_worker)
            plsc.store_scatter(src_v, [jnp.where(mine, d, 0)], t0 + lax.iota(I32, SC_LANES), mask=mine)

        @pl.loop(0, per_worker // rows_per_step)
        def _(j):
            off = pl.multiple_of(j * rows_per_step, rows_per_step)
            pltpu.async_copy(gate_hbm.at[src_v.at[pl.ds(off, rows_per_step)]], gates_v, sem_g).wait()
            pltpu.sync_copy(gates_v, out_g_hbm.at[pl.ds(lo + off, rows_per_step)])

        _sc_move_rows(src_v, h2_hbm, out_h_hbm, lo, per_worker, idx_v, pieces_v, sem_h)

    return dispatch(h2_flat, gate_rows, dest)


def _expert_kernel(meta, x_ref, gv_ref, wgu_hbm, wd_hbm, o_ref, wgu_ref, wd_ref, ready_s, sems, *, tile, per_step):
    groups = tile // SUBLANES
    n_used = meta[2, LANES - 1]
    step = pl.program_id(0)

    def weight_copies(g):
        experts = pl.ds(g * EXP_PER_GROUP, EXP_PER_GROUP)
        return (pltpu.make_async_copy(wgu_hbm.at[experts], wgu_ref.at[experts], sems.at[g]),
                pltpu.make_async_copy(wd_hbm.at[experts], wd_ref.at[experts], sems.at[g]))

    def land_through(last_group):
        landed = ready_s[0]
        for g in range(N_EXP_GROUPS):
            @pl.when((g >= landed) & (g <= last_group))
            def _():
                for cp in weight_copies(g):
                    cp.wait()
                if g + 1 < N_EXP_GROUPS:
                    for cp in weight_copies(g + 1):
                        cp.start()
        ready_s[0] = jnp.maximum(landed, last_group + 1)

    @pl.when(step == 0)
    def _():
        ready_s[0] = 0
        for cp in weight_copies(0):
            cp.start()

    def one_tile(k, carry):
        t = step * per_step + k
        rows = pl.ds(pl.multiple_of(k * groups, groups), groups)

        @pl.when(t < n_used)
        def _():
            land_through(meta[0, t] // EXP_PER_GROUP)
            x = _unpack_bf16_pairs(_load_tiles(x_ref.at[rows]))
            gv = gv_ref[pl.ds(pl.multiple_of(k * tile, SUBLANES), tile), :]
            lane = lax.broadcasted_iota(I32, gv.shape, 1)
            out = None
            for e in (meta[0, t], meta[1, t]):
                ge = jnp.sum(jnp.where(lane == EXPERT_LANE0 + e, gv, 0.0), axis=-1, keepdims=True)
                h = jnp.dot(x, wgu_ref[e], preferred_element_type=F32)
                hg = h[:, 0:D_EXPERT]
                hid = (hg * _sigmoid(hg) * h[:, D_EXPERT:2 * D_EXPERT] * ge).astype(BF16)
                y = jnp.dot(hid, wd_ref[e], preferred_element_type=F32)
                out = y if out is None else out + y
            _store_tiles(o_ref.at[rows], _pack_bf16_pairs(out))

        return carry

    lax.fori_loop(0, per_step, one_tile, 0)

    @pl.when(step == pl.num_programs(0) - 1)
    def _():
        land_through(N_EXP_GROUPS - 1)


def _experts(sorted_h2, sorted_gates, meta, wgu, wd, tiling):
    tile, per_step, n_tiles = tiling
    step_rows = tile * per_step
    assert n_tiles * tile == sorted_h2.shape[0] * SUBLANES and n_tiles % per_step == 0

    def last_used(i, meta):
        return jnp.minimum(i, (meta[2, LANES - 1] - 1) // per_step)

    return pl.pallas_call(
        functools.partial(_expert_kernel, tile=tile, per_step=per_step),
        grid_spec=pltpu.PrefetchScalarGridSpec(
            num_scalar_prefetch=1,
            grid=(n_tiles // per_step,),
            in_specs=[
                _tiles_spec(step_rows, last_used, PACKED_CHUNKS),
                pl.BlockSpec((step_rows, LANES), lambda *a: (last_used(*a), 0)),
                pl.BlockSpec(memory_space=pl.ANY), pl.BlockSpec(memory_space=pl.ANY),
            ],
            out_specs=_tiles_spec(step_rows, last_used, PACKED_CHUNKS),
            scratch_shapes=[pltpu.VMEM(wgu.shape, wgu.dtype), pltpu.VMEM(wd.shape, wd.dtype),
                            pltpu.SMEM((1,), I32), pltpu.SemaphoreType.DMA((N_EXP_GROUPS,))],
        ),
        out_shape=jax.ShapeDtypeStruct(_tiles_shape(n_tiles * tile, PACKED_CHUNKS), U32),
        compiler_params=pltpu.CompilerParams(
            dimension_semantics=("arbitrary",), vmem_limit_bytes=V7X_VMEM_LIMIT_BYTES),
        name="moe_experts",
    )(meta, sorted_h2, sorted_gates, wgu, wd)


def _sc_row_gather(table_flat, idx, chunks):
    n = idx.shape[0]
    per_worker = n // SC_WORKERS
    assert n % SC_WORKERS == 0 and per_worker % SC_ROWS_PER_STEP == 0
    mesh = plsc.VectorSubcoreMesh(core_axis_name="c", subcore_axis_name="s")

    @functools.partial(
        pl.kernel, mesh=mesh,
        out_type=jax.ShapeDtypeStruct((n * chunks, LANES), table_flat.dtype),
        scratch_types=[pltpu.VMEM((per_worker,), I32)] + _sc_scratch(chunks, table_flat.dtype)
        + [pltpu.SemaphoreType.DMA],
        compiler_params=pltpu.CompilerParams(use_tc_tiling_on_sc=True, needs_layout_passes=False),
        name="sc_row_gather",
    )
    def gather(table_hbm, idx_hbm, out_hbm, src_v, idx_v, pieces_v, sem):
        worker = lax.axis_index("s") * SC_CORES + lax.axis_index("c")
        lo = worker * per_worker
        pltpu.sync_copy(idx_hbm.at[pl.ds(lo, per_worker)], src_v)
        _sc_move_rows(src_v, table_hbm, out_hbm, lo, per_worker, idx_v, pieces_v, sem)

    return gather(table_flat, idx)


def _final_kernel(x_ref, moe_ref, gt2_ref, gf_ref, o_ref, *, mod_row):
    gt2 = gt2_ref[pl.ds(mod_row(pl.program_id(0)), 1), :]
    y = x_ref[...] + gt2 * _unpack_bf16_pairs(_load_tiles(moe_ref)).astype(F32)
    o_ref[...] = _rms(y) * gf_ref[...]


def _final(xmid, moe_rows, mod, mod_row, gf):
    T = xmid.shape[0]
    return pl.pallas_call(
        functools.partial(_final_kernel, mod_row=mod_row),
        grid=(T // FINAL_BLOCK,),
        in_specs=[
            pl.BlockSpec((FINAL_BLOCK, D_MODEL), lambda i: (i, 0)),
            _tiles_spec(FINAL_BLOCK, lambda i: i, PACKED_CHUNKS),
            pl.BlockSpec((COND_ROWS, D_MODEL), lambda i: (0, 5)),
            pl.BlockSpec((1, D_MODEL), lambda i: (0, 0)),
        ],
        out_specs=pl.BlockSpec((FINAL_BLOCK, D_MODEL), lambda i: (i, 0)),
        out_shape=jax.ShapeDtypeStruct((T, D_MODEL), F32),
        compiler_params=pltpu.CompilerParams(
            dimension_semantics=("arbitrary",), vmem_limit_bytes=V7X_VMEM_LIMIT_BYTES),
        name="moe_final",
    )(xmid, moe_rows, mod, gf)


def _flat(tiles):
    return tiles.reshape(-1, LANES)


def _expert_tiling(T):
    tile = -(-(T * 9) // (8 * N_BUCKETS * 64)) * 64
    per_step = max(1, EXPERT_STEP_ROWS // tile)
    n_tiles = (T + N_BUCKETS * (tile - 1)) // tile
    while n_tiles % per_step or (n_tiles * tile) % (SC_WORKERS * SC_ROWS_PER_STEP):
        n_tiles += 1
    return tile, per_step, n_tiles


def _moe_dispatch(h2_tiles, gate_rows, onehot, tiling):
    T = gate_rows.shape[0]
    tile, _, n_tiles = tiling
    n_rows = n_tiles * tile
    assert n_tiles <= LANES and T % TOKEN_BLOCK == 0 and n_rows <= 2 ** 15
    dest, meta = _plan(onehot, tile)
    sorted_h2, sorted_gates = _sc_dispatch(_flat(h2_tiles), gate_rows, dest, n_rows)
    return sorted_h2.reshape(_tiles_shape(n_rows, PACKED_CHUNKS)), sorted_gates, dest, meta


def _moe_unpermute(moe_sorted_tiles, dest):
    chunks = moe_sorted_tiles.shape[1]
    return _sc_row_gather(_flat(moe_sorted_tiles), dest, chunks).reshape(_tiles_shape(dest.shape[0], chunks))


def _rope_tables(n_tokens):
    t = np.arange(n_tokens)
    row = (t // GRID_W).astype(np.float32)
    col = (t % GRID_W).astype(np.float32)
    freq = np.float32(ROPE_THETA) ** (-np.arange(ROPE_NF, dtype=np.float32) / np.float32(ROPE_NF))
    ang = np.concatenate([row[:, None] * freq] * 2 + [col[:, None] * freq] * 2, axis=-1)
    first = (np.arange(HEAD_DIM) % (2 * ROPE_NF)) < ROPE_NF
    sin = np.sin(ang)
    zero = np.float32(0.0)
    return (jnp.asarray(np.cos(ang)), jnp.asarray(np.where(first, -sin, zero)),
            jnp.asarray(np.where(first, zero, sin)))


def kernel(x_prompt, x_sample, cache_k, cache_v, c, c_ctx, norm1_g, norm2_g, w_ada, b_ada, w_in, q_norm_g, k_norm_g, w_pool, pool_scale, w_branch_a, w_branch_b, w_out, w_router_group, w_router_expert, w_exp_gate, w_exp_up, w_exp_down, final_norm_g):
    assert norm1_g.shape[0] == 1, "single-layer trunk"
    B, L_ctx, _ = x_prompt.shape
    Bs, L_lat, _ = x_sample.shape
    P = cache_k.shape[2]
    assert 1 + Bs <= COND_ROWS

    cond = jnp.concatenate([c_ctx[None, :], c, jnp.zeros((COND_ROWS - 1 - Bs, D_MODEL), F32)], axis=0)
    mod, w_in_b, wa_b, wb_b, wo_b, wpool_b = _ada(
        cond, w_ada[0], b_ada[0][None, :], w_pool[0],
        cast=(w_in[0], w_branch_a[0], w_branch_b[0], w_out[0]))

    wr = jnp.concatenate([w_router_group[0], w_router_expert[0],
                          jnp.zeros((D_MODEL, LANES - N_EXP_GROUPS - N_EXPERTS), F32)], axis=1)
    wr_hi = wr.astype(BF16)
    wr_lo = (wr - wr_hi.astype(F32)).astype(BF16)
    mix_w = (norm1_g[0][None, :], w_in_b, q_norm_g[0][None, :], k_norm_g[0][None, :],
             wpool_b, pool_scale[0][None, :], wa_b, wb_b, wo_b,
             norm2_g[0][None, :], jnp.concatenate([wr_hi, wr_lo], axis=1))
    gf = final_norm_g[None, :]

    xp2 = x_prompt.reshape(B * L_ctx, D_MODEL)
    xmid_p, h2_p, gate_p, oh_p, knew, vnew, wgu, wd = _mix(
        xp2, mod, lambda i: 0, None, None, mix_w, S=2, L=L_ctx, emit_kv=True, blocks_per_step=2,
        cast=((w_exp_gate[0], w_exp_up[0]), (w_exp_down[0],)))
    tiling_p = _expert_tiling(B * L_ctx)
    sh_p, sg_p, dest_p, meta_p = _moe_dispatch(h2_p, gate_p, oh_p, tiling_p)

    xs2 = x_sample.reshape(Bs * L_lat, D_MODEL)
    cache = (cache_k.reshape(Bs * P * N_KV_HEADS, HEAD_DIM), cache_v.reshape(Bs * P * N_KV_HEADS, HEAD_DIM))
    xmid_s, h2_s, gate_s, oh_s = _mix(xs2, mod, lambda i: 1 + i, cache, _rope_tables(L_lat), mix_w,
                                      S=1, L=L_lat, emit_kv=False, blocks_per_step=1)
    tiling_s = _expert_tiling(Bs * L_lat)
    sh_s, sg_s, dest_s, meta_s = _moe_dispatch(h2_s, gate_s, oh_s, tiling_s)

    moe_p = _moe_unpermute(_experts(sh_p, sg_p, meta_p, wgu, wd, tiling_p), dest_p)
    moe_s = _moe_unpermute(_experts(sh_s, sg_s, meta_s, wgu, wd, tiling_s), dest_s)
    y_prompt = _final(xmid_p, moe_p, mod, lambda i: 0, gf)
    blocks_per_seq = L_lat // FINAL_BLOCK
    y_sample = _final(xmid_s, moe_s, mod, lambda i: 1 + i // blocks_per_seq, gf)

    return (y_prompt.reshape(B, L_ctx, D_MODEL), y_sample.reshape(Bs, L_lat, D_MODEL),
            knew.reshape(B, 1, L_ctx, N_KV_HEADS, HEAD_DIM), vnew.reshape(B, 1, L_ctx, N_KV_HEADS, HEAD_DIM))
```

```python
import functools

import numpy as np
import jax
import jax.numpy as jnp
from jax import lax
from jax.experimental import pallas as pl
from jax.experimental.pallas import tpu as pltpu
from jax.experimental.pallas import tpu_sc as plsc

F32 = jnp.float32
BF16 = jnp.bfloat16
I32 = jnp.int32
U32 = jnp.uint32

D_MODEL = 1024
HEAD_DIM = 128
N_HEADS = 8
N_KV_HEADS = 2
GROUP = N_HEADS // N_KV_HEADS
ATTN_W = N_HEADS * HEAD_DIM
KV_W = N_KV_HEADS * HEAD_DIM
POOL_WINDOWS = (2, 4, 8, 16)
POOL_GC = 128
POOL_W = POOL_GC * len(POOL_WINDOWS)
IN_W = ATTN_W + 2 * KV_W + POOL_W + 2 * D_MODEL
GATE_COL = ATTN_W + 2 * KV_W + POOL_W
GRID_W = 64
ROPE_THETA = 10000.0
ROPE_NF = HEAD_DIM // 4
N_EXP_GROUPS = 4
EXP_PER_GROUP = 4
N_EXPERTS = 16
D_EXPERT = 256
EPS = 1e-6
LOG2_E = 1.4426950408889634

LANES = 128
SUBLANES = 8
COND_ROWS = SUBLANES
POOL_HALO = 8
ROW_BLOCK = 256
ADA_COLS = 768
EXPERT_LANE0 = N_EXP_GROUPS
PAIRS_PER_GROUP = EXP_PER_GROUP * (EXP_PER_GROUP - 1) // 2
N_BUCKETS = N_EXP_GROUPS * PAIRS_PER_GROUP
EXPERT_STEP_ROWS = 1536
TOKEN_BLOCK = 1024
FINAL_BLOCK = 1024
ROW_CHUNKS = D_MODEL // LANES
SC_CORES = 2
SC_SUBCORES = 16
SC_WORKERS = SC_CORES * SC_SUBCORES
SC_LANES = 16
SC_PIECES_PER_GATHER = 128
SC_ROWS_PER_STEP = 64
PACKED_CHUNKS = ROW_CHUNKS // 2
V7X_VMEM_LIMIT_BYTES = 56 * 1024 * 1024


def _sigmoid(x):
    return 1.0 / (1.0 + jnp.exp(-x))


def _rms(x):
    return x * lax.rsqrt(jnp.mean(x * x, axis=-1, keepdims=True) + EPS)


def _resident(shape):
    zeros = (0,) * len(shape)
    return pl.BlockSpec(shape, lambda i, *_: zeros, pipeline_mode=pl.Buffered(1))


def _tiles_shape(n, chunks=ROW_CHUNKS):
    return (n // SUBLANES, chunks, SUBLANES, LANES)


def _tiles_spec(n, block_index, chunks=ROW_CHUNKS):
    return pl.BlockSpec(_tiles_shape(n, chunks), lambda *a: (block_index(*a), 0, 0, 0))


def _store_tiles(ref, x):
    for c in range(ref.shape[1]):
        ref[:, c, :, :] = x[:, c * LANES:(c + 1) * LANES].reshape(x.shape[0] // SUBLANES, SUBLANES, LANES)


def _load_tiles(ref):
    n = ref.shape[0] * SUBLANES
    return jnp.concatenate([ref[:, c, :, :].reshape(n, LANES) for c in range(ref.shape[1])], axis=1)


def _pack_bf16_pairs(x):
    bits = pltpu.bitcast(x.astype(BF16).astype(F32), U32)
    w = x.shape[1] // 2
    return bits[:, :w] | (bits[:, w:] >> 16)


def _unpack_bf16_pairs(words):
    hi = pltpu.bitcast(words & jnp.uint32(0xFFFF0000), F32).astype(BF16)
    lo = pltpu.bitcast(words << 16, F32).astype(BF16)
    return jnp.concatenate([hi, lo], axis=1)


def _row(x):
    return jnp.transpose(jnp.broadcast_to(x, (x.shape[0], LANES)))[0:1, :]


def _ada_kernel(c_ref, w_ref, b_ref, *refs, steps_per_pool_group):
    n_cast = len(refs) // 2 - 1
    c = c_ref[...]
    s = (c * _sigmoid(c)).astype(BF16)
    refs[n_cast + 1][...] = jnp.dot(s, w_ref[...].astype(BF16), preferred_element_type=F32) + b_ref[...]
    for src, dst in zip(refs[:n_cast], refs[n_cast + 2:]):
        dst[...] = src[...].astype(BF16)
    pool_src, pool_dst = refs[n_cast], refs[-1]
    wide = jnp.concatenate([pool_src[0]] * len(POOL_WINDOWS), axis=1)
    lane_group = lax.broadcasted_iota(I32, wide.shape, 1) // POOL_GC
    pool_dst[...] = jnp.where(lane_group == pl.program_id(0) // steps_per_pool_group, wide, 0.0).astype(BF16)


def _ada(cond, w_ada, b_ada, w_pool, cast=()):
    n = w_ada.shape[1]
    n_steps = n // ADA_COLS
    cast_specs = []
    for w in cast:
        assert w.ndim == 2 and w.shape[0] % (n_steps * 2 * SUBLANES) == 0
        cast_specs.append(pl.BlockSpec((w.shape[0] // n_steps, w.shape[1]), lambda j: (j, 0)))
    pool_rows = POOL_W // n_steps
    spg = POOL_GC // pool_rows
    assert w_pool.shape == (len(POOL_WINDOWS), POOL_GC, POOL_GC) and POOL_GC % pool_rows == 0
    assert pool_rows % (2 * SUBLANES) == 0
    pool_in = pl.BlockSpec((1, pool_rows, POOL_GC), lambda j: (j // spg, j % spg, 0))
    pool_out = pl.BlockSpec((pool_rows, POOL_W), lambda j: (j, 0))
    return pl.pallas_call(
        functools.partial(_ada_kernel, steps_per_pool_group=spg),
        grid=(n_steps,),
        in_specs=[
            pl.BlockSpec((COND_ROWS, D_MODEL), lambda j: (0, 0)),
            pl.BlockSpec((D_MODEL, ADA_COLS), lambda j: (0, j)),
            pl.BlockSpec((1, ADA_COLS), lambda j: (0, j)),
        ] + cast_specs + [pool_in],
        out_specs=[pl.BlockSpec((COND_ROWS, ADA_COLS), lambda j: (0, j))] + cast_specs + [pool_out],
        out_shape=[jax.ShapeDtypeStruct((COND_ROWS, n), F32)] + [jax.ShapeDtypeStruct(w.shape, BF16) for w in cast]
        + [jax.ShapeDtypeStruct((POOL_W, POOL_W), BF16)],
        name="ada_mod",
    )(cond, w_ada, b_ada, *cast, w_pool)


def _route(logits):
    lane = lax.broadcasted_iota(I32, logits.shape, 1).astype(F32)
    neg = jnp.float32(-1e30)
    far = jnp.float32(LANES)
    is_g = lane < N_EXP_GROUPS
    gl = jnp.where(is_g, logits, neg)
    gmax = jnp.max(gl, axis=-1, keepdims=True)
    gsel = jnp.min(jnp.where(gl == gmax, lane, far), axis=-1, keepdims=True)
    psel = 1.0 / jnp.sum(jnp.where(is_g, jnp.exp(gl - gmax), 0.0), axis=-1, keepdims=True)
    e_lo = EXPERT_LANE0 + EXP_PER_GROUP * gsel
    el = jnp.where(lane >= e_lo, jnp.where(lane < e_lo + EXP_PER_GROUP, logits, neg), neg)
    v1 = jnp.max(el, axis=-1, keepdims=True)
    i1 = jnp.min(jnp.where(el == v1, lane, far), axis=-1, keepdims=True)
    el2 = jnp.where(lane == i1, neg, el)
    v2 = jnp.max(el2, axis=-1, keepdims=True)
    i2 = jnp.min(jnp.where(el2 == v2, jnp.where(lane == i1, far, lane), far), axis=-1, keepdims=True)
    e2 = jnp.exp(v2 - v1)
    w1 = psel / (1.0 + e2)
    w2 = psel * e2 / (1.0 + e2)
    gate = jnp.where(lane == i1, w1, jnp.where(lane == i2, w2, 0.0))
    a = jnp.minimum(i1, i2) - e_lo
    b = jnp.maximum(i1, i2) - e_lo
    pair = a * (7.0 - a) * 0.5 + (b - a - 1.0)
    return gate, gsel * PAIRS_PER_GROUP + pair


def _mix_kernel(*refs, S, L, P, use_rope, emit_kv, n_cast, n_blocks, U, mod_row, tile):
    it = iter(refs)
    x_ref = next(it)
    mod_ref = next(it)
    if P:
        ck_ref = next(it)
        cv_ref = next(it)
    if use_rope:
        cos_ref = next(it)
        sneg_ref = next(it)
        spos_ref = next(it)
    (g1_ref, win_ref, qg_ref, kg_ref, wpool_ref, pscale_ref, wa_ref, wb_ref, wo_ref,
     g2_ref, wr_ref) = (next(it) for _ in range(11))
    cast_in = [[next(it) for _ in range(n)] for n in n_cast]
    xmid_ref = next(it)
    h2_ref = next(it)
    gate_ref = next(it)
    dest_ref = next(it)
    meta_ref = next(it)
    if emit_kv:
        knew_ref = next(it)
        vnew_ref = next(it)
    cast_out = [next(it) for _ in n_cast]
    q_s, k_s, v_s, xp_s, h_s, attn_s, xm_s, mod2_s, oh_s = (next(it) for _ in range(9))

    TM = S * L
    RB = ROW_BLOCK
    nrb = TM // RB
    n_steps = n_blocks // U
    score_gain = HEAD_DIM ** -0.5 * LOG2_E
    step = pl.program_id(0)
    block0 = U * jnp.minimum(step, n_steps - 1)
    slot = step % 2

    mod_at = pl.ds(mod_row(jnp.minimum(step, n_steps - 1) // (nrb // U)), 1)
    sh1 = mod_ref[mod_at, 0:D_MODEL]
    gain1 = g1_ref[...] * (1.0 + mod_ref[mod_at, D_MODEL:2 * D_MODEL])
    gt1 = mod_ref[mod_at, 2 * D_MODEL:3 * D_MODEL]
    sh2 = mod_ref[mod_at, 3 * D_MODEL:4 * D_MODEL]
    gain2 = g2_ref[...] * (1.0 + mod_ref[mod_at, 4 * D_MODEL:5 * D_MODEL])
    qg = qg_ref[...] * score_gain
    kg = kg_ref[...]

    def project(r, carry):
        r0 = pl.multiple_of(r * RB, RB)
        s = r0 // L
        o = pl.multiple_of(r0 % L, RB)
        hb = (_rms(x_ref[pl.ds(r0, RB), :]) * gain1 + sh1).astype(BF16)
        h_s[pl.ds(r0, RB), :] = hb
        p1 = jnp.dot(hb, win_ref[:, 0:GATE_COL], preferred_element_type=F32)
        if use_rope:
            cs = cos_ref[pl.ds(o, RB), :]
            sn = sneg_ref[pl.ds(o, RB), :]
            sp = spos_ref[pl.ds(o, RB), :]

        def rope(t):
            return (t * cs + pltpu.roll(t, HEAD_DIM - ROPE_NF, 1) * sn + pltpu.roll(t, ROPE_NF, 1) * sp)

        for hd in range(N_HEADS):
            qh = _rms(p1[:, hd * HEAD_DIM:(hd + 1) * HEAD_DIM]) * qg
            if use_rope:
                qh = rope(qh)
            q_s[hd, pl.ds(r0, RB), :] = qh.astype(BF16)
        for kh in range(N_KV_HEADS):
            c0 = ATTN_W + kh * HEAD_DIM
            kk = _rms(p1[:, c0:c0 + HEAD_DIM]) * kg
            if emit_kv:
                knew_ref[pl.ds(N_KV_HEADS * r0 + kh, RB, stride=N_KV_HEADS), :] = kk
            if use_rope:
                kk = rope(kk)
            k_s[s, pl.ds(P + o, RB), kh * HEAD_DIM:(kh + 1) * HEAD_DIM] = kk.astype(BF16)
        vv = p1[:, ATTN_W + KV_W:ATTN_W + 2 * KV_W]
        if emit_kv:
            for kh in range(N_KV_HEADS):
                vnew_ref[pl.ds(N_KV_HEADS * r0 + kh, RB, stride=N_KV_HEADS), :] = (
                    vv[:, kh * HEAD_DIM:(kh + 1) * HEAD_DIM])
        v_s[s, pl.ds(P + o, RB), :] = vv.astype(BF16)
        xp_s[s, pl.ds(POOL_HALO + o, RB), :] = p1[:, ATTN_W + 2 * KV_W:GATE_COL]
        return carry

    @pl.when(step == 0)
    def _():
        xm_s[1] = jnp.zeros((U * RB, D_MODEL), F32)
        mod2_s[1] = jnp.zeros((2, D_MODEL), F32)

    @pl.when((step < n_steps) & (step % (nrb // U) == 0))
    def _():
        if P:
            for kh in range(N_KV_HEADS):
                cols = slice(kh * HEAD_DIM, (kh + 1) * HEAD_DIM)
                k_s[0, 0:P, cols] = ck_ref[pl.ds(kh, P, stride=N_KV_HEADS), :].astype(BF16)
                v_s[0, 0:P, cols] = cv_ref[pl.ds(kh, P, stride=N_KV_HEADS), :].astype(BF16)
        xp_s[:, 0:POOL_HALO, :] = jnp.zeros((S, POOL_HALO, POOL_W), F32)
        xp_s[:, L + POOL_HALO:L + 2 * POOL_HALO, :] = jnp.zeros((S, POOL_HALO, POOL_W), F32)
        lax.fori_loop(0, TM // RB, project, 0)
        for srcs, dst in zip(cast_in, cast_out):
            col = 0
            for src in srcs:
                dst[..., col:col + src.shape[-1]] = src[...].astype(BF16)
                col += src.shape[-1]

    def mix(u):
        r0 = pl.multiple_of(((block0 + u) % nrb) * RB, RB)
        s = r0 // L
        o = pl.multiple_of(r0 % L, RB)
        attn_u = attn_s.at[u]
        rows = slice(u * RB, (u + 1) * RB)

        for hd in range(N_HEADS):
            kh = hd // GROUP
            k = k_s[s, :, kh * HEAD_DIM:(kh + 1) * HEAD_DIM]
            v = v_s[s, :, kh * HEAD_DIM:(kh + 1) * HEAD_DIM]
            qh = q_s[hd, pl.ds(r0, RB), :]
            sc = lax.dot_general(qh, k, (((1,), (1,)), ((), ())), preferred_element_type=F32)
            e = jnp.exp2(sc - jnp.max(sc, axis=-1, keepdims=True))
            den = jnp.sum(e, axis=-1, keepdims=True)
            oh = jnp.dot(e.astype(BF16), v, preferred_element_type=F32) / den
            attn_u[:, hd * HEAD_DIM:(hd + 1) * HEAD_DIM] = oh.astype(BF16)
        a = jnp.dot(attn_u[...], wa_ref[...], preferred_element_type=F32)

        t = o + lax.broadcasted_iota(I32, (RB, 1), 0)
        RW = RB + 2 * POOL_HALO
        parts = []
        for gi, w in enumerate(POOL_WINDOWS):
            cols = slice(gi * POOL_GC, (gi + 1) * POOL_GC)
            xw = xp_s[s, pl.ds(o, RW), cols]
            run = xw
            span = 1
            while span < w:
                run = run + pltpu.roll(run, span, 0)
                span *= 2
            if w // 2 > 1:
                run = pltpu.roll(run, RW - (w // 2 - 1), 0)
            tot = run[POOL_HALO:POOL_HALO + RB]
            cnt = (jnp.minimum(t + w // 2, L) - jnp.maximum(t - w // 2, 0)).astype(F32)
            parts.append(tot / cnt - xw[POOL_HALO:POOL_HALO + RB])
        dpool = jnp.concatenate(parts, axis=1).astype(BF16)
        pooled = jnp.dot(dpool, wpool_ref[...], preferred_element_type=F32) * pscale_ref[...]
        b = jnp.dot(pooled.astype(BF16), wb_ref[...], preferred_element_type=F32)

        gates = jnp.dot(h_s[pl.ds(r0, RB), :], win_ref[:, GATE_COL:IN_W], preferred_element_type=F32)
        merged = _sigmoid(gates[:, 0:D_MODEL]) * a + _sigmoid(gates[:, D_MODEL:2 * D_MODEL]) * b
        upd = jnp.dot(merged.astype(BF16), wo_ref[...], preferred_element_type=F32)
        xm = x_ref[pl.ds(r0, RB), :] + gt1 * upd
        xmid_ref[rows, :] = xm
        xm_s[slot, rows, :] = xm

    def moe_prep(u):
        rows = slice(u * RB, (u + 1) * RB)
        h2 = _rms(xm_s[1 - slot, rows, :]) * mod2_s[1 - slot, 0:1, :] + mod2_s[1 - slot, 1:2, :]
        hi = h2.astype(BF16)
        lo = (h2 - hi.astype(F32)).astype(BF16)
        l1 = jnp.dot(hi, wr_ref[...], preferred_element_type=F32)
        l2 = jnp.dot(lo, wr_ref[:, 0:LANES], preferred_element_type=F32)
        gate, bucket = _route(l1[:, 0:LANES] + l1[:, LANES:2 * LANES] + l2)
        groups = pl.ds(u * (RB // SUBLANES), RB // SUBLANES)
        _store_tiles(h2_ref.at[groups], _pack_bf16_pairs(h2))
        gate_ref[rows, :] = gate
        lane = lax.broadcasted_iota(I32, (RB, LANES), 1).astype(F32)
        first = pl.multiple_of((U * jnp.maximum(step - 1, 0) + u) * RB, RB)
        oh_s[pl.ds(first, RB), :] = jnp.where(lane == bucket, 1.0, 0.0).astype(BF16)

    mod2_s[slot, 0:1, :] = gain2
    mod2_s[slot, 1:2, :] = sh2

    @pl.when(step < n_steps)
    def _():
        for u in range(U):
            moe_prep(u)
        for u in range(U):
            mix(u)

    @pl.when(step == n_steps)
    def _():
        for u in range(U):
            moe_prep(u)
        _plan_rows(oh_s, dest_ref, meta_ref, n_blocks=n_blocks * RB // TOKEN_BLOCK, tile=tile)


def _mix(x2d, mod, mod_row, cache, rope_tabs, weights, *, S, L, emit_kv, blocks_per_step, tile, cast=()):
    T = x2d.shape[0]
    TM = S * L
    P = cache[0].shape[0] // (T // L * N_KV_HEADS) if cache is not None else 0
    use_rope = rope_tabs is not None
    assert T % TM == 0 and L % ROW_BLOCK == 0
    assert not (use_rope or P) or S == 1
    Lk = P + L

    args = [x2d, mod]
    nrb = TM // ROW_BLOCK
    n_blocks = T // ROW_BLOCK
    step_rows = blocks_per_step * ROW_BLOCK
    steps_per_group = nrb // blocks_per_step
    n_mix_steps = n_blocks // blocks_per_step
    assert nrb % blocks_per_step == 0

    def mixed(s):
        return jnp.minimum(s, n_mix_steps - 1)

    def group(s):
        return mixed(s) // steps_per_group

    def prepared(s):
        return jnp.maximum(s - 1, 0)

    in_specs = [
        pl.BlockSpec((TM, D_MODEL), lambda s: (group(s), 0)),
        _resident(mod.shape),
    ]
    if P:
        args += list(cache)
        in_specs += [pl.BlockSpec((P * N_KV_HEADS, HEAD_DIM), lambda s: (group(s), 0))] * 2
    if use_rope:
        args += list(rope_tabs)
        in_specs += [_resident((L, HEAD_DIM))] * 3
    args += list(weights)
    in_specs += [_resident(w.shape) for w in weights]
    n_steps = T // TM
    def per_group(shape):
        assert shape[0] % n_steps == 0
        blk = (shape[0] // n_steps,) + shape[1:]
        return pl.BlockSpec(blk, lambda s, n=len(blk): (group(s),) + (0,) * (n - 1))

    cast_out_shapes = [ws[0].shape[:-1] + (sum(w.shape[-1] for w in ws),) for ws in cast]
    for ws in cast:
        args += list(ws)
        in_specs += [per_group(w.shape) for w in ws]

    assert T % TOKEN_BLOCK == 0
    out_shape = [jax.ShapeDtypeStruct((T, D_MODEL), F32), jax.ShapeDtypeStruct(_tiles_shape(T, PACKED_CHUNKS), U32),
                 jax.ShapeDtypeStruct((T, LANES), F32),
                 jax.ShapeDtypeStruct((T // TOKEN_BLOCK, 1, TOKEN_BLOCK), I32),
                 jax.ShapeDtypeStruct((SUBLANES, LANES), I32)]
    out_specs = [pl.BlockSpec((step_rows, D_MODEL), lambda s: (mixed(s), 0)),
                 _tiles_spec(step_rows, prepared, PACKED_CHUNKS),
                 pl.BlockSpec((step_rows, LANES), lambda s: (prepared(s), 0)),
                 pl.BlockSpec((T // TOKEN_BLOCK, 1, TOKEN_BLOCK), lambda s: (0, 0, 0)),
                 pl.BlockSpec((SUBLANES, LANES), lambda s: (0, 0))]
    if emit_kv:
        out_shape += [jax.ShapeDtypeStruct((T * N_KV_HEADS, HEAD_DIM), F32)] * 2
        out_specs += [pl.BlockSpec((TM * N_KV_HEADS, HEAD_DIM), lambda s: (group(s), 0))] * 2
    out_shape += [jax.ShapeDtypeStruct(shp, BF16) for shp in cast_out_shapes]
    out_specs += [per_group(shp) for shp in cast_out_shapes]

    scratch = [
        pltpu.VMEM((N_HEADS, TM, HEAD_DIM), BF16),
        pltpu.VMEM((S, Lk, KV_W), BF16),
        pltpu.VMEM((S, Lk, KV_W), BF16),
        pltpu.VMEM((S, L + 2 * POOL_HALO, POOL_W), F32),
        pltpu.VMEM((TM, D_MODEL), BF16),
        pltpu.VMEM((blocks_per_step, ROW_BLOCK, ATTN_W), BF16),
        pltpu.VMEM((2, step_rows, D_MODEL), F32),
        pltpu.VMEM((2, 2, D_MODEL), F32),
        pltpu.VMEM((T, LANES), BF16),
    ]
    kern = functools.partial(_mix_kernel, S=S, L=L, P=P, use_rope=use_rope, emit_kv=emit_kv,
                             n_cast=tuple(len(ws) for ws in cast), n_blocks=n_blocks, U=blocks_per_step,
                             mod_row=mod_row, tile=tile)
    return pl.pallas_call(
        kern,
        grid=(n_mix_steps + 1,),
        in_specs=in_specs,
        out_specs=out_specs,
        out_shape=out_shape,
        scratch_shapes=scratch,
        compiler_params=pltpu.CompilerParams(
            dimension_semantics=("arbitrary",), vmem_limit_bytes=V7X_VMEM_LIMIT_BYTES),
        name="mixer_rope" if use_rope else "mixer_ctx",
    )(*args)


def _plan_rows(oh_ref, dest_ref, meta_ref, *, n_blocks, tile):
    TB = TOKEN_BLOCK
    lane = lax.broadcasted_iota(I32, (SUBLANES, LANES), 1)

    def count(b, acc):
        oh = oh_ref[pl.ds(pl.multiple_of(b * TB, TB), TB), :].astype(F32)
        return acc + jnp.sum(oh, axis=0, keepdims=True)

    counts = lax.fori_loop(0, n_blocks, count, jnp.zeros((SUBLANES, LANES), F32))
    padded = jnp.floor((counts + (tile - 0.5)) * (1.0 / tile)) * tile
    ends = padded
    step = 1
    while step < LANES:
        ends = ends + jnp.where(lane >= step, pltpu.roll(ends, step, 1), 0.0)
        step *= 2
    starts = ends - padded

    tri = jnp.where(lax.broadcasted_iota(I32, (TB, TB), 1) < lax.broadcasted_iota(I32, (TB, TB), 0),
                    1.0, 0.0).astype(BF16)

    def place(b, seen):
        oh = oh_ref[pl.ds(pl.multiple_of(b * TB, TB), TB), :]
        ohf = oh.astype(F32)
        rank = jnp.dot(tri, oh, preferred_element_type=F32)
        base = (starts + seen)[0:1, :]
        d = jnp.sum(ohf * (rank + base), axis=1, keepdims=True)
        dest_ref[b] = _row(d).astype(I32)
        return seen + jnp.sum(ohf, axis=0, keepdims=True)

    lax.fori_loop(0, n_blocks, place, jnp.zeros((SUBLANES, LANES), F32))

    tile_row0 = lax.broadcasted_iota(I32, (LANES, LANES), 0).astype(F32) * tile
    is_bucket = lax.broadcasted_iota(I32, (LANES, LANES), 1) < N_BUCKETS
    done = jnp.sum(jnp.where(is_bucket, jnp.where(ends[0:1, :] <= tile_row0, 1.0, 0.0), 0.0),
                   axis=1, keepdims=True)
    bkt = jnp.minimum(done, N_BUCKETS - 1.0)
    grp = (jnp.where(bkt >= PAIRS_PER_GROUP, 1.0, 0.0) + jnp.where(bkt >= 2 * PAIRS_PER_GROUP, 1.0, 0.0)
           + jnp.where(bkt >= 3 * PAIRS_PER_GROUP, 1.0, 0.0))
    pair = bkt - PAIRS_PER_GROUP * grp
    a = jnp.where(pair >= 3.0, 1.0, 0.0) + jnp.where(pair >= 5.0, 1.0, 0.0)
    b = pair - a * (7.0 - a) * 0.5 + a + 1.0
    e1 = EXP_PER_GROUP * grp + a
    e2 = EXP_PER_GROUP * grp + b
    meta = jnp.concatenate(
        [_row(e1), _row(e2), jnp.floor(ends[0:1, :] * (1.0 / tile) + 0.5),
         jnp.zeros((SUBLANES - 3, LANES), F32)], axis=0)
    meta_ref[...] = meta.astype(I32)


def _sc_move_rows(src_v, table_hbm, out_hbm, lo, n_rows, idx_v, pieces_v, sem):
    chunks = pieces_v.shape[0] // SC_ROWS_PER_STEP
    lane = lax.iota(I32, SC_LANES)
    row_in_group = lane & (SUBLANES - 1)
    chunk_in_pair = lane >> 3
    rows_per_gather = SC_PIECES_PER_GATHER // chunks

    @pl.loop(0, n_rows // SC_ROWS_PER_STEP)
    def _(step):
        copies = []
        for g in range(SC_ROWS_PER_STEP // rows_per_gather):
            r0 = step * SC_ROWS_PER_STEP + g * rows_per_gather
            for v in range(SC_PIECES_PER_GATHER // SC_LANES):
                group, chunk0 = v // (chunks // 2), 2 * (v % (chunks // 2))
                tok = plsc.load_gather(src_v, [r0 + group * SUBLANES + row_in_group])
                piece = (tok >> 3) * (SUBLANES * chunks) + (chunk0 + chunk_in_pair) * SUBLANES + (tok & 7)
                idx_v[pl.ds(g * SC_PIECES_PER_GATHER + v * SC_LANES, SC_LANES)] = piece
            window = pl.ds(g * SC_PIECES_PER_GATHER, SC_PIECES_PER_GATHER)
            copies.append(pltpu.async_copy(table_hbm.at[idx_v.at[window]], pieces_v.at[window], sem))
        for cp in copies:
            cp.wait()
        first = pl.multiple_of((lo + step * SC_ROWS_PER_STEP) * chunks, SC_ROWS_PER_STEP * chunks)
        pltpu.sync_copy(pieces_v, out_hbm.at[pl.ds(first, SC_ROWS_PER_STEP * chunks)])


def _sc_scratch(chunks, dtype):
    return [pltpu.VMEM((SC_ROWS_PER_STEP * chunks,), I32), pltpu.VMEM((SC_ROWS_PER_STEP * chunks, LANES), dtype)]


def _sc_dispatch(h2_flat, gate_rows, dest, n_rows):
    T = dest.shape[0]
    per_worker = n_rows // SC_WORKERS
    rows_per_step = SC_ROWS_PER_STEP
    chunks = h2_flat.shape[0] // T
    assert n_rows % SC_WORKERS == 0 and per_worker % rows_per_step == 0 and T % SC_LANES == 0
    mesh = plsc.VectorSubcoreMesh(core_axis_name="c", subcore_axis_name="s")

    @functools.partial(
        pl.kernel, mesh=mesh,
        out_type=[jax.ShapeDtypeStruct((n_rows * chunks, LANES), h2_flat.dtype),
                  jax.ShapeDtypeStruct((n_rows, LANES), F32)],
        scratch_types=[pltpu.VMEM((T,), I32), pltpu.VMEM((per_worker,), I32)]
        + _sc_scratch(chunks, h2_flat.dtype)
        + [pltpu.VMEM((rows_per_step, LANES), F32), pltpu.SemaphoreType.DMA, pltpu.SemaphoreType.DMA],
        compiler_params=pltpu.CompilerParams(use_tc_tiling_on_sc=True, needs_layout_passes=False),
        name="sc_dispatch",
    )
    def dispatch(h2_hbm, gate_hbm, dest_hbm, out_h_hbm, out_g_hbm,
                 dest_v, src_v, idx_v, pieces_v, gates_v, sem_h, sem_g):
        worker = lax.axis_index("s") * SC_CORES + lax.axis_index("c")
        lo = worker * per_worker
        pltpu.sync_copy(dest_hbm, dest_v)

        @pl.loop(0, per_worker // SC_LANES)
        def _(j):
            j0 = pl.multiple_of(j * SC_LANES, SC_LANES)
            src_v[pl.ds(j0, SC_LANES)] = lax.rem(lo + j0 + lax.iota(I32, SC_LANES), T)

        @pl.loop(0, T // SC_LANES)
        def _(j):
            t0 = pl.multiple_of(j * SC_LANES, SC_LANES)
            d = dest_v[pl.ds(t0, SC_LANES)] - lo
            mine = (d >= 0) & (d < per_worker)
            plsc.store_scatter(src_v, [jnp.where(mine, d, 0)], t0 + lax.iota(I32, SC_LANES), mask=mine)

        @pl.loop(0, per_worker // rows_per_step)
        def _(j):
            off = pl.multiple_of(j * rows_per_step, rows_per_step)
            pltpu.async_copy(gate_hbm.at[src_v.at[pl.ds(off, rows_per_step)]], gates_v, sem_g).wait()
            pltpu.sync_copy(gates_v, out_g_hbm.at[pl.ds(lo + off, rows_per_step)])

        _sc_move_rows(src_v, h2_hbm, out_h_hbm, lo, per_worker, idx_v, pieces_v, sem_h)

    return dispatch(h2_flat, gate_rows, dest)


def _expert_kernel(meta, x_ref, gv_ref, wgu_hbm, wd_hbm, o_ref, wgu_ref, wd_ref, ready_s, sems, *, tile, per_step):
    groups = tile // SUBLANES
    n_used = meta[2, LANES - 1]
    step = pl.program_id(0)

    def weight_copies(g):
        experts = pl.ds(g * EXP_PER_GROUP, EXP_PER_GROUP)
        return (pltpu.make_async_copy(wgu_hbm.at[experts], wgu_ref.at[experts], sems.at[g]),
                pltpu.make_async_copy(wd_hbm.at[experts], wd_ref.at[experts], sems.at[g]))

    def land_through(last_group):
        landed = ready_s[0]
        for g in range(N_EXP_GROUPS):
            @pl.when((g >= landed) & (g <= last_group))
            def _():
                for cp in weight_copies(g):
                    cp.wait()
                if g + 1 < N_EXP_GROUPS:
                    for cp in weight_copies(g + 1):
                        cp.start()
        ready_s[0] = jnp.maximum(landed, last_group + 1)

    @pl.when(step == 0)
    def _():
        ready_s[0] = 0
        for cp in weight_copies(0):
            cp.start()

    def one_tile(k, carry):
        t = step * per_step + k
        rows = pl.ds(pl.multiple_of(k * groups, groups), groups)

        @pl.when(t < n_used)
        def _():
            land_through(meta[0, t] // EXP_PER_GROUP)
            x = _unpack_bf16_pairs(_load_tiles(x_ref.at[rows]))
            gv = gv_ref[pl.ds(pl.multiple_of(k * tile, SUBLANES), tile), :]
            lane = lax.broadcasted_iota(I32, gv.shape, 1)
            out = None
            for e in (meta[0, t], meta[1, t]):
                ge = jnp.sum(jnp.where(lane == EXPERT_LANE0 + e, gv, 0.0), axis=-1, keepdims=True)
                h = jnp.dot(x, wgu_ref[e], preferred_element_type=F32)
                hg = h[:, 0:D_EXPERT]
                hid = (hg * _sigmoid(hg) * h[:, D_EXPERT:2 * D_EXPERT] * ge).astype(BF16)
                y = jnp.dot(hid, wd_ref[e], preferred_element_type=F32)
                out = y if out is None else out + y
            _store_tiles(o_ref.at[rows], _pack_bf16_pairs(out))

        return carry

    lax.fori_loop(0, per_step, one_tile, 0)

    @pl.when(step == pl.num_programs(0) - 1)
    def _():
        land_through(N_EXP_GROUPS - 1)


def _experts(sorted_h2, sorted_gates, meta, wgu, wd, tiling):
    tile, per_step, n_tiles = tiling
    step_rows = tile * per_step
    assert n_tiles * tile == sorted_h2.shape[0] * SUBLANES and n_tiles % per_step == 0

    def last_used(i, meta):
        return jnp.minimum(i, (meta[2, LANES - 1] - 1) // per_step)

    return pl.pallas_call(
        functools.partial(_expert_kernel, tile=tile, per_step=per_step),
        grid_spec=pltpu.PrefetchScalarGridSpec(
            num_scalar_prefetch=1,
            grid=(n_tiles // per_step,),
            in_specs=[
                _tiles_spec(step_rows, last_used, PACKED_CHUNKS),
                pl.BlockSpec((step_rows, LANES), lambda *a: (last_used(*a), 0)),
                pl.BlockSpec(memory_space=pl.ANY), pl.BlockSpec(memory_space=pl.ANY),
            ],
            out_specs=_tiles_spec(step_rows, last_used, PACKED_CHUNKS),
            scratch_shapes=[pltpu.VMEM(wgu.shape, wgu.dtype), pltpu.VMEM(wd.shape, wd.dtype),
                            pltpu.SMEM((1,), I32), pltpu.SemaphoreType.DMA((N_EXP_GROUPS,))],
        ),
        out_shape=jax.ShapeDtypeStruct(_tiles_shape(n_tiles * tile, PACKED_CHUNKS), U32),
        compiler_params=pltpu.CompilerParams(
            dimension_semantics=("arbitrary",), vmem_limit_bytes=V7X_VMEM_LIMIT_BYTES),
        name="moe_experts",
    )(meta, sorted_h2, sorted_gates, wgu, wd)


def _sc_row_gather(table_flat, idx, chunks):
    n = idx.shape[0]
    per_worker = n // SC_WORKERS
    assert n % SC_WORKERS == 0 and per_worker % SC_ROWS_PER_STEP == 0
    mesh = plsc.VectorSubcoreMesh(core_axis_name="c", subcore_axis_name="s")

    @functools.partial(
        pl.kernel, mesh=mesh,
        out_type=jax.ShapeDtypeStruct((n * chunks, LANES), table_flat.dtype),
        scratch_types=[pltpu.VMEM((per_worker,), I32)] + _sc_scratch(chunks, table_flat.dtype)
        + [pltpu.SemaphoreType.DMA],
        compiler_params=pltpu.CompilerParams(use_tc_tiling_on_sc=True, needs_layout_passes=False),
        name="sc_row_gather",
    )
    def gather(table_hbm, idx_hbm, out_hbm, src_v, idx_v, pieces_v, sem):
        worker = lax.axis_index("s") * SC_CORES + lax.axis_index("c")
        lo = worker * per_worker
        pltpu.sync_copy(idx_hbm.at[pl.ds(lo, per_worker)], src_v)
        _sc_move_rows(src_v, table_hbm, out_hbm, lo, per_worker, idx_v, pieces_v, sem)

    return gather(table_flat, idx)


def _final_kernel(x_ref, moe_ref, gt2_ref, gf_ref, o_ref, *, mod_row):
    gt2 = gt2_ref[pl.ds(mod_row(pl.program_id(0)), 1), :]
    y = x_ref[...] + gt2 * _unpack_bf16_pairs(_load_tiles(moe_ref)).astype(F32)
    o_ref[...] = _rms(y) * gf_ref[...]


def _final(xmid, moe_rows, mod, mod_row, gf):
    T = xmid.shape[0]
    return pl.pallas_call(
        functools.partial(_final_kernel, mod_row=mod_row),
        grid=(T // FINAL_BLOCK,),
        in_specs=[
            pl.BlockSpec((FINAL_BLOCK, D_MODEL), lambda i: (i, 0)),
            _tiles_spec(FINAL_BLOCK, lambda i: i, PACKED_CHUNKS),
            pl.BlockSpec((COND_ROWS, D_MODEL), lambda i: (0, 5)),
            pl.BlockSpec((1, D_MODEL), lambda i: (0, 0)),
        ],
        out_specs=pl.BlockSpec((FINAL_BLOCK, D_MODEL), lambda i: (i, 0)),
        out_shape=jax.ShapeDtypeStruct((T, D_MODEL), F32),
        compiler_params=pltpu.CompilerParams(
            dimension_semantics=("arbitrary",), vmem_limit_bytes=V7X_VMEM_LIMIT_BYTES),
        name="moe_final",
    )(xmid, moe_rows, mod, gf)


def _flat(tiles):
    return tiles.reshape(-1, LANES)


def _expert_tiling(T):
    tile = -(-(T * 9) // (8 * N_BUCKETS * 64)) * 64
    per_step = max(1, EXPERT_STEP_ROWS // tile)
    n_tiles = (T + N_BUCKETS * (tile - 1)) // tile
    while n_tiles % per_step or (n_tiles * tile) % (SC_WORKERS * SC_ROWS_PER_STEP):
        n_tiles += 1
    return tile, per_step, n_tiles


def _moe_dispatch(h2_tiles, gate_rows, dest, tiling):
    tile, _, n_tiles = tiling
    n_rows = n_tiles * tile
    assert n_tiles <= LANES
    sorted_h2, sorted_gates = _sc_dispatch(_flat(h2_tiles), gate_rows, dest, n_rows)
    return sorted_h2.reshape(_tiles_shape(n_rows, PACKED_CHUNKS)), sorted_gates


def _moe_unpermute(moe_sorted_tiles, dest):
    chunks = moe_sorted_tiles.shape[1]
    return _sc_row_gather(_flat(moe_sorted_tiles), dest, chunks).reshape(_tiles_shape(dest.shape[0], chunks))


def _rope_tables(n_tokens):
    t = np.arange(n_tokens)
    row = (t // GRID_W).astype(np.float32)
    col = (t % GRID_W).astype(np.float32)
    freq = np.float32(ROPE_THETA) ** (-np.arange(ROPE_NF, dtype=np.float32) / np.float32(ROPE_NF))
    ang = np.concatenate([row[:, None] * freq] * 2 + [col[:, None] * freq] * 2, axis=-1)
    first = (np.arange(HEAD_DIM) % (2 * ROPE_NF)) < ROPE_NF
    sin = np.sin(ang)
    zero = np.float32(0.0)
    return (jnp.asarray(np.cos(ang)), jnp.asarray(np.where(first, -sin, zero)),
            jnp.asarray(np.where(first, zero, sin)))


def kernel(x_prompt, x_sample, cache_k, cache_v, c, c_ctx, norm1_g, norm2_g, w_ada, b_ada, w_in, q_norm_g, k_norm_g, w_pool, pool_scale, w_branch_a, w_branch_b, w_out, w_router_group, w_router_expert, w_exp_gate, w_exp_up, w_exp_down, final_norm_g):
    assert norm1_g.shape[0] == 1, "single-layer trunk"
    B, L_ctx, _ = x_prompt.shape
    Bs, L_lat, _ = x_sample.shape
    P = cache_k.shape[2]
    assert 1 + Bs <= COND_ROWS

    cond = jnp.concatenate([c_ctx[None, :], c, jnp.zeros((COND_ROWS - 1 - Bs, D_MODEL), F32)], axis=0)
    mod, w_in_b, wa_b, wb_b, wo_b, wpool_b = _ada(
        cond, w_ada[0], b_ada[0][None, :], w_pool[0],
        cast=(w_in[0], w_branch_a[0], w_branch_b[0], w_out[0]))

    wr = jnp.concatenate([w_router_group[0], w_router_expert[0],
                          jnp.zeros((D_MODEL, LANES - N_EXP_GROUPS - N_EXPERTS), F32)], axis=1)
    wr_hi = wr.astype(BF16)
    wr_lo = (wr - wr_hi.astype(F32)).astype(BF16)
    mix_w = (norm1_g[0][None, :], w_in_b, q_norm_g[0][None, :], k_norm_g[0][None, :],
             wpool_b, pool_scale[0][None, :], wa_b, wb_b, wo_b,
             norm2_g[0][None, :], jnp.concatenate([wr_hi, wr_lo], axis=1))
    gf = final_norm_g[None, :]

    xp2 = x_prompt.reshape(B * L_ctx, D_MODEL)
    tiling_p = _expert_tiling(B * L_ctx)
    xmid_p, h2_p, gate_p, dest_p, meta_p, knew, vnew, wgu, wd = _mix(
        xp2, mod, lambda i: 0, None, None, mix_w, S=2, L=L_ctx, emit_kv=True, blocks_per_step=2,
        tile=tiling_p[0], cast=((w_exp_gate[0], w_exp_up[0]), (w_exp_down[0],)))
    dest_p = dest_p.reshape(B * L_ctx)
    sh_p, sg_p = _moe_dispatch(h2_p, gate_p, dest_p, tiling_p)

    xs2 = x_sample.reshape(Bs * L_lat, D_MODEL)
    cache = (cache_k.reshape(Bs * P * N_KV_HEADS, HEAD_DIM), cache_v.reshape(Bs * P * N_KV_HEADS, HEAD_DIM))
    tiling_s = _expert_tiling(Bs * L_lat)
    xmid_s, h2_s, gate_s, dest_s, meta_s = _mix(
        xs2, mod, lambda i: 1 + i, cache, _rope_tables(L_lat), mix_w,
        S=1, L=L_lat, emit_kv=False, blocks_per_step=1, tile=tiling_s[0])
    dest_s = dest_s.reshape(Bs * L_lat)
    sh_s, sg_s = _moe_dispatch(h2_s, gate_s, dest_s, tiling_s)

    moe_p = _moe_unpermute(_experts(sh_p, sg_p, meta_p, wgu, wd, tiling_p), dest_p)
    moe_s = _moe_unpermute(_experts(sh_s, sg_s, meta_s, wgu, wd, tiling_s), dest_s)
    y_prompt = _final(xmid_p, moe_p, mod, lambda i: 0, gf)
    blocks_per_seq = L_lat // FINAL_BLOCK
    y_sample = _final(xmid_s, moe_s, mod, lambda i: 1 + i // blocks_per_seq, gf)

    return (y_prompt.reshape(B, L_ctx, D_MODEL), y_sample.reshape(Bs, L_lat, D_MODEL),
            knew.reshape(B, 1, L_ctx, N_KV_HEADS, HEAD_DIM), vnew.reshape(B, 1, L_ctx, N_KV_HEADS, HEAD_DIM))
```

```python
import functools

import numpy as np
import jax
import jax.numpy as jnp
from jax import lax
from jax.experimental import pallas as pl
from jax.experimental.pallas import tpu as pltpu
from jax.experimental.pallas import tpu_sc as plsc

F32 = jnp.float32
BF16 = jnp.bfloat16
I32 = jnp.int32
U32 = jnp.uint32

D_MODEL = 1024
HEAD_DIM = 128
N_HEADS = 8
N_KV_HEADS = 2
GROUP = N_HEADS // N_KV_HEADS
ATTN_W = N_HEADS * HEAD_DIM
KV_W = N_KV_HEADS * HEAD_DIM
POOL_WINDOWS = (2, 4, 8, 16)
POOL_GC = 128
POOL_W = POOL_GC * len(POOL_WINDOWS)
IN_W = ATTN_W + 2 * KV_W + POOL_W + 2 * D_MODEL
GATE_COL = ATTN_W + 2 * KV_W + POOL_W
GRID_W = 64
ROPE_THETA = 10000.0
ROPE_NF = HEAD_DIM // 4
N_EXP_GROUPS = 4
EXP_PER_GROUP = 4
N_EXPERTS = 16
D_EXPERT = 256
EPS = 1e-6
LOG2_E = 1.4426950408889634

LANES = 128
SUBLANES = 8
COND_ROWS = SUBLANES
POOL_HALO = 8
ROW_BLOCK = 256
ADA_COLS = 768
EXPERT_LANE0 = N_EXP_GROUPS
PAIRS_PER_GROUP = EXP_PER_GROUP * (EXP_PER_GROUP - 1) // 2
N_BUCKETS = N_EXP_GROUPS * PAIRS_PER_GROUP
EXPERT_STEP_ROWS = 1536
TOKEN_BLOCK = 1024
FINAL_BLOCK = 1024
ROW_CHUNKS = D_MODEL // LANES
SC_CORES = 2
SC_SUBCORES = 16
SC_WORKERS = SC_CORES * SC_SUBCORES
SC_LANES = 16
SC_PIECES_PER_GATHER = 128
SC_ROWS_PER_STEP = 64
PACKED_CHUNKS = ROW_CHUNKS // 2
V7X_VMEM_LIMIT_BYTES = 56 * 1024 * 1024


def _sigmoid(x):
    return 1.0 / (1.0 + jnp.exp(-x))


def _rms(x):
    return x * lax.rsqrt(jnp.mean(x * x, axis=-1, keepdims=True) + EPS)


def _resident(shape):
    zeros = (0,) * len(shape)
    return pl.BlockSpec(shape, lambda i, *_: zeros, pipeline_mode=pl.Buffered(1))


def _tiles_shape(n, chunks=ROW_CHUNKS):
    return (n // SUBLANES, chunks, SUBLANES, LANES)


def _tiles_spec(n, block_index, chunks=ROW_CHUNKS):
    return pl.BlockSpec(_tiles_shape(n, chunks), lambda *a: (block_index(*a), 0, 0, 0))


def _store_tiles(ref, x):
    for c in range(ref.shape[1]):
        ref[:, c, :, :] = x[:, c * LANES:(c + 1) * LANES].reshape(x.shape[0] // SUBLANES, SUBLANES, LANES)


def _load_tiles(ref):
    n = ref.shape[0] * SUBLANES
    return jnp.concatenate([ref[:, c, :, :].reshape(n, LANES) for c in range(ref.shape[1])], axis=1)


def _pack_bf16_pairs(x):
    bits = pltpu.bitcast(x.astype(BF16).astype(F32), U32)
    w = x.shape[1] // 2
    return bits[:, :w] | (bits[:, w:] >> 16)


def _unpack_bf16_pairs(words):
    hi = pltpu.bitcast(words & jnp.uint32(0xFFFF0000), F32).astype(BF16)
    lo = pltpu.bitcast(words << 16, F32).astype(BF16)
    return jnp.concatenate([hi, lo], axis=1)


def _row(x):
    return jnp.transpose(jnp.broadcast_to(x, (x.shape[0], LANES)))[0:1, :]


def _ada_kernel(c_ref, w_ref, b_ref, *refs, steps_per_pool_group):
    n_cast = len(refs) // 2 - 1
    c = c_ref[...]
    s = (c * _sigmoid(c)).astype(BF16)
    refs[n_cast + 1][...] = jnp.dot(s, w_ref[...].astype(BF16), preferred_element_type=F32) + b_ref[...]
    for src, dst in zip(refs[:n_cast], refs[n_cast + 2:]):
        dst[...] = src[...].astype(BF16)
    pool_src, pool_dst = refs[n_cast], refs[-1]
    wide = jnp.concatenate([pool_src[0]] * len(POOL_WINDOWS), axis=1)
    lane_group = lax.broadcasted_iota(I32, wide.shape, 1) // POOL_GC
    pool_dst[...] = jnp.where(lane_group == pl.program_id(0) // steps_per_pool_group, wide, 0.0).astype(BF16)


def _ada(cond, w_ada, b_ada, w_pool, cast=()):
    n = w_ada.shape[1]
    n_steps = n // ADA_COLS
    cast_specs = []
    for w in cast:
        assert w.ndim == 2 and w.shape[0] % (n_steps * 2 * SUBLANES) == 0
        cast_specs.append(pl.BlockSpec((w.shape[0] // n_steps, w.shape[1]), lambda j: (j, 0)))
    pool_rows = POOL_W // n_steps
    spg = POOL_GC // pool_rows
    assert w_pool.shape == (len(POOL_WINDOWS), POOL_GC, POOL_GC) and POOL_GC % pool_rows == 0
    assert pool_rows % (2 * SUBLANES) == 0
    pool_in = pl.BlockSpec((1, pool_rows, POOL_GC), lambda j: (j // spg, j % spg, 0))
    pool_out = pl.BlockSpec((pool_rows, POOL_W), lambda j: (j, 0))
    return pl.pallas_call(
        functools.partial(_ada_kernel, steps_per_pool_group=spg),
        grid=(n_steps,),
        in_specs=[
            pl.BlockSpec((COND_ROWS, D_MODEL), lambda j: (0, 0)),
            pl.BlockSpec((D_MODEL, ADA_COLS), lambda j: (0, j)),
            pl.BlockSpec((1, ADA_COLS), lambda j: (0, j)),
        ] + cast_specs + [pool_in],
        out_specs=[pl.BlockSpec((COND_ROWS, ADA_COLS), lambda j: (0, j))] + cast_specs + [pool_out],
        out_shape=[jax.ShapeDtypeStruct((COND_ROWS, n), F32)] + [jax.ShapeDtypeStruct(w.shape, BF16) for w in cast]
        + [jax.ShapeDtypeStruct((POOL_W, POOL_W), BF16)],
        name="ada_mod",
    )(cond, w_ada, b_ada, *cast, w_pool)


def _route(logits):
    lane = lax.broadcasted_iota(I32, logits.shape, 1).astype(F32)
    neg = jnp.float32(-1e30)
    far = jnp.float32(LANES)
    is_g = lane < N_EXP_GROUPS
    gl = jnp.where(is_g, logits, neg)
    gmax = jnp.max(gl, axis=-1, keepdims=True)
    gsel = jnp.min(jnp.where(gl == gmax, lane, far), axis=-1, keepdims=True)
    psel = 1.0 / jnp.sum(jnp.where(is_g, jnp.exp(gl - gmax), 0.0), axis=-1, keepdims=True)
    e_lo = EXPERT_LANE0 + EXP_PER_GROUP * gsel
    el = jnp.where(lane >= e_lo, jnp.where(lane < e_lo + EXP_PER_GROUP, logits, neg), neg)
    v1 = jnp.max(el, axis=-1, keepdims=True)
    i1 = jnp.min(jnp.where(el == v1, lane, far), axis=-1, keepdims=True)
    el2 = jnp.where(lane == i1, neg, el)
    v2 = jnp.max(el2, axis=-1, keepdims=True)
    i2 = jnp.min(jnp.where(el2 == v2, jnp.where(lane == i1, far, lane), far), axis=-1, keepdims=True)
    e2 = jnp.exp(v2 - v1)
    w1 = psel / (1.0 + e2)
    w2 = psel * e2 / (1.0 + e2)
    gate = jnp.where(lane == i1, w1, jnp.where(lane == i2, w2, 0.0))
    a = jnp.minimum(i1, i2) - e_lo
    b = jnp.maximum(i1, i2) - e_lo
    pair = a * (7.0 - a) * 0.5 + (b - a - 1.0)
    return gate, gsel * PAIRS_PER_GROUP + pair


def _mix_kernel(*refs, S, L, P, use_rope, emit_kv, n_cast, n_blocks, U, mod_row, tile):
    it = iter(refs)
    x_ref = next(it)
    mod_ref = next(it)
    if P:
        ck_ref = next(it)
        cv_ref = next(it)
    if use_rope:
        cos_ref = next(it)
        sneg_ref = next(it)
        spos_ref = next(it)
    (g1_ref, win_ref, qg_ref, kg_ref, wpool_hbm, pscale_ref, wa_hbm, wb_hbm, wo_hbm,
     g2_ref, wr_ref) = (next(it) for _ in range(11))
    cast_in = [[next(it) for _ in range(n)] for n in n_cast]
    xmid_ref = next(it)
    h2_ref = next(it)
    gate_ref = next(it)
    dest_ref = next(it)
    meta_ref = next(it)
    if emit_kv:
        knew_ref = next(it)
        vnew_ref = next(it)
    cast_out = [next(it) for _ in n_cast]
    q_s, k_s, v_s, xp_s, h_s, attn_s, xm_s, mod2_s, oh_s = (next(it) for _ in range(9))
    wpool_ref, wa_ref, wb_ref, wo_ref, late_sems = (next(it) for _ in range(5))
    late_copies = [pltpu.make_async_copy(src, dst, late_sems.at[i]) for i, (src, dst) in enumerate(
        ((wa_hbm, wa_ref), (wpool_hbm, wpool_ref), (wb_hbm, wb_ref), (wo_hbm, wo_ref)))]

    TM = S * L
    RB = ROW_BLOCK
    nrb = TM // RB
    n_steps = n_blocks // U
    score_gain = HEAD_DIM ** -0.5 * LOG2_E
    step = pl.program_id(0)
    block0 = U * jnp.minimum(step, n_steps - 1)
    slot = step % 2

    mod_at = pl.ds(mod_row(jnp.minimum(step, n_steps - 1) // (nrb // U)), 1)
    sh1 = mod_ref[mod_at, 0:D_MODEL]
    gain1 = g1_ref[...] * (1.0 + mod_ref[mod_at, D_MODEL:2 * D_MODEL])
    gt1 = mod_ref[mod_at, 2 * D_MODEL:3 * D_MODEL]
    sh2 = mod_ref[mod_at, 3 * D_MODEL:4 * D_MODEL]
    gain2 = g2_ref[...] * (1.0 + mod_ref[mod_at, 4 * D_MODEL:5 * D_MODEL])
    qg = qg_ref[...] * score_gain
    kg = kg_ref[...]

    def project(r, carry):
        r0 = pl.multiple_of(r * RB, RB)
        s = r0 // L
        o = pl.multiple_of(r0 % L, RB)
        hb = (_rms(x_ref[pl.ds(r0, RB), :]) * gain1 + sh1).astype(BF16)
        h_s[pl.ds(r0, RB), :] = hb
        p1 = jnp.dot(hb, win_ref[:, 0:GATE_COL], preferred_element_type=F32)
        if use_rope:
            cs = cos_ref[pl.ds(o, RB), :]
            sn = sneg_ref[pl.ds(o, RB), :]
            sp = spos_ref[pl.ds(o, RB), :]

        def rope(t):
            return (t * cs + pltpu.roll(t, HEAD_DIM - ROPE_NF, 1) * sn + pltpu.roll(t, ROPE_NF, 1) * sp)

        for hd in range(N_HEADS):
            qh = _rms(p1[:, hd * HEAD_DIM:(hd + 1) * HEAD_DIM]) * qg
            if use_rope:
                qh = rope(qh)
            q_s[hd, pl.ds(r0, RB), :] = qh.astype(BF16)
        for kh in range(N_KV_HEADS):
            c0 = ATTN_W + kh * HEAD_DIM
            kk = _rms(p1[:, c0:c0 + HEAD_DIM]) * kg
            if emit_kv:
                knew_ref[pl.ds(N_KV_HEADS * r0 + kh, RB, stride=N_KV_HEADS), :] = kk
            if use_rope:
                kk = rope(kk)
            k_s[s, pl.ds(P + o, RB), kh * HEAD_DIM:(kh + 1) * HEAD_DIM] = kk.astype(BF16)
        vv = p1[:, ATTN_W + KV_W:ATTN_W + 2 * KV_W]
        if emit_kv:
            for kh in range(N_KV_HEADS):
                vnew_ref[pl.ds(N_KV_HEADS * r0 + kh, RB, stride=N_KV_HEADS), :] = (
                    vv[:, kh * HEAD_DIM:(kh + 1) * HEAD_DIM])
        v_s[s, pl.ds(P + o, RB), :] = vv.astype(BF16)
        xp_s[s, pl.ds(POOL_HALO + o, RB), :] = p1[:, ATTN_W + 2 * KV_W:GATE_COL]
        return carry

    @pl.when(step == 0)
    def _():
        xm_s[1] = jnp.zeros((U * RB, D_MODEL), F32)
        mod2_s[1] = jnp.zeros((2, D_MODEL), F32)
        for cp in late_copies:
            cp.start()

    @pl.when((step < n_steps) & (step % (nrb // U) == 0))
    def _():
        if P:
            for kh in range(N_KV_HEADS):
                cols = slice(kh * HEAD_DIM, (kh + 1) * HEAD_DIM)
                k_s[0, 0:P, cols] = ck_ref[pl.ds(kh, P, stride=N_KV_HEADS), :].astype(BF16)
                v_s[0, 0:P, cols] = cv_ref[pl.ds(kh, P, stride=N_KV_HEADS), :].astype(BF16)
        xp_s[:, 0:POOL_HALO, :] = jnp.zeros((S, POOL_HALO, POOL_W), F32)
        xp_s[:, L + POOL_HALO:L + 2 * POOL_HALO, :] = jnp.zeros((S, POOL_HALO, POOL_W), F32)
        lax.fori_loop(0, TM // RB, project, 0)
        for srcs, dst in zip(cast_in, cast_out):
            col = 0
            for src in srcs:
                dst[..., col:col + src.shape[-1]] = src[...].astype(BF16)
                col += src.shape[-1]

    @pl.when(step == 0)
    def _():
        for cp in late_copies:
            cp.wait()

    def mix(u):
        r0 = pl.multiple_of(((block0 + u) % nrb) * RB, RB)
        s = r0 // L
        o = pl.multiple_of(r0 % L, RB)
        attn_u = attn_s.at[u]
        rows = slice(u * RB, (u + 1) * RB)

        for hd in range(N_HEADS):
            kh = hd // GROUP
            k = k_s[s, :, kh * HEAD_DIM:(kh + 1) * HEAD_DIM]
            v = v_s[s, :, kh * HEAD_DIM:(kh + 1) * HEAD_DIM]
            qh = q_s[hd, pl.ds(r0, RB), :]
            sc = lax.dot_general(qh, k, (((1,), (1,)), ((), ())), preferred_element_type=F32)
            e = jnp.exp2(sc - jnp.max(sc, axis=-1, keepdims=True))
            den = jnp.sum(e, axis=-1, keepdims=True)
            oh = jnp.dot(e.astype(BF16), v, preferred_element_type=F32) / den
            attn_u[:, hd * HEAD_DIM:(hd + 1) * HEAD_DIM] = oh.astype(BF16)
        a = jnp.dot(attn_u[...], wa_ref[...], preferred_element_type=F32)

        t = o + lax.broadcasted_iota(I32, (RB, 1), 0)
        RW = RB + 2 * POOL_HALO
        parts = []
        for gi, w in enumerate(POOL_WINDOWS):
            cols = slice(gi * POOL_GC, (gi + 1) * POOL_GC)
            xw = xp_s[s, pl.ds(o, RW), cols]
            run = xw
            span = 1
            while span < w:
                run = run + pltpu.roll(run, span, 0)
                span *= 2
            if w // 2 > 1:
                run = pltpu.roll(run, RW - (w // 2 - 1), 0)
            tot = run[POOL_HALO:POOL_HALO + RB]
            cnt = (jnp.minimum(t + w // 2, L) - jnp.maximum(t - w // 2, 0)).astype(F32)
            parts.append(tot / cnt - xw[POOL_HALO:POOL_HALO + RB])
        dpool = jnp.concatenate(parts, axis=1).astype(BF16)
        pooled = jnp.dot(dpool, wpool_ref[...], preferred_element_type=F32) * pscale_ref[...]
        b = jnp.dot(pooled.astype(BF16), wb_ref[...], preferred_element_type=F32)

        gates = jnp.dot(h_s[pl.ds(r0, RB), :], win_ref[:, GATE_COL:IN_W], preferred_element_type=F32)
        merged = _sigmoid(gates[:, 0:D_MODEL]) * a + _sigmoid(gates[:, D_MODEL:2 * D_MODEL]) * b
        upd = jnp.dot(merged.astype(BF16), wo_ref[...], preferred_element_type=F32)
        xm = x_ref[pl.ds(r0, RB), :] + gt1 * upd
        xmid_ref[rows, :] = xm
        xm_s[slot, rows, :] = xm

    def moe_prep(u):
        rows = slice(u * RB, (u + 1) * RB)
        h2 = _rms(xm_s[1 - slot, rows, :]) * mod2_s[1 - slot, 0:1, :] + mod2_s[1 - slot, 1:2, :]
        hi = h2.astype(BF16)
        lo = (h2 - hi.astype(F32)).astype(BF16)
        l1 = jnp.dot(hi, wr_ref[...], preferred_element_type=F32)
        l2 = jnp.dot(lo, wr_ref[:, 0:LANES], preferred_element_type=F32)
        gate, bucket = _route(l1[:, 0:LANES] + l1[:, LANES:2 * LANES] + l2)
        groups = pl.ds(u * (RB // SUBLANES), RB // SUBLANES)
        _store_tiles(h2_ref.at[groups], _pack_bf16_pairs(h2))
        gate_ref[rows, :] = gate
        lane = lax.broadcasted_iota(I32, (RB, LANES), 1).astype(F32)
        first = pl.multiple_of((U * jnp.maximum(step - 1, 0) + u) * RB, RB)
        oh_s[pl.ds(first, RB), :] = jnp.where(lane == bucket, 1.0, 0.0).astype(BF16)

    mod2_s[slot, 0:1, :] = gain2
    mod2_s[slot, 1:2, :] = sh2

    @pl.when(step < n_steps)
    def _():
        for u in range(U):
            moe_prep(u)
        for u in range(U):
            mix(u)

    @pl.when(step == n_steps)
    def _():
        for u in range(U):
            moe_prep(u)
        _plan_rows(oh_s, dest_ref, meta_ref, n_blocks=n_blocks * RB // TOKEN_BLOCK, tile=tile)


def _mix(x2d, mod, mod_row, cache, rope_tabs, weights, *, S, L, emit_kv, blocks_per_step, tile, cast=()):
    T = x2d.shape[0]
    TM = S * L
    P = cache[0].shape[0] // (T // L * N_KV_HEADS) if cache is not None else 0
    use_rope = rope_tabs is not None
    assert T % TM == 0 and L % ROW_BLOCK == 0
    assert not (use_rope or P) or S == 1
    Lk = P + L

    args = [x2d, mod]
    nrb = TM // ROW_BLOCK
    n_blocks = T // ROW_BLOCK
    step_rows = blocks_per_step * ROW_BLOCK
    steps_per_group = nrb // blocks_per_step
    n_mix_steps = n_blocks // blocks_per_step
    assert nrb % blocks_per_step == 0

    def mixed(s):
        return jnp.minimum(s, n_mix_steps - 1)

    def group(s):
        return mixed(s) // steps_per_group

    def prepared(s):
        return jnp.maximum(s - 1, 0)

    in_specs = [
        pl.BlockSpec((TM, D_MODEL), lambda s: (group(s), 0)),
        _resident(mod.shape),
    ]
    if P:
        args += list(cache)
        in_specs += [pl.BlockSpec((P * N_KV_HEADS, HEAD_DIM), lambda s: (group(s), 0))] * 2
    if use_rope:
        args += list(rope_tabs)
        in_specs += [_resident((L, HEAD_DIM))] * 3
    args += list(weights)
    late = (4, 6, 7, 8)
    in_specs += [pl.BlockSpec(memory_space=pl.ANY) if i in late else _resident(w.shape)
                 for i, w in enumerate(weights)]
    n_steps = T // TM
    def per_group(shape):
        assert shape[0] % n_steps == 0
        blk = (shape[0] // n_steps,) + shape[1:]
        return pl.BlockSpec(blk, lambda s, n=len(blk): (group(s),) + (0,) * (n - 1))

    cast_out_shapes = [ws[0].shape[:-1] + (sum(w.shape[-1] for w in ws),) for ws in cast]
    for ws in cast:
        args += list(ws)
        in_specs += [per_group(w.shape) for w in ws]

    assert T % TOKEN_BLOCK == 0
    out_shape = [jax.ShapeDtypeStruct((T, D_MODEL), F32), jax.ShapeDtypeStruct(_tiles_shape(T, PACKED_CHUNKS), U32),
                 jax.ShapeDtypeStruct((T, LANES), F32),
                 jax.ShapeDtypeStruct((T // TOKEN_BLOCK, 1, TOKEN_BLOCK), I32),
                 jax.ShapeDtypeStruct((SUBLANES, LANES), I32)]
    out_specs = [pl.BlockSpec((step_rows, D_MODEL), lambda s: (mixed(s), 0)),
                 _tiles_spec(step_rows, prepared, PACKED_CHUNKS),
                 pl.BlockSpec((step_rows, LANES), lambda s: (prepared(s), 0)),
                 pl.BlockSpec((T // TOKEN_BLOCK, 1, TOKEN_BLOCK), lambda s: (0, 0, 0)),
                 pl.BlockSpec((SUBLANES, LANES), lambda s: (0, 0))]
    if emit_kv:
        out_shape += [jax.ShapeDtypeStruct((T * N_KV_HEADS, HEAD_DIM), F32)] * 2
        out_specs += [pl.BlockSpec((TM * N_KV_HEADS, HEAD_DIM), lambda s: (group(s), 0))] * 2
    out_shape += [jax.ShapeDtypeStruct(shp, BF16) for shp in cast_out_shapes]
    out_specs += [per_group(shp) for shp in cast_out_shapes]

    scratch = [
        pltpu.VMEM((N_HEADS, TM, HEAD_DIM), BF16),
        pltpu.VMEM((S, Lk, KV_W), BF16),
        pltpu.VMEM((S, Lk, KV_W), BF16),
        pltpu.VMEM((S, L + 2 * POOL_HALO, POOL_W), F32),
        pltpu.VMEM((TM, D_MODEL), BF16),
        pltpu.VMEM((blocks_per_step, ROW_BLOCK, ATTN_W), BF16),
        pltpu.VMEM((2, step_rows, D_MODEL), F32),
        pltpu.VMEM((2, 2, D_MODEL), F32),
        pltpu.VMEM((T, LANES), BF16),
    ] + [pltpu.VMEM(weights[i].shape, weights[i].dtype) for i in late] + [pltpu.SemaphoreType.DMA((len(late),))]
    kern = functools.partial(_mix_kernel, S=S, L=L, P=P, use_rope=use_rope, emit_kv=emit_kv,
                             n_cast=tuple(len(ws) for ws in cast), n_blocks=n_blocks, U=blocks_per_step,
                             mod_row=mod_row, tile=tile)
    return pl.pallas_call(
        kern,
        grid=(n_mix_steps + 1,),
        in_specs=in_specs,
        out_specs=out_specs,
        out_shape=out_shape,
        scratch_shapes=scratch,
        compiler_params=pltpu.CompilerParams(
            dimension_semantics=("arbitrary",), vmem_limit_bytes=V7X_VMEM_LIMIT_BYTES),
        name="mixer_rope" if use_rope else "mixer_ctx",
    )(*args)


def _plan_rows(oh_ref, dest_ref, meta_ref, *, n_blocks, tile):
    TB = TOKEN_BLOCK
    lane = lax.broadcasted_iota(I32, (SUBLANES, LANES), 1)

    def count(b, acc):
        oh = oh_ref[pl.ds(pl.multiple_of(b * TB, TB), TB), :].astype(F32)
        return acc + jnp.sum(oh, axis=0, keepdims=True)

    counts = lax.fori_loop(0, n_blocks, count, jnp.zeros((SUBLANES, LANES), F32))
    padded = jnp.floor((counts + (tile - 0.5)) * (1.0 / tile)) * tile
    ends = padded
    step = 1
    while step < LANES:
        ends = ends + jnp.where(lane >= step, pltpu.roll(ends, step, 1), 0.0)
        step *= 2
    starts = ends - padded

    tri = jnp.where(lax.broadcasted_iota(I32, (TB, TB), 1) < lax.broadcasted_iota(I32, (TB, TB), 0),
                    1.0, 0.0).astype(BF16)

    def place(b, seen):
        oh = oh_ref[pl.ds(pl.multiple_of(b * TB, TB), TB), :]
        ohf = oh.astype(F32)
        rank = jnp.dot(tri, oh, preferred_element_type=F32)
        base = (starts + seen)[0:1, :]
        d = jnp.sum(ohf * (rank + base), axis=1, keepdims=True)
        dest_ref[b] = _row(d).astype(I32)
        return seen + jnp.sum(ohf, axis=0, keepdims=True)

    lax.fori_loop(0, n_blocks, place, jnp.zeros((SUBLANES, LANES), F32))

    tile_row0 = lax.broadcasted_iota(I32, (LANES, LANES), 0).astype(F32) * tile
    is_bucket = lax.broadcasted_iota(I32, (LANES, LANES), 1) < N_BUCKETS
    done = jnp.sum(jnp.where(is_bucket, jnp.where(ends[0:1, :] <= tile_row0, 1.0, 0.0), 0.0),
                   axis=1, keepdims=True)
    bkt = jnp.minimum(done, N_BUCKETS - 1.0)
    grp = (jnp.where(bkt >= PAIRS_PER_GROUP, 1.0, 0.0) + jnp.where(bkt >= 2 * PAIRS_PER_GROUP, 1.0, 0.0)
           + jnp.where(bkt >= 3 * PAIRS_PER_GROUP, 1.0, 0.0))
    pair = bkt - PAIRS_PER_GROUP * grp
    a = jnp.where(pair >= 3.0, 1.0, 0.0) + jnp.where(pair >= 5.0, 1.0, 0.0)
    b = pair - a * (7.0 - a) * 0.5 + a + 1.0
    e1 = EXP_PER_GROUP * grp + a
    e2 = EXP_PER_GROUP * grp + b
    meta = jnp.concatenate(
        [_row(e1), _row(e2), jnp.floor(ends[0:1, :] * (1.0 / tile) + 0.5),
         jnp.zeros((SUBLANES - 3, LANES), F32)], axis=0)
    meta_ref[...] = meta.astype(I32)


def _sc_move_rows(src_v, table_hbm, out_hbm, lo, n_rows, idx_v, pieces_v, sem):
    chunks = pieces_v.shape[0] // SC_ROWS_PER_STEP
    lane = lax.iota(I32, SC_LANES)
    row_in_group = lane & (SUBLANES - 1)
    chunk_in_pair = lane >> 3
    rows_per_gather = SC_PIECES_PER_GATHER // chunks

    @pl.loop(0, n_rows // SC_ROWS_PER_STEP)
    def _(step):
        copies = []
        for g in range(SC_ROWS_PER_STEP // rows_per_gather):
            r0 = step * SC_ROWS_PER_STEP + g * rows_per_gather
            for v in range(SC_PIECES_PER_GATHER // SC_LANES):
                group, chunk0 = v // (chunks // 2), 2 * (v % (chunks // 2))
                tok = plsc.load_gather(src_v, [r0 + group * SUBLANES + row_in_group])
                piece = (tok >> 3) * (SUBLANES * chunks) + (chunk0 + chunk_in_pair) * SUBLANES + (tok & 7)
                idx_v[pl.ds(g * SC_PIECES_PER_GATHER + v * SC_LANES, SC_LANES)] = piece
            window = pl.ds(g * SC_PIECES_PER_GATHER, SC_PIECES_PER_GATHER)
            copies.append(pltpu.async_copy(table_hbm.at[idx_v.at[window]], pieces_v.at[window], sem))
        for cp in copies:
            cp.wait()
        first = pl.multiple_of((lo + step * SC_ROWS_PER_STEP) * chunks, SC_ROWS_PER_STEP * chunks)
        pltpu.sync_copy(pieces_v, out_hbm.at[pl.ds(first, SC_ROWS_PER_STEP * chunks)])


def _sc_scratch(chunks, dtype):
    return [pltpu.VMEM((SC_ROWS_PER_STEP * chunks,), I32), pltpu.VMEM((SC_ROWS_PER_STEP * chunks, LANES), dtype)]


def _sc_dispatch(h2_flat, gate_rows, dest, n_rows):
    T = dest.shape[0]
    per_worker = n_rows // SC_WORKERS
    rows_per_step = SC_ROWS_PER_STEP
    chunks = h2_flat.shape[0] // T
    assert n_rows % SC_WORKERS == 0 and per_worker % rows_per_step == 0 and T % SC_LANES == 0
    mesh = plsc.VectorSubcoreMesh(core_axis_name="c", subcore_axis_name="s")

    @functools.partial(
        pl.kernel, mesh=mesh,
        out_type=[jax.ShapeDtypeStruct((n_rows * chunks, LANES), h2_flat.dtype),
                  jax.ShapeDtypeStruct((n_rows, LANES), F32)],
        scratch_types=[pltpu.VMEM((T,), I32), pltpu.VMEM((per_worker,), I32)]
        + _sc_scratch(chunks, h2_flat.dtype)
        + [pltpu.VMEM((rows_per_step, LANES), F32), pltpu.SemaphoreType.DMA, pltpu.SemaphoreType.DMA],
        compiler_params=pltpu.CompilerParams(use_tc_tiling_on_sc=True, needs_layout_passes=False),
        name="sc_dispatch",
    )
    def dispatch(h2_hbm, gate_hbm, dest_hbm, out_h_hbm, out_g_hbm,
                 dest_v, src_v, idx_v, pieces_v, gates_v, sem_h, sem_g):
        worker = lax.axis_index("s") * SC_CORES + lax.axis_index("c")
        lo = worker * per_worker
        pltpu.sync_copy(dest_hbm, dest_v)

        @pl.loop(0, per_worker // SC_LANES)
        def _(j):
            j0 = pl.multiple_of(j * SC_LANES, SC_LANES)
            src_v[pl.ds(j0, SC_LANES)] = lax.rem(lo + j0 + lax.iota(I32, SC_LANES), T)

        @pl.loop(0, T // SC_LANES)
        def _(j):
            t0 = pl.multiple_of(j * SC_LANES, SC_LANES)
            d = dest_v[pl.ds(t0, SC_LANES)] - lo
            mine = (d >= 0) & (d < per_worker)
            plsc.store_scatter(src_v, [jnp.where(mine, d, 0)], t0 + lax.iota(I32, SC_LANES), mask=mine)

        @pl.loop(0, per_worker // rows_per_step)
        def _(j):
            off = pl.multiple_of(j * rows_per_step, rows_per_step)
            pltpu.async_copy(gate_hbm.at[src_v.at[pl.ds(off, rows_per_step)]], gates_v, sem_g).wait()
            pltpu.sync_copy(gates_v, out_g_hbm.at[pl.ds(lo + off, rows_per_step)])

        _sc_move_rows(src_v, h2_hbm, out_h_hbm, lo, per_worker, idx_v, pieces_v, sem_h)

    return dispatch(h2_flat, gate_rows, dest)


def _expert_kernel(meta, x_ref, gv_ref, wgu_hbm, wd_hbm, o_ref, wgu_ref, wd_ref, ready_s, sems, *, tile, per_step):
    groups = tile // SUBLANES
    n_used = meta[2, LANES - 1]
    step = pl.program_id(0)

    def weight_copies(g):
        experts = pl.ds(g * EXP_PER_GROUP, EXP_PER_GROUP)
        return (pltpu.make_async_copy(wgu_hbm.at[experts], wgu_ref.at[experts], sems.at[g]),
                pltpu.make_async_copy(wd_hbm.at[experts], wd_ref.at[experts], sems.at[g]))

    def land_through(last_group):
        landed = ready_s[0]
        for g in range(N_EXP_GROUPS):
            @pl.when((g >= landed) & (g <= last_group))
            def _():
                for cp in weight_copies(g):
                    cp.wait()
                if g + 1 < N_EXP_GROUPS:
                    for cp in weight_copies(g + 1):
                        cp.start()
        ready_s[0] = jnp.maximum(landed, last_group + 1)

    @pl.when(step == 0)
    def _():
        ready_s[0] = 0
        for cp in weight_copies(0):
            cp.start()

    def one_tile(k, carry):
        t = step * per_step + k
        rows = pl.ds(pl.multiple_of(k * groups, groups), groups)

        @pl.when(t < n_used)
        def _():
            land_through(meta[0, t] // EXP_PER_GROUP)
            x = _unpack_bf16_pairs(_load_tiles(x_ref.at[rows]))
            gv = gv_ref[pl.ds(pl.multiple_of(k * tile, SUBLANES), tile), :]
            lane = lax.broadcasted_iota(I32, gv.shape, 1)
            out = None
            for e in (meta[0, t], meta[1, t]):
                ge = jnp.sum(jnp.where(lane == EXPERT_LANE0 + e, gv, 0.0), axis=-1, keepdims=True)
                h = jnp.dot(x, wgu_ref[e], preferred_element_type=F32)
                hg = h[:, 0:D_EXPERT]
                hid = (hg * _sigmoid(hg) * h[:, D_EXPERT:2 * D_EXPERT] * ge).astype(BF16)
                y = jnp.dot(hid, wd_ref[e], preferred_element_type=F32)
                out = y if out is None else out + y
            _store_tiles(o_ref.at[rows], _pack_bf16_pairs(out))

        return carry

    lax.fori_loop(0, per_step, one_tile, 0)

    @pl.when(step == pl.num_programs(0) - 1)
    def _():
        land_through(N_EXP_GROUPS - 1)


def _experts(sorted_h2, sorted_gates, meta, wgu, wd, tiling):
    tile, per_step, n_tiles = tiling
    step_rows = tile * per_step
    assert n_tiles * tile == sorted_h2.shape[0] * SUBLANES and n_tiles % per_step == 0

    def last_used(i, meta):
        return jnp.minimum(i, (meta[2, LANES - 1] - 1) // per_step)

    return pl.pallas_call(
        functools.partial(_expert_kernel, tile=tile, per_step=per_step),
        grid_spec=pltpu.PrefetchScalarGridSpec(
            num_scalar_prefetch=1,
            grid=(n_tiles // per_step,),
            in_specs=[
                _tiles_spec(step_rows, last_used, PACKED_CHUNKS),
                pl.BlockSpec((step_rows, LANES), lambda *a: (last_used(*a), 0)),
                pl.BlockSpec(memory_space=pl.ANY), pl.BlockSpec(memory_space=pl.ANY),
            ],
            out_specs=_tiles_spec(step_rows, last_used, PACKED_CHUNKS),
            scratch_shapes=[pltpu.VMEM(wgu.shape, wgu.dtype), pltpu.VMEM(wd.shape, wd.dtype),
                            pltpu.SMEM((1,), I32), pltpu.SemaphoreType.DMA((N_EXP_GROUPS,))],
        ),
        out_shape=jax.ShapeDtypeStruct(_tiles_shape(n_tiles * tile, PACKED_CHUNKS), U32),
        compiler_params=pltpu.CompilerParams(
            dimension_semantics=("arbitrary",), vmem_limit_bytes=V7X_VMEM_LIMIT_BYTES),
        name="moe_experts",
    )(meta, sorted_h2, sorted_gates, wgu, wd)


def _sc_row_gather(table_flat, idx, chunks):
    n = idx.shape[0]
    per_worker = n // SC_WORKERS
    assert n % SC_WORKERS == 0 and per_worker % SC_ROWS_PER_STEP == 0
    mesh = plsc.VectorSubcoreMesh(core_axis_name="c", subcore_axis_name="s")

    @functools.partial(
        pl.kernel, mesh=mesh,
        out_type=jax.ShapeDtypeStruct((n * chunks, LANES), table_flat.dtype),
        scratch_types=[pltpu.VMEM((per_worker,), I32)] + _sc_scratch(chunks, table_flat.dtype)
        + [pltpu.SemaphoreType.DMA],
        compiler_params=pltpu.CompilerParams(use_tc_tiling_on_sc=True, needs_layout_passes=False),
        name="sc_row_gather",
    )
    def gather(table_hbm, idx_hbm, out_hbm, src_v, idx_v, pieces_v, sem):
        worker = lax.axis_index("s") * SC_CORES + lax.axis_index("c")
        lo = worker * per_worker
        pltpu.sync_copy(idx_hbm.at[pl.ds(lo, per_worker)], src_v)
        _sc_move_rows(src_v, table_hbm, out_hbm, lo, per_worker, idx_v, pieces_v, sem)

    return gather(table_flat, idx)


def _final_kernel(x_ref, moe_ref, gt2_ref, gf_ref, o_ref, *, mod_row):
    gt2 = gt2_ref[pl.ds(mod_row(pl.program_id(0)), 1), :]
    y = x_ref[...] + gt2 * _unpack_bf16_pairs(_load_tiles(moe_ref)).astype(F32)
    o_ref[...] = _rms(y) * gf_ref[...]


def _final(xmid, moe_rows, mod, mod_row, gf):
    T = xmid.shape[0]
    return pl.pallas_call(
        functools.partial(_final_kernel, mod_row=mod_row),
        grid=(T // FINAL_BLOCK,),
        in_specs=[
            pl.BlockSpec((FINAL_BLOCK, D_MODEL), lambda i: (i, 0)),
            _tiles_spec(FINAL_BLOCK, lambda i: i, PACKED_CHUNKS),
            pl.BlockSpec((COND_ROWS, D_MODEL), lambda i: (0, 5)),
            pl.BlockSpec((1, D_MODEL), lambda i: (0, 0)),
        ],
        out_specs=pl.BlockSpec((FINAL_BLOCK, D_MODEL), lambda i: (i, 0)),
        out_shape=jax.ShapeDtypeStruct((T, D_MODEL), F32),
        compiler_params=pltpu.CompilerParams(
            dimension_semantics=("arbitrary",), vmem_limit_bytes=V7X_VMEM_LIMIT_BYTES),
        name="moe_final",
    )(xmid, moe_rows, mod, gf)


def _flat(tiles):
    return tiles.reshape(-1, LANES)


def _expert_tiling(T):
    tile = -(-(T * 9) // (8 * N_BUCKETS * 64)) * 64
    per_step = max(1, EXPERT_STEP_ROWS // tile)
    n_tiles = (T + N_BUCKETS * (tile - 1)) // tile
    while n_tiles % per_step or (n_tiles * tile) % (SC_WORKERS * SC_ROWS_PER_STEP):
        n_tiles += 1
    return tile, per_step, n_tiles


def _moe_dispatch(h2_tiles, gate_rows, dest, tiling):
    tile, _, n_tiles = tiling
    n_rows = n_tiles * tile
    assert n_tiles <= LANES
    sorted_h2, sorted_gates = _sc_dispatch(_flat(h2_tiles), gate_rows, dest, n_rows)
    return sorted_h2.reshape(_tiles_shape(n_rows, PACKED_CHUNKS)), sorted_gates


def _moe_unpermute(moe_sorted_tiles, dest):
    chunks = moe_sorted_tiles.shape[1]
    return _sc_row_gather(_flat(moe_sorted_tiles), dest, chunks).reshape(_tiles_shape(dest.shape[0], chunks))


def _rope_tables(n_tokens):
    t = np.arange(n_tokens)
    row = (t // GRID_W).astype(np.float32)
    col = (t % GRID_W).astype(np.float32)
    freq = np.float32(ROPE_THETA) ** (-np.arange(ROPE_NF, dtype=np.float32) / np.float32(ROPE_NF))
    ang = np.concatenate([row[:, None] * freq] * 2 + [col[:, None] * freq] * 2, axis=-1)
    first = (np.arange(HEAD_DIM) % (2 * ROPE_NF)) < ROPE_NF
    sin = np.sin(ang)
    zero = np.float32(0.0)
    return (jnp.asarray(np.cos(ang)), jnp.asarray(np.where(first, -sin, zero)),
            jnp.asarray(np.where(first, zero, sin)))


def kernel(x_prompt, x_sample, cache_k, cache_v, c, c_ctx, norm1_g, norm2_g, w_ada, b_ada, w_in, q_norm_g, k_norm_g, w_pool, pool_scale, w_branch_a, w_branch_b, w_out, w_router_group, w_router_expert, w_exp_gate, w_exp_up, w_exp_down, final_norm_g):
    assert norm1_g.shape[0] == 1, "single-layer trunk"
    B, L_ctx, _ = x_prompt.shape
    Bs, L_lat, _ = x_sample.shape
    P = cache_k.shape[2]
    assert 1 + Bs <= COND_ROWS

    cond = jnp.concatenate([c_ctx[None, :], c, jnp.zeros((COND_ROWS - 1 - Bs, D_MODEL), F32)], axis=0)
    mod, w_in_b, wa_b, wb_b, wo_b, wpool_b = _ada(
        cond, w_ada[0], b_ada[0][None, :], w_pool[0],
        cast=(w_in[0], w_branch_a[0], w_branch_b[0], w_out[0]))

    wr = jnp.concatenate([w_router_group[0], w_router_expert[0],
                          jnp.zeros((D_MODEL, LANES - N_EXP_GROUPS - N_EXPERTS), F32)], axis=1)
    wr_hi = wr.astype(BF16)
    wr_lo = (wr - wr_hi.astype(F32)).astype(BF16)
    mix_w = (norm1_g[0][None, :], w_in_b, q_norm_g[0][None, :], k_norm_g[0][None, :],
             wpool_b, pool_scale[0][None, :], wa_b, wb_b, wo_b,
             norm2_g[0][None, :], jnp.concatenate([wr_hi, wr_lo], axis=1))
    gf = final_norm_g[None, :]

    xp2 = x_prompt.reshape(B * L_ctx, D_MODEL)
    tiling_p = _expert_tiling(B * L_ctx)
    xmid_p, h2_p, gate_p, dest_p, meta_p, knew, vnew, wgu, wd = _mix(
        xp2, mod, lambda i: 0, None, None, mix_w, S=2, L=L_ctx, emit_kv=True, blocks_per_step=2,
        tile=tiling_p[0], cast=((w_exp_gate[0], w_exp_up[0]), (w_exp_down[0],)))
    dest_p = dest_p.reshape(B * L_ctx)
    sh_p, sg_p = _moe_dispatch(h2_p, gate_p, dest_p, tiling_p)

    xs2 = x_sample.reshape(Bs * L_lat, D_MODEL)
    cache = (cache_k.reshape(Bs * P * N_KV_HEADS, HEAD_DIM), cache_v.reshape(Bs * P * N_KV_HEADS, HEAD_DIM))
    tiling_s = _expert_tiling(Bs * L_lat)
    xmid_s, h2_s, gate_s, dest_s, meta_s = _mix(
        xs2, mod, lambda i: 1 + i, cache, _rope_tables(L_lat), mix_w,
        S=1, L=L_lat, emit_kv=False, blocks_per_step=1, tile=tiling_s[0])
    dest_s = dest_s.reshape(Bs * L_lat)
    sh_s, sg_s = _moe_dispatch(h2_s, gate_s, dest_s, tiling_s)

    moe_p = _moe_unpermute(_experts(sh_p, sg_p, meta_p, wgu, wd, tiling_p), dest_p)
    moe_s = _moe_unpermute(_experts(sh_s, sg_s, meta_s, wgu, wd, tiling_s), dest_s)
    y_prompt = _final(xmid_p, moe_p, mod, lambda i: 0, gf)
    blocks_per_seq = L_lat // FINAL_BLOCK
    y_sample = _final(xmid_s, moe_s, mod, lambda i: 1 + i // blocks_per_seq, gf)

    return (y_prompt.reshape(B, L_ctx, D_MODEL), y_sample.reshape(Bs, L_lat, D_MODEL),
            knew.reshape(B, 1, L_ctx, N_KV_HEADS, HEAD_DIM), vnew.reshape(B, 1, L_ctx, N_KV_HEADS, HEAD_DIM))
```

```python
import functools

import numpy as np
import jax
import jax.numpy as jnp
from jax import lax
from jax.experimental import pallas as pl
from jax.experimental.pallas import tpu as pltpu
from jax.experimental.pallas import tpu_sc as plsc

F32 = jnp.float32
BF16 = jnp.bfloat16
I32 = jnp.int32
U32 = jnp.uint32

D_MODEL = 1024
HEAD_DIM = 128
N_HEADS = 8
N_KV_HEADS = 2
GROUP = N_HEADS // N_KV_HEADS
ATTN_W = N_HEADS * HEAD_DIM
KV_W = N_KV_HEADS * HEAD_DIM
POOL_WINDOWS = (2, 4, 8, 16)
POOL_GC = 128
POOL_W = POOL_GC * len(POOL_WINDOWS)
IN_W = ATTN_W + 2 * KV_W + POOL_W + 2 * D_MODEL
GATE_COL = ATTN_W + 2 * KV_W + POOL_W
GRID_W = 64
ROPE_THETA = 10000.0
ROPE_NF = HEAD_DIM // 4
N_EXP_GROUPS = 4
EXP_PER_GROUP = 4
N_EXPERTS = 16
D_EXPERT = 256
EPS = 1e-6
LOG2_E = 1.4426950408889634

LANES = 128
SUBLANES = 8
COND_ROWS = SUBLANES
POOL_HALO = 8
ROW_BLOCK = 256
ADA_COLS = 768
EXPERT_LANE0 = N_EXP_GROUPS
PAIRS_PER_GROUP = EXP_PER_GROUP * (EXP_PER_GROUP - 1) // 2
N_BUCKETS = N_EXP_GROUPS * PAIRS_PER_GROUP
EXPERT_STEP_ROWS = 1536
TOKEN_BLOCK = 1024
FINAL_BLOCK = 1024
ROW_CHUNKS = D_MODEL // LANES
SC_CORES = 2
SC_SUBCORES = 16
SC_WORKERS = SC_CORES * SC_SUBCORES
SC_LANES = 16
SC_PIECES_PER_GATHER = 128
SC_ROWS_PER_STEP = 64
PACKED_CHUNKS = ROW_CHUNKS // 2
V7X_VMEM_LIMIT_BYTES = 56 * 1024 * 1024


def _sigmoid(x):
    return 1.0 / (1.0 + jnp.exp(-x))


def _rms(x):
    return x * lax.rsqrt(jnp.mean(x * x, axis=-1, keepdims=True) + EPS)


def _resident(shape):
    zeros = (0,) * len(shape)
    return pl.BlockSpec(shape, lambda i, *_: zeros, pipeline_mode=pl.Buffered(1))


def _tiles_shape(n, chunks=ROW_CHUNKS):
    return (n // SUBLANES, chunks, SUBLANES, LANES)


def _tiles_spec(n, block_index, chunks=ROW_CHUNKS):
    return pl.BlockSpec(_tiles_shape(n, chunks), lambda *a: (block_index(*a), 0, 0, 0))


def _store_tiles(ref, x):
    for c in range(ref.shape[1]):
        ref[:, c, :, :] = x[:, c * LANES:(c + 1) * LANES].reshape(x.shape[0] // SUBLANES, SUBLANES, LANES)


def _load_tiles(ref):
    n = ref.shape[0] * SUBLANES
    return jnp.concatenate([ref[:, c, :, :].reshape(n, LANES) for c in range(ref.shape[1])], axis=1)


def _pack_bf16_pairs(x):
    bits = pltpu.bitcast(x.astype(BF16).astype(F32), U32)
    w = x.shape[1] // 2
    return bits[:, :w] | (bits[:, w:] >> 16)


def _unpack_bf16_pairs(words):
    hi = pltpu.bitcast(words & jnp.uint32(0xFFFF0000), F32).astype(BF16)
    lo = pltpu.bitcast(words << 16, F32).astype(BF16)
    return jnp.concatenate([hi, lo], axis=1)


def _row(x):
    return jnp.transpose(jnp.broadcast_to(x, (x.shape[0], LANES)))[0:1, :]


def _ada_kernel(c_ref, w_ref, b_ref, *refs, steps_per_pool_group):
    n_cast = len(refs) // 2 - 1
    c = c_ref[...]
    s = (c * _sigmoid(c)).astype(BF16)
    refs[n_cast + 1][...] = jnp.dot(s, w_ref[...].astype(BF16), preferred_element_type=F32) + b_ref[...]
    for src, dst in zip(refs[:n_cast], refs[n_cast + 2:]):
        dst[...] = src[...].astype(BF16)
    pool_src, pool_dst = refs[n_cast], refs[-1]
    wide = jnp.concatenate([pool_src[0]] * len(POOL_WINDOWS), axis=1)
    lane_group = lax.broadcasted_iota(I32, wide.shape, 1) // POOL_GC
    pool_dst[...] = jnp.where(lane_group == pl.program_id(0) // steps_per_pool_group, wide, 0.0).astype(BF16)


def _ada(cond, w_ada, b_ada, w_pool, cast=()):
    n = w_ada.shape[1]
    n_steps = n // ADA_COLS
    cast_specs = []
    for w in cast:
        assert w.ndim == 2 and w.shape[0] % (n_steps * 2 * SUBLANES) == 0
        cast_specs.append(pl.BlockSpec((w.shape[0] // n_steps, w.shape[1]), lambda j: (j, 0)))
    pool_rows = POOL_W // n_steps
    spg = POOL_GC // pool_rows
    assert w_pool.shape == (len(POOL_WINDOWS), POOL_GC, POOL_GC) and POOL_GC % pool_rows == 0
    assert pool_rows % (2 * SUBLANES) == 0
    pool_in = pl.BlockSpec((1, pool_rows, POOL_GC), lambda j: (j // spg, j % spg, 0))
    pool_out = pl.BlockSpec((pool_rows, POOL_W), lambda j: (j, 0))
    return pl.pallas_call(
        functools.partial(_ada_kernel, steps_per_pool_group=spg),
        grid=(n_steps,),
        in_specs=[
            pl.BlockSpec((COND_ROWS, D_MODEL), lambda j: (0, 0)),
            pl.BlockSpec((D_MODEL, ADA_COLS), lambda j: (0, j)),
            pl.BlockSpec((1, ADA_COLS), lambda j: (0, j)),
        ] + cast_specs + [pool_in],
        out_specs=[pl.BlockSpec((COND_ROWS, ADA_COLS), lambda j: (0, j))] + cast_specs + [pool_out],
        out_shape=[jax.ShapeDtypeStruct((COND_ROWS, n), F32)] + [jax.ShapeDtypeStruct(w.shape, BF16) for w in cast]
        + [jax.ShapeDtypeStruct((POOL_W, POOL_W), BF16)],
        name="ada_mod",
    )(cond, w_ada, b_ada, *cast, w_pool)


def _route(logits):
    lane = lax.broadcasted_iota(I32, logits.shape, 1).astype(F32)
    neg = jnp.float32(-1e30)
    far = jnp.float32(LANES)
    is_g = lane < N_EXP_GROUPS
    gl = jnp.where(is_g, logits, neg)
    gmax = jnp.max(gl, axis=-1, keepdims=True)
    gsel = jnp.min(jnp.where(gl == gmax, lane, far), axis=-1, keepdims=True)
    psel = 1.0 / jnp.sum(jnp.where(is_g, jnp.exp(gl - gmax), 0.0), axis=-1, keepdims=True)
    e_lo = EXPERT_LANE0 + EXP_PER_GROUP * gsel
    el = jnp.where(lane >= e_lo, jnp.where(lane < e_lo + EXP_PER_GROUP, logits, neg), neg)
    v1 = jnp.max(el, axis=-1, keepdims=True)
    i1 = jnp.min(jnp.where(el == v1, lane, far), axis=-1, keepdims=True)
    el2 = jnp.where(lane == i1, neg, el)
    v2 = jnp.max(el2, axis=-1, keepdims=True)
    i2 = jnp.min(jnp.where(el2 == v2, jnp.where(lane == i1, far, lane), far), axis=-1, keepdims=True)
    e2 = jnp.exp(v2 - v1)
    w1 = psel / (1.0 + e2)
    w2 = psel * e2 / (1.0 + e2)
    gate = jnp.where(lane == i1, w1, jnp.where(lane == i2, w2, 0.0))
    a = jnp.minimum(i1, i2) - e_lo
    b = jnp.maximum(i1, i2) - e_lo
    pair = a * (7.0 - a) * 0.5 + (b - a - 1.0)
    return gate, gsel * PAIRS_PER_GROUP + pair


def _mix_kernel(*refs, S, L, P, use_rope, emit_kv, n_cast, n_blocks, U, mod_row, tile):
    it = iter(refs)
    x_ref = next(it)
    mod_ref = next(it)
    if P:
        ck_ref = next(it)
        cv_ref = next(it)
    if use_rope:
        cos_ref = next(it)
        sneg_ref = next(it)
        spos_ref = next(it)
    (g1_ref, win_ref, qg_ref, kg_ref, wpool_hbm, pscale_ref, wa_hbm, wb_hbm, wo_hbm,
     g2_ref, wr_ref) = (next(it) for _ in range(11))
    cast_in = [[next(it) for _ in range(n)] for n in n_cast]
    xmid_ref = next(it)
    h2_ref = next(it)
    gate_ref = next(it)
    dest_ref = next(it)
    meta_ref = next(it)
    if emit_kv:
        knew_ref = next(it)
        vnew_ref = next(it)
    cast_out = [next(it) for _ in n_cast]
    q_s, k_s, v_s, xp_s, h_s, attn_s, xm_s, mod2_s, oh_s = (next(it) for _ in range(9))
    wpool_ref, wa_ref, wb_ref, wo_ref, late_sems = (next(it) for _ in range(5))
    late_copies = [pltpu.make_async_copy(src, dst, late_sems.at[i]) for i, (src, dst) in enumerate(
        ((wa_hbm, wa_ref), (wpool_hbm, wpool_ref), (wb_hbm, wb_ref), (wo_hbm, wo_ref)))]

    TM = S * L
    RB = ROW_BLOCK
    nrb = TM // RB
    n_steps = n_blocks // U
    score_gain = HEAD_DIM ** -0.5 * LOG2_E
    step = pl.program_id(0)
    block0 = U * jnp.minimum(step, n_steps - 1)
    slot = step % 2

    mod_at = pl.ds(mod_row(jnp.minimum(step, n_steps - 1) // (nrb // U)), 1)
    sh1 = mod_ref[mod_at, 0:D_MODEL]
    gain1 = g1_ref[...] * (1.0 + mod_ref[mod_at, D_MODEL:2 * D_MODEL])
    gt1 = mod_ref[mod_at, 2 * D_MODEL:3 * D_MODEL]
    sh2 = mod_ref[mod_at, 3 * D_MODEL:4 * D_MODEL]
    gain2 = g2_ref[...] * (1.0 + mod_ref[mod_at, 4 * D_MODEL:5 * D_MODEL])
    qg = qg_ref[...] * score_gain
    kg = kg_ref[...]

    def project(r, carry):
        r0 = pl.multiple_of(r * RB, RB)
        s = r0 // L
        o = pl.multiple_of(r0 % L, RB)
        hb = (_rms(x_ref[pl.ds(r0, RB), :]) * gain1 + sh1).astype(BF16)
        h_s[pl.ds(r0, RB), :] = hb
        p1 = jnp.dot(hb, win_ref[:, 0:GATE_COL], preferred_element_type=F32)
        if use_rope:
            cs = cos_ref[pl.ds(o, RB), :]
            sn = sneg_ref[pl.ds(o, RB), :]
            sp = spos_ref[pl.ds(o, RB), :]

        def rope(t):
            return (t * cs + pltpu.roll(t, HEAD_DIM - ROPE_NF, 1) * sn + pltpu.roll(t, ROPE_NF, 1) * sp)

        for hd in range(N_HEADS):
            qh = _rms(p1[:, hd * HEAD_DIM:(hd + 1) * HEAD_DIM]) * qg
            if use_rope:
                qh = rope(qh)
            q_s[hd, pl.ds(r0, RB), :] = qh.astype(BF16)
        for kh in range(N_KV_HEADS):
            c0 = ATTN_W + kh * HEAD_DIM
            kk = _rms(p1[:, c0:c0 + HEAD_DIM]) * kg
            if emit_kv:
                knew_ref[pl.ds(N_KV_HEADS * r0 + kh, RB, stride=N_KV_HEADS), :] = kk
            if use_rope:
                kk = rope(kk)
            k_s[s, pl.ds(P + o, RB), kh * HEAD_DIM:(kh + 1) * HEAD_DIM] = kk.astype(BF16)
        vv = p1[:, ATTN_W + KV_W:ATTN_W + 2 * KV_W]
        if emit_kv:
            for kh in range(N_KV_HEADS):
                vnew_ref[pl.ds(N_KV_HEADS * r0 + kh, RB, stride=N_KV_HEADS), :] = (
                    vv[:, kh * HEAD_DIM:(kh + 1) * HEAD_DIM])
        v_s[s, pl.ds(P + o, RB), :] = vv.astype(BF16)
        xp_s[s, pl.ds(POOL_HALO + o, RB), :] = p1[:, ATTN_W + 2 * KV_W:GATE_COL]
        return carry

    @pl.when(step == 0)
    def _():
        xm_s[1] = jnp.zeros((U * RB, D_MODEL), F32)
        mod2_s[1] = jnp.zeros((2, D_MODEL), F32)
        for cp in late_copies:
            cp.start()

    @pl.when((step < n_steps) & (step % (nrb // U) == 0))
    def _():
        if P:
            for kh in range(N_KV_HEADS):
                cols = slice(kh * HEAD_DIM, (kh + 1) * HEAD_DIM)
                k_s[0, 0:P, cols] = ck_ref[pl.ds(kh, P, stride=N_KV_HEADS), :].astype(BF16)
                v_s[0, 0:P, cols] = cv_ref[pl.ds(kh, P, stride=N_KV_HEADS), :].astype(BF16)
        xp_s[:, 0:POOL_HALO, :] = jnp.zeros((S, POOL_HALO, POOL_W), F32)
        xp_s[:, L + POOL_HALO:L + 2 * POOL_HALO, :] = jnp.zeros((S, POOL_HALO, POOL_W), F32)
        lax.fori_loop(0, TM // RB, project, 0)
        for srcs, dst in zip(cast_in, cast_out):
            col = 0
            for src in srcs:
                dst[..., col:col + src.shape[-1]] = src[...].astype(BF16)
                col += src.shape[-1]

    @pl.when(step == 0)
    def _():
        for cp in late_copies:
            cp.wait()

    def mix(u):
        r0 = pl.multiple_of(((block0 + u) % nrb) * RB, RB)
        s = r0 // L
        o = pl.multiple_of(r0 % L, RB)
        attn_u = attn_s.at[u]
        rows = slice(u * RB, (u + 1) * RB)

        for hd in range(N_HEADS):
            kh = hd // GROUP
            k = k_s[s, :, kh * HEAD_DIM:(kh + 1) * HEAD_DIM]
            v = v_s[s, :, kh * HEAD_DIM:(kh + 1) * HEAD_DIM]
            qh = q_s[hd, pl.ds(r0, RB), :]
            sc = lax.dot_general(qh, k, (((1,), (1,)), ((), ())), preferred_element_type=F32)
            e = jnp.exp2(sc - jnp.max(sc, axis=-1, keepdims=True))
            den = jnp.sum(e, axis=-1, keepdims=True)
            oh = jnp.dot(e.astype(BF16), v, preferred_element_type=F32) / den
            attn_u[:, hd * HEAD_DIM:(hd + 1) * HEAD_DIM] = oh.astype(BF16)
        a = jnp.dot(attn_u[...], wa_ref[...], preferred_element_type=F32)

        t = o + lax.broadcasted_iota(I32, (RB, 1), 0)
        RW = RB + 2 * POOL_HALO
        parts = []
        for gi, w in enumerate(POOL_WINDOWS):
            cols = slice(gi * POOL_GC, (gi + 1) * POOL_GC)
            xw = xp_s[s, pl.ds(o, RW), cols]
            run = xw
            span = 1
            while span < w:
                run = run + pltpu.roll(run, span, 0)
                span *= 2
            if w // 2 > 1:
                run = pltpu.roll(run, RW - (w // 2 - 1), 0)
            tot = run[POOL_HALO:POOL_HALO + RB]
            cnt = (jnp.minimum(t + w // 2, L) - jnp.maximum(t - w // 2, 0)).astype(F32)
            parts.append(tot / cnt - xw[POOL_HALO:POOL_HALO + RB])
        dpool = jnp.concatenate(parts, axis=1).astype(BF16)
        pooled = jnp.dot(dpool, wpool_ref[...], preferred_element_type=F32) * pscale_ref[...]
        b = jnp.dot(pooled.astype(BF16), wb_ref[...], preferred_element_type=F32)

        gates = jnp.dot(h_s[pl.ds(r0, RB), :], win_ref[:, GATE_COL:IN_W], preferred_element_type=F32)
        merged = _sigmoid(gates[:, 0:D_MODEL]) * a + _sigmoid(gates[:, D_MODEL:2 * D_MODEL]) * b
        upd = jnp.dot(merged.astype(BF16), wo_ref[...], preferred_element_type=F32)
        xm = x_ref[pl.ds(r0, RB), :] + gt1 * upd
        xmid_ref[rows, :] = xm
        xm_s[slot, rows, :] = xm

    def moe_prep(u):
        rows = slice(u * RB, (u + 1) * RB)
        h2 = _rms(xm_s[1 - slot, rows, :]) * mod2_s[1 - slot, 0:1, :] + mod2_s[1 - slot, 1:2, :]
        hi = h2.astype(BF16)
        lo = (h2 - hi.astype(F32)).astype(BF16)
        l1 = jnp.dot(hi, wr_ref[...], preferred_element_type=F32)
        l2 = jnp.dot(lo, wr_ref[:, 0:LANES], preferred_element_type=F32)
        gate, bucket = _route(l1[:, 0:LANES] + l1[:, LANES:2 * LANES] + l2)
        groups = pl.ds(u * (RB // SUBLANES), RB // SUBLANES)
        _store_tiles(h2_ref.at[groups], _pack_bf16_pairs(h2))
        gate_ref[rows, :] = gate
        lane = lax.broadcasted_iota(I32, (RB, LANES), 1).astype(F32)
        first = pl.multiple_of((U * jnp.maximum(step - 1, 0) + u) * RB, RB)
        oh_s[pl.ds(first, RB), :] = jnp.where(lane == bucket, 1.0, 0.0).astype(BF16)

    mod2_s[slot, 0:1, :] = gain2
    mod2_s[slot, 1:2, :] = sh2

    @pl.when(step < n_steps)
    def _():
        for u in range(U):
            moe_prep(u)
        for u in range(U):
            mix(u)

    @pl.when(step == n_steps)
    def _():
        for u in range(U):
            moe_prep(u)
        _plan_rows(oh_s, dest_ref, meta_ref, n_blocks=n_blocks * RB // TOKEN_BLOCK, tile=tile)


def _mix(x2d, mod, mod_row, cache, rope_tabs, weights, *, S, L, emit_kv, blocks_per_step, tile, cast=()):
    T = x2d.shape[0]
    TM = S * L
    P = cache[0].shape[0] // (T // L * N_KV_HEADS) if cache is not None else 0
    use_rope = rope_tabs is not None
    assert T % TM == 0 and L % ROW_BLOCK == 0
    assert not (use_rope or P) or S == 1
    Lk = P + L

    args = [x2d, mod]
    nrb = TM // ROW_BLOCK
    n_blocks = T // ROW_BLOCK
    step_rows = blocks_per_step * ROW_BLOCK
    steps_per_group = nrb // blocks_per_step
    n_mix_steps = n_blocks // blocks_per_step
    assert nrb % blocks_per_step == 0

    def mixed(s):
        return jnp.minimum(s, n_mix_steps - 1)

    def group(s):
        return mixed(s) // steps_per_group

    def prepared(s):
        return jnp.maximum(s - 1, 0)

    in_specs = [
        pl.BlockSpec((TM, D_MODEL), lambda s: (group(s), 0)),
        _resident(mod.shape),
    ]
    if P:
        args += list(cache)
        in_specs += [pl.BlockSpec((P * N_KV_HEADS, HEAD_DIM), lambda s: (group(s), 0))] * 2
    if use_rope:
        args += list(rope_tabs)
        in_specs += [_resident((L, HEAD_DIM))] * 3
    args += list(weights)
    late = (4, 6, 7, 8)
    in_specs += [pl.BlockSpec(memory_space=pl.ANY) if i in late else _resident(w.shape)
                 for i, w in enumerate(weights)]
    n_steps = T // TM
    def per_group(shape):
        assert shape[0] % n_steps == 0
        blk = (shape[0] // n_steps,) + shape[1:]
        return pl.BlockSpec(blk, lambda s, n=len(blk): (group(s),) + (0,) * (n - 1))

    cast_out_shapes = [ws[0].shape[:-1] + (sum(w.shape[-1] for w in ws),) for ws in cast]
    for ws in cast:
        args += list(ws)
        in_specs += [per_group(w.shape) for w in ws]

    assert T % TOKEN_BLOCK == 0
    out_shape = [jax.ShapeDtypeStruct((T, D_MODEL), F32), jax.ShapeDtypeStruct(_tiles_shape(T, PACKED_CHUNKS), U32),
                 jax.ShapeDtypeStruct((T, LANES), F32),
                 jax.ShapeDtypeStruct((T // TOKEN_BLOCK, 1, TOKEN_BLOCK), I32),
                 jax.ShapeDtypeStruct((SUBLANES, LANES), I32)]
    out_specs = [pl.BlockSpec((step_rows, D_MODEL), lambda s: (mixed(s), 0)),
                 _tiles_spec(step_rows, prepared, PACKED_CHUNKS),
                 pl.BlockSpec((step_rows, LANES), lambda s: (prepared(s), 0)),
                 pl.BlockSpec((T // TOKEN_BLOCK, 1, TOKEN_BLOCK), lambda s: (0, 0, 0)),
                 pl.BlockSpec((SUBLANES, LANES), lambda s: (0, 0))]
    if emit_kv:
        out_shape += [jax.ShapeDtypeStruct((T * N_KV_HEADS, HEAD_DIM), F32)] * 2
        out_specs += [pl.BlockSpec((TM * N_KV_HEADS, HEAD_DIM), lambda s: (group(s), 0))] * 2
    out_shape += [jax.ShapeDtypeStruct(shp, BF16) for shp in cast_out_shapes]
    out_specs += [per_group(shp) for shp in cast_out_shapes]

    scratch = [
        pltpu.VMEM((N_HEADS, TM, HEAD_DIM), BF16),
        pltpu.VMEM((S, Lk, KV_W), BF16),
        pltpu.VMEM((S, Lk, KV_W), BF16),
        pltpu.VMEM((S, L + 2 * POOL_HALO, POOL_W), F32),
        pltpu.VMEM((TM, D_MODEL), BF16),
        pltpu.VMEM((blocks_per_step, ROW_BLOCK, ATTN_W), BF16),
        pltpu.VMEM((2, step_rows, D_MODEL), F32),
        pltpu.VMEM((2, 2, D_MODEL), F32),
        pltpu.VMEM((T, LANES), BF16),
    ] + [pltpu.VMEM(weights[i].shape, weights[i].dtype) for i in late] + [pltpu.SemaphoreType.DMA((len(late),))]
    kern = functools.partial(_mix_kernel, S=S, L=L, P=P, use_rope=use_rope, emit_kv=emit_kv,
                             n_cast=tuple(len(ws) for ws in cast), n_blocks=n_blocks, U=blocks_per_step,
                             mod_row=mod_row, tile=tile)
    return pl.pallas_call(
        kern,
        grid=(n_mix_steps + 1,),
        in_specs=in_specs,
        out_specs=out_specs,
        out_shape=out_shape,
        scratch_shapes=scratch,
        compiler_params=pltpu.CompilerParams(
            dimension_semantics=("arbitrary",), vmem_limit_bytes=V7X_VMEM_LIMIT_BYTES),
        name="mixer_rope" if use_rope else "mixer_ctx",
    )(*args)


def _plan_rows(oh_ref, dest_ref, meta_ref, *, n_blocks, tile):
    TB = TOKEN_BLOCK
    lane = lax.broadcasted_iota(I32, (SUBLANES, LANES), 1)

    def count(b, acc):
        oh = oh_ref[pl.ds(pl.multiple_of(b * TB, TB), TB), :].astype(F32)
        return acc + jnp.sum(oh, axis=0, keepdims=True)

    counts = lax.fori_loop(0, n_blocks, count, jnp.zeros((SUBLANES, LANES), F32))
    padded = jnp.floor((counts + (tile - 0.5)) * (1.0 / tile)) * tile
    ends = padded
    step = 1
    while step < LANES:
        ends = ends + jnp.where(lane >= step, pltpu.roll(ends, step, 1), 0.0)
        step *= 2
    starts = ends - padded

    tri = jnp.where(lax.broadcasted_iota(I32, (TB, TB), 1) < lax.broadcasted_iota(I32, (TB, TB), 0),
                    1.0, 0.0).astype(BF16)

    def place(b, seen):
        oh = oh_ref[pl.ds(pl.multiple_of(b * TB, TB), TB), :]
        ohf = oh.astype(F32)
        rank = jnp.dot(tri, oh, preferred_element_type=F32)
        base = (starts + seen)[0:1, :]
        d = jnp.sum(ohf * (rank + base), axis=1, keepdims=True)
        dest_ref[b] = _row(d).astype(I32)
        return seen + jnp.sum(ohf, axis=0, keepdims=True)

    lax.fori_loop(0, n_blocks, place, jnp.zeros((SUBLANES, LANES), F32))

    tile_row0 = lax.broadcasted_iota(I32, (LANES, LANES), 0).astype(F32) * tile
    is_bucket = lax.broadcasted_iota(I32, (LANES, LANES), 1) < N_BUCKETS
    done = jnp.sum(jnp.where(is_bucket, jnp.where(ends[0:1, :] <= tile_row0, 1.0, 0.0), 0.0),
                   axis=1, keepdims=True)
    bkt = jnp.minimum(done, N_BUCKETS - 1.0)
    grp = (jnp.where(bkt >= PAIRS_PER_GROUP, 1.0, 0.0) + jnp.where(bkt >= 2 * PAIRS_PER_GROUP, 1.0, 0.0)
           + jnp.where(bkt >= 3 * PAIRS_PER_GROUP, 1.0, 0.0))
    pair = bkt - PAIRS_PER_GROUP * grp
    a = jnp.where(pair >= 3.0, 1.0, 0.0) + jnp.where(pair >= 5.0, 1.0, 0.0)
    b = pair - a * (7.0 - a) * 0.5 + a + 1.0
    e1 = EXP_PER_GROUP * grp + a
    e2 = EXP_PER_GROUP * grp + b
    meta = jnp.concatenate(
        [_row(e1), _row(e2), jnp.floor(ends[0:1, :] * (1.0 / tile) + 0.5),
         jnp.zeros((SUBLANES - 3, LANES), F32)], axis=0)
    meta_ref[...] = meta.astype(I32)


def _sc_move_rows(src_v, table_hbm, out_hbm, lo, n_rows, idx_v, pieces_v, sem):
    chunks = pieces_v.shape[0] // SC_ROWS_PER_STEP
    lane = lax.iota(I32, SC_LANES)
    row_in_group = lane & (SUBLANES - 1)
    chunk_in_pair = lane >> 3
    rows_per_gather = SC_PIECES_PER_GATHER // chunks

    @pl.loop(0, n_rows // SC_ROWS_PER_STEP)
    def _(step):
        copies = []
        for g in range(SC_ROWS_PER_STEP // rows_per_gather):
            r0 = step * SC_ROWS_PER_STEP + g * rows_per_gather
            for v in range(SC_PIECES_PER_GATHER // SC_LANES):
                group, chunk0 = v // (chunks // 2), 2 * (v % (chunks // 2))
                tok = plsc.load_gather(src_v, [r0 + group * SUBLANES + row_in_group])
                piece = (tok >> 3) * (SUBLANES * chunks) + (chunk0 + chunk_in_pair) * SUBLANES + (tok & 7)
                idx_v[pl.ds(g * SC_PIECES_PER_GATHER + v * SC_LANES, SC_LANES)] = piece
            window = pl.ds(g * SC_PIECES_PER_GATHER, SC_PIECES_PER_GATHER)
            copies.append(pltpu.async_copy(table_hbm.at[idx_v.at[window]], pieces_v.at[window], sem))
        for cp in copies:
            cp.wait()
        first = pl.multiple_of((lo + step * SC_ROWS_PER_STEP) * chunks, SC_ROWS_PER_STEP * chunks)
        pltpu.sync_copy(pieces_v, out_hbm.at[pl.ds(first, SC_ROWS_PER_STEP * chunks)])


def _sc_scratch(chunks, dtype):
    return [pltpu.VMEM((SC_ROWS_PER_STEP * chunks,), I32), pltpu.VMEM((SC_ROWS_PER_STEP * chunks, LANES), dtype)]


def _sc_dispatch(h2_flat, gate_rows, dest, n_rows):
    T = dest.shape[0]
    per_worker = n_rows // SC_WORKERS
    rows_per_step = SC_ROWS_PER_STEP
    chunks = h2_flat.shape[0] // T
    assert n_rows % SC_WORKERS == 0 and per_worker % rows_per_step == 0 and T % SC_LANES == 0
    mesh = plsc.VectorSubcoreMesh(core_axis_name="c", subcore_axis_name="s")

    @functools.partial(
        pl.kernel, mesh=mesh,
        out_type=[jax.ShapeDtypeStruct((n_rows * chunks, LANES), h2_flat.dtype),
                  jax.ShapeDtypeStruct((n_rows, LANES), F32)],
        scratch_types=[pltpu.VMEM((T,), I32), pltpu.VMEM((per_worker,), I32)]
        + _sc_scratch(chunks, h2_flat.dtype)
        + [pltpu.VMEM((rows_per_step, LANES), F32), pltpu.SemaphoreType.DMA, pltpu.SemaphoreType.DMA],
        compiler_params=pltpu.CompilerParams(use_tc_tiling_on_sc=True, needs_layout_passes=False),
        name="sc_dispatch",
    )
    def dispatch(h2_hbm, gate_hbm, dest_hbm, out_h_hbm, out_g_hbm,
                 dest_v, src_v, idx_v, pieces_v, gates_v, sem_h, sem_g):
        worker = lax.axis_index("s") * SC_CORES + lax.axis_index("c")
        lo = worker * per_worker
        pltpu.sync_copy(dest_hbm, dest_v)

        @pl.loop(0, per_worker // SC_LANES)
        def _(j):
            j0 = pl.multiple_of(j * SC_LANES, SC_LANES)
            src_v[pl.ds(j0, SC_LANES)] = lax.rem(lo + j0 + lax.iota(I32, SC_LANES), T)

        @pl.loop(0, T // SC_LANES)
        def _(j):
            t0 = pl.multiple_of(j * SC_LANES, SC_LANES)
            d = dest_v[pl.ds(t0, SC_LANES)] - lo
            mine = (d >= 0) & (d < per_worker)
            plsc.store_scatter(src_v, [jnp.where(mine, d, 0)], t0 + lax.iota(I32, SC_LANES), mask=mine)

        @pl.loop(0, per_worker // rows_per_step)
        def _(j):
            off = pl.multiple_of(j * rows_per_step, rows_per_step)
            pltpu.async_copy(gate_hbm.at[src_v.at[pl.ds(off, rows_per_step)]], gates_v, sem_g).wait()
            pltpu.sync_copy(gates_v, out_g_hbm.at[pl.ds(lo + off, rows_per_step)])

        _sc_move_rows(src_v, h2_hbm, out_h_hbm, lo, per_worker, idx_v, pieces_v, sem_h)

    return dispatch(h2_flat, gate_rows, dest)


def _expert_kernel(meta, x_ref, gv_ref, wgu_hbm, wd_hbm, o_ref, wgu_ref, wd_ref, ready_s, sems, *, tile, per_step):
    groups = tile // SUBLANES
    n_used = meta[2, LANES - 1]
    step = pl.program_id(0)

    def weight_copies(g):
        experts = pl.ds(g * EXP_PER_GROUP, EXP_PER_GROUP)
        return (pltpu.make_async_copy(wgu_hbm.at[experts], wgu_ref.at[experts], sems.at[g]),
                pltpu.make_async_copy(wd_hbm.at[experts], wd_ref.at[experts], sems.at[g]))

    def land_through(last_group):
        landed = ready_s[0]
        for g in range(N_EXP_GROUPS):
            @pl.when((g >= landed) & (g <= last_group))
            def _():
                for cp in weight_copies(g):
                    cp.wait()
                if g + 1 < N_EXP_GROUPS:
                    for cp in weight_copies(g + 1):
                        cp.start()
        ready_s[0] = jnp.maximum(landed, last_group + 1)

    @pl.when(step == 0)
    def _():
        ready_s[0] = 0
        for cp in weight_copies(0):
            cp.start()

    def one_tile(k, carry):
        t = step * per_step + k
        rows = pl.ds(pl.multiple_of(k * groups, groups), groups)

        @pl.when(t < n_used)
        def _():
            land_through(meta[0, t] // EXP_PER_GROUP)
            x = _unpack_bf16_pairs(_load_tiles(x_ref.at[rows]))
            gv = gv_ref[pl.ds(pl.multiple_of(k * tile, SUBLANES), tile), :]
            lane = lax.broadcasted_iota(I32, gv.shape, 1)
            out = None
            for e in (meta[0, t], meta[1, t]):
                ge = jnp.sum(jnp.where(lane == EXPERT_LANE0 + e, gv, 0.0), axis=-1, keepdims=True)
                h = jnp.dot(x, wgu_ref[e], preferred_element_type=F32)
                hg = h[:, 0:D_EXPERT]
                hid = (hg * _sigmoid(hg) * h[:, D_EXPERT:2 * D_EXPERT] * ge).astype(BF16)
                y = jnp.dot(hid, wd_ref[e], preferred_element_type=F32)
                out = y if out is None else out + y
            _store_tiles(o_ref.at[rows], _pack_bf16_pairs(out))

        return carry

    lax.fori_loop(0, per_step, one_tile, 0)

    @pl.when(step == pl.num_programs(0) - 1)
    def _():
        land_through(N_EXP_GROUPS - 1)


def _experts(sorted_h2, sorted_gates, meta, wgu, wd, tiling):
    tile, per_step, n_tiles = tiling
    step_rows = tile * per_step
    assert n_tiles * tile == sorted_h2.shape[0] * SUBLANES and n_tiles % per_step == 0

    def last_used(i, meta):
        return jnp.minimum(i, (meta[2, LANES - 1] - 1) // per_step)

    return pl.pallas_call(
        functools.partial(_expert_kernel, tile=tile, per_step=per_step),
        grid_spec=pltpu.PrefetchScalarGridSpec(
            num_scalar_prefetch=1,
            grid=(n_tiles // per_step,),
            in_specs=[
                _tiles_spec(step_rows, last_used, PACKED_CHUNKS),
                pl.BlockSpec((step_rows, LANES), lambda *a: (last_used(*a), 0)),
                pl.BlockSpec(memory_space=pl.ANY), pl.BlockSpec(memory_space=pl.ANY),
            ],
            out_specs=_tiles_spec(step_rows, last_used, PACKED_CHUNKS),
            scratch_shapes=[pltpu.VMEM(wgu.shape, wgu.dtype), pltpu.VMEM(wd.shape, wd.dtype),
                            pltpu.SMEM((1,), I32), pltpu.SemaphoreType.DMA((N_EXP_GROUPS,))],
        ),
        out_shape=jax.ShapeDtypeStruct(_tiles_shape(n_tiles * tile, PACKED_CHUNKS), U32),
        compiler_params=pltpu.CompilerParams(
            dimension_semantics=("arbitrary",), vmem_limit_bytes=V7X_VMEM_LIMIT_BYTES),
        name="moe_experts",
    )(meta, sorted_h2, sorted_gates, wgu, wd)


def _sc_row_gather(table_flat, idx, chunks):
    n = idx.shape[0]
    per_worker = n // SC_WORKERS
    assert n % SC_WORKERS == 0 and per_worker % SC_ROWS_PER_STEP == 0
    mesh = plsc.VectorSubcoreMesh(core_axis_name="c", subcore_axis_name="s")

    @functools.partial(
        pl.kernel, mesh=mesh,
        out_type=jax.ShapeDtypeStruct((n * chunks, LANES), table_flat.dtype),
        scratch_types=[pltpu.VMEM((per_worker,), I32)] + _sc_scratch(chunks, table_flat.dtype)
        + [pltpu.SemaphoreType.DMA],
        compiler_params=pltpu.CompilerParams(use_tc_tiling_on_sc=True, needs_layout_passes=False),
        name="sc_row_gather",
    )
    def gather(table_hbm, idx_hbm, out_hbm, src_v, idx_v, pieces_v, sem):
        worker = lax.axis_index("s") * SC_CORES + lax.axis_index("c")
        lo = worker * per_worker
        pltpu.sync_copy(idx_hbm.at[pl.ds(lo, per_worker)], src_v)
        _sc_move_rows(src_v, table_hbm, out_hbm, lo, per_worker, idx_v, pieces_v, sem)

    return gather(table_flat, idx)


def _final_kernel(x_ref, moe_ref, gt2_ref, gf_ref, o_ref, *, mod_row):
    gt2 = gt2_ref[pl.ds(mod_row(pl.program_id(0)), 1), :]
    y = x_ref[...] + gt2 * _unpack_bf16_pairs(_load_tiles(moe_ref)).astype(F32)
    o_ref[...] = _rms(y) * gf_ref[...]


def _final(xmid, moe_rows, mod, mod_row, gf):
    T = xmid.shape[0]
    return pl.pallas_call(
        functools.partial(_final_kernel, mod_row=mod_row),
        grid=(T // FINAL_BLOCK,),
        in_specs=[
            pl.BlockSpec((FINAL_BLOCK, D_MODEL), lambda i: (i, 0)),
            _tiles_spec(FINAL_BLOCK, lambda i: i, PACKED_CHUNKS),
            pl.BlockSpec((COND_ROWS, D_MODEL), lambda i: (0, 5)),
            pl.BlockSpec((1, D_MODEL), lambda i: (0, 0)),
        ],
        out_specs=pl.BlockSpec((FINAL_BLOCK, D_MODEL), lambda i: (i, 0)),
        out_shape=jax.ShapeDtypeStruct((T, D_MODEL), F32),
        compiler_params=pltpu.CompilerParams(
            dimension_semantics=("arbitrary",), vmem_limit_bytes=V7X_VMEM_LIMIT_BYTES),
        name="moe_final",
    )(xmid, moe_rows, mod, gf)


def _flat(tiles):
    return tiles.reshape(-1, LANES)


def _expert_tiling(T):
    tile = max(256, -(-(T * 9) // (8 * N_BUCKETS * 64)) * 64)
    per_step = max(1, EXPERT_STEP_ROWS // tile)
    n_tiles = (T + N_BUCKETS * (tile - 1)) // tile
    while n_tiles % per_step or (n_tiles * tile) % (SC_WORKERS * SC_ROWS_PER_STEP):
        n_tiles += 1
    return tile, per_step, n_tiles


def _moe_dispatch(h2_tiles, gate_rows, dest, tiling):
    tile, _, n_tiles = tiling
    n_rows = n_tiles * tile
    assert n_tiles <= LANES
    sorted_h2, sorted_gates = _sc_dispatch(_flat(h2_tiles), gate_rows, dest, n_rows)
    return sorted_h2.reshape(_tiles_shape(n_rows, PACKED_CHUNKS)), sorted_gates


def _moe_unpermute(moe_sorted_tiles, dest):
    chunks = moe_sorted_tiles.shape[1]
    return _sc_row_gather(_flat(moe_sorted_tiles), dest, chunks).reshape(_tiles_shape(dest.shape[0], chunks))


def _rope_tables(n_tokens):
    t = np.arange(n_tokens)
    row = (t // GRID_W).astype(np.float32)
    col = (t % GRID_W).astype(np.float32)
    freq = np.float32(ROPE_THETA) ** (-np.arange(ROPE_NF, dtype=np.float32) / np.float32(ROPE_NF))
    ang = np.concatenate([row[:, None] * freq] * 2 + [col[:, None] * freq] * 2, axis=-1)
    first = (np.arange(HEAD_DIM) % (2 * ROPE_NF)) < ROPE_NF
    sin = np.sin(ang)
    zero = np.float32(0.0)
    return (jnp.asarray(np.cos(ang)), jnp.asarray(np.where(first, -sin, zero)),
            jnp.asarray(np.where(first, zero, sin)))


def kernel(x_prompt, x_sample, cache_k, cache_v, c, c_ctx, norm1_g, norm2_g, w_ada, b_ada, w_in, q_norm_g, k_norm_g, w_pool, pool_scale, w_branch_a, w_branch_b, w_out, w_router_group, w_router_expert, w_exp_gate, w_exp_up, w_exp_down, final_norm_g):
    assert norm1_g.shape[0] == 1, "single-layer trunk"
    B, L_ctx, _ = x_prompt.shape
    Bs, L_lat, _ = x_sample.shape
    P = cache_k.shape[2]
    assert 1 + Bs <= COND_ROWS

    cond = jnp.concatenate([c_ctx[None, :], c, jnp.zeros((COND_ROWS - 1 - Bs, D_MODEL), F32)], axis=0)
    mod, w_in_b, wa_b, wb_b, wo_b, wpool_b = _ada(
        cond, w_ada[0], b_ada[0][None, :], w_pool[0],
        cast=(w_in[0], w_branch_a[0], w_branch_b[0], w_out[0]))

    wr = jnp.concatenate([w_router_group[0], w_router_expert[0],
                          jnp.zeros((D_MODEL, LANES - N_EXP_GROUPS - N_EXPERTS), F32)], axis=1)
    wr_hi = wr.astype(BF16)
    wr_lo = (wr - wr_hi.astype(F32)).astype(BF16)
    mix_w = (norm1_g[0][None, :], w_in_b, q_norm_g[0][None, :], k_norm_g[0][None, :],
             wpool_b, pool_scale[0][None, :], wa_b, wb_b, wo_b,
             norm2_g[0][None, :], jnp.concatenate([wr_hi, wr_lo], axis=1))
    gf = final_norm_g[None, :]

    xp2 = x_prompt.reshape(B * L_ctx, D_MODEL)
    tiling_p = _expert_tiling(B * L_ctx)
    xmid_p, h2_p, gate_p, dest_p, meta_p, knew, vnew, wgu, wd = _mix(
        xp2, mod, lambda i: 0, None, None, mix_w, S=2, L=L_ctx, emit_kv=True, blocks_per_step=2,
        tile=tiling_p[0], cast=((w_exp_gate[0], w_exp_up[0]), (w_exp_down[0],)))
    dest_p = dest_p.reshape(B * L_ctx)
    sh_p, sg_p = _moe_dispatch(h2_p, gate_p, dest_p, tiling_p)

    xs2 = x_sample.reshape(Bs * L_lat, D_MODEL)
    cache = (cache_k.reshape(Bs * P * N_KV_HEADS, HEAD_DIM), cache_v.reshape(Bs * P * N_KV_HEADS, HEAD_DIM))
    tiling_s = _expert_tiling(Bs * L_lat)
    xmid_s, h2_s, gate_s, dest_s, meta_s = _mix(
        xs2, mod, lambda i: 1 + i, cache, _rope_tables(L_lat), mix_w,
        S=1, L=L_lat, emit_kv=False, blocks_per_step=1, tile=tiling_s[0])
    dest_s = dest_s.reshape(Bs * L_lat)
    sh_s, sg_s = _moe_dispatch(h2_s, gate_s, dest_s, tiling_s)

    moe_p = _moe_unpermute(_experts(sh_p, sg_p, meta_p, wgu, wd, tiling_p), dest_p)
    moe_s = _moe_unpermute(_experts(sh_s, sg_s, meta_s, wgu, wd, tiling_s), dest_s)
    y_prompt = _final(xmid_p, moe_p, mod, lambda i: 0, gf)
    blocks_per_seq = L_lat // FINAL_BLOCK
    y_sample = _final(xmid_s, moe_s, mod, lambda i: 1 + i // blocks_per_seq, gf)

    return (y_prompt.reshape(B, L_ctx, D_MODEL), y_sample.reshape(Bs, L_lat, D_MODEL),
            knew.reshape(B, 1, L_ctx, N_KV_HEADS, HEAD_DIM), vnew.reshape(B, 1, L_ctx, N_KV_HEADS, HEAD_DIM))
```

```python
import functools

import numpy as np
import jax
import jax.numpy as jnp
from jax import lax
from jax.experimental import pallas as pl
from jax.experimental.pallas import tpu as pltpu
from jax.experimental.pallas import tpu_sc as plsc

F32 = jnp.float32
BF16 = jnp.bfloat16
I32 = jnp.int32
U32 = jnp.uint32

D_MODEL = 1024
HEAD_DIM = 128
N_HEADS = 8
N_KV_HEADS = 2
GROUP = N_HEADS // N_KV_HEADS
ATTN_W = N_HEADS * HEAD_DIM
KV_W = N_KV_HEADS * HEAD_DIM
POOL_WINDOWS = (2, 4, 8, 16)
POOL_GC = 128
POOL_W = POOL_GC * len(POOL_WINDOWS)
IN_W = ATTN_W + 2 * KV_W + POOL_W + 2 * D_MODEL
GATE_COL = ATTN_W + 2 * KV_W + POOL_W
GRID_W = 64
ROPE_THETA = 10000.0
ROPE_NF = HEAD_DIM // 4
N_EXP_GROUPS = 4
EXP_PER_GROUP = 4
N_EXPERTS = 16
D_EXPERT = 256
EPS = 1e-6
LOG2_E = 1.4426950408889634

LANES = 128
SUBLANES = 8
COND_ROWS = SUBLANES
POOL_HALO = 8
ROW_BLOCK = 256
ADA_COLS = 768
EXPERT_LANE0 = N_EXP_GROUPS
PAIRS_PER_GROUP = EXP_PER_GROUP * (EXP_PER_GROUP - 1) // 2
N_BUCKETS = N_EXP_GROUPS * PAIRS_PER_GROUP
EXPERT_STEP_ROWS = 1536
TOKEN_BLOCK = 1024
FINAL_BLOCK = 1024
ROW_CHUNKS = D_MODEL // LANES
SC_CORES = 2
SC_SUBCORES = 16
SC_WORKERS = SC_CORES * SC_SUBCORES
SC_LANES = 16
SC_PIECES_PER_GATHER = 128
SC_ROWS_PER_STEP = 64
PACKED_CHUNKS = ROW_CHUNKS // 2
V7X_VMEM_LIMIT_BYTES = 56 * 1024 * 1024


def _sigmoid(x):
    return 1.0 / (1.0 + jnp.exp(-x))


def _rms(x):
    return x * lax.rsqrt(jnp.mean(x * x, axis=-1, keepdims=True) + EPS)


def _resident(shape):
    zeros = (0,) * len(shape)
    return pl.BlockSpec(shape, lambda i, *_: zeros, pipeline_mode=pl.Buffered(1))


def _tiles_shape(n, chunks=ROW_CHUNKS):
    return (n // SUBLANES, chunks, SUBLANES, LANES)


def _tiles_spec(n, block_index, chunks=ROW_CHUNKS):
    return pl.BlockSpec(_tiles_shape(n, chunks), lambda *a: (block_index(*a), 0, 0, 0))


def _store_tiles(ref, x):
    for c in range(ref.shape[1]):
        ref[:, c, :, :] = x[:, c * LANES:(c + 1) * LANES].reshape(x.shape[0] // SUBLANES, SUBLANES, LANES)


def _load_tiles(ref):
    n = ref.shape[0] * SUBLANES
    return jnp.concatenate([ref[:, c, :, :].reshape(n, LANES) for c in range(ref.shape[1])], axis=1)


def _pack_bf16_pairs(x):
    bits = pltpu.bitcast(x.astype(BF16).astype(F32), U32)
    w = x.shape[1] // 2
    return bits[:, :w] | (bits[:, w:] >> 16)


def _unpack_bf16_pairs(words):
    hi = pltpu.bitcast(words & jnp.uint32(0xFFFF0000), F32).astype(BF16)
    lo = pltpu.bitcast(words << 16, F32).astype(BF16)
    return jnp.concatenate([hi, lo], axis=1)


def _row(x):
    return jnp.transpose(jnp.broadcast_to(x, (x.shape[0], LANES)))[0:1, :]


def _ada_kernel(c_ref, w_ref, b_ref, *refs, steps_per_pool_group):
    n_cast = len(refs) // 2 - 1
    c = c_ref[...]
    s = (c * _sigmoid(c)).astype(BF16)
    refs[n_cast + 1][...] = jnp.dot(s, w_ref[...].astype(BF16), preferred_element_type=F32) + b_ref[...]
    for src, dst in zip(refs[:n_cast], refs[n_cast + 2:]):
        dst[...] = src[...].astype(BF16)
    pool_src, pool_dst = refs[n_cast], refs[-1]
    wide = jnp.concatenate([pool_src[0]] * len(POOL_WINDOWS), axis=1)
    lane_group = lax.broadcasted_iota(I32, wide.shape, 1) // POOL_GC
    pool_dst[...] = jnp.where(lane_group == pl.program_id(0) // steps_per_pool_group, wide, 0.0).astype(BF16)


def _ada(cond, w_ada, b_ada, w_pool, cast=()):
    n = w_ada.shape[1]
    n_steps = n // ADA_COLS
    cast_specs = []
    for w in cast:
        assert w.ndim == 2 and w.shape[0] % (n_steps * 2 * SUBLANES) == 0
        cast_specs.append(pl.BlockSpec((w.shape[0] // n_steps, w.shape[1]), lambda j: (j, 0)))
    pool_rows = POOL_W // n_steps
    spg = POOL_GC // pool_rows
    assert w_pool.shape == (len(POOL_WINDOWS), POOL_GC, POOL_GC) and POOL_GC % pool_rows == 0
    assert pool_rows % (2 * SUBLANES) == 0
    pool_in = pl.BlockSpec((1, pool_rows, POOL_GC), lambda j: (j // spg, j % spg, 0))
    pool_out = pl.BlockSpec((pool_rows, POOL_W), lambda j: (j, 0))
    return pl.pallas_call(
        functools.partial(_ada_kernel, steps_per_pool_group=spg),
        grid=(n_steps,),
        in_specs=[
            pl.BlockSpec((COND_ROWS, D_MODEL), lambda j: (0, 0)),
            pl.BlockSpec((D_MODEL, ADA_COLS), lambda j: (0, j)),
            pl.BlockSpec((1, ADA_COLS), lambda j: (0, j)),
        ] + cast_specs + [pool_in],
        out_specs=[pl.BlockSpec((COND_ROWS, ADA_COLS), lambda j: (0, j))] + cast_specs + [pool_out],
        out_shape=[jax.ShapeDtypeStruct((COND_ROWS, n), F32)] + [jax.ShapeDtypeStruct(w.shape, BF16) for w in cast]
        + [jax.ShapeDtypeStruct((POOL_W, POOL_W), BF16)],
        name="ada_mod",
    )(cond, w_ada, b_ada, *cast, w_pool)


def _route(logits):
    lane = lax.broadcasted_iota(I32, logits.shape, 1).astype(F32)
    neg = jnp.float32(-1e30)
    far = jnp.float32(LANES)
    is_g = lane < N_EXP_GROUPS
    gl = jnp.where(is_g, logits, neg)
    gmax = jnp.max(gl, axis=-1, keepdims=True)
    gsel = jnp.min(jnp.where(gl == gmax, lane, far), axis=-1, keepdims=True)
    psel = 1.0 / jnp.sum(jnp.where(is_g, jnp.exp(gl - gmax), 0.0), axis=-1, keepdims=True)
    e_lo = EXPERT_LANE0 + EXP_PER_GROUP * gsel
    el = jnp.where(lane >= e_lo, jnp.where(lane < e_lo + EXP_PER_GROUP, logits, neg), neg)
    v1 = jnp.max(el, axis=-1, keepdims=True)
    i1 = jnp.min(jnp.where(el == v1, lane, far), axis=-1, keepdims=True)
    el2 = jnp.where(lane == i1, neg, el)
    v2 = jnp.max(el2, axis=-1, keepdims=True)
    i2 = jnp.min(jnp.where(el2 == v2, jnp.where(lane == i1, far, lane), far), axis=-1, keepdims=True)
    e2 = jnp.exp(v2 - v1)
    w1 = psel / (1.0 + e2)
    w2 = psel * e2 / (1.0 + e2)
    gate = jnp.where(lane == i1, w1, jnp.where(lane == i2, w2, 0.0))
    a = jnp.minimum(i1, i2) - e_lo
    b = jnp.maximum(i1, i2) - e_lo
    pair = a * (7.0 - a) * 0.5 + (b - a - 1.0)
    return gate, gsel * PAIRS_PER_GROUP + pair


def _mix_kernel(*refs, S, L, P, use_rope, emit_kv, n_cast, n_blocks, U, mod_row, tile):
    it = iter(refs)
    x_ref = next(it)
    mod_ref = next(it)
    if P:
        ck_ref = next(it)
        cv_ref = next(it)
    if use_rope:
        cos_ref = next(it)
        sneg_ref = next(it)
        spos_ref = next(it)
    (g1_ref, win_ref, qg_ref, kg_ref, wpool_hbm, pscale_ref, wa_hbm, wb_hbm, wo_hbm,
     g2_ref, wr_ref) = (next(it) for _ in range(11))
    cast_in = [[next(it) for _ in range(n)] for n in n_cast]
    xmid_ref = next(it)
    h2_ref = next(it)
    gate_ref = next(it)
    dest_ref = next(it)
    meta_ref = next(it)
    if emit_kv:
        knew_ref = next(it)
        vnew_ref = next(it)
    cast_out = [next(it) for _ in n_cast]
    q_s, k_s, v_s, xp_s, h_s, attn_s, xm_s, mod2_s, oh_s = (next(it) for _ in range(9))
    wpool_ref, wa_ref, wb_ref, wo_ref, late_sems = (next(it) for _ in range(5))
    late_copies = [pltpu.make_async_copy(src, dst, late_sems.at[i]) for i, (src, dst) in enumerate(
        ((wa_hbm, wa_ref), (wpool_hbm, wpool_ref), (wb_hbm, wb_ref), (wo_hbm, wo_ref)))]

    TM = S * L
    RB = ROW_BLOCK
    nrb = TM // RB
    n_steps = n_blocks // U
    score_gain = HEAD_DIM ** -0.5 * LOG2_E
    step = pl.program_id(0)
    block0 = U * jnp.minimum(step, n_steps - 1)
    slot = step % 2

    mod_at = pl.ds(mod_row(jnp.minimum(step, n_steps - 1) // (nrb // U)), 1)
    sh1 = mod_ref[mod_at, 0:D_MODEL]
    gain1 = g1_ref[...] * (1.0 + mod_ref[mod_at, D_MODEL:2 * D_MODEL])
    gt1 = mod_ref[mod_at, 2 * D_MODEL:3 * D_MODEL]
    sh2 = mod_ref[mod_at, 3 * D_MODEL:4 * D_MODEL]
    gain2 = g2_ref[...] * (1.0 + mod_ref[mod_at, 4 * D_MODEL:5 * D_MODEL])
    qg = qg_ref[...] * score_gain
    kg = kg_ref[...]

    def project(r, carry):
        r0 = pl.multiple_of(r * RB, RB)
        s = r0 // L
        o = pl.multiple_of(r0 % L, RB)
        hb = (_rms(x_ref[pl.ds(r0, RB), :]) * gain1 + sh1).astype(BF16)
        h_s[pl.ds(r0, RB), :] = hb
        p1 = jnp.dot(hb, win_ref[:, 0:GATE_COL], preferred_element_type=F32)
        if use_rope:
            cs = cos_ref[pl.ds(o, RB), :]
            sn = sneg_ref[pl.ds(o, RB), :]
            sp = spos_ref[pl.ds(o, RB), :]

        def rope(t):
            return (t * cs + pltpu.roll(t, HEAD_DIM - ROPE_NF, 1) * sn + pltpu.roll(t, ROPE_NF, 1) * sp)

        for hd in range(N_HEADS):
            qh = _rms(p1[:, hd * HEAD_DIM:(hd + 1) * HEAD_DIM]) * qg
            if use_rope:
                qh = rope(qh)
            q_s[hd, pl.ds(r0, RB), :] = qh.astype(BF16)
        for kh in range(N_KV_HEADS):
            c0 = ATTN_W + kh * HEAD_DIM
            kk = _rms(p1[:, c0:c0 + HEAD_DIM]) * kg
            if emit_kv:
                knew_ref[pl.ds(N_KV_HEADS * r0 + kh, RB, stride=N_KV_HEADS), :] = kk
            if use_rope:
                kk = rope(kk)
            k_s[s, pl.ds(P + o, RB), kh * HEAD_DIM:(kh + 1) * HEAD_DIM] = kk.astype(BF16)
        vv = p1[:, ATTN_W + KV_W:ATTN_W + 2 * KV_W]
        if emit_kv:
            for kh in range(N_KV_HEADS):
                vnew_ref[pl.ds(N_KV_HEADS * r0 + kh, RB, stride=N_KV_HEADS), :] = (
                    vv[:, kh * HEAD_DIM:(kh + 1) * HEAD_DIM])
        v_s[s, pl.ds(P + o, RB), :] = vv.astype(BF16)
        xp_s[s, pl.ds(POOL_HALO + o, RB), :] = p1[:, ATTN_W + 2 * KV_W:GATE_COL]
        return carry

    @pl.when(step == 0)
    def _():
        xm_s[1] = jnp.zeros((U * RB, D_MODEL), F32)
        mod2_s[1] = jnp.zeros((2, D_MODEL), F32)
        for cp in late_copies:
            cp.start()

    @pl.when((step < n_steps) & (step % (nrb // U) == 0))
    def _():
        if P:
            for kh in range(N_KV_HEADS):
                cols = slice(kh * HEAD_DIM, (kh + 1) * HEAD_DIM)
                k_s[0, 0:P, cols] = ck_ref[pl.ds(kh, P, stride=N_KV_HEADS), :].astype(BF16)
                v_s[0, 0:P, cols] = cv_ref[pl.ds(kh, P, stride=N_KV_HEADS), :].astype(BF16)
        xp_s[:, 0:POOL_HALO, :] = jnp.zeros((S, POOL_HALO, POOL_W), F32)
        xp_s[:, L + POOL_HALO:L + 2 * POOL_HALO, :] = jnp.zeros((S, POOL_HALO, POOL_W), F32)
        lax.fori_loop(0, TM // RB, project, 0)
        for srcs, dst in zip(cast_in, cast_out):
            col = 0
            for src in srcs:
                dst[..., col:col + src.shape[-1]] = src[...].astype(BF16)
                col += src.shape[-1]

    @pl.when(step == 0)
    def _():
        for cp in late_copies:
            cp.wait()

    def mix(u):
        r0 = pl.multiple_of(((block0 + u) % nrb) * RB, RB)
        s = r0 // L
        o = pl.multiple_of(r0 % L, RB)
        attn_u = attn_s.at[u]
        rows = slice(u * RB, (u + 1) * RB)

        for hd in range(N_HEADS):
            kh = hd // GROUP
            k = k_s[s, :, kh * HEAD_DIM:(kh + 1) * HEAD_DIM]
            v = v_s[s, :, kh * HEAD_DIM:(kh + 1) * HEAD_DIM]
            qh = q_s[hd, pl.ds(r0, RB), :]
            sc = lax.dot_general(qh, k, (((1,), (1,)), ((), ())), preferred_element_type=F32)
            e = jnp.exp2(sc - jnp.max(sc, axis=-1, keepdims=True))
            den = jnp.sum(e, axis=-1, keepdims=True)
            oh = jnp.dot(e.astype(BF16), v, preferred_element_type=F32) / den
            attn_u[:, hd * HEAD_DIM:(hd + 1) * HEAD_DIM] = oh.astype(BF16)
        a = jnp.dot(attn_u[...], wa_ref[...], preferred_element_type=F32)

        t = o + lax.broadcasted_iota(I32, (RB, 1), 0)
        RW = RB + 2 * POOL_HALO
        parts = []
        for gi, w in enumerate(POOL_WINDOWS):
            cols = slice(gi * POOL_GC, (gi + 1) * POOL_GC)
            xw = xp_s[s, pl.ds(o, RW), cols]
            run = xw
            span = 1
            while span < w:
                run = run + pltpu.roll(run, span, 0)
                span *= 2
            if w // 2 > 1:
                run = pltpu.roll(run, RW - (w // 2 - 1), 0)
            tot = run[POOL_HALO:POOL_HALO + RB]
            cnt = (jnp.minimum(t + w // 2, L) - jnp.maximum(t - w // 2, 0)).astype(F32)
            parts.append(tot / cnt - xw[POOL_HALO:POOL_HALO + RB])
        dpool = jnp.concatenate(parts, axis=1).astype(BF16)
        pooled = jnp.dot(dpool, wpool_ref[...], preferred_element_type=F32) * pscale_ref[...]
        b = jnp.dot(pooled.astype(BF16), wb_ref[...], preferred_element_type=F32)

        gates = jnp.dot(h_s[pl.ds(r0, RB), :], win_ref[:, GATE_COL:IN_W], preferred_element_type=F32)
        merged = _sigmoid(gates[:, 0:D_MODEL]) * a + _sigmoid(gates[:, D_MODEL:2 * D_MODEL]) * b
        upd = jnp.dot(merged.astype(BF16), wo_ref[...], preferred_element_type=F32)
        xm = x_ref[pl.ds(r0, RB), :] + gt1 * upd
        xmid_ref[rows, :] = xm
        xm_s[slot, rows, :] = xm

    def moe_prep(u):
        rows = slice(u * RB, (u + 1) * RB)
        h2 = _rms(xm_s[1 - slot, rows, :]) * mod2_s[1 - slot, 0:1, :] + mod2_s[1 - slot, 1:2, :]
        hi = h2.astype(BF16)
        lo = (h2 - hi.astype(F32)).astype(BF16)
        l1 = jnp.dot(hi, wr_ref[...], preferred_element_type=F32)
        l2 = jnp.dot(lo, wr_ref[:, 0:LANES], preferred_element_type=F32)
        gate, bucket = _route(l1[:, 0:LANES] + l1[:, LANES:2 * LANES] + l2)
        groups = pl.ds(u * (RB // SUBLANES), RB // SUBLANES)
        _store_tiles(h2_ref.at[groups], _pack_bf16_pairs(h2))
        gate_ref[rows, :] = gate
        lane = lax.broadcasted_iota(I32, (RB, LANES), 1).astype(F32)
        first = pl.multiple_of((U * jnp.maximum(step - 1, 0) + u) * RB, RB)
        oh_s[pl.ds(first, RB), :] = jnp.where(lane == bucket, 1.0, 0.0).astype(BF16)

    mod2_s[slot, 0:1, :] = gain2
    mod2_s[slot, 1:2, :] = sh2

    @pl.when(step < n_steps)
    def _():
        for u in range(U):
            moe_prep(u)
        for u in range(U):
            mix(u)

    @pl.when(step == n_steps)
    def _():
        for u in range(U):
            moe_prep(u)
        _plan_rows(oh_s, dest_ref, meta_ref, n_blocks=n_blocks * RB // TOKEN_BLOCK, tile=tile)


def _mix(x2d, mod, mod_row, cache, rope_tabs, weights, *, S, L, emit_kv, blocks_per_step, tile, cast=()):
    T = x2d.shape[0]
    TM = S * L
    P = cache[0].shape[0] // (T // L * N_KV_HEADS) if cache is not None else 0
    use_rope = rope_tabs is not None
    assert T % TM == 0 and L % ROW_BLOCK == 0
    assert not (use_rope or P) or S == 1
    Lk = P + L

    args = [x2d, mod]
    nrb = TM // ROW_BLOCK
    n_blocks = T // ROW_BLOCK
    step_rows = blocks_per_step * ROW_BLOCK
    steps_per_group = nrb // blocks_per_step
    n_mix_steps = n_blocks // blocks_per_step
    assert nrb % blocks_per_step == 0

    def mixed(s):
        return jnp.minimum(s, n_mix_steps - 1)

    def group(s):
        return mixed(s) // steps_per_group

    def prepared(s):
        return jnp.maximum(s - 1, 0)

    in_specs = [
        pl.BlockSpec((TM, D_MODEL), lambda s: (group(s), 0)),
        _resident(mod.shape),
    ]
    if P:
        args += list(cache)
        in_specs += [pl.BlockSpec((P * N_KV_HEADS, HEAD_DIM), lambda s: (group(s), 0))] * 2
    if use_rope:
        args += list(rope_tabs)
        in_specs += [_resident((L, HEAD_DIM))] * 3
    args += list(weights)
    late = (4, 6, 7, 8)
    in_specs += [pl.BlockSpec(memory_space=pl.ANY) if i in late else _resident(w.shape)
                 for i, w in enumerate(weights)]
    n_steps = T // TM
    def per_group(shape):
        assert shape[0] % n_steps == 0
        blk = (shape[0] // n_steps,) + shape[1:]
        return pl.BlockSpec(blk, lambda s, n=len(blk): (group(s),) + (0,) * (n - 1))

    cast_out_shapes = [ws[0].shape[:-1] + (sum(w.shape[-1] for w in ws),) for ws in cast]
    for ws in cast:
        args += list(ws)
        in_specs += [per_group(w.shape) for w in ws]

    assert T % TOKEN_BLOCK == 0
    out_shape = [jax.ShapeDtypeStruct((T, D_MODEL), F32), jax.ShapeDtypeStruct(_tiles_shape(T, PACKED_CHUNKS), U32),
                 jax.ShapeDtypeStruct((T, LANES), F32),
                 jax.ShapeDtypeStruct((T // TOKEN_BLOCK, 1, TOKEN_BLOCK), I32),
                 jax.ShapeDtypeStruct((SUBLANES, LANES), I32)]
    out_specs = [pl.BlockSpec((step_rows, D_MODEL), lambda s: (mixed(s), 0)),
                 _tiles_spec(step_rows, prepared, PACKED_CHUNKS),
                 pl.BlockSpec((step_rows, LANES), lambda s: (prepared(s), 0)),
                 pl.BlockSpec((T // TOKEN_BLOCK, 1, TOKEN_BLOCK), lambda s: (0, 0, 0)),
                 pl.BlockSpec((SUBLANES, LANES), lambda s: (0, 0))]
    if emit_kv:
        out_shape += [jax.ShapeDtypeStruct((T * N_KV_HEADS, HEAD_DIM), F32)] * 2
        out_specs += [pl.BlockSpec((TM * N_KV_HEADS, HEAD_DIM), lambda s: (group(s), 0))] * 2
    out_shape += [jax.ShapeDtypeStruct(shp, BF16) for shp in cast_out_shapes]
    out_specs += [per_group(shp) for shp in cast_out_shapes]

    scratch = [
        pltpu.VMEM((N_HEADS, TM, HEAD_DIM), BF16),
        pltpu.VMEM((S, Lk, KV_W), BF16),
        pltpu.VMEM((S, Lk, KV_W), BF16),
        pltpu.VMEM((S, L + 2 * POOL_HALO, POOL_W), F32),
        pltpu.VMEM((TM, D_MODEL), BF16),
        pltpu.VMEM((blocks_per_step, ROW_BLOCK, ATTN_W), BF16),
        pltpu.VMEM((2, step_rows, D_MODEL), F32),
        pltpu.VMEM((2, 2, D_MODEL), F32),
        pltpu.VMEM((T, LANES), BF16),
    ] + [pltpu.VMEM(weights[i].shape, weights[i].dtype) for i in late] + [pltpu.SemaphoreType.DMA((len(late),))]
    kern = functools.partial(_mix_kernel, S=S, L=L, P=P, use_rope=use_rope, emit_kv=emit_kv,
                             n_cast=tuple(len(ws) for ws in cast), n_blocks=n_blocks, U=blocks_per_step,
                             mod_row=mod_row, tile=tile)
    return pl.pallas_call(
        kern,
        grid=(n_mix_steps + 1,),
        in_specs=in_specs,
        out_specs=out_specs,
        out_shape=out_shape,
        scratch_shapes=scratch,
        compiler_params=pltpu.CompilerParams(
            dimension_semantics=("arbitrary",), vmem_limit_bytes=V7X_VMEM_LIMIT_BYTES),
        name="mixer_rope" if use_rope else "mixer_ctx",
    )(*args)


def _plan_rows(oh_ref, dest_ref, meta_ref, *, n_blocks, tile):
    TB = TOKEN_BLOCK
    lane = lax.broadcasted_iota(I32, (SUBLANES, LANES), 1)

    def count(b, acc):
        oh = oh_ref[pl.ds(pl.multiple_of(b * TB, TB), TB), :].astype(F32)
        return acc + jnp.sum(oh, axis=0, keepdims=True)

    counts = lax.fori_loop(0, n_blocks, count, jnp.zeros((SUBLANES, LANES), F32))
    padded = jnp.floor((counts + (tile - 0.5)) * (1.0 / tile)) * tile
    ends = padded
    step = 1
    while step < LANES:
        ends = ends + jnp.where(lane >= step, pltpu.roll(ends, step, 1), 0.0)
        step *= 2
    starts = ends - padded

    tri = jnp.where(lax.broadcasted_iota(I32, (TB, TB), 1) < lax.broadcasted_iota(I32, (TB, TB), 0),
                    1.0, 0.0).astype(BF16)

    def place(b, seen):
        oh = oh_ref[pl.ds(pl.multiple_of(b * TB, TB), TB), :]
        ohf = oh.astype(F32)
        rank = jnp.dot(tri, oh, preferred_element_type=F32)
        base = (starts + seen)[0:1, :]
        d = jnp.sum(ohf * (rank + base), axis=1, keepdims=True)
        dest_ref[b] = _row(d).astype(I32)
        return seen + jnp.sum(ohf, axis=0, keepdims=True)

    lax.fori_loop(0, n_blocks, place, jnp.zeros((SUBLANES, LANES), F32))

    tile_row0 = lax.broadcasted_iota(I32, (LANES, LANES), 0).astype(F32) * tile
    is_bucket = lax.broadcasted_iota(I32, (LANES, LANES), 1) < N_BUCKETS
    done = jnp.sum(jnp.where(is_bucket, jnp.where(ends[0:1, :] <= tile_row0, 1.0, 0.0), 0.0),
                   axis=1, keepdims=True)
    bkt = jnp.minimum(done, N_BUCKETS - 1.0)
    grp = (jnp.where(bkt >= PAIRS_PER_GROUP, 1.0, 0.0) + jnp.where(bkt >= 2 * PAIRS_PER_GROUP, 1.0, 0.0)
           + jnp.where(bkt >= 3 * PAIRS_PER_GROUP, 1.0, 0.0))
    pair = bkt - PAIRS_PER_GROUP * grp
    a = jnp.where(pair >= 3.0, 1.0, 0.0) + jnp.where(pair >= 5.0, 1.0, 0.0)
    b = pair - a * (7.0 - a) * 0.5 + a + 1.0
    e1 = EXP_PER_GROUP * grp + a
    e2 = EXP_PER_GROUP * grp + b
    meta = jnp.concatenate(
        [_row(e1), _row(e2), jnp.floor(ends[0:1, :] * (1.0 / tile) + 0.5),
         jnp.zeros((SUBLANES - 3, LANES), F32)], axis=0)
    meta_ref[...] = meta.astype(I32)


def _sc_move_rows(src_v, table_hbm, out_hbm, lo, n_rows, idx_v, pieces_v, sem):
    chunks = pieces_v.shape[0] // SC_ROWS_PER_STEP
    lane = lax.iota(I32, SC_LANES)
    row_in_group = lane & (SUBLANES - 1)
    chunk_in_pair = lane >> 3
    rows_per_gather = SC_PIECES_PER_GATHER // chunks

    @pl.loop(0, n_rows // SC_ROWS_PER_STEP)
    def _(step):
        copies = []
        for g in range(SC_ROWS_PER_STEP // rows_per_gather):
            r0 = step * SC_ROWS_PER_STEP + g * rows_per_gather
            for v in range(SC_PIECES_PER_GATHER // SC_LANES):
                group, chunk0 = v // (chunks // 2), 2 * (v % (chunks // 2))
                tok = plsc.load_gather(src_v, [r0 + group * SUBLANES + row_in_group])
                piece = (tok >> 3) * (SUBLANES * chunks) + (chunk0 + chunk_in_pair) * SUBLANES + (tok & 7)
                idx_v[pl.ds(g * SC_PIECES_PER_GATHER + v * SC_LANES, SC_LANES)] = piece
            window = pl.ds(g * SC_PIECES_PER_GATHER, SC_PIECES_PER_GATHER)
            copies.append(pltpu.async_copy(table_hbm.at[idx_v.at[window]], pieces_v.at[window], sem))
        for cp in copies:
            cp.wait()
        first = pl.multiple_of((lo + step * SC_ROWS_PER_STEP) * chunks, SC_ROWS_PER_STEP * chunks)
        pltpu.sync_copy(pieces_v, out_hbm.at[pl.ds(first, SC_ROWS_PER_STEP * chunks)])


def _sc_scratch(chunks, dtype):
    return [pltpu.VMEM((SC_ROWS_PER_STEP * chunks,), I32), pltpu.VMEM((SC_ROWS_PER_STEP * chunks, LANES), dtype)]


def _sc_dispatch(h2_flat, gate_rows, dest, n_rows):
    T = dest.shape[0]
    per_worker = n_rows // SC_WORKERS
    rows_per_step = SC_ROWS_PER_STEP
    chunks = h2_flat.shape[0] // T
    assert n_rows % SC_WORKERS == 0 and per_worker % rows_per_step == 0 and T % SC_LANES == 0
    mesh = plsc.VectorSubcoreMesh(core_axis_name="c", subcore_axis_name="s")

    @functools.partial(
        pl.kernel, mesh=mesh,
        out_type=[jax.ShapeDtypeStruct((n_rows * chunks, LANES), h2_flat.dtype),
                  jax.ShapeDtypeStruct((n_rows, LANES), F32)],
        scratch_types=[pltpu.VMEM((T,), I32), pltpu.VMEM((per_worker,), I32)]
        + _sc_scratch(chunks, h2_flat.dtype)
        + [pltpu.VMEM((rows_per_step, LANES), F32), pltpu.SemaphoreType.DMA, pltpu.SemaphoreType.DMA],
        compiler_params=pltpu.CompilerParams(use_tc_tiling_on_sc=True, needs_layout_passes=False),
        name="sc_dispatch",
    )
    def dispatch(h2_hbm, gate_hbm, dest_hbm, out_h_hbm, out_g_hbm,
                 dest_v, src_v, idx_v, pieces_v, gates_v, sem_h, sem_g):
        worker = lax.axis_index("s") * SC_CORES + lax.axis_index("c")
        lo = worker * per_worker
        pltpu.sync_copy(dest_hbm, dest_v)

        @pl.loop(0, per_worker // SC_LANES)
        def _(j):
            j0 = pl.multiple_of(j * SC_LANES, SC_LANES)
            src_v[pl.ds(j0, SC_LANES)] = lax.rem(lo + j0 + lax.iota(I32, SC_LANES), T)

        @pl.loop(0, T // SC_LANES)
        def _(j):
            t0 = pl.multiple_of(j * SC_LANES, SC_LANES)
            d = dest_v[pl.ds(t0, SC_LANES)] - lo
            mine = (d >= 0) & (d < per_worker)
            plsc.store_scatter(src_v, [jnp.where(mine, d, 0)], t0 + lax.iota(I32, SC_LANES), mask=mine)

        @pl.loop(0, per_worker // rows_per_step)
        def _(j):
            off = pl.multiple_of(j * rows_per_step, rows_per_step)
            pltpu.async_copy(gate_hbm.at[src_v.at[pl.ds(off, rows_per_step)]], gates_v, sem_g).wait()
            pltpu.sync_copy(gates_v, out_g_hbm.at[pl.ds(lo + off, rows_per_step)])

        _sc_move_rows(src_v, h2_hbm, out_h_hbm, lo, per_worker, idx_v, pieces_v, sem_h)

    return dispatch(h2_flat, gate_rows, dest)


def _expert_kernel(meta, x_ref, gv_ref, wgu_hbm, wd_hbm, o_ref, wgu_ref, wd_ref, ready_s, sems, *, tile, per_step):
    groups = tile // SUBLANES
    n_used = meta[2, LANES - 1]
    step = pl.program_id(0)

    def weight_copies(g):
        experts = pl.ds(g * EXP_PER_GROUP, EXP_PER_GROUP)
        return (pltpu.make_async_copy(wgu_hbm.at[experts], wgu_ref.at[experts], sems.at[g]),
                pltpu.make_async_copy(wd_hbm.at[experts], wd_ref.at[experts], sems.at[g]))

    def land_through(last_group):
        landed = ready_s[0]
        for g in range(N_EXP_GROUPS):
            @pl.when((g >= landed) & (g <= last_group))
            def _():
                for cp in weight_copies(g):
                    cp.wait()
                if g + 1 < N_EXP_GROUPS:
                    for cp in weight_copies(g + 1):
                        cp.start()
        ready_s[0] = jnp.maximum(landed, last_group + 1)

    @pl.when(step == 0)
    def _():
        ready_s[0] = 0
        for cp in weight_copies(0):
            cp.start()

    def one_tile(k, t):
        rows = pl.ds(pl.multiple_of(k * groups, groups), groups)
        x = _unpack_bf16_pairs(_load_tiles(x_ref.at[rows]))
        gv = gv_ref[pl.ds(pl.multiple_of(k * tile, SUBLANES), tile), :]
        lane = lax.broadcasted_iota(I32, gv.shape, 1)
        out = None
        for e in (meta[0, t], meta[1, t]):
            ge = jnp.sum(jnp.where(lane == EXPERT_LANE0 + e, gv, 0.0), axis=-1, keepdims=True)
            h = jnp.dot(x, wgu_ref[e], preferred_element_type=F32)
            hg = h[:, 0:D_EXPERT]
            hid = (hg * _sigmoid(hg) * h[:, D_EXPERT:2 * D_EXPERT] * ge).astype(BF16)
            y = jnp.dot(hid, wd_ref[e], preferred_element_type=F32)
            out = y if out is None else out + y
        _store_tiles(o_ref.at[rows], _pack_bf16_pairs(out))

    def two_tiles(j, carry):
        k = 2 * j
        t = step * per_step + k

        @pl.when(t + 1 < n_used)
        def _():
            land_through(meta[0, t + 1] // EXP_PER_GROUP)
            one_tile(k, t)
            one_tile(k + 1, t + 1)

        @pl.when(t + 1 == n_used)
        def _():
            land_through(meta[0, t] // EXP_PER_GROUP)
            one_tile(k, t)

        return carry

    assert per_step % 2 == 0
    lax.fori_loop(0, per_step // 2, two_tiles, 0)

    @pl.when(step == pl.num_programs(0) - 1)
    def _():
        land_through(N_EXP_GROUPS - 1)


def _experts(sorted_h2, sorted_gates, meta, wgu, wd, tiling):
    tile, per_step, n_tiles = tiling
    step_rows = tile * per_step
    assert n_tiles * tile == sorted_h2.shape[0] * SUBLANES and n_tiles % per_step == 0

    def last_used(i, meta):
        return jnp.minimum(i, (meta[2, LANES - 1] - 1) // per_step)

    return pl.pallas_call(
        functools.partial(_expert_kernel, tile=tile, per_step=per_step),
        grid_spec=pltpu.PrefetchScalarGridSpec(
            num_scalar_prefetch=1,
            grid=(n_tiles // per_step,),
            in_specs=[
                _tiles_spec(step_rows, last_used, PACKED_CHUNKS),
                pl.BlockSpec((step_rows, LANES), lambda *a: (last_used(*a), 0)),
                pl.BlockSpec(memory_space=pl.ANY), pl.BlockSpec(memory_space=pl.ANY),
            ],
            out_specs=_tiles_spec(step_rows, last_used, PACKED_CHUNKS),
            scratch_shapes=[pltpu.VMEM(wgu.shape, wgu.dtype), pltpu.VMEM(wd.shape, wd.dtype),
                            pltpu.SMEM((1,), I32), pltpu.SemaphoreType.DMA((N_EXP_GROUPS,))],
        ),
        out_shape=jax.ShapeDtypeStruct(_tiles_shape(n_tiles * tile, PACKED_CHUNKS), U32),
        compiler_params=pltpu.CompilerParams(
            dimension_semantics=("arbitrary",), vmem_limit_bytes=V7X_VMEM_LIMIT_BYTES),
        name="moe_experts",
    )(meta, sorted_h2, sorted_gates, wgu, wd)


def _sc_row_gather(table_flat, idx, chunks):
    n = idx.shape[0]
    per_worker = n // SC_WORKERS
    assert n % SC_WORKERS == 0 and per_worker % SC_ROWS_PER_STEP == 0
    mesh = plsc.VectorSubcoreMesh(core_axis_name="c", subcore_axis_name="s")

    @functools.partial(
        pl.kernel, mesh=mesh,
        out_type=jax.ShapeDtypeStruct((n * chunks, LANES), table_flat.dtype),
        scratch_types=[pltpu.VMEM((per_worker,), I32)] + _sc_scratch(chunks, table_flat.dtype)
        + [pltpu.SemaphoreType.DMA],
        compiler_params=pltpu.CompilerParams(use_tc_tiling_on_sc=True, needs_layout_passes=False),
        name="sc_row_gather",
    )
    def gather(table_hbm, idx_hbm, out_hbm, src_v, idx_v, pieces_v, sem):
        worker = lax.axis_index("s") * SC_CORES + lax.axis_index("c")
        lo = worker * per_worker
        pltpu.sync_copy(idx_hbm.at[pl.ds(lo, per_worker)], src_v)
        _sc_move_rows(src_v, table_hbm, out_hbm, lo, per_worker, idx_v, pieces_v, sem)

    return gather(table_flat, idx)


def _final_kernel(x_ref, moe_ref, gt2_ref, gf_ref, o_ref, *, mod_row):
    gt2 = gt2_ref[pl.ds(mod_row(pl.program_id(0)), 1), :]
    y = x_ref[...] + gt2 * _unpack_bf16_pairs(_load_tiles(moe_ref)).astype(F32)
    o_ref[...] = _rms(y) * gf_ref[...]


def _final(xmid, moe_rows, mod, mod_row, gf):
    T = xmid.shape[0]
    return pl.pallas_call(
        functools.partial(_final_kernel, mod_row=mod_row),
        grid=(T // FINAL_BLOCK,),
        in_specs=[
            pl.BlockSpec((FINAL_BLOCK, D_MODEL), lambda i: (i, 0)),
            _tiles_spec(FINAL_BLOCK, lambda i: i, PACKED_CHUNKS),
            pl.BlockSpec((COND_ROWS, D_MODEL), lambda i: (0, 5)),
            pl.BlockSpec((1, D_MODEL), lambda i: (0, 0)),
        ],
        out_specs=pl.BlockSpec((FINAL_BLOCK, D_MODEL), lambda i: (i, 0)),
        out_shape=jax.ShapeDtypeStruct((T, D_MODEL), F32),
        compiler_params=pltpu.CompilerParams(
            dimension_semantics=("arbitrary",), vmem_limit_bytes=V7X_VMEM_LIMIT_BYTES),
        name="moe_final",
    )(xmid, moe_rows, mod, gf)


def _flat(tiles):
    return tiles.reshape(-1, LANES)


def _expert_tiling(T):
    tile = max(256, -(-(T * 9) // (8 * N_BUCKETS * 64)) * 64)
    per_step = max(2, EXPERT_STEP_ROWS // tile // 2 * 2)
    n_tiles = (T + N_BUCKETS * (tile - 1)) // tile
    while n_tiles % per_step or (n_tiles * tile) % (SC_WORKERS * SC_ROWS_PER_STEP):
        n_tiles += 1
    return tile, per_step, n_tiles


def _moe_dispatch(h2_tiles, gate_rows, dest, tiling):
    tile, _, n_tiles = tiling
    n_rows = n_tiles * tile
    assert n_tiles <= LANES
    sorted_h2, sorted_gates = _sc_dispatch(_flat(h2_tiles), gate_rows, dest, n_rows)
    return sorted_h2.reshape(_tiles_shape(n_rows, PACKED_CHUNKS)), sorted_gates


def _moe_unpermute(moe_sorted_tiles, dest):
    chunks = moe_sorted_tiles.shape[1]
    return _sc_row_gather(_flat(moe_sorted_tiles), dest, chunks).reshape(_tiles_shape(dest.shape[0], chunks))


def _rope_tables(n_tokens):
    t = np.arange(n_tokens)
    row = (t // GRID_W).astype(np.float32)
    col = (t % GRID_W).astype(np.float32)
    freq = np.float32(ROPE_THETA) ** (-np.arange(ROPE_NF, dtype=np.float32) / np.float32(ROPE_NF))
    ang = np.concatenate([row[:, None] * freq] * 2 + [col[:, None] * freq] * 2, axis=-1)
    first = (np.arange(HEAD_DIM) % (2 * ROPE_NF)) < ROPE_NF
    sin = np.sin(ang)
    zero = np.float32(0.0)
    return (jnp.asarray(np.cos(ang)), jnp.asarray(np.where(first, -sin, zero)),
            jnp.asarray(np.where(first, zero, sin)))


def kernel(x_prompt, x_sample, cache_k, cache_v, c, c_ctx, norm1_g, norm2_g, w_ada, b_ada, w_in, q_norm_g, k_norm_g, w_pool, pool_scale, w_branch_a, w_branch_b, w_out, w_router_group, w_router_expert, w_exp_gate, w_exp_up, w_exp_down, final_norm_g):
    assert norm1_g.shape[0] == 1, "single-layer trunk"
    B, L_ctx, _ = x_prompt.shape
    Bs, L_lat, _ = x_sample.shape
    P = cache_k.shape[2]
    assert 1 + Bs <= COND_ROWS

    cond = jnp.concatenate([c_ctx[None, :], c, jnp.zeros((COND_ROWS - 1 - Bs, D_MODEL), F32)], axis=0)
    mod, w_in_b, wa_b, wb_b, wo_b, wpool_b = _ada(
        cond, w_ada[0], b_ada[0][None, :], w_pool[0],
        cast=(w_in[0], w_branch_a[0], w_branch_b[0], w_out[0]))

    wr = jnp.concatenate([w_router_group[0], w_router_expert[0],
                          jnp.zeros((D_MODEL, LANES - N_EXP_GROUPS - N_EXPERTS), F32)], axis=1)
    wr_hi = wr.astype(BF16)
    wr_lo = (wr - wr_hi.astype(F32)).astype(BF16)
    mix_w = (norm1_g[0][None, :], w_in_b, q_norm_g[0][None, :], k_norm_g[0][None, :],
             wpool_b, pool_scale[0][None, :], wa_b, wb_b, wo_b,
             norm2_g[0][None, :], jnp.concatenate([wr_hi, wr_lo], axis=1))
    gf = final_norm_g[None, :]

    xp2 = x_prompt.reshape(B * L_ctx, D_MODEL)
    tiling_p = _expert_tiling(B * L_ctx)
    xmid_p, h2_p, gate_p, dest_p, meta_p, knew, vnew, wgu, wd = _mix(
        xp2, mod, lambda i: 0, None, None, mix_w, S=2, L=L_ctx, emit_kv=True, blocks_per_step=2,
        tile=tiling_p[0], cast=((w_exp_gate[0], w_exp_up[0]), (w_exp_down[0],)))
    dest_p = dest_p.reshape(B * L_ctx)
    sh_p, sg_p = _moe_dispatch(h2_p, gate_p, dest_p, tiling_p)

    xs2 = x_sample.reshape(Bs * L_lat, D_MODEL)
    cache = (cache_k.reshape(Bs * P * N_KV_HEADS, HEAD_DIM), cache_v.reshape(Bs * P * N_KV_HEADS, HEAD_DIM))
    tiling_s = _expert_tiling(Bs * L_lat)
    xmid_s, h2_s, gate_s, dest_s, meta_s = _mix(
        xs2, mod, lambda i: 1 + i, cache, _rope_tables(L_lat), mix_w,
        S=1, L=L_lat, emit_kv=False, blocks_per_step=1, tile=tiling_s[0])
    dest_s = dest_s.reshape(Bs * L_lat)
    sh_s, sg_s = _moe_dispatch(h2_s, gate_s, dest_s, tiling_s)

    moe_p = _moe_unpermute(_experts(sh_p, sg_p, meta_p, wgu, wd, tiling_p), dest_p)
    moe_s = _moe_unpermute(_experts(sh_s, sg_s, meta_s, wgu, wd, tiling_s), dest_s)
    y_prompt = _final(xmid_p, moe_p, mod, lambda i: 0, gf)
    blocks_per_seq = L_lat // FINAL_BLOCK
    y_sample = _final(xmid_s, moe_s, mod, lambda i: 1 + i // blocks_per_seq, gf)

    return (y_prompt.reshape(B, L_ctx, D_MODEL), y_sample.reshape(Bs, L_lat, D_MODEL),
            knew.reshape(B, 1, L_ctx, N_KV_HEADS, HEAD_DIM), vnew.reshape(B, 1, L_ctx, N_KV_HEADS, HEAD_DIM))
```

```python
import functools

import numpy as np
import jax
import jax.numpy as jnp
from jax import lax
from jax.experimental import pallas as pl
from jax.experimental.pallas import tpu as pltpu
from jax.experimental.pallas import tpu_sc as plsc

F32 = jnp.float32
BF16 = jnp.bfloat16
I32 = jnp.int32
U32 = jnp.uint32

D_MODEL = 1024
HEAD_DIM = 128
N_HEADS = 8
N_KV_HEADS = 2
GROUP = N_HEADS // N_KV_HEADS
ATTN_W = N_HEADS * HEAD_DIM
KV_W = N_KV_HEADS * HEAD_DIM
POOL_WINDOWS = (2, 4, 8, 16)
POOL_GC = 128
POOL_W = POOL_GC * len(POOL_WINDOWS)
IN_W = ATTN_W + 2 * KV_W + POOL_W + 2 * D_MODEL
GATE_COL = ATTN_W + 2 * KV_W + POOL_W
GRID_W = 64
ROPE_THETA = 10000.0
ROPE_NF = HEAD_DIM // 4
N_EXP_GROUPS = 4
EXP_PER_GROUP = 4
N_EXPERTS = 16
D_EXPERT = 256
EPS = 1e-6
LOG2_E = 1.4426950408889634

LANES = 128
SUBLANES = 8
COND_ROWS = SUBLANES
POOL_HALO = 8
ROW_BLOCK = 256
ADA_COLS = 768
EXPERT_LANE0 = N_EXP_GROUPS
PAIRS_PER_GROUP = EXP_PER_GROUP * (EXP_PER_GROUP - 1) // 2
N_BUCKETS = N_EXP_GROUPS * PAIRS_PER_GROUP
EXPERT_STEP_ROWS = 1536
TOKEN_BLOCK = 1024
FINAL_BLOCK = 1024
ROW_CHUNKS = D_MODEL // LANES
SC_CORES = 2
SC_SUBCORES = 16
SC_WORKERS = SC_CORES * SC_SUBCORES
SC_LANES = 16
SC_PIECES_PER_GATHER = 128
SC_ROWS_PER_STEP = 64
PACKED_CHUNKS = ROW_CHUNKS // 2
V7X_VMEM_LIMIT_BYTES = 56 * 1024 * 1024


def _sigmoid(x):
    return 1.0 / (1.0 + jnp.exp(-x))


def _rms(x):
    return x * lax.rsqrt(jnp.mean(x * x, axis=-1, keepdims=True) + EPS)


def _resident(shape):
    zeros = (0,) * len(shape)
    return pl.BlockSpec(shape, lambda i, *_: zeros, pipeline_mode=pl.Buffered(1))


def _tiles_shape(n, chunks=ROW_CHUNKS):
    return (n // SUBLANES, chunks, SUBLANES, LANES)


def _tiles_spec(n, block_index, chunks=ROW_CHUNKS):
    return pl.BlockSpec(_tiles_shape(n, chunks), lambda *a: (block_index(*a), 0, 0, 0))


def _store_tiles(ref, x):
    for c in range(ref.shape[1]):
        ref[:, c, :, :] = x[:, c * LANES:(c + 1) * LANES].reshape(x.shape[0] // SUBLANES, SUBLANES, LANES)


def _load_tiles(ref):
    n = ref.shape[0] * SUBLANES
    return jnp.concatenate([ref[:, c, :, :].reshape(n, LANES) for c in range(ref.shape[1])], axis=1)


def _pack_bf16_pairs(x):
    bits = pltpu.bitcast(x.astype(BF16).astype(F32), U32)
    w = x.shape[1] // 2
    return bits[:, :w] | (bits[:, w:] >> 16)


def _unpack_bf16_pairs(words):
    hi = pltpu.bitcast(words & jnp.uint32(0xFFFF0000), F32).astype(BF16)
    lo = pltpu.bitcast(words << 16, F32).astype(BF16)
    return jnp.concatenate([hi, lo], axis=1)


def _row(x):
    return jnp.transpose(jnp.broadcast_to(x, (x.shape[0], LANES)))[0:1, :]


def _ada_kernel(c_ref, w_ref, b_ref, *refs, steps_per_pool_group):
    n_cast = len(refs) // 2 - 1
    c = c_ref[...]
    s = (c * _sigmoid(c)).astype(BF16)
    refs[n_cast + 1][...] = jnp.dot(s, w_ref[...].astype(BF16), preferred_element_type=F32) + b_ref[...]
    for src, dst in zip(refs[:n_cast], refs[n_cast + 2:]):
        dst[...] = src[...].astype(BF16)
    pool_src, pool_dst = refs[n_cast], refs[-1]
    wide = jnp.concatenate([pool_src[0]] * len(POOL_WINDOWS), axis=1)
    lane_group = lax.broadcasted_iota(I32, wide.shape, 1) // POOL_GC
    pool_dst[...] = jnp.where(lane_group == pl.program_id(0) // steps_per_pool_group, wide, 0.0).astype(BF16)


def _ada(cond, w_ada, b_ada, w_pool, cast=()):
    n = w_ada.shape[1]
    n_steps = n // ADA_COLS
    cast_specs = []
    for w in cast:
        assert w.ndim == 2 and w.shape[0] % (n_steps * 2 * SUBLANES) == 0
        cast_specs.append(pl.BlockSpec((w.shape[0] // n_steps, w.shape[1]), lambda j: (j, 0)))
    pool_rows = POOL_W // n_steps
    spg = POOL_GC // pool_rows
    assert w_pool.shape == (len(POOL_WINDOWS), POOL_GC, POOL_GC) and POOL_GC % pool_rows == 0
    assert pool_rows % (2 * SUBLANES) == 0
    pool_in = pl.BlockSpec((1, pool_rows, POOL_GC), lambda j: (j // spg, j % spg, 0))
    pool_out = pl.BlockSpec((pool_rows, POOL_W), lambda j: (j, 0))
    return pl.pallas_call(
        functools.partial(_ada_kernel, steps_per_pool_group=spg),
        grid=(n_steps,),
        in_specs=[
            pl.BlockSpec((COND_ROWS, D_MODEL), lambda j: (0, 0)),
            pl.BlockSpec((D_MODEL, ADA_COLS), lambda j: (0, j)),
            pl.BlockSpec((1, ADA_COLS), lambda j: (0, j)),
        ] + cast_specs + [pool_in],
        out_specs=[pl.BlockSpec((COND_ROWS, ADA_COLS), lambda j: (0, j))] + cast_specs + [pool_out],
        out_shape=[jax.ShapeDtypeStruct((COND_ROWS, n), F32)] + [jax.ShapeDtypeStruct(w.shape, BF16) for w in cast]
        + [jax.ShapeDtypeStruct((POOL_W, POOL_W), BF16)],
        name="ada_mod",
    )(cond, w_ada, b_ada, *cast, w_pool)


def _route(logits):
    lane = lax.broadcasted_iota(I32, logits.shape, 1).astype(F32)
    neg = jnp.float32(-1e30)
    far = jnp.float32(LANES)
    is_g = lane < N_EXP_GROUPS
    gl = jnp.where(is_g, logits, neg)
    gmax = jnp.max(gl, axis=-1, keepdims=True)
    gsel = jnp.min(jnp.where(gl == gmax, lane, far), axis=-1, keepdims=True)
    psel = 1.0 / jnp.sum(jnp.where(is_g, jnp.exp(gl - gmax), 0.0), axis=-1, keepdims=True)
    e_lo = EXPERT_LANE0 + EXP_PER_GROUP * gsel
    el = jnp.where(lane >= e_lo, jnp.where(lane < e_lo + EXP_PER_GROUP, logits, neg), neg)
    v1 = jnp.max(el, axis=-1, keepdims=True)
    i1 = jnp.min(jnp.where(el == v1, lane, far), axis=-1, keepdims=True)
    el2 = jnp.where(lane == i1, neg, el)
    v2 = jnp.max(el2, axis=-1, keepdims=True)
    i2 = jnp.min(jnp.where(el2 == v2, jnp.where(lane == i1, far, lane), far), axis=-1, keepdims=True)
    e2 = jnp.exp(v2 - v1)
    w1 = psel / (1.0 + e2)
    w2 = psel * e2 / (1.0 + e2)
    gate = jnp.where(lane == i1, w1, jnp.where(lane == i2, w2, 0.0))
    a = jnp.minimum(i1, i2) - e_lo
    b = jnp.maximum(i1, i2) - e_lo
    pair = a * (7.0 - a) * 0.5 + (b - a - 1.0)
    return gate, gsel * PAIRS_PER_GROUP + pair


def _mix_kernel(*refs, S, L, P, use_rope, emit_kv, n_cast, n_blocks, U, mod_row, tile):
    it = iter(refs)
    x_ref = next(it)
    mod_ref = next(it)
    if P:
        ck_ref = next(it)
        cv_ref = next(it)
    if use_rope:
        cos_ref = next(it)
        sneg_ref = next(it)
        spos_ref = next(it)
    (g1_ref, win_ref, qg_ref, kg_ref, wpool_hbm, pscale_ref, wa_hbm, wb_hbm, wo_hbm,
     g2_ref, wr_ref, win_hbm) = (next(it) for _ in range(12))
    cast_in = [[next(it) for _ in range(n)] for n in n_cast]
    xmid_ref = next(it)
    h2_ref = next(it)
    gate_ref = next(it)
    dest_ref = next(it)
    meta_ref = next(it)
    if emit_kv:
        knew_ref = next(it)
        vnew_ref = next(it)
    cast_out = [next(it) for _ in n_cast]
    q_s, k_s, v_s, xp_s, h_s, attn_s, xm_s, mod2_s, oh_s = (next(it) for _ in range(9))
    wpool_ref, wa_ref, wb_ref, wo_ref, wgate_ref, late_sems = (next(it) for _ in range(6))
    late_copies = [pltpu.make_async_copy(src, dst, late_sems.at[i]) for i, (src, dst) in enumerate(
        ((wa_hbm, wa_ref), (wpool_hbm, wpool_ref), (wb_hbm, wb_ref),
         (win_hbm.at[:, pl.ds(GATE_COL, IN_W - GATE_COL)], wgate_ref), (wo_hbm, wo_ref)))]

    TM = S * L
    RB = ROW_BLOCK
    nrb = TM // RB
    n_steps = n_blocks // U
    score_gain = HEAD_DIM ** -0.5 * LOG2_E
    step = pl.program_id(0)
    block0 = U * jnp.minimum(step, n_steps - 1)
    slot = step % 2

    mod_at = pl.ds(mod_row(jnp.minimum(step, n_steps - 1) // (nrb // U)), 1)
    sh1 = mod_ref[mod_at, 0:D_MODEL]
    gain1 = g1_ref[...] * (1.0 + mod_ref[mod_at, D_MODEL:2 * D_MODEL])
    gt1 = mod_ref[mod_at, 2 * D_MODEL:3 * D_MODEL]
    sh2 = mod_ref[mod_at, 3 * D_MODEL:4 * D_MODEL]
    gain2 = g2_ref[...] * (1.0 + mod_ref[mod_at, 4 * D_MODEL:5 * D_MODEL])
    qg = qg_ref[...] * score_gain
    kg = kg_ref[...]

    def project(r, carry):
        r0 = pl.multiple_of(r * RB, RB)
        s = r0 // L
        o = pl.multiple_of(r0 % L, RB)
        hb = (_rms(x_ref[pl.ds(r0, RB), :]) * gain1 + sh1).astype(BF16)
        h_s[pl.ds(r0, RB), :] = hb
        p1 = jnp.dot(hb, win_ref[...], preferred_element_type=F32)
        if use_rope:
            cs = cos_ref[pl.ds(o, RB), :]
            sn = sneg_ref[pl.ds(o, RB), :]
            sp = spos_ref[pl.ds(o, RB), :]

        def rope(t):
            return (t * cs + pltpu.roll(t, HEAD_DIM - ROPE_NF, 1) * sn + pltpu.roll(t, ROPE_NF, 1) * sp)

        for hd in range(N_HEADS):
            qh = _rms(p1[:, hd * HEAD_DIM:(hd + 1) * HEAD_DIM]) * qg
            if use_rope:
                qh = rope(qh)
            q_s[hd, pl.ds(r0, RB), :] = qh.astype(BF16)
        for kh in range(N_KV_HEADS):
            c0 = ATTN_W + kh * HEAD_DIM
            kk = _rms(p1[:, c0:c0 + HEAD_DIM]) * kg
            if emit_kv:
                knew_ref[pl.ds(N_KV_HEADS * r0 + kh, RB, stride=N_KV_HEADS), :] = kk
            if use_rope:
                kk = rope(kk)
            k_s[s, pl.ds(P + o, RB), kh * HEAD_DIM:(kh + 1) * HEAD_DIM] = kk.astype(BF16)
        vv = p1[:, ATTN_W + KV_W:ATTN_W + 2 * KV_W]
        if emit_kv:
            for kh in range(N_KV_HEADS):
                vnew_ref[pl.ds(N_KV_HEADS * r0 + kh, RB, stride=N_KV_HEADS), :] = (
                    vv[:, kh * HEAD_DIM:(kh + 1) * HEAD_DIM])
        v_s[s, pl.ds(P + o, RB), :] = vv.astype(BF16)
        xp_s[s, pl.ds(POOL_HALO + o, RB), :] = p1[:, ATTN_W + 2 * KV_W:GATE_COL]
        return carry

    @pl.when(step == 0)
    def _():
        xm_s[1] = jnp.zeros((U * RB, D_MODEL), F32)
        mod2_s[1] = jnp.zeros((2, D_MODEL), F32)
        for cp in late_copies:
            cp.start()

    @pl.when((step < n_steps) & (step % (nrb // U) == 0))
    def _():
        if P:
            for kh in range(N_KV_HEADS):
                cols = slice(kh * HEAD_DIM, (kh + 1) * HEAD_DIM)
                k_s[0, 0:P, cols] = ck_ref[pl.ds(kh, P, stride=N_KV_HEADS), :].astype(BF16)
                v_s[0, 0:P, cols] = cv_ref[pl.ds(kh, P, stride=N_KV_HEADS), :].astype(BF16)
        xp_s[:, 0:POOL_HALO, :] = jnp.zeros((S, POOL_HALO, POOL_W), F32)
        xp_s[:, L + POOL_HALO:L + 2 * POOL_HALO, :] = jnp.zeros((S, POOL_HALO, POOL_W), F32)
        lax.fori_loop(0, TM // RB, project, 0)
        for srcs, dst in zip(cast_in, cast_out):
            col = 0
            for src in srcs:
                dst[..., col:col + src.shape[-1]] = src[...].astype(BF16)
                col += src.shape[-1]

    @pl.when(step == 0)
    def _():
        for cp in late_copies:
            cp.wait()

    def mix(u):
        r0 = pl.multiple_of(((block0 + u) % nrb) * RB, RB)
        s = r0 // L
        o = pl.multiple_of(r0 % L, RB)
        attn_u = attn_s.at[u]
        rows = slice(u * RB, (u + 1) * RB)

        for hd in range(N_HEADS):
            kh = hd // GROUP
            k = k_s[s, :, kh * HEAD_DIM:(kh + 1) * HEAD_DIM]
            v = v_s[s, :, kh * HEAD_DIM:(kh + 1) * HEAD_DIM]
            qh = q_s[hd, pl.ds(r0, RB), :]
            sc = lax.dot_general(qh, k, (((1,), (1,)), ((), ())), preferred_element_type=F32)
            e = jnp.exp2(sc - jnp.max(sc, axis=-1, keepdims=True))
            den = jnp.sum(e, axis=-1, keepdims=True)
            oh = jnp.dot(e.astype(BF16), v, preferred_element_type=F32) / den
            attn_u[:, hd * HEAD_DIM:(hd + 1) * HEAD_DIM] = oh.astype(BF16)
        a = jnp.dot(attn_u[...], wa_ref[...], preferred_element_type=F32)

        t = o + lax.broadcasted_iota(I32, (RB, 1), 0)
        RW = RB + 2 * POOL_HALO
        parts = []
        for gi, w in enumerate(POOL_WINDOWS):
            cols = slice(gi * POOL_GC, (gi + 1) * POOL_GC)
            xw = xp_s[s, pl.ds(o, RW), cols]
            run = xw
            span = 1
            while span < w:
                run = run + pltpu.roll(run, span, 0)
                span *= 2
            if w // 2 > 1:
                run = pltpu.roll(run, RW - (w // 2 - 1), 0)
            tot = run[POOL_HALO:POOL_HALO + RB]
            cnt = (jnp.minimum(t + w // 2, L) - jnp.maximum(t - w // 2, 0)).astype(F32)
            parts.append(tot / cnt - xw[POOL_HALO:POOL_HALO + RB])
        dpool = jnp.concatenate(parts, axis=1).astype(BF16)
        pooled = jnp.dot(dpool, wpool_ref[...], preferred_element_type=F32) * pscale_ref[...]
        b = jnp.dot(pooled.astype(BF16), wb_ref[...], preferred_element_type=F32)

        gates = jnp.dot(h_s[pl.ds(r0, RB), :], wgate_ref[...], preferred_element_type=F32)
        merged = _sigmoid(gates[:, 0:D_MODEL]) * a + _sigmoid(gates[:, D_MODEL:2 * D_MODEL]) * b
        upd = jnp.dot(merged.astype(BF16), wo_ref[...], preferred_element_type=F32)
        xm = x_ref[pl.ds(r0, RB), :] + gt1 * upd
        xmid_ref[rows, :] = xm
        xm_s[slot, rows, :] = xm

    def moe_prep(u):
        rows = slice(u * RB, (u + 1) * RB)
        h2 = _rms(xm_s[1 - slot, rows, :]) * mod2_s[1 - slot, 0:1, :] + mod2_s[1 - slot, 1:2, :]
        hi = h2.astype(BF16)
        lo = (h2 - hi.astype(F32)).astype(BF16)
        l1 = jnp.dot(hi, wr_ref[...], preferred_element_type=F32)
        l2 = jnp.dot(lo, wr_ref[:, 0:LANES], preferred_element_type=F32)
        gate, bucket = _route(l1[:, 0:LANES] + l1[:, LANES:2 * LANES] + l2)
        groups = pl.ds(u * (RB // SUBLANES), RB // SUBLANES)
        _store_tiles(h2_ref.at[groups], _pack_bf16_pairs(h2))
        gate_ref[rows, :] = gate
        lane = lax.broadcasted_iota(I32, (RB, LANES), 1).astype(F32)
        first = pl.multiple_of((U * jnp.maximum(step - 1, 0) + u) * RB, RB)
        oh_s[pl.ds(first, RB), :] = jnp.where(lane == bucket, 1.0, 0.0).astype(BF16)

    mod2_s[slot, 0:1, :] = gain2
    mod2_s[slot, 1:2, :] = sh2

    @pl.when(step < n_steps)
    def _():
        for u in range(U):
            moe_prep(u)
        for u in range(U):
            mix(u)

    @pl.when(step == n_steps)
    def _():
        for u in range(U):
            moe_prep(u)
        _plan_rows(oh_s, dest_ref, meta_ref, n_blocks=n_blocks * RB // TOKEN_BLOCK, tile=tile)


def _mix(x2d, mod, mod_row, cache, rope_tabs, weights, *, S, L, emit_kv, blocks_per_step, tile, cast=()):
    T = x2d.shape[0]
    TM = S * L
    P = cache[0].shape[0] // (T // L * N_KV_HEADS) if cache is not None else 0
    use_rope = rope_tabs is not None
    assert T % TM == 0 and L % ROW_BLOCK == 0
    assert not (use_rope or P) or S == 1
    Lk = P + L

    args = [x2d, mod]
    nrb = TM // ROW_BLOCK
    n_blocks = T // ROW_BLOCK
    step_rows = blocks_per_step * ROW_BLOCK
    steps_per_group = nrb // blocks_per_step
    n_mix_steps = n_blocks // blocks_per_step
    assert nrb % blocks_per_step == 0

    def mixed(s):
        return jnp.minimum(s, n_mix_steps - 1)

    def group(s):
        return mixed(s) // steps_per_group

    def prepared(s):
        return jnp.maximum(s - 1, 0)

    in_specs = [
        pl.BlockSpec((TM, D_MODEL), lambda s: (group(s), 0)),
        _resident(mod.shape),
    ]
    if P:
        args += list(cache)
        in_specs += [pl.BlockSpec((P * N_KV_HEADS, HEAD_DIM), lambda s: (group(s), 0))] * 2
    if use_rope:
        args += list(rope_tabs)
        in_specs += [_resident((L, HEAD_DIM))] * 3
    args += list(weights)
    late = (4, 6, 7, 8)
    in_specs += [pl.BlockSpec(memory_space=pl.ANY) if i in late else _resident(w.shape)
                 for i, w in enumerate(weights)]
    assert weights[1].shape == (D_MODEL, IN_W)
    in_specs[len(in_specs) - len(weights) + 1] = _resident((D_MODEL, GATE_COL))
    args.append(weights[1])
    in_specs.append(pl.BlockSpec(memory_space=pl.ANY))
    n_steps = T // TM
    def per_group(shape):
        assert shape[0] % n_steps == 0
        blk = (shape[0] // n_steps,) + shape[1:]
        return pl.BlockSpec(blk, lambda s, n=len(blk): (group(s),) + (0,) * (n - 1))

    cast_out_shapes = [ws[0].shape[:-1] + (sum(w.shape[-1] for w in ws),) for ws in cast]
    for ws in cast:
        args += list(ws)
        in_specs += [per_group(w.shape) for w in ws]

    assert T % TOKEN_BLOCK == 0
    out_shape = [jax.ShapeDtypeStruct((T, D_MODEL), F32), jax.ShapeDtypeStruct(_tiles_shape(T, PACKED_CHUNKS), U32),
                 jax.ShapeDtypeStruct((T, LANES), F32),
                 jax.ShapeDtypeStruct((T // TOKEN_BLOCK, 1, TOKEN_BLOCK), I32),
                 jax.ShapeDtypeStruct((SUBLANES, LANES), I32)]
    out_specs = [pl.BlockSpec((step_rows, D_MODEL), lambda s: (mixed(s), 0)),
                 _tiles_spec(step_rows, prepared, PACKED_CHUNKS),
                 pl.BlockSpec((step_rows, LANES), lambda s: (prepared(s), 0)),
                 pl.BlockSpec((T // TOKEN_BLOCK, 1, TOKEN_BLOCK), lambda s: (0, 0, 0)),
                 pl.BlockSpec((SUBLANES, LANES), lambda s: (0, 0))]
    if emit_kv:
        out_shape += [jax.ShapeDtypeStruct((T * N_KV_HEADS, HEAD_DIM), F32)] * 2
        out_specs += [pl.BlockSpec((TM * N_KV_HEADS, HEAD_DIM), lambda s: (group(s), 0))] * 2
    out_shape += [jax.ShapeDtypeStruct(shp, BF16) for shp in cast_out_shapes]
    out_specs += [per_group(shp) for shp in cast_out_shapes]

    scratch = [
        pltpu.VMEM((N_HEADS, TM, HEAD_DIM), BF16),
        pltpu.VMEM((S, Lk, KV_W), BF16),
        pltpu.VMEM((S, Lk, KV_W), BF16),
        pltpu.VMEM((S, L + 2 * POOL_HALO, POOL_W), F32),
        pltpu.VMEM((TM, D_MODEL), BF16),
        pltpu.VMEM((blocks_per_step, ROW_BLOCK, ATTN_W), BF16),
        pltpu.VMEM((2, step_rows, D_MODEL), F32),
        pltpu.VMEM((2, 2, D_MODEL), F32),
        pltpu.VMEM((T, LANES), BF16),
    ] + [pltpu.VMEM(weights[i].shape, weights[i].dtype) for i in late] + [
        pltpu.VMEM((D_MODEL, IN_W - GATE_COL), BF16), pltpu.SemaphoreType.DMA((len(late) + 1,))]
    kern = functools.partial(_mix_kernel, S=S, L=L, P=P, use_rope=use_rope, emit_kv=emit_kv,
                             n_cast=tuple(len(ws) for ws in cast), n_blocks=n_blocks, U=blocks_per_step,
                             mod_row=mod_row, tile=tile)
    return pl.pallas_call(
        kern,
        grid=(n_mix_steps + 1,),
        in_specs=in_specs,
        out_specs=out_specs,
        out_shape=out_shape,
        scratch_shapes=scratch,
        compiler_params=pltpu.CompilerParams(
            dimension_semantics=("arbitrary",), vmem_limit_bytes=V7X_VMEM_LIMIT_BYTES),
        name="mixer_rope" if use_rope else "mixer_ctx",
    )(*args)


def _plan_rows(oh_ref, dest_ref, meta_ref, *, n_blocks, tile):
    TB = TOKEN_BLOCK
    lane = lax.broadcasted_iota(I32, (SUBLANES, LANES), 1)

    def count(b, acc):
        oh = oh_ref[pl.ds(pl.multiple_of(b * TB, TB), TB), :].astype(F32)
        return acc + jnp.sum(oh, axis=0, keepdims=True)

    counts = lax.fori_loop(0, n_blocks, count, jnp.zeros((SUBLANES, LANES), F32))
    padded = jnp.floor((counts + (tile - 0.5)) * (1.0 / tile)) * tile
    ends = padded
    step = 1
    while step < LANES:
        ends = ends + jnp.where(lane >= step, pltpu.roll(ends, step, 1), 0.0)
        step *= 2
    starts = ends - padded

    tri = jnp.where(lax.broadcasted_iota(I32, (TB, TB), 1) < lax.broadcasted_iota(I32, (TB, TB), 0),
                    1.0, 0.0).astype(BF16)

    def place(b, seen):
        oh = oh_ref[pl.ds(pl.multiple_of(b * TB, TB), TB), :]
        ohf = oh.astype(F32)
        rank = jnp.dot(tri, oh, preferred_element_type=F32)
        base = (starts + seen)[0:1, :]
        d = jnp.sum(ohf * (rank + base), axis=1, keepdims=True)
        dest_ref[b] = _row(d).astype(I32)
        return seen + jnp.sum(ohf, axis=0, keepdims=True)

    lax.fori_loop(0, n_blocks, place, jnp.zeros((SUBLANES, LANES), F32))

    tile_row0 = lax.broadcasted_iota(I32, (LANES, LANES), 0).astype(F32) * tile
    is_bucket = lax.broadcasted_iota(I32, (LANES, LANES), 1) < N_BUCKETS
    done = jnp.sum(jnp.where(is_bucket, jnp.where(ends[0:1, :] <= tile_row0, 1.0, 0.0), 0.0),
                   axis=1, keepdims=True)
    bkt = jnp.minimum(done, N_BUCKETS - 1.0)
    grp = (jnp.where(bkt >= PAIRS_PER_GROUP, 1.0, 0.0) + jnp.where(bkt >= 2 * PAIRS_PER_GROUP, 1.0, 0.0)
           + jnp.where(bkt >= 3 * PAIRS_PER_GROUP, 1.0, 0.0))
    pair = bkt - PAIRS_PER_GROUP * grp
    a = jnp.where(pair >= 3.0, 1.0, 0.0) + jnp.where(pair >= 5.0, 1.0, 0.0)
    b = pair - a * (7.0 - a) * 0.5 + a + 1.0
    e1 = EXP_PER_GROUP * grp + a
    e2 = EXP_PER_GROUP * grp + b
    meta = jnp.concatenate(
        [_row(e1), _row(e2), jnp.floor(ends[0:1, :] * (1.0 / tile) + 0.5),
         jnp.zeros((SUBLANES - 3, LANES), F32)], axis=0)
    meta_ref[...] = meta.astype(I32)


def _sc_move_rows(src_v, table_hbm, out_hbm, lo, n_rows, idx_v, pieces_v, sem):
    chunks = pieces_v.shape[0] // SC_ROWS_PER_STEP
    lane = lax.iota(I32, SC_LANES)
    row_in_group = lane & (SUBLANES - 1)
    chunk_in_pair = lane >> 3
    rows_per_gather = SC_PIECES_PER_GATHER // chunks

    @pl.loop(0, n_rows // SC_ROWS_PER_STEP)
    def _(step):
        copies = []
        for g in range(SC_ROWS_PER_STEP // rows_per_gather):
            r0 = step * SC_ROWS_PER_STEP + g * rows_per_gather
            for v in range(SC_PIECES_PER_GATHER // SC_LANES):
                group, chunk0 = v // (chunks // 2), 2 * (v % (chunks // 2))
                tok = plsc.load_gather(src_v, [r0 + group * SUBLANES + row_in_group])
                piece = (tok >> 3) * (SUBLANES * chunks) + (chunk0 + chunk_in_pair) * SUBLANES + (tok & 7)
                idx_v[pl.ds(g * SC_PIECES_PER_GATHER + v * SC_LANES, SC_LANES)] = piece
            window = pl.ds(g * SC_PIECES_PER_GATHER, SC_PIECES_PER_GATHER)
            copies.append(pltpu.async_copy(table_hbm.at[idx_v.at[window]], pieces_v.at[window], sem))
        for cp in copies:
            cp.wait()
        first = pl.multiple_of((lo + step * SC_ROWS_PER_STEP) * chunks, SC_ROWS_PER_STEP * chunks)
        pltpu.sync_copy(pieces_v, out_hbm.at[pl.ds(first, SC_ROWS_PER_STEP * chunks)])


def _sc_scratch(chunks, dtype):
    return [pltpu.VMEM((SC_ROWS_PER_STEP * chunks,), I32), pltpu.VMEM((SC_ROWS_PER_STEP * chunks, LANES), dtype)]


def _sc_dispatch(h2_flat, gate_rows, dest, n_rows):
    T = dest.shape[0]
    per_worker = n_rows // SC_WORKERS
    rows_per_step = SC_ROWS_PER_STEP
    chunks = h2_flat.shape[0] // T
    assert n_rows % SC_WORKERS == 0 and per_worker % rows_per_step == 0 and T % SC_LANES == 0
    mesh = plsc.VectorSubcoreMesh(core_axis_name="c", subcore_axis_name="s")

    @functools.partial(
        pl.kernel, mesh=mesh,
        out_type=[jax.ShapeDtypeStruct((n_rows * chunks, LANES), h2_flat.dtype),
                  jax.ShapeDtypeStruct((n_rows, LANES), F32)],
        scratch_types=[pltpu.VMEM((T,), I32), pltpu.VMEM((per_worker,), I32)]
        + _sc_scratch(chunks, h2_flat.dtype)
        + [pltpu.VMEM((rows_per_step, LANES), F32), pltpu.SemaphoreType.DMA, pltpu.SemaphoreType.DMA],
        compiler_params=pltpu.CompilerParams(use_tc_tiling_on_sc=True, needs_layout_passes=False),
        name="sc_dispatch",
    )
    def dispatch(h2_hbm, gate_hbm, dest_hbm, out_h_hbm, out_g_hbm,
                 dest_v, src_v, idx_v, pieces_v, gates_v, sem_h, sem_g):
        worker = lax.axis_index("s") * SC_CORES + lax.axis_index("c")
        lo = worker * per_worker
        pltpu.sync_copy(dest_hbm, dest_v)

        @pl.loop(0, per_worker // SC_LANES)
        def _(j):
            j0 = pl.multiple_of(j * SC_LANES, SC_LANES)
            src_v[pl.ds(j0, SC_LANES)] = lax.rem(lo + j0 + lax.iota(I32, SC_LANES), T)

        @pl.loop(0, T // SC_LANES)
        def _(j):
            t0 = pl.multiple_of(j * SC_LANES, SC_LANES)
            d = dest_v[pl.ds(t0, SC_LANES)] - lo
            mine = (d >= 0) & (d < per_worker)
            plsc.store_scatter(src_v, [jnp.where(mine, d, 0)], t0 + lax.iota(I32, SC_LANES), mask=mine)

        @pl.loop(0, per_worker // rows_per_step)
        def _(j):
            off = pl.multiple_of(j * rows_per_step, rows_per_step)
            pltpu.async_copy(gate_hbm.at[src_v.at[pl.ds(off, rows_per_step)]], gates_v, sem_g).wait()
            pltpu.sync_copy(gates_v, out_g_hbm.at[pl.ds(lo + off, rows_per_step)])

        _sc_move_rows(src_v, h2_hbm, out_h_hbm, lo, per_worker, idx_v, pieces_v, sem_h)

    return dispatch(h2_flat, gate_rows, dest)


def _expert_kernel(meta, x_ref, gv_ref, wgu_hbm, wd_hbm, o_ref, wgu_ref, wd_ref, ready_s, sems, *, tile, per_step):
    groups = tile // SUBLANES
    n_used = meta[2, LANES - 1]
    step = pl.program_id(0)

    def weight_copies(g):
        experts = pl.ds(g * EXP_PER_GROUP, EXP_PER_GROUP)
        return (pltpu.make_async_copy(wgu_hbm.at[experts], wgu_ref.at[experts], sems.at[g]),
                pltpu.make_async_copy(wd_hbm.at[experts], wd_ref.at[experts], sems.at[g]))

    def land_through(last_group):
        landed = ready_s[0]
        for g in range(N_EXP_GROUPS):
            @pl.when((g >= landed) & (g <= last_group))
            def _():
                for cp in weight_copies(g):
                    cp.wait()
                if g + 1 < N_EXP_GROUPS:
                    for cp in weight_copies(g + 1):
                        cp.start()
        ready_s[0] = jnp.maximum(landed, last_group + 1)

    @pl.when(step == 0)
    def _():
        ready_s[0] = 0
        for cp in weight_copies(0):
            cp.start()

    def one_tile(k, t):
        rows = pl.ds(pl.multiple_of(k * groups, groups), groups)
        x = _unpack_bf16_pairs(_load_tiles(x_ref.at[rows]))
        gv = gv_ref[pl.ds(pl.multiple_of(k * tile, SUBLANES), tile), :]
        lane = lax.broadcasted_iota(I32, gv.shape, 1)
        out = None
        for e in (meta[0, t], meta[1, t]):
            ge = jnp.sum(jnp.where(lane == EXPERT_LANE0 + e, gv, 0.0), axis=-1, keepdims=True)
            h = jnp.dot(x, wgu_ref[e], preferred_element_type=F32)
            hg = h[:, 0:D_EXPERT]
            hid = (hg * _sigmoid(hg) * h[:, D_EXPERT:2 * D_EXPERT] * ge).astype(BF16)
            y = jnp.dot(hid, wd_ref[e], preferred_element_type=F32)
            out = y if out is None else out + y
        _store_tiles(o_ref.at[rows], _pack_bf16_pairs(out))

    def two_tiles(j, carry):
        k = 2 * j
        t = step * per_step + k

        @pl.when(t + 1 < n_used)
        def _():
            land_through(meta[0, t + 1] // EXP_PER_GROUP)
            one_tile(k, t)
            one_tile(k + 1, t + 1)

        @pl.when(t + 1 == n_used)
        def _():
            land_through(meta[0, t] // EXP_PER_GROUP)
            one_tile(k, t)

        return carry

    assert per_step % 2 == 0
    lax.fori_loop(0, per_step // 2, two_tiles, 0)

    @pl.when(step == pl.num_programs(0) - 1)
    def _():
        land_through(N_EXP_GROUPS - 1)


def _experts(sorted_h2, sorted_gates, meta, wgu, wd, tiling):
    tile, per_step, n_tiles = tiling
    step_rows = tile * per_step
    assert n_tiles * tile == sorted_h2.shape[0] * SUBLANES and n_tiles % per_step == 0

    def last_used(i, meta):
        return jnp.minimum(i, (meta[2, LANES - 1] - 1) // per_step)

    return pl.pallas_call(
        functools.partial(_expert_kernel, tile=tile, per_step=per_step),
        grid_spec=pltpu.PrefetchScalarGridSpec(
            num_scalar_prefetch=1,
            grid=(n_tiles // per_step,),
            in_specs=[
                _tiles_spec(step_rows, last_used, PACKED_CHUNKS),
                pl.BlockSpec((step_rows, LANES), lambda *a: (last_used(*a), 0)),
                pl.BlockSpec(memory_space=pl.ANY), pl.BlockSpec(memory_space=pl.ANY),
            ],
            out_specs=_tiles_spec(step_rows, last_used, PACKED_CHUNKS),
            scratch_shapes=[pltpu.VMEM(wgu.shape, wgu.dtype), pltpu.VMEM(wd.shape, wd.dtype),
                            pltpu.SMEM((1,), I32), pltpu.SemaphoreType.DMA((N_EXP_GROUPS,))],
        ),
        out_shape=jax.ShapeDtypeStruct(_tiles_shape(n_tiles * tile, PACKED_CHUNKS), U32),
        compiler_params=pltpu.CompilerParams(
            dimension_semantics=("arbitrary",), vmem_limit_bytes=V7X_VMEM_LIMIT_BYTES),
        name="moe_experts",
    )(meta, sorted_h2, sorted_gates, wgu, wd)


def _sc_row_gather(table_flat, idx, chunks):
    n = idx.shape[0]
    per_worker = n // SC_WORKERS
    assert n % SC_WORKERS == 0 and per_worker % SC_ROWS_PER_STEP == 0
    mesh = plsc.VectorSubcoreMesh(core_axis_name="c", subcore_axis_name="s")

    @functools.partial(
        pl.kernel, mesh=mesh,
        out_type=jax.ShapeDtypeStruct((n * chunks, LANES), table_flat.dtype),
        scratch_types=[pltpu.VMEM((per_worker,), I32)] + _sc_scratch(chunks, table_flat.dtype)
        + [pltpu.SemaphoreType.DMA],
        compiler_params=pltpu.CompilerParams(use_tc_tiling_on_sc=True, needs_layout_passes=False),
        name="sc_row_gather",
    )
    def gather(table_hbm, idx_hbm, out_hbm, src_v, idx_v, pieces_v, sem):
        worker = lax.axis_index("s") * SC_CORES + lax.axis_index("c")
        lo = worker * per_worker
        pltpu.sync_copy(idx_hbm.at[pl.ds(lo, per_worker)], src_v)
        _sc_move_rows(src_v, table_hbm, out_hbm, lo, per_worker, idx_v, pieces_v, sem)

    return gather(table_flat, idx)


def _final_kernel(x_ref, moe_ref, gt2_ref, gf_ref, o_ref, *, mod_row):
    gt2 = gt2_ref[pl.ds(mod_row(pl.program_id(0)), 1), :]
    y = x_ref[...] + gt2 * _unpack_bf16_pairs(_load_tiles(moe_ref)).astype(F32)
    o_ref[...] = _rms(y) * gf_ref[...]


def _final(xmid, moe_rows, mod, mod_row, gf):
    T = xmid.shape[0]
    return pl.pallas_call(
        functools.partial(_final_kernel, mod_row=mod_row),
        grid=(T // FINAL_BLOCK,),
        in_specs=[
            pl.BlockSpec((FINAL_BLOCK, D_MODEL), lambda i: (i, 0)),
            _tiles_spec(FINAL_BLOCK, lambda i: i, PACKED_CHUNKS),
            pl.BlockSpec((COND_ROWS, D_MODEL), lambda i: (0, 5)),
            pl.BlockSpec((1, D_MODEL), lambda i: (0, 0)),
        ],
        out_specs=pl.BlockSpec((FINAL_BLOCK, D_MODEL), lambda i: (i, 0)),
        out_shape=jax.ShapeDtypeStruct((T, D_MODEL), F32),
        compiler_params=pltpu.CompilerParams(
            dimension_semantics=("arbitrary",), vmem_limit_bytes=V7X_VMEM_LIMIT_BYTES),
        name="moe_final",
    )(xmid, moe_rows, mod, gf)


def _flat(tiles):
    return tiles.reshape(-1, LANES)


def _expert_tiling(T):
    tile = max(256, -(-(T * 9) // (8 * N_BUCKETS * 64)) * 64)
    per_step = max(2, EXPERT_STEP_ROWS // tile // 2 * 2)
    n_tiles = (T + N_BUCKETS * (tile - 1)) // tile
    while n_tiles % per_step or (n_tiles * tile) % (SC_WORKERS * SC_ROWS_PER_STEP):
        n_tiles += 1
    return tile, per_step, n_tiles


def _moe_dispatch(h2_tiles, gate_rows, dest, tiling):
    tile, _, n_tiles = tiling
    n_rows = n_tiles * tile
    assert n_tiles <= LANES
    sorted_h2, sorted_gates = _sc_dispatch(_flat(h2_tiles), gate_rows, dest, n_rows)
    return sorted_h2.reshape(_tiles_shape(n_rows, PACKED_CHUNKS)), sorted_gates


def _moe_unpermute(moe_sorted_tiles, dest):
    chunks = moe_sorted_tiles.shape[1]
    return _sc_row_gather(_flat(moe_sorted_tiles), dest, chunks).reshape(_tiles_shape(dest.shape[0], chunks))


def _rope_tables(n_tokens):
    t = np.arange(n_tokens)
    row = (t // GRID_W).astype(np.float32)
    col = (t % GRID_W).astype(np.float32)
    freq = np.float32(ROPE_THETA) ** (-np.arange(ROPE_NF, dtype=np.float32) / np.float32(ROPE_NF))
    ang = np.concatenate([row[:, None] * freq] * 2 + [col[:, None] * freq] * 2, axis=-1)
    first = (np.arange(HEAD_DIM) % (2 * ROPE_NF)) < ROPE_NF
    sin = np.sin(ang)
    zero = np.float32(0.0)
    return (jnp.asarray(np.cos(ang)), jnp.asarray(np.where(first, -sin, zero)),
            jnp.asarray(np.where(first, zero, sin)))


def kernel(x_prompt, x_sample, cache_k, cache_v, c, c_ctx, norm1_g, norm2_g, w_ada, b_ada, w_in, q_norm_g, k_norm_g, w_pool, pool_scale, w_branch_a, w_branch_b, w_out, w_router_group, w_router_expert, w_exp_gate, w_exp_up, w_exp_down, final_norm_g):
    assert norm1_g.shape[0] == 1, "single-layer trunk"
    B, L_ctx, _ = x_prompt.shape
    Bs, L_lat, _ = x_sample.shape
    P = cache_k.shape[2]
    assert 1 + Bs <= COND_ROWS

    cond = jnp.concatenate([c_ctx[None, :], c, jnp.zeros((COND_ROWS - 1 - Bs, D_MODEL), F32)], axis=0)
    mod, w_in_b, wa_b, wb_b, wo_b, wpool_b = _ada(
        cond, w_ada[0], b_ada[0][None, :], w_pool[0],
        cast=(w_in[0], w_branch_a[0], w_branch_b[0], w_out[0]))

    wr = jnp.concatenate([w_router_group[0], w_router_expert[0],
                          jnp.zeros((D_MODEL, LANES - N_EXP_GROUPS - N_EXPERTS), F32)], axis=1)
    wr_hi = wr.astype(BF16)
    wr_lo = (wr - wr_hi.astype(F32)).astype(BF16)
    mix_w = (norm1_g[0][None, :], w_in_b, q_norm_g[0][None, :], k_norm_g[0][None, :],
             wpool_b, pool_scale[0][None, :], wa_b, wb_b, wo_b,
             norm2_g[0][None, :], jnp.concatenate([wr_hi, wr_lo], axis=1))
    gf = final_norm_g[None, :]

    xp2 = x_prompt.reshape(B * L_ctx, D_MODEL)
    tiling_p = _expert_tiling(B * L_ctx)
    xmid_p, h2_p, gate_p, dest_p, meta_p, knew, vnew, wgu, wd = _mix(
        xp2, mod, lambda i: 0, None, None, mix_w, S=2, L=L_ctx, emit_kv=True, blocks_per_step=2,
        tile=tiling_p[0], cast=((w_exp_gate[0], w_exp_up[0]), (w_exp_down[0],)))
    dest_p = dest_p.reshape(B * L_ctx)
    sh_p, sg_p = _moe_dispatch(h2_p, gate_p, dest_p, tiling_p)

    xs2 = x_sample.reshape(Bs * L_lat, D_MODEL)
    cache = (cache_k.reshape(Bs * P * N_KV_HEADS, HEAD_DIM), cache_v.reshape(Bs * P * N_KV_HEADS, HEAD_DIM))
    tiling_s = _expert_tiling(Bs * L_lat)
    xmid_s, h2_s, gate_s, dest_s, meta_s = _mix(
        xs2, mod, lambda i: 1 + i, cache, _rope_tables(L_lat), mix_w,
        S=1, L=L_lat, emit_kv=False, blocks_per_step=1, tile=tiling_s[0])
    dest_s = dest_s.reshape(Bs * L_lat)
    sh_s, sg_s = _moe_dispatch(h2_s, gate_s, dest_s, tiling_s)

    moe_p = _moe_unpermute(_experts(sh_p, sg_p, meta_p, wgu, wd, tiling_p), dest_p)
    moe_s = _moe_unpermute(_experts(sh_s, sg_s, meta_s, wgu, wd, tiling_s), dest_s)
    y_prompt = _final(xmid_p, moe_p, mod, lambda i: 0, gf)
    blocks_per_seq = L_lat // FINAL_BLOCK
    y_sample = _final(xmid_s, moe_s, mod, lambda i: 1 + i // blocks_per_seq, gf)

    return (y_prompt.reshape(B, L_ctx, D_MODEL), y_sample.reshape(Bs, L_lat, D_MODEL),
            knew.reshape(B, 1, L_ctx, N_KV_HEADS, HEAD_DIM), vnew.reshape(B, 1, L_ctx, N_KV_HEADS, HEAD_DIM))
```

```python
import functools

import numpy as np
import jax
import jax.numpy as jnp
from jax import lax
from jax.experimental import pallas as pl
from jax.experimental.pallas import tpu as pltpu
from jax.experimental.pallas import tpu_sc as plsc

F32 = jnp.float32
BF16 = jnp.bfloat16
I32 = jnp.int32
U32 = jnp.uint32

D_MODEL = 1024
HEAD_DIM = 128
N_HEADS = 8
N_KV_HEADS = 2
GROUP = N_HEADS // N_KV_HEADS
ATTN_W = N_HEADS * HEAD_DIM
KV_W = N_KV_HEADS * HEAD_DIM
POOL_WINDOWS = (2, 4, 8, 16)
POOL_GC = 128
POOL_W = POOL_GC * len(POOL_WINDOWS)
IN_W = ATTN_W + 2 * KV_W + POOL_W + 2 * D_MODEL
GATE_COL = ATTN_W + 2 * KV_W + POOL_W
GRID_W = 64
ROPE_THETA = 10000.0
ROPE_NF = HEAD_DIM // 4
N_EXP_GROUPS = 4
EXP_PER_GROUP = 4
N_EXPERTS = 16
D_EXPERT = 256
EPS = 1e-6
LOG2_E = 1.4426950408889634

LANES = 128
SUBLANES = 8
COND_ROWS = SUBLANES
POOL_HALO = 8
ROW_BLOCK = 256
ADA_COLS = 768
EXPERT_LANE0 = N_EXP_GROUPS
PAIRS_PER_GROUP = EXP_PER_GROUP * (EXP_PER_GROUP - 1) // 2
N_BUCKETS = N_EXP_GROUPS * PAIRS_PER_GROUP
EXPERT_STEP_ROWS = 1536
TOKEN_BLOCK = 1024
FINAL_BLOCK = 512
ROW_CHUNKS = D_MODEL // LANES
SC_CORES = 2
SC_SUBCORES = 16
SC_WORKERS = SC_CORES * SC_SUBCORES
SC_LANES = 16
SC_PIECES_PER_GATHER = 128
SC_ROWS_PER_STEP = 64
PACKED_CHUNKS = ROW_CHUNKS // 2
V7X_VMEM_LIMIT_BYTES = 56 * 1024 * 1024


def _sigmoid(x):
    return 1.0 / (1.0 + jnp.exp(-x))


def _rms(x):
    return x * lax.rsqrt(jnp.mean(x * x, axis=-1, keepdims=True) + EPS)


def _resident(shape):
    zeros = (0,) * len(shape)
    return pl.BlockSpec(shape, lambda i, *_: zeros, pipeline_mode=pl.Buffered(1))


def _tiles_shape(n, chunks=ROW_CHUNKS):
    return (n // SUBLANES, chunks, SUBLANES, LANES)


def _tiles_spec(n, block_index, chunks=ROW_CHUNKS):
    return pl.BlockSpec(_tiles_shape(n, chunks), lambda *a: (block_index(*a), 0, 0, 0))


def _store_tiles(ref, x):
    for c in range(ref.shape[1]):
        ref[:, c, :, :] = x[:, c * LANES:(c + 1) * LANES].reshape(x.shape[0] // SUBLANES, SUBLANES, LANES)


def _load_tiles(ref):
    n = ref.shape[0] * SUBLANES
    return jnp.concatenate([ref[:, c, :, :].reshape(n, LANES) for c in range(ref.shape[1])], axis=1)


def _pack_bf16_pairs(x):
    bits = pltpu.bitcast(x.astype(BF16).astype(F32), U32)
    w = x.shape[1] // 2
    return bits[:, :w] | (bits[:, w:] >> 16)


def _unpack_bf16_pairs(words):
    hi = pltpu.bitcast(words & jnp.uint32(0xFFFF0000), F32).astype(BF16)
    lo = pltpu.bitcast(words << 16, F32).astype(BF16)
    return jnp.concatenate([hi, lo], axis=1)


def _row(x):
    return jnp.transpose(jnp.broadcast_to(x, (x.shape[0], LANES)))[0:1, :]


def _ada_kernel(c_ref, w_ref, b_ref, *refs, steps_per_pool_group):
    n_cast = len(refs) // 2 - 1
    c = c_ref[...]
    s = (c * _sigmoid(c)).astype(BF16)
    refs[n_cast + 1][...] = jnp.dot(s, w_ref[...].astype(BF16), preferred_element_type=F32) + b_ref[...]
    for src, dst in zip(refs[:n_cast], refs[n_cast + 2:]):
        dst[...] = src[...].astype(BF16)
    pool_src, pool_dst = refs[n_cast], refs[-1]
    wide = jnp.concatenate([pool_src[0]] * len(POOL_WINDOWS), axis=1)
    lane_group = lax.broadcasted_iota(I32, wide.shape, 1) // POOL_GC
    pool_dst[...] = jnp.where(lane_group == pl.program_id(0) // steps_per_pool_group, wide, 0.0).astype(BF16)


def _ada(cond, w_ada, b_ada, w_pool, cast=()):
    n = w_ada.shape[1]
    n_steps = n // ADA_COLS
    cast_specs = []
    for w in cast:
        assert w.ndim == 2 and w.shape[0] % (n_steps * 2 * SUBLANES) == 0
        cast_specs.append(pl.BlockSpec((w.shape[0] // n_steps, w.shape[1]), lambda j: (j, 0)))
    pool_rows = POOL_W // n_steps
    spg = POOL_GC // pool_rows
    assert w_pool.shape == (len(POOL_WINDOWS), POOL_GC, POOL_GC) and POOL_GC % pool_rows == 0
    assert pool_rows % (2 * SUBLANES) == 0
    pool_in = pl.BlockSpec((1, pool_rows, POOL_GC), lambda j: (j // spg, j % spg, 0))
    pool_out = pl.BlockSpec((pool_rows, POOL_W), lambda j: (j, 0))
    return pl.pallas_call(
        functools.partial(_ada_kernel, steps_per_pool_group=spg),
        grid=(n_steps,),
        in_specs=[
            pl.BlockSpec((COND_ROWS, D_MODEL), lambda j: (0, 0)),
            pl.BlockSpec((D_MODEL, ADA_COLS), lambda j: (0, j)),
            pl.BlockSpec((1, ADA_COLS), lambda j: (0, j)),
        ] + cast_specs + [pool_in],
        out_specs=[pl.BlockSpec((COND_ROWS, ADA_COLS), lambda j: (0, j))] + cast_specs + [pool_out],
        out_shape=[jax.ShapeDtypeStruct((COND_ROWS, n), F32)] + [jax.ShapeDtypeStruct(w.shape, BF16) for w in cast]
        + [jax.ShapeDtypeStruct((POOL_W, POOL_W), BF16)],
        name="ada_mod",
    )(cond, w_ada, b_ada, *cast, w_pool)


def _route(logits):
    lane = lax.broadcasted_iota(I32, logits.shape, 1).astype(F32)
    neg = jnp.float32(-1e30)
    far = jnp.float32(LANES)
    is_g = lane < N_EXP_GROUPS
    gl = jnp.where(is_g, logits, neg)
    gmax = jnp.max(gl, axis=-1, keepdims=True)
    gsel = jnp.min(jnp.where(gl == gmax, lane, far), axis=-1, keepdims=True)
    psel = 1.0 / jnp.sum(jnp.where(is_g, jnp.exp(gl - gmax), 0.0), axis=-1, keepdims=True)
    e_lo = EXPERT_LANE0 + EXP_PER_GROUP * gsel
    el = jnp.where(lane >= e_lo, jnp.where(lane < e_lo + EXP_PER_GROUP, logits, neg), neg)
    v1 = jnp.max(el, axis=-1, keepdims=True)
    i1 = jnp.min(jnp.where(el == v1, lane, far), axis=-1, keepdims=True)
    el2 = jnp.where(lane == i1, neg, el)
    v2 = jnp.max(el2, axis=-1, keepdims=True)
    i2 = jnp.min(jnp.where(el2 == v2, jnp.where(lane == i1, far, lane), far), axis=-1, keepdims=True)
    e2 = jnp.exp(v2 - v1)
    w1 = psel / (1.0 + e2)
    w2 = psel * e2 / (1.0 + e2)
    gate = jnp.where(lane == i1, w1, jnp.where(lane == i2, w2, 0.0))
    a = jnp.minimum(i1, i2) - e_lo
    b = jnp.maximum(i1, i2) - e_lo
    pair = a * (7.0 - a) * 0.5 + (b - a - 1.0)
    return gate, gsel * PAIRS_PER_GROUP + pair


def _mix_kernel(*refs, S, L, P, use_rope, emit_kv, n_cast, n_blocks, U, mod_row, tile):
    it = iter(refs)
    x_ref = next(it)
    mod_ref = next(it)
    if P:
        ck_ref = next(it)
        cv_ref = next(it)
    if use_rope:
        cos_ref = next(it)
        sneg_ref = next(it)
        spos_ref = next(it)
    (g1_ref, win_ref, qg_ref, kg_ref, wpool_hbm, pscale_ref, wa_hbm, wb_hbm, wo_hbm,
     g2_ref, wr_ref, win_hbm) = (next(it) for _ in range(12))
    cast_in = [[next(it) for _ in range(n)] for n in n_cast]
    xmid_ref = next(it)
    h2_ref = next(it)
    gate_ref = next(it)
    dest_ref = next(it)
    meta_ref = next(it)
    if emit_kv:
        knew_ref = next(it)
        vnew_ref = next(it)
    cast_out = [next(it) for _ in n_cast]
    q_s, k_s, v_s, xp_s, h_s, attn_s, xm_s, mod2_s, oh_s = (next(it) for _ in range(9))
    wpool_ref, wa_ref, wb_ref, wo_ref, wgate_ref, late_sems = (next(it) for _ in range(6))
    late_copies = [pltpu.make_async_copy(src, dst, late_sems.at[i]) for i, (src, dst) in enumerate(
        ((wa_hbm, wa_ref), (wpool_hbm, wpool_ref), (wb_hbm, wb_ref),
         (win_hbm.at[:, pl.ds(GATE_COL, IN_W - GATE_COL)], wgate_ref), (wo_hbm, wo_ref)))]

    TM = S * L
    RB = ROW_BLOCK
    nrb = TM // RB
    n_steps = n_blocks // U
    score_gain = HEAD_DIM ** -0.5 * LOG2_E
    step = pl.program_id(0)
    block0 = U * jnp.minimum(step, n_steps - 1)
    slot = step % 2

    mod_at = pl.ds(mod_row(jnp.minimum(step, n_steps - 1) // (nrb // U)), 1)
    sh1 = mod_ref[mod_at, 0:D_MODEL]
    gain1 = g1_ref[...] * (1.0 + mod_ref[mod_at, D_MODEL:2 * D_MODEL])
    gt1 = mod_ref[mod_at, 2 * D_MODEL:3 * D_MODEL]
    sh2 = mod_ref[mod_at, 3 * D_MODEL:4 * D_MODEL]
    gain2 = g2_ref[...] * (1.0 + mod_ref[mod_at, 4 * D_MODEL:5 * D_MODEL])
    qg = qg_ref[...] * score_gain
    kg = kg_ref[...]

    def project(r, carry):
        r0 = pl.multiple_of(r * RB, RB)
        s = r0 // L
        o = pl.multiple_of(r0 % L, RB)
        hb = (_rms(x_ref[pl.ds(r0, RB), :]) * gain1 + sh1).astype(BF16)
        h_s[pl.ds(r0, RB), :] = hb
        p1 = jnp.dot(hb, win_ref[...], preferred_element_type=F32)
        if use_rope:
            cs = cos_ref[pl.ds(o, RB), :]
            sn = sneg_ref[pl.ds(o, RB), :]
            sp = spos_ref[pl.ds(o, RB), :]

        def rope(t):
            return (t * cs + pltpu.roll(t, HEAD_DIM - ROPE_NF, 1) * sn + pltpu.roll(t, ROPE_NF, 1) * sp)

        for hd in range(N_HEADS):
            qh = _rms(p1[:, hd * HEAD_DIM:(hd + 1) * HEAD_DIM]) * qg
            if use_rope:
                qh = rope(qh)
            q_s[hd, pl.ds(r0, RB), :] = qh.astype(BF16)
        for kh in range(N_KV_HEADS):
            c0 = ATTN_W + kh * HEAD_DIM
            kk = _rms(p1[:, c0:c0 + HEAD_DIM]) * kg
            if emit_kv:
                knew_ref[pl.ds(N_KV_HEADS * r0 + kh, RB, stride=N_KV_HEADS), :] = kk
            if use_rope:
                kk = rope(kk)
            k_s[s, pl.ds(P + o, RB), kh * HEAD_DIM:(kh + 1) * HEAD_DIM] = kk.astype(BF16)
        vv = p1[:, ATTN_W + KV_W:ATTN_W + 2 * KV_W]
        if emit_kv:
            for kh in range(N_KV_HEADS):
                vnew_ref[pl.ds(N_KV_HEADS * r0 + kh, RB, stride=N_KV_HEADS), :] = (
                    vv[:, kh * HEAD_DIM:(kh + 1) * HEAD_DIM])
        v_s[s, pl.ds(P + o, RB), :] = vv.astype(BF16)
        xp_s[s, pl.ds(POOL_HALO + o, RB), :] = p1[:, ATTN_W + 2 * KV_W:GATE_COL]
        return carry

    @pl.when(step == 0)
    def _():
        xm_s[1] = jnp.zeros((U * RB, D_MODEL), F32)
        mod2_s[1] = jnp.zeros((2, D_MODEL), F32)
        for cp in late_copies:
            cp.start()

    @pl.when((step < n_steps) & (step % (nrb // U) == 0))
    def _():
        if P:
            for kh in range(N_KV_HEADS):
                cols = slice(kh * HEAD_DIM, (kh + 1) * HEAD_DIM)
                k_s[0, 0:P, cols] = ck_ref[pl.ds(kh, P, stride=N_KV_HEADS), :].astype(BF16)
                v_s[0, 0:P, cols] = cv_ref[pl.ds(kh, P, stride=N_KV_HEADS), :].astype(BF16)
        xp_s[:, 0:POOL_HALO, :] = jnp.zeros((S, POOL_HALO, POOL_W), F32)
        xp_s[:, L + POOL_HALO:L + 2 * POOL_HALO, :] = jnp.zeros((S, POOL_HALO, POOL_W), F32)
        lax.fori_loop(0, TM // RB, project, 0)
        for srcs, dst in zip(cast_in, cast_out):
            col = 0
            for src in srcs:
                dst[..., col:col + src.shape[-1]] = src[...].astype(BF16)
                col += src.shape[-1]

    @pl.when(step == 0)
    def _():
        for cp in late_copies:
            cp.wait()

    def mix(u):
        r0 = pl.multiple_of(((block0 + u) % nrb) * RB, RB)
        s = r0 // L
        o = pl.multiple_of(r0 % L, RB)
        attn_u = attn_s.at[u]
        rows = slice(u * RB, (u + 1) * RB)

        for hd in range(N_HEADS):
            kh = hd // GROUP
            k = k_s[s, :, kh * HEAD_DIM:(kh + 1) * HEAD_DIM]
            v = v_s[s, :, kh * HEAD_DIM:(kh + 1) * HEAD_DIM]
            qh = q_s[hd, pl.ds(r0, RB), :]
            sc = lax.dot_general(qh, k, (((1,), (1,)), ((), ())), preferred_element_type=F32)
            e = jnp.exp2(sc - jnp.max(sc, axis=-1, keepdims=True))
            den = jnp.sum(e, axis=-1, keepdims=True)
            oh = jnp.dot(e.astype(BF16), v, preferred_element_type=F32) / den
            attn_u[:, hd * HEAD_DIM:(hd + 1) * HEAD_DIM] = oh.astype(BF16)
        a = jnp.dot(attn_u[...], wa_ref[...], preferred_element_type=F32)

        t = o + lax.broadcasted_iota(I32, (RB, 1), 0)
        RW = RB + 2 * POOL_HALO
        parts = []
        for gi, w in enumerate(POOL_WINDOWS):
            cols = slice(gi * POOL_GC, (gi + 1) * POOL_GC)
            xw = xp_s[s, pl.ds(o, RW), cols]
            run = xw
            span = 1
            while span < w:
                run = run + pltpu.roll(run, span, 0)
                span *= 2
            if w // 2 > 1:
                run = pltpu.roll(run, RW - (w // 2 - 1), 0)
            tot = run[POOL_HALO:POOL_HALO + RB]
            cnt = (jnp.minimum(t + w // 2, L) - jnp.maximum(t - w // 2, 0)).astype(F32)
            parts.append(tot / cnt - xw[POOL_HALO:POOL_HALO + RB])
        dpool = jnp.concatenate(parts, axis=1).astype(BF16)
        pooled = jnp.dot(dpool, wpool_ref[...], preferred_element_type=F32) * pscale_ref[...]
        b = jnp.dot(pooled.astype(BF16), wb_ref[...], preferred_element_type=F32)

        gates = jnp.dot(h_s[pl.ds(r0, RB), :], wgate_ref[...], preferred_element_type=F32)
        merged = _sigmoid(gates[:, 0:D_MODEL]) * a + _sigmoid(gates[:, D_MODEL:2 * D_MODEL]) * b
        upd = jnp.dot(merged.astype(BF16), wo_ref[...], preferred_element_type=F32)
        xm = x_ref[pl.ds(r0, RB), :] + gt1 * upd
        xmid_ref[rows, :] = xm
        xm_s[slot, rows, :] = xm

    def moe_prep(u):
        rows = slice(u * RB, (u + 1) * RB)
        h2 = _rms(xm_s[1 - slot, rows, :]) * mod2_s[1 - slot, 0:1, :] + mod2_s[1 - slot, 1:2, :]
        hi = h2.astype(BF16)
        lo = (h2 - hi.astype(F32)).astype(BF16)
        l1 = jnp.dot(hi, wr_ref[...], preferred_element_type=F32)
        l2 = jnp.dot(lo, wr_ref[:, 0:LANES], preferred_element_type=F32)
        gate, bucket = _route(l1[:, 0:LANES] + l1[:, LANES:2 * LANES] + l2)
        groups = pl.ds(u * (RB // SUBLANES), RB // SUBLANES)
        _store_tiles(h2_ref.at[groups], _pack_bf16_pairs(h2))
        gate_ref[rows, :] = gate
        lane = lax.broadcasted_iota(I32, (RB, LANES), 1).astype(F32)
        first = pl.multiple_of((U * jnp.maximum(step - 1, 0) + u) * RB, RB)
        oh_s[pl.ds(first, RB), :] = jnp.where(lane == bucket, 1.0, 0.0).astype(BF16)

    mod2_s[slot, 0:1, :] = gain2
    mod2_s[slot, 1:2, :] = sh2

    @pl.when(step < n_steps)
    def _():
        for u in range(U):
            moe_prep(u)
        for u in range(U):
            mix(u)

    @pl.when(step == n_steps)
    def _():
        for u in range(U):
            moe_prep(u)
        _plan_rows(oh_s, dest_ref, meta_ref, n_blocks=n_blocks * RB // TOKEN_BLOCK, tile=tile)


def _mix(x2d, mod, mod_row, cache, rope_tabs, weights, *, S, L, emit_kv, blocks_per_step, tile, cast=()):
    T = x2d.shape[0]
    TM = S * L
    P = cache[0].shape[0] // (T // L * N_KV_HEADS) if cache is not None else 0
    use_rope = rope_tabs is not None
    assert T % TM == 0 and L % ROW_BLOCK == 0
    assert not (use_rope or P) or S == 1
    Lk = P + L

    args = [x2d, mod]
    nrb = TM // ROW_BLOCK
    n_blocks = T // ROW_BLOCK
    step_rows = blocks_per_step * ROW_BLOCK
    steps_per_group = nrb // blocks_per_step
    n_mix_steps = n_blocks // blocks_per_step
    assert nrb % blocks_per_step == 0

    def mixed(s):
        return jnp.minimum(s, n_mix_steps - 1)

    def group(s):
        return mixed(s) // steps_per_group

    def prepared(s):
        return jnp.maximum(s - 1, 0)

    in_specs = [
        pl.BlockSpec((TM, D_MODEL), lambda s: (group(s), 0)),
        _resident(mod.shape),
    ]
    if P:
        args += list(cache)
        in_specs += [pl.BlockSpec((P * N_KV_HEADS, HEAD_DIM), lambda s: (group(s), 0))] * 2
    if use_rope:
        args += list(rope_tabs)
        in_specs += [_resident((L, HEAD_DIM))] * 3
    args += list(weights)
    late = (4, 6, 7, 8)
    in_specs += [pl.BlockSpec(memory_space=pl.ANY) if i in late else _resident(w.shape)
                 for i, w in enumerate(weights)]
    assert weights[1].shape == (D_MODEL, IN_W)
    in_specs[len(in_specs) - len(weights) + 1] = _resident((D_MODEL, GATE_COL))
    args.append(weights[1])
    in_specs.append(pl.BlockSpec(memory_space=pl.ANY))
    n_steps = T // TM
    def per_group(shape):
        assert shape[0] % n_steps == 0
        blk = (shape[0] // n_steps,) + shape[1:]
        return pl.BlockSpec(blk, lambda s, n=len(blk): (group(s),) + (0,) * (n - 1))

    cast_out_shapes = [ws[0].shape[:-1] + (sum(w.shape[-1] for w in ws),) for ws in cast]
    for ws in cast:
        args += list(ws)
        in_specs += [per_group(w.shape) for w in ws]

    assert T % TOKEN_BLOCK == 0
    out_shape = [jax.ShapeDtypeStruct((T, D_MODEL), F32), jax.ShapeDtypeStruct(_tiles_shape(T, PACKED_CHUNKS), U32),
                 jax.ShapeDtypeStruct((T, LANES), F32),
                 jax.ShapeDtypeStruct((T // TOKEN_BLOCK, 1, TOKEN_BLOCK), I32),
                 jax.ShapeDtypeStruct((SUBLANES, LANES), I32)]
    out_specs = [pl.BlockSpec((step_rows, D_MODEL), lambda s: (mixed(s), 0)),
                 _tiles_spec(step_rows, prepared, PACKED_CHUNKS),
                 pl.BlockSpec((step_rows, LANES), lambda s: (prepared(s), 0)),
                 pl.BlockSpec((T // TOKEN_BLOCK, 1, TOKEN_BLOCK), lambda s: (0, 0, 0)),
                 pl.BlockSpec((SUBLANES, LANES), lambda s: (0, 0))]
    if emit_kv:
        out_shape += [jax.ShapeDtypeStruct((T * N_KV_HEADS, HEAD_DIM), F32)] * 2
        out_specs += [pl.BlockSpec((TM * N_KV_HEADS, HEAD_DIM), lambda s: (group(s), 0))] * 2
    out_shape += [jax.ShapeDtypeStruct(shp, BF16) for shp in cast_out_shapes]
    out_specs += [per_group(shp) for shp in cast_out_shapes]

    scratch = [
        pltpu.VMEM((N_HEADS, TM, HEAD_DIM), BF16),
        pltpu.VMEM((S, Lk, KV_W), BF16),
        pltpu.VMEM((S, Lk, KV_W), BF16),
        pltpu.VMEM((S, L + 2 * POOL_HALO, POOL_W), F32),
        pltpu.VMEM((TM, D_MODEL), BF16),
        pltpu.VMEM((blocks_per_step, ROW_BLOCK, ATTN_W), BF16),
        pltpu.VMEM((2, step_rows, D_MODEL), F32),
        pltpu.VMEM((2, 2, D_MODEL), F32),
        pltpu.VMEM((T, LANES), BF16),
    ] + [pltpu.VMEM(weights[i].shape, weights[i].dtype) for i in late] + [
        pltpu.VMEM((D_MODEL, IN_W - GATE_COL), BF16), pltpu.SemaphoreType.DMA((len(late) + 1,))]
    kern = functools.partial(_mix_kernel, S=S, L=L, P=P, use_rope=use_rope, emit_kv=emit_kv,
                             n_cast=tuple(len(ws) for ws in cast), n_blocks=n_blocks, U=blocks_per_step,
                             mod_row=mod_row, tile=tile)
    return pl.pallas_call(
        kern,
        grid=(n_mix_steps + 1,),
        in_specs=in_specs,
        out_specs=out_specs,
        out_shape=out_shape,
        scratch_shapes=scratch,
        compiler_params=pltpu.CompilerParams(
            dimension_semantics=("arbitrary",), vmem_limit_bytes=V7X_VMEM_LIMIT_BYTES),
        name="mixer_rope" if use_rope else "mixer_ctx",
    )(*args)


def _plan_rows(oh_ref, dest_ref, meta_ref, *, n_blocks, tile):
    TB = TOKEN_BLOCK
    lane = lax.broadcasted_iota(I32, (SUBLANES, LANES), 1)

    def count(b, acc):
        oh = oh_ref[pl.ds(pl.multiple_of(b * TB, TB), TB), :].astype(F32)
        return acc + jnp.sum(oh, axis=0, keepdims=True)

    counts = lax.fori_loop(0, n_blocks, count, jnp.zeros((SUBLANES, LANES), F32))
    padded = jnp.floor((counts + (tile - 0.5)) * (1.0 / tile)) * tile
    ends = padded
    step = 1
    while step < LANES:
        ends = ends + jnp.where(lane >= step, pltpu.roll(ends, step, 1), 0.0)
        step *= 2
    starts = ends - padded

    tri = jnp.where(lax.broadcasted_iota(I32, (TB, TB), 1) < lax.broadcasted_iota(I32, (TB, TB), 0),
                    1.0, 0.0).astype(BF16)

    def place(b, seen):
        oh = oh_ref[pl.ds(pl.multiple_of(b * TB, TB), TB), :]
        ohf = oh.astype(F32)
        rank = jnp.dot(tri, oh, preferred_element_type=F32)
        base = (starts + seen)[0:1, :]
        d = jnp.sum(ohf * (rank + base), axis=1, keepdims=True)
        dest_ref[b] = _row(d).astype(I32)
        return seen + jnp.sum(ohf, axis=0, keepdims=True)

    lax.fori_loop(0, n_blocks, place, jnp.zeros((SUBLANES, LANES), F32))

    tile_row0 = lax.broadcasted_iota(I32, (LANES, LANES), 0).astype(F32) * tile
    is_bucket = lax.broadcasted_iota(I32, (LANES, LANES), 1) < N_BUCKETS
    done = jnp.sum(jnp.where(is_bucket, jnp.where(ends[0:1, :] <= tile_row0, 1.0, 0.0), 0.0),
                   axis=1, keepdims=True)
    bkt = jnp.minimum(done, N_BUCKETS - 1.0)
    grp = (jnp.where(bkt >= PAIRS_PER_GROUP, 1.0, 0.0) + jnp.where(bkt >= 2 * PAIRS_PER_GROUP, 1.0, 0.0)
           + jnp.where(bkt >= 3 * PAIRS_PER_GROUP, 1.0, 0.0))
    pair = bkt - PAIRS_PER_GROUP * grp
    a = jnp.where(pair >= 3.0, 1.0, 0.0) + jnp.where(pair >= 5.0, 1.0, 0.0)
    b = pair - a * (7.0 - a) * 0.5 + a + 1.0
    e1 = EXP_PER_GROUP * grp + a
    e2 = EXP_PER_GROUP * grp + b
    meta = jnp.concatenate(
        [_row(e1), _row(e2), jnp.floor(ends[0:1, :] * (1.0 / tile) + 0.5),
         jnp.zeros((SUBLANES - 3, LANES), F32)], axis=0)
    meta_ref[...] = meta.astype(I32)


def _sc_move_rows(src_v, table_hbm, out_hbm, lo, n_rows, idx_v, pieces_v, sem):
    chunks = pieces_v.shape[0] // SC_ROWS_PER_STEP
    lane = lax.iota(I32, SC_LANES)
    row_in_group = lane & (SUBLANES - 1)
    chunk_in_pair = lane >> 3
    rows_per_gather = SC_PIECES_PER_GATHER // chunks

    @pl.loop(0, n_rows // SC_ROWS_PER_STEP)
    def _(step):
        copies = []
        for g in range(SC_ROWS_PER_STEP // rows_per_gather):
            r0 = step * SC_ROWS_PER_STEP + g * rows_per_gather
            for v in range(SC_PIECES_PER_GATHER // SC_LANES):
                group, chunk0 = v // (chunks // 2), 2 * (v % (chunks // 2))
                tok = plsc.load_gather(src_v, [r0 + group * SUBLANES + row_in_group])
                piece = (tok >> 3) * (SUBLANES * chunks) + (chunk0 + chunk_in_pair) * SUBLANES + (tok & 7)
                idx_v[pl.ds(g * SC_PIECES_PER_GATHER + v * SC_LANES, SC_LANES)] = piece
            window = pl.ds(g * SC_PIECES_PER_GATHER, SC_PIECES_PER_GATHER)
            copies.append(pltpu.async_copy(table_hbm.at[idx_v.at[window]], pieces_v.at[window], sem))
        for cp in copies:
            cp.wait()
        first = pl.multiple_of((lo + step * SC_ROWS_PER_STEP) * chunks, SC_ROWS_PER_STEP * chunks)
        pltpu.sync_copy(pieces_v, out_hbm.at[pl.ds(first, SC_ROWS_PER_STEP * chunks)])


def _sc_scratch(chunks, dtype):
    return [pltpu.VMEM((SC_ROWS_PER_STEP * chunks,), I32), pltpu.VMEM((SC_ROWS_PER_STEP * chunks, LANES), dtype)]


def _sc_dispatch(h2_flat, gate_rows, dest, n_rows):
    T = dest.shape[0]
    per_worker = n_rows // SC_WORKERS
    rows_per_step = SC_ROWS_PER_STEP
    chunks = h2_flat.shape[0] // T
    assert n_rows % SC_WORKERS == 0 and per_worker % rows_per_step == 0 and T % SC_LANES == 0
    mesh = plsc.VectorSubcoreMesh(core_axis_name="c", subcore_axis_name="s")

    @functools.partial(
        pl.kernel, mesh=mesh,
        out_type=[jax.ShapeDtypeStruct((n_rows * chunks, LANES), h2_flat.dtype),
                  jax.ShapeDtypeStruct((n_rows, LANES), F32)],
        scratch_types=[pltpu.VMEM((T,), I32), pltpu.VMEM((per_worker,), I32)]
        + _sc_scratch(chunks, h2_flat.dtype)
        + [pltpu.VMEM((rows_per_step, LANES), F32), pltpu.SemaphoreType.DMA, pltpu.SemaphoreType.DMA],
        compiler_params=pltpu.CompilerParams(use_tc_tiling_on_sc=True, needs_layout_passes=False),
        name="sc_dispatch",
    )
    def dispatch(h2_hbm, gate_hbm, dest_hbm, out_h_hbm, out_g_hbm,
                 dest_v, src_v, idx_v, pieces_v, gates_v, sem_h, sem_g):
        worker = lax.axis_index("s") * SC_CORES + lax.axis_index("c")
        lo = worker * per_worker
        pltpu.sync_copy(dest_hbm, dest_v)

        @pl.loop(0, per_worker // SC_LANES)
        def _(j):
            j0 = pl.multiple_of(j * SC_LANES, SC_LANES)
            src_v[pl.ds(j0, SC_LANES)] = lax.rem(lo + j0 + lax.iota(I32, SC_LANES), T)

        @pl.loop(0, T // SC_LANES)
        def _(j):
            t0 = pl.multiple_of(j * SC_LANES, SC_LANES)
            d = dest_v[pl.ds(t0, SC_LANES)] - lo
            mine = (d >= 0) & (d < per_worker)
            plsc.store_scatter(src_v, [jnp.where(mine, d, 0)], t0 + lax.iota(I32, SC_LANES), mask=mine)

        @pl.loop(0, per_worker // rows_per_step)
        def _(j):
            off = pl.multiple_of(j * rows_per_step, rows_per_step)
            pltpu.async_copy(gate_hbm.at[src_v.at[pl.ds(off, rows_per_step)]], gates_v, sem_g).wait()
            pltpu.sync_copy(gates_v, out_g_hbm.at[pl.ds(lo + off, rows_per_step)])

        _sc_move_rows(src_v, h2_hbm, out_h_hbm, lo, per_worker, idx_v, pieces_v, sem_h)

    return dispatch(h2_flat, gate_rows, dest)


def _expert_kernel(meta, x_ref, gv_ref, wgu_hbm, wd_hbm, o_ref, wgu_ref, wd_ref, ready_s, sems, *, tile, per_step):
    groups = tile // SUBLANES
    n_used = meta[2, LANES - 1]
    step = pl.program_id(0)

    def weight_copies(g):
        experts = pl.ds(g * EXP_PER_GROUP, EXP_PER_GROUP)
        return (pltpu.make_async_copy(wgu_hbm.at[experts], wgu_ref.at[experts], sems.at[g]),
                pltpu.make_async_copy(wd_hbm.at[experts], wd_ref.at[experts], sems.at[g]))

    def land_through(last_group):
        landed = ready_s[0]
        for g in range(N_EXP_GROUPS):
            @pl.when((g >= landed) & (g <= last_group))
            def _():
                for cp in weight_copies(g):
                    cp.wait()
                if g + 1 < N_EXP_GROUPS:
                    for cp in weight_copies(g + 1):
                        cp.start()
        ready_s[0] = jnp.maximum(landed, last_group + 1)

    @pl.when(step == 0)
    def _():
        ready_s[0] = 0
        for cp in weight_copies(0):
            cp.start()

    def one_tile(k, t):
        rows = pl.ds(pl.multiple_of(k * groups, groups), groups)
        x = _unpack_bf16_pairs(_load_tiles(x_ref.at[rows]))
        gv = gv_ref[pl.ds(pl.multiple_of(k * tile, SUBLANES), tile), :]
        lane = lax.broadcasted_iota(I32, gv.shape, 1)
        out = None
        for e in (meta[0, t], meta[1, t]):
            ge = jnp.sum(jnp.where(lane == EXPERT_LANE0 + e, gv, 0.0), axis=-1, keepdims=True)
            h = jnp.dot(x, wgu_ref[e], preferred_element_type=F32)
            hg = h[:, 0:D_EXPERT]
            hid = (hg * _sigmoid(hg) * h[:, D_EXPERT:2 * D_EXPERT] * ge).astype(BF16)
            y = jnp.dot(hid, wd_ref[e], preferred_element_type=F32)
            out = y if out is None else out + y
        _store_tiles(o_ref.at[rows], _pack_bf16_pairs(out))

    def two_tiles(j, carry):
        k = 2 * j
        t = step * per_step + k

        @pl.when(t + 1 < n_used)
        def _():
            land_through(meta[0, t + 1] // EXP_PER_GROUP)
            one_tile(k, t)
            one_tile(k + 1, t + 1)

        @pl.when(t + 1 == n_used)
        def _():
            land_through(meta[0, t] // EXP_PER_GROUP)
            one_tile(k, t)

        return carry

    assert per_step % 2 == 0
    lax.fori_loop(0, per_step // 2, two_tiles, 0)

    @pl.when(step == pl.num_programs(0) - 1)
    def _():
        land_through(N_EXP_GROUPS - 1)


def _experts(sorted_h2, sorted_gates, meta, wgu, wd, tiling):
    tile, per_step, n_tiles = tiling
    step_rows = tile * per_step
    assert n_tiles * tile == sorted_h2.shape[0] * SUBLANES and n_tiles % per_step == 0

    def last_used(i, meta):
        return jnp.minimum(i, (meta[2, LANES - 1] - 1) // per_step)

    return pl.pallas_call(
        functools.partial(_expert_kernel, tile=tile, per_step=per_step),
        grid_spec=pltpu.PrefetchScalarGridSpec(
            num_scalar_prefetch=1,
            grid=(n_tiles // per_step,),
            in_specs=[
                _tiles_spec(step_rows, last_used, PACKED_CHUNKS),
                pl.BlockSpec((step_rows, LANES), lambda *a: (last_used(*a), 0)),
                pl.BlockSpec(memory_space=pl.ANY), pl.BlockSpec(memory_space=pl.ANY),
            ],
            out_specs=_tiles_spec(step_rows, last_used, PACKED_CHUNKS),
            scratch_shapes=[pltpu.VMEM(wgu.shape, wgu.dtype), pltpu.VMEM(wd.shape, wd.dtype),
                            pltpu.SMEM((1,), I32), pltpu.SemaphoreType.DMA((N_EXP_GROUPS,))],
        ),
        out_shape=jax.ShapeDtypeStruct(_tiles_shape(n_tiles * tile, PACKED_CHUNKS), U32),
        compiler_params=pltpu.CompilerParams(
            dimension_semantics=("arbitrary",), vmem_limit_bytes=V7X_VMEM_LIMIT_BYTES),
        name="moe_experts",
    )(meta, sorted_h2, sorted_gates, wgu, wd)


def _sc_row_gather(table_flat, idx, chunks):
    n = idx.shape[0]
    per_worker = n // SC_WORKERS
    assert n % SC_WORKERS == 0 and per_worker % SC_ROWS_PER_STEP == 0
    mesh = plsc.VectorSubcoreMesh(core_axis_name="c", subcore_axis_name="s")

    @functools.partial(
        pl.kernel, mesh=mesh,
        out_type=jax.ShapeDtypeStruct((n * chunks, LANES), table_flat.dtype),
        scratch_types=[pltpu.VMEM((per_worker,), I32)] + _sc_scratch(chunks, table_flat.dtype)
        + [pltpu.SemaphoreType.DMA],
        compiler_params=pltpu.CompilerParams(use_tc_tiling_on_sc=True, needs_layout_passes=False),
        name="sc_row_gather",
    )
    def gather(table_hbm, idx_hbm, out_hbm, src_v, idx_v, pieces_v, sem):
        worker = lax.axis_index("s") * SC_CORES + lax.axis_index("c")
        lo = worker * per_worker
        pltpu.sync_copy(idx_hbm.at[pl.ds(lo, per_worker)], src_v)
        _sc_move_rows(src_v, table_hbm, out_hbm, lo, per_worker, idx_v, pieces_v, sem)

    return gather(table_flat, idx)


def _final_kernel(x_ref, moe_ref, gt2_ref, gf_ref, o_ref, *, mod_row):
    gt2 = gt2_ref[pl.ds(mod_row(pl.program_id(0)), 1), :]
    y = x_ref[...] + gt2 * _unpack_bf16_pairs(_load_tiles(moe_ref)).astype(F32)
    o_ref[...] = _rms(y) * gf_ref[...]


def _final(xmid, moe_rows, mod, mod_row, gf):
    T = xmid.shape[0]
    return pl.pallas_call(
        functools.partial(_final_kernel, mod_row=mod_row),
        grid=(T // FINAL_BLOCK,),
        in_specs=[
            pl.BlockSpec((FINAL_BLOCK, D_MODEL), lambda i: (i, 0)),
            _tiles_spec(FINAL_BLOCK, lambda i: i, PACKED_CHUNKS),
            pl.BlockSpec((COND_ROWS, D_MODEL), lambda i: (0, 5)),
            pl.BlockSpec((1, D_MODEL), lambda i: (0, 0)),
        ],
        out_specs=pl.BlockSpec((FINAL_BLOCK, D_MODEL), lambda i: (i, 0)),
        out_shape=jax.ShapeDtypeStruct((T, D_MODEL), F32),
        compiler_params=pltpu.CompilerParams(
            dimension_semantics=("arbitrary",), vmem_limit_bytes=V7X_VMEM_LIMIT_BYTES),
        name="moe_final",
    )(xmid, moe_rows, mod, gf)


def _flat(tiles):
    return tiles.reshape(-1, LANES)


def _expert_tiling(T):
    tile = max(256, -(-(T * 9) // (8 * N_BUCKETS * 64)) * 64)
    per_step = max(2, EXPERT_STEP_ROWS // tile // 2 * 2)
    n_tiles = (T + N_BUCKETS * (tile - 1)) // tile
    while n_tiles % per_step or (n_tiles * tile) % (SC_WORKERS * SC_ROWS_PER_STEP):
        n_tiles += 1
    return tile, per_step, n_tiles


def _moe_dispatch(h2_tiles, gate_rows, dest, tiling):
    tile, _, n_tiles = tiling
    n_rows = n_tiles * tile
    assert n_tiles <= LANES
    sorted_h2, sorted_gates = _sc_dispatch(_flat(h2_tiles), gate_rows, dest, n_rows)
    return sorted_h2.reshape(_tiles_shape(n_rows, PACKED_CHUNKS)), sorted_gates


def _moe_unpermute(moe_sorted_tiles, dest):
    chunks = moe_sorted_tiles.shape[1]
    return _sc_row_gather(_flat(moe_sorted_tiles), dest, chunks).reshape(_tiles_shape(dest.shape[0], chunks))


def _rope_tables(n_tokens):
    t = np.arange(n_tokens)
    row = (t // GRID_W).astype(np.float32)
    col = (t % GRID_W).astype(np.float32)
    freq = np.float32(ROPE_THETA) ** (-np.arange(ROPE_NF, dtype=np.float32) / np.float32(ROPE_NF))
    ang = np.concatenate([row[:, None] * freq] * 2 + [col[:, None] * freq] * 2, axis=-1)
    first = (np.arange(HEAD_DIM) % (2 * ROPE_NF)) < ROPE_NF
    sin = np.sin(ang)
    zero = np.float32(0.0)
    return (jnp.asarray(np.cos(ang)), jnp.asarray(np.where(first, -sin, zero)),
            jnp.asarray(np.where(first, zero, sin)))


def kernel(x_prompt, x_sample, cache_k, cache_v, c, c_ctx, norm1_g, norm2_g, w_ada, b_ada, w_in, q_norm_g, k_norm_g, w_pool, pool_scale, w_branch_a, w_branch_b, w_out, w_router_group, w_router_expert, w_exp_gate, w_exp_up, w_exp_down, final_norm_g):
    assert norm1_g.shape[0] == 1, "single-layer trunk"
    B, L_ctx, _ = x_prompt.shape
    Bs, L_lat, _ = x_sample.shape
    P = cache_k.shape[2]
    assert 1 + Bs <= COND_ROWS

    cond = jnp.concatenate([c_ctx[None, :], c, jnp.zeros((COND_ROWS - 1 - Bs, D_MODEL), F32)], axis=0)
    mod, w_in_b, wa_b, wb_b, wo_b, wpool_b = _ada(
        cond, w_ada[0], b_ada[0][None, :], w_pool[0],
        cast=(w_in[0], w_branch_a[0], w_branch_b[0], w_out[0]))

    wr = jnp.concatenate([w_router_group[0], w_router_expert[0],
                          jnp.zeros((D_MODEL, LANES - N_EXP_GROUPS - N_EXPERTS), F32)], axis=1)
    wr_hi = wr.astype(BF16)
    wr_lo = (wr - wr_hi.astype(F32)).astype(BF16)
    mix_w = (norm1_g[0][None, :], w_in_b, q_norm_g[0][None, :], k_norm_g[0][None, :],
             wpool_b, pool_scale[0][None, :], wa_b, wb_b, wo_b,
             norm2_g[0][None, :], jnp.concatenate([wr_hi, wr_lo], axis=1))
    gf = final_norm_g[None, :]

    xp2 = x_prompt.reshape(B * L_ctx, D_MODEL)
    tiling_p = _expert_tiling(B * L_ctx)
    xmid_p, h2_p, gate_p, dest_p, meta_p, knew, vnew, wgu, wd = _mix(
        xp2, mod, lambda i: 0, None, None, mix_w, S=2, L=L_ctx, emit_kv=True, blocks_per_step=2,
        tile=tiling_p[0], cast=((w_exp_gate[0], w_exp_up[0]), (w_exp_down[0],)))
    dest_p = dest_p.reshape(B * L_ctx)
    sh_p, sg_p = _moe_dispatch(h2_p, gate_p, dest_p, tiling_p)

    xs2 = x_sample.reshape(Bs * L_lat, D_MODEL)
    cache = (cache_k.reshape(Bs * P * N_KV_HEADS, HEAD_DIM), cache_v.reshape(Bs * P * N_KV_HEADS, HEAD_DIM))
    tiling_s = _expert_tiling(Bs * L_lat)
    xmid_s, h2_s, gate_s, dest_s, meta_s = _mix(
        xs2, mod, lambda i: 1 + i, cache, _rope_tables(L_lat), mix_w,
        S=1, L=L_lat, emit_kv=False, blocks_per_step=1, tile=tiling_s[0])
    dest_s = dest_s.reshape(Bs * L_lat)
    sh_s, sg_s = _moe_dispatch(h2_s, gate_s, dest_s, tiling_s)

    moe_p = _moe_unpermute(_experts(sh_p, sg_p, meta_p, wgu, wd, tiling_p), dest_p)
    moe_s = _moe_unpermute(_experts(sh_s, sg_s, meta_s, wgu, wd, tiling_s), dest_s)
    y_prompt = _final(xmid_p, moe_p, mod, lambda i: 0, gf)
    blocks_per_seq = L_lat // FINAL_BLOCK
    y_sample = _final(xmid_s, moe_s, mod, lambda i: 1 + i // blocks_per_seq, gf)

    return (y_prompt.reshape(B, L_ctx, D_MODEL), y_sample.reshape(Bs, L_lat, D_MODEL),
            knew.reshape(B, 1, L_ctx, N_KV_HEADS, HEAD_DIM), vnew.reshape(B, 1, L_ctx, N_KV_HEADS, HEAD_DIM))
```

```python
import functools

import numpy as np
import jax
import jax.numpy as jnp
from jax import lax
from jax.experimental import pallas as pl
from jax.experimental.pallas import tpu as pltpu
from jax.experimental.pallas import tpu_sc as plsc

F32 = jnp.float32
BF16 = jnp.bfloat16
I32 = jnp.int32
U32 = jnp.uint32

D_MODEL = 1024
HEAD_DIM = 128
N_HEADS = 8
N_KV_HEADS = 2
GROUP = N_HEADS // N_KV_HEADS
ATTN_W = N_HEADS * HEAD_DIM
KV_W = N_KV_HEADS * HEAD_DIM
POOL_WINDOWS = (2, 4, 8, 16)
POOL_GC = 128
POOL_W = POOL_GC * len(POOL_WINDOWS)
IN_W = ATTN_W + 2 * KV_W + POOL_W + 2 * D_MODEL
GATE_COL = ATTN_W + 2 * KV_W + POOL_W
GRID_W = 64
ROPE_THETA = 10000.0
ROPE_NF = HEAD_DIM // 4
N_EXP_GROUPS = 4
EXP_PER_GROUP = 4
N_EXPERTS = 16
D_EXPERT = 256
EPS = 1e-6
LOG2_E = 1.4426950408889634

LANES = 128
SUBLANES = 8
COND_ROWS = SUBLANES
POOL_HALO = 8
ROW_BLOCK = 256
ADA_COLS = 768
EXPERT_LANE0 = N_EXP_GROUPS
PAIRS_PER_GROUP = EXP_PER_GROUP * (EXP_PER_GROUP - 1) // 2
N_BUCKETS = N_EXP_GROUPS * PAIRS_PER_GROUP
EXPERT_STEP_ROWS = 1536
TOKEN_BLOCK = 1024
FINAL_BLOCK = 1024
ROW_CHUNKS = D_MODEL // LANES
SC_CORES = 2
SC_SUBCORES = 16
SC_WORKERS = SC_CORES * SC_SUBCORES
SC_LANES = 16
SC_PIECES_PER_GATHER = 128
SC_ROWS_PER_STEP = 64
PACKED_CHUNKS = ROW_CHUNKS // 2
V7X_VMEM_LIMIT_BYTES = 56 * 1024 * 1024


def _sigmoid(x):
    return 1.0 / (1.0 + jnp.exp(-x))


def _rms(x):
    return x * lax.rsqrt(jnp.mean(x * x, axis=-1, keepdims=True) + EPS)


def _resident(shape):
    zeros = (0,) * len(shape)
    return pl.BlockSpec(shape, lambda i, *_: zeros, pipeline_mode=pl.Buffered(1))


def _tiles_shape(n, chunks=ROW_CHUNKS):
    return (n // SUBLANES, chunks, SUBLANES, LANES)


def _tiles_spec(n, block_index, chunks=ROW_CHUNKS):
    return pl.BlockSpec(_tiles_shape(n, chunks), lambda *a: (block_index(*a), 0, 0, 0))


def _store_tiles(ref, x):
    for c in range(ref.shape[1]):
        ref[:, c, :, :] = x[:, c * LANES:(c + 1) * LANES].reshape(x.shape[0] // SUBLANES, SUBLANES, LANES)


def _load_tiles(ref):
    n = ref.shape[0] * SUBLANES
    return jnp.concatenate([ref[:, c, :, :].reshape(n, LANES) for c in range(ref.shape[1])], axis=1)


def _pack_bf16_pairs(x):
    bits = pltpu.bitcast(x.astype(BF16).astype(F32), U32)
    w = x.shape[1] // 2
    return bits[:, :w] | (bits[:, w:] >> 16)


def _unpack_bf16_pairs(words):
    hi = pltpu.bitcast(words & jnp.uint32(0xFFFF0000), F32).astype(BF16)
    lo = pltpu.bitcast(words << 16, F32).astype(BF16)
    return jnp.concatenate([hi, lo], axis=1)


def _row(x):
    return jnp.transpose(jnp.broadcast_to(x, (x.shape[0], LANES)))[0:1, :]


def _ada_kernel(c_ref, w_ref, b_ref, *refs, steps_per_pool_group):
    n_cast = len(refs) // 2 - 1
    c = c_ref[...]
    s = (c * _sigmoid(c)).astype(BF16)
    refs[n_cast + 1][...] = jnp.dot(s, w_ref[...].astype(BF16), preferred_element_type=F32) + b_ref[...]
    for src, dst in zip(refs[:n_cast], refs[n_cast + 2:]):
        dst[...] = src[...].astype(BF16)
    pool_src, pool_dst = refs[n_cast], refs[-1]
    wide = jnp.concatenate([pool_src[0]] * len(POOL_WINDOWS), axis=1)
    lane_group = lax.broadcasted_iota(I32, wide.shape, 1) // POOL_GC
    pool_dst[...] = jnp.where(lane_group == pl.program_id(0) // steps_per_pool_group, wide, 0.0).astype(BF16)


def _ada(cond, w_ada, b_ada, w_pool, cast=()):
    n = w_ada.shape[1]
    n_steps = n // ADA_COLS
    cast_specs = []
    for w in cast:
        assert w.ndim == 2 and w.shape[0] % (n_steps * 2 * SUBLANES) == 0
        cast_specs.append(pl.BlockSpec((w.shape[0] // n_steps, w.shape[1]), lambda j: (j, 0)))
    pool_rows = POOL_W // n_steps
    spg = POOL_GC // pool_rows
    assert w_pool.shape == (len(POOL_WINDOWS), POOL_GC, POOL_GC) and POOL_GC % pool_rows == 0
    assert pool_rows % (2 * SUBLANES) == 0
    pool_in = pl.BlockSpec((1, pool_rows, POOL_GC), lambda j: (j // spg, j % spg, 0))
    pool_out = pl.BlockSpec((pool_rows, POOL_W), lambda j: (j, 0))
    return pl.pallas_call(
        functools.partial(_ada_kernel, steps_per_pool_group=spg),
        grid=(n_steps,),
        in_specs=[
            pl.BlockSpec((COND_ROWS, D_MODEL), lambda j: (0, 0)),
            pl.BlockSpec((D_MODEL, ADA_COLS), lambda j: (0, j)),
            pl.BlockSpec((1, ADA_COLS), lambda j: (0, j)),
        ] + cast_specs + [pool_in],
        out_specs=[pl.BlockSpec((COND_ROWS, ADA_COLS), lambda j: (0, j))] + cast_specs + [pool_out],
        out_shape=[jax.ShapeDtypeStruct((COND_ROWS, n), F32)] + [jax.ShapeDtypeStruct(w.shape, BF16) for w in cast]
        + [jax.ShapeDtypeStruct((POOL_W, POOL_W), BF16)],
        name="ada_mod",
    )(cond, w_ada, b_ada, *cast, w_pool)


def _route(logits):
    lane = lax.broadcasted_iota(I32, logits.shape, 1).astype(F32)
    neg = jnp.float32(-1e30)
    far = jnp.float32(LANES)
    is_g = lane < N_EXP_GROUPS
    gl = jnp.where(is_g, logits, neg)
    gmax = jnp.max(gl, axis=-1, keepdims=True)
    gsel = jnp.min(jnp.where(gl == gmax, lane, far), axis=-1, keepdims=True)
    psel = 1.0 / jnp.sum(jnp.where(is_g, jnp.exp(gl - gmax), 0.0), axis=-1, keepdims=True)
    e_lo = EXPERT_LANE0 + EXP_PER_GROUP * gsel
    el = jnp.where(lane >= e_lo, jnp.where(lane < e_lo + EXP_PER_GROUP, logits, neg), neg)
    v1 = jnp.max(el, axis=-1, keepdims=True)
    i1 = jnp.min(jnp.where(el == v1, lane, far), axis=-1, keepdims=True)
    el2 = jnp.where(lane == i1, neg, el)
    v2 = jnp.max(el2, axis=-1, keepdims=True)
    i2 = jnp.min(jnp.where(el2 == v2, jnp.where(lane == i1, far, lane), far), axis=-1, keepdims=True)
    e2 = jnp.exp(v2 - v1)
    w1 = psel / (1.0 + e2)
    w2 = psel * e2 / (1.0 + e2)
    gate = jnp.where(lane == i1, w1, jnp.where(lane == i2, w2, 0.0))
    a = jnp.minimum(i1, i2) - e_lo
    b = jnp.maximum(i1, i2) - e_lo
    pair = a * (7.0 - a) * 0.5 + (b - a - 1.0)
    return gate, gsel * PAIRS_PER_GROUP + pair


def _mix_kernel(*refs, S, L, P, use_rope, emit_kv, n_cast, n_blocks, U, mod_row, tile):
    it = iter(refs)
    x_ref = next(it)
    mod_ref = next(it)
    if P:
        ck_ref = next(it)
        cv_ref = next(it)
    if use_rope:
        cos_ref = next(it)
        sneg_ref = next(it)
        spos_ref = next(it)
    (g1_ref, win_ref, qg_ref, kg_ref, wpool_hbm, pscale_ref, wa_hbm, wb_hbm, wo_hbm,
     g2_ref, wr_ref, win_hbm) = (next(it) for _ in range(12))
    cast_in = [[next(it) for _ in range(n)] for n in n_cast]
    xmid_ref = next(it)
    h2_ref = next(it)
    gate_ref = next(it)
    dest_ref = next(it)
    meta_ref = next(it)
    if emit_kv:
        knew_ref = next(it)
        vnew_ref = next(it)
    cast_out = [next(it) for _ in n_cast]
    q_s, k_s, v_s, xp_s, h_s, attn_s, xm_s, mod2_s, oh_s = (next(it) for _ in range(9))
    wpool_ref, wa_ref, wb_ref, wo_ref, wgate_ref, late_sems = (next(it) for _ in range(6))
    late_copies = [pltpu.make_async_copy(src, dst, late_sems.at[i]) for i, (src, dst) in enumerate(
        ((wa_hbm, wa_ref), (wpool_hbm, wpool_ref), (wb_hbm, wb_ref),
         (win_hbm.at[:, pl.ds(GATE_COL, IN_W - GATE_COL)], wgate_ref), (wo_hbm, wo_ref)))]

    TM = S * L
    RB = ROW_BLOCK
    nrb = TM // RB
    n_steps = n_blocks // U
    score_gain = HEAD_DIM ** -0.5 * LOG2_E
    step = pl.program_id(0)
    block0 = U * jnp.minimum(step, n_steps - 1)
    slot = step % 2

    mod_at = pl.ds(mod_row(jnp.minimum(step, n_steps - 1) // (nrb // U)), 1)
    sh1 = mod_ref[mod_at, 0:D_MODEL]
    gain1 = g1_ref[...] * (1.0 + mod_ref[mod_at, D_MODEL:2 * D_MODEL])
    gt1 = mod_ref[mod_at, 2 * D_MODEL:3 * D_MODEL]
    sh2 = mod_ref[mod_at, 3 * D_MODEL:4 * D_MODEL]
    gain2 = g2_ref[...] * (1.0 + mod_ref[mod_at, 4 * D_MODEL:5 * D_MODEL])
    qg = qg_ref[...] * score_gain
    kg = kg_ref[...]

    def project(r, carry):
        r0 = pl.multiple_of(r * RB, RB)
        s = r0 // L
        o = pl.multiple_of(r0 % L, RB)
        hb = (_rms(x_ref[pl.ds(r0, RB), :]) * gain1 + sh1).astype(BF16)
        h_s[pl.ds(r0, RB), :] = hb
        p1 = jnp.dot(hb, win_ref[...], preferred_element_type=F32)
        if use_rope:
            cs = cos_ref[pl.ds(o, RB), :]
            sn = sneg_ref[pl.ds(o, RB), :]
            sp = spos_ref[pl.ds(o, RB), :]

        def rope(t):
            return (t * cs + pltpu.roll(t, HEAD_DIM - ROPE_NF, 1) * sn + pltpu.roll(t, ROPE_NF, 1) * sp)

        for hd in range(N_HEADS):
            qh = _rms(p1[:, hd * HEAD_DIM:(hd + 1) * HEAD_DIM]) * qg
            if use_rope:
                qh = rope(qh)
            q_s[hd, pl.ds(r0, RB), :] = qh.astype(BF16)
        for kh in range(N_KV_HEADS):
            c0 = ATTN_W + kh * HEAD_DIM
            kk = _rms(p1[:, c0:c0 + HEAD_DIM]) * kg
            if emit_kv:
                knew_ref[pl.ds(N_KV_HEADS * r0 + kh, RB, stride=N_KV_HEADS), :] = kk
            if use_rope:
                kk = rope(kk)
            k_s[s, pl.ds(P + o, RB), kh * HEAD_DIM:(kh + 1) * HEAD_DIM] = kk.astype(BF16)
        vv = p1[:, ATTN_W + KV_W:ATTN_W + 2 * KV_W]
        if emit_kv:
            for kh in range(N_KV_HEADS):
                vnew_ref[pl.ds(N_KV_HEADS * r0 + kh, RB, stride=N_KV_HEADS), :] = (
                    vv[:, kh * HEAD_DIM:(kh + 1) * HEAD_DIM])
        v_s[s, pl.ds(P + o, RB), :] = vv.astype(BF16)
        xp_s[s, pl.ds(POOL_HALO + o, RB), :] = p1[:, ATTN_W + 2 * KV_W:GATE_COL]
        return carry

    @pl.when(step == 0)
    def _():
        for cp in late_copies:
            cp.start()

    @pl.when((step < n_steps) & (step % (nrb // U) == 0))
    def _():
        if P:
            for kh in range(N_KV_HEADS):
                cols = slice(kh * HEAD_DIM, (kh + 1) * HEAD_DIM)
                k_s[0, 0:P, cols] = ck_ref[pl.ds(kh, P, stride=N_KV_HEADS), :].astype(BF16)
                v_s[0, 0:P, cols] = cv_ref[pl.ds(kh, P, stride=N_KV_HEADS), :].astype(BF16)
        xp_s[:, 0:POOL_HALO, :] = jnp.zeros((S, POOL_HALO, POOL_W), F32)
        xp_s[:, L + POOL_HALO:L + 2 * POOL_HALO, :] = jnp.zeros((S, POOL_HALO, POOL_W), F32)
        lax.fori_loop(0, TM // RB, project, 0)
        for srcs, dst in zip(cast_in, cast_out):
            col = 0
            for src in srcs:
                dst[..., col:col + src.shape[-1]] = src[...].astype(BF16)
                col += src.shape[-1]

    @pl.when(step == 0)
    def _():
        for cp in late_copies:
            cp.wait()

    def mix(u):
        r0 = pl.multiple_of(((block0 + u) % nrb) * RB, RB)
        s = r0 // L
        o = pl.multiple_of(r0 % L, RB)
        attn_u = attn_s.at[u]
        rows = slice(u * RB, (u + 1) * RB)

        for hd in range(N_HEADS):
            kh = hd // GROUP
            k = k_s[s, :, kh * HEAD_DIM:(kh + 1) * HEAD_DIM]
            v = v_s[s, :, kh * HEAD_DIM:(kh + 1) * HEAD_DIM]
            qh = q_s[hd, pl.ds(r0, RB), :]
            sc = lax.dot_general(qh, k, (((1,), (1,)), ((), ())), preferred_element_type=F32)
            e = jnp.exp2(sc - jnp.max(sc, axis=-1, keepdims=True))
            den = jnp.sum(e, axis=-1, keepdims=True)
            oh = jnp.dot(e.astype(BF16), v, preferred_element_type=F32) / den
            attn_u[:, hd * HEAD_DIM:(hd + 1) * HEAD_DIM] = oh.astype(BF16)
        a = jnp.dot(attn_u[...], wa_ref[...], preferred_element_type=F32)

        t = o + lax.broadcasted_iota(I32, (RB, 1), 0)
        RW = RB + 2 * POOL_HALO
        parts = []
        for gi, w in enumerate(POOL_WINDOWS):
            cols = slice(gi * POOL_GC, (gi + 1) * POOL_GC)
            xw = xp_s[s, pl.ds(o, RW), cols]
            run = xw
            span = 1
            while span < w:
                run = run + pltpu.roll(run, span, 0)
                span *= 2
            if w // 2 > 1:
                run = pltpu.roll(run, RW - (w // 2 - 1), 0)
            tot = run[POOL_HALO:POOL_HALO + RB]
            cnt = (jnp.minimum(t + w // 2, L) - jnp.maximum(t - w // 2, 0)).astype(F32)
            parts.append(tot / cnt - xw[POOL_HALO:POOL_HALO + RB])
        dpool = jnp.concatenate(parts, axis=1).astype(BF16)
        pooled = jnp.dot(dpool, wpool_ref[...], preferred_element_type=F32) * pscale_ref[...]
        b = jnp.dot(pooled.astype(BF16), wb_ref[...], preferred_element_type=F32)

        gates = jnp.dot(h_s[pl.ds(r0, RB), :], wgate_ref[...], preferred_element_type=F32)
        merged = _sigmoid(gates[:, 0:D_MODEL]) * a + _sigmoid(gates[:, D_MODEL:2 * D_MODEL]) * b
        upd = jnp.dot(merged.astype(BF16), wo_ref[...], preferred_element_type=F32)
        xm = x_ref[pl.ds(r0, RB), :] + gt1 * upd
        xmid_ref[rows, :] = xm
        xm_s[slot, rows, :] = xm

    def moe_prep(u):
        rows = slice(u * RB, (u + 1) * RB)
        h2 = _rms(xm_s[1 - slot, rows, :]) * mod2_s[1 - slot, 0:1, :] + mod2_s[1 - slot, 1:2, :]
        hi = h2.astype(BF16)
        lo = (h2 - hi.astype(F32)).astype(BF16)
        l1 = jnp.dot(hi, wr_ref[...], preferred_element_type=F32)
        l2 = jnp.dot(lo, wr_ref[:, 0:LANES], preferred_element_type=F32)
        gate, bucket = _route(l1[:, 0:LANES] + l1[:, LANES:2 * LANES] + l2)
        groups = pl.ds(u * (RB // SUBLANES), RB // SUBLANES)
        _store_tiles(h2_ref.at[groups], _pack_bf16_pairs(h2))
        gate_ref[rows, :] = gate
        lane = lax.broadcasted_iota(I32, (RB, LANES), 1).astype(F32)
        first = pl.multiple_of((U * jnp.maximum(step - 1, 0) + u) * RB, RB)
        oh_s[pl.ds(first, RB), :] = jnp.where(lane == bucket, 1.0, 0.0).astype(BF16)

    mod2_s[slot, 0:1, :] = gain2
    mod2_s[slot, 1:2, :] = sh2

    @pl.when(step == 0)
    def _():
        for u in range(U):
            mix(u)

    @pl.when((step > 0) & (step < n_steps))
    def _():
        for u in range(U):
            moe_prep(u)
        for u in range(U):
            mix(u)

    @pl.when(step == n_steps)
    def _():
        for u in range(U):
            moe_prep(u)
        _plan_rows(oh_s, dest_ref, meta_ref, n_blocks=n_blocks * RB // TOKEN_BLOCK, tile=tile)


def _mix(x2d, mod, mod_row, cache, rope_tabs, weights, *, S, L, emit_kv, blocks_per_step, tile, cast=()):
    T = x2d.shape[0]
    TM = S * L
    P = cache[0].shape[0] // (T // L * N_KV_HEADS) if cache is not None else 0
    use_rope = rope_tabs is not None
    assert T % TM == 0 and L % ROW_BLOCK == 0
    assert not (use_rope or P) or S == 1
    Lk = P + L

    args = [x2d, mod]
    nrb = TM // ROW_BLOCK
    n_blocks = T // ROW_BLOCK
    step_rows = blocks_per_step * ROW_BLOCK
    steps_per_group = nrb // blocks_per_step
    n_mix_steps = n_blocks // blocks_per_step
    assert nrb % blocks_per_step == 0

    def mixed(s):
        return jnp.minimum(s, n_mix_steps - 1)

    def group(s):
        return mixed(s) // steps_per_group

    def prepared(s):
        return jnp.maximum(s - 1, 0)

    in_specs = [
        pl.BlockSpec((TM, D_MODEL), lambda s: (group(s), 0)),
        _resident(mod.shape),
    ]
    if P:
        args += list(cache)
        in_specs += [pl.BlockSpec((P * N_KV_HEADS, HEAD_DIM), lambda s: (group(s), 0))] * 2
    if use_rope:
        args += list(rope_tabs)
        in_specs += [_resident((L, HEAD_DIM))] * 3
    args += list(weights)
    late = (4, 6, 7, 8)
    in_specs += [pl.BlockSpec(memory_space=pl.ANY) if i in late else _resident(w.shape)
                 for i, w in enumerate(weights)]
    assert weights[1].shape == (D_MODEL, IN_W)
    in_specs[len(in_specs) - len(weights) + 1] = _resident((D_MODEL, GATE_COL))
    args.append(weights[1])
    in_specs.append(pl.BlockSpec(memory_space=pl.ANY))
    n_steps = T // TM
    def per_group(shape):
        assert shape[0] % n_steps == 0
        blk = (shape[0] // n_steps,) + shape[1:]
        return pl.BlockSpec(blk, lambda s, n=len(blk): (group(s),) + (0,) * (n - 1))

    cast_out_shapes = [ws[0].shape[:-1] + (sum(w.shape[-1] for w in ws),) for ws in cast]
    for ws in cast:
        args += list(ws)
        in_specs += [per_group(w.shape) for w in ws]

    assert T % TOKEN_BLOCK == 0
    out_shape = [jax.ShapeDtypeStruct((T, D_MODEL), F32), jax.ShapeDtypeStruct(_tiles_shape(T, PACKED_CHUNKS), U32),
                 jax.ShapeDtypeStruct((T, LANES), F32),
                 jax.ShapeDtypeStruct((T // TOKEN_BLOCK, 1, TOKEN_BLOCK), I32),
                 jax.ShapeDtypeStruct((SUBLANES, LANES), I32)]
    out_specs = [pl.BlockSpec((step_rows, D_MODEL), lambda s: (mixed(s), 0)),
                 _tiles_spec(step_rows, prepared, PACKED_CHUNKS),
                 pl.BlockSpec((step_rows, LANES), lambda s: (prepared(s), 0)),
                 pl.BlockSpec((T // TOKEN_BLOCK, 1, TOKEN_BLOCK), lambda s: (0, 0, 0)),
                 pl.BlockSpec((SUBLANES, LANES), lambda s: (0, 0))]
    if emit_kv:
        out_shape += [jax.ShapeDtypeStruct((T * N_KV_HEADS, HEAD_DIM), F32)] * 2
        out_specs += [pl.BlockSpec((TM * N_KV_HEADS, HEAD_DIM), lambda s: (group(s), 0))] * 2
    out_shape += [jax.ShapeDtypeStruct(shp, BF16) for shp in cast_out_shapes]
    out_specs += [per_group(shp) for shp in cast_out_shapes]

    scratch = [
        pltpu.VMEM((N_HEADS, TM, HEAD_DIM), BF16),
        pltpu.VMEM((S, Lk, KV_W), BF16),
        pltpu.VMEM((S, Lk, KV_W), BF16),
        pltpu.VMEM((S, L + 2 * POOL_HALO, POOL_W), F32),
        pltpu.VMEM((TM, D_MODEL), BF16),
        pltpu.VMEM((blocks_per_step, ROW_BLOCK, ATTN_W), BF16),
        pltpu.VMEM((2, step_rows, D_MODEL), F32),
        pltpu.VMEM((2, 2, D_MODEL), F32),
        pltpu.VMEM((T, LANES), BF16),
    ] + [pltpu.VMEM(weights[i].shape, weights[i].dtype) for i in late] + [
        pltpu.VMEM((D_MODEL, IN_W - GATE_COL), BF16), pltpu.SemaphoreType.DMA((len(late) + 1,))]
    kern = functools.partial(_mix_kernel, S=S, L=L, P=P, use_rope=use_rope, emit_kv=emit_kv,
                             n_cast=tuple(len(ws) for ws in cast), n_blocks=n_blocks, U=blocks_per_step,
                             mod_row=mod_row, tile=tile)
    return pl.pallas_call(
        kern,
        grid=(n_mix_steps + 1,),
        in_specs=in_specs,
        out_specs=out_specs,
        out_shape=out_shape,
        scratch_shapes=scratch,
        compiler_params=pltpu.CompilerParams(
            dimension_semantics=("arbitrary",), vmem_limit_bytes=V7X_VMEM_LIMIT_BYTES),
        name="mixer_rope" if use_rope else "mixer_ctx",
    )(*args)


def _plan_rows(oh_ref, dest_ref, meta_ref, *, n_blocks, tile):
    TB = TOKEN_BLOCK
    lane = lax.broadcasted_iota(I32, (SUBLANES, LANES), 1)

    def count(b, acc):
        oh = oh_ref[pl.ds(pl.multiple_of(b * TB, TB), TB), :].astype(F32)
        return acc + jnp.sum(oh, axis=0, keepdims=True)

    counts = lax.fori_loop(0, n_blocks, count, jnp.zeros((SUBLANES, LANES), F32))
    padded = jnp.floor((counts + (tile - 0.5)) * (1.0 / tile)) * tile
    ends = padded
    step = 1
    while step < LANES:
        ends = ends + jnp.where(lane >= step, pltpu.roll(ends, step, 1), 0.0)
        step *= 2
    starts = ends - padded

    tri = jnp.where(lax.broadcasted_iota(I32, (TB, TB), 1) < lax.broadcasted_iota(I32, (TB, TB), 0),
                    1.0, 0.0).astype(BF16)

    def place(b, seen):
        oh = oh_ref[pl.ds(pl.multiple_of(b * TB, TB), TB), :]
        ohf = oh.astype(F32)
        rank = jnp.dot(tri, oh, preferred_element_type=F32)
        base = (starts + seen)[0:1, :]
        d = jnp.sum(ohf * (rank + base), axis=1, keepdims=True)
        dest_ref[b] = _row(d).astype(I32)
        return seen + jnp.sum(ohf, axis=0, keepdims=True)

    lax.fori_loop(0, n_blocks, place, jnp.zeros((SUBLANES, LANES), F32))

    tile_row0 = lax.broadcasted_iota(I32, (LANES, LANES), 0).astype(F32) * tile
    is_bucket = lax.broadcasted_iota(I32, (LANES, LANES), 1) < N_BUCKETS
    done = jnp.sum(jnp.where(is_bucket, jnp.where(ends[0:1, :] <= tile_row0, 1.0, 0.0), 0.0),
                   axis=1, keepdims=True)
    bkt = jnp.minimum(done, N_BUCKETS - 1.0)
    grp = (jnp.where(bkt >= PAIRS_PER_GROUP, 1.0, 0.0) + jnp.where(bkt >= 2 * PAIRS_PER_GROUP, 1.0, 0.0)
           + jnp.where(bkt >= 3 * PAIRS_PER_GROUP, 1.0, 0.0))
    pair = bkt - PAIRS_PER_GROUP * grp
    a = jnp.where(pair >= 3.0, 1.0, 0.0) + jnp.where(pair >= 5.0, 1.0, 0.0)
    b = pair - a * (7.0 - a) * 0.5 + a + 1.0
    e1 = EXP_PER_GROUP * grp + a
    e2 = EXP_PER_GROUP * grp + b
    meta = jnp.concatenate(
        [_row(e1), _row(e2), jnp.floor(ends[0:1, :] * (1.0 / tile) + 0.5),
         jnp.zeros((SUBLANES - 3, LANES), F32)], axis=0)
    meta_ref[...] = meta.astype(I32)


def _sc_move_rows(src_v, table_hbm, out_hbm, lo, n_rows, idx_v, pieces_v, sem):
    chunks = pieces_v.shape[0] // SC_ROWS_PER_STEP
    lane = lax.iota(I32, SC_LANES)
    row_in_group = lane & (SUBLANES - 1)
    chunk_in_pair = lane >> 3
    rows_per_gather = SC_PIECES_PER_GATHER // chunks

    @pl.loop(0, n_rows // SC_ROWS_PER_STEP)
    def _(step):
        copies = []
        for g in range(SC_ROWS_PER_STEP // rows_per_gather):
            r0 = step * SC_ROWS_PER_STEP + g * rows_per_gather
            for v in range(SC_PIECES_PER_GATHER // SC_LANES):
                group, chunk0 = v // (chunks // 2), 2 * (v % (chunks // 2))
                tok = plsc.load_gather(src_v, [r0 + group * SUBLANES + row_in_group])
                piece = (tok >> 3) * (SUBLANES * chunks) + (chunk0 + chunk_in_pair) * SUBLANES + (tok & 7)
                idx_v[pl.ds(g * SC_PIECES_PER_GATHER + v * SC_LANES, SC_LANES)] = piece
            window = pl.ds(g * SC_PIECES_PER_GATHER, SC_PIECES_PER_GATHER)
            copies.append(pltpu.async_copy(table_hbm.at[idx_v.at[window]], pieces_v.at[window], sem))
        for cp in copies:
            cp.wait()
        first = pl.multiple_of((lo + step * SC_ROWS_PER_STEP) * chunks, SC_ROWS_PER_STEP * chunks)
        pltpu.sync_copy(pieces_v, out_hbm.at[pl.ds(first, SC_ROWS_PER_STEP * chunks)])


def _sc_scratch(chunks, dtype):
    return [pltpu.VMEM((SC_ROWS_PER_STEP * chunks,), I32), pltpu.VMEM((SC_ROWS_PER_STEP * chunks, LANES), dtype)]


def _sc_dispatch(h2_flat, gate_rows, dest, n_rows):
    T = dest.shape[0]
    per_worker = n_rows // SC_WORKERS
    rows_per_step = SC_ROWS_PER_STEP
    chunks = h2_flat.shape[0] // T
    assert n_rows % SC_WORKERS == 0 and per_worker % rows_per_step == 0 and T % SC_LANES == 0
    mesh = plsc.VectorSubcoreMesh(core_axis_name="c", subcore_axis_name="s")

    @functools.partial(
        pl.kernel, mesh=mesh,
        out_type=[jax.ShapeDtypeStruct((n_rows * chunks, LANES), h2_flat.dtype),
                  jax.ShapeDtypeStruct((n_rows, LANES), F32)],
        scratch_types=[pltpu.VMEM((T,), I32), pltpu.VMEM((per_worker,), I32)]
        + _sc_scratch(chunks, h2_flat.dtype)
        + [pltpu.VMEM((rows_per_step, LANES), F32), pltpu.SemaphoreType.DMA, pltpu.SemaphoreType.DMA],
        compiler_params=pltpu.CompilerParams(use_tc_tiling_on_sc=True, needs_layout_passes=False),
        name="sc_dispatch",
    )
    def dispatch(h2_hbm, gate_hbm, dest_hbm, out_h_hbm, out_g_hbm,
                 dest_v, src_v, idx_v, pieces_v, gates_v, sem_h, sem_g):
        worker = lax.axis_index("s") * SC_CORES + lax.axis_index("c")
        lo = worker * per_worker
        pltpu.sync_copy(dest_hbm, dest_v)

        @pl.loop(0, per_worker // SC_LANES)
        def _(j):
            j0 = pl.multiple_of(j * SC_LANES, SC_LANES)
            src_v[pl.ds(j0, SC_LANES)] = lax.rem(lo + j0 + lax.iota(I32, SC_LANES), T)

        @pl.loop(0, T // SC_LANES)
        def _(j):
            t0 = pl.multiple_of(j * SC_LANES, SC_LANES)
            d = dest_v[pl.ds(t0, SC_LANES)] - lo
            mine = (d >= 0) & (d < per_worker)
            plsc.store_scatter(src_v, [jnp.where(mine, d, 0)], t0 + lax.iota(I32, SC_LANES), mask=mine)

        @pl.loop(0, per_worker // rows_per_step)
        def _(j):
            off = pl.multiple_of(j * rows_per_step, rows_per_step)
            pltpu.async_copy(gate_hbm.at[src_v.at[pl.ds(off, rows_per_step)]], gates_v, sem_g).wait()
            pltpu.sync_copy(gates_v, out_g_hbm.at[pl.ds(lo + off, rows_per_step)])

        _sc_move_rows(src_v, h2_hbm, out_h_hbm, lo, per_worker, idx_v, pieces_v, sem_h)

    return dispatch(h2_flat, gate_rows, dest)


def _expert_kernel(meta, x_ref, gv_ref, wgu_hbm, wd_hbm, o_ref, wgu_ref, wd_ref, ready_s, sems, *, tile, per_step):
    groups = tile // SUBLANES
    n_used = meta[2, LANES - 1]
    step = pl.program_id(0)

    def weight_copies(g):
        experts = pl.ds(g * EXP_PER_GROUP, EXP_PER_GROUP)
        return (pltpu.make_async_copy(wgu_hbm.at[experts], wgu_ref.at[experts], sems.at[g]),
                pltpu.make_async_copy(wd_hbm.at[experts], wd_ref.at[experts], sems.at[g]))

    def land_through(last_group):
        landed = ready_s[0]
        for g in range(N_EXP_GROUPS):
            @pl.when((g >= landed) & (g <= last_group))
            def _():
                for cp in weight_copies(g):
                    cp.wait()
                if g + 1 < N_EXP_GROUPS:
                    for cp in weight_copies(g + 1):
                        cp.start()
        ready_s[0] = jnp.maximum(landed, last_group + 1)

    @pl.when(step == 0)
    def _():
        ready_s[0] = 0
        for cp in weight_copies(0):
            cp.start()

    def one_tile(k, t):
        rows = pl.ds(pl.multiple_of(k * groups, groups), groups)
        x = _unpack_bf16_pairs(_load_tiles(x_ref.at[rows]))
        gv = gv_ref[pl.ds(pl.multiple_of(k * tile, SUBLANES), tile), :]
        lane = lax.broadcasted_iota(I32, gv.shape, 1)
        out = None
        for e in (meta[0, t], meta[1, t]):
            ge = jnp.sum(jnp.where(lane == EXPERT_LANE0 + e, gv, 0.0), axis=-1, keepdims=True)
            h = jnp.dot(x, wgu_ref[e], preferred_element_type=F32)
            hg = h[:, 0:D_EXPERT]
            hid = (hg * _sigmoid(hg) * h[:, D_EXPERT:2 * D_EXPERT] * ge).astype(BF16)
            y = jnp.dot(hid, wd_ref[e], preferred_element_type=F32)
            out = y if out is None else out + y
        _store_tiles(o_ref.at[rows], _pack_bf16_pairs(out))

    def two_tiles(j, carry):
        k = 2 * j
        t = step * per_step + k

        @pl.when(t + 1 < n_used)
        def _():
            land_through(meta[0, t + 1] // EXP_PER_GROUP)
            one_tile(k, t)
            one_tile(k + 1, t + 1)

        @pl.when(t + 1 == n_used)
        def _():
            land_through(meta[0, t] // EXP_PER_GROUP)
            one_tile(k, t)

        return carry

    assert per_step % 2 == 0
    lax.fori_loop(0, per_step // 2, two_tiles, 0)

    @pl.when(step == pl.num_programs(0) - 1)
    def _():
        land_through(N_EXP_GROUPS - 1)


def _experts(sorted_h2, sorted_gates, meta, wgu, wd, tiling):
    tile, per_step, n_tiles = tiling
    step_rows = tile * per_step
    assert n_tiles * tile == sorted_h2.shape[0] * SUBLANES and n_tiles % per_step == 0

    def last_used(i, meta):
        return jnp.minimum(i, (meta[2, LANES - 1] - 1) // per_step)

    return pl.pallas_call(
        functools.partial(_expert_kernel, tile=tile, per_step=per_step),
        grid_spec=pltpu.PrefetchScalarGridSpec(
            num_scalar_prefetch=1,
            grid=(n_tiles // per_step,),
            in_specs=[
                _tiles_spec(step_rows, last_used, PACKED_CHUNKS),
                pl.BlockSpec((step_rows, LANES), lambda *a: (last_used(*a), 0)),
                pl.BlockSpec(memory_space=pl.ANY), pl.BlockSpec(memory_space=pl.ANY),
            ],
            out_specs=_tiles_spec(step_rows, last_used, PACKED_CHUNKS),
            scratch_shapes=[pltpu.VMEM(wgu.shape, wgu.dtype), pltpu.VMEM(wd.shape, wd.dtype),
                            pltpu.SMEM((1,), I32), pltpu.SemaphoreType.DMA((N_EXP_GROUPS,))],
        ),
        out_shape=jax.ShapeDtypeStruct(_tiles_shape(n_tiles * tile, PACKED_CHUNKS), U32),
        compiler_params=pltpu.CompilerParams(
            dimension_semantics=("arbitrary",), vmem_limit_bytes=V7X_VMEM_LIMIT_BYTES),
        name="moe_experts",
    )(meta, sorted_h2, sorted_gates, wgu, wd)


def _sc_row_gather(table_flat, idx, chunks):
    n = idx.shape[0]
    per_worker = n // SC_WORKERS
    assert n % SC_WORKERS == 0 and per_worker % SC_ROWS_PER_STEP == 0
    mesh = plsc.VectorSubcoreMesh(core_axis_name="c", subcore_axis_name="s")

    @functools.partial(
        pl.kernel, mesh=mesh,
        out_type=jax.ShapeDtypeStruct((n * chunks, LANES), table_flat.dtype),
        scratch_types=[pltpu.VMEM((per_worker,), I32)] + _sc_scratch(chunks, table_flat.dtype)
        + [pltpu.SemaphoreType.DMA],
        compiler_params=pltpu.CompilerParams(use_tc_tiling_on_sc=True, needs_layout_passes=False),
        name="sc_row_gather",
    )
    def gather(table_hbm, idx_hbm, out_hbm, src_v, idx_v, pieces_v, sem):
        worker = lax.axis_index("s") * SC_CORES + lax.axis_index("c")
        lo = worker * per_worker
        pltpu.sync_copy(idx_hbm.at[pl.ds(lo, per_worker)], src_v)
        _sc_move_rows(src_v, table_hbm, out_hbm, lo, per_worker, idx_v, pieces_v, sem)

    return gather(table_flat, idx)


def _final_kernel(x_ref, moe_ref, gt2_ref, gf_ref, o_ref, *, mod_row):
    gt2 = gt2_ref[pl.ds(mod_row(pl.program_id(0)), 1), :]
    y = x_ref[...] + gt2 * _unpack_bf16_pairs(_load_tiles(moe_ref)).astype(F32)
    o_ref[...] = _rms(y) * gf_ref[...]


def _final(xmid, moe_rows, mod, mod_row, gf):
    T = xmid.shape[0]
    return pl.pallas_call(
        functools.partial(_final_kernel, mod_row=mod_row),
        grid=(T // FINAL_BLOCK,),
        in_specs=[
            pl.BlockSpec((FINAL_BLOCK, D_MODEL), lambda i: (i, 0)),
            _tiles_spec(FINAL_BLOCK, lambda i: i, PACKED_CHUNKS),
            pl.BlockSpec((COND_ROWS, D_MODEL), lambda i: (0, 5)),
            pl.BlockSpec((1, D_MODEL), lambda i: (0, 0)),
        ],
        out_specs=pl.BlockSpec((FINAL_BLOCK, D_MODEL), lambda i: (i, 0)),
        out_shape=jax.ShapeDtypeStruct((T, D_MODEL), F32),
        compiler_params=pltpu.CompilerParams(
            dimension_semantics=("arbitrary",), vmem_limit_bytes=V7X_VMEM_LIMIT_BYTES),
        name="moe_final",
    )(xmid, moe_rows, mod, gf)


def _flat(tiles):
    return tiles.reshape(-1, LANES)


def _expert_tiling(T):
    tile = max(256, -(-(T * 9) // (8 * N_BUCKETS * 64)) * 64)
    per_step = max(2, EXPERT_STEP_ROWS // tile // 2 * 2)
    n_tiles = (T + N_BUCKETS * (tile - 1)) // tile
    while n_tiles % per_step or (n_tiles * tile) % (SC_WORKERS * SC_ROWS_PER_STEP):
        n_tiles += 1
    return tile, per_step, n_tiles


def _moe_dispatch(h2_tiles, gate_rows, dest, tiling):
    tile, _, n_tiles = tiling
    n_rows = n_tiles * tile
    assert n_tiles <= LANES
    sorted_h2, sorted_gates = _sc_dispatch(_flat(h2_tiles), gate_rows, dest, n_rows)
    return sorted_h2.reshape(_tiles_shape(n_rows, PACKED_CHUNKS)), sorted_gates


def _moe_unpermute(moe_sorted_tiles, dest):
    chunks = moe_sorted_tiles.shape[1]
    return _sc_row_gather(_flat(moe_sorted_tiles), dest, chunks).reshape(_tiles_shape(dest.shape[0], chunks))


def _rope_tables(n_tokens):
    t = np.arange(n_tokens)
    row = (t // GRID_W).astype(np.float32)
    col = (t % GRID_W).astype(np.float32)
    freq = np.float32(ROPE_THETA) ** (-np.arange(ROPE_NF, dtype=np.float32) / np.float32(ROPE_NF))
    ang = np.concatenate([row[:, None] * freq] * 2 + [col[:, None] * freq] * 2, axis=-1)
    first = (np.arange(HEAD_DIM) % (2 * ROPE_NF)) < ROPE_NF
    sin = np.sin(ang)
    zero = np.float32(0.0)
    return (jnp.asarray(np.cos(ang)), jnp.asarray(np.where(first, -sin, zero)),
            jnp.asarray(np.where(first, zero, sin)))


def kernel(x_prompt, x_sample, cache_k, cache_v, c, c_ctx, norm1_g, norm2_g, w_ada, b_ada, w_in, q_norm_g, k_norm_g, w_pool, pool_scale, w_branch_a, w_branch_b, w_out, w_router_group, w_router_expert, w_exp_gate, w_exp_up, w_exp_down, final_norm_g):
    assert norm1_g.shape[0] == 1, "single-layer trunk"
    B, L_ctx, _ = x_prompt.shape
    Bs, L_lat, _ = x_sample.shape
    P = cache_k.shape[2]
    assert 1 + Bs <= COND_ROWS

    cond = jnp.concatenate([c_ctx[None, :], c, jnp.zeros((COND_ROWS - 1 - Bs, D_MODEL), F32)], axis=0)
    mod, w_in_b, wa_b, wb_b, wo_b, wpool_b = _ada(
        cond, w_ada[0], b_ada[0][None, :], w_pool[0],
        cast=(w_in[0], w_branch_a[0], w_branch_b[0], w_out[0]))

    wr = jnp.concatenate([w_router_group[0], w_router_expert[0],
                          jnp.zeros((D_MODEL, LANES - N_EXP_GROUPS - N_EXPERTS), F32)], axis=1)
    wr_hi = wr.astype(BF16)
    wr_lo = (wr - wr_hi.astype(F32)).astype(BF16)
    mix_w = (norm1_g[0][None, :], w_in_b, q_norm_g[0][None, :], k_norm_g[0][None, :],
             wpool_b, pool_scale[0][None, :], wa_b, wb_b, wo_b,
             norm2_g[0][None, :], jnp.concatenate([wr_hi, wr_lo], axis=1))
    gf = final_norm_g[None, :]

    xp2 = x_prompt.reshape(B * L_ctx, D_MODEL)
    tiling_p = _expert_tiling(B * L_ctx)
    xmid_p, h2_p, gate_p, dest_p, meta_p, knew, vnew, wgu, wd = _mix(
        xp2, mod, lambda i: 0, None, None, mix_w, S=2, L=L_ctx, emit_kv=True, blocks_per_step=2,
        tile=tiling_p[0], cast=((w_exp_gate[0], w_exp_up[0]), (w_exp_down[0],)))
    dest_p = dest_p.reshape(B * L_ctx)
    sh_p, sg_p = _moe_dispatch(h2_p, gate_p, dest_p, tiling_p)

    xs2 = x_sample.reshape(Bs * L_lat, D_MODEL)
    cache = (cache_k.reshape(Bs * P * N_KV_HEADS, HEAD_DIM), cache_v.reshape(Bs * P * N_KV_HEADS, HEAD_DIM))
    tiling_s = _expert_tiling(Bs * L_lat)
    xmid_s, h2_s, gate_s, dest_s, meta_s = _mix(
        xs2, mod, lambda i: 1 + i, cache, _rope_tables(L_lat), mix_w,
        S=1, L=L_lat, emit_kv=False, blocks_per_step=1, tile=tiling_s[0])
    dest_s = dest_s.reshape(Bs * L_lat)
    sh_s, sg_s = _moe_dispatch(h2_s, gate_s, dest_s, tiling_s)

    moe_p = _moe_unpermute(_experts(sh_p, sg_p, meta_p, wgu, wd, tiling_p), dest_p)
    moe_s = _moe_unpermute(_experts(sh_s, sg_s, meta_s, wgu, wd, tiling_s), dest_s)
    y_prompt = _final(xmid_p, moe_p, mod, lambda i: 0, gf)
    blocks_per_seq = L_lat // FINAL_BLOCK
    y_sample = _final(xmid_s, moe_s, mod, lambda i: 1 + i // blocks_per_seq, gf)

    return (y_prompt.reshape(B, L_ctx, D_MODEL), y_sample.reshape(Bs, L_lat, D_MODEL),
            knew.reshape(B, 1, L_ctx, N_KV_HEADS, HEAD_DIM), vnew.reshape(B, 1, L_ctx, N_KV_HEADS, HEAD_DIM))
```

```python
import functools

import numpy as np
import jax
import jax.numpy as jnp
from jax import lax
from jax.experimental import pallas as pl
from jax.experimental.pallas import tpu as pltpu
from jax.experimental.pallas import tpu_sc as plsc

F32 = jnp.float32
BF16 = jnp.bfloat16
I32 = jnp.int32
U32 = jnp.uint32

D_MODEL = 1024
HEAD_DIM = 128
N_HEADS = 8
N_KV_HEADS = 2
GROUP = N_HEADS // N_KV_HEADS
ATTN_W = N_HEADS * HEAD_DIM
KV_W = N_KV_HEADS * HEAD_DIM
POOL_WINDOWS = (2, 4, 8, 16)
POOL_GC = 128
POOL_W = POOL_GC * len(POOL_WINDOWS)
IN_W = ATTN_W + 2 * KV_W + POOL_W + 2 * D_MODEL
GATE_COL = ATTN_W + 2 * KV_W + POOL_W
GRID_W = 64
ROPE_THETA = 10000.0
ROPE_NF = HEAD_DIM // 4
N_EXP_GROUPS = 4
EXP_PER_GROUP = 4
N_EXPERTS = 16
D_EXPERT = 256
EPS = 1e-6
LOG2_E = 1.4426950408889634

LANES = 128
SUBLANES = 8
COND_ROWS = SUBLANES
POOL_HALO = 8
ROW_BLOCK = 256
ADA_COLS = 768
EXPERT_LANE0 = N_EXP_GROUPS
PAIRS_PER_GROUP = EXP_PER_GROUP * (EXP_PER_GROUP - 1) // 2
N_BUCKETS = N_EXP_GROUPS * PAIRS_PER_GROUP
EXPERT_STEP_ROWS = 1536
TOKEN_BLOCK = 1024
FINAL_BLOCK = 1024
ROW_CHUNKS = D_MODEL // LANES
SC_CORES = 2
SC_SUBCORES = 16
SC_WORKERS = SC_CORES * SC_SUBCORES
SC_LANES = 16
SC_PIECES_PER_GATHER = 128
SC_ROWS_PER_STEP = 64
PACKED_CHUNKS = ROW_CHUNKS // 2
V7X_VMEM_LIMIT_BYTES = 56 * 1024 * 1024


def _sigmoid(x):
    return 1.0 / (1.0 + jnp.exp(-x))


def _rms(x):
    return x * lax.rsqrt(jnp.mean(x * x, axis=-1, keepdims=True) + EPS)


def _resident(shape):
    zeros = (0,) * len(shape)
    return pl.BlockSpec(shape, lambda i, *_: zeros, pipeline_mode=pl.Buffered(1))


def _tiles_shape(n, chunks=ROW_CHUNKS):
    return (n // SUBLANES, chunks, SUBLANES, LANES)


def _tiles_spec(n, block_index, chunks=ROW_CHUNKS):
    return pl.BlockSpec(_tiles_shape(n, chunks), lambda *a: (block_index(*a), 0, 0, 0))


def _store_tiles(ref, x):
    for c in range(ref.shape[1]):
        ref[:, c, :, :] = x[:, c * LANES:(c + 1) * LANES].reshape(x.shape[0] // SUBLANES, SUBLANES, LANES)


def _load_tiles(ref):
    n = ref.shape[0] * SUBLANES
    return jnp.concatenate([ref[:, c, :, :].reshape(n, LANES) for c in range(ref.shape[1])], axis=1)


def _pack_bf16_pairs(x):
    bits = pltpu.bitcast(x.astype(BF16).astype(F32), U32)
    w = x.shape[1] // 2
    return bits[:, :w] | (bits[:, w:] >> 16)


def _unpack_bf16_pairs(words):
    hi = pltpu.bitcast(words & jnp.uint32(0xFFFF0000), F32).astype(BF16)
    lo = pltpu.bitcast(words << 16, F32).astype(BF16)
    return jnp.concatenate([hi, lo], axis=1)


def _row(x):
    return jnp.transpose(jnp.broadcast_to(x, (x.shape[0], LANES)))[0:1, :]


def _ada_kernel(c_ref, w_ref, b_ref, *refs, steps_per_pool_group):
    n_cast = len(refs) // 2 - 1
    c = c_ref[...]
    s = (c * _sigmoid(c)).astype(BF16)
    refs[n_cast + 1][...] = jnp.dot(s, w_ref[...].astype(BF16), preferred_element_type=F32) + b_ref[...]
    for src, dst in zip(refs[:n_cast], refs[n_cast + 2:]):
        dst[...] = src[...].astype(BF16)
    pool_src, pool_dst = refs[n_cast], refs[-1]
    wide = jnp.concatenate([pool_src[0]] * len(POOL_WINDOWS), axis=1)
    lane_group = lax.broadcasted_iota(I32, wide.shape, 1) // POOL_GC
    pool_dst[...] = jnp.where(lane_group == pl.program_id(0) // steps_per_pool_group, wide, 0.0).astype(BF16)


def _ada(cond, w_ada, b_ada, w_pool, cast=()):
    n = w_ada.shape[1]
    n_steps = n // ADA_COLS
    cast_specs = []
    for w in cast:
        assert w.ndim == 2 and w.shape[0] % (n_steps * 2 * SUBLANES) == 0
        cast_specs.append(pl.BlockSpec((w.shape[0] // n_steps, w.shape[1]), lambda j: (j, 0)))
    pool_rows = POOL_W // n_steps
    spg = POOL_GC // pool_rows
    assert w_pool.shape == (len(POOL_WINDOWS), POOL_GC, POOL_GC) and POOL_GC % pool_rows == 0
    assert pool_rows % (2 * SUBLANES) == 0
    pool_in = pl.BlockSpec((1, pool_rows, POOL_GC), lambda j: (j // spg, j % spg, 0))
    pool_out = pl.BlockSpec((pool_rows, POOL_W), lambda j: (j, 0))
    return pl.pallas_call(
        functools.partial(_ada_kernel, steps_per_pool_group=spg),
        grid=(n_steps,),
        in_specs=[
            pl.BlockSpec((COND_ROWS, D_MODEL), lambda j: (0, 0)),
            pl.BlockSpec((D_MODEL, ADA_COLS), lambda j: (0, j)),
            pl.BlockSpec((1, ADA_COLS), lambda j: (0, j)),
        ] + cast_specs + [pool_in],
        out_specs=[pl.BlockSpec((COND_ROWS, ADA_COLS), lambda j: (0, j))] + cast_specs + [pool_out],
        out_shape=[jax.ShapeDtypeStruct((COND_ROWS, n), F32)] + [jax.ShapeDtypeStruct(w.shape, BF16) for w in cast]
        + [jax.ShapeDtypeStruct((POOL_W, POOL_W), BF16)],
        name="ada_mod",
    )(cond, w_ada, b_ada, *cast, w_pool)


def _route(logits):
    lane = lax.broadcasted_iota(I32, logits.shape, 1).astype(F32)
    neg = jnp.float32(-1e30)
    far = jnp.float32(LANES)
    is_g = lane < N_EXP_GROUPS
    gl = jnp.where(is_g, logits, neg)
    gmax = jnp.max(gl, axis=-1, keepdims=True)
    gsel = jnp.min(jnp.where(gl == gmax, lane, far), axis=-1, keepdims=True)
    psel = 1.0 / jnp.sum(jnp.where(is_g, jnp.exp(gl - gmax), 0.0), axis=-1, keepdims=True)
    e_lo = EXPERT_LANE0 + EXP_PER_GROUP * gsel
    el = jnp.where(lane >= e_lo, jnp.where(lane < e_lo + EXP_PER_GROUP, logits, neg), neg)
    v1 = jnp.max(el, axis=-1, keepdims=True)
    i1 = jnp.min(jnp.where(el == v1, lane, far), axis=-1, keepdims=True)
    el2 = jnp.where(lane == i1, neg, el)
    v2 = jnp.max(el2, axis=-1, keepdims=True)
    i2 = jnp.min(jnp.where(el2 == v2, jnp.where(lane == i1, far, lane), far), axis=-1, keepdims=True)
    e2 = jnp.exp(v2 - v1)
    w1 = psel / (1.0 + e2)
    w2 = psel * e2 / (1.0 + e2)
    gate = jnp.where(lane == i1, w1, jnp.where(lane == i2, w2, 0.0))
    a = jnp.minimum(i1, i2) - e_lo
    b = jnp.maximum(i1, i2) - e_lo
    pair = a * (7.0 - a) * 0.5 + (b - a - 1.0)
    return gate, gsel * PAIRS_PER_GROUP + pair


def _mix_kernel(*refs, S, L, P, use_rope, emit_kv, n_cast, n_blocks, U, mod_row, tile, loop_heads):
    it = iter(refs)
    x_ref = next(it)
    mod_ref = next(it)
    if P:
        ck_ref = next(it)
        cv_ref = next(it)
    if use_rope:
        cos_ref = next(it)
        sneg_ref = next(it)
        spos_ref = next(it)
    (g1_ref, win_ref, qg_ref, kg_ref, wpool_hbm, pscale_ref, wa_hbm, wb_hbm, wo_hbm,
     g2_ref, wr_ref, win_hbm) = (next(it) for _ in range(12))
    cast_in = [[next(it) for _ in range(n)] for n in n_cast]
    xmid_ref = next(it)
    h2_ref = next(it)
    gate_ref = next(it)
    dest_ref = next(it)
    meta_ref = next(it)
    if emit_kv:
        knew_ref = next(it)
        vnew_ref = next(it)
    cast_out = [next(it) for _ in n_cast]
    q_s, k_s, v_s, xp_s, h_s, attn_s, xm_s, mod2_s, oh_s = (next(it) for _ in range(9))
    wpool_ref, wa_ref, wb_ref, wo_ref, wgate_ref, late_sems = (next(it) for _ in range(6))
    late_copies = [pltpu.make_async_copy(src, dst, late_sems.at[i]) for i, (src, dst) in enumerate(
        ((wa_hbm, wa_ref), (wpool_hbm, wpool_ref), (wb_hbm, wb_ref),
         (win_hbm.at[:, pl.ds(GATE_COL, IN_W - GATE_COL)], wgate_ref), (wo_hbm, wo_ref)))]

    TM = S * L
    RB = ROW_BLOCK
    nrb = TM // RB
    n_steps = n_blocks // U
    score_gain = HEAD_DIM ** -0.5 * LOG2_E
    step = pl.program_id(0)
    block0 = U * jnp.minimum(step, n_steps - 1)
    slot = step % 2

    mod_at = pl.ds(mod_row(jnp.minimum(step, n_steps - 1) // (nrb // U)), 1)
    sh1 = mod_ref[mod_at, 0:D_MODEL]
    gain1 = g1_ref[...] * (1.0 + mod_ref[mod_at, D_MODEL:2 * D_MODEL])
    gt1 = mod_ref[mod_at, 2 * D_MODEL:3 * D_MODEL]
    sh2 = mod_ref[mod_at, 3 * D_MODEL:4 * D_MODEL]
    gain2 = g2_ref[...] * (1.0 + mod_ref[mod_at, 4 * D_MODEL:5 * D_MODEL])
    qg = qg_ref[...] * score_gain
    kg = kg_ref[...]

    def project(r, carry):
        r0 = pl.multiple_of(r * RB, RB)
        s = r0 // L
        o = pl.multiple_of(r0 % L, RB)
        hb = (_rms(x_ref[pl.ds(r0, RB), :]) * gain1 + sh1).astype(BF16)
        h_s[pl.ds(r0, RB), :] = hb
        p1 = jnp.dot(hb, win_ref[...], preferred_element_type=F32)
        if use_rope:
            cs = cos_ref[pl.ds(o, RB), :]
            sn = sneg_ref[pl.ds(o, RB), :]
            sp = spos_ref[pl.ds(o, RB), :]

        def rope(t):
            return (t * cs + pltpu.roll(t, HEAD_DIM - ROPE_NF, 1) * sn + pltpu.roll(t, ROPE_NF, 1) * sp)

        for hd in range(N_HEADS):
            qh = _rms(p1[:, hd * HEAD_DIM:(hd + 1) * HEAD_DIM]) * qg
            if use_rope:
                qh = rope(qh)
            q_s[hd, pl.ds(r0, RB), :] = qh.astype(BF16)
        for kh in range(N_KV_HEADS):
            c0 = ATTN_W + kh * HEAD_DIM
            kk = _rms(p1[:, c0:c0 + HEAD_DIM]) * kg
            if emit_kv:
                knew_ref[pl.ds(N_KV_HEADS * r0 + kh, RB, stride=N_KV_HEADS), :] = kk
            if use_rope:
                kk = rope(kk)
            k_s[s, pl.ds(P + o, RB), kh * HEAD_DIM:(kh + 1) * HEAD_DIM] = kk.astype(BF16)
        vv = p1[:, ATTN_W + KV_W:ATTN_W + 2 * KV_W]
        if emit_kv:
            for kh in range(N_KV_HEADS):
                vnew_ref[pl.ds(N_KV_HEADS * r0 + kh, RB, stride=N_KV_HEADS), :] = (
                    vv[:, kh * HEAD_DIM:(kh + 1) * HEAD_DIM])
        v_s[s, pl.ds(P + o, RB), :] = vv.astype(BF16)
        xp_s[s, pl.ds(POOL_HALO + o, RB), :] = p1[:, ATTN_W + 2 * KV_W:GATE_COL]
        return carry

    @pl.when(step == 0)
    def _():
        xm_s[1] = jnp.zeros((U * RB, D_MODEL), F32)
        mod2_s[1] = jnp.zeros((2, D_MODEL), F32)
        for cp in late_copies:
            cp.start()

    @pl.when((step < n_steps) & (step % (nrb // U) == 0))
    def _():
        if P:
            for kh in range(N_KV_HEADS):
                cols = slice(kh * HEAD_DIM, (kh + 1) * HEAD_DIM)
                k_s[0, 0:P, cols] = ck_ref[pl.ds(kh, P, stride=N_KV_HEADS), :].astype(BF16)
                v_s[0, 0:P, cols] = cv_ref[pl.ds(kh, P, stride=N_KV_HEADS), :].astype(BF16)
        xp_s[:, 0:POOL_HALO, :] = jnp.zeros((S, POOL_HALO, POOL_W), F32)
        xp_s[:, L + POOL_HALO:L + 2 * POOL_HALO, :] = jnp.zeros((S, POOL_HALO, POOL_W), F32)
        lax.fori_loop(0, TM // RB, project, 0)
        for srcs, dst in zip(cast_in, cast_out):
            col = 0
            for src in srcs:
                dst[..., col:col + src.shape[-1]] = src[...].astype(BF16)
                col += src.shape[-1]

    @pl.when(step == 0)
    def _():
        for cp in late_copies:
            cp.wait()

    def mix(u):
        r0 = pl.multiple_of(((block0 + u) % nrb) * RB, RB)
        s = r0 // L
        o = pl.multiple_of(r0 % L, RB)
        attn_u = attn_s.at[u]
        rows = slice(u * RB, (u + 1) * RB)

        def head(hd, kh):
            k = k_s[s, :, kh * HEAD_DIM:(kh + 1) * HEAD_DIM]
            v = v_s[s, :, kh * HEAD_DIM:(kh + 1) * HEAD_DIM]
            qh = q_s[hd, pl.ds(r0, RB), :]
            sc = lax.dot_general(qh, k, (((1,), (1,)), ((), ())), preferred_element_type=F32)
            e = jnp.exp2(sc - jnp.max(sc, axis=-1, keepdims=True))
            den = jnp.sum(e, axis=-1, keepdims=True)
            oh = jnp.dot(e.astype(BF16), v, preferred_element_type=F32) / den
            attn_u[hd] = oh.astype(BF16)

        for kh in range(N_KV_HEADS):
            if loop_heads:
                def body(g, carry, kh=kh):
                    head(kh * GROUP + g, kh)
                    return carry
                lax.fori_loop(0, GROUP, body, 0)
            else:
                for g in range(GROUP):
                    head(kh * GROUP + g, kh)
        attn = jnp.concatenate([attn_u[hd] for hd in range(N_HEADS)], axis=1)
        a = jnp.dot(attn, wa_ref[...], preferred_element_type=F32)

        t = o + lax.broadcasted_iota(I32, (RB, 1), 0)
        RW = RB + 2 * POOL_HALO
        parts = []
        for gi, w in enumerate(POOL_WINDOWS):
            cols = slice(gi * POOL_GC, (gi + 1) * POOL_GC)
            xw = xp_s[s, pl.ds(o, RW), cols]
            run = xw
            span = 1
            while span < w:
                run = run + pltpu.roll(run, span, 0)
                span *= 2
            if w // 2 > 1:
                run = pltpu.roll(run, RW - (w // 2 - 1), 0)
            tot = run[POOL_HALO:POOL_HALO + RB]
            cnt = (jnp.minimum(t + w // 2, L) - jnp.maximum(t - w // 2, 0)).astype(F32)
            parts.append(tot / cnt - xw[POOL_HALO:POOL_HALO + RB])
        dpool = jnp.concatenate(parts, axis=1).astype(BF16)
        pooled = jnp.dot(dpool, wpool_ref[...], preferred_element_type=F32) * pscale_ref[...]
        b = jnp.dot(pooled.astype(BF16), wb_ref[...], preferred_element_type=F32)

        gates = jnp.dot(h_s[pl.ds(r0, RB), :], wgate_ref[...], preferred_element_type=F32)
        merged = _sigmoid(gates[:, 0:D_MODEL]) * a + _sigmoid(gates[:, D_MODEL:2 * D_MODEL]) * b
        upd = jnp.dot(merged.astype(BF16), wo_ref[...], preferred_element_type=F32)
        xm = x_ref[pl.ds(r0, RB), :] + gt1 * upd
        xmid_ref[rows, :] = xm
        xm_s[slot, rows, :] = xm

    def moe_prep(u):
        rows = slice(u * RB, (u + 1) * RB)
        h2 = _rms(xm_s[1 - slot, rows, :]) * mod2_s[1 - slot, 0:1, :] + mod2_s[1 - slot, 1:2, :]
        hi = h2.astype(BF16)
        lo = (h2 - hi.astype(F32)).astype(BF16)
        l1 = jnp.dot(hi, wr_ref[...], preferred_element_type=F32)
        l2 = jnp.dot(lo, wr_ref[:, 0:LANES], preferred_element_type=F32)
        gate, bucket = _route(l1[:, 0:LANES] + l1[:, LANES:2 * LANES] + l2)
        groups = pl.ds(u * (RB // SUBLANES), RB // SUBLANES)
        _store_tiles(h2_ref.at[groups], _pack_bf16_pairs(h2))
        gate_ref[rows, :] = gate
        lane = lax.broadcasted_iota(I32, (RB, LANES), 1).astype(F32)
        first = pl.multiple_of((U * jnp.maximum(step - 1, 0) + u) * RB, RB)
        oh_s[pl.ds(first, RB), :] = jnp.where(lane == bucket, 1.0, 0.0).astype(BF16)

    mod2_s[slot, 0:1, :] = gain2
    mod2_s[slot, 1:2, :] = sh2

    @pl.when(step < n_steps)
    def _():
        for u in range(U):
            moe_prep(u)
        for u in range(U):
            mix(u)

    @pl.when(step == n_steps)
    def _():
        for u in range(U):
            moe_prep(u)
        _plan_rows(oh_s, dest_ref, meta_ref, n_blocks=n_blocks * RB // TOKEN_BLOCK, tile=tile)


def _mix(x2d, mod, mod_row, cache, rope_tabs, weights, *, S, L, emit_kv, blocks_per_step, tile, cast=()):
    T = x2d.shape[0]
    TM = S * L
    P = cache[0].shape[0] // (T // L * N_KV_HEADS) if cache is not None else 0
    use_rope = rope_tabs is not None
    assert T % TM == 0 and L % ROW_BLOCK == 0
    assert not (use_rope or P) or S == 1
    Lk = P + L

    args = [x2d, mod]
    nrb = TM // ROW_BLOCK
    n_blocks = T // ROW_BLOCK
    step_rows = blocks_per_step * ROW_BLOCK
    steps_per_group = nrb // blocks_per_step
    n_mix_steps = n_blocks // blocks_per_step
    assert nrb % blocks_per_step == 0

    def mixed(s):
        return jnp.minimum(s, n_mix_steps - 1)

    def group(s):
        return mixed(s) // steps_per_group

    def prepared(s):
        return jnp.maximum(s - 1, 0)

    in_specs = [
        pl.BlockSpec((TM, D_MODEL), lambda s: (group(s), 0)),
        _resident(mod.shape),
    ]
    if P:
        args += list(cache)
        in_specs += [pl.BlockSpec((P * N_KV_HEADS, HEAD_DIM), lambda s: (group(s), 0))] * 2
    if use_rope:
        args += list(rope_tabs)
        in_specs += [_resident((L, HEAD_DIM))] * 3
    args += list(weights)
    late = (4, 6, 7, 8)
    in_specs += [pl.BlockSpec(memory_space=pl.ANY) if i in late else _resident(w.shape)
                 for i, w in enumerate(weights)]
    assert weights[1].shape == (D_MODEL, IN_W)
    in_specs[len(in_specs) - len(weights) + 1] = _resident((D_MODEL, GATE_COL))
    args.append(weights[1])
    in_specs.append(pl.BlockSpec(memory_space=pl.ANY))
    n_steps = T // TM
    def per_group(shape):
        assert shape[0] % n_steps == 0
        blk = (shape[0] // n_steps,) + shape[1:]
        return pl.BlockSpec(blk, lambda s, n=len(blk): (group(s),) + (0,) * (n - 1))

    cast_out_shapes = [ws[0].shape[:-1] + (sum(w.shape[-1] for w in ws),) for ws in cast]
    for ws in cast:
        args += list(ws)
        in_specs += [per_group(w.shape) for w in ws]

    assert T % TOKEN_BLOCK == 0
    out_shape = [jax.ShapeDtypeStruct((T, D_MODEL), F32), jax.ShapeDtypeStruct(_tiles_shape(T, PACKED_CHUNKS), U32),
                 jax.ShapeDtypeStruct((T, LANES), F32),
                 jax.ShapeDtypeStruct((T // TOKEN_BLOCK, 1, TOKEN_BLOCK), I32),
                 jax.ShapeDtypeStruct((SUBLANES, LANES), I32)]
    out_specs = [pl.BlockSpec((step_rows, D_MODEL), lambda s: (mixed(s), 0)),
                 _tiles_spec(step_rows, prepared, PACKED_CHUNKS),
                 pl.BlockSpec((step_rows, LANES), lambda s: (prepared(s), 0)),
                 pl.BlockSpec((T // TOKEN_BLOCK, 1, TOKEN_BLOCK), lambda s: (0, 0, 0)),
                 pl.BlockSpec((SUBLANES, LANES), lambda s: (0, 0))]
    if emit_kv:
        out_shape += [jax.ShapeDtypeStruct((T * N_KV_HEADS, HEAD_DIM), F32)] * 2
        out_specs += [pl.BlockSpec((TM * N_KV_HEADS, HEAD_DIM), lambda s: (group(s), 0))] * 2
    out_shape += [jax.ShapeDtypeStruct(shp, BF16) for shp in cast_out_shapes]
    out_specs += [per_group(shp) for shp in cast_out_shapes]

    scratch = [
        pltpu.VMEM((N_HEADS, TM, HEAD_DIM), BF16),
        pltpu.VMEM((S, Lk, KV_W), BF16),
        pltpu.VMEM((S, Lk, KV_W), BF16),
        pltpu.VMEM((S, L + 2 * POOL_HALO, POOL_W), F32),
        pltpu.VMEM((TM, D_MODEL), BF16),
        pltpu.VMEM((blocks_per_step, N_HEADS, ROW_BLOCK, HEAD_DIM), BF16),
        pltpu.VMEM((2, step_rows, D_MODEL), F32),
        pltpu.VMEM((2, 2, D_MODEL), F32),
        pltpu.VMEM((T, LANES), BF16),
    ] + [pltpu.VMEM(weights[i].shape, weights[i].dtype) for i in late] + [
        pltpu.VMEM((D_MODEL, IN_W - GATE_COL), BF16), pltpu.SemaphoreType.DMA((len(late) + 1,))]
    kern = functools.partial(_mix_kernel, S=S, L=L, P=P, use_rope=use_rope, emit_kv=emit_kv,
                             n_cast=tuple(len(ws) for ws in cast), n_blocks=n_blocks, U=blocks_per_step,
                             mod_row=mod_row, tile=tile, loop_heads=Lk >= 4 * ROW_BLOCK)
    return pl.pallas_call(
        kern,
        grid=(n_mix_steps + 1,),
        in_specs=in_specs,
        out_specs=out_specs,
        out_shape=out_shape,
        scratch_shapes=scratch,
        compiler_params=pltpu.CompilerParams(
            dimension_semantics=("arbitrary",), vmem_limit_bytes=V7X_VMEM_LIMIT_BYTES),
        name="mixer_rope" if use_rope else "mixer_ctx",
    )(*args)


def _plan_rows(oh_ref, dest_ref, meta_ref, *, n_blocks, tile):
    TB = TOKEN_BLOCK
    lane = lax.broadcasted_iota(I32, (SUBLANES, LANES), 1)

    def count(b, acc):
        oh = oh_ref[pl.ds(pl.multiple_of(b * TB, TB), TB), :].astype(F32)
        return acc + jnp.sum(oh, axis=0, keepdims=True)

    counts = lax.fori_loop(0, n_blocks, count, jnp.zeros((SUBLANES, LANES), F32))
    padded = jnp.floor((counts + (tile - 0.5)) * (1.0 / tile)) * tile
    ends = padded
    step = 1
    while step < LANES:
        ends = ends + jnp.where(lane >= step, pltpu.roll(ends, step, 1), 0.0)
        step *= 2
    starts = ends - padded

    tri = jnp.where(lax.broadcasted_iota(I32, (TB, TB), 1) < lax.broadcasted_iota(I32, (TB, TB), 0),
                    1.0, 0.0).astype(BF16)

    def place(b, seen):
        oh = oh_ref[pl.ds(pl.multiple_of(b * TB, TB), TB), :]
        ohf = oh.astype(F32)
        rank = jnp.dot(tri, oh, preferred_element_type=F32)
        base = (starts + seen)[0:1, :]
        d = jnp.sum(ohf * (rank + base), axis=1, keepdims=True)
        dest_ref[b] = _row(d).astype(I32)
        return seen + jnp.sum(ohf, axis=0, keepdims=True)

    lax.fori_loop(0, n_blocks, place, jnp.zeros((SUBLANES, LANES), F32))

    tile_row0 = lax.broadcasted_iota(I32, (LANES, LANES), 0).astype(F32) * tile
    is_bucket = lax.broadcasted_iota(I32, (LANES, LANES), 1) < N_BUCKETS
    done = jnp.sum(jnp.where(is_bucket, jnp.where(ends[0:1, :] <= tile_row0, 1.0, 0.0), 0.0),
                   axis=1, keepdims=True)
    bkt = jnp.minimum(done, N_BUCKETS - 1.0)
    grp = (jnp.where(bkt >= PAIRS_PER_GROUP, 1.0, 0.0) + jnp.where(bkt >= 2 * PAIRS_PER_GROUP, 1.0, 0.0)
           + jnp.where(bkt >= 3 * PAIRS_PER_GROUP, 1.0, 0.0))
    pair = bkt - PAIRS_PER_GROUP * grp
    a = jnp.where(pair >= 3.0, 1.0, 0.0) + jnp.where(pair >= 5.0, 1.0, 0.0)
    b = pair - a * (7.0 - a) * 0.5 + a + 1.0
    e1 = EXP_PER_GROUP * grp + a
    e2 = EXP_PER_GROUP * grp + b
    meta = jnp.concatenate(
        [_row(e1), _row(e2), jnp.floor(ends[0:1, :] * (1.0 / tile) + 0.5),
         jnp.zeros((SUBLANES - 3, LANES), F32)], axis=0)
    meta_ref[...] = meta.astype(I32)


def _sc_move_rows(src_v, table_hbm, out_hbm, lo, n_rows, idx_v, pieces_v, sem):
    chunks = pieces_v.shape[0] // SC_ROWS_PER_STEP
    lane = lax.iota(I32, SC_LANES)
    row_in_group = lane & (SUBLANES - 1)
    chunk_in_pair = lane >> 3
    rows_per_gather = SC_PIECES_PER_GATHER // chunks

    @pl.loop(0, n_rows // SC_ROWS_PER_STEP)
    def _(step):
        copies = []
        for g in range(SC_ROWS_PER_STEP // rows_per_gather):
            r0 = step * SC_ROWS_PER_STEP + g * rows_per_gather
            for v in range(SC_PIECES_PER_GATHER // SC_LANES):
                group, chunk0 = v // (chunks // 2), 2 * (v % (chunks // 2))
                tok = plsc.load_gather(src_v, [r0 + group * SUBLANES + row_in_group])
                piece = (tok >> 3) * (SUBLANES * chunks) + (chunk0 + chunk_in_pair) * SUBLANES + (tok & 7)
                idx_v[pl.ds(g * SC_PIECES_PER_GATHER + v * SC_LANES, SC_LANES)] = piece
            window = pl.ds(g * SC_PIECES_PER_GATHER, SC_PIECES_PER_GATHER)
            copies.append(pltpu.async_copy(table_hbm.at[idx_v.at[window]], pieces_v.at[window], sem))
        for cp in copies:
            cp.wait()
        first = pl.multiple_of((lo + step * SC_ROWS_PER_STEP) * chunks, SC_ROWS_PER_STEP * chunks)
        pltpu.sync_copy(pieces_v, out_hbm.at[pl.ds(first, SC_ROWS_PER_STEP * chunks)])


def _sc_scratch(chunks, dtype):
    return [pltpu.VMEM((SC_ROWS_PER_STEP * chunks,), I32), pltpu.VMEM((SC_ROWS_PER_STEP * chunks, LANES), dtype)]


def _sc_dispatch(h2_flat, gate_rows, dest, n_rows):
    T = dest.shape[0]
    per_worker = n_rows // SC_WORKERS
    rows_per_step = SC_ROWS_PER_STEP
    chunks = h2_flat.shape[0] // T
    assert n_rows % SC_WORKERS == 0 and per_worker % rows_per_step == 0 and T % SC_LANES == 0
    mesh = plsc.VectorSubcoreMesh(core_axis_name="c", subcore_axis_name="s")

    @functools.partial(
        pl.kernel, mesh=mesh,
        out_type=[jax.ShapeDtypeStruct((n_rows * chunks, LANES), h2_flat.dtype),
                  jax.ShapeDtypeStruct((n_rows, LANES), F32)],
        scratch_types=[pltpu.VMEM((T,), I32), pltpu.VMEM((per_worker,), I32)]
        + _sc_scratch(chunks, h2_flat.dtype)
        + [pltpu.VMEM((rows_per_step, LANES), F32), pltpu.SemaphoreType.DMA, pltpu.SemaphoreType.DMA],
        compiler_params=pltpu.CompilerParams(use_tc_tiling_on_sc=True, needs_layout_passes=False),
        name="sc_dispatch",
    )
    def dispatch(h2_hbm, gate_hbm, dest_hbm, out_h_hbm, out_g_hbm,
                 dest_v, src_v, idx_v, pieces_v, gates_v, sem_h, sem_g):
        worker = lax.axis_index("s") * SC_CORES + lax.axis_index("c")
        lo = worker * per_worker
        pltpu.sync_copy(dest_hbm, dest_v)

        @pl.loop(0, per_worker // SC_LANES)
        def _(j):
            j0 = pl.multiple_of(j * SC_LANES, SC_LANES)
            src_v[pl.ds(j0, SC_LANES)] = lax.rem(lo + j0 + lax.iota(I32, SC_LANES), T)

        @pl.loop(0, T // SC_LANES)
        def _(j):
            t0 = pl.multiple_of(j * SC_LANES, SC_LANES)
            d = dest_v[pl.ds(t0, SC_LANES)] - lo
            mine = (d >= 0) & (d < per_worker)
            plsc.store_scatter(src_v, [jnp.where(mine, d, 0)], t0 + lax.iota(I32, SC_LANES), mask=mine)

        @pl.loop(0, per_worker // rows_per_step)
        def _(j):
            off = pl.multiple_of(j * rows_per_step, rows_per_step)
            pltpu.async_copy(gate_hbm.at[src_v.at[pl.ds(off, rows_per_step)]], gates_v, sem_g).wait()
            pltpu.sync_copy(gates_v, out_g_hbm.at[pl.ds(lo + off, rows_per_step)])

        _sc_move_rows(src_v, h2_hbm, out_h_hbm, lo, per_worker, idx_v, pieces_v, sem_h)

    return dispatch(h2_flat, gate_rows, dest)


def _expert_kernel(meta, x_ref, gv_ref, wgu_hbm, wd_hbm, o_ref, wgu_ref, wd_ref, ready_s, sems, *, tile, per_step):
    groups = tile // SUBLANES
    n_used = meta[2, LANES - 1]
    step = pl.program_id(0)

    def weight_copies(g):
        experts = pl.ds(g * EXP_PER_GROUP, EXP_PER_GROUP)
        return (pltpu.make_async_copy(wgu_hbm.at[experts], wgu_ref.at[experts], sems.at[g]),
                pltpu.make_async_copy(wd_hbm.at[experts], wd_ref.at[experts], sems.at[g]))

    def land_through(last_group):
        landed = ready_s[0]
        for g in range(N_EXP_GROUPS):
            @pl.when((g >= landed) & (g <= last_group))
            def _():
                for cp in weight_copies(g):
                    cp.wait()
                if g + 1 < N_EXP_GROUPS:
                    for cp in weight_copies(g + 1):
                        cp.start()
        ready_s[0] = jnp.maximum(landed, last_group + 1)

    @pl.when(step == 0)
    def _():
        ready_s[0] = 0
        for cp in weight_copies(0):
            cp.start()

    def one_tile(k, t):
        rows = pl.ds(pl.multiple_of(k * groups, groups), groups)
        x = _unpack_bf16_pairs(_load_tiles(x_ref.at[rows]))
        gv = gv_ref[pl.ds(pl.multiple_of(k * tile, SUBLANES), tile), :]
        lane = lax.broadcasted_iota(I32, gv.shape, 1)
        out = None
        for e in (meta[0, t], meta[1, t]):
            ge = jnp.sum(jnp.where(lane == EXPERT_LANE0 + e, gv, 0.0), axis=-1, keepdims=True)
            h = jnp.dot(x, wgu_ref[e], preferred_element_type=F32)
            hg = h[:, 0:D_EXPERT]
            hid = (hg * _sigmoid(hg) * h[:, D_EXPERT:2 * D_EXPERT] * ge).astype(BF16)
            y = jnp.dot(hid, wd_ref[e], preferred_element_type=F32)
            out = y if out is None else out + y
        _store_tiles(o_ref.at[rows], _pack_bf16_pairs(out))

    def two_tiles(j, carry):
        k = 2 * j
        t = step * per_step + k

        @pl.when(t + 1 < n_used)
        def _():
            land_through(meta[0, t + 1] // EXP_PER_GROUP)
            one_tile(k, t)
            one_tile(k + 1, t + 1)

        @pl.when(t + 1 == n_used)
        def _():
            land_through(meta[0, t] // EXP_PER_GROUP)
            one_tile(k, t)

        return carry

    assert per_step % 2 == 0
    lax.fori_loop(0, per_step // 2, two_tiles, 0)

    @pl.when(step == pl.num_programs(0) - 1)
    def _():
        land_through(N_EXP_GROUPS - 1)


def _experts(sorted_h2, sorted_gates, meta, wgu, wd, tiling):
    tile, per_step, n_tiles = tiling
    step_rows = tile * per_step
    assert n_tiles * tile == sorted_h2.shape[0] * SUBLANES and n_tiles % per_step == 0

    def last_used(i, meta):
        return jnp.minimum(i, (meta[2, LANES - 1] - 1) // per_step)

    return pl.pallas_call(
        functools.partial(_expert_kernel, tile=tile, per_step=per_step),
        grid_spec=pltpu.PrefetchScalarGridSpec(
            num_scalar_prefetch=1,
            grid=(n_tiles // per_step,),
            in_specs=[
                _tiles_spec(step_rows, last_used, PACKED_CHUNKS),
                pl.BlockSpec((step_rows, LANES), lambda *a: (last_used(*a), 0)),
                pl.BlockSpec(memory_space=pl.ANY), pl.BlockSpec(memory_space=pl.ANY),
            ],
            out_specs=_tiles_spec(step_rows, last_used, PACKED_CHUNKS),
            scratch_shapes=[pltpu.VMEM(wgu.shape, wgu.dtype), pltpu.VMEM(wd.shape, wd.dtype),
                            pltpu.SMEM((1,), I32), pltpu.SemaphoreType.DMA((N_EXP_GROUPS,))],
        ),
        out_shape=jax.ShapeDtypeStruct(_tiles_shape(n_tiles * tile, PACKED_CHUNKS), U32),
        compiler_params=pltpu.CompilerParams(
            dimension_semantics=("arbitrary",), vmem_limit_bytes=V7X_VMEM_LIMIT_BYTES),
        name="moe_experts",
    )(meta, sorted_h2, sorted_gates, wgu, wd)


def _sc_row_gather(table_flat, idx, chunks):
    n = idx.shape[0]
    per_worker = n // SC_WORKERS
    assert n % SC_WORKERS == 0 and per_worker % SC_ROWS_PER_STEP == 0
    mesh = plsc.VectorSubcoreMesh(core_axis_name="c", subcore_axis_name="s")

    @functools.partial(
        pl.kernel, mesh=mesh,
        out_type=jax.ShapeDtypeStruct((n * chunks, LANES), table_flat.dtype),
        scratch_types=[pltpu.VMEM((per_worker,), I32)] + _sc_scratch(chunks, table_flat.dtype)
        + [pltpu.SemaphoreType.DMA],
        compiler_params=pltpu.CompilerParams(use_tc_tiling_on_sc=True, needs_layout_passes=False),
        name="sc_row_gather",
    )
    def gather(table_hbm, idx_hbm, out_hbm, src_v, idx_v, pieces_v, sem):
        worker = lax.axis_index("s") * SC_CORES + lax.axis_index("c")
        lo = worker * per_worker
        pltpu.sync_copy(idx_hbm.at[pl.ds(lo, per_worker)], src_v)
        _sc_move_rows(src_v, table_hbm, out_hbm, lo, per_worker, idx_v, pieces_v, sem)

    return gather(table_flat, idx)


def _final_kernel(x_ref, moe_ref, gt2_ref, gf_ref, o_ref, *, mod_row):
    gt2 = gt2_ref[pl.ds(mod_row(pl.program_id(0)), 1), :]
    y = x_ref[...] + gt2 * _unpack_bf16_pairs(_load_tiles(moe_ref)).astype(F32)
    o_ref[...] = _rms(y) * gf_ref[...]


def _final(xmid, moe_rows, mod, mod_row, gf):
    T = xmid.shape[0]
    return pl.pallas_call(
        functools.partial(_final_kernel, mod_row=mod_row),
        grid=(T // FINAL_BLOCK,),
        in_specs=[
            pl.BlockSpec((FINAL_BLOCK, D_MODEL), lambda i: (i, 0)),
            _tiles_spec(FINAL_BLOCK, lambda i: i, PACKED_CHUNKS),
            pl.BlockSpec((COND_ROWS, D_MODEL), lambda i: (0, 5)),
            pl.BlockSpec((1, D_MODEL), lambda i: (0, 0)),
        ],
        out_specs=pl.BlockSpec((FINAL_BLOCK, D_MODEL), lambda i: (i, 0)),
        out_shape=jax.ShapeDtypeStruct((T, D_MODEL), F32),
        compiler_params=pltpu.CompilerParams(
            dimension_semantics=("arbitrary",), vmem_limit_bytes=V7X_VMEM_LIMIT_BYTES),
        name="moe_final",
    )(xmid, moe_rows, mod, gf)


def _flat(tiles):
    return tiles.reshape(-1, LANES)


def _expert_tiling(T):
    tile = max(256, -(-(T * 9) // (8 * N_BUCKETS * 64)) * 64)
    per_step = max(2, EXPERT_STEP_ROWS // tile // 2 * 2)
    n_tiles = (T + N_BUCKETS * (tile - 1)) // tile
    while n_tiles % per_step or (n_tiles * tile) % (SC_WORKERS * SC_ROWS_PER_STEP):
        n_tiles += 1
    return tile, per_step, n_tiles


def _moe_dispatch(h2_tiles, gate_rows, dest, tiling):
    tile, _, n_tiles = tiling
    n_rows = n_tiles * tile
    assert n_tiles <= LANES
    sorted_h2, sorted_gates = _sc_dispatch(_flat(h2_tiles), gate_rows, dest, n_rows)
    return sorted_h2.reshape(_tiles_shape(n_rows, PACKED_CHUNKS)), sorted_gates


def _moe_unpermute(moe_sorted_tiles, dest):
    chunks = moe_sorted_tiles.shape[1]
    return _sc_row_gather(_flat(moe_sorted_tiles), dest, chunks).reshape(_tiles_shape(dest.shape[0], chunks))


def _rope_tables(n_tokens):
    t = np.arange(n_tokens)
    row = (t // GRID_W).astype(np.float32)
    col = (t % GRID_W).astype(np.float32)
    freq = np.float32(ROPE_THETA) ** (-np.arange(ROPE_NF, dtype=np.float32) / np.float32(ROPE_NF))
    ang = np.concatenate([row[:, None] * freq] * 2 + [col[:, None] * freq] * 2, axis=-1)
    first = (np.arange(HEAD_DIM) % (2 * ROPE_NF)) < ROPE_NF
    sin = np.sin(ang)
    zero = np.float32(0.0)
    return (jnp.asarray(np.cos(ang)), jnp.asarray(np.where(first, -sin, zero)),
            jnp.asarray(np.where(first, zero, sin)))


def kernel(x_prompt, x_sample, cache_k, cache_v, c, c_ctx, norm1_g, norm2_g, w_ada, b_ada, w_in, q_norm_g, k_norm_g, w_pool, pool_scale, w_branch_a, w_branch_b, w_out, w_router_group, w_router_expert, w_exp_gate, w_exp_up, w_exp_down, final_norm_g):
    assert norm1_g.shape[0] == 1, "single-layer trunk"
    B, L_ctx, _ = x_prompt.shape
    Bs, L_lat, _ = x_sample.shape
    P = cache_k.shape[2]
    assert 1 + Bs <= COND_ROWS

    cond = jnp.concatenate([c_ctx[None, :], c, jnp.zeros((COND_ROWS - 1 - Bs, D_MODEL), F32)], axis=0)
    mod, w_in_b, wa_b, wb_b, wo_b, wpool_b = _ada(
        cond, w_ada[0], b_ada[0][None, :], w_pool[0],
        cast=(w_in[0], w_branch_a[0], w_branch_b[0], w_out[0]))

    wr = jnp.concatenate([w_router_group[0], w_router_expert[0],
                          jnp.zeros((D_MODEL, LANES - N_EXP_GROUPS - N_EXPERTS), F32)], axis=1)
    wr_hi = wr.astype(BF16)
    wr_lo = (wr - wr_hi.astype(F32)).astype(BF16)
    mix_w = (norm1_g[0][None, :], w_in_b, q_norm_g[0][None, :], k_norm_g[0][None, :],
             wpool_b, pool_scale[0][None, :], wa_b, wb_b, wo_b,
             norm2_g[0][None, :], jnp.concatenate([wr_hi, wr_lo], axis=1))
    gf = final_norm_g[None, :]

    xp2 = x_prompt.reshape(B * L_ctx, D_MODEL)
    tiling_p = _expert_tiling(B * L_ctx)
    xmid_p, h2_p, gate_p, dest_p, meta_p, knew, vnew, wgu, wd = _mix(
        xp2, mod, lambda i: 0, None, None, mix_w, S=2, L=L_ctx, emit_kv=True, blocks_per_step=2,
        tile=tiling_p[0], cast=((w_exp_gate[0], w_exp_up[0]), (w_exp_down[0],)))
    dest_p = dest_p.reshape(B * L_ctx)
    sh_p, sg_p = _moe_dispatch(h2_p, gate_p, dest_p, tiling_p)

    xs2 = x_sample.reshape(Bs * L_lat, D_MODEL)
    cache = (cache_k.reshape(Bs * P * N_KV_HEADS, HEAD_DIM), cache_v.reshape(Bs * P * N_KV_HEADS, HEAD_DIM))
    tiling_s = _expert_tiling(Bs * L_lat)
    xmid_s, h2_s, gate_s, dest_s, meta_s = _mix(
        xs2, mod, lambda i: 1 + i, cache, _rope_tables(L_lat), mix_w,
        S=1, L=L_lat, emit_kv=False, blocks_per_step=1, tile=tiling_s[0])
    dest_s = dest_s.reshape(Bs * L_lat)
    sh_s, sg_s = _moe_dispatch(h2_s, gate_s, dest_s, tiling_s)

    moe_p = _moe_unpermute(_experts(sh_p, sg_p, meta_p, wgu, wd, tiling_p), dest_p)
    moe_s = _moe_unpermute(_experts(sh_s, sg_s, meta_s, wgu, wd, tiling_s), dest_s)
    y_prompt = _final(xmid_p, moe_p, mod, lambda i: 0, gf)
    blocks_per_seq = L_lat // FINAL_BLOCK
    y_sample = _final(xmid_s, moe_s, mod, lambda i: 1 + i // blocks_per_seq, gf)

    return (y_prompt.reshape(B, L_ctx, D_MODEL), y_sample.reshape(Bs, L_lat, D_MODEL),
            knew.reshape(B, 1, L_ctx, N_KV_HEADS, HEAD_DIM), vnew.reshape(B, 1, L_ctx, N_KV_HEADS, HEAD_DIM))
```

```python
import functools

import numpy as np
import jax
import jax.numpy as jnp
from jax import lax
from jax.experimental import pallas as pl
from jax.experimental.pallas import tpu as pltpu
from jax.experimental.pallas import tpu_sc as plsc

F32 = jnp.float32
BF16 = jnp.bfloat16
I32 = jnp.int32
U32 = jnp.uint32

D_MODEL = 1024
HEAD_DIM = 128
N_HEADS = 8
N_KV_HEADS = 2
GROUP = N_HEADS // N_KV_HEADS
ATTN_W = N_HEADS * HEAD_DIM
KV_W = N_KV_HEADS * HEAD_DIM
POOL_WINDOWS = (2, 4, 8, 16)
POOL_GC = 128
POOL_W = POOL_GC * len(POOL_WINDOWS)
IN_W = ATTN_W + 2 * KV_W + POOL_W + 2 * D_MODEL
GATE_COL = ATTN_W + 2 * KV_W + POOL_W
GRID_W = 64
ROPE_THETA = 10000.0
ROPE_NF = HEAD_DIM // 4
N_EXP_GROUPS = 4
EXP_PER_GROUP = 4
N_EXPERTS = 16
D_EXPERT = 256
EPS = 1e-6
LOG2_E = 1.4426950408889634

LANES = 128
SUBLANES = 8
COND_ROWS = SUBLANES
POOL_HALO = 8
ROW_BLOCK = 256
ADA_COLS = 768
EXPERT_LANE0 = N_EXP_GROUPS
PAIRS_PER_GROUP = EXP_PER_GROUP * (EXP_PER_GROUP - 1) // 2
N_BUCKETS = N_EXP_GROUPS * PAIRS_PER_GROUP
EXPERT_STEP_ROWS = 1536
TOKEN_BLOCK = 1024
FINAL_BLOCK = 1024
FINAL_BUFFERS = 3
ROW_CHUNKS = D_MODEL // LANES
SC_CORES = 2
SC_SUBCORES = 16
SC_WORKERS = SC_CORES * SC_SUBCORES
SC_LANES = 16
SC_PIECES_PER_GATHER = 128
SC_ROWS_PER_STEP = 64
PACKED_CHUNKS = ROW_CHUNKS // 2
V7X_VMEM_LIMIT_BYTES = 56 * 1024 * 1024


def _sigmoid(x):
    return 1.0 / (1.0 + jnp.exp(-x))


def _rms(x):
    return x * lax.rsqrt(jnp.mean(x * x, axis=-1, keepdims=True) + EPS)


def _resident(shape):
    zeros = (0,) * len(shape)
    return pl.BlockSpec(shape, lambda i, *_: zeros, pipeline_mode=pl.Buffered(1))


def _tiles_shape(n, chunks=ROW_CHUNKS):
    return (n // SUBLANES, chunks, SUBLANES, LANES)


def _tiles_spec(n, block_index, chunks=ROW_CHUNKS):
    return pl.BlockSpec(_tiles_shape(n, chunks), lambda *a: (block_index(*a), 0, 0, 0))


def _store_tiles(ref, x):
    for c in range(ref.shape[1]):
        ref[:, c, :, :] = x[:, c * LANES:(c + 1) * LANES].reshape(x.shape[0] // SUBLANES, SUBLANES, LANES)


def _load_tiles(ref):
    n = ref.shape[0] * SUBLANES
    return jnp.concatenate([ref[:, c, :, :].reshape(n, LANES) for c in range(ref.shape[1])], axis=1)


def _pack_bf16_pairs(x):
    bits = pltpu.bitcast(x.astype(BF16).astype(F32), U32)
    w = x.shape[1] // 2
    return bits[:, :w] | (bits[:, w:] >> 16)


def _unpack_bf16_pairs(words):
    hi = pltpu.bitcast(words & jnp.uint32(0xFFFF0000), F32).astype(BF16)
    lo = pltpu.bitcast(words << 16, F32).astype(BF16)
    return jnp.concatenate([hi, lo], axis=1)


def _row(x):
    return jnp.transpose(jnp.broadcast_to(x, (x.shape[0], LANES)))[0:1, :]


def _ada_kernel(c_ref, w_ref, b_ref, *refs, steps_per_pool_group):
    n_cast = len(refs) // 2 - 1
    c = c_ref[...]
    s = (c * _sigmoid(c)).astype(BF16)
    refs[n_cast + 1][...] = jnp.dot(s, w_ref[...].astype(BF16), preferred_element_type=F32) + b_ref[...]
    for src, dst in zip(refs[:n_cast], refs[n_cast + 2:]):
        dst[...] = src[...].astype(BF16)
    pool_src, pool_dst = refs[n_cast], refs[-1]
    wide = jnp.concatenate([pool_src[0]] * len(POOL_WINDOWS), axis=1)
    lane_group = lax.broadcasted_iota(I32, wide.shape, 1) // POOL_GC
    pool_dst[...] = jnp.where(lane_group == pl.program_id(0) // steps_per_pool_group, wide, 0.0).astype(BF16)


def _ada(cond, w_ada, b_ada, w_pool, cast=()):
    n = w_ada.shape[1]
    n_steps = n // ADA_COLS
    cast_specs = []
    for w in cast:
        assert w.ndim == 2 and w.shape[0] % (n_steps * 2 * SUBLANES) == 0
        cast_specs.append(pl.BlockSpec((w.shape[0] // n_steps, w.shape[1]), lambda j: (j, 0)))
    pool_rows = POOL_W // n_steps
    spg = POOL_GC // pool_rows
    assert w_pool.shape == (len(POOL_WINDOWS), POOL_GC, POOL_GC) and POOL_GC % pool_rows == 0
    assert pool_rows % (2 * SUBLANES) == 0
    pool_in = pl.BlockSpec((1, pool_rows, POOL_GC), lambda j: (j // spg, j % spg, 0))
    pool_out = pl.BlockSpec((pool_rows, POOL_W), lambda j: (j, 0))
    return pl.pallas_call(
        functools.partial(_ada_kernel, steps_per_pool_group=spg),
        grid=(n_steps,),
        in_specs=[
            pl.BlockSpec((COND_ROWS, D_MODEL), lambda j: (0, 0)),
            pl.BlockSpec((D_MODEL, ADA_COLS), lambda j: (0, j)),
            pl.BlockSpec((1, ADA_COLS), lambda j: (0, j)),
        ] + cast_specs + [pool_in],
        out_specs=[pl.BlockSpec((COND_ROWS, ADA_COLS), lambda j: (0, j))] + cast_specs + [pool_out],
        out_shape=[jax.ShapeDtypeStruct((COND_ROWS, n), F32)] + [jax.ShapeDtypeStruct(w.shape, BF16) for w in cast]
        + [jax.ShapeDtypeStruct((POOL_W, POOL_W), BF16)],
        name="ada_mod",
    )(cond, w_ada, b_ada, *cast, w_pool)


def _route(logits):
    lane = lax.broadcasted_iota(I32, logits.shape, 1).astype(F32)
    neg = jnp.float32(-1e30)
    far = jnp.float32(LANES)
    is_g = lane < N_EXP_GROUPS
    gl = jnp.where(is_g, logits, neg)
    gmax = jnp.max(gl, axis=-1, keepdims=True)
    gsel = jnp.min(jnp.where(gl == gmax, lane, far), axis=-1, keepdims=True)
    psel = 1.0 / jnp.sum(jnp.where(is_g, jnp.exp(gl - gmax), 0.0), axis=-1, keepdims=True)
    e_lo = EXPERT_LANE0 + EXP_PER_GROUP * gsel
    el = jnp.where(lane >= e_lo, jnp.where(lane < e_lo + EXP_PER_GROUP, logits, neg), neg)
    v1 = jnp.max(el, axis=-1, keepdims=True)
    i1 = jnp.min(jnp.where(el == v1, lane, far), axis=-1, keepdims=True)
    el2 = jnp.where(lane == i1, neg, el)
    v2 = jnp.max(el2, axis=-1, keepdims=True)
    i2 = jnp.min(jnp.where(el2 == v2, jnp.where(lane == i1, far, lane), far), axis=-1, keepdims=True)
    e2 = jnp.exp(v2 - v1)
    w1 = psel / (1.0 + e2)
    w2 = psel * e2 / (1.0 + e2)
    gate = jnp.where(lane == i1, w1, jnp.where(lane == i2, w2, 0.0))
    a = jnp.minimum(i1, i2) - e_lo
    b = jnp.maximum(i1, i2) - e_lo
    pair = a * (7.0 - a) * 0.5 + (b - a - 1.0)
    return gate, gsel * PAIRS_PER_GROUP + pair


def _mix_kernel(*refs, S, L, P, use_rope, emit_kv, n_cast, n_blocks, U, mod_row, tile):
    it = iter(refs)
    x_ref = next(it)
    mod_ref = next(it)
    if P:
        ck_ref = next(it)
        cv_ref = next(it)
    if use_rope:
        cos_ref = next(it)
        sneg_ref = next(it)
        spos_ref = next(it)
    (g1_ref, win_ref, qg_ref, kg_ref, wpool_hbm, pscale_ref, wa_hbm, wb_hbm, wo_hbm,
     g2_ref, wr_ref, win_hbm) = (next(it) for _ in range(12))
    cast_in = [[next(it) for _ in range(n)] for n in n_cast]
    xmid_ref = next(it)
    h2_ref = next(it)
    gate_ref = next(it)
    dest_ref = next(it)
    meta_ref = next(it)
    if emit_kv:
        knew_ref = next(it)
        vnew_ref = next(it)
    cast_out = [next(it) for _ in n_cast]
    q_s, k_s, v_s, xp_s, h_s, attn_s, xm_s, mod2_s, oh_s = (next(it) for _ in range(9))
    wpool_ref, wa_ref, wb_ref, wo_ref, wgate_ref, late_sems = (next(it) for _ in range(6))
    late_copies = [pltpu.make_async_copy(src, dst, late_sems.at[i]) for i, (src, dst) in enumerate(
        ((wa_hbm, wa_ref), (wpool_hbm, wpool_ref), (wb_hbm, wb_ref),
         (win_hbm.at[:, pl.ds(GATE_COL, IN_W - GATE_COL)], wgate_ref), (wo_hbm, wo_ref)))]

    TM = S * L
    RB = ROW_BLOCK
    nrb = TM // RB
    n_steps = n_blocks // U
    score_gain = HEAD_DIM ** -0.5 * LOG2_E
    step = pl.program_id(0)
    block0 = U * jnp.minimum(step, n_steps - 1)
    slot = step % 2

    mod_at = pl.ds(mod_row(jnp.minimum(step, n_steps - 1) // (nrb // U)), 1)
    sh1 = mod_ref[mod_at, 0:D_MODEL]
    gain1 = g1_ref[...] * (1.0 + mod_ref[mod_at, D_MODEL:2 * D_MODEL])
    gt1 = mod_ref[mod_at, 2 * D_MODEL:3 * D_MODEL]
    sh2 = mod_ref[mod_at, 3 * D_MODEL:4 * D_MODEL]
    gain2 = g2_ref[...] * (1.0 + mod_ref[mod_at, 4 * D_MODEL:5 * D_MODEL])
    qg = qg_ref[...] * score_gain
    kg = kg_ref[...]

    def project(r, carry):
        r0 = pl.multiple_of(r * RB, RB)
        s = r0 // L
        o = pl.multiple_of(r0 % L, RB)
        hb = (_rms(x_ref[pl.ds(r0, RB), :]) * gain1 + sh1).astype(BF16)
        h_s[pl.ds(r0, RB), :] = hb
        p1 = jnp.dot(hb, win_ref[...], preferred_element_type=F32)
        if use_rope:
            cs = cos_ref[pl.ds(o, RB), :]
            sn = sneg_ref[pl.ds(o, RB), :]
            sp = spos_ref[pl.ds(o, RB), :]

        def rope(t):
            return (t * cs + pltpu.roll(t, HEAD_DIM - ROPE_NF, 1) * sn + pltpu.roll(t, ROPE_NF, 1) * sp)

        for hd in range(N_HEADS):
            qh = _rms(p1[:, hd * HEAD_DIM:(hd + 1) * HEAD_DIM]) * qg
            if use_rope:
                qh = rope(qh)
            q_s[hd, pl.ds(r0, RB), :] = qh.astype(BF16)
        for kh in range(N_KV_HEADS):
            c0 = ATTN_W + kh * HEAD_DIM
            kk = _rms(p1[:, c0:c0 + HEAD_DIM]) * kg
            if emit_kv:
                knew_ref[pl.ds(N_KV_HEADS * r0 + kh, RB, stride=N_KV_HEADS), :] = kk
            if use_rope:
                kk = rope(kk)
            k_s[s, pl.ds(P + o, RB), kh * HEAD_DIM:(kh + 1) * HEAD_DIM] = kk.astype(BF16)
        vv = p1[:, ATTN_W + KV_W:ATTN_W + 2 * KV_W]
        if emit_kv:
            for kh in range(N_KV_HEADS):
                vnew_ref[pl.ds(N_KV_HEADS * r0 + kh, RB, stride=N_KV_HEADS), :] = (
                    vv[:, kh * HEAD_DIM:(kh + 1) * HEAD_DIM])
        v_s[s, pl.ds(P + o, RB), :] = vv.astype(BF16)
        xp_s[s, pl.ds(POOL_HALO + o, RB), :] = p1[:, ATTN_W + 2 * KV_W:GATE_COL]
        return carry

    @pl.when(step == 0)
    def _():
        xm_s[1] = jnp.zeros((U * RB, D_MODEL), F32)
        mod2_s[1] = jnp.zeros((2, D_MODEL), F32)
        for cp in late_copies:
            cp.start()

    @pl.when((step < n_steps) & (step % (nrb // U) == 0))
    def _():
        if P:
            for kh in range(N_KV_HEADS):
                cols = slice(kh * HEAD_DIM, (kh + 1) * HEAD_DIM)
                k_s[0, 0:P, cols] = ck_ref[pl.ds(kh, P, stride=N_KV_HEADS), :].astype(BF16)
                v_s[0, 0:P, cols] = cv_ref[pl.ds(kh, P, stride=N_KV_HEADS), :].astype(BF16)
        xp_s[:, 0:POOL_HALO, :] = jnp.zeros((S, POOL_HALO, POOL_W), F32)
        xp_s[:, L + POOL_HALO:L + 2 * POOL_HALO, :] = jnp.zeros((S, POOL_HALO, POOL_W), F32)
        lax.fori_loop(0, TM // RB, project, 0)
        for srcs, dst in zip(cast_in, cast_out):
            col = 0
            for src in srcs:
                dst[..., col:col + src.shape[-1]] = src[...].astype(BF16)
                col += src.shape[-1]

    @pl.when(step == 0)
    def _():
        for cp in late_copies:
            cp.wait()

    def mix(u):
        r0 = pl.multiple_of(((block0 + u) % nrb) * RB, RB)
        s = r0 // L
        o = pl.multiple_of(r0 % L, RB)
        attn_u = attn_s.at[u]
        rows = slice(u * RB, (u + 1) * RB)

        for hd in range(N_HEADS):
            kh = hd // GROUP
            k = k_s[s, :, kh * HEAD_DIM:(kh + 1) * HEAD_DIM]
            v = v_s[s, :, kh * HEAD_DIM:(kh + 1) * HEAD_DIM]
            qh = q_s[hd, pl.ds(r0, RB), :]
            sc = lax.dot_general(qh, k, (((1,), (1,)), ((), ())), preferred_element_type=F32)
            e = jnp.exp2(sc - jnp.max(sc, axis=-1, keepdims=True))
            den = jnp.sum(e, axis=-1, keepdims=True)
            oh = jnp.dot(e.astype(BF16), v, preferred_element_type=F32) / den
            attn_u[:, hd * HEAD_DIM:(hd + 1) * HEAD_DIM] = oh.astype(BF16)
        a = jnp.dot(attn_u[...], wa_ref[...], preferred_element_type=F32)

        t = o + lax.broadcasted_iota(I32, (RB, 1), 0)
        RW = RB + 2 * POOL_HALO
        parts = []
        for gi, w in enumerate(POOL_WINDOWS):
            cols = slice(gi * POOL_GC, (gi + 1) * POOL_GC)
            xw = xp_s[s, pl.ds(o, RW), cols]
            run = xw
            span = 1
            while span < w:
                run = run + pltpu.roll(run, span, 0)
                span *= 2
            if w // 2 > 1:
                run = pltpu.roll(run, RW - (w // 2 - 1), 0)
            tot = run[POOL_HALO:POOL_HALO + RB]
            cnt = (jnp.minimum(t + w // 2, L) - jnp.maximum(t - w // 2, 0)).astype(F32)
            parts.append(tot / cnt - xw[POOL_HALO:POOL_HALO + RB])
        dpool = jnp.concatenate(parts, axis=1).astype(BF16)
        pooled = jnp.dot(dpool, wpool_ref[...], preferred_element_type=F32) * pscale_ref[...]
        b = jnp.dot(pooled.astype(BF16), wb_ref[...], preferred_element_type=F32)

        gates = jnp.dot(h_s[pl.ds(r0, RB), :], wgate_ref[...], preferred_element_type=F32)
        merged = _sigmoid(gates[:, 0:D_MODEL]) * a + _sigmoid(gates[:, D_MODEL:2 * D_MODEL]) * b
        upd = jnp.dot(merged.astype(BF16), wo_ref[...], preferred_element_type=F32)
        xm = x_ref[pl.ds(r0, RB), :] + gt1 * upd
        xmid_ref[rows, :] = xm
        xm_s[slot, rows, :] = xm

    def moe_prep(u):
        rows = slice(u * RB, (u + 1) * RB)
        h2 = _rms(xm_s[1 - slot, rows, :]) * mod2_s[1 - slot, 0:1, :] + mod2_s[1 - slot, 1:2, :]
        hi = h2.astype(BF16)
        lo = (h2 - hi.astype(F32)).astype(BF16)
        l1 = jnp.dot(hi, wr_ref[...], preferred_element_type=F32)
        l2 = jnp.dot(lo, wr_ref[:, 0:LANES], preferred_element_type=F32)
        gate, bucket = _route(l1[:, 0:LANES] + l1[:, LANES:2 * LANES] + l2)
        groups = pl.ds(u * (RB // SUBLANES), RB // SUBLANES)
        _store_tiles(h2_ref.at[groups], _pack_bf16_pairs(h2))
        gate_ref[rows, :] = gate
        lane = lax.broadcasted_iota(I32, (RB, LANES), 1).astype(F32)
        first = pl.multiple_of((U * jnp.maximum(step - 1, 0) + u) * RB, RB)
        oh_s[pl.ds(first, RB), :] = jnp.where(lane == bucket, 1.0, 0.0).astype(BF16)

    mod2_s[slot, 0:1, :] = gain2
    mod2_s[slot, 1:2, :] = sh2

    @pl.when(step < n_steps)
    def _():
        for u in range(U):
            moe_prep(u)
        for u in range(U):
            mix(u)

    @pl.when(step == n_steps)
    def _():
        for u in range(U):
            moe_prep(u)
        _plan_rows(oh_s, dest_ref, meta_ref, n_blocks=n_blocks * RB // TOKEN_BLOCK, tile=tile)


def _mix(x2d, mod, mod_row, cache, rope_tabs, weights, *, S, L, emit_kv, blocks_per_step, tile, cast=()):
    T = x2d.shape[0]
    TM = S * L
    P = cache[0].shape[0] // (T // L * N_KV_HEADS) if cache is not None else 0
    use_rope = rope_tabs is not None
    assert T % TM == 0 and L % ROW_BLOCK == 0
    assert not (use_rope or P) or S == 1
    Lk = P + L

    args = [x2d, mod]
    nrb = TM // ROW_BLOCK
    n_blocks = T // ROW_BLOCK
    step_rows = blocks_per_step * ROW_BLOCK
    steps_per_group = nrb // blocks_per_step
    n_mix_steps = n_blocks // blocks_per_step
    assert nrb % blocks_per_step == 0

    def mixed(s):
        return jnp.minimum(s, n_mix_steps - 1)

    def group(s):
        return mixed(s) // steps_per_group

    def prepared(s):
        return jnp.maximum(s - 1, 0)

    in_specs = [
        pl.BlockSpec((TM, D_MODEL), lambda s: (group(s), 0)),
        _resident(mod.shape),
    ]
    if P:
        args += list(cache)
        in_specs += [pl.BlockSpec((P * N_KV_HEADS, HEAD_DIM), lambda s: (group(s), 0))] * 2
    if use_rope:
        args += list(rope_tabs)
        in_specs += [_resident((L, HEAD_DIM))] * 3
    args += list(weights)
    late = (4, 6, 7, 8)
    in_specs += [pl.BlockSpec(memory_space=pl.ANY) if i in late else _resident(w.shape)
                 for i, w in enumerate(weights)]
    assert weights[1].shape == (D_MODEL, IN_W)
    in_specs[len(in_specs) - len(weights) + 1] = _resident((D_MODEL, GATE_COL))
    args.append(weights[1])
    in_specs.append(pl.BlockSpec(memory_space=pl.ANY))
    n_steps = T // TM
    def per_group(shape):
        assert shape[0] % n_steps == 0
        blk = (shape[0] // n_steps,) + shape[1:]
        return pl.BlockSpec(blk, lambda s, n=len(blk): (group(s),) + (0,) * (n - 1))

    cast_out_shapes = [ws[0].shape[:-1] + (sum(w.shape[-1] for w in ws),) for ws in cast]
    for ws in cast:
        args += list(ws)
        in_specs += [per_group(w.shape) for w in ws]

    assert T % TOKEN_BLOCK == 0
    out_shape = [jax.ShapeDtypeStruct((T, D_MODEL), F32), jax.ShapeDtypeStruct(_tiles_shape(T, PACKED_CHUNKS), U32),
                 jax.ShapeDtypeStruct((T, LANES), F32),
                 jax.ShapeDtypeStruct((T // TOKEN_BLOCK, 1, TOKEN_BLOCK), I32),
                 jax.ShapeDtypeStruct((SUBLANES, LANES), I32)]
    out_specs = [pl.BlockSpec((step_rows, D_MODEL), lambda s: (mixed(s), 0)),
                 _tiles_spec(step_rows, prepared, PACKED_CHUNKS),
                 pl.BlockSpec((step_rows, LANES), lambda s: (prepared(s), 0)),
                 pl.BlockSpec((T // TOKEN_BLOCK, 1, TOKEN_BLOCK), lambda s: (0, 0, 0)),
                 pl.BlockSpec((SUBLANES, LANES), lambda s: (0, 0))]
    if emit_kv:
        out_shape += [jax.ShapeDtypeStruct((T * N_KV_HEADS, HEAD_DIM), F32)] * 2
        out_specs += [pl.BlockSpec((TM * N_KV_HEADS, HEAD_DIM), lambda s: (group(s), 0))] * 2
    out_shape += [jax.ShapeDtypeStruct(shp, BF16) for shp in cast_out_shapes]
    out_specs += [per_group(shp) for shp in cast_out_shapes]

    scratch = [
        pltpu.VMEM((N_HEADS, TM, HEAD_DIM), BF16),
        pltpu.VMEM((S, Lk, KV_W), BF16),
        pltpu.VMEM((S, Lk, KV_W), BF16),
        pltpu.VMEM((S, L + 2 * POOL_HALO, POOL_W), F32),
        pltpu.VMEM((TM, D_MODEL), BF16),
        pltpu.VMEM((blocks_per_step, ROW_BLOCK, ATTN_W), BF16),
        pltpu.VMEM((2, step_rows, D_MODEL), F32),
        pltpu.VMEM((2, 2, D_MODEL), F32),
        pltpu.VMEM((T, LANES), BF16),
    ] + [pltpu.VMEM(weights[i].shape, weights[i].dtype) for i in late] + [
        pltpu.VMEM((D_MODEL, IN_W - GATE_COL), BF16), pltpu.SemaphoreType.DMA((len(late) + 1,))]
    kern = functools.partial(_mix_kernel, S=S, L=L, P=P, use_rope=use_rope, emit_kv=emit_kv,
                             n_cast=tuple(len(ws) for ws in cast), n_blocks=n_blocks, U=blocks_per_step,
                             mod_row=mod_row, tile=tile)
    return pl.pallas_call(
        kern,
        grid=(n_mix_steps + 1,),
        in_specs=in_specs,
        out_specs=out_specs,
        out_shape=out_shape,
        scratch_shapes=scratch,
        compiler_params=pltpu.CompilerParams(
            dimension_semantics=("arbitrary",), vmem_limit_bytes=V7X_VMEM_LIMIT_BYTES),
        name="mixer_rope" if use_rope else "mixer_ctx",
    )(*args)


def _plan_rows(oh_ref, dest_ref, meta_ref, *, n_blocks, tile):
    TB = TOKEN_BLOCK
    lane = lax.broadcasted_iota(I32, (SUBLANES, LANES), 1)

    def count(b, acc):
        oh = oh_ref[pl.ds(pl.multiple_of(b * TB, TB), TB), :].astype(F32)
        return acc + jnp.sum(oh, axis=0, keepdims=True)

    counts = lax.fori_loop(0, n_blocks, count, jnp.zeros((SUBLANES, LANES), F32))
    padded = jnp.floor((counts + (tile - 0.5)) * (1.0 / tile)) * tile
    ends = padded
    step = 1
    while step < LANES:
        ends = ends + jnp.where(lane >= step, pltpu.roll(ends, step, 1), 0.0)
        step *= 2
    starts = ends - padded

    tri = jnp.where(lax.broadcasted_iota(I32, (TB, TB), 1) < lax.broadcasted_iota(I32, (TB, TB), 0),
                    1.0, 0.0).astype(BF16)

    def place(b, seen):
        oh = oh_ref[pl.ds(pl.multiple_of(b * TB, TB), TB), :]
        ohf = oh.astype(F32)
        rank = jnp.dot(tri, oh, preferred_element_type=F32)
        base = (starts + seen)[0:1, :]
        d = jnp.sum(ohf * (rank + base), axis=1, keepdims=True)
        dest_ref[b] = _row(d).astype(I32)
        return seen + jnp.sum(ohf, axis=0, keepdims=True)

    lax.fori_loop(0, n_blocks, place, jnp.zeros((SUBLANES, LANES), F32))

    tile_row0 = lax.broadcasted_iota(I32, (LANES, LANES), 0).astype(F32) * tile
    is_bucket = lax.broadcasted_iota(I32, (LANES, LANES), 1) < N_BUCKETS
    done = jnp.sum(jnp.where(is_bucket, jnp.where(ends[0:1, :] <= tile_row0, 1.0, 0.0), 0.0),
                   axis=1, keepdims=True)
    bkt = jnp.minimum(done, N_BUCKETS - 1.0)
    grp = (jnp.where(bkt >= PAIRS_PER_GROUP, 1.0, 0.0) + jnp.where(bkt >= 2 * PAIRS_PER_GROUP, 1.0, 0.0)
           + jnp.where(bkt >= 3 * PAIRS_PER_GROUP, 1.0, 0.0))
    pair = bkt - PAIRS_PER_GROUP * grp
    a = jnp.where(pair >= 3.0, 1.0, 0.0) + jnp.where(pair >= 5.0, 1.0, 0.0)
    b = pair - a * (7.0 - a) * 0.5 + a + 1.0
    e1 = EXP_PER_GROUP * grp + a
    e2 = EXP_PER_GROUP * grp + b
    meta = jnp.concatenate(
        [_row(e1), _row(e2), jnp.floor(ends[0:1, :] * (1.0 / tile) + 0.5),
         jnp.zeros((SUBLANES - 3, LANES), F32)], axis=0)
    meta_ref[...] = meta.astype(I32)


def _sc_move_rows(src_v, table_hbm, out_hbm, lo, n_rows, idx_v, pieces_v, sem):
    chunks = pieces_v.shape[0] // SC_ROWS_PER_STEP
    lane = lax.iota(I32, SC_LANES)
    row_in_group = lane & (SUBLANES - 1)
    chunk_in_pair = lane >> 3
    rows_per_gather = SC_PIECES_PER_GATHER // chunks

    @pl.loop(0, n_rows // SC_ROWS_PER_STEP)
    def _(step):
        copies = []
        for g in range(SC_ROWS_PER_STEP // rows_per_gather):
            r0 = step * SC_ROWS_PER_STEP + g * rows_per_gather
            for v in range(SC_PIECES_PER_GATHER // SC_LANES):
                group, chunk0 = v // (chunks // 2), 2 * (v % (chunks // 2))
                tok = plsc.load_gather(src_v, [r0 + group * SUBLANES + row_in_group])
                piece = (tok >> 3) * (SUBLANES * chunks) + (chunk0 + chunk_in_pair) * SUBLANES + (tok & 7)
                idx_v[pl.ds(g * SC_PIECES_PER_GATHER + v * SC_LANES, SC_LANES)] = piece
            window = pl.ds(g * SC_PIECES_PER_GATHER, SC_PIECES_PER_GATHER)
            copies.append(pltpu.async_copy(table_hbm.at[idx_v.at[window]], pieces_v.at[window], sem))
        for cp in copies:
            cp.wait()
        first = pl.multiple_of((lo + step * SC_ROWS_PER_STEP) * chunks, SC_ROWS_PER_STEP * chunks)
        pltpu.sync_copy(pieces_v, out_hbm.at[pl.ds(first, SC_ROWS_PER_STEP * chunks)])


def _sc_scratch(chunks, dtype):
    return [pltpu.VMEM((SC_ROWS_PER_STEP * chunks,), I32), pltpu.VMEM((SC_ROWS_PER_STEP * chunks, LANES), dtype)]


def _sc_dispatch(h2_flat, gate_rows, dest, n_rows):
    T = dest.shape[0]
    per_worker = n_rows // SC_WORKERS
    rows_per_step = SC_ROWS_PER_STEP
    chunks = h2_flat.shape[0] // T
    assert n_rows % SC_WORKERS == 0 and per_worker % rows_per_step == 0 and T % SC_LANES == 0
    mesh = plsc.VectorSubcoreMesh(core_axis_name="c", subcore_axis_name="s")

    @functools.partial(
        pl.kernel, mesh=mesh,
        out_type=[jax.ShapeDtypeStruct((n_rows * chunks, LANES), h2_flat.dtype),
                  jax.ShapeDtypeStruct((n_rows, LANES), F32)],
        scratch_types=[pltpu.VMEM((T,), I32), pltpu.VMEM((per_worker,), I32)]
        + _sc_scratch(chunks, h2_flat.dtype)
        + [pltpu.VMEM((rows_per_step, LANES), F32), pltpu.SemaphoreType.DMA, pltpu.SemaphoreType.DMA],
        compiler_params=pltpu.CompilerParams(use_tc_tiling_on_sc=True, needs_layout_passes=False),
        name="sc_dispatch",
    )
    def dispatch(h2_hbm, gate_hbm, dest_hbm, out_h_hbm, out_g_hbm,
                 dest_v, src_v, idx_v, pieces_v, gates_v, sem_h, sem_g):
        worker = lax.axis_index("s") * SC_CORES + lax.axis_index("c")
        lo = worker * per_worker
        pltpu.sync_copy(dest_hbm, dest_v)

        @pl.loop(0, per_worker // SC_LANES)
        def _(j):
            j0 = pl.multiple_of(j * SC_LANES, SC_LANES)
            src_v[pl.ds(j0, SC_LANES)] = lax.rem(lo + j0 + lax.iota(I32, SC_LANES), T)

        @pl.loop(0, T // SC_LANES)
        def _(j):
            t0 = pl.multiple_of(j * SC_LANES, SC_LANES)
            d = dest_v[pl.ds(t0, SC_LANES)] - lo
            mine = (d >= 0) & (d < per_worker)
            plsc.store_scatter(src_v, [jnp.where(mine, d, 0)], t0 + lax.iota(I32, SC_LANES), mask=mine)

        @pl.loop(0, per_worker // rows_per_step)
        def _(j):
            off = pl.multiple_of(j * rows_per_step, rows_per_step)
            pltpu.async_copy(gate_hbm.at[src_v.at[pl.ds(off, rows_per_step)]], gates_v, sem_g).wait()
            pltpu.sync_copy(gates_v, out_g_hbm.at[pl.ds(lo + off, rows_per_step)])

        _sc_move_rows(src_v, h2_hbm, out_h_hbm, lo, per_worker, idx_v, pieces_v, sem_h)

    return dispatch(h2_flat, gate_rows, dest)


def _expert_kernel(meta, x_ref, gv_ref, wgu_hbm, wd_hbm, o_ref, wgu_ref, wd_ref, ready_s, sems, *, tile, per_step):
    groups = tile // SUBLANES
    n_used = meta[2, LANES - 1]
    step = pl.program_id(0)

    def weight_copies(g):
        experts = pl.ds(g * EXP_PER_GROUP, EXP_PER_GROUP)
        return (pltpu.make_async_copy(wgu_hbm.at[experts], wgu_ref.at[experts], sems.at[g]),
                pltpu.make_async_copy(wd_hbm.at[experts], wd_ref.at[experts], sems.at[g]))

    def land_through(last_group):
        landed = ready_s[0]
        for g in range(N_EXP_GROUPS):
            @pl.when((g >= landed) & (g <= last_group))
            def _():
                for cp in weight_copies(g):
                    cp.wait()
                if g + 1 < N_EXP_GROUPS:
                    for cp in weight_copies(g + 1):
                        cp.start()
        ready_s[0] = jnp.maximum(landed, last_group + 1)

    @pl.when(step == 0)
    def _():
        ready_s[0] = 0
        for cp in weight_copies(0):
            cp.start()

    def one_tile(k, t):
        rows = pl.ds(pl.multiple_of(k * groups, groups), groups)
        x = _unpack_bf16_pairs(_load_tiles(x_ref.at[rows]))
        gv = gv_ref[pl.ds(pl.multiple_of(k * tile, SUBLANES), tile), :]
        lane = lax.broadcasted_iota(I32, gv.shape, 1)
        out = None
        for e in (meta[0, t], meta[1, t]):
            ge = jnp.sum(jnp.where(lane == EXPERT_LANE0 + e, gv, 0.0), axis=-1, keepdims=True)
            h = jnp.dot(x, wgu_ref[e], preferred_element_type=F32)
            hg = h[:, 0:D_EXPERT]
            hid = (hg * _sigmoid(hg) * h[:, D_EXPERT:2 * D_EXPERT] * ge).astype(BF16)
            y = jnp.dot(hid, wd_ref[e], preferred_element_type=F32)
            out = y if out is None else out + y
        _store_tiles(o_ref.at[rows], _pack_bf16_pairs(out))

    def two_tiles(j, carry):
        k = 2 * j
        t = step * per_step + k

        @pl.when(t + 1 < n_used)
        def _():
            land_through(meta[0, t + 1] // EXP_PER_GROUP)
            one_tile(k, t)
            one_tile(k + 1, t + 1)

        @pl.when(t + 1 == n_used)
        def _():
            land_through(meta[0, t] // EXP_PER_GROUP)
            one_tile(k, t)

        return carry

    assert per_step % 2 == 0
    lax.fori_loop(0, per_step // 2, two_tiles, 0)

    @pl.when(step == pl.num_programs(0) - 1)
    def _():
        land_through(N_EXP_GROUPS - 1)


def _experts(sorted_h2, sorted_gates, meta, wgu, wd, tiling):
    tile, per_step, n_tiles = tiling
    step_rows = tile * per_step
    assert n_tiles * tile == sorted_h2.shape[0] * SUBLANES and n_tiles % per_step == 0

    def last_used(i, meta):
        return jnp.minimum(i, (meta[2, LANES - 1] - 1) // per_step)

    return pl.pallas_call(
        functools.partial(_expert_kernel, tile=tile, per_step=per_step),
        grid_spec=pltpu.PrefetchScalarGridSpec(
            num_scalar_prefetch=1,
            grid=(n_tiles // per_step,),
            in_specs=[
                _tiles_spec(step_rows, last_used, PACKED_CHUNKS),
                pl.BlockSpec((step_rows, LANES), lambda *a: (last_used(*a), 0)),
                pl.BlockSpec(memory_space=pl.ANY), pl.BlockSpec(memory_space=pl.ANY),
            ],
            out_specs=_tiles_spec(step_rows, last_used, PACKED_CHUNKS),
            scratch_shapes=[pltpu.VMEM(wgu.shape, wgu.dtype), pltpu.VMEM(wd.shape, wd.dtype),
                            pltpu.SMEM((1,), I32), pltpu.SemaphoreType.DMA((N_EXP_GROUPS,))],
        ),
        out_shape=jax.ShapeDtypeStruct(_tiles_shape(n_tiles * tile, PACKED_CHUNKS), U32),
        compiler_params=pltpu.CompilerParams(
            dimension_semantics=("arbitrary",), vmem_limit_bytes=V7X_VMEM_LIMIT_BYTES),
        name="moe_experts",
    )(meta, sorted_h2, sorted_gates, wgu, wd)


def _sc_row_gather(table_flat, idx, chunks):
    n = idx.shape[0]
    per_worker = n // SC_WORKERS
    assert n % SC_WORKERS == 0 and per_worker % SC_ROWS_PER_STEP == 0
    mesh = plsc.VectorSubcoreMesh(core_axis_name="c", subcore_axis_name="s")

    @functools.partial(
        pl.kernel, mesh=mesh,
        out_type=jax.ShapeDtypeStruct((n * chunks, LANES), table_flat.dtype),
        scratch_types=[pltpu.VMEM((per_worker,), I32)] + _sc_scratch(chunks, table_flat.dtype)
        + [pltpu.SemaphoreType.DMA],
        compiler_params=pltpu.CompilerParams(use_tc_tiling_on_sc=True, needs_layout_passes=False),
        name="sc_row_gather",
    )
    def gather(table_hbm, idx_hbm, out_hbm, src_v, idx_v, pieces_v, sem):
        worker = lax.axis_index("s") * SC_CORES + lax.axis_index("c")
        lo = worker * per_worker
        pltpu.sync_copy(idx_hbm.at[pl.ds(lo, per_worker)], src_v)
        _sc_move_rows(src_v, table_hbm, out_hbm, lo, per_worker, idx_v, pieces_v, sem)

    return gather(table_flat, idx)


def _final_kernel(x_hbm, moe_hbm, mod_ref, gf_ref, o_hbm, block_s, *, mod_row, n_blocks):
    block_s[0] = 0

    def block(x_ref, moe_ref, o_ref):
        i = block_s[0]
        block_s[0] = i + 1
        gt2 = mod_ref[pl.ds(mod_row(i), 1), 5 * D_MODEL:6 * D_MODEL]
        y = x_ref[...] + gt2 * _unpack_bf16_pairs(_load_tiles(moe_ref)).astype(F32)
        o_ref[...] = _rms(y) * gf_ref[...]

    deep = pl.Buffered(FINAL_BUFFERS)
    pltpu.emit_pipeline(
        block,
        grid=(n_blocks,),
        in_specs=[
            pl.BlockSpec((FINAL_BLOCK, D_MODEL), lambda i: (i, 0), pipeline_mode=deep),
            pl.BlockSpec(_tiles_shape(FINAL_BLOCK, PACKED_CHUNKS), lambda i: (i, 0, 0, 0), pipeline_mode=deep),
        ],
        out_specs=[pl.BlockSpec((FINAL_BLOCK, D_MODEL), lambda i: (i, 0))],
    )(x_hbm, moe_hbm, o_hbm)


def _final(xmid, moe_rows, mod, mod_row, gf):
    T = xmid.shape[0]
    return pl.pallas_call(
        functools.partial(_final_kernel, mod_row=mod_row, n_blocks=T // FINAL_BLOCK),
        in_specs=[
            pl.BlockSpec(memory_space=pl.ANY), pl.BlockSpec(memory_space=pl.ANY),
            pl.BlockSpec(memory_space=pltpu.VMEM), pl.BlockSpec(memory_space=pltpu.VMEM),
        ],
        out_specs=pl.BlockSpec(memory_space=pl.ANY),
        out_shape=jax.ShapeDtypeStruct((T, D_MODEL), F32),
        scratch_shapes=[pltpu.SMEM((1,), I32)],
        compiler_params=pltpu.CompilerParams(vmem_limit_bytes=V7X_VMEM_LIMIT_BYTES),
        name="moe_final",
    )(xmid, moe_rows, mod, gf)


def _flat(tiles):
    return tiles.reshape(-1, LANES)


def _expert_tiling(T):
    tile = max(256, -(-(T * 9) // (8 * N_BUCKETS * 64)) * 64)
    per_step = max(2, EXPERT_STEP_ROWS // tile // 2 * 2)
    n_tiles = (T + N_BUCKETS * (tile - 1)) // tile
    while n_tiles % per_step or (n_tiles * tile) % (SC_WORKERS * SC_ROWS_PER_STEP):
        n_tiles += 1
    return tile, per_step, n_tiles


def _moe_dispatch(h2_tiles, gate_rows, dest, tiling):
    tile, _, n_tiles = tiling
    n_rows = n_tiles * tile
    assert n_tiles <= LANES
    sorted_h2, sorted_gates = _sc_dispatch(_flat(h2_tiles), gate_rows, dest, n_rows)
    return sorted_h2.reshape(_tiles_shape(n_rows, PACKED_CHUNKS)), sorted_gates


def _moe_unpermute(moe_sorted_tiles, dest):
    chunks = moe_sorted_tiles.shape[1]
    return _sc_row_gather(_flat(moe_sorted_tiles), dest, chunks).reshape(_tiles_shape(dest.shape[0], chunks))


def _rope_tables(n_tokens):
    t = np.arange(n_tokens)
    row = (t // GRID_W).astype(np.float32)
    col = (t % GRID_W).astype(np.float32)
    freq = np.float32(ROPE_THETA) ** (-np.arange(ROPE_NF, dtype=np.float32) / np.float32(ROPE_NF))
    ang = np.concatenate([row[:, None] * freq] * 2 + [col[:, None] * freq] * 2, axis=-1)
    first = (np.arange(HEAD_DIM) % (2 * ROPE_NF)) < ROPE_NF
    sin = np.sin(ang)
    zero = np.float32(0.0)
    return (jnp.asarray(np.cos(ang)), jnp.asarray(np.where(first, -sin, zero)),
            jnp.asarray(np.where(first, zero, sin)))


def kernel(x_prompt, x_sample, cache_k, cache_v, c, c_ctx, norm1_g, norm2_g, w_ada, b_ada, w_in, q_norm_g, k_norm_g, w_pool, pool_scale, w_branch_a, w_branch_b, w_out, w_router_group, w_router_expert, w_exp_gate, w_exp_up, w_exp_down, final_norm_g):
    assert norm1_g.shape[0] == 1, "single-layer trunk"
    B, L_ctx, _ = x_prompt.shape
    Bs, L_lat, _ = x_sample.shape
    P = cache_k.shape[2]
    assert 1 + Bs <= COND_ROWS

    cond = jnp.concatenate([c_ctx[None, :], c, jnp.zeros((COND_ROWS - 1 - Bs, D_MODEL), F32)], axis=0)
    mod, w_in_b, wa_b, wb_b, wo_b, wpool_b = _ada(
        cond, w_ada[0], b_ada[0][None, :], w_pool[0],
        cast=(w_in[0], w_branch_a[0], w_branch_b[0], w_out[0]))

    wr = jnp.concatenate([w_router_group[0], w_router_expert[0],
                          jnp.zeros((D_MODEL, LANES - N_EXP_GROUPS - N_EXPERTS), F32)], axis=1)
    wr_hi = wr.astype(BF16)
    wr_lo = (wr - wr_hi.astype(F32)).astype(BF16)
    mix_w = (norm1_g[0][None, :], w_in_b, q_norm_g[0][None, :], k_norm_g[0][None, :],
             wpool_b, pool_scale[0][None, :], wa_b, wb_b, wo_b,
             norm2_g[0][None, :], jnp.concatenate([wr_hi, wr_lo], axis=1))
    gf = final_norm_g[None, :]

    xp2 = x_prompt.reshape(B * L_ctx, D_MODEL)
    tiling_p = _expert_tiling(B * L_ctx)
    xmid_p, h2_p, gate_p, dest_p, meta_p, knew, vnew, wgu, wd = _mix(
        xp2, mod, lambda i: 0, None, None, mix_w, S=2, L=L_ctx, emit_kv=True, blocks_per_step=2,
        tile=tiling_p[0], cast=((w_exp_gate[0], w_exp_up[0]), (w_exp_down[0],)))
    dest_p = dest_p.reshape(B * L_ctx)
    sh_p, sg_p = _moe_dispatch(h2_p, gate_p, dest_p, tiling_p)

    xs2 = x_sample.reshape(Bs * L_lat, D_MODEL)
    cache = (cache_k.reshape(Bs * P * N_KV_HEADS, HEAD_DIM), cache_v.reshape(Bs * P * N_KV_HEADS, HEAD_DIM))
    tiling_s = _expert_tiling(Bs * L_lat)
    xmid_s, h2_s, gate_s, dest_s, meta_s = _mix(
        xs2, mod, lambda i: 1 + i, cache, _rope_tables(L_lat), mix_w,
        S=1, L=L_lat, emit_kv=False, blocks_per_step=1, tile=tiling_s[0])
    dest_s = dest_s.reshape(Bs * L_lat)
    sh_s, sg_s = _moe_dispatch(h2_s, gate_s, dest_s, tiling_s)

    moe_p = _moe_unpermute(_experts(sh_p, sg_p, meta_p, wgu, wd, tiling_p), dest_p)
    moe_s = _moe_unpermute(_experts(sh_s, sg_s, meta_s, wgu, wd, tiling_s), dest_s)
    y_prompt = _final(xmid_p, moe_p, mod, lambda i: 0, gf)
    blocks_per_seq = L_lat // FINAL_BLOCK
    y_sample = _final(xmid_s, moe_s, mod, lambda i: 1 + i // blocks_per_seq, gf)

    return (y_prompt.reshape(B, L_ctx, D_MODEL), y_sample.reshape(Bs, L_lat, D_MODEL),
            knew.reshape(B, 1, L_ctx, N_KV_HEADS, HEAD_DIM), vnew.reshape(B, 1, L_ctx, N_KV_HEADS, HEAD_DIM))
```
